```python
import jax, jax.numpy as jnp
from jax import lax
import numpy as np


D_MODEL = 1024
BATCH = 8
SEQ = 4096
DEPTH = 4

N_META = 16
CHUNK = 128
PAD = CHUNK - N_META
EPS = 1e-6
SSD_HEADS = 16
SSD_HEAD_DIM = 64
SSD_D_INNER = SSD_HEADS * SSD_HEAD_DIM
SSD_GROUPS = 2
SSD_HEADS_PER_GROUP = SSD_HEADS // SSD_GROUPS
SSD_STATE = 128
SSD_CONV = 4
SSD_CONV_CH = SSD_D_INNER + 2 * SSD_GROUPS * SSD_STATE
MLA_HEADS = 16
MLA_NOPE = 64
MLA_ROPE = 32
MLA_V = 64
MLA_Q_RANK = 384
MLA_KV_RANK = 256
ROPE_BASE = 10000.0
IN_SPLITS = (SSD_D_INNER, SSD_CONV_CH, SSD_HEADS, MLA_Q_RANK, MLA_KV_RANK, MLA_ROPE)
D_IN = sum(IN_SPLITS)
AB_WIDTH = SSD_D_INNER + MLA_HEADS * MLA_V
LRU_WIDTH = 1280
LRU_BLOCKS = 10
LRU_BLOCK = LRU_WIDTH // LRU_BLOCKS
LRU_CONV = 4
LRU_C = 8.0
D_FF = 4 * D_MODEL
N_EVEN = (DEPTH + 1) // 2
N_ODD = DEPTH // 2

kernel_name = 'hybrid_ssd_mla_rglru_sandwich_meta'


def rmsnorm(x, g):
    xf = x.astype(jnp.float32)
    y = xf * lax.rsqrt(jnp.mean(xf * xf, axis=-1, keepdims=True) + EPS)
    return (y * g.astype(jnp.float32)).astype(x.dtype)


def _split(x, sizes):
    offs = np.cumsum(sizes)[:-1].tolist()
    return jnp.split(x, offs, axis=-1)


def pad_front(x, n):
    return jnp.pad(x, [(0, 0), (n, 0)] + [(0, 0)] * (x.ndim - 2))


def causal_dwconv(x, w, b):
    k, t = w.shape[0], x.shape[1]
    xp = jnp.pad(x, ((0, 0), (k - 1, 0), (0, 0)))
    out = b
    for i in range(k):
        out = out + xp[:, i:i + t] * w[i]
    return out


def rope_tables(t, dim):
    inv = ROPE_BASE ** (-jnp.arange(0, dim, 2, dtype=jnp.float32) / dim)
    ang = jnp.arange(t, dtype=jnp.float32)[:, None] * inv[None, :]
    return jnp.cos(ang), jnp.sin(ang)


def apply_rope(x, cos, sin):
    half = x.shape[-1] // 2
    x1, x2 = x[..., :half], x[..., half:]
    return jnp.concatenate([x1 * cos - x2 * sin, x2 * cos + x1 * sin], axis=-1).astype(x.dtype)


def ssd_chunked_scan(xdt, da, bm, cm):
    b, tp = xdt.shape[:2]
    nc = tp // CHUNK
    g, k = SSD_GROUPS, SSD_HEADS_PER_GROUP
    x = xdt.reshape(b, nc, CHUNK, g, k, SSD_HEAD_DIM)
    a = da.reshape(b, nc, CHUNK, g, k).transpose(0, 1, 3, 4, 2)
    bc = bm.reshape(b, nc, CHUNK, g, SSD_STATE)
    cc = cm.reshape(b, nc, CHUNK, g, SSD_STATE)
    a_cs = jnp.cumsum(a, axis=-1)
    causal = jnp.tril(jnp.ones((CHUNK, CHUNK), dtype=bool))
    seg = a_cs[..., :, None] - a_cs[..., None, :]
    decay_in = jnp.exp(jnp.where(causal, seg, -jnp.inf))
    cb = jnp.einsum('bclgn,bcsgn->bcgls', cc, bc)
    y_diag = jnp.einsum('bcgkls,bcsgkp->bclgkp', cb[:, :, :, None] * decay_in, x)
    decay_to_end = jnp.exp(a_cs[..., -1:] - a_cs)
    states = jnp.einsum('bclgn,bcgkl,bclgkp->bcgkpn', bc, decay_to_end, x).astype(jnp.float32)
    chunk_decay = jnp.exp(a_cs[..., -1])

    def step(h, inp):
        dec, st = inp
        return dec[..., None, None] * h + st, h

    h0 = jnp.zeros((b, g, k, SSD_HEAD_DIM, SSD_STATE), jnp.float32)
    _, prev = lax.scan(step, h0, (jnp.moveaxis(chunk_decay, 1, 0), jnp.moveaxis(states, 1, 0)))
    prev = jnp.moveaxis(prev, 0, 1)
    y_off = jnp.einsum('bclgn,bcgkpn,bcgkl->bclgkp', cc.astype(jnp.float32), prev, jnp.exp(a_cs))
    y = y_diag.astype(jnp.float32) + y_off
    return y.reshape(b, tp, SSD_HEADS, SSD_HEAD_DIM).astype(xdt.dtype)


def ssd_branch(z, xbc, dt_raw, conv_w, conv_b, dt_bias, a_log, d_skip, norm_g):
    b, t, _ = z.shape
    xbc = jax.nn.silu(causal_dwconv(xbc, conv_w, conv_b))
    xs, bm, cm = _split(xbc, (SSD_D_INNER, SSD_GROUPS * SSD_STATE, SSD_GROUPS * SSD_STATE))
    dt = jax.nn.softplus(dt_raw.astype(jnp.float32) + dt_bias.astype(jnp.float32))
    a = -jnp.exp(a_log.astype(jnp.float32))
    xh = xs.reshape(b, t, SSD_HEADS, SSD_HEAD_DIM)
    xdt = pad_front(xh * dt[..., None].astype(xh.dtype), PAD)
    da = pad_front(dt * a, PAD)
    bm = pad_front(bm.reshape(b, t, SSD_GROUPS, SSD_STATE), PAD)
    cm = pad_front(cm.reshape(b, t, SSD_GROUPS, SSD_STATE), PAD)
    y = ssd_chunked_scan(xdt, da, bm, cm)[:, PAD:]
    y = y + d_skip[:, None] * xh
    y = y.reshape(b, t, SSD_D_INNER)
    return rmsnorm(y * jax.nn.silu(z), norm_g)


def mla_branch(cq, ckv, krope, q_norm_g, w_q_up, kv_norm_g, w_kv_up, cos, sin):
    b, t, _ = cq.shape
    q = (rmsnorm(cq, q_norm_g) @ w_q_up).reshape(b, t, MLA_HEADS, MLA_NOPE + MLA_ROPE)
    q = jnp.concatenate([q[..., :MLA_NOPE], apply_rope(q[..., MLA_NOPE:], cos[:, None], sin[:, None])], axis=-1)
    kv = (rmsnorm(ckv, kv_norm_g) @ w_kv_up).reshape(b, t, MLA_HEADS, MLA_NOPE + MLA_V)
    k_r = apply_rope(krope, cos, sin)
    k = jnp.concatenate([kv[..., :MLA_NOPE],
                         jnp.broadcast_to(k_r[:, :, None], (b, t, MLA_HEADS, MLA_ROPE))], axis=-1)
    v = kv[..., MLA_NOPE:]
    qp, kp, vp = pad_front(q, PAD), pad_front(k, PAD), pad_front(v, PAD)
    tp = t + PAD
    nb = tp // CHUNK
    scale = (MLA_NOPE + MLA_ROPE) ** -0.5
    kidx = jnp.arange(tp)

    def block(j):
        q_blk = lax.dynamic_slice_in_dim(qp, j * CHUNK, CHUNK, axis=1)
        s = jnp.einsum('bqhd,bkhd->bhqk', q_blk, kp).astype(jnp.float32) * scale
        qidx = j * CHUNK + jnp.arange(CHUNK)
        mask = (kidx[None, :] <= qidx[:, None]) & (kidx[None, :] >= PAD)
        s = jnp.where(mask, s, -1e30)
        p = jax.nn.softmax(s, axis=-1).astype(vp.dtype)
        return jnp.einsum('bhqk,bkhd->bqhd', p, vp)

    o = lax.map(block, jnp.arange(nb))
    o = o.transpose(1, 0, 2, 3, 4).reshape(b, tp, MLA_HEADS * MLA_V)
    return o[:, PAD:]


def mixer_ssd_mla(h, w_in, conv_w, conv_b, dt_bias, a_log, d_skip, ssd_norm_g,
                  q_norm_g, w_q_up, kv_norm_g, w_kv_up, w_out, cos, sin):
    z, xbc, dt_raw, cq, ckv, krope = _split(h @ w_in, IN_SPLITS)
    y_ssd = ssd_branch(z, xbc, dt_raw, conv_w, conv_b, dt_bias, a_log, d_skip, ssd_norm_g)
    y_att = mla_branch(cq, ckv, krope, q_norm_g, w_q_up, kv_norm_g, w_kv_up, cos, sin)
    return jnp.concatenate([y_ssd, y_att], axis=-1) @ w_out


def _lru_combine(c1, c2):
    a1, b1 = c1
    a2, b2 = c2
    return a1 * a2, a2 * b1 + b2


def mixer_rglru(h, w_x, w_y, conv_w, conv_b, w_a, b_a, w_i, b_i, lam, w_out):
    b, t, _ = h.shape
    gate = jax.nn.gelu(h @ w_y)
    xr = causal_dwconv(h @ w_x, conv_w, conv_b)
    xb = xr.reshape(b, t, LRU_BLOCKS, LRU_BLOCK)
    r = jax.nn.sigmoid(jnp.einsum('btni,nij->btnj', xb, w_a).reshape(b, t, LRU_WIDTH) + b_a)
    i = jax.nn.sigmoid(jnp.einsum('btni,nij->btnj', xb, w_i).reshape(b, t, LRU_WIDTH) + b_i)
    log_a = -LRU_C * r.astype(jnp.float32) * jax.nn.softplus(-lam.astype(jnp.float32))
    a = jnp.exp(log_a)
    u = jnp.sqrt(-jnp.expm1(2.0 * log_a)) * (i * xr).astype(jnp.float32)
    _, hs = lax.associative_scan(_lru_combine, (a, u), axis=1)
    return (hs.astype(h.dtype) * gate) @ w_out


def _normal(k, shape, fan_in):
    return jax.random.normal(k, shape, jnp.float32) * (fan_in ** -0.5)


def _fwd_setup_inputs(seed: int = 0) -> dict:
    key = jax.random.key(seed)
    ks = iter(jax.random.split(key, 40))
    gain = lambda k, shape: 1.0 + 0.02 * jax.random.normal(k, shape, jnp.float32)
    small = lambda k, shape: 0.02 * jax.random.normal(k, shape, jnp.float32)
    x = jax.random.normal(next(ks), (BATCH, SEQ, D_MODEL), jnp.float32)
    meta_tokens = jax.random.normal(next(ks), (N_META, D_MODEL), jnp.float32)
    mix_pre_g = gain(next(ks), (DEPTH, D_MODEL))
    mix_post_g = gain(next(ks), (DEPTH, D_MODEL))
    mlp_pre_g = gain(next(ks), (DEPTH, D_MODEL))
    mlp_post_g = gain(next(ks), (DEPTH, D_MODEL))
    w_up = _normal(next(ks), (DEPTH, D_MODEL, D_FF), D_MODEL)
    w_down = _normal(next(ks), (DEPTH, D_FF, D_MODEL), D_FF)
    w_in = _normal(next(ks), (N_EVEN, D_MODEL, D_IN), D_MODEL)
    ssd_conv_w = _normal(next(ks), (N_EVEN, SSD_CONV, SSD_CONV_CH), SSD_CONV)
    ssd_conv_b = small(next(ks), (N_EVEN, SSD_CONV_CH))
    dt = jnp.exp(jax.random.uniform(next(ks), (N_EVEN, SSD_HEADS), jnp.float32,
                                    minval=np.log(1e-3), maxval=np.log(1e-1)))
    ssd_dt_bias = dt + jnp.log(-jnp.expm1(-dt))
    ssd_a_log = jnp.log(jax.random.uniform(next(ks), (N_EVEN, SSD_HEADS), jnp.float32, minval=1.0, maxval=16.0))
    ssd_d = 1.0 + 0.1 * jax.random.normal(next(ks), (N_EVEN, SSD_HEADS), jnp.float32)
    ssd_norm_g = gain(next(ks), (N_EVEN, SSD_D_INNER))
    mla_q_norm_g = gain(next(ks), (N_EVEN, MLA_Q_RANK))
    mla_w_q_up = _normal(next(ks), (N_EVEN, MLA_Q_RANK, MLA_HEADS * (MLA_NOPE + MLA_ROPE)), MLA_Q_RANK)
    mla_kv_norm_g = gain(next(ks), (N_EVEN, MLA_KV_RANK))
    mla_w_kv_up = _normal(next(ks), (N_EVEN, MLA_KV_RANK, MLA_HEADS * (MLA_NOPE + MLA_V)), MLA_KV_RANK)
    w_out_ab = _normal(next(ks), (N_EVEN, AB_WIDTH, D_MODEL), AB_WIDTH)
    rg_w_x = _normal(next(ks), (N_ODD, D_MODEL, LRU_WIDTH), D_MODEL)
    rg_w_y = _normal(next(ks), (N_ODD, D_MODEL, LRU_WIDTH), D_MODEL)
    rg_conv_w = _normal(next(ks), (N_ODD, LRU_CONV, LRU_WIDTH), LRU_CONV)
    rg_conv_b = small(next(ks), (N_ODD, LRU_WIDTH))
    rg_w_a = _normal(next(ks), (N_ODD, LRU_BLOCKS, LRU_BLOCK, LRU_BLOCK), LRU_BLOCK)
    rg_b_a = small(next(ks), (N_ODD, LRU_WIDTH))
    rg_w_i = _normal(next(ks), (N_ODD, LRU_BLOCKS, LRU_BLOCK, LRU_BLOCK), LRU_BLOCK)
    rg_b_i = small(next(ks), (N_ODD, LRU_WIDTH))
    a8 = jax.random.uniform(next(ks), (N_ODD, LRU_WIDTH), jnp.float32, minval=0.9, maxval=0.999)
    base = a8 ** (1.0 / LRU_C)
    rg_lambda = jnp.log(base) - jnp.log1p(-base)
    rg_w_out = _normal(next(ks), (N_ODD, LRU_WIDTH, D_MODEL), LRU_WIDTH)
    return {'x': x, 'meta_tokens': meta_tokens, 'mix_pre_g': mix_pre_g, 'mix_post_g': mix_post_g,
            'mlp_pre_g': mlp_pre_g, 'mlp_post_g': mlp_post_g, 'w_up': w_up, 'w_down': w_down,
            'w_in': w_in, 'ssd_conv_w': ssd_conv_w, 'ssd_conv_b': ssd_conv_b, 'ssd_dt_bias': ssd_dt_bias,
            'ssd_a_log': ssd_a_log, 'ssd_d': ssd_d, 'ssd_norm_g': ssd_norm_g,
            'mla_q_norm_g': mla_q_norm_g, 'mla_w_q_up': mla_w_q_up, 'mla_kv_norm_g': mla_kv_norm_g,
            'mla_w_kv_up': mla_w_kv_up, 'w_out_ab': w_out_ab, 'rg_w_x': rg_w_x, 'rg_w_y': rg_w_y,
            'rg_conv_w': rg_conv_w, 'rg_conv_b': rg_conv_b, 'rg_w_a': rg_w_a, 'rg_b_a': rg_b_a,
            'rg_w_i': rg_w_i, 'rg_b_i': rg_b_i, 'rg_lambda': rg_lambda, 'rg_w_out': rg_w_out}


def _fwd_reference(x, meta_tokens, mix_pre_g, mix_post_g, mlp_pre_g, mlp_post_g, w_up, w_down,
              w_in, ssd_conv_w, ssd_conv_b, ssd_dt_bias, ssd_a_log, ssd_d, ssd_norm_g,
              mla_q_norm_g, mla_w_q_up, mla_kv_norm_g, mla_w_kv_up, w_out_ab,
              rg_w_x, rg_w_y, rg_conv_w, rg_conv_b, rg_w_a, rg_b_a, rg_w_i, rg_b_i, rg_lambda, rg_w_out):
    b = x.shape[0]
    meta = jnp.broadcast_to(meta_tokens[None].astype(x.dtype), (b, N_META, D_MODEL))
    h = jnp.concatenate([meta, x], axis=1)
    cos, sin = rope_tables(h.shape[1], MLA_ROPE)
    for layer in range(DEPTH):
        hn = rmsnorm(h, mix_pre_g[layer])
        if layer % 2 == 0:
            e = layer // 2
            m = mixer_ssd_mla(hn, w_in[e], ssd_conv_w[e], ssd_conv_b[e], ssd_dt_bias[e], ssd_a_log[e],
                              ssd_d[e], ssd_norm_g[e], mla_q_norm_g[e], mla_w_q_up[e], mla_kv_norm_g[e],
                              mla_w_kv_up[e], w_out_ab[e], cos, sin)
        else:
            o = layer // 2
            m = mixer_rglru(hn, rg_w_x[o], rg_w_y[o], rg_conv_w[o], rg_conv_b[o], rg_w_a[o], rg_b_a[o],
                            rg_w_i[o], rg_b_i[o], rg_lambda[o], rg_w_out[o])
        h = h + rmsnorm(m, mix_post_g[layer])
        hn = rmsnorm(h, mlp_pre_g[layer])
        u = jnp.square(jax.nn.relu(hn @ w_up[layer]))
        h = h + rmsnorm(u @ w_down[layer], mlp_post_g[layer])
    return h[:, N_META:]


import jax as _jax
import jax.numpy as _jnp

TWIN_FORMAT = 'train_step'
FWD_PARAMS = ['x', 'meta_tokens', 'mix_pre_g', 'mix_post_g', 'mlp_pre_g', 'mlp_post_g', 'w_up', 'w_down', 'w_in', 'ssd_conv_w', 'ssd_conv_b', 'ssd_dt_bias', 'ssd_a_log', 'ssd_d', 'ssd_norm_g', 'mla_q_norm_g', 'mla_w_q_up', 'mla_kv_norm_g', 'mla_w_kv_up', 'w_out_ab', 'rg_w_x', 'rg_w_y', 'rg_conv_w', 'rg_conv_b', 'rg_w_a', 'rg_b_a', 'rg_w_i', 'rg_b_i', 'rg_lambda', 'rg_w_out']
TWIN_WEIGHTS = ['meta_tokens', 'mix_pre_g', 'mix_post_g', 'mlp_pre_g', 'mlp_post_g', 'w_up', 'w_down', 'w_in', 'ssd_conv_w', 'ssd_conv_b', 'ssd_dt_bias', 'ssd_a_log', 'ssd_d', 'ssd_norm_g', 'mla_q_norm_g', 'mla_w_q_up', 'mla_kv_norm_g', 'mla_w_kv_up', 'w_out_ab', 'rg_w_x', 'rg_w_y', 'rg_conv_w', 'rg_conv_b', 'rg_w_a', 'rg_b_a', 'rg_w_i', 'rg_b_i', 'rg_lambda', 'rg_w_out']
TWIN_DIFF_INPUT = 'x'
TWIN_INPUTS = ['x', 'meta_tokens', 'mix_pre_g', 'mix_post_g', 'mlp_pre_g', 'mlp_post_g', 'w_up', 'w_down', 'w_in', 'ssd_conv_w', 'ssd_conv_b', 'ssd_dt_bias', 'ssd_a_log', 'ssd_d', 'ssd_norm_g', 'mla_q_norm_g', 'mla_w_q_up', 'mla_kv_norm_g', 'mla_w_kv_up', 'w_out_ab', 'rg_w_x', 'rg_w_y', 'rg_conv_w', 'rg_conv_b', 'rg_w_a', 'rg_b_a', 'rg_w_i', 'rg_b_i', 'rg_lambda', 'rg_w_out', 'loss_target', 'm_meta_tokens', 'm_mix_pre_g', 'm_mix_post_g', 'm_mlp_pre_g', 'm_mlp_post_g', 'm_w_up', 'm_w_down', 'm_w_in', 'm_ssd_conv_w', 'm_ssd_conv_b', 'm_ssd_dt_bias', 'm_ssd_a_log', 'm_ssd_d', 'm_ssd_norm_g', 'm_mla_q_norm_g', 'm_mla_w_q_up', 'm_mla_kv_norm_g', 'm_mla_w_kv_up', 'm_w_out_ab', 'm_rg_w_x', 'm_rg_w_y', 'm_rg_conv_w', 'm_rg_conv_b', 'm_rg_w_a', 'm_rg_b_a', 'm_rg_w_i', 'm_rg_b_i', 'm_rg_lambda', 'm_rg_w_out', 'v_meta_tokens', 'v_mix_pre_g', 'v_mix_post_g', 'v_mlp_pre_g', 'v_mlp_post_g', 'v_w_up', 'v_w_down', 'v_w_in', 'v_ssd_conv_w', 'v_ssd_conv_b', 'v_ssd_dt_bias', 'v_ssd_a_log', 'v_ssd_d', 'v_ssd_norm_g', 'v_mla_q_norm_g', 'v_mla_w_q_up', 'v_mla_kv_norm_g', 'v_mla_w_kv_up', 'v_w_out_ab', 'v_rg_w_x', 'v_rg_w_y', 'v_rg_conv_w', 'v_rg_conv_b', 'v_rg_w_a', 'v_rg_b_a', 'v_rg_w_i', 'v_rg_b_i', 'v_rg_lambda', 'v_rg_w_out']
TWIN_OUTPUTS = ['loss', 'grad_x', 'grad_meta_tokens', 'grad_mix_pre_g', 'grad_mix_post_g', 'grad_mlp_pre_g', 'grad_mlp_post_g', 'grad_w_up', 'grad_w_down', 'grad_w_in', 'grad_ssd_conv_w', 'grad_ssd_conv_b', 'grad_ssd_dt_bias', 'grad_ssd_a_log', 'grad_ssd_d', 'grad_ssd_norm_g', 'grad_mla_q_norm_g', 'grad_mla_w_q_up', 'grad_mla_kv_norm_g', 'grad_mla_w_kv_up', 'grad_w_out_ab', 'grad_rg_w_x', 'grad_rg_w_y', 'grad_rg_conv_w', 'grad_rg_conv_b', 'grad_rg_w_a', 'grad_rg_b_a', 'grad_rg_w_i', 'grad_rg_b_i', 'grad_rg_lambda', 'grad_rg_w_out', 'delta_meta_tokens', 'delta_mix_pre_g', 'delta_mix_post_g', 'delta_mlp_pre_g', 'delta_mlp_post_g', 'delta_w_up', 'delta_w_down', 'delta_w_in', 'delta_ssd_conv_w', 'delta_ssd_conv_b', 'delta_ssd_dt_bias', 'delta_ssd_a_log', 'delta_ssd_d', 'delta_ssd_norm_g', 'delta_mla_q_norm_g', 'delta_mla_w_q_up', 'delta_mla_kv_norm_g', 'delta_mla_w_kv_up', 'delta_w_out_ab', 'delta_rg_w_x', 'delta_rg_w_y', 'delta_rg_conv_w', 'delta_rg_conv_b', 'delta_rg_w_a', 'delta_rg_b_a', 'delta_rg_w_i', 'delta_rg_b_i', 'delta_rg_lambda', 'delta_rg_w_out', 'new_m_meta_tokens', 'new_m_mix_pre_g', 'new_m_mix_post_g', 'new_m_mlp_pre_g', 'new_m_mlp_post_g', 'new_m_w_up', 'new_m_w_down', 'new_m_w_in', 'new_m_ssd_conv_w', 'new_m_ssd_conv_b', 'new_m_ssd_dt_bias', 'new_m_ssd_a_log', 'new_m_ssd_d', 'new_m_ssd_norm_g', 'new_m_mla_q_norm_g', 'new_m_mla_w_q_up', 'new_m_mla_kv_norm_g', 'new_m_mla_w_kv_up', 'new_m_w_out_ab', 'new_m_rg_w_x', 'new_m_rg_w_y', 'new_m_rg_conv_w', 'new_m_rg_conv_b', 'new_m_rg_w_a', 'new_m_rg_b_a', 'new_m_rg_w_i', 'new_m_rg_b_i', 'new_m_rg_lambda', 'new_m_rg_w_out', 'new_v_meta_tokens', 'new_v_mix_pre_g', 'new_v_mix_post_g', 'new_v_mlp_pre_g', 'new_v_mlp_post_g', 'new_v_w_up', 'new_v_w_down', 'new_v_w_in', 'new_v_ssd_conv_w', 'new_v_ssd_conv_b', 'new_v_ssd_dt_bias', 'new_v_ssd_a_log', 'new_v_ssd_d', 'new_v_ssd_norm_g', 'new_v_mla_q_norm_g', 'new_v_mla_w_q_up', 'new_v_mla_kv_norm_g', 'new_v_mla_w_kv_up', 'new_v_w_out_ab', 'new_v_rg_w_x', 'new_v_rg_w_y', 'new_v_rg_conv_w', 'new_v_rg_conv_b', 'new_v_rg_w_a', 'new_v_rg_b_a', 'new_v_rg_w_i', 'new_v_rg_b_i', 'new_v_rg_lambda', 'new_v_rg_w_out']
TWIN_LEAF_KINDS = {'loss': 'loss', 'grad_x': 'grad_x', 'grad_meta_tokens': 'grad_w', 'grad_mix_pre_g': 'grad_w', 'grad_mix_post_g': 'grad_w', 'grad_mlp_pre_g': 'grad_w', 'grad_mlp_post_g': 'grad_w', 'grad_w_up': 'grad_w', 'grad_w_down': 'grad_w', 'grad_w_in': 'grad_w', 'grad_ssd_conv_w': 'grad_w', 'grad_ssd_conv_b': 'grad_w', 'grad_ssd_dt_bias': 'grad_w', 'grad_ssd_a_log': 'grad_w', 'grad_ssd_d': 'grad_w', 'grad_ssd_norm_g': 'grad_w', 'grad_mla_q_norm_g': 'grad_w', 'grad_mla_w_q_up': 'grad_w', 'grad_mla_kv_norm_g': 'grad_w', 'grad_mla_w_kv_up': 'grad_w', 'grad_w_out_ab': 'grad_w', 'grad_rg_w_x': 'grad_w', 'grad_rg_w_y': 'grad_w', 'grad_rg_conv_w': 'grad_w', 'grad_rg_conv_b': 'grad_w', 'grad_rg_w_a': 'grad_w', 'grad_rg_b_a': 'grad_w', 'grad_rg_w_i': 'grad_w', 'grad_rg_b_i': 'grad_w', 'grad_rg_lambda': 'grad_w', 'grad_rg_w_out': 'grad_w', 'delta_meta_tokens': 'delta_w', 'delta_mix_pre_g': 'delta_w', 'delta_mix_post_g': 'delta_w', 'delta_mlp_pre_g': 'delta_w', 'delta_mlp_post_g': 'delta_w', 'delta_w_up': 'delta_w', 'delta_w_down': 'delta_w', 'delta_w_in': 'delta_w', 'delta_ssd_conv_w': 'delta_w', 'delta_ssd_conv_b': 'delta_w', 'delta_ssd_dt_bias': 'delta_w', 'delta_ssd_a_log': 'delta_w', 'delta_ssd_d': 'delta_w', 'delta_ssd_norm_g': 'delta_w', 'delta_mla_q_norm_g': 'delta_w', 'delta_mla_w_q_up': 'delta_w', 'delta_mla_kv_norm_g': 'delta_w', 'delta_mla_w_kv_up': 'delta_w', 'delta_w_out_ab': 'delta_w', 'delta_rg_w_x': 'delta_w', 'delta_rg_w_y': 'delta_w', 'delta_rg_conv_w': 'delta_w', 'delta_rg_conv_b': 'delta_w', 'delta_rg_w_a': 'delta_w', 'delta_rg_b_a': 'delta_w', 'delta_rg_w_i': 'delta_w', 'delta_rg_b_i': 'delta_w', 'delta_rg_lambda': 'delta_w', 'delta_rg_w_out': 'delta_w', 'new_m_meta_tokens': 'new_m', 'new_m_mix_pre_g': 'new_m', 'new_m_mix_post_g': 'new_m', 'new_m_mlp_pre_g': 'new_m', 'new_m_mlp_post_g': 'new_m', 'new_m_w_up': 'new_m', 'new_m_w_down': 'new_m', 'new_m_w_in': 'new_m', 'new_m_ssd_conv_w': 'new_m', 'new_m_ssd_conv_b': 'new_m', 'new_m_ssd_dt_bias': 'new_m', 'new_m_ssd_a_log': 'new_m', 'new_m_ssd_d': 'new_m', 'new_m_ssd_norm_g': 'new_m', 'new_m_mla_q_norm_g': 'new_m', 'new_m_mla_w_q_up': 'new_m', 'new_m_mla_kv_norm_g': 'new_m', 'new_m_mla_w_kv_up': 'new_m', 'new_m_w_out_ab': 'new_m', 'new_m_rg_w_x': 'new_m', 'new_m_rg_w_y': 'new_m', 'new_m_rg_conv_w': 'new_m', 'new_m_rg_conv_b': 'new_m', 'new_m_rg_w_a': 'new_m', 'new_m_rg_b_a': 'new_m', 'new_m_rg_w_i': 'new_m', 'new_m_rg_b_i': 'new_m', 'new_m_rg_lambda': 'new_m', 'new_m_rg_w_out': 'new_m', 'new_v_meta_tokens': 'new_v', 'new_v_mix_pre_g': 'new_v', 'new_v_mix_post_g': 'new_v', 'new_v_mlp_pre_g': 'new_v', 'new_v_mlp_post_g': 'new_v', 'new_v_w_up': 'new_v', 'new_v_w_down': 'new_v', 'new_v_w_in': 'new_v', 'new_v_ssd_conv_w': 'new_v', 'new_v_ssd_conv_b': 'new_v', 'new_v_ssd_dt_bias': 'new_v', 'new_v_ssd_a_log': 'new_v', 'new_v_ssd_d': 'new_v', 'new_v_ssd_norm_g': 'new_v', 'new_v_mla_q_norm_g': 'new_v', 'new_v_mla_w_q_up': 'new_v', 'new_v_mla_kv_norm_g': 'new_v', 'new_v_mla_w_kv_up': 'new_v', 'new_v_w_out_ab': 'new_v', 'new_v_rg_w_x': 'new_v', 'new_v_rg_w_y': 'new_v', 'new_v_rg_conv_w': 'new_v', 'new_v_rg_conv_b': 'new_v', 'new_v_rg_w_a': 'new_v', 'new_v_rg_b_a': 'new_v', 'new_v_rg_w_i': 'new_v', 'new_v_rg_b_i': 'new_v', 'new_v_rg_lambda': 'new_v', 'new_v_rg_w_out': 'new_v'}


def _forward(args):
    return _fwd_reference(*[args[k] for k in FWD_PARAMS])


def _output_shape():
    out = _jax.eval_shape(lambda: _forward(_fwd_setup_inputs(0)))
    return out.shape, out.dtype

N_MICROBATCH = 1
ADAM_LR = 0.001
ADAM_B1 = 0.9
ADAM_B2 = 0.999
ADAM_EPS = 1e-08
ADAM_WD = 0.01
ADAM_STEP = 10
PER_EXAMPLE_BATCH_AXIS = {'x': 0, 'loss_target': 0}
SHARED_INPUTS = []
_WEIGHT_DTYPES = {'meta_tokens': _jnp.float32, 'mix_pre_g': _jnp.float32, 'mix_post_g': _jnp.float32, 'mlp_pre_g': _jnp.float32, 'mlp_post_g': _jnp.float32, 'w_up': _jnp.float32, 'w_down': _jnp.float32, 'w_in': _jnp.float32, 'ssd_conv_w': _jnp.float32, 'ssd_conv_b': _jnp.float32, 'ssd_dt_bias': _jnp.float32, 'ssd_a_log': _jnp.float32, 'ssd_d': _jnp.float32, 'ssd_norm_g': _jnp.float32, 'mla_q_norm_g': _jnp.float32, 'mla_w_q_up': _jnp.float32, 'mla_kv_norm_g': _jnp.float32, 'mla_w_kv_up': _jnp.float32, 'w_out_ab': _jnp.float32, 'rg_w_x': _jnp.float32, 'rg_w_y': _jnp.float32, 'rg_conv_w': _jnp.float32, 'rg_conv_b': _jnp.float32, 'rg_w_a': _jnp.float32, 'rg_b_a': _jnp.float32, 'rg_w_i': _jnp.float32, 'rg_b_i': _jnp.float32, 'rg_lambda': _jnp.float32, 'rg_w_out': _jnp.float32}
MOMENT_SCALE = {'meta_tokens': 3.912552e-01, 'mix_pre_g': 1.831223e+01, 'mix_post_g': 3.901683e+01, 'mlp_pre_g': 1.065191e+01, 'mlp_post_g': 4.238381e+01, 'w_up': 5.230004e+00, 'w_down': 2.456060e+01, 'w_in': 5.985191e+00, 'ssd_conv_w': 8.031276e+00, 'ssd_conv_b': 2.459565e+01, 'ssd_dt_bias': 1.153623e+01, 'ssd_a_log': 3.910776e+01, 'ssd_d': 5.220947e+01, 'ssd_norm_g': 1.266258e+01, 'mla_q_norm_g': 5.461663e-01, 'mla_w_q_up': 2.571975e-01, 'mla_kv_norm_g': 1.978594e+01, 'mla_w_kv_up': 6.681365e+00, 'w_out_ab': 1.679414e+01, 'rg_w_x': 1.959919e+01, 'rg_w_y': 1.041252e+01, 'rg_conv_w': 1.983788e+01, 'rg_conv_b': 6.302601e+01, 'rg_w_a': 1.785068e+00, 'rg_b_a': 2.164090e+00, 'rg_w_i': 3.970989e+00, 'rg_b_i': 7.699765e+00, 'rg_lambda': 4.949489e+00, 'rg_w_out': 2.063972e+01}


def _to_microbatches(a, axis):
    t = _jnp.moveaxis(a, axis, 0)
    t = t.reshape((N_MICROBATCH, t.shape[0] // N_MICROBATCH) + t.shape[1:])
    return _jnp.moveaxis(t, 1, axis + 1)


def setup_inputs(seed: int = 0) -> dict:
    inp = _fwd_setup_inputs(seed)
    key = _jax.random.fold_in(_jax.random.key(seed), 7919)
    shape, _ = _output_shape()
    out = dict(inp)
    out["loss_target"] = _jax.random.normal(_jax.random.fold_in(key, 0), shape, _jnp.float32)
    for i, name in enumerate(TWIN_WEIGHTS):
        w = inp[name].astype(_jnp.float32)
        if MOMENT_SCALE is None:
            s = _jnp.sqrt(_jnp.mean(_jnp.square(w)) + 1e-30)
        else:
            s = MOMENT_SCALE[name]
        km, kv = _jax.random.split(_jax.random.fold_in(key, i + 1))
        out[name] = w
        out["m_" + name] = s * _jax.random.normal(km, w.shape, _jnp.float32)
        out["v_" + name] = (s * s) * _jax.random.uniform(kv, w.shape, _jnp.float32, 0.5, 1.5)
    if N_MICROBATCH > 1:
        for name, axis in PER_EXAMPLE_BATCH_AXIS.items():
            out[name] = _to_microbatches(out[name], axis)
    return {'x': out['x'], 'meta_tokens': out['meta_tokens'], 'mix_pre_g': out['mix_pre_g'], 'mix_post_g': out['mix_post_g'], 'mlp_pre_g': out['mlp_pre_g'], 'mlp_post_g': out['mlp_post_g'], 'w_up': out['w_up'], 'w_down': out['w_down'], 'w_in': out['w_in'], 'ssd_conv_w': out['ssd_conv_w'], 'ssd_conv_b': out['ssd_conv_b'], 'ssd_dt_bias': out['ssd_dt_bias'], 'ssd_a_log': out['ssd_a_log'], 'ssd_d': out['ssd_d'], 'ssd_norm_g': out['ssd_norm_g'], 'mla_q_norm_g': out['mla_q_norm_g'], 'mla_w_q_up': out['mla_w_q_up'], 'mla_kv_norm_g': out['mla_kv_norm_g'], 'mla_w_kv_up': out['mla_w_kv_up'], 'w_out_ab': out['w_out_ab'], 'rg_w_x': out['rg_w_x'], 'rg_w_y': out['rg_w_y'], 'rg_conv_w': out['rg_conv_w'], 'rg_conv_b': out['rg_conv_b'], 'rg_w_a': out['rg_w_a'], 'rg_b_a': out['rg_b_a'], 'rg_w_i': out['rg_w_i'], 'rg_b_i': out['rg_b_i'], 'rg_lambda': out['rg_lambda'], 'rg_w_out': out['rg_w_out'], 'loss_target': out['loss_target'], 'm_meta_tokens': out['m_meta_tokens'], 'm_mix_pre_g': out['m_mix_pre_g'], 'm_mix_post_g': out['m_mix_post_g'], 'm_mlp_pre_g': out['m_mlp_pre_g'], 'm_mlp_post_g': out['m_mlp_post_g'], 'm_w_up': out['m_w_up'], 'm_w_down': out['m_w_down'], 'm_w_in': out['m_w_in'], 'm_ssd_conv_w': out['m_ssd_conv_w'], 'm_ssd_conv_b': out['m_ssd_conv_b'], 'm_ssd_dt_bias': out['m_ssd_dt_bias'], 'm_ssd_a_log': out['m_ssd_a_log'], 'm_ssd_d': out['m_ssd_d'], 'm_ssd_norm_g': out['m_ssd_norm_g'], 'm_mla_q_norm_g': out['m_mla_q_norm_g'], 'm_mla_w_q_up': out['m_mla_w_q_up'], 'm_mla_kv_norm_g': out['m_mla_kv_norm_g'], 'm_mla_w_kv_up': out['m_mla_w_kv_up'], 'm_w_out_ab': out['m_w_out_ab'], 'm_rg_w_x': out['m_rg_w_x'], 'm_rg_w_y': out['m_rg_w_y'], 'm_rg_conv_w': out['m_rg_conv_w'], 'm_rg_conv_b': out['m_rg_conv_b'], 'm_rg_w_a': out['m_rg_w_a'], 'm_rg_b_a': out['m_rg_b_a'], 'm_rg_w_i': out['m_rg_w_i'], 'm_rg_b_i': out['m_rg_b_i'], 'm_rg_lambda': out['m_rg_lambda'], 'm_rg_w_out': out['m_rg_w_out'], 'v_meta_tokens': out['v_meta_tokens'], 'v_mix_pre_g': out['v_mix_pre_g'], 'v_mix_post_g': out['v_mix_post_g'], 'v_mlp_pre_g': out['v_mlp_pre_g'], 'v_mlp_post_g': out['v_mlp_post_g'], 'v_w_up': out['v_w_up'], 'v_w_down': out['v_w_down'], 'v_w_in': out['v_w_in'], 'v_ssd_conv_w': out['v_ssd_conv_w'], 'v_ssd_conv_b': out['v_ssd_conv_b'], 'v_ssd_dt_bias': out['v_ssd_dt_bias'], 'v_ssd_a_log': out['v_ssd_a_log'], 'v_ssd_d': out['v_ssd_d'], 'v_ssd_norm_g': out['v_ssd_norm_g'], 'v_mla_q_norm_g': out['v_mla_q_norm_g'], 'v_mla_w_q_up': out['v_mla_w_q_up'], 'v_mla_kv_norm_g': out['v_mla_kv_norm_g'], 'v_mla_w_kv_up': out['v_mla_w_kv_up'], 'v_w_out_ab': out['v_w_out_ab'], 'v_rg_w_x': out['v_rg_w_x'], 'v_rg_w_y': out['v_rg_w_y'], 'v_rg_conv_w': out['v_rg_conv_w'], 'v_rg_conv_b': out['v_rg_conv_b'], 'v_rg_w_a': out['v_rg_w_a'], 'v_rg_b_a': out['v_rg_b_a'], 'v_rg_w_i': out['v_rg_w_i'], 'v_rg_b_i': out['v_rg_b_i'], 'v_rg_lambda': out['v_rg_lambda'], 'v_rg_w_out': out['v_rg_w_out']}


def _loss(weights, diff, rest, loss_target):
    with _jax.named_scope("forward"):
        args = {**rest, TWIN_DIFF_INPUT: diff, **{k: w.astype(_WEIGHT_DTYPES[k]) for k, w in weights.items()}}
        y = _forward(args)
    with _jax.named_scope("loss_head"):
        err = _jnp.square(y.astype(_jnp.float32) - loss_target)
        return 0.5 * _jnp.sum(_jnp.mean(err, axis=-1)) if err.ndim else 0.5 * err


def _adamw(w, g, m, v):
    m = ADAM_B1 * m + (1.0 - ADAM_B1) * g
    v = ADAM_B2 * v + (1.0 - ADAM_B2) * _jnp.square(g)
    m_hat = m / (1.0 - ADAM_B1 ** ADAM_STEP)
    v_hat = v / (1.0 - ADAM_B2 ** ADAM_STEP)
    delta = -ADAM_LR * (m_hat / (_jnp.sqrt(v_hat) + ADAM_EPS) + ADAM_WD * w)
    return delta, m, v


def reference(x, meta_tokens, mix_pre_g, mix_post_g, mlp_pre_g, mlp_post_g, w_up, w_down, w_in, ssd_conv_w, ssd_conv_b, ssd_dt_bias, ssd_a_log, ssd_d, ssd_norm_g, mla_q_norm_g, mla_w_q_up, mla_kv_norm_g, mla_w_kv_up, w_out_ab, rg_w_x, rg_w_y, rg_conv_w, rg_conv_b, rg_w_a, rg_b_a, rg_w_i, rg_b_i, rg_lambda, rg_w_out, loss_target, m_meta_tokens, m_mix_pre_g, m_mix_post_g, m_mlp_pre_g, m_mlp_post_g, m_w_up, m_w_down, m_w_in, m_ssd_conv_w, m_ssd_conv_b, m_ssd_dt_bias, m_ssd_a_log, m_ssd_d, m_ssd_norm_g, m_mla_q_norm_g, m_mla_w_q_up, m_mla_kv_norm_g, m_mla_w_kv_up, m_w_out_ab, m_rg_w_x, m_rg_w_y, m_rg_conv_w, m_rg_conv_b, m_rg_w_a, m_rg_b_a, m_rg_w_i, m_rg_b_i, m_rg_lambda, m_rg_w_out, v_meta_tokens, v_mix_pre_g, v_mix_post_g, v_mlp_pre_g, v_mlp_post_g, v_w_up, v_w_down, v_w_in, v_ssd_conv_w, v_ssd_conv_b, v_ssd_dt_bias, v_ssd_a_log, v_ssd_d, v_ssd_norm_g, v_mla_q_norm_g, v_mla_w_q_up, v_mla_kv_norm_g, v_mla_w_kv_up, v_w_out_ab, v_rg_w_x, v_rg_w_y, v_rg_conv_w, v_rg_conv_b, v_rg_w_a, v_rg_b_a, v_rg_w_i, v_rg_b_i, v_rg_lambda, v_rg_w_out):
    given = dict(x=x, meta_tokens=meta_tokens, mix_pre_g=mix_pre_g, mix_post_g=mix_post_g, mlp_pre_g=mlp_pre_g, mlp_post_g=mlp_post_g, w_up=w_up, w_down=w_down, w_in=w_in, ssd_conv_w=ssd_conv_w, ssd_conv_b=ssd_conv_b, ssd_dt_bias=ssd_dt_bias, ssd_a_log=ssd_a_log, ssd_d=ssd_d, ssd_norm_g=ssd_norm_g, mla_q_norm_g=mla_q_norm_g, mla_w_q_up=mla_w_q_up, mla_kv_norm_g=mla_kv_norm_g, mla_w_kv_up=mla_w_kv_up, w_out_ab=w_out_ab, rg_w_x=rg_w_x, rg_w_y=rg_w_y, rg_conv_w=rg_conv_w, rg_conv_b=rg_conv_b, rg_w_a=rg_w_a, rg_b_a=rg_b_a, rg_w_i=rg_w_i, rg_b_i=rg_b_i, rg_lambda=rg_lambda, rg_w_out=rg_w_out, loss_target=loss_target, m_meta_tokens=m_meta_tokens, m_mix_pre_g=m_mix_pre_g, m_mix_post_g=m_mix_post_g, m_mlp_pre_g=m_mlp_pre_g, m_mlp_post_g=m_mlp_post_g, m_w_up=m_w_up, m_w_down=m_w_down, m_w_in=m_w_in, m_ssd_conv_w=m_ssd_conv_w, m_ssd_conv_b=m_ssd_conv_b, m_ssd_dt_bias=m_ssd_dt_bias, m_ssd_a_log=m_ssd_a_log, m_ssd_d=m_ssd_d, m_ssd_norm_g=m_ssd_norm_g, m_mla_q_norm_g=m_mla_q_norm_g, m_mla_w_q_up=m_mla_w_q_up, m_mla_kv_norm_g=m_mla_kv_norm_g, m_mla_w_kv_up=m_mla_w_kv_up, m_w_out_ab=m_w_out_ab, m_rg_w_x=m_rg_w_x, m_rg_w_y=m_rg_w_y, m_rg_conv_w=m_rg_conv_w, m_rg_conv_b=m_rg_conv_b, m_rg_w_a=m_rg_w_a, m_rg_b_a=m_rg_b_a, m_rg_w_i=m_rg_w_i, m_rg_b_i=m_rg_b_i, m_rg_lambda=m_rg_lambda, m_rg_w_out=m_rg_w_out, v_meta_tokens=v_meta_tokens, v_mix_pre_g=v_mix_pre_g, v_mix_post_g=v_mix_post_g, v_mlp_pre_g=v_mlp_pre_g, v_mlp_post_g=v_mlp_post_g, v_w_up=v_w_up, v_w_down=v_w_down, v_w_in=v_w_in, v_ssd_conv_w=v_ssd_conv_w, v_ssd_conv_b=v_ssd_conv_b, v_ssd_dt_bias=v_ssd_dt_bias, v_ssd_a_log=v_ssd_a_log, v_ssd_d=v_ssd_d, v_ssd_norm_g=v_ssd_norm_g, v_mla_q_norm_g=v_mla_q_norm_g, v_mla_w_q_up=v_mla_w_q_up, v_mla_kv_norm_g=v_mla_kv_norm_g, v_mla_w_kv_up=v_mla_w_kv_up, v_w_out_ab=v_w_out_ab, v_rg_w_x=v_rg_w_x, v_rg_w_y=v_rg_w_y, v_rg_conv_w=v_rg_conv_w, v_rg_conv_b=v_rg_conv_b, v_rg_w_a=v_rg_w_a, v_rg_b_a=v_rg_b_a, v_rg_w_i=v_rg_w_i, v_rg_b_i=v_rg_b_i, v_rg_lambda=v_rg_lambda, v_rg_w_out=v_rg_w_out)
    weights = {n: given[n] for n in TWIN_WEIGHTS}
    shared = {n: given[n] for n in SHARED_INPUTS}
    per_example = {n: given[n] for n in ['x']}
    grad_fn = _jax.value_and_grad(_loss, argnums=(0, 1))

    def one_microbatch(ex, loss_target):
        ex = dict(ex)
        diff = ex.pop(TWIN_DIFF_INPUT)
        return grad_fn(weights, diff, {**shared, **ex}, loss_target)

    if N_MICROBATCH == 1:
        loss, (grad_w, grad_x) = one_microbatch(per_example, given["loss_target"])
    else:
        def body(carry, xs):
            loss_sum, grad_sum = carry
            l_k, (gw_k, gx_k) = one_microbatch(xs[0], xs[1])
            with _jax.named_scope("update"):
                return (loss_sum + l_k, _jax.tree.map(_jnp.add, grad_sum, gw_k)), gx_k

        init = (_jnp.zeros((), _jnp.float32), _jax.tree.map(_jnp.zeros_like, weights))
        (loss, grad_w), grad_x = _jax.lax.scan(body, init, (per_example, given["loss_target"]))
    with _jax.named_scope("update"):
        delta_w, new_m, new_v = {}, {}, {}
        for n in TWIN_WEIGHTS:
            delta_w[n], new_m[n], new_v[n] = _adamw(weights[n], grad_w[n], given["m_" + n], given["v_" + n])
    return (loss, grad_x, *[grad_w[n] for n in TWIN_WEIGHTS], *[delta_w[n] for n in TWIN_WEIGHTS],
            *[new_m[n] for n in TWIN_WEIGHTS], *[new_v[n] for n in TWIN_WEIGHTS])
```

```python
import functools

import jax
import jax.numpy as jnp
import numpy as np
from jax import lax
from jax.experimental import pallas as pl
from jax.experimental.pallas import tpu as pltpu

F32 = jnp.float32
BF16 = jnp.bfloat16

D_MODEL = 1024
DEPTH = 4
N_META = 16
CHUNK = 128
PAD = CHUNK - N_META
EPS = 1e-6
SSD_HEADS = 16
SSD_HEAD_DIM = 64
SSD_D_INNER = SSD_HEADS * SSD_HEAD_DIM
SSD_GROUPS = 2
SSD_STATE = 128
SSD_CONV_CH = SSD_D_INNER + 2 * SSD_GROUPS * SSD_STATE
MLA_HEADS = 16
MLA_NOPE = 64
MLA_ROPE = 32
MLA_V = 64
MLA_Q_RANK = 384
MLA_KV_RANK = 256
ROPE_BASE = 10000.0
LRU_WIDTH = 1280
LRU_BLOCKS = 10
LRU_BLOCK = 128
LRU_C = 8.0
D_FF = 4 * D_MODEL
ADAM_LR, ADAM_B1, ADAM_B2, ADAM_EPS, ADAM_WD, ADAM_STEP = 0.001, 0.9, 0.999, 1e-08, 0.01, 10

LANES = 128
VMEM_LIMIT = 56 * 1024 * 1024
HEAD_SLOT = 128
PROJ_Z, PROJ_XBC, PROJ_DT, PROJ_CQ, PROJ_CKV, PROJ_KR = 0, 1024, 2560, 2688, 3072, 3328
PROJ_W = 3456


def _tile(n, cap, mult=8):
    for t in range(min(n, cap), 0, -1):
        if n % t == 0 and t % mult == 0:
            return t
    return n


def _params(sem):
    return pltpu.CompilerParams(dimension_semantics=sem, vmem_limit_bytes=VMEM_LIMIT)


def _full_spec(shape, ngrid):
    nd = len(shape)
    if ngrid == 1:
        return pl.BlockSpec(shape, lambda i: (0,) * nd)
    if ngrid == 2:
        return pl.BlockSpec(shape, lambda i, j: (0,) * nd)
    return pl.BlockSpec(shape, lambda i, j, k: (0,) * nd)


_DIMS = {"nn": (((1,), (0,)), ((), ())), "nt": (((1,), (1,)), ((), ())), "tn": (((0,), (0,)), ((), ()))}


def _mm(a, b, mode, name, out_dtype=F32, add=None):
    if mode == "nn":
        (m, kc), (_, n) = a.shape, b.shape
    elif mode == "nt":
        (m, kc), (n, _) = a.shape, b.shape
    else:
        (kc, m), (_, n) = a.shape, b.shape
    tm = _tile(m, 1024, LANES) if mode == "tn" else _tile(m, 528, 16)
    tn = _tile(n, 1280, LANES)
    tk = _tile(kc, 1024 if mode != "tn" else 528, LANES if mode != "tn" else 16)
    nk = kc // tk
    if mode == "nn":
        a_spec = pl.BlockSpec((tm, tk), lambda i, j, k: (i, k))
        b_spec = pl.BlockSpec((tk, tn), lambda i, j, k: (k, j))
    elif mode == "nt":
        a_spec = pl.BlockSpec((tm, tk), lambda i, j, k: (i, k))
        b_spec = pl.BlockSpec((tn, tk), lambda i, j, k: (j, k))
    else:
        a_spec = pl.BlockSpec((tk, tm), lambda i, j, k: (k, i))
        b_spec = pl.BlockSpec((tk, tn), lambda i, j, k: (k, j))
    dims = _DIMS[mode]
    o_spec = pl.BlockSpec((tm, tn), lambda i, j, k: (i, j))
    nadd = 0 if add is None else 1

    def body(a_ref, b_ref, *rest):
        o_ref, acc = rest[nadd], rest[nadd + 1:]
        p = lax.dot_general(a_ref[...].astype(BF16), b_ref[...].astype(BF16), dims, preferred_element_type=F32)

        def emit(v):
            o_ref[...] = (v + rest[0][...] if nadd else v).astype(o_ref.dtype)

        if nk == 1:
            emit(p)
        else:
            k = pl.program_id(2)

            @pl.when(k == 0)
            def _():
                acc[0][...] = p

            @pl.when(k > 0)
            def _():
                acc[0][...] += p

            @pl.when(k == nk - 1)
            def _():
                emit(acc[0][...])

    return pl.pallas_call(
        body, name=name, grid=(m // tm, n // tn, nk),
        in_specs=[a_spec, b_spec] + [o_spec] * nadd, out_specs=o_spec,
        out_shape=jax.ShapeDtypeStruct((m, n), out_dtype),
        scratch_shapes=[pltpu.VMEM((tm, tn), F32)] if nk > 1 else [],
        compiler_params=_params(("parallel", "parallel", "arbitrary")),
    )(a, b, *([add] if nadd else []))


def _rowarg(r):
    return r if isinstance(r, tuple) else (r, r.shape[1], 0)


def _rowspec(r, tr, ncol):
    _, w, cb = r
    if ncol > 1:
        return pl.BlockSpec((tr, w // ncol), lambda j, i: (i, j))
    return pl.BlockSpec((tr, w), lambda j, i: (i, cb))


def _rowwise(name, f, rows, params, outs, tr=None, ncol=1):
    rows = [_rowarg(r) for r in rows]
    t = rows[0][0].shape[0]
    tr = tr or _tile(t, 528)
    nr, npm = len(rows), len(params)

    def body(*refs):
        vals = [r[...] for r in refs[:nr]] + [(p[0] if ncol > 1 else p[...]) for p in refs[nr:nr + npm]]
        res = f(pl.program_id(1) * tr, *vals)
        for o_ref, v in zip(refs[nr + npm:], res):
            o_ref[...] = v.astype(o_ref.dtype)

    def pspec(p):
        if ncol > 1:
            return pl.BlockSpec((1,) + p.shape[1:], lambda j, i, n=p.ndim: (j,) + (0,) * (n - 1))
        return _full_spec(p.shape, 2)

    return pl.pallas_call(
        body, name=name, grid=(ncol, t // tr),
        in_specs=[_rowspec(r, tr, ncol) for r in rows] + [pspec(p) for p in params],
        out_specs=[pl.BlockSpec((tr, w // ncol), lambda j, i: (i, j)) for w, _ in outs],
        out_shape=[jax.ShapeDtypeStruct((t, w), dt) for w, dt in outs],
        compiler_params=_params(("parallel", "parallel")),
    )(*[r[0] for r in rows], *params)


def _rowwise_vjp(name, f, rows, params, cts, tr=None, ncol=1, row_dtypes=None):
    rows = [_rowarg(r) for r in rows]
    cts = [_rowarg(c) for c in cts]
    t = rows[0][0].shape[0]
    tr = tr or _tile(t, 528)
    nr, npm, nc = len(rows), len(params), len(cts)
    row_dtypes = row_dtypes or [F32] * nr

    def body(*refs):
        i = pl.program_id(1)
        vals = [r[...] for r in refs[:nr]] + [(p[0] if ncol > 1 else p[...]) for p in refs[nr:nr + npm]]
        ct = tuple(c[...].astype(F32) for c in refs[nr + npm:nr + npm + nc])
        _, vjp = jax.vjp(lambda *a: tuple(f(i * tr, *a)), *vals)
        g = vjp(ct)
        outs = refs[nr + npm + nc:]
        for o_ref, v in zip(outs[:nr], g[:nr]):
            o_ref[...] = v.astype(o_ref.dtype)
        pg = [(v[None] if ncol > 1 else v) for v in g[nr:]]

        @pl.when(i == 0)
        def _():
            for o_ref, v in zip(outs[nr:], pg):
                o_ref[...] = v

        @pl.when(i > 0)
        def _():
            for o_ref, v in zip(outs[nr:], pg):
                o_ref[...] += v

    def pspec(p):
        if ncol > 1:
            return pl.BlockSpec((1,) + p.shape[1:], lambda j, i, n=p.ndim: (j,) + (0,) * (n - 1))
        return _full_spec(p.shape, 2)

    res = pl.pallas_call(
        body, name=name, grid=(ncol, t // tr),
        in_specs=[_rowspec(r, tr, ncol) for r in rows] + [pspec(p) for p in params] + [_rowspec(c, tr, ncol) for c in cts],
        out_specs=[pl.BlockSpec((tr, w // ncol), lambda j, i: (i, j)) for _, w, _ in rows] + [pspec(p) for p in params],
        out_shape=[jax.ShapeDtypeStruct((t, w), dt) for (_, w, _), dt in zip(rows, row_dtypes)]
        + [jax.ShapeDtypeStruct(p.shape, F32) for p in params],
        compiler_params=_params(("parallel", "arbitrary")),
    )(*[r[0] for r in rows], *params, *[c[0] for c in cts])
    return res[:nr], res[nr:]


def _valid(row0, tr):
    return (row0 + lax.broadcasted_iota(jnp.int32, (tr, 1), 0)) >= PAD


def _rms(x, g):
    return x * lax.rsqrt(jnp.mean(x * x, axis=-1, keepdims=True) + EPS) * g


def _softplus(x):
    return jnp.where(x < -15.0, jnp.exp(x), jnp.maximum(x, 0.0) + jnp.log(1.0 + jnp.exp(-jnp.abs(x))))


def _neg_expm1(z):
    return jnp.where(z > -0.01, -z * (1.0 + z * (0.5 + z * (1.0 / 6.0))), 1.0 - jnp.exp(z))


def _prenorm(h, g, name):
    return _rowwise(name, lambda r0, x, gg: (_rms(x, gg),), [h], [g], [(_rowarg(h)[1], BF16)])[0]


def _add_postnorm(h, ms, g, name):
    def f(r0, x, *rest):
        return (x + _rms(functools.reduce(jnp.add, rest[:-1]), rest[-1]),)

    return _rowwise(name, f, [h] + list(ms), [g], [(h.shape[1], F32)])[0]


def _sum_rows(ms, name):
    return _rowwise(name, lambda r0, *a: (functools.reduce(jnp.add, a),), list(ms), [], [(ms[0].shape[1], F32)])[0]


def _postnorm_bwd(m, g, dh, name):
    (dm,), (dg,) = _rowwise_vjp(name, lambda r0, mm, gg: (_rms(mm, gg),), [m], [g], [dh])
    return dm, dg


def _prenorm_bwd_add(h, g, dhns, dh, name):
    t, w = h.shape
    tr = _tile(t, 528)
    nd = len(dhns)

    def body(h_ref, g_ref, *refs):
        dh_ref, o_ref, dg_ref = refs[nd:]
        i = pl.program_id(0)
        _, vjp = jax.vjp(_rms, h_ref[...], g_ref[...])
        dhn = refs[0][...].astype(F32)
        for r in refs[1:nd]:
            dhn = dhn + r[...].astype(F32)
        dx, dg = vjp(dhn)
        o_ref[...] = dh_ref[...] + dx

        @pl.when(i == 0)
        def _():
            dg_ref[...] = dg

        @pl.when(i > 0)
        def _():
            dg_ref[...] += dg

    row = pl.BlockSpec((tr, w), lambda i: (i, 0))
    return pl.pallas_call(
        body, name=name, grid=(t // tr,), in_specs=[row, _full_spec(g.shape, 1)] + [row] * (nd + 1),
        out_specs=[row, _full_spec(g.shape, 1)],
        out_shape=[jax.ShapeDtypeStruct((t, w), F32), jax.ShapeDtypeStruct(g.shape, F32)],
        compiler_params=_params(("arbitrary",)),
    )(h, g, *dhns, dh)


def _relu2(a, name):
    return _rowwise(name, lambda r0, x: (jnp.square(jnp.maximum(x, 0.0)),), [a], [], [(a.shape[1], BF16)], tr=_tile(a.shape[0], 264))[0]


def _relu2_bwd(a, du, name):
    return _rowwise(name, lambda r0, x, d: (2.0 * jnp.maximum(x, 0.0) * d,), [a, du], [], [(a.shape[1], BF16)],
                    tr=_tile(a.shape[0], 264))[0]


def _loss_and_grad(h, target, name):
    t, w = h.shape
    nb = t // CHUNK

    def body(h_ref, t_ref, s_ref, dh_ref):
        i = pl.program_id(0)

        @pl.when(i == 0)
        def _():
            s_ref[...] = jnp.zeros_like(s_ref)
            dh_ref[...] = jnp.zeros_like(dh_ref)

        @pl.when(i > 0)
        def _():
            err = h_ref[...] - t_ref[...]
            s_ref[...] += jnp.sum(err * err)
            dh_ref[...] = err * (1.0 / w)

    return pl.pallas_call(
        body, name=name, grid=(nb,),
        in_specs=[pl.BlockSpec((CHUNK, w), lambda i: (i, 0)), pl.BlockSpec((CHUNK, w), lambda i: (jnp.maximum(i - 1, 0), 0))],
        out_specs=[_full_spec((1, LANES), 1), pl.BlockSpec((CHUNK, w), lambda i: (i, 0))],
        out_shape=[jax.ShapeDtypeStruct((1, LANES), F32), jax.ShapeDtypeStruct((t, w), F32)],
        compiler_params=_params(("arbitrary",)),
    )(h, target)


def _mlp_fwd(h, p, l):
    hn = _prenorm(h, p["mlp_pre_g"][l], "mlp_prenorm")
    a = _mm(hn, p["w_up"][l], "nn", "mlp_up")
    u = _relu2(a, "mlp_relu2")
    d = _mm(u, p["w_down"][l], "nn", "mlp_down")
    h2 = _add_postnorm(h, [d], p["mlp_post_g"][l], "mlp_postnorm")
    return h2, (h, hn, a, u, d)


def _mlp_bwd(dh, saved, p, l, grads):
    h, hn, a, u, d = saved
    dd, grads["mlp_post_g"][l] = _postnorm_bwd(d, p["mlp_post_g"][l], dh, "mlp_postnorm_bwd")
    grads["w_down"][l] = _mm(u, dd, "tn", "mlp_down_dw")
    du = _mm(dd, p["w_down"][l], "nt", "mlp_down_dx")
    da = _relu2_bwd(a, du, "mlp_relu2_bwd")
    grads["w_up"][l] = _mm(hn, da, "tn", "mlp_up_dw")
    dhn = _mm(da, p["w_up"][l], "nt", "mlp_up_dx")
    dh, grads["mlp_pre_g"][l] = _prenorm_bwd_add(h, p["mlp_pre_g"][l], [dhn], dh, "mlp_prenorm_bwd")
    return dh


def _dot(a, b, mode):
    return lax.dot_general(a.astype(BF16), b.astype(BF16), _DIMS[mode], preferred_element_type=F32)


@jax.custom_vjp
def _bnn(a, b):
    return _dot(a, b, "nn")


_bnn.defvjp(lambda a, b: (_dot(a, b, "nn"), (a, b)), lambda r, ct: (_dot(ct, r[1], "nt"), _dot(r[0], ct, "tn")))


@jax.custom_vjp
def _bnt(a, b):
    return _dot(a, b, "nt")


_bnt.defvjp(lambda a, b: (_dot(a, b, "nt"), (a, b)), lambda r, ct: (_dot(ct, r[1], "nn"), _dot(ct, r[0], "tn")))


@jax.custom_vjp
def _btn(a, b):
    return _dot(a, b, "tn")


_btn.defvjp(lambda a, b: (_dot(a, b, "tn"), (a, b)), lambda r, ct: (_dot(r[1], ct, "nt"), _dot(r[0], ct, "nn")))


CONV_K = 4
HALO = 8


def _conv_fwd(x, w, b, name, cw, c0=0):
    t, c = x.shape[0], w.shape[1]
    tr = _tile(t, 528)
    hb = tr // HALO

    def body(x_ref, halo_ref, w_ref, b_ref, o_ref, ext):
        i = pl.program_id(1)
        ext[pl.ds(0, HALO), :] = jnp.where(i > 0, halo_ref[...], 0.0)
        ext[pl.ds(HALO, tr), :] = x_ref[...]
        acc = jnp.broadcast_to(b_ref[...], (tr, cw))
        for k in range(CONV_K):
            acc = acc + w_ref[pl.ds(k, 1), :] * ext[pl.ds(HALO - (CONV_K - 1) + k, tr), :]
        o_ref[...] = acc

    return pl.pallas_call(
        body, name=name, grid=(c // cw, t // tr),
        in_specs=[pl.BlockSpec((tr, cw), lambda j, i: (i, c0 + j)),
                  pl.BlockSpec((HALO, cw), lambda j, i: (jnp.maximum(i * hb - 1, 0), c0 + j)),
                  pl.BlockSpec((CONV_K, cw), lambda j, i: (0, j)), pl.BlockSpec((1, cw), lambda j, i: (0, j))],
        out_specs=pl.BlockSpec((tr, cw), lambda j, i: (i, j)),
        out_shape=jax.ShapeDtypeStruct((t, c), F32),
        scratch_shapes=[pltpu.VMEM((tr + HALO, cw), F32)],
        compiler_params=_params(("parallel", "parallel")),
    )(x, x, w, b)


def _conv_bwd(x, w, dy, name, cw, c0=0):
    t, c = x.shape[0], w.shape[1]
    tr = _tile(t, 528)
    hb = tr // HALO
    nb = t // tr

    def body(x_ref, xh_ref, w_ref, dy_ref, dyh_ref, dx_ref, dw_ref, db_ref, xe, de):
        c = cw
        i = pl.program_id(1)
        xe[pl.ds(0, HALO), :] = jnp.where(i > 0, xh_ref[...], 0.0)
        xe[pl.ds(HALO, tr), :] = x_ref[...]
        de[pl.ds(0, tr), :] = dy_ref[...]
        de[pl.ds(tr, HALO), :] = jnp.where(i < nb - 1, dyh_ref[...], 0.0)
        dy = dy_ref[...]
        acc = jnp.zeros((tr, c), F32)
        dw = jnp.zeros((CONV_K, c), F32)
        rows = lax.broadcasted_iota(jnp.int32, (CONV_K, 1), 0)
        for k in range(CONV_K):
            acc = acc + w_ref[pl.ds(k, 1), :] * de[pl.ds(CONV_K - 1 - k, tr), :]
            dwk = jnp.sum(dy * xe[pl.ds(HALO - (CONV_K - 1) + k, tr), :], axis=0, keepdims=True)
            dw = dw + jnp.where(rows == k, dwk, 0.0)
        dx_ref[...] = jnp.where(_valid(i * tr, tr), acc, 0.0)
        db = jnp.sum(dy, axis=0, keepdims=True)

        @pl.when(i == 0)
        def _():
            dw_ref[...] = dw
            db_ref[...] = db

        @pl.when(i > 0)
        def _():
            dw_ref[...] += dw
            db_ref[...] += db

    row = pl.BlockSpec((tr, cw), lambda j, i: (i, j))
    return pl.pallas_call(
        body, name=name, grid=(c // cw, nb),
        in_specs=[pl.BlockSpec((tr, cw), lambda j, i: (i, c0 + j)),
                  pl.BlockSpec((HALO, cw), lambda j, i: (jnp.maximum(i * hb - 1, 0), c0 + j)),
                  pl.BlockSpec((CONV_K, cw), lambda j, i: (0, j)),
                  row, pl.BlockSpec((HALO, cw), lambda j, i: (jnp.minimum((i + 1) * hb, t // HALO - 1), j))],
        out_specs=[row, pl.BlockSpec((CONV_K, cw), lambda j, i: (0, j)), pl.BlockSpec((1, cw), lambda j, i: (0, j))],
        out_shape=[jax.ShapeDtypeStruct((t, c), F32), jax.ShapeDtypeStruct((CONV_K, c), F32), jax.ShapeDtypeStruct((1, c), F32)],
        scratch_shapes=[pltpu.VMEM((tr + HALO, cw), F32), pltpu.VMEM((tr + HALO, cw), F32)],
        compiler_params=_params(("parallel", "arbitrary")),
    )(x, x, w, dy, dy)


SUB = 8


def _lru_scan(a, u, name):
    t, c = a.shape
    tr = _tile(t, 528)

    def body(a_ref, u_ref, o_ref, carry):
        @pl.when(pl.program_id(0) == 0)
        def _():
            carry[...] = jnp.zeros_like(carry)

        rows = lax.broadcasted_iota(jnp.int32, (SUB, 1), 0)

        def step(k, cin):
            r = pl.multiple_of(k * SUB, SUB)
            av, uv = a_ref[pl.ds(r, SUB), :], u_ref[pl.ds(r, SUB), :]
            for d in (1, 2, 4):
                m = rows >= d
                uv = uv + av * jnp.where(m, pltpu.roll(uv, d, 0), 0.0)
                av = av * jnp.where(m, pltpu.roll(av, d, 0), 1.0)
            hv = uv + av * cin
            o_ref[pl.ds(r, SUB), :] = hv
            return jnp.broadcast_to(hv[SUB - 1:SUB, :], (SUB, c))

        carry[...] = lax.fori_loop(0, tr // SUB, step, carry[...])

    row = pl.BlockSpec((tr, c), lambda i: (i, 0))
    return pl.pallas_call(
        body, name=name, grid=(t // tr,), in_specs=[row, row], out_specs=row,
        out_shape=jax.ShapeDtypeStruct((t, c), F32), scratch_shapes=[pltpu.VMEM((SUB, c), F32)],
        compiler_params=_params(("arbitrary",)),
    )(a, u)


def _lru_scan_bwd(a, hs, dy, name):
    t, c = a.shape
    tr = _tile(t, 528)
    nb, nt = t // tr, tr // SUB

    def body(a_ref, h_ref, hh_ref, dy_ref, du_ref, da_ref, gcar, acar):
        i = pl.program_id(0)

        @pl.when(i == 0)
        def _():
            gcar[...] = jnp.zeros_like(gcar)
            acar[...] = jnp.zeros_like(acar)

        rows = lax.broadcasted_iota(jnp.int32, (SUB, 1), 0)
        hhalo = jnp.where(i < nb - 1, hh_ref[...], 0.0)

        def step(kk, car):
            gin, a_next_first = car
            k = nt - 1 - kk
            r = pl.multiple_of(k * SUB, SUB)
            av, hv, dv = a_ref[pl.ds(r, SUB), :], h_ref[pl.ds(r, SUB), :], dy_ref[pl.ds(r, SUB), :]
            rp = pl.multiple_of(jnp.maximum(k - 1, 0) * SUB, SUB)
            hp = jnp.where(k > 0, h_ref[pl.ds(rp, SUB), :], hhalo)
            cv = jnp.where(rows < SUB - 1, pltpu.roll(av, SUB - 1, 0), a_next_first)
            gv = dv
            for d in (1, 2, 4):
                m = rows < SUB - d
                gv = gv + cv * jnp.where(m, pltpu.roll(gv, SUB - d, 0), 0.0)
                cv = cv * jnp.where(m, pltpu.roll(cv, SUB - d, 0), 1.0)
            gv = gv + cv * gin
            hprev = jnp.where(rows >= 1, pltpu.roll(hv, 1, 0), jnp.broadcast_to(hp[SUB - 1:SUB, :], (SUB, c)))
            du_ref[pl.ds(r, SUB), :] = gv
            da_ref[pl.ds(r, SUB), :] = gv * hprev
            return jnp.broadcast_to(gv[0:1, :], (SUB, c)), jnp.broadcast_to(av[0:1, :], (SUB, c))

        g, af = lax.fori_loop(0, nt, step, (gcar[...], acar[...]))
        gcar[...] = g
        acar[...] = af

    hb = tr // SUB
    row = pl.BlockSpec((tr, c), lambda i: (nb - 1 - i, 0))
    halo = pl.BlockSpec((SUB, c), lambda i: (jnp.maximum((nb - 1 - i) * hb - 1, 0), 0))
    return pl.pallas_call(
        body, name=name, grid=(nb,), in_specs=[row, row, halo, row], out_specs=[row, row],
        out_shape=[jax.ShapeDtypeStruct((t, c), F32)] * 2,
        scratch_shapes=[pltpu.VMEM((SUB, c), F32), pltpu.VMEM((SUB, c), F32)],
        compiler_params=_params(("arbitrary",)),
    )(a, hs, hs, dy)


def _lru_gates(row0, xr, wa, ba, wi, bi, lam):
    r = jax.nn.sigmoid(_bnn(xr, wa) + ba)
    i = jax.nn.sigmoid(_bnn(xr, wi) + bi)
    log_a = -LRU_C * r * _softplus(-lam)
    u = jnp.sqrt(_neg_expm1(2.0 * log_a)) * (i * xr)
    return jnp.exp(log_a), jnp.where(_valid(row0, xr.shape[0]), u, 0.0)


def _lru_gate_out(row0, hs, yw):
    return (hs * jax.nn.gelu(yw),)


def _rglru_fwd(h, p, l, o):
    hn = _prenorm(h, p["mix_pre_g"][l], "rg_prenorm")
    xw = _mm(hn, p["rg_w_x"][o], "nn", "rg_in_x")
    yw = _mm(hn, p["rg_w_y"][o], "nn", "rg_in_y")
    xr = _conv_fwd(xw, p["rg_conv_w"][o], p["rg_conv_b"][o], "rg_conv", cw=LRU_WIDTH // 2)
    gp = [p["rg_w_a"][o], p["rg_b_a"][o], p["rg_w_i"][o], p["rg_b_i"][o], p["rg_lambda"][o]]
    a, u = _rowwise("rg_gates", _lru_gates, [xr], gp, [(LRU_WIDTH, F32)] * 2, ncol=LRU_BLOCKS)
    hs = _lru_scan(a, u, "rg_scan")
    hg = _rowwise("rg_gate_out", _lru_gate_out, [hs, yw], [], [(LRU_WIDTH, BF16)])[0]
    m = _mm(hg, p["rg_w_out"][o], "nn", "rg_out")
    h2 = _add_postnorm(h, [m], p["mix_post_g"][l], "rg_postnorm")
    return h2, (h, hn, xw, yw, xr, a, hs, hg, m)


def _rglru_bwd(dh, saved, p, l, o, grads):
    h, hn, xw, yw, xr, a, hs, hg, m = saved
    dm, grads["mix_post_g"][l] = _postnorm_bwd(m, p["mix_post_g"][l], dh, "rg_postnorm_bwd")
    grads["rg_w_out"][o] = _mm(hg, dm, "tn", "rg_out_dw")
    dhg = _mm(dm, p["rg_w_out"][o], "nt", "rg_out_dx")
    (dhs, dyw), _ = _rowwise_vjp("rg_gate_out_bwd", _lru_gate_out, [hs, yw], [], [dhg])
    du, da = _lru_scan_bwd(a, hs, dhs, "rg_scan_bwd")
    gp = [p["rg_w_a"][o], p["rg_b_a"][o], p["rg_w_i"][o], p["rg_b_i"][o], p["rg_lambda"][o]]
    (dxr,), gg = _rowwise_vjp("rg_gates_bwd", _lru_gates, [xr], gp, [da, du], ncol=LRU_BLOCKS)
    grads["rg_w_a"][o], grads["rg_b_a"][o], grads["rg_w_i"][o], grads["rg_b_i"][o], grads["rg_lambda"][o] = gg
    dxw, grads["rg_conv_w"][o], grads["rg_conv_b"][o] = _conv_bwd(xw, p["rg_conv_w"][o], dxr, "rg_conv_bwd", cw=LRU_WIDTH // 2)
    grads["rg_w_x"][o] = _mm(hn, dxw, "tn", "rg_in_x_dw")
    grads["rg_w_y"][o] = _mm(hn, dyw, "tn", "rg_in_y_dw")
    dhx = _mm(dxw, p["rg_w_x"][o], "nt", "rg_in_x_dx")
    dhy = _mm(dyw, p["rg_w_y"][o], "nt", "rg_in_y_dx")
    dh, grads["mix_pre_g"][l] = _prenorm_bwd_add(h, p["mix_pre_g"][l], [dhx, dhy], dh, "rg_prenorm_bwd")
    return dh


SSD_GW = SSD_D_INNER // SSD_GROUPS
SSD_GH = SSD_HEADS // SSD_GROUPS
XACT_B = SSD_D_INNER // SSD_STATE
XACT_C = XACT_B + SSD_GROUPS


def _hp(a, b, dims=_DIMS["nn"]):
    return lax.dot_general(a, b, dims, precision=lax.Precision.HIGHEST, preferred_element_type=F32)


def _ssd_chunk(xs, bm, cm, dt, da, ht, g):
    l = CHUNK
    ri = lax.broadcasted_iota(jnp.int32, (l, l), 0)
    ci = lax.broadcasted_iota(jnp.int32, (l, l), 1)
    causal = ri >= ci
    tri = causal.astype(F32)
    hr = lax.broadcasted_iota(jnp.int32, (LANES, SSD_GW), 0)
    hc = lax.broadcasted_iota(jnp.int32, (LANES, SSD_GW), 1)
    expand = (hr == g * SSD_GH + hc // SSD_HEAD_DIM).astype(F32)
    acs = _hp(tri, da)
    acs_t = _hp(da, tri, (((0,), (1,)), ((), ())))
    acs_e = _hp(acs, expand)
    x = xs * _hp(dt, expand)
    gmat = _bnt(cm, bm)
    lane = lax.broadcasted_iota(jnp.int32, (1, LANES), 1)
    sub = lax.broadcasted_iota(jnp.int32, (LANES, 1), 0)
    colhead = lax.broadcasted_iota(jnp.int32, (1, SSD_GW), 1) // SSD_HEAD_DIM
    y = _bnn(cm, ht) * jnp.exp(acs_e)
    for k in range(SSD_GH):
        hh = g * SSD_GH + k
        col = jnp.sum(jnp.where(lane == hh, acs, 0.0), axis=1, keepdims=True)
        row = jnp.sum(jnp.where(sub == hh, acs_t, 0.0), axis=0, keepdims=True)
        decay = jnp.exp(jnp.where(causal, col - row, -1e30))
        y = y + _bnn(gmat * decay, jnp.where(colhead == k, x, 0.0))
    last = lax.broadcasted_iota(jnp.int32, (l, 1), 0) == l - 1
    a_last = jnp.sum(jnp.where(last, acs_e, 0.0), axis=0, keepdims=True)
    st = _btn(bm, x * jnp.exp(a_last - acs_e))
    return y, ht * jnp.exp(a_last) + st


def _ssd_specs(nc, rev):
    def cc(c):
        return nc - 1 - c if rev else c

    return [pl.BlockSpec((CHUNK, SSD_GW), lambda c, g: (cc(c), g)),
            pl.BlockSpec((CHUNK, SSD_STATE), lambda c, g: (cc(c), XACT_B + g)),
            pl.BlockSpec((CHUNK, SSD_STATE), lambda c, g: (cc(c), XACT_C + g)),
            pl.BlockSpec((CHUNK, LANES), lambda c, g: (cc(c), 0)),
            pl.BlockSpec((CHUNK, LANES), lambda c, g: (cc(c), 0))]


def _ssd_scan(xact, dt, da, name):
    t = xact.shape[0]
    nc = t // CHUNK

    def body(xs_ref, b_ref, c_ref, dt_ref, da_ref, y_ref, hs_ref, state):
        c, g = pl.program_id(0), pl.program_id(1)

        @pl.when(c == 0)
        def _():
            state[g] = jnp.zeros((SSD_STATE, SSD_GW), F32)

        ht = state[g]
        hs_ref[0] = ht
        y, ht2 = _ssd_chunk(xs_ref[...], b_ref[...], c_ref[...], dt_ref[...], da_ref[...], ht, g)
        y_ref[...] = y
        state[g] = ht2

    return pl.pallas_call(
        body, name=name, grid=(nc, SSD_GROUPS), in_specs=_ssd_specs(nc, False),
        out_specs=[pl.BlockSpec((CHUNK, SSD_GW), lambda c, g: (c, g)),
                   pl.BlockSpec((1, SSD_STATE, SSD_GW), lambda c, g: (c * SSD_GROUPS + g, 0, 0))],
        out_shape=[jax.ShapeDtypeStruct((t, SSD_D_INNER), F32), jax.ShapeDtypeStruct((nc * SSD_GROUPS, SSD_STATE, SSD_GW), F32)],
        scratch_shapes=[pltpu.VMEM((SSD_GROUPS, SSD_STATE, SSD_GW), F32)],
        compiler_params=_params(("arbitrary", "arbitrary")),
    )(xact, xact, xact, dt, da)


def _ssd_scan_bwd(xact, dt, da, hsave, dy, dxskip, name):
    t = xact.shape[0]
    nc = t // CHUNK

    def body(xs_ref, b_ref, c_ref, dt_ref, da_ref, hs_ref, dy_ref, sk_ref, dxs_ref, db_ref, dc_ref, ddt_ref, dda_ref, dstate):
        c, g = pl.program_id(0), pl.program_id(1)

        @pl.when(c == 0)
        def _():
            dstate[g] = jnp.zeros((SSD_STATE, SSD_GW), F32)

        _, vjp = jax.vjp(lambda *a: _ssd_chunk(*a, g), xs_ref[...], b_ref[...], c_ref[...], dt_ref[...], da_ref[...], hs_ref[0])
        dxs, dbm, dcm, ddt, dda, dht = vjp((dy_ref[...], dstate[g]))
        dxs_ref[...] = dxs + sk_ref[...]
        db_ref[...] = dbm
        dc_ref[...] = dcm
        dstate[g] = dht

        @pl.when(g == 0)
        def _():
            ddt_ref[...] = ddt
            dda_ref[...] = dda

        @pl.when(g > 0)
        def _():
            ddt_ref[...] += ddt
            dda_ref[...] += dda

    grp = pl.BlockSpec((CHUNK, SSD_GW), lambda c, g: (nc - 1 - c, g))
    st = pl.BlockSpec((CHUNK, SSD_STATE), lambda c, g: (nc - 1 - c, g))
    hd = pl.BlockSpec((CHUNK, LANES), lambda c, g: (nc - 1 - c, 0))
    return pl.pallas_call(
        body, name=name, grid=(nc, SSD_GROUPS),
        in_specs=_ssd_specs(nc, True) + [pl.BlockSpec((1, SSD_STATE, SSD_GW), lambda c, g: ((nc - 1 - c) * SSD_GROUPS + g, 0, 0)), grp, grp],
        out_specs=[grp, st, st, hd, hd],
        out_shape=[jax.ShapeDtypeStruct((t, SSD_D_INNER), F32), jax.ShapeDtypeStruct((t, SSD_GROUPS * SSD_STATE), F32),
                   jax.ShapeDtypeStruct((t, SSD_GROUPS * SSD_STATE), F32), jax.ShapeDtypeStruct((t, LANES), F32),
                   jax.ShapeDtypeStruct((t, LANES), F32)],
        scratch_shapes=[pltpu.VMEM((SSD_GROUPS, SSD_STATE, SSD_GW), F32)],
        compiler_params=_params(("arbitrary", "arbitrary")),
    )(xact, xact, xact, dt, da, hsave, dy, dxskip)


def _ssd_act(row0, xc):
    return (jnp.where(_valid(row0, xc.shape[0]), jax.nn.silu(xc), 0.0),)


def _ssd_dt(row0, dtraw, dt_bias, a_log):
    dt = jnp.where(_valid(row0, dtraw.shape[0]), _softplus(dtraw + dt_bias), 0.0)
    return dt, dt * -jnp.exp(a_log)


def _ssd_post(row0, y, xs, z, d_skip, norm_g):
    hr = lax.broadcasted_iota(jnp.int32, (LANES, SSD_D_INNER), 0)
    hc = lax.broadcasted_iota(jnp.int32, (LANES, SSD_D_INNER), 1)
    expand = (hr == hc // SSD_HEAD_DIM).astype(F32)
    d_e = jnp.sum(_hp(jnp.broadcast_to(d_skip, (SUB, LANES)), expand), axis=0, keepdims=True) * (1.0 / SUB)
    return (_rms((y + xs * d_e) * jax.nn.silu(z), norm_g),)


ROPE_LO, ROPE_MID, ROPE_HI = MLA_NOPE, MLA_NOPE + MLA_ROPE // 2, MLA_NOPE + MLA_ROPE
ATT_SCALE = (MLA_NOPE + MLA_ROPE) ** -0.5


def _swap_halves(x):
    lane = lax.broadcasted_iota(jnp.int32, (1, LANES), 1)
    sw = jnp.where(lane < ROPE_MID, pltpu.roll(x, LANES - MLA_ROPE // 2, 1), pltpu.roll(x, MLA_ROPE // 2, 1))
    return jnp.where((lane >= ROPE_LO) & (lane < ROPE_HI), sw, 0.0)


def _rope(x, cos, sin):
    return x * cos + _swap_halves(x) * sin


def _rope_t(dy, cos, sin):
    return dy * cos + _swap_halves(dy * sin)


def _att_mask(i, j, blk):
    rowid = i * blk + lax.broadcasted_iota(jnp.int32, (blk, 1), 0)
    colid = j * blk + lax.broadcasted_iota(jnp.int32, (1, blk), 1)
    return (colid <= rowid) & (colid >= PAD)


def _attn_fwd(q, kv, kr, cos, sin, name):
    t = q.shape[0]
    blk = _tile(t, 384, LANES)
    nq = t // blk

    def body(q_ref, kv_ref, kr_ref, cos_ref, sin_ref, o_ref):
        i = pl.program_id(1)
        lane = lax.broadcasted_iota(jnp.int32, (1, LANES), 1)
        qb = _rope(q_ref[...], cos_ref[...], sin_ref[...]).astype(BF16)

        def step(j, car):
            m, l, acc = car
            r = pl.multiple_of(j * blk, blk)
            kvb = kv_ref[pl.ds(r, blk), :]
            kmat = jnp.where(lane < MLA_NOPE, kvb, kr_ref[pl.ds(r, blk), :]).astype(BF16)
            s = lax.dot_general(qb, kmat, _DIMS["nt"], preferred_element_type=F32) * ATT_SCALE
            s = jnp.where(_att_mask(i, j, blk), s, -1e30)
            m2 = jnp.maximum(m, jnp.max(s, axis=1, keepdims=True))
            al = jnp.exp(m - m2)
            pm = jnp.exp(s - m2)
            l2 = al * l + jnp.sum(pm, axis=1, keepdims=True)
            acc2 = al * acc + lax.dot_general(pm.astype(BF16), kvb.astype(BF16), _DIMS["nn"], preferred_element_type=F32)
            return m2, l2, acc2

        init = (jnp.full((blk, 1), -1e30, F32), jnp.zeros((blk, 1), F32), jnp.zeros((blk, LANES), F32))
        m, l, acc = lax.fori_loop(0, i + 1, step, init)
        out = jnp.where(lane >= MLA_NOPE, acc / l, m + jnp.log(l))
        o_ref[...] = jnp.where(_valid(i * blk, blk), out, 0.0)

    return pl.pallas_call(
        body, name=name, grid=(MLA_HEADS, nq),
        in_specs=[pl.BlockSpec((blk, LANES), lambda h, i: (i, h)), pl.BlockSpec((t, LANES), lambda h, i: (0, h)),
                  pl.BlockSpec((t, LANES), lambda h, i: (0, 0)), pl.BlockSpec((blk, LANES), lambda h, i: (i, 0)),
                  pl.BlockSpec((blk, LANES), lambda h, i: (i, 0))],
        out_specs=pl.BlockSpec((blk, LANES), lambda h, i: (i, h)),
        out_shape=jax.ShapeDtypeStruct((t, MLA_HEADS * LANES), F32),
        compiler_params=_params(("parallel", "parallel")),
    )(q, kv, kr, cos, sin)


def _attn_bwd(q, kv, kr, cos, sin, o, do, name):
    t = q.shape[0]
    blk = _tile(t, 384, LANES)
    nq = t // blk

    def body(q_ref, o_ref, do_ref, cos_ref, sin_ref, kv_ref, kr_ref, dq_ref, dkv_ref, dkr_ref):
        h, j = pl.program_id(0), pl.program_id(1)
        lane = lax.broadcasted_iota(jnp.int32, (1, LANES), 1)

        @pl.when(j == 0)
        def _():
            dq_ref[...] = jnp.zeros_like(dq_ref)

        @pl.when((h == 0) & (j == 0))
        def _():
            dkr_ref[...] = jnp.zeros_like(dkr_ref)

        kvb = kv_ref[...]
        kmat = jnp.where(lane < MLA_NOPE, kvb, kr_ref[...]).astype(BF16)
        vmat = kvb.astype(BF16)

        def step(ii, car):
            dk, dv = car
            i = j + ii
            rows = pl.ds(pl.multiple_of(i * blk, blk), blk)
            qb = _rope(q_ref[rows, :], cos_ref[rows, :], sin_ref[rows, :]).astype(BF16)
            ob, dob = o_ref[rows, :], do_ref[rows, :]
            delta = jnp.sum(dob * ob, axis=1, keepdims=True)
            s = lax.dot_general(qb, kmat, _DIMS["nt"], preferred_element_type=F32) * ATT_SCALE
            pm = jnp.exp(jnp.where(_att_mask(i, j, blk), s, -1e30) - ob[:, 0:1])
            dobb = dob.astype(BF16)
            dv = dv + lax.dot_general(pm.astype(BF16), dobb, _DIMS["tn"], preferred_element_type=F32)
            dp = lax.dot_general(dobb, vmat, _DIMS["nt"], preferred_element_type=F32)
            ds = (pm * (dp - delta) * ATT_SCALE).astype(BF16)
            dq_ref[rows, :] += lax.dot_general(ds, kmat, _DIMS["nn"], preferred_element_type=F32)
            dk = dk + lax.dot_general(ds, qb, _DIMS["tn"], preferred_element_type=F32)
            return dk, dv

        zero = jnp.zeros((blk, LANES), F32)
        dk, dv = lax.fori_loop(0, nq - j, step, (zero, zero))
        dkv_ref[...] = jnp.where(lane < MLA_NOPE, dk, dv)
        rows = pl.ds(pl.multiple_of(j * blk, blk), blk)
        dkr_ref[rows, :] += jnp.where(lane >= MLA_NOPE, dk, 0.0)

        @pl.when(j == nq - 1)
        def _():
            dq_ref[...] = _rope_t(dq_ref[...], cos_ref[...], sin_ref[...])

    seq_h = pl.BlockSpec((t, LANES), lambda h, j: (0, h))
    seq = pl.BlockSpec((t, LANES), lambda h, j: (0, 0))
    return pl.pallas_call(
        body, name=name, grid=(MLA_HEADS, nq),
        in_specs=[seq_h, seq_h, seq_h, seq, seq, pl.BlockSpec((blk, LANES), lambda h, j: (j, h)),
                  pl.BlockSpec((blk, LANES), lambda h, j: (j, 0))],
        out_specs=[seq_h, pl.BlockSpec((blk, LANES), lambda h, j: (j, h)), seq],
        out_shape=[jax.ShapeDtypeStruct((t, MLA_HEADS * LANES), F32), jax.ShapeDtypeStruct((t, MLA_HEADS * LANES), F32),
                   jax.ShapeDtypeStruct((t, LANES), F32)],
        compiler_params=_params(("arbitrary", "arbitrary")),
    )(q, o, do, cos, sin, kv, kr)


def _rms_rows(row0, x, g):
    return (_rms(x, g),)


def _ssdmla_fwd(h, p, l, e, cos, sin):
    hn = _prenorm(h, p["mix_pre_g"][l], "sm_prenorm")
    proj = _mm(hn, p["w_in"][e], "nn", "sm_in")
    xc = _conv_fwd(proj, p["ssd_conv_w"][e], p["ssd_conv_b"][e], "ssd_conv", cw=SSD_GW, c0=PROJ_XBC // SSD_GW)
    xact = _rowwise("ssd_act", _ssd_act, [xc], [], [(SSD_CONV_CH, F32)])[0]
    dt, da = _rowwise("ssd_dt", _ssd_dt, [(proj, LANES, PROJ_DT // LANES)], [p["ssd_dt_bias"][e], p["ssd_a_log"][e]],
                      [(LANES, F32)] * 2)
    y, hsave = _ssd_scan(xact, dt, da, "ssd_scan")
    y_ssd = _rowwise("ssd_post", _ssd_post, [y, (xact, SSD_D_INNER, 0), (proj, SSD_D_INNER, 0)],
                     [p["ssd_d"][e], p["ssd_norm_g"][e]], [(SSD_D_INNER, BF16)])[0]
    cqn = _prenorm((proj, MLA_Q_RANK, PROJ_CQ // MLA_Q_RANK), p["mla_q_norm_g"][e], "mla_qnorm")
    ckvn = _prenorm((proj, MLA_KV_RANK, PROJ_CKV // MLA_KV_RANK), p["mla_kv_norm_g"][e], "mla_kvnorm")
    q = _mm(cqn, p["mla_w_q_up"][e], "nn", "mla_q_up")
    kv = _mm(ckvn, p["mla_w_kv_up"][e], "nn", "mla_kv_up")
    kr = _rowwise("mla_krope", lambda r0, x, c, s: (_rope(x, c, s),), [(proj, LANES, PROJ_KR // LANES), cos, sin], [],
                  [(LANES, F32)])[0]
    o = _attn_fwd(q, kv, kr, cos, sin, "mla_attn")
    m1 = _mm(y_ssd, p["w_out_ssd"][e], "nn", "sm_out_ssd")
    m = _mm(o, p["w_out_att"][e], "nn", "sm_out_att", add=m1)
    h2 = _add_postnorm(h, [m], p["mix_post_g"][l], "sm_postnorm")
    return h2, (h, hn, proj, xc, xact, dt, da, y, hsave, y_ssd, cqn, ckvn, q, kv, kr, o, m)


def _ssdmla_bwd(dh, saved, p, l, e, cos, sin, grads):
    h, hn, proj, xc, xact, dt, da, y, hsave, y_ssd, cqn, ckvn, q, kv, kr, o, m = saved
    dm, grads["mix_post_g"][l] = _postnorm_bwd(m, p["mix_post_g"][l], dh, "sm_postnorm_bwd")
    grads["w_out_ssd"][e] = _mm(y_ssd, dm, "tn", "sm_out_ssd_dw")
    grads["w_out_att"][e] = _mm(o, dm, "tn", "sm_out_att_dw")
    dy_ssd = _mm(dm, p["w_out_ssd"][e], "nt", "sm_out_ssd_dx")
    do = _mm(dm, p["w_out_att"][e], "nt", "sm_out_att_dx")
    dq, dkv, dkr = _attn_bwd(q, kv, kr, cos, sin, o, do, "mla_attn_bwd")
    dkr_raw = _rowwise("mla_krope_bwd", lambda r0, d, c, s: (_rope_t(d, c, s),), [dkr, cos, sin], [], [(LANES, F32)])[0]
    grads["mla_w_q_up"][e] = _mm(cqn, dq, "tn", "mla_q_up_dw")
    dcqn = _mm(dq, p["mla_w_q_up"][e], "nt", "mla_q_up_dx")
    (dcq,), (grads["mla_q_norm_g"][e],) = _rowwise_vjp(
        "mla_qnorm_bwd", _rms_rows, [(proj, MLA_Q_RANK, PROJ_CQ // MLA_Q_RANK)], [p["mla_q_norm_g"][e]], [dcqn])
    grads["mla_w_kv_up"][e] = _mm(ckvn, dkv, "tn", "mla_kv_up_dw")
    dckvn = _mm(dkv, p["mla_w_kv_up"][e], "nt", "mla_kv_up_dx")
    (dckv,), (grads["mla_kv_norm_g"][e],) = _rowwise_vjp(
        "mla_kvnorm_bwd", _rms_rows, [(proj, MLA_KV_RANK, PROJ_CKV // MLA_KV_RANK)], [p["mla_kv_norm_g"][e]], [dckvn])
    (dy, dxskip, dz), (grads["ssd_d"][e], grads["ssd_norm_g"][e]) = _rowwise_vjp(
        "ssd_post_bwd", _ssd_post, [y, (xact, SSD_D_INNER, 0), (proj, SSD_D_INNER, 0)], [p["ssd_d"][e], p["ssd_norm_g"][e]], [dy_ssd])
    dxs, db, dc, ddt, dda = _ssd_scan_bwd(xact, dt, da, hsave, dy, dxskip, "ssd_scan_bwd")
    dxact = jnp.concatenate([dxs, db, dc], axis=1)
    (dxc,), _ = _rowwise_vjp("ssd_act_bwd", _ssd_act, [xc], [], [dxact])
    dxbc, grads["ssd_conv_w"][e], grads["ssd_conv_b"][e] = _conv_bwd(
        proj, p["ssd_conv_w"][e], dxc, "ssd_conv_bwd", cw=SSD_GW, c0=PROJ_XBC // SSD_GW)
    (ddtraw,), (grads["ssd_dt_bias"][e], grads["ssd_a_log"][e]) = _rowwise_vjp(
        "ssd_dt_bwd", _ssd_dt, [(proj, LANES, PROJ_DT // LANES)], [p["ssd_dt_bias"][e], p["ssd_a_log"][e]], [ddt, dda])
    dproj = jnp.concatenate([dz, dxbc, ddtraw, dcq, dckv, dkr_raw], axis=1)
    grads["w_in"][e] = _mm(hn, dproj, "tn", "sm_in_dw")
    dhn = _mm(dproj, p["w_in"][e], "nt", "sm_in_dx")
    dh, grads["mix_pre_g"][l] = _prenorm_bwd_add(h, p["mix_pre_g"][l], [dhn], dh, "sm_prenorm_bwd")
    return dh


GAINS = ("mix_pre_g", "mix_post_g", "mlp_pre_g", "mlp_post_g", "ssd_norm_g", "mla_q_norm_g", "mla_kv_norm_g", "ssd_conv_b", "rg_conv_b")
HEAD_VECS = ("ssd_dt_bias", "ssd_a_log", "ssd_d")
LRU_VECS = ("rg_b_a", "rg_b_i", "rg_lambda")
IN_DT_END = SSD_D_INNER + SSD_CONV_CH + SSD_HEADS
IN_KR = IN_DT_END + MLA_Q_RANK + MLA_KV_RANK


def _layout_params(w):
    p = {k: w[k][:, None, :] for k in GAINS}
    for k in HEAD_VECS:
        p[k] = jnp.pad(w[k], ((0, 0), (0, LANES - SSD_HEADS)))[:, None, :]
    for k in LRU_VECS:
        p[k] = w[k].reshape(-1, LRU_BLOCKS, 1, LRU_BLOCK)
    for k in ("w_up", "w_down", "mla_w_kv_up", "rg_w_x", "rg_w_y", "rg_w_out"):
        p[k] = w[k].astype(BF16)
    for k in ("ssd_conv_w", "rg_conv_w", "rg_w_a", "rg_w_i"):
        p[k] = w[k]
    wi = w["w_in"]

    def zcols(n):
        return jnp.zeros(wi.shape[:2] + (n,), wi.dtype)

    p["w_in"] = jnp.concatenate([wi[..., :IN_DT_END], zcols(PROJ_CQ - IN_DT_END), wi[..., IN_DT_END:IN_KR], zcols(ROPE_LO),
                                 wi[..., IN_KR:], zcols(LANES - ROPE_HI)], axis=-1).astype(BF16)
    wq = w["mla_w_q_up"].reshape(-1, MLA_Q_RANK, MLA_HEADS, MLA_NOPE + MLA_ROPE)
    p["mla_w_q_up"] = jnp.pad(wq, ((0, 0), (0, 0), (0, 0), (0, LANES - MLA_NOPE - MLA_ROPE))).reshape(-1, MLA_Q_RANK, MLA_HEADS * LANES).astype(BF16)
    wo = w["w_out_ab"]
    p["w_out_ssd"] = wo[:, :SSD_D_INNER].astype(BF16)
    wa = wo[:, SSD_D_INNER:].reshape(-1, MLA_HEADS, MLA_V, D_MODEL)
    p["w_out_att"] = jnp.pad(wa, ((0, 0), (0, 0), (LANES - MLA_V, 0), (0, 0))).reshape(-1, MLA_HEADS * LANES, D_MODEL).astype(BF16)
    return p


def _natural_grads(g):
    s = {k: jnp.stack(v) for k, v in g.items()}
    out = {k: s[k][:, 0, :] for k in GAINS}
    for k in HEAD_VECS:
        out[k] = s[k][:, 0, :SSD_HEADS]
    for k in LRU_VECS:
        out[k] = s[k].reshape(-1, LRU_WIDTH)
    for k in ("w_up", "w_down", "mla_w_kv_up", "rg_w_x", "rg_w_y", "rg_w_out", "ssd_conv_w", "rg_conv_w", "rg_w_a", "rg_w_i"):
        out[k] = s[k]
    gi = s["w_in"]
    out["w_in"] = jnp.concatenate([gi[..., :IN_DT_END], gi[..., PROJ_CQ:PROJ_KR], gi[..., PROJ_KR + ROPE_LO:PROJ_KR + ROPE_HI]], axis=-1)
    gq = s["mla_w_q_up"].reshape(-1, MLA_Q_RANK, MLA_HEADS, LANES)[..., :MLA_NOPE + MLA_ROPE]
    out["mla_w_q_up"] = gq.reshape(-1, MLA_Q_RANK, MLA_HEADS * (MLA_NOPE + MLA_ROPE))
    ga = s["w_out_att"].reshape(-1, MLA_HEADS, LANES, D_MODEL)[:, :, LANES - MLA_V:, :].reshape(-1, MLA_HEADS * MLA_V, D_MODEL)
    out["w_out_ab"] = jnp.concatenate([s["w_out_ssd"], ga], axis=1)
    return out


def _rope_tables(t):
    pos = (jnp.arange(t) - PAD).astype(F32)
    inv = ROPE_BASE ** (-jnp.arange(0, MLA_ROPE, 2, dtype=F32) / MLA_ROPE)
    ang = pos[:, None] * inv[None, :]
    c, s = jnp.cos(ang), jnp.sin(ang)
    one, zero = jnp.ones((t, MLA_NOPE), F32), jnp.zeros((t, MLA_NOPE), F32)
    tail = LANES - ROPE_HI
    return (jnp.concatenate([one, c, c, one[:, :tail]], axis=1), jnp.concatenate([zero, -s, s, zero[:, :tail]], axis=1))


GRAD_KEYS = GAINS + HEAD_VECS + LRU_VECS + ("w_up", "w_down", "mla_w_kv_up", "rg_w_x", "rg_w_y", "rg_w_out", "ssd_conv_w",
                                            "rg_conv_w", "rg_w_a", "rg_w_i", "w_in", "mla_w_q_up", "w_out_ssd", "w_out_att")


def _device_step(x, meta, target, p):
    t = PAD + N_META + x.shape[0]
    cos, sin = _rope_tables(t)
    h = jnp.concatenate([jnp.zeros((PAD, D_MODEL), F32), meta, x], axis=0)
    n_even, n_odd = (DEPTH + 1) // 2, DEPTH // 2
    saved = []
    for l in range(DEPTH):
        if l % 2 == 0:
            h, sm = _ssdmla_fwd(h, p, l, l // 2, cos, sin)
        else:
            h, sm = _rglru_fwd(h, p, l, l // 2)
        h, sp = _mlp_fwd(h, p, l)
        saved.append((sm, sp))
    sq, dh = _loss_and_grad(h, target, "loss")
    per_layer = {"mix_pre_g": DEPTH, "mix_post_g": DEPTH, "mlp_pre_g": DEPTH, "mlp_post_g": DEPTH, "w_up": DEPTH, "w_down": DEPTH}
    grads = {k: [None] * per_layer.get(k, n_odd if k.startswith("rg_") else n_even) for k in GRAD_KEYS}
    for l in reversed(range(DEPTH)):
        sm, sp = saved[l]
        dh = _mlp_bwd(dh, sp, p, l, grads)
        if l % 2 == 0:
            dh = _ssdmla_bwd(dh, sm, p, l, l // 2, cos, sin, grads)
        else:
            dh = _rglru_bwd(dh, sm, p, l, l // 2, grads)
    return sq, dh, grads


N_CHIPS = 4
MESH = pl.DeviceIdType.MESH
ANY = pl.BlockSpec(memory_space=pl.ANY)


def _mesh_pos():
    return lax.axis_index("x"), lax.axis_index("y"), lax.axis_index("c")


def _other_chips(x, y):
    return [(1 - x, y), (x, 1 - y), (1 - x, 1 - y)]


def _remote(src, dst, send_sems, recv_sems, k, to):
    return pltpu.make_async_remote_copy(src_ref=src, dst_ref=dst, send_sem=send_sems.at[k], recv_sem=recv_sems.at[k],
                                        device_id=to, device_id_type=MESH)


def _gather_chips(src, name):
    _, half, w = src.shape
    nc = N_CHIPS - 1

    def body(src_ref, out_ref, send_sems, recv_sems, local_sem):
        x, y, c = _mesh_pos()
        sib = (x, y, 1 - c)
        chips = _other_chips(x, y)
        mine = pltpu.make_async_copy(src_ref, out_ref.at[2 * x + y], local_sem)
        mine.start()
        first = [_remote(src_ref.at[c], out_ref.at[2 * x + y, c], send_sems, recv_sems, j, (cx, cy, c))
                 for j, (cx, cy) in enumerate(chips)]
        for cp in first:
            cp.start()
        passed = [_remote(out_ref.at[2 * cx + cy, c], out_ref.at[2 * cx + cy, c], send_sems, recv_sems, nc + j, sib)
                  for j, (cx, cy) in enumerate(chips)]
        for j, (cx, cy) in enumerate(chips):
            _remote(src_ref.at[c], out_ref.at[2 * cx + cy, c], send_sems, recv_sems, j, (cx, cy, c)).wait_recv()
            passed[j].start()
        for j, (cx, cy) in enumerate(chips):
            _remote(src_ref.at[c], out_ref.at[2 * cx + cy, 1 - c], send_sems, recv_sems, nc + j, sib).wait_recv()
        for cp in first + passed:
            cp.wait_send()
        mine.wait()

    return pl.pallas_call(
        body, name=name, in_specs=[ANY], out_specs=ANY,
        out_shape=jax.ShapeDtypeStruct((N_CHIPS, 2, half, w), src.dtype),
        scratch_shapes=[pltpu.SemaphoreType.DMA((2 * nc,)), pltpu.SemaphoreType.DMA((2 * nc,)), pltpu.SemaphoreType.DMA],
    )(src)


def _pair_exchange(g, name):
    n, _, half, w = g.shape

    def body(g_ref, o_ref, send_sems, recv_sems):
        x, y, c = _mesh_pos()
        sib = (x, y, 1 - c)
        cps = [_remote(g_ref.at[k, 1 - c], o_ref.at[k], send_sems, recv_sems, k, sib) for k in range(n)]
        for cp in cps:
            cp.start()
        for cp in cps:
            cp.wait_recv()
        for cp in cps:
            cp.wait_send()

    return pl.pallas_call(
        body, name=name, in_specs=[ANY], out_specs=ANY, out_shape=jax.ShapeDtypeStruct((n, half, w), g.dtype),
        scratch_shapes=[pltpu.SemaphoreType.DMA((n,)), pltpu.SemaphoreType.DMA((n,))],
    )(g)


def _chip_exchange(p, name):
    n, half, w = p.shape
    nc = N_CHIPS - 1

    def body(p_ref, q_ref, send_sems, recv_sems, local_sem):
        x, y, c = _mesh_pos()
        me = 2 * x + y
        chips = _other_chips(x, y)
        mine = pltpu.make_async_copy(p_ref.at[me], q_ref.at[me], local_sem)
        mine.start()
        sends = [_remote(p_ref.at[2 * cx + cy], q_ref.at[me], send_sems, recv_sems, j, (cx, cy, c)) for j, (cx, cy) in enumerate(chips)]
        for cp in sends:
            cp.start()
        for j, (cx, cy) in enumerate(chips):
            _remote(p_ref.at[me], q_ref.at[2 * cx + cy], send_sems, recv_sems, j, (cx, cy, c)).wait_recv()
        for cp in sends:
            cp.wait_send()
        mine.wait()

    return pl.pallas_call(
        body, name=name, in_specs=[ANY], out_specs=ANY, out_shape=jax.ShapeDtypeStruct((n, half, w), p.dtype),
        scratch_shapes=[pltpu.SemaphoreType.DMA((nc,)), pltpu.SemaphoreType.DMA((nc,)), pltpu.SemaphoreType.DMA],
    )(p)


def _pair_share(f, name):
    half, w = f.shape

    def body(f_ref, o_ref, send_sems, recv_sems, local_sem):
        x, y, c = _mesh_pos()
        mine = pltpu.make_async_copy(f_ref, o_ref.at[c], local_sem)
        mine.start()
        cp = _remote(f_ref, o_ref.at[c], send_sems, recv_sems, 0, (x, y, 1 - c))
        cp.start()
        _remote(f_ref, o_ref.at[1 - c], send_sems, recv_sems, 0, (x, y, 1 - c)).wait_recv()
        cp.wait_send()
        mine.wait()

    return pl.pallas_call(
        body, name=name, in_specs=[ANY], out_specs=ANY, out_shape=jax.ShapeDtypeStruct((2, half, w), f.dtype),
        scratch_shapes=[pltpu.SemaphoreType.DMA((1,)), pltpu.SemaphoreType.DMA((1,)), pltpu.SemaphoreType.DMA],
    )(f)


SUM_ROWS = 4096


def _sum_pair(g, ra, c, name):
    n, _, half, w = g.shape
    tr = _tile(half, SUM_ROWS)

    def body(c_ref, g_ref, r_ref, o_ref):
        o_ref[...] = g_ref[0] + r_ref[...]

    return pl.pallas_call(
        body, name=name,
        grid_spec=pltpu.PrefetchScalarGridSpec(
            num_scalar_prefetch=1, grid=(n, half // tr),
            in_specs=[pl.BlockSpec((1, 1, tr, w), lambda s, i, cr: (s, cr[0], i, 0)), pl.BlockSpec((1, tr, w), lambda s, i, cr: (s, i, 0))],
            out_specs=pl.BlockSpec((1, tr, w), lambda s, i, cr: (s, i, 0))),
        out_shape=jax.ShapeDtypeStruct((n, half, w), F32),
        compiler_params=_params(("parallel", "parallel")),
    )(c.reshape(1).astype(jnp.int32), g, ra)


def _sum_chips(q, name):
    n, half, w = q.shape
    tr = _tile(half, SUM_ROWS)

    def body(*refs):
        acc = refs[0][0].astype(F32)
        for r in refs[1:n]:
            acc = acc + r[0].astype(F32)
        refs[n][...] = acc

    return pl.pallas_call(
        body, name=name, grid=(half // tr,),
        in_specs=[pl.BlockSpec((1, tr, w), lambda i, k=k: (k, i, 0)) for k in range(n)],
        out_specs=pl.BlockSpec((tr, w), lambda i: (i, 0)),
        out_shape=jax.ShapeDtypeStruct((half, w), F32),
        compiler_params=_params(("parallel",)),
    )(*([q] * n))


def _adamw(g, w, m, v, name):
    def f(r0, gg, ww, mm, vv):
        m2 = ADAM_B1 * mm + (1.0 - ADAM_B1) * gg
        v2 = ADAM_B2 * vv + (1.0 - ADAM_B2) * jnp.square(gg)
        m_hat = m2 / (1.0 - ADAM_B1 ** ADAM_STEP)
        v_hat = v2 / (1.0 - ADAM_B2 ** ADAM_STEP)
        return -ADAM_LR * (m_hat / (jnp.sqrt(v_hat) + ADAM_EPS) + ADAM_WD * ww), m2, v2

    return _rowwise(name, f, [g, w, m, v], [], [(g.shape[1], F32)] * 3, tr=_tile(g.shape[0], 512))


WEIGHTS = (
    ("meta_tokens", (N_META, D_MODEL), 1), ("mix_pre_g", (DEPTH, D_MODEL), None), ("mix_post_g", (DEPTH, D_MODEL), None),
    ("mlp_pre_g", (DEPTH, D_MODEL), None), ("mlp_post_g", (DEPTH, D_MODEL), None), ("w_up", (DEPTH, D_MODEL, D_FF), 2),
    ("w_down", (DEPTH, D_FF, D_MODEL), 1), ("w_in", (2, D_MODEL, 3248), 2), ("ssd_conv_w", (2, CONV_K, SSD_CONV_CH), 2),
    ("ssd_conv_b", (2, SSD_CONV_CH), None), ("ssd_dt_bias", (2, SSD_HEADS), None), ("ssd_a_log", (2, SSD_HEADS), None),
    ("ssd_d", (2, SSD_HEADS), None), ("ssd_norm_g", (2, SSD_D_INNER), None), ("mla_q_norm_g", (2, MLA_Q_RANK), None),
    ("mla_w_q_up", (2, MLA_Q_RANK, MLA_HEADS * (MLA_NOPE + MLA_ROPE)), 2), ("mla_kv_norm_g", (2, MLA_KV_RANK), None),
    ("mla_w_kv_up", (2, MLA_KV_RANK, MLA_HEADS * (MLA_NOPE + MLA_V)), 2), ("w_out_ab", (2, SSD_D_INNER + MLA_HEADS * MLA_V, D_MODEL), 1),
    ("rg_w_x", (2, D_MODEL, LRU_WIDTH), 2), ("rg_w_y", (2, D_MODEL, LRU_WIDTH), 2), ("rg_conv_w", (2, CONV_K, LRU_WIDTH), 2),
    ("rg_conv_b", (2, LRU_WIDTH), 1), ("rg_w_a", (2, LRU_BLOCKS, LRU_BLOCK, LRU_BLOCK), None), ("rg_b_a", (2, LRU_WIDTH), 1),
    ("rg_w_i", (2, LRU_BLOCKS, LRU_BLOCK, LRU_BLOCK), None), ("rg_b_i", (2, LRU_WIDTH), 1), ("rg_lambda", (2, LRU_WIDTH), 1),
    ("rg_w_out", (2, LRU_WIDTH, D_MODEL), 1),
)
MATMUL_WEIGHTS = ("w_up", "w_down", "w_in", "mla_w_q_up", "mla_w_kv_up", "w_out_ab", "rg_w_x", "rg_w_y", "rg_w_out")
FLAT_QUANTUM = 2 * SUM_ROWS * LANES


def _shard_shape(shape, d):
    return shape[:d] + (shape[d] // N_CHIPS,) + shape[d + 1:]


def _shard_major(full, d):
    s = full.shape
    return jnp.moveaxis(full.reshape(s[:d] + (N_CHIPS, s[d] // N_CHIPS) + s[d + 1:]), d, 0).reshape(N_CHIPS, -1)


def _from_shard_major(a, shape, d):
    ss = _shard_shape(shape, d)
    return jnp.moveaxis(a.reshape((N_CHIPS,) + ss), 0, d).reshape(shape)


def _pad_cols(a, quantum):
    n = a.shape[-1]
    return jnp.pad(a, [(0, 0)] * (a.ndim - 1) + [(0, -n % quantum)])


def _gather_weights(w):
    parts = []
    for name, shape, d in WEIGHTS:
        if d is None:
            continue
        flat = w[name].reshape(-1)
        parts.append(flat.astype(BF16) if name in MATMUL_WEIGHTS else lax.bitcast_convert_type(flat, BF16).reshape(-1))
    flat = _pad_cols(jnp.concatenate(parts), 2 * 16 * LANES)
    got = _gather_chips(flat.reshape(2, -1, LANES), "gather_weights").reshape(N_CHIPS, -1)
    full, off = {}, 0
    for name, shape, d in WEIGHTS:
        if d is None:
            full[name] = w[name]
            continue
        n = int(np.prod(_shard_shape(shape, d)))
        if name in MATMUL_WEIGHTS:
            full[name] = _from_shard_major(got[:, off:off + n], shape, d)
            off += n
        else:
            a = lax.bitcast_convert_type(got[:, off:off + 2 * n].reshape(N_CHIPS, n, 2), F32)
            full[name] = _from_shard_major(a, shape, d)
            off += 2 * n
    return full


def _reduce_grads(g, c):
    sharded = jnp.concatenate([_shard_major(g[name], d) for name, _, d in WEIGHTS if d is not None], axis=1)
    rep = _pad_cols(jnp.concatenate([g[name].reshape(-1) for name, _, d in WEIGHTS if d is None]), N_CHIPS * 2 * 8 * LANES)
    n_sh, n_rep = sharded.shape[1], rep.shape[0] // N_CHIPS
    flat = _pad_cols(jnp.concatenate([sharded, rep.reshape(N_CHIPS, n_rep)], axis=1), FLAT_QUANTUM)
    gbuf = flat.reshape(N_CHIPS, 2, -1, LANES)
    ra = _pair_exchange(gbuf, "grads_pair_exchange")
    p = _sum_pair(gbuf, ra, c, "grads_pair_sum")
    q = _chip_exchange(p, "grads_chip_exchange")
    f = _pair_share(_sum_chips(q, "grads_chip_sum"), "grads_pair_share").reshape(-1)
    rep_all = _gather_chips(f[n_sh:n_sh + n_rep].reshape(2, -1, LANES), "grads_gather_replicated").reshape(-1)
    out, off_s, off_r = {}, 0, 0
    for name, shape, d in WEIGHTS:
        if d is None:
            n = int(np.prod(shape))
            out[name] = rep_all[off_r:off_r + n].reshape(shape)
            off_r += n
        else:
            ss = _shard_shape(shape, d)
            n = int(np.prod(ss))
            out[name] = f[off_s:off_s + n].reshape(ss)
            off_s += n
    return out


def kernel(x, meta_tokens, mix_pre_g, mix_post_g, mlp_pre_g, mlp_post_g, w_up, w_down, w_in, ssd_conv_w, ssd_conv_b, ssd_dt_bias, ssd_a_log, ssd_d, ssd_norm_g, mla_q_norm_g, mla_w_q_up, mla_kv_norm_g, mla_w_kv_up, w_out_ab, rg_w_x, rg_w_y, rg_conv_w, rg_conv_b, rg_w_a, rg_b_a, rg_w_i, rg_b_i, rg_lambda, rg_w_out, loss_target, m_meta_tokens, m_mix_pre_g, m_mix_post_g, m_mlp_pre_g, m_mlp_post_g, m_w_up, m_w_down, m_w_in, m_ssd_conv_w, m_ssd_conv_b, m_ssd_dt_bias, m_ssd_a_log, m_ssd_d, m_ssd_norm_g, m_mla_q_norm_g, m_mla_w_q_up, m_mla_kv_norm_g, m_mla_w_kv_up, m_w_out_ab, m_rg_w_x, m_rg_w_y, m_rg_conv_w, m_rg_conv_b, m_rg_w_a, m_rg_b_a, m_rg_w_i, m_rg_b_i, m_rg_lambda, m_rg_w_out, v_meta_tokens, v_mix_pre_g, v_mix_post_g, v_mlp_pre_g, v_mlp_post_g, v_w_up, v_w_down, v_w_in, v_ssd_conv_w, v_ssd_conv_b, v_ssd_dt_bias, v_ssd_a_log, v_ssd_d, v_ssd_norm_g, v_mla_q_norm_g, v_mla_w_q_up, v_mla_kv_norm_g, v_mla_w_kv_up, v_w_out_ab, v_rg_w_x, v_rg_w_y, v_rg_conv_w, v_rg_conv_b, v_rg_w_a, v_rg_b_a, v_rg_w_i, v_rg_b_i, v_rg_lambda, v_rg_w_out):
    names = [n for n, _, _ in WEIGHTS]
    w = dict(zip(names, (meta_tokens, mix_pre_g, mix_post_g, mlp_pre_g, mlp_post_g, w_up, w_down, w_in, ssd_conv_w, ssd_conv_b, ssd_dt_bias, ssd_a_log, ssd_d, ssd_norm_g, mla_q_norm_g, mla_w_q_up, mla_kv_norm_g, mla_w_kv_up, w_out_ab, rg_w_x, rg_w_y, rg_conv_w, rg_conv_b, rg_w_a, rg_b_a, rg_w_i, rg_b_i, rg_lambda, rg_w_out)))
    m = dict(zip(names, (m_meta_tokens, m_mix_pre_g, m_mix_post_g, m_mlp_pre_g, m_mlp_post_g, m_w_up, m_w_down, m_w_in, m_ssd_conv_w, m_ssd_conv_b, m_ssd_dt_bias, m_ssd_a_log, m_ssd_d, m_ssd_norm_g, m_mla_q_norm_g, m_mla_w_q_up, m_mla_kv_norm_g, m_mla_w_kv_up, m_w_out_ab, m_rg_w_x, m_rg_w_y, m_rg_conv_w, m_rg_conv_b, m_rg_w_a, m_rg_b_a, m_rg_w_i, m_rg_b_i, m_rg_lambda, m_rg_w_out)))
    v = dict(zip(names, (v_meta_tokens, v_mix_pre_g, v_mix_post_g, v_mlp_pre_g, v_mlp_post_g, v_w_up, v_w_down, v_w_in, v_ssd_conv_w, v_ssd_conv_b, v_ssd_dt_bias, v_ssd_a_log, v_ssd_d, v_ssd_norm_g, v_mla_q_norm_g, v_mla_w_q_up, v_mla_kv_norm_g, v_mla_w_kv_up, v_w_out_ab, v_rg_w_x, v_rg_w_y, v_rg_conv_w, v_rg_conv_b, v_rg_w_a, v_rg_b_a, v_rg_w_i, v_rg_b_i, v_rg_lambda, v_rg_w_out)))
    full = _gather_weights(w)
    p = _layout_params({k: a for k, a in full.items() if k != "meta_tokens"})
    sq, dh, grads = _device_step(x[0], full["meta_tokens"], loss_target[0], p)
    local = _natural_grads(grads)
    local["meta_tokens"] = dh[PAD:PAD + N_META]
    loss = lax.psum(0.5 * sq[0, 0] / D_MODEL, ("x", "y", "c"))
    g = _reduce_grads(local, lax.axis_index("c"))
    delta, new_m, new_v = {}, {}, {}
    for name in names:
        shape = g[name].shape
        two_d = (int(np.prod(shape[:-1])), shape[-1])
        res = _adamw(g[name].reshape(two_d), w[name].reshape(two_d), m[name].reshape(two_d), v[name].reshape(two_d), "adamw")
        delta[name], new_m[name], new_v[name] = (r.reshape(shape) for r in res)
    grad_x = dh[PAD + N_META:][None]
    return (loss, grad_x, *[g[n] for n in names], *[delta[n] for n in names], *[new_m[n] for n in names], *[new_v[n] for n in names])
```

```python
import functools

import jax
import jax.numpy as jnp
import numpy as np
from jax import lax
from jax.experimental import pallas as pl
from jax.experimental.pallas import tpu as pltpu

F32 = jnp.float32
BF16 = jnp.bfloat16

D_MODEL = 1024
DEPTH = 4
N_META = 16
CHUNK = 128
PAD = CHUNK - N_META
EPS = 1e-6
SSD_HEADS = 16
SSD_HEAD_DIM = 64
SSD_D_INNER = SSD_HEADS * SSD_HEAD_DIM
SSD_GROUPS = 2
SSD_STATE = 128
SSD_CONV_CH = SSD_D_INNER + 2 * SSD_GROUPS * SSD_STATE
MLA_HEADS = 16
MLA_NOPE = 64
MLA_ROPE = 32
MLA_V = 64
MLA_Q_RANK = 384
MLA_KV_RANK = 256
ROPE_BASE = 10000.0
LRU_WIDTH = 1280
LRU_BLOCKS = 10
LRU_BLOCK = 128
LRU_C = 8.0
D_FF = 4 * D_MODEL
ADAM_LR, ADAM_B1, ADAM_B2, ADAM_EPS, ADAM_WD, ADAM_STEP = 0.001, 0.9, 0.999, 1e-08, 0.01, 10

LANES = 128
VMEM_LIMIT = 56 * 1024 * 1024
HEAD_SLOT = 128
PROJ_Z, PROJ_XBC, PROJ_DT, PROJ_CQ, PROJ_CKV, PROJ_KR = 0, 1024, 2560, 2688, 3072, 3328
PROJ_W = 3456


def _tile(n, cap, mult=8):
    for t in range(min(n, cap), 0, -1):
        if n % t == 0 and t % mult == 0:
            return t
    return n


def _params(sem):
    return pltpu.CompilerParams(dimension_semantics=sem, vmem_limit_bytes=VMEM_LIMIT)


def _full_spec(shape, ngrid):
    nd = len(shape)
    if ngrid == 1:
        return pl.BlockSpec(shape, lambda i: (0,) * nd)
    if ngrid == 2:
        return pl.BlockSpec(shape, lambda i, j: (0,) * nd)
    return pl.BlockSpec(shape, lambda i, j, k: (0,) * nd)


_DIMS = {"nn": (((1,), (0,)), ((), ())), "nt": (((1,), (1,)), ((), ())), "tn": (((0,), (0,)), ((), ()))}


class Gathered:
    def __init__(self, arr, kind, layer):
        self.arr, self.kind, self.layer = arr, kind, layer
        _, _, r, c = arr.shape
        self.shape = (r, N_CHIPS * c) if kind == "col" else (N_CHIPS * r, c)


N_CHIPS = 4


def _mm(a, b, mode, name, out_dtype=F32, add=None, out_chip_major=False):
    if mode == "nn":
        (m, kc), (_, n) = a.shape, b.shape
    elif mode == "nt":
        (m, kc), (n, _) = a.shape, b.shape
    else:
        (kc, m), (_, n) = a.shape, b.shape
    tm = _tile(m, 1024, LANES) if mode == "tn" else _tile(m, 528, 16)
    tn = _tile(n // N_CHIPS if out_chip_major else n, 1280, LANES)
    tk = _tile(kc, 1024 if mode != "tn" else 528, LANES if mode != "tn" else 16)
    nk = kc // tk
    if mode == "tn":
        a_spec = pl.BlockSpec((tk, tm), lambda i, j, k: (k, i))
    else:
        a_spec = pl.BlockSpec((tm, tk), lambda i, j, k: (i, k))
    b_arr = b
    if isinstance(b, Gathered):
        b_arr, layer = b.arr, b.layer
        sr, sc = b.arr.shape[2:]
        br, bc = (tk, tn) if mode == "nn" else (tn, tk)
        assert mode in ("nn", "nt") and sr % br == 0 and sc % bc == 0

        def b_map(i, j, k):
            r, c = (k, j) if mode == "nn" else (j, k)
            if b.kind == "col":
                return ((c * bc) // sc, layer, r, ((c * bc) % sc) // bc)
            return ((r * br) // sr, layer, ((r * br) % sr) // br, c)

        b_spec = pl.BlockSpec((None, None, br, bc), b_map)
    elif mode == "nt":
        b_spec = pl.BlockSpec((tn, tk), lambda i, j, k: (j, k))
    else:
        b_spec = pl.BlockSpec((tk, tn), lambda i, j, k: (k, j))
    dims = _DIMS[mode]
    if out_chip_major:
        ns = n // N_CHIPS
        o_spec = pl.BlockSpec((None, tm, tn), lambda i, j, k: ((j * tn) // ns, i, ((j * tn) % ns) // tn))
        o_shape = jax.ShapeDtypeStruct((N_CHIPS, m, ns), out_dtype)
    else:
        o_spec = pl.BlockSpec((tm, tn), lambda i, j, k: (i, j))
        o_shape = jax.ShapeDtypeStruct((m, n), out_dtype)
    nadd = 0 if add is None else 1

    def body(a_ref, b_ref, *rest):
        o_ref, acc = rest[nadd], rest[nadd + 1:]
        p = lax.dot_general(a_ref[...].astype(BF16), b_ref[...].astype(BF16), dims, preferred_element_type=F32)

        def emit(v):
            o_ref[...] = (v + rest[0][...] if nadd else v).astype(o_ref.dtype)

        if nk == 1:
            emit(p)
        else:
            k = pl.program_id(2)

            @pl.when(k == 0)
            def _():
                acc[0][...] = p

            @pl.when(k > 0)
            def _():
                acc[0][...] += p

            @pl.when(k == nk - 1)
            def _():
                emit(acc[0][...])

    return pl.pallas_call(
        body, name=name, grid=(m // tm, n // tn, nk),
        in_specs=[a_spec, b_spec] + [o_spec] * nadd, out_specs=o_spec,
        out_shape=o_shape,
        scratch_shapes=[pltpu.VMEM((tm, tn), F32)] if nk > 1 else [],
        compiler_params=_params(("parallel", "parallel", "arbitrary")),
    )(a, b_arr, *([add] if nadd else []))


def _rowarg(r):
    return r if isinstance(r, tuple) else (r, r.shape[1], 0)


def _rowspec(r, tr, ncol):
    _, w, cb = r
    if ncol > 1:
        return pl.BlockSpec((tr, w // ncol), lambda j, i: (i, j))
    return pl.BlockSpec((tr, w), lambda j, i: (i, cb))


def _rowwise(name, f, rows, params, outs, tr=None, ncol=1):
    rows = [_rowarg(r) for r in rows]
    t = rows[0][0].shape[0]
    tr = tr or _tile(t, 528)
    nr, npm = len(rows), len(params)

    def body(*refs):
        vals = [r[...] for r in refs[:nr]] + [(p[0] if ncol > 1 else p[...]) for p in refs[nr:nr + npm]]
        res = f(pl.program_id(1) * tr, *vals)
        for o_ref, v in zip(refs[nr + npm:], res):
            o_ref[...] = v.astype(o_ref.dtype)

    def pspec(p):
        if ncol > 1:
            return pl.BlockSpec((1,) + p.shape[1:], lambda j, i, n=p.ndim: (j,) + (0,) * (n - 1))
        return _full_spec(p.shape, 2)

    return pl.pallas_call(
        body, name=name, grid=(ncol, t // tr),
        in_specs=[_rowspec(r, tr, ncol) for r in rows] + [pspec(p) for p in params],
        out_specs=[pl.BlockSpec((tr, w // ncol), lambda j, i: (i, j)) for w, _ in outs],
        out_shape=[jax.ShapeDtypeStruct((t, w), dt) for w, dt in outs],
        compiler_params=_params(("parallel", "parallel")),
    )(*[r[0] for r in rows], *params)


def _rowwise_vjp(name, f, rows, params, cts, tr=None, ncol=1, row_dtypes=None):
    rows = [_rowarg(r) for r in rows]
    cts = [_rowarg(c) for c in cts]
    t = rows[0][0].shape[0]
    tr = tr or _tile(t, 528)
    nr, npm, nc = len(rows), len(params), len(cts)
    row_dtypes = row_dtypes or [F32] * nr

    def body(*refs):
        i = pl.program_id(1)
        vals = [r[...] for r in refs[:nr]] + [(p[0] if ncol > 1 else p[...]) for p in refs[nr:nr + npm]]
        ct = tuple(c[...].astype(F32) for c in refs[nr + npm:nr + npm + nc])
        _, vjp = jax.vjp(lambda *a: tuple(f(i * tr, *a)), *vals)
        g = vjp(ct)
        outs = refs[nr + npm + nc:]
        for o_ref, v in zip(outs[:nr], g[:nr]):
            o_ref[...] = v.astype(o_ref.dtype)
        pg = [(v[None] if ncol > 1 else v) for v in g[nr:]]

        @pl.when(i == 0)
        def _():
            for o_ref, v in zip(outs[nr:], pg):
                o_ref[...] = v

        @pl.when(i > 0)
        def _():
            for o_ref, v in zip(outs[nr:], pg):
                o_ref[...] += v

    def pspec(p):
        if ncol > 1:
            return pl.BlockSpec((1,) + p.shape[1:], lambda j, i, n=p.ndim: (j,) + (0,) * (n - 1))
        return _full_spec(p.shape, 2)

    res = pl.pallas_call(
        body, name=name, grid=(ncol, t // tr),
        in_specs=[_rowspec(r, tr, ncol) for r in rows] + [pspec(p) for p in params] + [_rowspec(c, tr, ncol) for c in cts],
        out_specs=[pl.BlockSpec((tr, w // ncol), lambda j, i: (i, j)) for _, w, _ in rows] + [pspec(p) for p in params],
        out_shape=[jax.ShapeDtypeStruct((t, w), dt) for (_, w, _), dt in zip(rows, row_dtypes)]
        + [jax.ShapeDtypeStruct(p.shape, F32) for p in params],
        compiler_params=_params(("parallel", "arbitrary")),
    )(*[r[0] for r in rows], *params, *[c[0] for c in cts])
    return res[:nr], res[nr:]


def _valid(row0, tr):
    return (row0 + lax.broadcasted_iota(jnp.int32, (tr, 1), 0)) >= PAD


def _rms(x, g):
    return x * lax.rsqrt(jnp.mean(x * x, axis=-1, keepdims=True) + EPS) * g


def _softplus(x):
    return jnp.where(x < -15.0, jnp.exp(x), jnp.maximum(x, 0.0) + jnp.log(1.0 + jnp.exp(-jnp.abs(x))))


def _neg_expm1(z):
    return jnp.where(z > -0.01, -z * (1.0 + z * (0.5 + z * (1.0 / 6.0))), 1.0 - jnp.exp(z))


def _prenorm(h, g, name):
    return _rowwise(name, lambda r0, x, gg: (_rms(x, gg),), [h], [g], [(_rowarg(h)[1], BF16)])[0]


def _add_postnorm(h, ms, g, name):
    def f(r0, x, *rest):
        return (x + _rms(functools.reduce(jnp.add, rest[:-1]), rest[-1]),)

    return _rowwise(name, f, [h] + list(ms), [g], [(h.shape[1], F32)])[0]


def _postnorm_bwd(m, g, dh, name):
    (dm,), (dg,) = _rowwise_vjp(name, lambda r0, mm, gg: (_rms(mm, gg),), [m], [g], [dh])
    return dm, dg


def _prenorm_bwd_add(h, g, dhns, dh, name):
    t, w = h.shape
    tr = _tile(t, 528)
    nd = len(dhns)

    def body(h_ref, g_ref, *refs):
        dh_ref, o_ref, dg_ref = refs[nd:]
        i = pl.program_id(0)
        _, vjp = jax.vjp(_rms, h_ref[...], g_ref[...])
        dhn = refs[0][...].astype(F32)
        for r in refs[1:nd]:
            dhn = dhn + r[...].astype(F32)
        dx, dg = vjp(dhn)
        o_ref[...] = dh_ref[...] + dx

        @pl.when(i == 0)
        def _():
            dg_ref[...] = dg

        @pl.when(i > 0)
        def _():
            dg_ref[...] += dg

    row = pl.BlockSpec((tr, w), lambda i: (i, 0))
    return pl.pallas_call(
        body, name=name, grid=(t // tr,), in_specs=[row, _full_spec(g.shape, 1)] + [row] * (nd + 1),
        out_specs=[row, _full_spec(g.shape, 1)],
        out_shape=[jax.ShapeDtypeStruct((t, w), F32), jax.ShapeDtypeStruct(g.shape, F32)],
        compiler_params=_params(("arbitrary",)),
    )(h, g, *dhns, dh)


def _relu2(a, name):
    return _rowwise(name, lambda r0, x: (jnp.square(jnp.maximum(x, 0.0)),), [a], [], [(a.shape[1], BF16)], tr=_tile(a.shape[0], 264))[0]


def _relu2_bwd(a, du, name):
    return _rowwise(name, lambda r0, x, d: (2.0 * jnp.maximum(x, 0.0) * d,), [a, du], [], [(a.shape[1], BF16)],
                    tr=_tile(a.shape[0], 264))[0]


def _loss_and_grad(h, target, name):
    t, w = h.shape
    nb = t // CHUNK

    def body(h_ref, t_ref, s_ref, dh_ref):
        i = pl.program_id(0)

        @pl.when(i == 0)
        def _():
            s_ref[...] = jnp.zeros_like(s_ref)
            dh_ref[...] = jnp.zeros_like(dh_ref)

        @pl.when(i > 0)
        def _():
            err = h_ref[...] - t_ref[...]
            s_ref[...] += jnp.sum(err * err)
            dh_ref[...] = err * (1.0 / w)

    return pl.pallas_call(
        body, name=name, grid=(nb,),
        in_specs=[pl.BlockSpec((CHUNK, w), lambda i: (i, 0)), pl.BlockSpec((CHUNK, w), lambda i: (jnp.maximum(i - 1, 0), 0))],
        out_specs=[_full_spec((1, LANES), 1), pl.BlockSpec((CHUNK, w), lambda i: (i, 0))],
        out_shape=[jax.ShapeDtypeStruct((1, LANES), F32), jax.ShapeDtypeStruct((t, w), F32)],
        compiler_params=_params(("arbitrary",)),
    )(h, target)


def _mlp_fwd(h, p, l):
    hn = _prenorm(h, p["mlp_pre_g"][l], "mlp_prenorm")
    a = _mm(hn, p["w_up"][l], "nn", "mlp_up")
    u = _relu2(a, "mlp_relu2")
    d = _mm(u, p["w_down"][l], "nn", "mlp_down")
    h2 = _add_postnorm(h, [d], p["mlp_post_g"][l], "mlp_postnorm")
    return h2, (h, hn, a, u, d)


def _mlp_bwd(dh, saved, p, l, grads):
    h, hn, a, u, d = saved
    dd, grads["mlp_post_g"][l] = _postnorm_bwd(d, p["mlp_post_g"][l], dh, "mlp_postnorm_bwd")
    grads["w_down"][l] = _mm(u, dd, "tn", "mlp_down_dw")
    du = _mm(dd, p["w_down"][l], "nt", "mlp_down_dx")
    da = _relu2_bwd(a, du, "mlp_relu2_bwd")
    grads["w_up"][l] = _mm(hn, da, "tn", "mlp_up_dw", out_chip_major=True)
    dhn = _mm(da, p["w_up"][l], "nt", "mlp_up_dx")
    dh, grads["mlp_pre_g"][l] = _prenorm_bwd_add(h, p["mlp_pre_g"][l], [dhn], dh, "mlp_prenorm_bwd")
    return dh


def _dot(a, b, mode):
    return lax.dot_general(a.astype(BF16), b.astype(BF16), _DIMS[mode], preferred_element_type=F32)


@jax.custom_vjp
def _bnn(a, b):
    return _dot(a, b, "nn")


_bnn.defvjp(lambda a, b: (_dot(a, b, "nn"), (a, b)), lambda r, ct: (_dot(ct, r[1], "nt"), _dot(r[0], ct, "tn")))


@jax.custom_vjp
def _bnt(a, b):
    return _dot(a, b, "nt")


_bnt.defvjp(lambda a, b: (_dot(a, b, "nt"), (a, b)), lambda r, ct: (_dot(ct, r[1], "nn"), _dot(ct, r[0], "tn")))


@jax.custom_vjp
def _btn(a, b):
    return _dot(a, b, "tn")


_btn.defvjp(lambda a, b: (_dot(a, b, "tn"), (a, b)), lambda r, ct: (_dot(r[1], ct, "nt"), _dot(r[0], ct, "nn")))


CONV_K = 4
HALO = 8


def _conv_fwd(x, w, b, name, cw, c0=0):
    t, c = x.shape[0], w.shape[1]
    tr = _tile(t, 528)
    hb = tr // HALO

    def body(x_ref, halo_ref, w_ref, b_ref, o_ref, ext):
        i = pl.program_id(1)
        ext[pl.ds(0, HALO), :] = jnp.where(i > 0, halo_ref[...], 0.0)
        ext[pl.ds(HALO, tr), :] = x_ref[...]
        acc = jnp.broadcast_to(b_ref[...], (tr, cw))
        for k in range(CONV_K):
            acc = acc + w_ref[pl.ds(k, 1), :] * ext[pl.ds(HALO - (CONV_K - 1) + k, tr), :]
        o_ref[...] = acc

    return pl.pallas_call(
        body, name=name, grid=(c // cw, t // tr),
        in_specs=[pl.BlockSpec((tr, cw), lambda j, i: (i, c0 + j)),
                  pl.BlockSpec((HALO, cw), lambda j, i: (jnp.maximum(i * hb - 1, 0), c0 + j)),
                  pl.BlockSpec((CONV_K, cw), lambda j, i: (0, j)), pl.BlockSpec((1, cw), lambda j, i: (0, j))],
        out_specs=pl.BlockSpec((tr, cw), lambda j, i: (i, j)),
        out_shape=jax.ShapeDtypeStruct((t, c), F32),
        scratch_shapes=[pltpu.VMEM((tr + HALO, cw), F32)],
        compiler_params=_params(("parallel", "parallel")),
    )(x, x, w, b)


def _conv_bwd(x, w, dy, name, cw, c0=0):
    t, c = x.shape[0], w.shape[1]
    tr = _tile(t, 528)
    hb = tr // HALO
    nb = t // tr

    def body(x_ref, xh_ref, w_ref, dy_ref, dyh_ref, dx_ref, dw_ref, db_ref, xe, de):
        c = cw
        i = pl.program_id(1)
        xe[pl.ds(0, HALO), :] = jnp.where(i > 0, xh_ref[...], 0.0)
        xe[pl.ds(HALO, tr), :] = x_ref[...]
        de[pl.ds(0, tr), :] = dy_ref[...]
        de[pl.ds(tr, HALO), :] = jnp.where(i < nb - 1, dyh_ref[...], 0.0)
        dy = dy_ref[...]
        acc = jnp.zeros((tr, c), F32)
        dw = jnp.zeros((CONV_K, c), F32)
        rows = lax.broadcasted_iota(jnp.int32, (CONV_K, 1), 0)
        for k in range(CONV_K):
            acc = acc + w_ref[pl.ds(k, 1), :] * de[pl.ds(CONV_K - 1 - k, tr), :]
            dwk = jnp.sum(dy * xe[pl.ds(HALO - (CONV_K - 1) + k, tr), :], axis=0, keepdims=True)
            dw = dw + jnp.where(rows == k, dwk, 0.0)
        dx_ref[...] = jnp.where(_valid(i * tr, tr), acc, 0.0)
        db = jnp.sum(dy, axis=0, keepdims=True)

        @pl.when(i == 0)
        def _():
            dw_ref[...] = dw
            db_ref[...] = db

        @pl.when(i > 0)
        def _():
            dw_ref[...] += dw
            db_ref[...] += db

    row = pl.BlockSpec((tr, cw), lambda j, i: (i, j))
    return pl.pallas_call(
        body, name=name, grid=(c // cw, nb),
        in_specs=[pl.BlockSpec((tr, cw), lambda j, i: (i, c0 + j)),
                  pl.BlockSpec((HALO, cw), lambda j, i: (jnp.maximum(i * hb - 1, 0), c0 + j)),
                  pl.BlockSpec((CONV_K, cw), lambda j, i: (0, j)),
                  row, pl.BlockSpec((HALO, cw), lambda j, i: (jnp.minimum((i + 1) * hb, t // HALO - 1), j))],
        out_specs=[row, pl.BlockSpec((CONV_K, cw), lambda j, i: (0, j)), pl.BlockSpec((1, cw), lambda j, i: (0, j))],
        out_shape=[jax.ShapeDtypeStruct((t, c), F32), jax.ShapeDtypeStruct((CONV_K, c), F32), jax.ShapeDtypeStruct((1, c), F32)],
        scratch_shapes=[pltpu.VMEM((tr + HALO, cw), F32), pltpu.VMEM((tr + HALO, cw), F32)],
        compiler_params=_params(("parallel", "arbitrary")),
    )(x, x, w, dy, dy)


SUB = 8


def _lru_scan(a, u, name):
    t, c = a.shape
    tr = _tile(t, 528)

    def body(a_ref, u_ref, o_ref, carry):
        @pl.when(pl.program_id(0) == 0)
        def _():
            carry[...] = jnp.zeros_like(carry)

        rows = lax.broadcasted_iota(jnp.int32, (SUB, 1), 0)

        def step(k, cin):
            r = pl.multiple_of(k * SUB, SUB)
            av, uv = a_ref[pl.ds(r, SUB), :], u_ref[pl.ds(r, SUB), :]
            for d in (1, 2, 4):
                m = rows >= d
                uv = uv + av * jnp.where(m, pltpu.roll(uv, d, 0), 0.0)
                av = av * jnp.where(m, pltpu.roll(av, d, 0), 1.0)
            hv = uv + av * cin
            o_ref[pl.ds(r, SUB), :] = hv
            return jnp.broadcast_to(hv[SUB - 1:SUB, :], (SUB, c))

        carry[...] = lax.fori_loop(0, tr // SUB, step, carry[...])

    row = pl.BlockSpec((tr, c), lambda i: (i, 0))
    return pl.pallas_call(
        body, name=name, grid=(t // tr,), in_specs=[row, row], out_specs=row,
        out_shape=jax.ShapeDtypeStruct((t, c), F32), scratch_shapes=[pltpu.VMEM((SUB, c), F32)],
        compiler_params=_params(("arbitrary",)),
    )(a, u)


def _lru_scan_bwd(a, hs, dy, name):
    t, c = a.shape
    tr = _tile(t, 528)
    nb, nt = t // tr, tr // SUB

    def body(a_ref, h_ref, hh_ref, dy_ref, du_ref, da_ref, gcar, acar):
        i = pl.program_id(0)

        @pl.when(i == 0)
        def _():
            gcar[...] = jnp.zeros_like(gcar)
            acar[...] = jnp.zeros_like(acar)

        rows = lax.broadcasted_iota(jnp.int32, (SUB, 1), 0)
        hhalo = jnp.where(i < nb - 1, hh_ref[...], 0.0)

        def step(kk, car):
            gin, a_next_first = car
            k = nt - 1 - kk
            r = pl.multiple_of(k * SUB, SUB)
            av, hv, dv = a_ref[pl.ds(r, SUB), :], h_ref[pl.ds(r, SUB), :], dy_ref[pl.ds(r, SUB), :]
            rp = pl.multiple_of(jnp.maximum(k - 1, 0) * SUB, SUB)
            hp = jnp.where(k > 0, h_ref[pl.ds(rp, SUB), :], hhalo)
            cv = jnp.where(rows < SUB - 1, pltpu.roll(av, SUB - 1, 0), a_next_first)
            gv = dv
            for d in (1, 2, 4):
                m = rows < SUB - d
                gv = gv + cv * jnp.where(m, pltpu.roll(gv, SUB - d, 0), 0.0)
                cv = cv * jnp.where(m, pltpu.roll(cv, SUB - d, 0), 1.0)
            gv = gv + cv * gin
            hprev = jnp.where(rows >= 1, pltpu.roll(hv, 1, 0), jnp.broadcast_to(hp[SUB - 1:SUB, :], (SUB, c)))
            du_ref[pl.ds(r, SUB), :] = gv
            da_ref[pl.ds(r, SUB), :] = gv * hprev
            return jnp.broadcast_to(gv[0:1, :], (SUB, c)), jnp.broadcast_to(av[0:1, :], (SUB, c))

        g, af = lax.fori_loop(0, nt, step, (gcar[...], acar[...]))
        gcar[...] = g
        acar[...] = af

    hb = tr // SUB
    row = pl.BlockSpec((tr, c), lambda i: (nb - 1 - i, 0))
    halo = pl.BlockSpec((SUB, c), lambda i: (jnp.maximum((nb - 1 - i) * hb - 1, 0), 0))
    return pl.pallas_call(
        body, name=name, grid=(nb,), in_specs=[row, row, halo, row], out_specs=[row, row],
        out_shape=[jax.ShapeDtypeStruct((t, c), F32)] * 2,
        scratch_shapes=[pltpu.VMEM((SUB, c), F32), pltpu.VMEM((SUB, c), F32)],
        compiler_params=_params(("arbitrary",)),
    )(a, hs, hs, dy)


def _lru_gates(row0, xr, wa, ba, wi, bi, lam):
    r = jax.nn.sigmoid(_bnn(xr, wa) + ba)
    i = jax.nn.sigmoid(_bnn(xr, wi) + bi)
    log_a = -LRU_C * r * _softplus(-lam)
    u = jnp.sqrt(_neg_expm1(2.0 * log_a)) * (i * xr)
    return jnp.exp(log_a), jnp.where(_valid(row0, xr.shape[0]), u, 0.0)


def _lru_gate_out(row0, hs, yw):
    return (hs * jax.nn.gelu(yw),)


def _rglru_fwd(h, p, l, o):
    hn = _prenorm(h, p["mix_pre_g"][l], "rg_prenorm")
    xw = _mm(hn, p["rg_w_x"][o], "nn", "rg_in_x")
    yw = _mm(hn, p["rg_w_y"][o], "nn", "rg_in_y")
    xr = _conv_fwd(xw, p["rg_conv_w"][o], p["rg_conv_b"][o], "rg_conv", cw=LRU_WIDTH // 2)
    gp = [p["rg_w_a"][o], p["rg_b_a"][o], p["rg_w_i"][o], p["rg_b_i"][o], p["rg_lambda"][o]]
    a, u = _rowwise("rg_gates", _lru_gates, [xr], gp, [(LRU_WIDTH, F32)] * 2, ncol=LRU_BLOCKS)
    hs = _lru_scan(a, u, "rg_scan")
    hg = _rowwise("rg_gate_out", _lru_gate_out, [hs, yw], [], [(LRU_WIDTH, BF16)])[0]
    m = _mm(hg, p["rg_w_out"][o], "nn", "rg_out")
    h2 = _add_postnorm(h, [m], p["mix_post_g"][l], "rg_postnorm")
    return h2, (h, hn, xw, yw, xr, a, hs, hg, m)


def _rglru_bwd(dh, saved, p, l, o, grads):
    h, hn, xw, yw, xr, a, hs, hg, m = saved
    dm, grads["mix_post_g"][l] = _postnorm_bwd(m, p["mix_post_g"][l], dh, "rg_postnorm_bwd")
    grads["rg_w_out"][o] = _mm(hg, dm, "tn", "rg_out_dw")
    dhg = _mm(dm, p["rg_w_out"][o], "nt", "rg_out_dx")
    (dhs, dyw), _ = _rowwise_vjp("rg_gate_out_bwd", _lru_gate_out, [hs, yw], [], [dhg])
    du, da = _lru_scan_bwd(a, hs, dhs, "rg_scan_bwd")
    gp = [p["rg_w_a"][o], p["rg_b_a"][o], p["rg_w_i"][o], p["rg_b_i"][o], p["rg_lambda"][o]]
    (dxr,), gg = _rowwise_vjp("rg_gates_bwd", _lru_gates, [xr], gp, [da, du], ncol=LRU_BLOCKS)
    grads["rg_w_a"][o], grads["rg_b_a"][o], grads["rg_w_i"][o], grads["rg_b_i"][o], grads["rg_lambda"][o] = gg
    dxw, grads["rg_conv_w"][o], grads["rg_conv_b"][o] = _conv_bwd(xw, p["rg_conv_w"][o], dxr, "rg_conv_bwd", cw=LRU_WIDTH // 2)
    grads["rg_w_x"][o] = _mm(hn, dxw, "tn", "rg_in_x_dw")
    grads["rg_w_y"][o] = _mm(hn, dyw, "tn", "rg_in_y_dw")
    dhx = _mm(dxw, p["rg_w_x"][o], "nt", "rg_in_x_dx")
    dhy = _mm(dyw, p["rg_w_y"][o], "nt", "rg_in_y_dx")
    dh, grads["mix_pre_g"][l] = _prenorm_bwd_add(h, p["mix_pre_g"][l], [dhx, dhy], dh, "rg_prenorm_bwd")
    return dh


SSD_GW = SSD_D_INNER // SSD_GROUPS
SSD_GH = SSD_HEADS // SSD_GROUPS
XACT_B = SSD_D_INNER // SSD_STATE
XACT_C = XACT_B + SSD_GROUPS


def _hp(a, b, dims=_DIMS["nn"]):
    return lax.dot_general(a, b, dims, precision=lax.Precision.HIGHEST, preferred_element_type=F32)


def _ssd_chunk(xs, bm, cm, dt, da, ht, g):
    l = CHUNK
    ri = lax.broadcasted_iota(jnp.int32, (l, l), 0)
    ci = lax.broadcasted_iota(jnp.int32, (l, l), 1)
    causal = ri >= ci
    tri = causal.astype(F32)
    hr = lax.broadcasted_iota(jnp.int32, (LANES, SSD_GW), 0)
    hc = lax.broadcasted_iota(jnp.int32, (LANES, SSD_GW), 1)
    expand = (hr == g * SSD_GH + hc // SSD_HEAD_DIM).astype(F32)
    acs = _hp(tri, da)
    acs_t = _hp(da, tri, (((0,), (1,)), ((), ())))
    acs_e = _hp(acs, expand)
    x = xs * _hp(dt, expand)
    gmat = _bnt(cm, bm)
    lane = lax.broadcasted_iota(jnp.int32, (1, LANES), 1)
    sub = lax.broadcasted_iota(jnp.int32, (LANES, 1), 0)
    colhead = lax.broadcasted_iota(jnp.int32, (1, SSD_GW), 1) // SSD_HEAD_DIM
    y = _bnn(cm, ht) * jnp.exp(acs_e)
    for k in range(SSD_GH):
        hh = g * SSD_GH + k
        col = jnp.sum(jnp.where(lane == hh, acs, 0.0), axis=1, keepdims=True)
        row = jnp.sum(jnp.where(sub == hh, acs_t, 0.0), axis=0, keepdims=True)
        decay = jnp.exp(jnp.where(causal, col - row, -1e30))
        y = y + _bnn(gmat * decay, jnp.where(colhead == k, x, 0.0))
    last = lax.broadcasted_iota(jnp.int32, (l, 1), 0) == l - 1
    a_last = jnp.sum(jnp.where(last, acs_e, 0.0), axis=0, keepdims=True)
    st = _btn(bm, x * jnp.exp(a_last - acs_e))
    return y, ht * jnp.exp(a_last) + st


def _ssd_specs(nc, rev):
    def cc(c):
        return nc - 1 - c if rev else c

    return [pl.BlockSpec((CHUNK, SSD_GW), lambda c, g: (cc(c), g)),
            pl.BlockSpec((CHUNK, SSD_STATE), lambda c, g: (cc(c), XACT_B + g)),
            pl.BlockSpec((CHUNK, SSD_STATE), lambda c, g: (cc(c), XACT_C + g)),
            pl.BlockSpec((CHUNK, LANES), lambda c, g: (cc(c), 0)),
            pl.BlockSpec((CHUNK, LANES), lambda c, g: (cc(c), 0))]


def _ssd_scan(xact, dt, da, name):
    t = xact.shape[0]
    nc = t // CHUNK

    def body(xs_ref, b_ref, c_ref, dt_ref, da_ref, y_ref, hs_ref, state):
        c, g = pl.program_id(0), pl.program_id(1)

        @pl.when(c == 0)
        def _():
            state[g] = jnp.zeros((SSD_STATE, SSD_GW), F32)

        ht = state[g]
        hs_ref[0] = ht
        y, ht2 = _ssd_chunk(xs_ref[...], b_ref[...], c_ref[...], dt_ref[...], da_ref[...], ht, g)
        y_ref[...] = y
        state[g] = ht2

    return pl.pallas_call(
        body, name=name, grid=(nc, SSD_GROUPS), in_specs=_ssd_specs(nc, False),
        out_specs=[pl.BlockSpec((CHUNK, SSD_GW), lambda c, g: (c, g)),
                   pl.BlockSpec((1, SSD_STATE, SSD_GW), lambda c, g: (c * SSD_GROUPS + g, 0, 0))],
        out_shape=[jax.ShapeDtypeStruct((t, SSD_D_INNER), F32), jax.ShapeDtypeStruct((nc * SSD_GROUPS, SSD_STATE, SSD_GW), F32)],
        scratch_shapes=[pltpu.VMEM((SSD_GROUPS, SSD_STATE, SSD_GW), F32)],
        compiler_params=_params(("arbitrary", "arbitrary")),
    )(xact, xact, xact, dt, da)


def _ssd_scan_bwd(xact, dt, da, hsave, dy, dxskip, name):
    t = xact.shape[0]
    nc = t // CHUNK

    def body(xs_ref, b_ref, c_ref, dt_ref, da_ref, hs_ref, dy_ref, sk_ref, dxs_ref, db_ref, dc_ref, ddt_ref, dda_ref, dstate):
        c, g = pl.program_id(0), pl.program_id(1)

        @pl.when(c == 0)
        def _():
            dstate[g] = jnp.zeros((SSD_STATE, SSD_GW), F32)

        _, vjp = jax.vjp(lambda *a: _ssd_chunk(*a, g), xs_ref[...], b_ref[...], c_ref[...], dt_ref[...], da_ref[...], hs_ref[0])
        dxs, dbm, dcm, ddt, dda, dht = vjp((dy_ref[...], dstate[g]))
        dxs_ref[...] = dxs + sk_ref[...]
        db_ref[...] = dbm
        dc_ref[...] = dcm
        dstate[g] = dht

        @pl.when(g == 0)
        def _():
            ddt_ref[...] = ddt
            dda_ref[...] = dda

        @pl.when(g > 0)
        def _():
            ddt_ref[...] += ddt
            dda_ref[...] += dda

    grp = pl.BlockSpec((CHUNK, SSD_GW), lambda c, g: (nc - 1 - c, g))
    st = pl.BlockSpec((CHUNK, SSD_STATE), lambda c, g: (nc - 1 - c, g))
    hd = pl.BlockSpec((CHUNK, LANES), lambda c, g: (nc - 1 - c, 0))
    return pl.pallas_call(
        body, name=name, grid=(nc, SSD_GROUPS),
        in_specs=_ssd_specs(nc, True) + [pl.BlockSpec((1, SSD_STATE, SSD_GW), lambda c, g: ((nc - 1 - c) * SSD_GROUPS + g, 0, 0)), grp, grp],
        out_specs=[grp, st, st, hd, hd],
        out_shape=[jax.ShapeDtypeStruct((t, SSD_D_INNER), F32), jax.ShapeDtypeStruct((t, SSD_GROUPS * SSD_STATE), F32),
                   jax.ShapeDtypeStruct((t, SSD_GROUPS * SSD_STATE), F32), jax.ShapeDtypeStruct((t, LANES), F32),
                   jax.ShapeDtypeStruct((t, LANES), F32)],
        scratch_shapes=[pltpu.VMEM((SSD_GROUPS, SSD_STATE, SSD_GW), F32)],
        compiler_params=_params(("arbitrary", "arbitrary")),
    )(xact, xact, xact, dt, da, hsave, dy, dxskip)


def _ssd_act(row0, xc):
    return (jnp.where(_valid(row0, xc.shape[0]), jax.nn.silu(xc), 0.0),)


def _ssd_dt(row0, dtraw, dt_bias, a_log):
    dt = jnp.where(_valid(row0, dtraw.shape[0]), _softplus(dtraw + dt_bias), 0.0)
    return dt, dt * -jnp.exp(a_log)


def _ssd_post(row0, y, xs, z, d_skip, norm_g):
    hr = lax.broadcasted_iota(jnp.int32, (LANES, SSD_D_INNER), 0)
    hc = lax.broadcasted_iota(jnp.int32, (LANES, SSD_D_INNER), 1)
    expand = (hr == hc // SSD_HEAD_DIM).astype(F32)
    d_e = jnp.sum(_hp(jnp.broadcast_to(d_skip, (SUB, LANES)), expand), axis=0, keepdims=True) * (1.0 / SUB)
    return (_rms((y + xs * d_e) * jax.nn.silu(z), norm_g),)


ROPE_LO, ROPE_MID, ROPE_HI = MLA_NOPE, MLA_NOPE + MLA_ROPE // 2, MLA_NOPE + MLA_ROPE
ATT_SCALE = (MLA_NOPE + MLA_ROPE) ** -0.5


def _swap_halves(x):
    lane = lax.broadcasted_iota(jnp.int32, (1, LANES), 1)
    sw = jnp.where(lane < ROPE_MID, pltpu.roll(x, LANES - MLA_ROPE // 2, 1), pltpu.roll(x, MLA_ROPE // 2, 1))
    return jnp.where((lane >= ROPE_LO) & (lane < ROPE_HI), sw, 0.0)


def _rope(x, cos, sin):
    return x * cos + _swap_halves(x) * sin


def _rope_t(dy, cos, sin):
    return dy * cos + _swap_halves(dy * sin)


def _att_mask(i, j, blk):
    rowid = i * blk + lax.broadcasted_iota(jnp.int32, (blk, 1), 0)
    colid = j * blk + lax.broadcasted_iota(jnp.int32, (1, blk), 1)
    return (colid <= rowid) & (colid >= PAD)


def _per_head(name, f, heads, shared, outs):
    t = heads[0].shape[0]
    tr = _tile(t, 528, 16)
    nh, ns = len(heads), len(shared)

    def body(*refs):
        res = f(*[r[...] for r in refs[:nh + ns]])
        for o_ref, v in zip(refs[nh + ns:], res):
            o_ref[...] = v.astype(o_ref.dtype)

    slot = pl.BlockSpec((tr, LANES), lambda h, i: (i, h))
    return pl.pallas_call(
        body, name=name, grid=(MLA_HEADS, t // tr),
        in_specs=[slot] * nh + [pl.BlockSpec((tr, LANES), lambda h, i: (i, 0))] * ns, out_specs=[slot] * len(outs),
        out_shape=[jax.ShapeDtypeStruct((t, MLA_HEADS * LANES), dt) for dt in outs],
        compiler_params=_params(("parallel", "parallel")),
    )(*heads, *shared)


def _key_slots(kv, kr):
    lane = lax.broadcasted_iota(jnp.int32, (1, LANES), 1)
    return jnp.where(lane < MLA_NOPE, kv, kr), kv


def _attn_fwd(qr, km, vb, name):
    t = qr.shape[0]
    blk = _tile(t, 384, LANES)
    nq = t // blk

    def body(q_ref, k_ref, v_ref, o_ref):
        i = pl.program_id(1)
        lane = lax.broadcasted_iota(jnp.int32, (1, LANES), 1)
        qb = q_ref[...]

        def step(j, car, masked):
            m, l, acc = car
            rows = pl.ds(pl.multiple_of(j * blk, blk), blk)
            s = lax.dot_general(qb, k_ref[rows, :], _DIMS["nt"], preferred_element_type=F32) * ATT_SCALE
            if masked:
                s = jnp.where(_att_mask(i, j, blk), s, -1e30)
            m2 = jnp.maximum(m, jnp.max(s, axis=1, keepdims=True))
            al = jnp.exp(m - m2)
            pm = jnp.exp(s - m2)
            l2 = al * l + jnp.sum(pm, axis=1, keepdims=True)
            acc2 = al * acc + lax.dot_general(pm.astype(BF16), v_ref[rows, :], _DIMS["nn"], preferred_element_type=F32)
            return m2, l2, acc2

        car = (jnp.full((blk, 1), -1e30, F32), jnp.zeros((blk, 1), F32), jnp.zeros((blk, LANES), F32))
        car = lax.fori_loop(0, jnp.where(i > 0, 2, 1), lambda k, c: step(k * i, c, True), car)
        m, l, acc = lax.fori_loop(1, i, lambda j, c: step(j, c, False), car)
        out = jnp.where(lane >= MLA_NOPE, acc / l, m + jnp.log(l))
        o_ref[...] = jnp.where(_valid(i * blk, blk), out, 0.0)

    seq_h = pl.BlockSpec((t, LANES), lambda h, i: (0, h))
    return pl.pallas_call(
        body, name=name, grid=(MLA_HEADS, nq),
        in_specs=[pl.BlockSpec((blk, LANES), lambda h, i: (i, h)), seq_h, seq_h],
        out_specs=pl.BlockSpec((blk, LANES), lambda h, i: (i, h)),
        out_shape=jax.ShapeDtypeStruct((t, MLA_HEADS * LANES), F32),
        compiler_params=_params(("parallel", "parallel")),
    )(qr, km, vb)


def _attn_bwd(qr, km, vb, o, do, name):
    t = qr.shape[0]
    blk = _tile(t, 384, LANES)
    nq = t // blk

    def body(q_ref, o_ref, do_ref, k_ref, v_ref, dq_ref, dkv_ref, dkr_ref):
        h, j = pl.program_id(0), pl.program_id(1)
        lane = lax.broadcasted_iota(jnp.int32, (1, LANES), 1)

        @pl.when(j == 0)
        def _():
            dq_ref[...] = jnp.zeros_like(dq_ref)

        @pl.when((h == 0) & (j == 0))
        def _():
            dkr_ref[...] = jnp.zeros_like(dkr_ref)

        kmat, vmat = k_ref[...], v_ref[...]

        def step(ii, car, masked):
            dk, dv = car
            i = j + ii
            rows = pl.ds(pl.multiple_of(i * blk, blk), blk)
            qb = q_ref[rows, :]
            ob, dob = o_ref[rows, :], do_ref[rows, :]
            delta = jnp.sum(dob * ob, axis=1, keepdims=True)
            s = lax.dot_general(qb, kmat, _DIMS["nt"], preferred_element_type=F32) * ATT_SCALE
            if masked:
                s = jnp.where(_att_mask(i, j, blk), s, -1e30)
            pm = jnp.exp(s - ob[:, 0:1])
            dobb = dob.astype(BF16)
            dv = dv + lax.dot_general(pm.astype(BF16), dobb, _DIMS["tn"], preferred_element_type=F32)
            dp = lax.dot_general(dobb, vmat, _DIMS["nt"], preferred_element_type=F32)
            ds = (pm * (dp - delta) * ATT_SCALE).astype(BF16)
            dq_ref[rows, :] += lax.dot_general(ds, kmat, _DIMS["nn"], preferred_element_type=F32)
            dk = dk + lax.dot_general(ds, qb, _DIMS["tn"], preferred_element_type=F32)
            return dk, dv

        zero = jnp.zeros((blk, LANES), F32)
        n_masked = jnp.where(j == 0, nq, 1)
        car = lax.fori_loop(0, n_masked, lambda ii, c: step(ii, c, True), (zero, zero))
        dk, dv = lax.fori_loop(n_masked, nq - j, lambda ii, c: step(ii, c, False), car)
        dkv_ref[...] = jnp.where(lane < MLA_NOPE, dk, dv)
        rows = pl.ds(pl.multiple_of(j * blk, blk), blk)
        dkr_ref[rows, :] += jnp.where(lane >= MLA_NOPE, dk, 0.0)

    seq_h = pl.BlockSpec((t, LANES), lambda h, j: (0, h))
    blk_h = pl.BlockSpec((blk, LANES), lambda h, j: (j, h))
    return pl.pallas_call(
        body, name=name, grid=(MLA_HEADS, nq),
        in_specs=[seq_h, seq_h, seq_h, blk_h, blk_h],
        out_specs=[seq_h, blk_h, pl.BlockSpec((t, LANES), lambda h, j: (0, 0))],
        out_shape=[jax.ShapeDtypeStruct((t, MLA_HEADS * LANES), F32), jax.ShapeDtypeStruct((t, MLA_HEADS * LANES), F32),
                   jax.ShapeDtypeStruct((t, LANES), F32)],
        compiler_params=_params(("arbitrary", "arbitrary")),
    )(qr, o, do, km, vb)


def _rms_rows(row0, x, g):
    return (_rms(x, g),)


def _ssdmla_fwd(h, p, l, e, cos, sin):
    hn = _prenorm(h, p["mix_pre_g"][l], "sm_prenorm")
    proj = _mm(hn, p["w_in"][e], "nn", "sm_in")
    xc = _conv_fwd(proj, p["ssd_conv_w"][e], p["ssd_conv_b"][e], "ssd_conv", cw=SSD_GW, c0=PROJ_XBC // SSD_GW)
    xact = _rowwise("ssd_act", _ssd_act, [xc], [], [(SSD_CONV_CH, F32)])[0]
    dt, da = _rowwise("ssd_dt", _ssd_dt, [(proj, LANES, PROJ_DT // LANES)], [p["ssd_dt_bias"][e], p["ssd_a_log"][e]],
                      [(LANES, F32)] * 2)
    y, hsave = _ssd_scan(xact, dt, da, "ssd_scan")
    y_ssd = _rowwise("ssd_post", _ssd_post, [y, (xact, SSD_D_INNER, 0), (proj, SSD_D_INNER, 0)],
                     [p["ssd_d"][e], p["ssd_norm_g"][e]], [(SSD_D_INNER, BF16)])[0]
    cqn = _prenorm((proj, MLA_Q_RANK, PROJ_CQ // MLA_Q_RANK), p["mla_q_norm_g"][e], "mla_qnorm")
    ckvn = _prenorm((proj, MLA_KV_RANK, PROJ_CKV // MLA_KV_RANK), p["mla_kv_norm_g"][e], "mla_kvnorm")
    q = _mm(cqn, p["mla_w_q_up"][e], "nn", "mla_q_up")
    kv = _mm(ckvn, p["mla_w_kv_up"][e], "nn", "mla_kv_up")
    kr = _rowwise("mla_krope", lambda r0, x, c, s: (_rope(x, c, s),), [(proj, LANES, PROJ_KR // LANES), cos, sin], [],
                  [(LANES, F32)])[0]
    qr = _per_head("mla_q_rope", lambda a, c, s: (_rope(a, c, s),), [q], [cos, sin], [BF16])[0]
    km, vb = _per_head("mla_key_slots", _key_slots, [kv], [kr], [BF16, BF16])
    o = _attn_fwd(qr, km, vb, "mla_attn")
    m1 = _mm(y_ssd, p["w_out_ssd"][e], "nn", "sm_out_ssd")
    m = _mm(o, p["w_out_att"][e], "nn", "sm_out_att", add=m1)
    h2 = _add_postnorm(h, [m], p["mix_post_g"][l], "sm_postnorm")
    return h2, (h, hn, proj, xc, xact, dt, da, y, hsave, y_ssd, cqn, ckvn, qr, km, vb, o, m)


def _ssdmla_bwd(dh, saved, p, l, e, cos, sin, grads):
    h, hn, proj, xc, xact, dt, da, y, hsave, y_ssd, cqn, ckvn, qr, km, vb, o, m = saved
    dm, grads["mix_post_g"][l] = _postnorm_bwd(m, p["mix_post_g"][l], dh, "sm_postnorm_bwd")
    grads["w_out_ssd"][e] = _mm(y_ssd, dm, "tn", "sm_out_ssd_dw")
    grads["w_out_att"][e] = _mm(o, dm, "tn", "sm_out_att_dw")
    dy_ssd = _mm(dm, p["w_out_ssd"][e], "nt", "sm_out_ssd_dx")
    do = _mm(dm, p["w_out_att"][e], "nt", "sm_out_att_dx")
    dqr, dkv, dkr = _attn_bwd(qr, km, vb, o, do, "mla_attn_bwd")
    dq = _per_head("mla_q_rope_bwd", lambda a, c, s: (_rope_t(a, c, s),), [dqr], [cos, sin], [F32])[0]
    dkr_raw = _rowwise("mla_krope_bwd", lambda r0, d, c, s: (_rope_t(d, c, s),), [dkr, cos, sin], [], [(LANES, F32)])[0]
    grads["mla_w_q_up"][e] = _mm(cqn, dq, "tn", "mla_q_up_dw")
    dcqn = _mm(dq, p["mla_w_q_up"][e], "nt", "mla_q_up_dx")
    (dcq,), (grads["mla_q_norm_g"][e],) = _rowwise_vjp(
        "mla_qnorm_bwd", _rms_rows, [(proj, MLA_Q_RANK, PROJ_CQ // MLA_Q_RANK)], [p["mla_q_norm_g"][e]], [dcqn])
    grads["mla_w_kv_up"][e] = _mm(ckvn, dkv, "tn", "mla_kv_up_dw")
    dckvn = _mm(dkv, p["mla_w_kv_up"][e], "nt", "mla_kv_up_dx")
    (dckv,), (grads["mla_kv_norm_g"][e],) = _rowwise_vjp(
        "mla_kvnorm_bwd", _rms_rows, [(proj, MLA_KV_RANK, PROJ_CKV // MLA_KV_RANK)], [p["mla_kv_norm_g"][e]], [dckvn])
    (dy, dxskip, dz), (grads["ssd_d"][e], grads["ssd_norm_g"][e]) = _rowwise_vjp(
        "ssd_post_bwd", _ssd_post, [y, (xact, SSD_D_INNER, 0), (proj, SSD_D_INNER, 0)], [p["ssd_d"][e], p["ssd_norm_g"][e]], [dy_ssd])
    dxs, db, dc, ddt, dda = _ssd_scan_bwd(xact, dt, da, hsave, dy, dxskip, "ssd_scan_bwd")
    dxact = jnp.concatenate([dxs, db, dc], axis=1)
    (dxc,), _ = _rowwise_vjp("ssd_act_bwd", _ssd_act, [xc], [], [dxact])
    dxbc, grads["ssd_conv_w"][e], grads["ssd_conv_b"][e] = _conv_bwd(
        proj, p["ssd_conv_w"][e], dxc, "ssd_conv_bwd", cw=SSD_GW, c0=PROJ_XBC // SSD_GW)
    (ddtraw,), (grads["ssd_dt_bias"][e], grads["ssd_a_log"][e]) = _rowwise_vjp(
        "ssd_dt_bwd", _ssd_dt, [(proj, LANES, PROJ_DT // LANES)], [p["ssd_dt_bias"][e], p["ssd_a_log"][e]], [ddt, dda])
    dproj = jnp.concatenate([dz, dxbc, ddtraw, dcq, dckv, dkr_raw], axis=1)
    grads["w_in"][e] = _mm(hn, dproj, "tn", "sm_in_dw")
    dhn = _mm(dproj, p["w_in"][e], "nt", "sm_in_dx")
    dh, grads["mix_pre_g"][l] = _prenorm_bwd_add(h, p["mix_pre_g"][l], [dhn], dh, "sm_prenorm_bwd")
    return dh


GAINS = ("mix_pre_g", "mix_post_g", "mlp_pre_g", "mlp_post_g", "ssd_norm_g", "mla_q_norm_g", "mla_kv_norm_g", "ssd_conv_b", "rg_conv_b")
HEAD_VECS = ("ssd_dt_bias", "ssd_a_log", "ssd_d")
LRU_VECS = ("rg_b_a", "rg_b_i", "rg_lambda")
IN_DT_END = SSD_D_INNER + SSD_CONV_CH + SSD_HEADS
IN_KR = IN_DT_END + MLA_Q_RANK + MLA_KV_RANK


def _layout_params(w):
    p = {k: w[k][:, None, :] for k in GAINS}
    for k in HEAD_VECS:
        p[k] = jnp.pad(w[k], ((0, 0), (0, LANES - SSD_HEADS)))[:, None, :]
    for k in LRU_VECS:
        p[k] = w[k].reshape(-1, LRU_BLOCKS, 1, LRU_BLOCK)
    for k in ("w_up", "w_down", "mla_w_kv_up", "rg_w_x", "rg_w_y", "rg_w_out"):
        p[k] = w[k] if isinstance(w[k], list) else w[k].astype(BF16)
    for k in ("ssd_conv_w", "rg_conv_w", "rg_w_a", "rg_w_i"):
        p[k] = w[k]
    wi = w["w_in"]

    def zcols(n):
        return jnp.zeros(wi.shape[:2] + (n,), wi.dtype)

    p["w_in"] = jnp.concatenate([wi[..., :IN_DT_END], zcols(PROJ_CQ - IN_DT_END), wi[..., IN_DT_END:IN_KR], zcols(ROPE_LO),
                                 wi[..., IN_KR:], zcols(LANES - ROPE_HI)], axis=-1).astype(BF16)
    wq = w["mla_w_q_up"].reshape(-1, MLA_Q_RANK, MLA_HEADS, MLA_NOPE + MLA_ROPE)
    p["mla_w_q_up"] = jnp.pad(wq, ((0, 0), (0, 0), (0, 0), (0, LANES - MLA_NOPE - MLA_ROPE))).reshape(-1, MLA_Q_RANK, MLA_HEADS * LANES).astype(BF16)
    wo = w["w_out_ab"]
    p["w_out_ssd"] = wo[:, :SSD_D_INNER].astype(BF16)
    wa = wo[:, SSD_D_INNER:].reshape(-1, MLA_HEADS, MLA_V, D_MODEL)
    p["w_out_att"] = jnp.pad(wa, ((0, 0), (0, 0), (LANES - MLA_V, 0), (0, 0))).reshape(-1, MLA_HEADS * LANES, D_MODEL).astype(BF16)
    return p


def _rope_tables(t):
    pos = (jnp.arange(t) - PAD).astype(F32)
    inv = ROPE_BASE ** (-jnp.arange(0, MLA_ROPE, 2, dtype=F32) / MLA_ROPE)
    ang = pos[:, None] * inv[None, :]
    c, s = jnp.cos(ang), jnp.sin(ang)
    one, zero = jnp.ones((t, MLA_NOPE), F32), jnp.zeros((t, MLA_NOPE), F32)
    tail = LANES - ROPE_HI
    return (jnp.concatenate([one, c, c, one[:, :tail]], axis=1), jnp.concatenate([zero, -s, s, zero[:, :tail]], axis=1))


GRAD_KEYS = GAINS + HEAD_VECS + LRU_VECS + ("w_up", "w_down", "mla_w_kv_up", "rg_w_x", "rg_w_y", "rg_w_out", "ssd_conv_w",
                                            "rg_conv_w", "rg_w_a", "rg_w_i", "w_in", "mla_w_q_up", "w_out_ssd", "w_out_att")


def _device_step(x, meta, target, p):
    t = PAD + N_META + x.shape[0]
    cos, sin = _rope_tables(t)
    h = jnp.concatenate([jnp.zeros((PAD, D_MODEL), F32), meta, x], axis=0)
    n_even, n_odd = (DEPTH + 1) // 2, DEPTH // 2
    saved = []
    for l in range(DEPTH):
        if l % 2 == 0:
            h, sm = _ssdmla_fwd(h, p, l, l // 2, cos, sin)
        else:
            h, sm = _rglru_fwd(h, p, l, l // 2)
        h, sp = _mlp_fwd(h, p, l)
        saved.append((sm, sp))
    sq, dh = _loss_and_grad(h, target, "loss")
    per_layer = {"mix_pre_g": DEPTH, "mix_post_g": DEPTH, "mlp_pre_g": DEPTH, "mlp_post_g": DEPTH, "w_up": DEPTH, "w_down": DEPTH}
    grads = {k: [None] * per_layer.get(k, n_odd if k.startswith("rg_") else n_even) for k in GRAD_KEYS}
    for l in reversed(range(DEPTH)):
        sm, sp = saved[l]
        dh = _mlp_bwd(dh, sp, p, l, grads)
        if l % 2 == 0:
            dh = _ssdmla_bwd(dh, sm, p, l, l // 2, cos, sin, grads)
        else:
            dh = _rglru_bwd(dh, sm, p, l, l // 2, grads)
    return sq, dh, grads


MESH = pl.DeviceIdType.MESH
ANY = pl.BlockSpec(memory_space=pl.ANY)


def _mesh_pos():
    return lax.axis_index("x"), lax.axis_index("y"), lax.axis_index("c")


def _other_chips(x, y):
    return [(1 - x, y), (x, 1 - y), (1 - x, 1 - y)]


def _remote(src, dst, send_sems, recv_sems, k, to):
    return pltpu.make_async_remote_copy(src_ref=src, dst_ref=dst, send_sem=send_sems.at[k], recv_sem=recv_sems.at[k],
                                        device_id=to, device_id_type=MESH)


def _gather_chips(srcs, name):
    n = len(srcs)
    nc = N_CHIPS - 1

    def body(*refs):
        src_refs, out_refs = refs[:n], refs[n:2 * n]
        send_sems, recv_sems, local_sems = refs[2 * n:]
        x, y, c = _mesh_pos()
        sib = (x, y, 1 - c)
        me = 2 * x + y
        chips = _other_chips(x, y)

        def half(ref, hc):
            h = ref.shape[0] // 2
            return ref.at[pl.ds(hc * h, h)]

        mine = [pltpu.make_async_copy(s, o.at[me], local_sems.at[t]) for t, (s, o) in enumerate(zip(src_refs, out_refs))]
        for cp in mine:
            cp.start()
        first = [_remote(half(s, c), half(o.at[me], c), send_sems, recv_sems, 2 * nc * t + j, (cx, cy, c))
                 for t, (s, o) in enumerate(zip(src_refs, out_refs)) for j, (cx, cy) in enumerate(chips)]
        for cp in first:
            cp.start()
        passed = []
        for t, (s, o) in enumerate(zip(src_refs, out_refs)):
            for j, (cx, cy) in enumerate(chips):
                slot = half(o.at[2 * cx + cy], c)
                _remote(half(s, c), slot, send_sems, recv_sems, 2 * nc * t + j, (cx, cy, c)).wait_recv()
                passed.append(_remote(slot, slot, send_sems, recv_sems, 2 * nc * t + nc + j, sib))
                passed[-1].start()
        for t, (s, o) in enumerate(zip(src_refs, out_refs)):
            for j, (cx, cy) in enumerate(chips):
                _remote(half(s, c), half(o.at[2 * cx + cy], 1 - c), send_sems, recv_sems, 2 * nc * t + nc + j, sib).wait_recv()
        for cp in first + passed:
            cp.wait_send()
        for cp in mine:
            cp.wait()

    return pl.pallas_call(
        body, name=name, in_specs=[ANY] * n, out_specs=[ANY] * n,
        out_shape=[jax.ShapeDtypeStruct((N_CHIPS,) + s.shape, s.dtype) for s in srcs],
        scratch_shapes=[pltpu.SemaphoreType.DMA((2 * nc * n,)), pltpu.SemaphoreType.DMA((2 * nc * n,)), pltpu.SemaphoreType.DMA((n,))],
    )(*srcs)


def _pair_exchange(gs, name):
    n = len(gs)

    def body(*refs):
        g_refs, o_refs, send_sems, recv_sems = refs[:n], refs[n:2 * n], refs[2 * n], refs[2 * n + 1]
        x, y, c = _mesh_pos()
        sib = (x, y, 1 - c)
        cps = [_remote(g.at[pl.ds(0, N_CHIPS), 1 - c], o, send_sems, recv_sems, t, sib) for t, (g, o) in enumerate(zip(g_refs, o_refs))]
        for cp in cps:
            cp.start()
        for cp in cps:
            cp.wait_recv()
        for cp in cps:
            cp.wait_send()

    return pl.pallas_call(
        body, name=name, in_specs=[ANY] * n, out_specs=[ANY] * n,
        out_shape=[jax.ShapeDtypeStruct((g.shape[0],) + g.shape[2:], g.dtype) for g in gs],
        scratch_shapes=[pltpu.SemaphoreType.DMA((n,)), pltpu.SemaphoreType.DMA((n,))],
    )(*gs)


def _chip_exchange(ps, groups, name):
    n, ng = len(ps), len(groups)
    nc = N_CHIPS - 1

    def body(*refs):
        p_refs, q_refs = refs[:n], refs[n:n + ng]
        send_sems, recv_sems, local_sems = refs[n + ng:]
        x, y, c = _mesh_pos()
        me = 2 * x + y
        chips = _other_chips(x, y)
        slots = [(t, q, li) for q, idx in zip(q_refs, groups) for li, t in enumerate(idx)]
        mine = [pltpu.make_async_copy(p_refs[t].at[me], q.at[me, li], local_sems.at[t]) for t, q, li in slots]
        for cp in mine:
            cp.start()
        sends = [_remote(p_refs[t].at[2 * cx + cy], q.at[me, li], send_sems, recv_sems, nc * t + j, (cx, cy, c))
                 for t, q, li in slots for j, (cx, cy) in enumerate(chips)]
        for cp in sends:
            cp.start()
        for t, q, li in slots:
            for j, (cx, cy) in enumerate(chips):
                _remote(p_refs[t].at[me], q.at[2 * cx + cy, li], send_sems, recv_sems, nc * t + j, (cx, cy, c)).wait_recv()
        for cp in sends:
            cp.wait_send()
        for cp in mine:
            cp.wait()

    return pl.pallas_call(
        body, name=name, in_specs=[ANY] * n, out_specs=[ANY] * ng,
        out_shape=[jax.ShapeDtypeStruct((N_CHIPS, len(idx)) + ps[idx[0]].shape[1:], ps[idx[0]].dtype) for idx in groups],
        scratch_shapes=[pltpu.SemaphoreType.DMA((nc * n,)), pltpu.SemaphoreType.DMA((nc * n,)), pltpu.SemaphoreType.DMA((n,))],
    )(*ps)


def _pair_share(fs, name):
    n = len(fs)

    def body(*refs):
        f_refs, o_refs = refs[:n], refs[n:2 * n]
        send_sems, recv_sems, local_sems = refs[2 * n:]
        x, y, c = _mesh_pos()
        sib = (x, y, 1 - c)
        nl = lambda r: pl.ds(0, r.shape[0])
        mine = [pltpu.make_async_copy(f, o.at[nl(f), c], local_sems.at[t]) for t, (f, o) in enumerate(zip(f_refs, o_refs))]
        cps = [_remote(f, o.at[nl(f), c], send_sems, recv_sems, t, sib) for t, (f, o) in enumerate(zip(f_refs, o_refs))]
        for cp in mine + cps:
            cp.start()
        for t, (f, o) in enumerate(zip(f_refs, o_refs)):
            _remote(f, o.at[nl(f), 1 - c], send_sems, recv_sems, t, sib).wait_recv()
        for cp in cps:
            cp.wait_send()
        for cp in mine:
            cp.wait()

    return pl.pallas_call(
        body, name=name, in_specs=[ANY] * n, out_specs=[ANY] * n,
        out_shape=[jax.ShapeDtypeStruct((f.shape[0], 2) + f.shape[1:], f.dtype) for f in fs],
        scratch_shapes=[pltpu.SemaphoreType.DMA((n,)), pltpu.SemaphoreType.DMA((n,)), pltpu.SemaphoreType.DMA((n,))],
    )(*fs)


SUM_BLOCK = 512 * 1024


def _sum_pair(g, ra, c, name):
    n, _, h, w = g.shape
    tr = _tile(h, max(16, SUM_BLOCK // w), 16)

    def body(c_ref, g_ref, r_ref, o_ref):
        o_ref[...] = (g_ref[0] + r_ref[...]).astype(o_ref.dtype)

    return pl.pallas_call(
        body, name=name,
        grid_spec=pltpu.PrefetchScalarGridSpec(
            num_scalar_prefetch=1, grid=(n, h // tr),
            in_specs=[pl.BlockSpec((1, 1, tr, w), lambda s, i, cr: (s, cr[0], i, 0)), pl.BlockSpec((1, tr, w), lambda s, i, cr: (s, i, 0))],
            out_specs=pl.BlockSpec((1, tr, w), lambda s, i, cr: (s, i, 0))),
        out_shape=jax.ShapeDtypeStruct((n, h, w), BF16),
        compiler_params=_params(("parallel", "parallel")),
    )(c.reshape(1).astype(jnp.int32), g, ra)


def _sum_chips(q, name):
    n, nl, h, w = q.shape
    tr = _tile(h, max(16, SUM_BLOCK // w), 16)

    def body(*refs):
        acc = refs[0][0].astype(F32)
        for r in refs[1:n]:
            acc = acc + r[0].astype(F32)
        refs[n][...] = acc

    return pl.pallas_call(
        body, name=name, grid=(nl, h // tr),
        in_specs=[pl.BlockSpec((1, 1, tr, w), lambda l, i, k=k: (k, l, i, 0)) for k in range(n)],
        out_specs=pl.BlockSpec((1, tr, w), lambda l, i: (l, i, 0)),
        out_shape=jax.ShapeDtypeStruct((nl, h, w), F32),
        compiler_params=_params(("parallel", "parallel")),
    )(*([q] * n))


def _adamw(g, w, m, v, name):
    def f(r0, gg, ww, mm, vv):
        m2 = ADAM_B1 * mm + (1.0 - ADAM_B1) * gg
        v2 = ADAM_B2 * vv + (1.0 - ADAM_B2) * jnp.square(gg)
        m_hat = m2 / (1.0 - ADAM_B1 ** ADAM_STEP)
        v_hat = v2 / (1.0 - ADAM_B2 ** ADAM_STEP)
        return -ADAM_LR * (m_hat / (jnp.sqrt(v_hat) + ADAM_EPS) + ADAM_WD * ww), m2, v2

    return _rowwise(name, f, [g, w, m, v], [], [(g.shape[1], F32)] * 3, tr=_tile(g.shape[0], 512))


WEIGHTS = (
    ("meta_tokens", (N_META, D_MODEL), 1), ("mix_pre_g", (DEPTH, D_MODEL), None), ("mix_post_g", (DEPTH, D_MODEL), None),
    ("mlp_pre_g", (DEPTH, D_MODEL), None), ("mlp_post_g", (DEPTH, D_MODEL), None), ("w_up", (DEPTH, D_MODEL, D_FF), 2),
    ("w_down", (DEPTH, D_FF, D_MODEL), 1), ("w_in", (2, D_MODEL, 3248), 2), ("ssd_conv_w", (2, CONV_K, SSD_CONV_CH), 2),
    ("ssd_conv_b", (2, SSD_CONV_CH), None), ("ssd_dt_bias", (2, SSD_HEADS), None), ("ssd_a_log", (2, SSD_HEADS), None),
    ("ssd_d", (2, SSD_HEADS), None), ("ssd_norm_g", (2, SSD_D_INNER), None), ("mla_q_norm_g", (2, MLA_Q_RANK), None),
    ("mla_w_q_up", (2, MLA_Q_RANK, MLA_HEADS * (MLA_NOPE + MLA_ROPE)), 2), ("mla_kv_norm_g", (2, MLA_KV_RANK), None),
    ("mla_w_kv_up", (2, MLA_KV_RANK, MLA_HEADS * (MLA_NOPE + MLA_V)), 2), ("w_out_ab", (2, SSD_D_INNER + MLA_HEADS * MLA_V, D_MODEL), 1),
    ("rg_w_x", (2, D_MODEL, LRU_WIDTH), 2), ("rg_w_y", (2, D_MODEL, LRU_WIDTH), 2), ("rg_conv_w", (2, CONV_K, LRU_WIDTH), 2),
    ("rg_conv_b", (2, LRU_WIDTH), 1), ("rg_w_a", (2, LRU_BLOCKS, LRU_BLOCK, LRU_BLOCK), None), ("rg_b_a", (2, LRU_WIDTH), 1),
    ("rg_w_i", (2, LRU_BLOCKS, LRU_BLOCK, LRU_BLOCK), None), ("rg_b_i", (2, LRU_WIDTH), 1), ("rg_lambda", (2, LRU_WIDTH), 1),
    ("rg_w_out", (2, LRU_WIDTH, D_MODEL), 1),
)
BIG = {"w_up": "col", "w_down": "row", "w_in": "col", "mla_w_q_up": "col", "mla_w_kv_up": "col", "w_out_ab": "row",
       "rg_w_x": "col", "rg_w_y": "col", "rg_w_out": "row"}
DIRECT = ("w_up", "w_down")
FLAT_QUANTUM = 2 * 16 * LANES
TABLE = {name: (shape, d) for name, shape, d in WEIGHTS}
SMALL_SHARDED = tuple(name for name, _, d in WEIGHTS if d is not None and name not in BIG)
REPLICATED = tuple(name for name, _, d in WEIGHTS if d is None)


def _chips_to_full(a, kind):
    if kind == "col":
        return jnp.moveaxis(a, 0, 2).reshape(a.shape[1], a.shape[2], -1)
    return jnp.moveaxis(a, 0, 1).reshape(a.shape[1], -1, a.shape[3])


def _full_to_chips(g, kind):
    if kind == "col":
        return jnp.moveaxis(g.reshape(g.shape[0], N_CHIPS, -1), 1, 0)
    return g.reshape(N_CHIPS, -1, g.shape[1])


def _chips_to_full_1(pc, kind):
    return jnp.moveaxis(pc, 0, 1).reshape(pc.shape[1], -1) if kind == "col" else pc.reshape(-1, pc.shape[2])


def _shard_shape(shape, d):
    return shape[:d] + (shape[d] // N_CHIPS,) + shape[d + 1:]


def _shard_major(full, d):
    s = full.shape
    return jnp.moveaxis(full.reshape(s[:d] + (N_CHIPS, s[d] // N_CHIPS) + s[d + 1:]), d, 0).reshape(N_CHIPS, -1)


def _from_shard_major(a, shape, d):
    ss = _shard_shape(shape, d)
    return jnp.moveaxis(a.reshape((N_CHIPS,) + ss), 0, d).reshape(shape)


def _pad_cols(a, quantum):
    n = a.shape[-1]
    return jnp.pad(a, [(0, 0)] * (a.ndim - 1) + [(0, -n % quantum)])


def _gather_weights(w):
    small = _pad_cols(jnp.concatenate([w[n].reshape(-1) for n in SMALL_SHARDED]), FLAT_QUANTUM).reshape(2, -1, LANES)
    outs = _gather_chips([w[n].astype(BF16) for n in BIG] + [small], "gather_weights")
    full = {n: w[n] for n in REPLICATED}
    for name, a in zip(BIG, outs):
        if name in DIRECT:
            full[name] = [Gathered(a, BIG[name], l) for l in range(a.shape[1])]
        else:
            full[name] = _chips_to_full(a, BIG[name])
    got, off = outs[-1].reshape(N_CHIPS, -1), 0
    for name in SMALL_SHARDED:
        shape, d = TABLE[name]
        n = int(np.prod(_shard_shape(shape, d)))
        full[name] = _from_shard_major(got[:, off:off + n], shape, d)
        off += n
    return full


def _big_pieces(g):
    def w_in(a):
        return jnp.concatenate([a[:, :IN_DT_END], a[:, PROJ_CQ:PROJ_KR], a[:, PROJ_KR + ROPE_LO:PROJ_KR + ROPE_HI]], axis=1)

    def q_up(a):
        return a.reshape(MLA_Q_RANK, MLA_HEADS, LANES)[:, :, :MLA_NOPE + MLA_ROPE].reshape(MLA_Q_RANK, -1)

    def out_ab(s, a):
        return jnp.concatenate([s, a.reshape(MLA_HEADS, LANES, D_MODEL)[:, LANES - MLA_V:, :].reshape(-1, D_MODEL)], axis=0)

    full = {"w_down": g["w_down"], "w_in": [w_in(a) for a in g["w_in"]], "mla_w_q_up": [q_up(a) for a in g["mla_w_q_up"]],
            "mla_w_kv_up": g["mla_w_kv_up"], "w_out_ab": [out_ab(s, a) for s, a in zip(g["w_out_ssd"], g["w_out_att"])],
            "rg_w_x": g["rg_w_x"], "rg_w_y": g["rg_w_y"], "rg_w_out": g["rg_w_out"]}
    return {name: (g[name] if name == "w_up" else [_full_to_chips(a, BIG[name]) for a in full[name]]) for name in BIG}


def _small_grads(g, dh):
    out = {k: jnp.stack(g[k])[:, 0, :] for k in GAINS}
    for k in HEAD_VECS:
        out[k] = jnp.stack(g[k])[:, 0, :SSD_HEADS]
    for k in LRU_VECS:
        out[k] = jnp.stack(g[k]).reshape(-1, LRU_WIDTH)
    for k in ("ssd_conv_w", "rg_conv_w", "rg_w_a", "rg_w_i"):
        out[k] = jnp.stack(g[k])
    out["meta_tokens"] = dh[PAD:PAD + N_META]
    return out


def _natural_grads(g, dh):
    out = _small_grads(g, dh)
    for name, pcs in _big_pieces(g).items():
        out[name] = jnp.stack([_chips_to_full_1(pc, BIG[name]) for pc in pcs])
    return out


def _reduce_grads(g, dh, c):
    big, small = _big_pieces(g), _small_grads(g, dh)
    pieces, groups = [], []
    for name in BIG:
        groups.append(list(range(len(pieces), len(pieces) + len(big[name]))))
        pieces += [pc.reshape(N_CHIPS, 2, pc.shape[1] // 2, pc.shape[2]) for pc in big[name]]
    sharded = jnp.concatenate([_shard_major(small[n], TABLE[n][1]) for n in SMALL_SHARDED], axis=1)
    rep = _pad_cols(jnp.concatenate([small[n].reshape(-1) for n in REPLICATED]), N_CHIPS * FLAT_QUANTUM)
    n_sh, n_rep = sharded.shape[1], rep.shape[0] // N_CHIPS
    flat = _pad_cols(jnp.concatenate([sharded, rep.reshape(N_CHIPS, n_rep)], axis=1), FLAT_QUANTUM)
    groups.append([len(pieces)])
    pieces.append(flat.reshape(N_CHIPS, 2, -1, LANES))
    ras = _pair_exchange(pieces, "grads_pair_exchange")
    ps = [_sum_pair(a, ra, c, "grads_pair_sum") for a, ra in zip(pieces, ras)]
    qs = _chip_exchange(ps, groups, "grads_chip_exchange")
    outs = _pair_share([_sum_chips(q, "grads_chip_sum") for q in qs], "grads_pair_share")
    out = {name: o.reshape(o.shape[0], -1, o.shape[3]) for name, o in zip(BIG, outs)}
    f = outs[-1].reshape(-1)
    rep_all = _gather_chips([f[n_sh:n_sh + n_rep].reshape(2, -1, LANES)], "grads_gather_replicated")[0].reshape(-1)
    off = 0
    for name in SMALL_SHARDED:
        ss = _shard_shape(*TABLE[name])
        n = int(np.prod(ss))
        out[name] = f[off:off + n].reshape(ss)
        off += n
    off = 0
    for name in REPLICATED:
        shape = TABLE[name][0]
        n = int(np.prod(shape))
        out[name] = rep_all[off:off + n].reshape(shape)
        off += n
    return out


def kernel(x, meta_tokens, mix_pre_g, mix_post_g, mlp_pre_g, mlp_post_g, w_up, w_down, w_in, ssd_conv_w, ssd_conv_b, ssd_dt_bias, ssd_a_log, ssd_d, ssd_norm_g, mla_q_norm_g, mla_w_q_up, mla_kv_norm_g, mla_w_kv_up, w_out_ab, rg_w_x, rg_w_y, rg_conv_w, rg_conv_b, rg_w_a, rg_b_a, rg_w_i, rg_b_i, rg_lambda, rg_w_out, loss_target, m_meta_tokens, m_mix_pre_g, m_mix_post_g, m_mlp_pre_g, m_mlp_post_g, m_w_up, m_w_down, m_w_in, m_ssd_conv_w, m_ssd_conv_b, m_ssd_dt_bias, m_ssd_a_log, m_ssd_d, m_ssd_norm_g, m_mla_q_norm_g, m_mla_w_q_up, m_mla_kv_norm_g, m_mla_w_kv_up, m_w_out_ab, m_rg_w_x, m_rg_w_y, m_rg_conv_w, m_rg_conv_b, m_rg_w_a, m_rg_b_a, m_rg_w_i, m_rg_b_i, m_rg_lambda, m_rg_w_out, v_meta_tokens, v_mix_pre_g, v_mix_post_g, v_mlp_pre_g, v_mlp_post_g, v_w_up, v_w_down, v_w_in, v_ssd_conv_w, v_ssd_conv_b, v_ssd_dt_bias, v_ssd_a_log, v_ssd_d, v_ssd_norm_g, v_mla_q_norm_g, v_mla_w_q_up, v_mla_kv_norm_g, v_mla_w_kv_up, v_w_out_ab, v_rg_w_x, v_rg_w_y, v_rg_conv_w, v_rg_conv_b, v_rg_w_a, v_rg_b_a, v_rg_w_i, v_rg_b_i, v_rg_lambda, v_rg_w_out):
    names = [n for n, _, _ in WEIGHTS]
    w = dict(zip(names, (meta_tokens, mix_pre_g, mix_post_g, mlp_pre_g, mlp_post_g, w_up, w_down, w_in, ssd_conv_w, ssd_conv_b, ssd_dt_bias, ssd_a_log, ssd_d, ssd_norm_g, mla_q_norm_g, mla_w_q_up, mla_kv_norm_g, mla_w_kv_up, w_out_ab, rg_w_x, rg_w_y, rg_conv_w, rg_conv_b, rg_w_a, rg_b_a, rg_w_i, rg_b_i, rg_lambda, rg_w_out)))
    m = dict(zip(names, (m_meta_tokens, m_mix_pre_g, m_mix_post_g, m_mlp_pre_g, m_mlp_post_g, m_w_up, m_w_down, m_w_in, m_ssd_conv_w, m_ssd_conv_b, m_ssd_dt_bias, m_ssd_a_log, m_ssd_d, m_ssd_norm_g, m_mla_q_norm_g, m_mla_w_q_up, m_mla_kv_norm_g, m_mla_w_kv_up, m_w_out_ab, m_rg_w_x, m_rg_w_y, m_rg_conv_w, m_rg_conv_b, m_rg_w_a, m_rg_b_a, m_rg_w_i, m_rg_b_i, m_rg_lambda, m_rg_w_out)))
    v = dict(zip(names, (v_meta_tokens, v_mix_pre_g, v_mix_post_g, v_mlp_pre_g, v_mlp_post_g, v_w_up, v_w_down, v_w_in, v_ssd_conv_w, v_ssd_conv_b, v_ssd_dt_bias, v_ssd_a_log, v_ssd_d, v_ssd_norm_g, v_mla_q_norm_g, v_mla_w_q_up, v_mla_kv_norm_g, v_mla_w_kv_up, v_w_out_ab, v_rg_w_x, v_rg_w_y, v_rg_conv_w, v_rg_conv_b, v_rg_w_a, v_rg_b_a, v_rg_w_i, v_rg_b_i, v_rg_lambda, v_rg_w_out)))
    full = _gather_weights(w)
    p = _layout_params({k: a for k, a in full.items() if k != "meta_tokens"})
    sq, dh, grads = _device_step(x[0], full["meta_tokens"], loss_target[0], p)
    loss = lax.psum(0.5 * sq[0, 0] / D_MODEL, ("x", "y", "c"))
    g = _reduce_grads(grads, dh, lax.axis_index("c"))
    delta, new_m, new_v = {}, {}, {}
    for name in names:
        shape = g[name].shape
        two_d = (int(np.prod(shape[:-1])), shape[-1])
        res = _adamw(g[name].reshape(two_d), w[name].reshape(two_d), m[name].reshape(two_d), v[name].reshape(two_d), "adamw")
        delta[name], new_m[name], new_v[name] = (r.reshape(shape) for r in res)
    grad_x = dh[PAD + N_META:][None]
    return (loss, grad_x, *[g[n] for n in names], *[delta[n] for n in names], *[new_m[n] for n in names], *[new_v[n] for n in names])
```

```python
import functools

import jax
import jax.numpy as jnp
import numpy as np
from jax import lax
from jax.experimental import pallas as pl
from jax.experimental.pallas import tpu as pltpu

F32 = jnp.float32
BF16 = jnp.bfloat16

D_MODEL = 1024
DEPTH = 4
N_META = 16
CHUNK = 128
PAD = CHUNK - N_META
EPS = 1e-6
SSD_HEADS = 16
SSD_HEAD_DIM = 64
SSD_D_INNER = SSD_HEADS * SSD_HEAD_DIM
SSD_GROUPS = 2
SSD_STATE = 128
SSD_CONV_CH = SSD_D_INNER + 2 * SSD_GROUPS * SSD_STATE
MLA_HEADS = 16
MLA_NOPE = 64
MLA_ROPE = 32
MLA_V = 64
MLA_Q_RANK = 384
MLA_KV_RANK = 256
ROPE_BASE = 10000.0
LRU_WIDTH = 1280
LRU_BLOCKS = 10
LRU_BLOCK = 128
LRU_C = 8.0
D_FF = 4 * D_MODEL
ADAM_LR, ADAM_B1, ADAM_B2, ADAM_EPS, ADAM_WD, ADAM_STEP = 0.001, 0.9, 0.999, 1e-08, 0.01, 10

LANES = 128
VMEM_LIMIT = 56 * 1024 * 1024
HEAD_SLOT = 128
PROJ_Z, PROJ_XBC, PROJ_DT, PROJ_CQ, PROJ_CKV, PROJ_KR = 0, 1024, 2560, 2688, 3072, 3328
PROJ_W = 3456


def _tile(n, cap, mult=8):
    for t in range(min(n, cap), 0, -1):
        if n % t == 0 and t % mult == 0:
            return t
    return n


def _params(sem):
    return pltpu.CompilerParams(dimension_semantics=sem, vmem_limit_bytes=VMEM_LIMIT)


def _full_spec(shape, ngrid):
    nd = len(shape)
    if ngrid == 1:
        return pl.BlockSpec(shape, lambda i: (0,) * nd)
    if ngrid == 2:
        return pl.BlockSpec(shape, lambda i, j: (0,) * nd)
    return pl.BlockSpec(shape, lambda i, j, k: (0,) * nd)


_DIMS = {"nn": (((1,), (0,)), ((), ())), "nt": (((1,), (1,)), ((), ())), "tn": (((0,), (0,)), ((), ()))}


class Gathered:
    def __init__(self, arr, kind, layer):
        self.arr, self.kind, self.layer = arr, kind, layer
        _, _, r, c = arr.shape
        self.shape = (r, N_CHIPS * c) if kind == "col" else (N_CHIPS * r, c)


N_CHIPS = 4


def _mm(a, b, mode, name, out_dtype=F32, add=None, out_chip_major=False):
    if mode == "nn":
        (m, kc), (_, n) = a.shape, b.shape
    elif mode == "nt":
        (m, kc), (n, _) = a.shape, b.shape
    else:
        (kc, m), (_, n) = a.shape, b.shape
    tm = _tile(m, 1024, LANES) if mode == "tn" else _tile(m, 528, 16)
    tn = _tile(n // N_CHIPS if out_chip_major else n, 1280, LANES)
    tk = _tile(kc, 1024 if mode != "tn" else 528, LANES if mode != "tn" else 16)
    nk = kc // tk
    if mode == "tn":
        a_spec = pl.BlockSpec((tk, tm), lambda i, j, k: (k, i))
    else:
        a_spec = pl.BlockSpec((tm, tk), lambda i, j, k: (i, k))
    b_arr = b
    if isinstance(b, Gathered):
        b_arr, layer = b.arr, b.layer
        sr, sc = b.arr.shape[2:]
        br, bc = (tk, tn) if mode == "nn" else (tn, tk)
        assert mode in ("nn", "nt") and sr % br == 0 and sc % bc == 0

        def b_map(i, j, k):
            r, c = (k, j) if mode == "nn" else (j, k)
            if b.kind == "col":
                return ((c * bc) // sc, layer, r, ((c * bc) % sc) // bc)
            return ((r * br) // sr, layer, ((r * br) % sr) // br, c)

        b_spec = pl.BlockSpec((None, None, br, bc), b_map)
    elif mode == "nt":
        b_spec = pl.BlockSpec((tn, tk), lambda i, j, k: (j, k))
    else:
        b_spec = pl.BlockSpec((tk, tn), lambda i, j, k: (k, j))
    dims = _DIMS[mode]
    if out_chip_major:
        ns = n // N_CHIPS
        o_spec = pl.BlockSpec((None, tm, tn), lambda i, j, k: ((j * tn) // ns, i, ((j * tn) % ns) // tn))
        o_shape = jax.ShapeDtypeStruct((N_CHIPS, m, ns), out_dtype)
    else:
        o_spec = pl.BlockSpec((tm, tn), lambda i, j, k: (i, j))
        o_shape = jax.ShapeDtypeStruct((m, n), out_dtype)
    nadd = 0 if add is None else 1

    def body(a_ref, b_ref, *rest):
        o_ref, acc = rest[nadd], rest[nadd + 1:]
        p = lax.dot_general(a_ref[...].astype(BF16), b_ref[...].astype(BF16), dims, preferred_element_type=F32)

        def emit(v):
            o_ref[...] = (v + rest[0][...] if nadd else v).astype(o_ref.dtype)

        if nk == 1:
            emit(p)
        else:
            k = pl.program_id(2)

            @pl.when(k == 0)
            def _():
                acc[0][...] = p

            @pl.when(k > 0)
            def _():
                acc[0][...] += p

            @pl.when(k == nk - 1)
            def _():
                emit(acc[0][...])

    return pl.pallas_call(
        body, name=name, grid=(m // tm, n // tn, nk),
        in_specs=[a_spec, b_spec] + [o_spec] * nadd, out_specs=o_spec,
        out_shape=o_shape,
        scratch_shapes=[pltpu.VMEM((tm, tn), F32)] if nk > 1 else [],
        compiler_params=_params(("parallel", "parallel", "arbitrary")),
    )(a, b_arr, *([add] if nadd else []))


def _rowarg(r):
    return r if isinstance(r, tuple) else (r, r.shape[1], 0)


def _rowspec(r, tr, ncol):
    _, w, cb = r
    if ncol > 1:
        return pl.BlockSpec((tr, w // ncol), lambda j, i: (i, j))
    return pl.BlockSpec((tr, w), lambda j, i: (i, cb))


def _rowwise(name, f, rows, params, outs, tr=None, ncol=1):
    rows = [_rowarg(r) for r in rows]
    t = rows[0][0].shape[0]
    tr = tr or _tile(t, 528)
    nr, npm = len(rows), len(params)

    def body(*refs):
        vals = [r[...] for r in refs[:nr]] + [(p[0] if ncol > 1 else p[...]) for p in refs[nr:nr + npm]]
        res = f(pl.program_id(1) * tr, *vals)
        for o_ref, v in zip(refs[nr + npm:], res):
            o_ref[...] = v.astype(o_ref.dtype)

    def pspec(p):
        if ncol > 1:
            return pl.BlockSpec((1,) + p.shape[1:], lambda j, i, n=p.ndim: (j,) + (0,) * (n - 1))
        return _full_spec(p.shape, 2)

    return pl.pallas_call(
        body, name=name, grid=(ncol, t // tr),
        in_specs=[_rowspec(r, tr, ncol) for r in rows] + [pspec(p) for p in params],
        out_specs=[pl.BlockSpec((tr, w // ncol), lambda j, i: (i, j)) for w, _ in outs],
        out_shape=[jax.ShapeDtypeStruct((t, w), dt) for w, dt in outs],
        compiler_params=_params(("parallel", "parallel")),
    )(*[r[0] for r in rows], *params)


def _rowwise_vjp(name, f, rows, params, cts, tr=None, ncol=1, row_dtypes=None):
    rows = [_rowarg(r) for r in rows]
    cts = [_rowarg(c) for c in cts]
    t = rows[0][0].shape[0]
    tr = tr or _tile(t, 528)
    nr, npm, nc = len(rows), len(params), len(cts)
    row_dtypes = row_dtypes or [F32] * nr

    def body(*refs):
        i = pl.program_id(1)
        vals = [r[...] for r in refs[:nr]] + [(p[0] if ncol > 1 else p[...]) for p in refs[nr:nr + npm]]
        ct = tuple(c[...].astype(F32) for c in refs[nr + npm:nr + npm + nc])
        _, vjp = jax.vjp(lambda *a: tuple(f(i * tr, *a)), *vals)
        g = vjp(ct)
        outs = refs[nr + npm + nc:]
        for o_ref, v in zip(outs[:nr], g[:nr]):
            o_ref[...] = v.astype(o_ref.dtype)
        pg = [(v[None] if ncol > 1 else v) for v in g[nr:]]

        @pl.when(i == 0)
        def _():
            for o_ref, v in zip(outs[nr:], pg):
                o_ref[...] = v

        @pl.when(i > 0)
        def _():
            for o_ref, v in zip(outs[nr:], pg):
                o_ref[...] += v

    def pspec(p):
        if ncol > 1:
            return pl.BlockSpec((1,) + p.shape[1:], lambda j, i, n=p.ndim: (j,) + (0,) * (n - 1))
        return _full_spec(p.shape, 2)

    res = pl.pallas_call(
        body, name=name, grid=(ncol, t // tr),
        in_specs=[_rowspec(r, tr, ncol) for r in rows] + [pspec(p) for p in params] + [_rowspec(c, tr, ncol) for c in cts],
        out_specs=[pl.BlockSpec((tr, w // ncol), lambda j, i: (i, j)) for _, w, _ in rows] + [pspec(p) for p in params],
        out_shape=[jax.ShapeDtypeStruct((t, w), dt) for (_, w, _), dt in zip(rows, row_dtypes)]
        + [jax.ShapeDtypeStruct(p.shape, F32) for p in params],
        compiler_params=_params(("parallel", "arbitrary")),
    )(*[r[0] for r in rows], *params, *[c[0] for c in cts])
    return res[:nr], res[nr:]


def _valid(row0, tr):
    return (row0 + lax.broadcasted_iota(jnp.int32, (tr, 1), 0)) >= PAD


def _rms(x, g):
    return x * lax.rsqrt(jnp.mean(x * x, axis=-1, keepdims=True) + EPS) * g


def _softplus(x):
    return jnp.where(x < -15.0, jnp.exp(x), jnp.maximum(x, 0.0) + jnp.log(1.0 + jnp.exp(-jnp.abs(x))))


def _neg_expm1(z):
    return jnp.where(z > -0.01, -z * (1.0 + z * (0.5 + z * (1.0 / 6.0))), 1.0 - jnp.exp(z))


def _prenorm(h, g, name):
    return _rowwise(name, lambda r0, x, gg: (_rms(x, gg),), [h], [g], [(_rowarg(h)[1], BF16)])[0]


def _add_postnorm(h, ms, g, name):
    def f(r0, x, *rest):
        return (x + _rms(functools.reduce(jnp.add, rest[:-1]), rest[-1]),)

    return _rowwise(name, f, [h] + list(ms), [g], [(h.shape[1], F32)])[0]


def _postnorm_bwd(m, g, dh, name):
    (dm,), (dg,) = _rowwise_vjp(name, lambda r0, mm, gg: (_rms(mm, gg),), [m], [g], [dh])
    return dm, dg


def _prenorm_bwd_add(h, g, dhns, dh, name):
    t, w = h.shape
    tr = _tile(t, 528)
    nd = len(dhns)

    def body(h_ref, g_ref, *refs):
        dh_ref, o_ref, dg_ref = refs[nd:]
        i = pl.program_id(0)
        _, vjp = jax.vjp(_rms, h_ref[...], g_ref[...])
        dhn = refs[0][...].astype(F32)
        for r in refs[1:nd]:
            dhn = dhn + r[...].astype(F32)
        dx, dg = vjp(dhn)
        o_ref[...] = dh_ref[...] + dx

        @pl.when(i == 0)
        def _():
            dg_ref[...] = dg

        @pl.when(i > 0)
        def _():
            dg_ref[...] += dg

    row = pl.BlockSpec((tr, w), lambda i: (i, 0))
    return pl.pallas_call(
        body, name=name, grid=(t // tr,), in_specs=[row, _full_spec(g.shape, 1)] + [row] * (nd + 1),
        out_specs=[row, _full_spec(g.shape, 1)],
        out_shape=[jax.ShapeDtypeStruct((t, w), F32), jax.ShapeDtypeStruct(g.shape, F32)],
        compiler_params=_params(("arbitrary",)),
    )(h, g, *dhns, dh)


def _relu2(a, name):
    return _rowwise(name, lambda r0, x: (jnp.square(jnp.maximum(x, 0.0)),), [a], [], [(a.shape[1], BF16)], tr=_tile(a.shape[0], 264))[0]


def _relu2_bwd(a, du, name):
    return _rowwise(name, lambda r0, x, d: (2.0 * jnp.maximum(x, 0.0) * d,), [a, du], [], [(a.shape[1], BF16)],
                    tr=_tile(a.shape[0], 264))[0]


def _loss_and_grad(h, target, name):
    t, w = h.shape
    nb = t // CHUNK

    def body(h_ref, t_ref, s_ref, dh_ref):
        i = pl.program_id(0)

        @pl.when(i == 0)
        def _():
            s_ref[...] = jnp.zeros_like(s_ref)
            dh_ref[...] = jnp.zeros_like(dh_ref)

        @pl.when(i > 0)
        def _():
            err = h_ref[...] - t_ref[...]
            s_ref[...] += jnp.sum(err * err)
            dh_ref[...] = err * (1.0 / w)

    return pl.pallas_call(
        body, name=name, grid=(nb,),
        in_specs=[pl.BlockSpec((CHUNK, w), lambda i: (i, 0)), pl.BlockSpec((CHUNK, w), lambda i: (jnp.maximum(i - 1, 0), 0))],
        out_specs=[_full_spec((1, LANES), 1), pl.BlockSpec((CHUNK, w), lambda i: (i, 0))],
        out_shape=[jax.ShapeDtypeStruct((1, LANES), F32), jax.ShapeDtypeStruct((t, w), F32)],
        compiler_params=_params(("arbitrary",)),
    )(h, target)


def _mlp_fwd(h, p, l):
    hn = _prenorm(h, p["mlp_pre_g"][l], "mlp_prenorm")
    a = _mm(hn, p["w_up"][l], "nn", "mlp_up")
    u = _relu2(a, "mlp_relu2")
    d = _mm(u, p["w_down"][l], "nn", "mlp_down")
    h2 = _add_postnorm(h, [d], p["mlp_post_g"][l], "mlp_postnorm")
    return h2, (h, hn, a, u, d)


def _mlp_bwd(dh, saved, p, l, grads):
    h, hn, a, u, d = saved
    dd, grads["mlp_post_g"][l] = _postnorm_bwd(d, p["mlp_post_g"][l], dh, "mlp_postnorm_bwd")
    grads["w_down"][l] = _mm(u, dd, "tn", "mlp_down_dw")
    du = _mm(dd, p["w_down"][l], "nt", "mlp_down_dx")
    da = _relu2_bwd(a, du, "mlp_relu2_bwd")
    grads["w_up"][l] = _mm(hn, da, "tn", "mlp_up_dw", out_chip_major=True)
    dhn = _mm(da, p["w_up"][l], "nt", "mlp_up_dx")
    dh, grads["mlp_pre_g"][l] = _prenorm_bwd_add(h, p["mlp_pre_g"][l], [dhn], dh, "mlp_prenorm_bwd")
    return dh


def _dot(a, b, mode):
    return lax.dot_general(a.astype(BF16), b.astype(BF16), _DIMS[mode], preferred_element_type=F32)


@jax.custom_vjp
def _bnn(a, b):
    return _dot(a, b, "nn")


_bnn.defvjp(lambda a, b: (_dot(a, b, "nn"), (a, b)), lambda r, ct: (_dot(ct, r[1], "nt"), _dot(r[0], ct, "tn")))


@jax.custom_vjp
def _bnt(a, b):
    return _dot(a, b, "nt")


_bnt.defvjp(lambda a, b: (_dot(a, b, "nt"), (a, b)), lambda r, ct: (_dot(ct, r[1], "nn"), _dot(ct, r[0], "tn")))


@jax.custom_vjp
def _btn(a, b):
    return _dot(a, b, "tn")


_btn.defvjp(lambda a, b: (_dot(a, b, "tn"), (a, b)), lambda r, ct: (_dot(r[1], ct, "nt"), _dot(r[0], ct, "nn")))


CONV_K = 4
HALO = 8


def _conv_fwd(x, w, b, name, cw, c0=0):
    t, c = x.shape[0], w.shape[1]
    tr = _tile(t, 528)
    hb = tr // HALO

    def body(x_ref, halo_ref, w_ref, b_ref, o_ref, ext):
        i = pl.program_id(1)
        ext[pl.ds(0, HALO), :] = jnp.where(i > 0, halo_ref[...], 0.0)
        ext[pl.ds(HALO, tr), :] = x_ref[...]
        acc = jnp.broadcast_to(b_ref[...], (tr, cw))
        for k in range(CONV_K):
            acc = acc + w_ref[pl.ds(k, 1), :] * ext[pl.ds(HALO - (CONV_K - 1) + k, tr), :]
        o_ref[...] = acc

    return pl.pallas_call(
        body, name=name, grid=(c // cw, t // tr),
        in_specs=[pl.BlockSpec((tr, cw), lambda j, i: (i, c0 + j)),
                  pl.BlockSpec((HALO, cw), lambda j, i: (jnp.maximum(i * hb - 1, 0), c0 + j)),
                  pl.BlockSpec((CONV_K, cw), lambda j, i: (0, j)), pl.BlockSpec((1, cw), lambda j, i: (0, j))],
        out_specs=pl.BlockSpec((tr, cw), lambda j, i: (i, j)),
        out_shape=jax.ShapeDtypeStruct((t, c), F32),
        scratch_shapes=[pltpu.VMEM((tr + HALO, cw), F32)],
        compiler_params=_params(("parallel", "parallel")),
    )(x, x, w, b)


def _conv_bwd(x, w, dy, name, cw, c0=0):
    t, c = x.shape[0], w.shape[1]
    tr = _tile(t, 528)
    hb = tr // HALO
    nb = t // tr

    def body(x_ref, xh_ref, w_ref, dy_ref, dyh_ref, dx_ref, dw_ref, db_ref, xe, de):
        c = cw
        i = pl.program_id(1)
        xe[pl.ds(0, HALO), :] = jnp.where(i > 0, xh_ref[...], 0.0)
        xe[pl.ds(HALO, tr), :] = x_ref[...]
        de[pl.ds(0, tr), :] = dy_ref[...]
        de[pl.ds(tr, HALO), :] = jnp.where(i < nb - 1, dyh_ref[...], 0.0)
        dy = dy_ref[...]
        acc = jnp.zeros((tr, c), F32)
        dw = jnp.zeros((CONV_K, c), F32)
        rows = lax.broadcasted_iota(jnp.int32, (CONV_K, 1), 0)
        for k in range(CONV_K):
            acc = acc + w_ref[pl.ds(k, 1), :] * de[pl.ds(CONV_K - 1 - k, tr), :]
            dwk = jnp.sum(dy * xe[pl.ds(HALO - (CONV_K - 1) + k, tr), :], axis=0, keepdims=True)
            dw = dw + jnp.where(rows == k, dwk, 0.0)
        dx_ref[...] = jnp.where(_valid(i * tr, tr), acc, 0.0)
        db = jnp.sum(dy, axis=0, keepdims=True)

        @pl.when(i == 0)
        def _():
            dw_ref[...] = dw
            db_ref[...] = db

        @pl.when(i > 0)
        def _():
            dw_ref[...] += dw
            db_ref[...] += db

    row = pl.BlockSpec((tr, cw), lambda j, i: (i, j))
    return pl.pallas_call(
        body, name=name, grid=(c // cw, nb),
        in_specs=[pl.BlockSpec((tr, cw), lambda j, i: (i, c0 + j)),
                  pl.BlockSpec((HALO, cw), lambda j, i: (jnp.maximum(i * hb - 1, 0), c0 + j)),
                  pl.BlockSpec((CONV_K, cw), lambda j, i: (0, j)),
                  row, pl.BlockSpec((HALO, cw), lambda j, i: (jnp.minimum((i + 1) * hb, t // HALO - 1), j))],
        out_specs=[row, pl.BlockSpec((CONV_K, cw), lambda j, i: (0, j)), pl.BlockSpec((1, cw), lambda j, i: (0, j))],
        out_shape=[jax.ShapeDtypeStruct((t, c), F32), jax.ShapeDtypeStruct((CONV_K, c), F32), jax.ShapeDtypeStruct((1, c), F32)],
        scratch_shapes=[pltpu.VMEM((tr + HALO, cw), F32), pltpu.VMEM((tr + HALO, cw), F32)],
        compiler_params=_params(("parallel", "arbitrary")),
    )(x, x, w, dy, dy)


SUB = 8


def _lru_scan(a, u, name):
    t, c = a.shape
    tr = _tile(t, 528)

    def body(a_ref, u_ref, o_ref, carry):
        @pl.when(pl.program_id(0) == 0)
        def _():
            carry[...] = jnp.zeros_like(carry)

        rows = lax.broadcasted_iota(jnp.int32, (SUB, 1), 0)

        def step(k, cin):
            r = pl.multiple_of(k * SUB, SUB)
            av, uv = a_ref[pl.ds(r, SUB), :], u_ref[pl.ds(r, SUB), :]
            for d in (1, 2, 4):
                m = rows >= d
                uv = uv + av * jnp.where(m, pltpu.roll(uv, d, 0), 0.0)
                av = av * jnp.where(m, pltpu.roll(av, d, 0), 1.0)
            hv = uv + av * cin
            o_ref[pl.ds(r, SUB), :] = hv
            return jnp.broadcast_to(hv[SUB - 1:SUB, :], (SUB, c))

        carry[...] = lax.fori_loop(0, tr // SUB, step, carry[...])

    row = pl.BlockSpec((tr, c), lambda i: (i, 0))
    return pl.pallas_call(
        body, name=name, grid=(t // tr,), in_specs=[row, row], out_specs=row,
        out_shape=jax.ShapeDtypeStruct((t, c), F32), scratch_shapes=[pltpu.VMEM((SUB, c), F32)],
        compiler_params=_params(("arbitrary",)),
    )(a, u)


def _lru_scan_bwd(a, hs, dy, name):
    t, c = a.shape
    tr = _tile(t, 528)
    nb, nt = t // tr, tr // SUB

    def body(a_ref, h_ref, hh_ref, dy_ref, du_ref, da_ref, gcar, acar):
        i = pl.program_id(0)

        @pl.when(i == 0)
        def _():
            gcar[...] = jnp.zeros_like(gcar)
            acar[...] = jnp.zeros_like(acar)

        rows = lax.broadcasted_iota(jnp.int32, (SUB, 1), 0)
        hhalo = jnp.where(i < nb - 1, hh_ref[...], 0.0)

        def step(kk, car):
            gin, a_next_first = car
            k = nt - 1 - kk
            r = pl.multiple_of(k * SUB, SUB)
            av, hv, dv = a_ref[pl.ds(r, SUB), :], h_ref[pl.ds(r, SUB), :], dy_ref[pl.ds(r, SUB), :]
            rp = pl.multiple_of(jnp.maximum(k - 1, 0) * SUB, SUB)
            hp = jnp.where(k > 0, h_ref[pl.ds(rp, SUB), :], hhalo)
            cv = jnp.where(rows < SUB - 1, pltpu.roll(av, SUB - 1, 0), a_next_first)
            gv = dv
            for d in (1, 2, 4):
                m = rows < SUB - d
                gv = gv + cv * jnp.where(m, pltpu.roll(gv, SUB - d, 0), 0.0)
                cv = cv * jnp.where(m, pltpu.roll(cv, SUB - d, 0), 1.0)
            gv = gv + cv * gin
            hprev = jnp.where(rows >= 1, pltpu.roll(hv, 1, 0), jnp.broadcast_to(hp[SUB - 1:SUB, :], (SUB, c)))
            du_ref[pl.ds(r, SUB), :] = gv
            da_ref[pl.ds(r, SUB), :] = gv * hprev
            return jnp.broadcast_to(gv[0:1, :], (SUB, c)), jnp.broadcast_to(av[0:1, :], (SUB, c))

        g, af = lax.fori_loop(0, nt, step, (gcar[...], acar[...]))
        gcar[...] = g
        acar[...] = af

    hb = tr // SUB
    row = pl.BlockSpec((tr, c), lambda i: (nb - 1 - i, 0))
    halo = pl.BlockSpec((SUB, c), lambda i: (jnp.maximum((nb - 1 - i) * hb - 1, 0), 0))
    return pl.pallas_call(
        body, name=name, grid=(nb,), in_specs=[row, row, halo, row], out_specs=[row, row],
        out_shape=[jax.ShapeDtypeStruct((t, c), F32)] * 2,
        scratch_shapes=[pltpu.VMEM((SUB, c), F32), pltpu.VMEM((SUB, c), F32)],
        compiler_params=_params(("arbitrary",)),
    )(a, hs, hs, dy)


def _lru_gates(row0, xr, wa, ba, wi, bi, lam):
    r = jax.nn.sigmoid(_bnn(xr, wa) + ba)
    i = jax.nn.sigmoid(_bnn(xr, wi) + bi)
    log_a = -LRU_C * r * _softplus(-lam)
    u = jnp.sqrt(_neg_expm1(2.0 * log_a)) * (i * xr)
    return jnp.exp(log_a), jnp.where(_valid(row0, xr.shape[0]), u, 0.0)


def _lru_gate_out(row0, hs, yw):
    return (hs * jax.nn.gelu(yw),)


def _rglru_fwd(h, p, l, o):
    hn = _prenorm(h, p["mix_pre_g"][l], "rg_prenorm")
    xw = _mm(hn, p["rg_w_x"][o], "nn", "rg_in_x")
    yw = _mm(hn, p["rg_w_y"][o], "nn", "rg_in_y")
    xr = _conv_fwd(xw, p["rg_conv_w"][o], p["rg_conv_b"][o], "rg_conv", cw=LRU_WIDTH // 2)
    gp = [p["rg_w_a"][o], p["rg_b_a"][o], p["rg_w_i"][o], p["rg_b_i"][o], p["rg_lambda"][o]]
    a, u = _rowwise("rg_gates", _lru_gates, [xr], gp, [(LRU_WIDTH, F32)] * 2, ncol=LRU_BLOCKS)
    hs = _lru_scan(a, u, "rg_scan")
    hg = _rowwise("rg_gate_out", _lru_gate_out, [hs, yw], [], [(LRU_WIDTH, BF16)])[0]
    m = _mm(hg, p["rg_w_out"][o], "nn", "rg_out")
    h2 = _add_postnorm(h, [m], p["mix_post_g"][l], "rg_postnorm")
    return h2, (h, hn, xw, yw, xr, a, hs, hg, m)


def _rglru_bwd(dh, saved, p, l, o, grads):
    h, hn, xw, yw, xr, a, hs, hg, m = saved
    dm, grads["mix_post_g"][l] = _postnorm_bwd(m, p["mix_post_g"][l], dh, "rg_postnorm_bwd")
    grads["rg_w_out"][o] = _mm(hg, dm, "tn", "rg_out_dw")
    dhg = _mm(dm, p["rg_w_out"][o], "nt", "rg_out_dx")
    (dhs, dyw), _ = _rowwise_vjp("rg_gate_out_bwd", _lru_gate_out, [hs, yw], [], [dhg])
    du, da = _lru_scan_bwd(a, hs, dhs, "rg_scan_bwd")
    gp = [p["rg_w_a"][o], p["rg_b_a"][o], p["rg_w_i"][o], p["rg_b_i"][o], p["rg_lambda"][o]]
    (dxr,), gg = _rowwise_vjp("rg_gates_bwd", _lru_gates, [xr], gp, [da, du], ncol=LRU_BLOCKS)
    grads["rg_w_a"][o], grads["rg_b_a"][o], grads["rg_w_i"][o], grads["rg_b_i"][o], grads["rg_lambda"][o] = gg
    dxw, grads["rg_conv_w"][o], grads["rg_conv_b"][o] = _conv_bwd(xw, p["rg_conv_w"][o], dxr, "rg_conv_bwd", cw=LRU_WIDTH // 2)
    grads["rg_w_x"][o] = _mm(hn, dxw, "tn", "rg_in_x_dw")
    grads["rg_w_y"][o] = _mm(hn, dyw, "tn", "rg_in_y_dw")
    dhx = _mm(dxw, p["rg_w_x"][o], "nt", "rg_in_x_dx")
    dhy = _mm(dyw, p["rg_w_y"][o], "nt", "rg_in_y_dx")
    dh, grads["mix_pre_g"][l] = _prenorm_bwd_add(h, p["mix_pre_g"][l], [dhx, dhy], dh, "rg_prenorm_bwd")
    return dh


SSD_GW = SSD_D_INNER // SSD_GROUPS
SSD_GH = SSD_HEADS // SSD_GROUPS
XACT_B = SSD_D_INNER // SSD_STATE
XACT_C = XACT_B + SSD_GROUPS


def _hp(a, b, dims=_DIMS["nn"]):
    return lax.dot_general(a, b, dims, precision=lax.Precision.HIGHEST, preferred_element_type=F32)


def _ssd_chunk(xs, bm, cm, dt, da, ht, g):
    l = CHUNK
    ri = lax.broadcasted_iota(jnp.int32, (l, l), 0)
    ci = lax.broadcasted_iota(jnp.int32, (l, l), 1)
    causal = ri >= ci
    tri = causal.astype(F32)
    hr = lax.broadcasted_iota(jnp.int32, (LANES, SSD_GW), 0)
    hc = lax.broadcasted_iota(jnp.int32, (LANES, SSD_GW), 1)
    expand = (hr == g * SSD_GH + hc // SSD_HEAD_DIM).astype(F32)
    acs = _hp(tri, da)
    acs_t = _hp(da, tri, (((0,), (1,)), ((), ())))
    acs_e = _hp(acs, expand)
    x = xs * _hp(dt, expand)
    gmat = _bnt(cm, bm)
    lane = lax.broadcasted_iota(jnp.int32, (1, LANES), 1)
    sub = lax.broadcasted_iota(jnp.int32, (LANES, 1), 0)
    colhead = lax.broadcasted_iota(jnp.int32, (1, SSD_GW), 1) // SSD_HEAD_DIM
    y = _bnn(cm, ht) * jnp.exp(acs_e)
    for k in range(SSD_GH):
        hh = g * SSD_GH + k
        col = jnp.sum(jnp.where(lane == hh, acs, 0.0), axis=1, keepdims=True)
        row = jnp.sum(jnp.where(sub == hh, acs_t, 0.0), axis=0, keepdims=True)
        decay = jnp.exp(jnp.where(causal, col - row, -1e30))
        y = y + _bnn(gmat * decay, jnp.where(colhead == k, x, 0.0))
    last = lax.broadcasted_iota(jnp.int32, (l, 1), 0) == l - 1
    a_last = jnp.sum(jnp.where(last, acs_e, 0.0), axis=0, keepdims=True)
    st = _btn(bm, x * jnp.exp(a_last - acs_e))
    return y, ht * jnp.exp(a_last) + st


def _ssd_specs(nc, rev):
    def cc(c):
        return nc - 1 - c if rev else c

    return [pl.BlockSpec((CHUNK, SSD_GW), lambda c, g: (cc(c), g)),
            pl.BlockSpec((CHUNK, SSD_STATE), lambda c, g: (cc(c), XACT_B + g)),
            pl.BlockSpec((CHUNK, SSD_STATE), lambda c, g: (cc(c), XACT_C + g)),
            pl.BlockSpec((CHUNK, LANES), lambda c, g: (cc(c), 0)),
            pl.BlockSpec((CHUNK, LANES), lambda c, g: (cc(c), 0))]


def _ssd_scan(xact, dt, da, name):
    t = xact.shape[0]
    nc = t // CHUNK

    def body(xs_ref, b_ref, c_ref, dt_ref, da_ref, y_ref, hs_ref, state):
        c, g = pl.program_id(0), pl.program_id(1)

        @pl.when(c == 0)
        def _():
            state[g] = jnp.zeros((SSD_STATE, SSD_GW), F32)

        ht = state[g]
        hs_ref[0] = ht
        y, ht2 = _ssd_chunk(xs_ref[...], b_ref[...], c_ref[...], dt_ref[...], da_ref[...], ht, g)
        y_ref[...] = y
        state[g] = ht2

    return pl.pallas_call(
        body, name=name, grid=(nc, SSD_GROUPS), in_specs=_ssd_specs(nc, False),
        out_specs=[pl.BlockSpec((CHUNK, SSD_GW), lambda c, g: (c, g)),
                   pl.BlockSpec((1, SSD_STATE, SSD_GW), lambda c, g: (c * SSD_GROUPS + g, 0, 0))],
        out_shape=[jax.ShapeDtypeStruct((t, SSD_D_INNER), F32), jax.ShapeDtypeStruct((nc * SSD_GROUPS, SSD_STATE, SSD_GW), F32)],
        scratch_shapes=[pltpu.VMEM((SSD_GROUPS, SSD_STATE, SSD_GW), F32)],
        compiler_params=_params(("arbitrary", "arbitrary")),
    )(xact, xact, xact, dt, da)


def _ssd_scan_bwd(xact, dt, da, hsave, dy, dxskip, name):
    t = xact.shape[0]
    nc = t // CHUNK

    def body(xs_ref, b_ref, c_ref, dt_ref, da_ref, hs_ref, dy_ref, sk_ref, dxs_ref, db_ref, dc_ref, ddt_ref, dda_ref, dstate):
        c, g = pl.program_id(0), pl.program_id(1)

        @pl.when(c == 0)
        def _():
            dstate[g] = jnp.zeros((SSD_STATE, SSD_GW), F32)

        _, vjp = jax.vjp(lambda *a: _ssd_chunk(*a, g), xs_ref[...], b_ref[...], c_ref[...], dt_ref[...], da_ref[...], hs_ref[0])
        dxs, dbm, dcm, ddt, dda, dht = vjp((dy_ref[...], dstate[g]))
        dxs_ref[...] = dxs + sk_ref[...]
        db_ref[...] = dbm
        dc_ref[...] = dcm
        dstate[g] = dht

        @pl.when(g == 0)
        def _():
            ddt_ref[...] = ddt
            dda_ref[...] = dda

        @pl.when(g > 0)
        def _():
            ddt_ref[...] += ddt
            dda_ref[...] += dda

    grp = pl.BlockSpec((CHUNK, SSD_GW), lambda c, g: (nc - 1 - c, g))
    st = pl.BlockSpec((CHUNK, SSD_STATE), lambda c, g: (nc - 1 - c, g))
    hd = pl.BlockSpec((CHUNK, LANES), lambda c, g: (nc - 1 - c, 0))
    return pl.pallas_call(
        body, name=name, grid=(nc, SSD_GROUPS),
        in_specs=_ssd_specs(nc, True) + [pl.BlockSpec((1, SSD_STATE, SSD_GW), lambda c, g: ((nc - 1 - c) * SSD_GROUPS + g, 0, 0)), grp, grp],
        out_specs=[grp, st, st, hd, hd],
        out_shape=[jax.ShapeDtypeStruct((t, SSD_D_INNER), F32), jax.ShapeDtypeStruct((t, SSD_GROUPS * SSD_STATE), F32),
                   jax.ShapeDtypeStruct((t, SSD_GROUPS * SSD_STATE), F32), jax.ShapeDtypeStruct((t, LANES), F32),
                   jax.ShapeDtypeStruct((t, LANES), F32)],
        scratch_shapes=[pltpu.VMEM((SSD_GROUPS, SSD_STATE, SSD_GW), F32)],
        compiler_params=_params(("arbitrary", "arbitrary")),
    )(xact, xact, xact, dt, da, hsave, dy, dxskip)


def _ssd_act(row0, xc):
    return (jnp.where(_valid(row0, xc.shape[0]), jax.nn.silu(xc), 0.0),)


def _ssd_dt(row0, dtraw, dt_bias, a_log):
    dt = jnp.where(_valid(row0, dtraw.shape[0]), _softplus(dtraw + dt_bias), 0.0)
    return dt, dt * -jnp.exp(a_log)


def _ssd_post(row0, y, xs, z, d_skip, norm_g):
    hr = lax.broadcasted_iota(jnp.int32, (LANES, SSD_D_INNER), 0)
    hc = lax.broadcasted_iota(jnp.int32, (LANES, SSD_D_INNER), 1)
    expand = (hr == hc // SSD_HEAD_DIM).astype(F32)
    d_e = jnp.sum(_hp(jnp.broadcast_to(d_skip, (SUB, LANES)), expand), axis=0, keepdims=True) * (1.0 / SUB)
    return (_rms((y + xs * d_e) * jax.nn.silu(z), norm_g),)


ROPE_LO, ROPE_MID, ROPE_HI = MLA_NOPE, MLA_NOPE + MLA_ROPE // 2, MLA_NOPE + MLA_ROPE
ATT_SCALE = (MLA_NOPE + MLA_ROPE) ** -0.5


def _slot_lane(width):
    return lax.broadcasted_iota(jnp.int32, (1, width), 1) % LANES


def _swap_halves(x):
    width = x.shape[1]
    lane = _slot_lane(width)
    sw = jnp.where(lane < ROPE_MID, pltpu.roll(x, width - MLA_ROPE // 2, 1), pltpu.roll(x, MLA_ROPE // 2, 1))
    return jnp.where((lane >= ROPE_LO) & (lane < ROPE_HI), sw, 0.0)


def _rope(x, cos, sin):
    n = x.shape[1] // LANES
    return x * jnp.tile(cos, (1, n)) + _swap_halves(x) * jnp.tile(sin, (1, n))


def _rope_t(dy, cos, sin):
    n = dy.shape[1] // LANES
    return dy * jnp.tile(cos, (1, n)) + _swap_halves(dy * jnp.tile(sin, (1, n)))


def _att_mask(i, j, blk):
    rowid = i * blk + lax.broadcasted_iota(jnp.int32, (blk, 1), 0)
    colid = j * blk + lax.broadcasted_iota(jnp.int32, (1, blk), 1)
    return (colid <= rowid) & (colid >= PAD)


def _key_slots(row0, kv, kr):
    width = kv.shape[1]
    return jnp.where(_slot_lane(width) < MLA_NOPE, kv, jnp.tile(kr, (1, width // LANES))), kv


def _attn_fwd(qr, km, vb, name):
    t = qr.shape[0]
    blk = _tile(t, 384, LANES)
    nq = t // blk

    def body(q_ref, k_ref, v_ref, o_ref):
        i = pl.program_id(1)
        lane = lax.broadcasted_iota(jnp.int32, (1, LANES), 1)
        qb = q_ref[...]

        def step(j, car, masked):
            m, l, acc = car
            rows = pl.ds(pl.multiple_of(j * blk, blk), blk)
            s = lax.dot_general(qb, k_ref[rows, :], _DIMS["nt"], preferred_element_type=F32) * ATT_SCALE
            if masked:
                s = jnp.where(_att_mask(i, j, blk), s, -1e30)
            m2 = jnp.maximum(m, jnp.max(s, axis=1, keepdims=True))
            al = jnp.exp(m - m2)
            pm = jnp.exp(s - m2)
            l2 = al * l + jnp.sum(pm, axis=1, keepdims=True)
            acc2 = al * acc + lax.dot_general(pm.astype(BF16), v_ref[rows, :], _DIMS["nn"], preferred_element_type=F32)
            return m2, l2, acc2

        car = (jnp.full((blk, 1), -1e30, F32), jnp.zeros((blk, 1), F32), jnp.zeros((blk, LANES), F32))
        car = lax.fori_loop(0, jnp.where(i > 0, 2, 1), lambda k, c: step(k * i, c, True), car)
        m, l, acc = lax.fori_loop(1, i, lambda j, c: step(j, c, False), car)
        out = jnp.where(lane >= MLA_NOPE, acc / l, m + jnp.log(l))
        o_ref[...] = jnp.where(_valid(i * blk, blk), out, 0.0)

    seq_h = pl.BlockSpec((t, LANES), lambda h, i: (0, h))
    return pl.pallas_call(
        body, name=name, grid=(MLA_HEADS, nq),
        in_specs=[pl.BlockSpec((blk, LANES), lambda h, i: (i, h)), seq_h, seq_h],
        out_specs=pl.BlockSpec((blk, LANES), lambda h, i: (i, h)),
        out_shape=jax.ShapeDtypeStruct((t, MLA_HEADS * LANES), F32),
        compiler_params=_params(("parallel", "parallel")),
    )(qr, km, vb)


def _attn_bwd(qr, km, vb, o, do, name):
    t = qr.shape[0]
    blk = _tile(t, 384, LANES)
    nq = t // blk

    def body(q_ref, o_ref, do_ref, k_ref, v_ref, dq_ref, dkv_ref, dkr_ref):
        h, j = pl.program_id(0), pl.program_id(1)
        lane = lax.broadcasted_iota(jnp.int32, (1, LANES), 1)

        @pl.when(j == 0)
        def _():
            dq_ref[...] = jnp.zeros_like(dq_ref)

        @pl.when((h == 0) & (j == 0))
        def _():
            dkr_ref[...] = jnp.zeros_like(dkr_ref)

        kmat, vmat = k_ref[...], v_ref[...]

        def step(ii, car, masked):
            dk, dv = car
            i = j + ii
            rows = pl.ds(pl.multiple_of(i * blk, blk), blk)
            qb = q_ref[rows, :]
            ob, dob = o_ref[rows, :], do_ref[rows, :]
            delta = jnp.sum(dob * ob, axis=1, keepdims=True)
            s = lax.dot_general(qb, kmat, _DIMS["nt"], preferred_element_type=F32) * ATT_SCALE
            if masked:
                s = jnp.where(_att_mask(i, j, blk), s, -1e30)
            pm = jnp.exp(s - ob[:, 0:1])
            dobb = dob.astype(BF16)
            dv = dv + lax.dot_general(pm.astype(BF16), dobb, _DIMS["tn"], preferred_element_type=F32)
            dp = lax.dot_general(dobb, vmat, _DIMS["nt"], preferred_element_type=F32)
            ds = (pm * (dp - delta) * ATT_SCALE).astype(BF16)
            dq_ref[rows, :] += lax.dot_general(ds, kmat, _DIMS["nn"], preferred_element_type=F32)
            dk = dk + lax.dot_general(ds, qb, _DIMS["tn"], preferred_element_type=F32)
            return dk, dv

        zero = jnp.zeros((blk, LANES), F32)
        n_masked = jnp.where(j == 0, nq, 1)
        car = lax.fori_loop(0, n_masked, lambda ii, c: step(ii, c, True), (zero, zero))
        dk, dv = lax.fori_loop(n_masked, nq - j, lambda ii, c: step(ii, c, False), car)
        dkv_ref[...] = jnp.where(lane < MLA_NOPE, dk, dv)
        rows = pl.ds(pl.multiple_of(j * blk, blk), blk)
        dkr_ref[rows, :] += jnp.where(lane >= MLA_NOPE, dk, 0.0)

    seq_h = pl.BlockSpec((t, LANES), lambda h, j: (0, h))
    blk_h = pl.BlockSpec((blk, LANES), lambda h, j: (j, h))
    return pl.pallas_call(
        body, name=name, grid=(MLA_HEADS, nq),
        in_specs=[seq_h, seq_h, seq_h, blk_h, blk_h],
        out_specs=[seq_h, blk_h, pl.BlockSpec((t, LANES), lambda h, j: (0, 0))],
        out_shape=[jax.ShapeDtypeStruct((t, MLA_HEADS * LANES), F32), jax.ShapeDtypeStruct((t, MLA_HEADS * LANES), F32),
                   jax.ShapeDtypeStruct((t, LANES), F32)],
        compiler_params=_params(("arbitrary", "arbitrary")),
    )(qr, o, do, km, vb)


def _rms_rows(row0, x, g):
    return (_rms(x, g),)


def _ssdmla_fwd(h, p, l, e, cos, sin):
    hn = _prenorm(h, p["mix_pre_g"][l], "sm_prenorm")
    proj = _mm(hn, p["w_in"][e], "nn", "sm_in")
    xc = _conv_fwd(proj, p["ssd_conv_w"][e], p["ssd_conv_b"][e], "ssd_conv", cw=SSD_GW, c0=PROJ_XBC // SSD_GW)
    xact = _rowwise("ssd_act", _ssd_act, [xc], [], [(SSD_CONV_CH, F32)])[0]
    dt, da = _rowwise("ssd_dt", _ssd_dt, [(proj, LANES, PROJ_DT // LANES)], [p["ssd_dt_bias"][e], p["ssd_a_log"][e]],
                      [(LANES, F32)] * 2)
    y, hsave = _ssd_scan(xact, dt, da, "ssd_scan")
    y_ssd = _rowwise("ssd_post", _ssd_post, [y, (xact, SSD_D_INNER, 0), (proj, SSD_D_INNER, 0)],
                     [p["ssd_d"][e], p["ssd_norm_g"][e]], [(SSD_D_INNER, BF16)])[0]
    cqn = _prenorm((proj, MLA_Q_RANK, PROJ_CQ // MLA_Q_RANK), p["mla_q_norm_g"][e], "mla_qnorm")
    ckvn = _prenorm((proj, MLA_KV_RANK, PROJ_CKV // MLA_KV_RANK), p["mla_kv_norm_g"][e], "mla_kvnorm")
    q = _mm(cqn, p["mla_w_q_up"][e], "nn", "mla_q_up")
    kv = _mm(ckvn, p["mla_w_kv_up"][e], "nn", "mla_kv_up")
    kr = _rowwise("mla_krope", lambda r0, x, c, s: (_rope(x, c, s),), [(proj, LANES, PROJ_KR // LANES), cos, sin], [],
                  [(LANES, F32)])[0]
    slots, tr = MLA_HEADS * LANES, _tile(h.shape[0], 264, 16)
    qr = _rowwise("mla_q_rope", lambda r0, a, c, s: (_rope(a, c, s),), [q, cos, sin], [], [(slots, BF16)], tr=tr)[0]
    km, vb = _rowwise("mla_key_slots", _key_slots, [kv, kr], [], [(slots, BF16)] * 2, tr=tr)
    o = _attn_fwd(qr, km, vb, "mla_attn")
    m1 = _mm(y_ssd, p["w_out_ssd"][e], "nn", "sm_out_ssd")
    m = _mm(o, p["w_out_att"][e], "nn", "sm_out_att", add=m1)
    h2 = _add_postnorm(h, [m], p["mix_post_g"][l], "sm_postnorm")
    return h2, (h, hn, proj, xc, xact, dt, da, y, hsave, y_ssd, cqn, ckvn, qr, km, vb, o, m)


def _ssdmla_bwd(dh, saved, p, l, e, cos, sin, grads):
    h, hn, proj, xc, xact, dt, da, y, hsave, y_ssd, cqn, ckvn, qr, km, vb, o, m = saved
    dm, grads["mix_post_g"][l] = _postnorm_bwd(m, p["mix_post_g"][l], dh, "sm_postnorm_bwd")
    grads["w_out_ssd"][e] = _mm(y_ssd, dm, "tn", "sm_out_ssd_dw")
    grads["w_out_att"][e] = _mm(o, dm, "tn", "sm_out_att_dw")
    dy_ssd = _mm(dm, p["w_out_ssd"][e], "nt", "sm_out_ssd_dx")
    do = _mm(dm, p["w_out_att"][e], "nt", "sm_out_att_dx")
    dqr, dkv, dkr = _attn_bwd(qr, km, vb, o, do, "mla_attn_bwd")
    dq = _rowwise("mla_q_rope_bwd", lambda r0, a, c, s: (_rope_t(a, c, s),), [dqr, cos, sin], [], [(MLA_HEADS * LANES, F32)],
                  tr=_tile(h.shape[0], 264, 16))[0]
    dkr_raw = _rowwise("mla_krope_bwd", lambda r0, d, c, s: (_rope_t(d, c, s),), [dkr, cos, sin], [], [(LANES, F32)])[0]
    grads["mla_w_q_up"][e] = _mm(cqn, dq, "tn", "mla_q_up_dw")
    dcqn = _mm(dq, p["mla_w_q_up"][e], "nt", "mla_q_up_dx")
    (dcq,), (grads["mla_q_norm_g"][e],) = _rowwise_vjp(
        "mla_qnorm_bwd", _rms_rows, [(proj, MLA_Q_RANK, PROJ_CQ // MLA_Q_RANK)], [p["mla_q_norm_g"][e]], [dcqn])
    grads["mla_w_kv_up"][e] = _mm(ckvn, dkv, "tn", "mla_kv_up_dw")
    dckvn = _mm(dkv, p["mla_w_kv_up"][e], "nt", "mla_kv_up_dx")
    (dckv,), (grads["mla_kv_norm_g"][e],) = _rowwise_vjp(
        "mla_kvnorm_bwd", _rms_rows, [(proj, MLA_KV_RANK, PROJ_CKV // MLA_KV_RANK)], [p["mla_kv_norm_g"][e]], [dckvn])
    (dy, dxskip, dz), (grads["ssd_d"][e], grads["ssd_norm_g"][e]) = _rowwise_vjp(
        "ssd_post_bwd", _ssd_post, [y, (xact, SSD_D_INNER, 0), (proj, SSD_D_INNER, 0)], [p["ssd_d"][e], p["ssd_norm_g"][e]], [dy_ssd])
    dxs, db, dc, ddt, dda = _ssd_scan_bwd(xact, dt, da, hsave, dy, dxskip, "ssd_scan_bwd")
    dxact = jnp.concatenate([dxs, db, dc], axis=1)
    (dxc,), _ = _rowwise_vjp("ssd_act_bwd", _ssd_act, [xc], [], [dxact])
    dxbc, grads["ssd_conv_w"][e], grads["ssd_conv_b"][e] = _conv_bwd(
        proj, p["ssd_conv_w"][e], dxc, "ssd_conv_bwd", cw=SSD_GW, c0=PROJ_XBC // SSD_GW)
    (ddtraw,), (grads["ssd_dt_bias"][e], grads["ssd_a_log"][e]) = _rowwise_vjp(
        "ssd_dt_bwd", _ssd_dt, [(proj, LANES, PROJ_DT // LANES)], [p["ssd_dt_bias"][e], p["ssd_a_log"][e]], [ddt, dda])
    dproj = jnp.concatenate([dz, dxbc, ddtraw, dcq, dckv, dkr_raw], axis=1)
    grads["w_in"][e] = _mm(hn, dproj, "tn", "sm_in_dw")
    dhn = _mm(dproj, p["w_in"][e], "nt", "sm_in_dx")
    dh, grads["mix_pre_g"][l] = _prenorm_bwd_add(h, p["mix_pre_g"][l], [dhn], dh, "sm_prenorm_bwd")
    return dh


GAINS = ("mix_pre_g", "mix_post_g", "mlp_pre_g", "mlp_post_g", "ssd_norm_g", "mla_q_norm_g", "mla_kv_norm_g", "ssd_conv_b", "rg_conv_b")
HEAD_VECS = ("ssd_dt_bias", "ssd_a_log", "ssd_d")
LRU_VECS = ("rg_b_a", "rg_b_i", "rg_lambda")
IN_DT_END = SSD_D_INNER + SSD_CONV_CH + SSD_HEADS
IN_KR = IN_DT_END + MLA_Q_RANK + MLA_KV_RANK


def _layout_params(w):
    p = {k: w[k][:, None, :] for k in GAINS}
    for k in HEAD_VECS:
        p[k] = jnp.pad(w[k], ((0, 0), (0, LANES - SSD_HEADS)))[:, None, :]
    for k in LRU_VECS:
        p[k] = w[k].reshape(-1, LRU_BLOCKS, 1, LRU_BLOCK)
    for k in ("w_up", "w_down", "mla_w_kv_up", "rg_w_x", "rg_w_y", "rg_w_out"):
        p[k] = w[k] if isinstance(w[k], list) else w[k].astype(BF16)
    for k in ("ssd_conv_w", "rg_conv_w", "rg_w_a", "rg_w_i"):
        p[k] = w[k]
    wi = w["w_in"]

    def zcols(n):
        return jnp.zeros(wi.shape[:2] + (n,), wi.dtype)

    p["w_in"] = jnp.concatenate([wi[..., :IN_DT_END], zcols(PROJ_CQ - IN_DT_END), wi[..., IN_DT_END:IN_KR], zcols(ROPE_LO),
                                 wi[..., IN_KR:], zcols(LANES - ROPE_HI)], axis=-1).astype(BF16)
    wq = w["mla_w_q_up"].reshape(-1, MLA_Q_RANK, MLA_HEADS, MLA_NOPE + MLA_ROPE)
    p["mla_w_q_up"] = jnp.pad(wq, ((0, 0), (0, 0), (0, 0), (0, LANES - MLA_NOPE - MLA_ROPE))).reshape(-1, MLA_Q_RANK, MLA_HEADS * LANES).astype(BF16)
    wo = w["w_out_ab"]
    p["w_out_ssd"] = wo[:, :SSD_D_INNER].astype(BF16)
    wa = wo[:, SSD_D_INNER:].reshape(-1, MLA_HEADS, MLA_V, D_MODEL)
    p["w_out_att"] = jnp.pad(wa, ((0, 0), (0, 0), (LANES - MLA_V, 0), (0, 0))).reshape(-1, MLA_HEADS * LANES, D_MODEL).astype(BF16)
    return p


def _rope_tables(t):
    pos = (jnp.arange(t) - PAD).astype(F32)
    inv = ROPE_BASE ** (-jnp.arange(0, MLA_ROPE, 2, dtype=F32) / MLA_ROPE)
    ang = pos[:, None] * inv[None, :]
    c, s = jnp.cos(ang), jnp.sin(ang)
    one, zero = jnp.ones((t, MLA_NOPE), F32), jnp.zeros((t, MLA_NOPE), F32)
    tail = LANES - ROPE_HI
    return (jnp.concatenate([one, c, c, one[:, :tail]], axis=1), jnp.concatenate([zero, -s, s, zero[:, :tail]], axis=1))


GRAD_KEYS = GAINS + HEAD_VECS + LRU_VECS + ("w_up", "w_down", "mla_w_kv_up", "rg_w_x", "rg_w_y", "rg_w_out", "ssd_conv_w",
                                            "rg_conv_w", "rg_w_a", "rg_w_i", "w_in", "mla_w_q_up", "w_out_ssd", "w_out_att")


def _device_step(x, meta, target, p):
    t = PAD + N_META + x.shape[0]
    cos, sin = _rope_tables(t)
    h = jnp.concatenate([jnp.zeros((PAD, D_MODEL), F32), meta, x], axis=0)
    n_even, n_odd = (DEPTH + 1) // 2, DEPTH // 2
    saved = []
    for l in range(DEPTH):
        if l % 2 == 0:
            h, sm = _ssdmla_fwd(h, p, l, l // 2, cos, sin)
        else:
            h, sm = _rglru_fwd(h, p, l, l // 2)
        h, sp = _mlp_fwd(h, p, l)
        saved.append((sm, sp))
    sq, dh = _loss_and_grad(h, target, "loss")
    per_layer = {"mix_pre_g": DEPTH, "mix_post_g": DEPTH, "mlp_pre_g": DEPTH, "mlp_post_g": DEPTH, "w_up": DEPTH, "w_down": DEPTH}
    grads = {k: [None] * per_layer.get(k, n_odd if k.startswith("rg_") else n_even) for k in GRAD_KEYS}
    for l in reversed(range(DEPTH)):
        sm, sp = saved[l]
        dh = _mlp_bwd(dh, sp, p, l, grads)
        if l % 2 == 0:
            dh = _ssdmla_bwd(dh, sm, p, l, l // 2, cos, sin, grads)
        else:
            dh = _rglru_bwd(dh, sm, p, l, l // 2, grads)
    return sq, dh, grads


MESH = pl.DeviceIdType.MESH
ANY = pl.BlockSpec(memory_space=pl.ANY)


def _mesh_pos():
    return lax.axis_index("x"), lax.axis_index("y"), lax.axis_index("c")


def _other_chips(x, y):
    return [(1 - x, y), (x, 1 - y), (1 - x, 1 - y)]


def _remote(src, dst, send_sems, recv_sems, k, to):
    return pltpu.make_async_remote_copy(src_ref=src, dst_ref=dst, send_sem=send_sems.at[k], recv_sem=recv_sems.at[k],
                                        device_id=to, device_id_type=MESH)


def _gather_chips(srcs, name):
    n = len(srcs)
    nc = N_CHIPS - 1

    def body(*refs):
        src_refs, out_refs = refs[:n], refs[n:2 * n]
        send_sems, recv_sems, local_sems, own_sems = refs[2 * n:]
        x, y, c = _mesh_pos()
        sib = (x, y, 1 - c)
        me = 2 * x + y
        chips = _other_chips(x, y)

        def half(ref, hc):
            h = ref.shape[0] // 2
            return ref.at[pl.ds(hc * h, h)]

        mine = [_remote(s, o.at[me], local_sems, own_sems, t, sib) for t, (s, o) in enumerate(zip(src_refs, out_refs))]
        for cp in mine:
            cp.start()
        first = [_remote(half(s, c), half(o.at[me], c), send_sems, recv_sems, 2 * nc * t + j, (cx, cy, c))
                 for t, (s, o) in enumerate(zip(src_refs, out_refs)) for j, (cx, cy) in enumerate(chips)]
        for cp in first:
            cp.start()
        passed = []
        for t, (s, o) in enumerate(zip(src_refs, out_refs)):
            for j, (cx, cy) in enumerate(chips):
                slot = half(o.at[2 * cx + cy], c)
                _remote(half(s, c), slot, send_sems, recv_sems, 2 * nc * t + j, (cx, cy, c)).wait_recv()
                passed.append(_remote(slot, slot, send_sems, recv_sems, 2 * nc * t + nc + j, sib))
                passed[-1].start()
        for t, (s, o) in enumerate(zip(src_refs, out_refs)):
            for j, (cx, cy) in enumerate(chips):
                _remote(half(s, c), half(o.at[2 * cx + cy], 1 - c), send_sems, recv_sems, 2 * nc * t + nc + j, sib).wait_recv()
        for cp in mine:
            cp.wait_recv()
        for cp in first + passed + mine:
            cp.wait_send()

    return pl.pallas_call(
        body, name=name, in_specs=[ANY] * n, out_specs=[ANY] * n,
        out_shape=[jax.ShapeDtypeStruct((N_CHIPS,) + s.shape, s.dtype) for s in srcs],
        scratch_shapes=[pltpu.SemaphoreType.DMA((2 * nc * n,)), pltpu.SemaphoreType.DMA((2 * nc * n,)),
                        pltpu.SemaphoreType.DMA((n,)), pltpu.SemaphoreType.DMA((n,))],
    )(*srcs)


def _pair_exchange(gs, name):
    n = len(gs)

    def body(*refs):
        g_refs, o_refs, send_sems, recv_sems = refs[:n], refs[n:2 * n], refs[2 * n], refs[2 * n + 1]
        x, y, c = _mesh_pos()
        sib = (x, y, 1 - c)
        cps = [_remote(g.at[pl.ds(0, N_CHIPS), 1 - c], o, send_sems, recv_sems, t, sib) for t, (g, o) in enumerate(zip(g_refs, o_refs))]
        for cp in cps:
            cp.start()
        for cp in cps:
            cp.wait_recv()
        for cp in cps:
            cp.wait_send()

    return pl.pallas_call(
        body, name=name, in_specs=[ANY] * n, out_specs=[ANY] * n,
        out_shape=[jax.ShapeDtypeStruct((g.shape[0],) + g.shape[2:], g.dtype) for g in gs],
        scratch_shapes=[pltpu.SemaphoreType.DMA((n,)), pltpu.SemaphoreType.DMA((n,))],
    )(*gs)


def _chip_exchange(ps, groups, name):
    n, ng = len(ps), len(groups)
    nc = N_CHIPS - 1

    def body(*refs):
        p_refs, q_refs = refs[:n], refs[n:n + ng]
        send_sems, recv_sems = refs[n + ng:]
        x, y, c = _mesh_pos()
        chips = _other_chips(x, y)
        slots = [(t, q, li) for q, idx in zip(q_refs, groups) for li, t in enumerate(idx)]
        sends = [_remote(p_refs[t].at[2 * cx + cy], q.at[j, li], send_sems, recv_sems, nc * t + j, (cx, cy, c))
                 for t, q, li in slots for j, (cx, cy) in enumerate(chips)]
        for cp in sends:
            cp.start()
        for cp in sends:
            cp.wait_recv()
        for cp in sends:
            cp.wait_send()

    return pl.pallas_call(
        body, name=name, in_specs=[ANY] * n, out_specs=[ANY] * ng,
        out_shape=[jax.ShapeDtypeStruct((nc, len(idx)) + ps[idx[0]].shape[1:], ps[idx[0]].dtype) for idx in groups],
        scratch_shapes=[pltpu.SemaphoreType.DMA((nc * n,)), pltpu.SemaphoreType.DMA((nc * n,))],
    )(*ps)


def _pair_share(fs, name):
    n = len(fs)

    def body(*refs):
        o_refs = refs[n:2 * n]
        send_sems, recv_sems = refs[2 * n:]
        x, y, c = _mesh_pos()
        sib = (x, y, 1 - c)
        cps = [_remote(o.at[pl.ds(0, o.shape[0]), c], o.at[pl.ds(0, o.shape[0]), c], send_sems, recv_sems, t, sib)
               for t, o in enumerate(o_refs)]
        for cp in cps:
            cp.start()
        for t, o in enumerate(o_refs):
            _remote(o.at[pl.ds(0, o.shape[0]), c], o.at[pl.ds(0, o.shape[0]), 1 - c], send_sems, recv_sems, t, sib).wait_recv()
        for cp in cps:
            cp.wait_send()

    return pl.pallas_call(
        body, name=name, in_specs=[ANY] * n, out_specs=[ANY] * n, input_output_aliases={t: t for t in range(n)},
        out_shape=[jax.ShapeDtypeStruct(f.shape, f.dtype) for f in fs],
        scratch_shapes=[pltpu.SemaphoreType.DMA((n,)), pltpu.SemaphoreType.DMA((n,))],
    )(*fs)


SUM_BLOCK = 512 * 1024


def _sum_pair(g, ra, c, name):
    n, _, h, w = g.shape
    tr = _tile(h, max(16, SUM_BLOCK // w), 16)

    def body(c_ref, g_ref, r_ref, o_ref):
        o_ref[...] = (g_ref[0] + r_ref[...]).astype(o_ref.dtype)

    return pl.pallas_call(
        body, name=name,
        grid_spec=pltpu.PrefetchScalarGridSpec(
            num_scalar_prefetch=1, grid=(n, h // tr),
            in_specs=[pl.BlockSpec((1, 1, tr, w), lambda s, i, cr: (s, cr[0], i, 0)), pl.BlockSpec((1, tr, w), lambda s, i, cr: (s, i, 0))],
            out_specs=pl.BlockSpec((1, tr, w), lambda s, i, cr: (s, i, 0))),
        out_shape=jax.ShapeDtypeStruct((n, h, w), BF16),
        compiler_params=_params(("parallel", "parallel")),
    )(c.reshape(1).astype(jnp.int32), g, ra)


def _sum_chips(ps, q, me_c, name):
    nc, nl, h, w = q.shape
    tr = _tile(h, max(16, SUM_BLOCK // (w * nl)), 16)

    def body(mc_ref, *refs):
        q_ref, o_ref = refs[nl], refs[nl + 1]
        for l in range(nl):
            acc = refs[l][0].astype(F32)
            for j in range(nc):
                acc = acc + q_ref[j, l].astype(F32)
            o_ref[l] = acc

    return pl.pallas_call(
        body, name=name,
        grid_spec=pltpu.PrefetchScalarGridSpec(
            num_scalar_prefetch=1, grid=(h // tr,),
            in_specs=[pl.BlockSpec((1, tr, w), lambda i, mc: (mc[0], i, 0))] * nl + [pl.BlockSpec((nc, nl, tr, w), lambda i, mc: (0, 0, i, 0))],
            out_specs=pl.BlockSpec((nl, None, tr, w), lambda i, mc: (0, mc[1], i, 0))),
        out_shape=jax.ShapeDtypeStruct((nl, 2, h, w), F32),
        compiler_params=_params(("parallel",)),
    )(me_c, *ps, q)


def _adamw(g, w, m, v, name):
    def f(r0, gg, ww, mm, vv):
        m2 = ADAM_B1 * mm + (1.0 - ADAM_B1) * gg
        v2 = ADAM_B2 * vv + (1.0 - ADAM_B2) * jnp.square(gg)
        m_hat = m2 / (1.0 - ADAM_B1 ** ADAM_STEP)
        v_hat = v2 / (1.0 - ADAM_B2 ** ADAM_STEP)
        return gg, -ADAM_LR * (m_hat / (jnp.sqrt(v_hat) + ADAM_EPS) + ADAM_WD * ww), m2, v2

    return _rowwise(name, f, [g, w, m, v], [], [(g.shape[1], F32)] * 4, tr=_tile(g.shape[0], 512))


WEIGHTS = (
    ("meta_tokens", (N_META, D_MODEL), 1), ("mix_pre_g", (DEPTH, D_MODEL), None), ("mix_post_g", (DEPTH, D_MODEL), None),
    ("mlp_pre_g", (DEPTH, D_MODEL), None), ("mlp_post_g", (DEPTH, D_MODEL), None), ("w_up", (DEPTH, D_MODEL, D_FF), 2),
    ("w_down", (DEPTH, D_FF, D_MODEL), 1), ("w_in", (2, D_MODEL, 3248), 2), ("ssd_conv_w", (2, CONV_K, SSD_CONV_CH), 2),
    ("ssd_conv_b", (2, SSD_CONV_CH), None), ("ssd_dt_bias", (2, SSD_HEADS), None), ("ssd_a_log", (2, SSD_HEADS), None),
    ("ssd_d", (2, SSD_HEADS), None), ("ssd_norm_g", (2, SSD_D_INNER), None), ("mla_q_norm_g", (2, MLA_Q_RANK), None),
    ("mla_w_q_up", (2, MLA_Q_RANK, MLA_HEADS * (MLA_NOPE + MLA_ROPE)), 2), ("mla_kv_norm_g", (2, MLA_KV_RANK), None),
    ("mla_w_kv_up", (2, MLA_KV_RANK, MLA_HEADS * (MLA_NOPE + MLA_V)), 2), ("w_out_ab", (2, SSD_D_INNER + MLA_HEADS * MLA_V, D_MODEL), 1),
    ("rg_w_x", (2, D_MODEL, LRU_WIDTH), 2), ("rg_w_y", (2, D_MODEL, LRU_WIDTH), 2), ("rg_conv_w", (2, CONV_K, LRU_WIDTH), 2),
    ("rg_conv_b", (2, LRU_WIDTH), 1), ("rg_w_a", (2, LRU_BLOCKS, LRU_BLOCK, LRU_BLOCK), None), ("rg_b_a", (2, LRU_WIDTH), 1),
    ("rg_w_i", (2, LRU_BLOCKS, LRU_BLOCK, LRU_BLOCK), None), ("rg_b_i", (2, LRU_WIDTH), 1), ("rg_lambda", (2, LRU_WIDTH), 1),
    ("rg_w_out", (2, LRU_WIDTH, D_MODEL), 1),
)
BIG = {"w_up": "col", "w_down": "row", "w_in": "col", "mla_w_q_up": "col", "mla_w_kv_up": "col", "w_out_ab": "row",
       "rg_w_x": "col", "rg_w_y": "col", "rg_w_out": "row"}
DIRECT = ("w_up", "w_down")
FLAT_QUANTUM = 2 * 16 * LANES
TABLE = {name: (shape, d) for name, shape, d in WEIGHTS}
SMALL_SHARDED = tuple(name for name, _, d in WEIGHTS if d is not None and name not in BIG)
REPLICATED = tuple(name for name, _, d in WEIGHTS if d is None)


def _chips_to_full(a, kind):
    if kind == "col":
        return jnp.moveaxis(a, 0, 2).reshape(a.shape[1], a.shape[2], -1)
    return jnp.moveaxis(a, 0, 1).reshape(a.shape[1], -1, a.shape[3])


def _full_to_chips(g, kind):
    if kind == "col":
        return jnp.moveaxis(g.reshape(g.shape[0], N_CHIPS, -1), 1, 0)
    return g.reshape(N_CHIPS, -1, g.shape[1])


def _chips_to_full_1(pc, kind):
    return jnp.moveaxis(pc, 0, 1).reshape(pc.shape[1], -1) if kind == "col" else pc.reshape(-1, pc.shape[2])


def _shard_shape(shape, d):
    return shape[:d] + (shape[d] // N_CHIPS,) + shape[d + 1:]


def _shard_major(full, d):
    s = full.shape
    return jnp.moveaxis(full.reshape(s[:d] + (N_CHIPS, s[d] // N_CHIPS) + s[d + 1:]), d, 0).reshape(N_CHIPS, -1)


def _from_shard_major(a, shape, d):
    ss = _shard_shape(shape, d)
    return jnp.moveaxis(a.reshape((N_CHIPS,) + ss), 0, d).reshape(shape)


def _pad_cols(a, quantum):
    n = a.shape[-1]
    return jnp.pad(a, [(0, 0)] * (a.ndim - 1) + [(0, -n % quantum)])


def _gather_weights(w):
    small = _pad_cols(jnp.concatenate([w[n].reshape(-1) for n in SMALL_SHARDED]), FLAT_QUANTUM).reshape(2, -1, LANES)
    outs = _gather_chips([w[n].astype(BF16) for n in BIG] + [small], "gather_weights")
    full = {n: w[n] for n in REPLICATED}
    for name, a in zip(BIG, outs):
        if name in DIRECT:
            full[name] = [Gathered(a, BIG[name], l) for l in range(a.shape[1])]
        else:
            full[name] = _chips_to_full(a, BIG[name])
    got, off = outs[-1].reshape(N_CHIPS, -1), 0
    for name in SMALL_SHARDED:
        shape, d = TABLE[name]
        n = int(np.prod(_shard_shape(shape, d)))
        full[name] = _from_shard_major(got[:, off:off + n], shape, d)
        off += n
    return full


def _big_pieces(g):
    def w_in(a):
        return jnp.concatenate([a[:, :IN_DT_END], a[:, PROJ_CQ:PROJ_KR], a[:, PROJ_KR + ROPE_LO:PROJ_KR + ROPE_HI]], axis=1)

    def q_up(a):
        return a.reshape(MLA_Q_RANK, MLA_HEADS, LANES)[:, :, :MLA_NOPE + MLA_ROPE].reshape(MLA_Q_RANK, -1)

    def out_ab(s, a):
        return jnp.concatenate([s, a.reshape(MLA_HEADS, LANES, D_MODEL)[:, LANES - MLA_V:, :].reshape(-1, D_MODEL)], axis=0)

    full = {"w_down": g["w_down"], "w_in": [w_in(a) for a in g["w_in"]], "mla_w_q_up": [q_up(a) for a in g["mla_w_q_up"]],
            "mla_w_kv_up": g["mla_w_kv_up"], "w_out_ab": [out_ab(s, a) for s, a in zip(g["w_out_ssd"], g["w_out_att"])],
            "rg_w_x": g["rg_w_x"], "rg_w_y": g["rg_w_y"], "rg_w_out": g["rg_w_out"]}
    return {name: (g[name] if name == "w_up" else [_full_to_chips(a, BIG[name]) for a in full[name]]) for name in BIG}


def _small_grads(g, dh):
    out = {k: jnp.stack(g[k])[:, 0, :] for k in GAINS}
    for k in HEAD_VECS:
        out[k] = jnp.stack(g[k])[:, 0, :SSD_HEADS]
    for k in LRU_VECS:
        out[k] = jnp.stack(g[k]).reshape(-1, LRU_WIDTH)
    for k in ("ssd_conv_w", "rg_conv_w", "rg_w_a", "rg_w_i"):
        out[k] = jnp.stack(g[k])
    out["meta_tokens"] = dh[PAD:PAD + N_META]
    return out


def _natural_grads(g, dh):
    out = _small_grads(g, dh)
    for name, pcs in _big_pieces(g).items():
        out[name] = jnp.stack([_chips_to_full_1(pc, BIG[name]) for pc in pcs])
    return out


def _reduce_grads(g, dh, c):
    big, small = _big_pieces(g), _small_grads(g, dh)
    pieces, groups = [], []
    for name in BIG:
        groups.append(list(range(len(pieces), len(pieces) + len(big[name]))))
        pieces += [pc.reshape(N_CHIPS, 2, pc.shape[1] // 2, pc.shape[2]) for pc in big[name]]
    sharded = jnp.concatenate([_shard_major(small[n], TABLE[n][1]) for n in SMALL_SHARDED], axis=1)
    rep = _pad_cols(jnp.concatenate([small[n].reshape(-1) for n in REPLICATED]), N_CHIPS * FLAT_QUANTUM)
    n_sh, n_rep = sharded.shape[1], rep.shape[0] // N_CHIPS
    flat = _pad_cols(jnp.concatenate([sharded, rep.reshape(N_CHIPS, n_rep)], axis=1), FLAT_QUANTUM)
    groups.append([len(pieces)])
    pieces.append(flat.reshape(N_CHIPS, 2, -1, LANES))
    ras = _pair_exchange(pieces, "grads_pair_exchange")
    ps = [_sum_pair(a, ra, c, "grads_pair_sum") for a, ra in zip(pieces, ras)]
    qs = _chip_exchange(ps, groups, "grads_chip_exchange")
    me_c = jnp.stack([2 * lax.axis_index("x") + lax.axis_index("y"), c]).astype(jnp.int32)
    outs = _pair_share([_sum_chips([ps[t] for t in idx], q, me_c, "grads_chip_sum") for idx, q in zip(groups, qs)], "grads_pair_share")
    out = {name: o.reshape(o.shape[0], -1, o.shape[3]) for name, o in zip(BIG, outs)}
    f = outs[-1].reshape(-1)
    rep_all = _gather_chips([f[n_sh:n_sh + n_rep].reshape(2, -1, LANES)], "grads_gather_replicated")[0].reshape(-1)
    off = 0
    for name in SMALL_SHARDED:
        ss = _shard_shape(*TABLE[name])
        n = int(np.prod(ss))
        out[name] = f[off:off + n].reshape(ss)
        off += n
    off = 0
    for name in REPLICATED:
        shape = TABLE[name][0]
        n = int(np.prod(shape))
        out[name] = rep_all[off:off + n].reshape(shape)
        off += n
    return out


def kernel(x, meta_tokens, mix_pre_g, mix_post_g, mlp_pre_g, mlp_post_g, w_up, w_down, w_in, ssd_conv_w, ssd_conv_b, ssd_dt_bias, ssd_a_log, ssd_d, ssd_norm_g, mla_q_norm_g, mla_w_q_up, mla_kv_norm_g, mla_w_kv_up, w_out_ab, rg_w_x, rg_w_y, rg_conv_w, rg_conv_b, rg_w_a, rg_b_a, rg_w_i, rg_b_i, rg_lambda, rg_w_out, loss_target, m_meta_tokens, m_mix_pre_g, m_mix_post_g, m_mlp_pre_g, m_mlp_post_g, m_w_up, m_w_down, m_w_in, m_ssd_conv_w, m_ssd_conv_b, m_ssd_dt_bias, m_ssd_a_log, m_ssd_d, m_ssd_norm_g, m_mla_q_norm_g, m_mla_w_q_up, m_mla_kv_norm_g, m_mla_w_kv_up, m_w_out_ab, m_rg_w_x, m_rg_w_y, m_rg_conv_w, m_rg_conv_b, m_rg_w_a, m_rg_b_a, m_rg_w_i, m_rg_b_i, m_rg_lambda, m_rg_w_out, v_meta_tokens, v_mix_pre_g, v_mix_post_g, v_mlp_pre_g, v_mlp_post_g, v_w_up, v_w_down, v_w_in, v_ssd_conv_w, v_ssd_conv_b, v_ssd_dt_bias, v_ssd_a_log, v_ssd_d, v_ssd_norm_g, v_mla_q_norm_g, v_mla_w_q_up, v_mla_kv_norm_g, v_mla_w_kv_up, v_w_out_ab, v_rg_w_x, v_rg_w_y, v_rg_conv_w, v_rg_conv_b, v_rg_w_a, v_rg_b_a, v_rg_w_i, v_rg_b_i, v_rg_lambda, v_rg_w_out):
    names = [n for n, _, _ in WEIGHTS]
    w = dict(zip(names, (meta_tokens, mix_pre_g, mix_post_g, mlp_pre_g, mlp_post_g, w_up, w_down, w_in, ssd_conv_w, ssd_conv_b, ssd_dt_bias, ssd_a_log, ssd_d, ssd_norm_g, mla_q_norm_g, mla_w_q_up, mla_kv_norm_g, mla_w_kv_up, w_out_ab, rg_w_x, rg_w_y, rg_conv_w, rg_conv_b, rg_w_a, rg_b_a, rg_w_i, rg_b_i, rg_lambda, rg_w_out)))
    m = dict(zip(names, (m_meta_tokens, m_mix_pre_g, m_mix_post_g, m_mlp_pre_g, m_mlp_post_g, m_w_up, m_w_down, m_w_in, m_ssd_conv_w, m_ssd_conv_b, m_ssd_dt_bias, m_ssd_a_log, m_ssd_d, m_ssd_norm_g, m_mla_q_norm_g, m_mla_w_q_up, m_mla_kv_norm_g, m_mla_w_kv_up, m_w_out_ab, m_rg_w_x, m_rg_w_y, m_rg_conv_w, m_rg_conv_b, m_rg_w_a, m_rg_b_a, m_rg_w_i, m_rg_b_i, m_rg_lambda, m_rg_w_out)))
    v = dict(zip(names, (v_meta_tokens, v_mix_pre_g, v_mix_post_g, v_mlp_pre_g, v_mlp_post_g, v_w_up, v_w_down, v_w_in, v_ssd_conv_w, v_ssd_conv_b, v_ssd_dt_bias, v_ssd_a_log, v_ssd_d, v_ssd_norm_g, v_mla_q_norm_g, v_mla_w_q_up, v_mla_kv_norm_g, v_mla_w_kv_up, v_w_out_ab, v_rg_w_x, v_rg_w_y, v_rg_conv_w, v_rg_conv_b, v_rg_w_a, v_rg_b_a, v_rg_w_i, v_rg_b_i, v_rg_lambda, v_rg_w_out)))
    full = _gather_weights(w)
    p = _layout_params({k: a for k, a in full.items() if k != "meta_tokens"})
    sq, dh, grads = _device_step(x[0], full["meta_tokens"], loss_target[0], p)
    loss = lax.psum(0.5 * sq[0, 0] / D_MODEL, ("x", "y", "c"))
    g = _reduce_grads(grads, dh, lax.axis_index("c"))
    grad, delta, new_m, new_v = {}, {}, {}, {}
    for name in names:
        shape = g[name].shape
        two_d = (int(np.prod(shape[:-1])), shape[-1])
        res = _adamw(g[name].reshape(two_d), w[name].reshape(two_d), m[name].reshape(two_d), v[name].reshape(two_d), "adamw")
        grad[name], delta[name], new_m[name], new_v[name] = (r.reshape(shape) for r in res)
    grad_x = dh[PAD + N_META:][None]
    return (loss, grad_x, *[grad[n] for n in names], *[delta[n] for n in names], *[new_m[n] for n in names], *[new_v[n] for n in names])
```

```python
import functools

import jax
import jax.numpy as jnp
import numpy as np
from jax import lax
from jax.experimental import pallas as pl
from jax.experimental.pallas import tpu as pltpu

F32 = jnp.float32
BF16 = jnp.bfloat16

D_MODEL = 1024
DEPTH = 4
N_META = 16
CHUNK = 128
PAD = CHUNK - N_META
EPS = 1e-6
SSD_HEADS = 16
SSD_HEAD_DIM = 64
SSD_D_INNER = SSD_HEADS * SSD_HEAD_DIM
SSD_GROUPS = 2
SSD_STATE = 128
SSD_CONV_CH = SSD_D_INNER + 2 * SSD_GROUPS * SSD_STATE
MLA_HEADS = 16
MLA_NOPE = 64
MLA_ROPE = 32
MLA_V = 64
MLA_Q_RANK = 384
MLA_KV_RANK = 256
ROPE_BASE = 10000.0
LRU_WIDTH = 1280
LRU_BLOCKS = 10
LRU_BLOCK = 128
LRU_C = 8.0
D_FF = 4 * D_MODEL
ADAM_LR, ADAM_B1, ADAM_B2, ADAM_EPS, ADAM_WD, ADAM_STEP = 0.001, 0.9, 0.999, 1e-08, 0.01, 10

LANES = 128
VMEM_LIMIT = 56 * 1024 * 1024
HEAD_SLOT = 128
PROJ_Z, PROJ_XBC, PROJ_DT, PROJ_CQ, PROJ_CKV, PROJ_KR = 0, 1024, 2560, 2688, 3072, 3328
PROJ_W = 3456


def _tile(n, cap, mult=8):
    for t in range(min(n, cap), 0, -1):
        if n % t == 0 and t % mult == 0:
            return t
    return n


def _params(sem):
    return pltpu.CompilerParams(dimension_semantics=sem, vmem_limit_bytes=VMEM_LIMIT)


def _full_spec(shape, ngrid):
    nd = len(shape)
    if ngrid == 1:
        return pl.BlockSpec(shape, lambda i: (0,) * nd)
    if ngrid == 2:
        return pl.BlockSpec(shape, lambda i, j: (0,) * nd)
    return pl.BlockSpec(shape, lambda i, j, k: (0,) * nd)


_DIMS = {"nn": (((1,), (0,)), ((), ())), "nt": (((1,), (1,)), ((), ())), "tn": (((0,), (0,)), ((), ()))}


class Gathered:
    def __init__(self, arr, kind, layer):
        self.arr, self.kind, self.layer = arr, kind, layer
        _, _, r, c = arr.shape
        self.shape = (r, N_CHIPS * c) if kind == "col" else (N_CHIPS * r, c)


N_CHIPS = 4


def _mm(a, b, mode, name, out_dtype=F32, add=None, out_chip_major=False):
    if mode == "nn":
        (m, kc), (_, n) = a.shape, b.shape
    elif mode == "nt":
        (m, kc), (n, _) = a.shape, b.shape
    else:
        (kc, m), (_, n) = a.shape, b.shape
    tm = _tile(m, 1024, LANES) if mode == "tn" else _tile(m, 1056, 16)
    tn = _tile(n // N_CHIPS if out_chip_major else n, 1280, LANES)
    tk = _tile(kc, 1024 if mode != "tn" else 1408, LANES)
    nk = kc // tk
    if mode == "tn":
        a_spec = pl.BlockSpec((tk, tm), lambda i, j, k: (k, i))
    else:
        a_spec = pl.BlockSpec((tm, tk), lambda i, j, k: (i, k))
    b_arr = b
    if isinstance(b, Gathered):
        b_arr, layer = b.arr, b.layer
        sr, sc = b.arr.shape[2:]
        br, bc = (tk, tn) if mode == "nn" else (tn, tk)
        assert mode in ("nn", "nt") and sr % br == 0 and sc % bc == 0

        def b_map(i, j, k):
            r, c = (k, j) if mode == "nn" else (j, k)
            if b.kind == "col":
                return ((c * bc) // sc, layer, r, ((c * bc) % sc) // bc)
            return ((r * br) // sr, layer, ((r * br) % sr) // br, c)

        b_spec = pl.BlockSpec((None, None, br, bc), b_map)
    elif mode == "nt":
        b_spec = pl.BlockSpec((tn, tk), lambda i, j, k: (j, k))
    else:
        b_spec = pl.BlockSpec((tk, tn), lambda i, j, k: (k, j))
    dims = _DIMS[mode]
    if out_chip_major:
        ns = n // N_CHIPS
        o_spec = pl.BlockSpec((None, tm, tn), lambda i, j, k: ((j * tn) // ns, i, ((j * tn) % ns) // tn))
        o_shape = jax.ShapeDtypeStruct((N_CHIPS, m, ns), out_dtype)
    else:
        o_spec = pl.BlockSpec((tm, tn), lambda i, j, k: (i, j))
        o_shape = jax.ShapeDtypeStruct((m, n), out_dtype)
    nadd = 0 if add is None else 1

    def body(a_ref, b_ref, *rest):
        o_ref, acc = rest[nadd], rest[nadd + 1:]
        p = lax.dot_general(a_ref[...].astype(BF16), b_ref[...].astype(BF16), dims, preferred_element_type=F32)

        def emit(v):
            o_ref[...] = (v + rest[0][...] if nadd else v).astype(o_ref.dtype)

        if nk == 1:
            emit(p)
        else:
            k = pl.program_id(2)

            @pl.when(k == 0)
            def _():
                acc[0][...] = p

            @pl.when(k > 0)
            def _():
                acc[0][...] += p

            @pl.when(k == nk - 1)
            def _():
                emit(acc[0][...])

    return pl.pallas_call(
        body, name=name, grid=(m // tm, n // tn, nk),
        in_specs=[a_spec, b_spec] + [o_spec] * nadd, out_specs=o_spec,
        out_shape=o_shape,
        scratch_shapes=[pltpu.VMEM((tm, tn), F32)] if nk > 1 else [],
        compiler_params=_params(("parallel", "parallel", "arbitrary")),
    )(a, b_arr, *([add] if nadd else []))


def _rowarg(r):
    return r if isinstance(r, tuple) else (r, r.shape[1], 0)


def _rowspec(r, tr, ncol):
    _, w, cb = r
    if ncol > 1:
        return pl.BlockSpec((tr, w // ncol), lambda j, i: (i, j))
    return pl.BlockSpec((tr, w), lambda j, i: (i, cb))


def _rowwise(name, f, rows, params, outs, tr=None, ncol=1):
    rows = [_rowarg(r) for r in rows]
    t = rows[0][0].shape[0]
    tr = tr or _tile(t, 528)
    nr, npm = len(rows), len(params)

    def body(*refs):
        vals = [r[...] for r in refs[:nr]] + [(p[0] if ncol > 1 else p[...]) for p in refs[nr:nr + npm]]
        res = f(pl.program_id(1) * tr, *vals)
        for o_ref, v in zip(refs[nr + npm:], res):
            o_ref[...] = v.astype(o_ref.dtype)

    def pspec(p):
        if ncol > 1:
            return pl.BlockSpec((1,) + p.shape[1:], lambda j, i, n=p.ndim: (j,) + (0,) * (n - 1))
        return _full_spec(p.shape, 2)

    return pl.pallas_call(
        body, name=name, grid=(ncol, t // tr),
        in_specs=[_rowspec(r, tr, ncol) for r in rows] + [pspec(p) for p in params],
        out_specs=[pl.BlockSpec((tr, w // ncol), lambda j, i: (i, j)) for w, _ in outs],
        out_shape=[jax.ShapeDtypeStruct((t, w), dt) for w, dt in outs],
        compiler_params=_params(("parallel", "parallel")),
    )(*[r[0] for r in rows], *params)


def _rowwise_vjp(name, f, rows, params, cts, tr=None, ncol=1, row_dtypes=None):
    rows = [_rowarg(r) for r in rows]
    cts = [_rowarg(c) for c in cts]
    t = rows[0][0].shape[0]
    tr = tr or _tile(t, 528)
    nr, npm, nc = len(rows), len(params), len(cts)
    row_dtypes = row_dtypes or [F32] * nr

    def body(*refs):
        i = pl.program_id(1)
        vals = [r[...] for r in refs[:nr]] + [(p[0] if ncol > 1 else p[...]) for p in refs[nr:nr + npm]]
        ct = tuple(c[...].astype(F32) for c in refs[nr + npm:nr + npm + nc])
        _, vjp = jax.vjp(lambda *a: tuple(f(i * tr, *a)), *vals)
        g = vjp(ct)
        outs = refs[nr + npm + nc:]
        for o_ref, v in zip(outs[:nr], g[:nr]):
            o_ref[...] = v.astype(o_ref.dtype)
        pg = [(v[None] if ncol > 1 else v) for v in g[nr:]]

        @pl.when(i == 0)
        def _():
            for o_ref, v in zip(outs[nr:], pg):
                o_ref[...] = v

        @pl.when(i > 0)
        def _():
            for o_ref, v in zip(outs[nr:], pg):
                o_ref[...] += v

    def pspec(p):
        if ncol > 1:
            return pl.BlockSpec((1,) + p.shape[1:], lambda j, i, n=p.ndim: (j,) + (0,) * (n - 1))
        return _full_spec(p.shape, 2)

    res = pl.pallas_call(
        body, name=name, grid=(ncol, t // tr),
        in_specs=[_rowspec(r, tr, ncol) for r in rows] + [pspec(p) for p in params] + [_rowspec(c, tr, ncol) for c in cts],
        out_specs=[pl.BlockSpec((tr, w // ncol), lambda j, i: (i, j)) for _, w, _ in rows] + [pspec(p) for p in params],
        out_shape=[jax.ShapeDtypeStruct((t, w), dt) for (_, w, _), dt in zip(rows, row_dtypes)]
        + [jax.ShapeDtypeStruct(p.shape, F32) for p in params],
        compiler_params=_params(("parallel", "arbitrary")),
    )(*[r[0] for r in rows], *params, *[c[0] for c in cts])
    return res[:nr], res[nr:]


def _valid(row0, tr):
    return (row0 + lax.broadcasted_iota(jnp.int32, (tr, 1), 0)) >= PAD


def _rms(x, g):
    return x * lax.rsqrt(jnp.mean(x * x, axis=-1, keepdims=True) + EPS) * g


def _softplus(x):
    return jnp.where(x < -15.0, jnp.exp(x), jnp.maximum(x, 0.0) + jnp.log(1.0 + jnp.exp(-jnp.abs(x))))


def _neg_expm1(z):
    return jnp.where(z > -0.01, -z * (1.0 + z * (0.5 + z * (1.0 / 6.0))), 1.0 - jnp.exp(z))


def _prenorm(h, g, name):
    return _rowwise(name, lambda r0, x, gg: (_rms(x, gg),), [h], [g], [(_rowarg(h)[1], BF16)])[0]


def _add_postnorm(h, ms, g, name):
    def f(r0, x, *rest):
        return (x + _rms(functools.reduce(jnp.add, rest[:-1]), rest[-1]),)

    return _rowwise(name, f, [h] + list(ms), [g], [(h.shape[1], F32)])[0]


def _postnorm_bwd(m, g, dh, name):
    (dm,), (dg,) = _rowwise_vjp(name, lambda r0, mm, gg: (_rms(mm, gg),), [m], [g], [dh])
    return dm, dg


def _prenorm_bwd_add(h, g, dhns, dh, name):
    t, w = h.shape
    tr = _tile(t, 528)
    nd = len(dhns)

    def body(h_ref, g_ref, *refs):
        dh_ref, o_ref, dg_ref = refs[nd:]
        i = pl.program_id(0)
        _, vjp = jax.vjp(_rms, h_ref[...], g_ref[...])
        dhn = refs[0][...].astype(F32)
        for r in refs[1:nd]:
            dhn = dhn + r[...].astype(F32)
        dx, dg = vjp(dhn)
        o_ref[...] = dh_ref[...] + dx

        @pl.when(i == 0)
        def _():
            dg_ref[...] = dg

        @pl.when(i > 0)
        def _():
            dg_ref[...] += dg

    row = pl.BlockSpec((tr, w), lambda i: (i, 0))
    return pl.pallas_call(
        body, name=name, grid=(t // tr,), in_specs=[row, _full_spec(g.shape, 1)] + [row] * (nd + 1),
        out_specs=[row, _full_spec(g.shape, 1)],
        out_shape=[jax.ShapeDtypeStruct((t, w), F32), jax.ShapeDtypeStruct(g.shape, F32)],
        compiler_params=_params(("arbitrary",)),
    )(h, g, *dhns, dh)


def _relu2(a, name):
    return _rowwise(name, lambda r0, x: (jnp.square(jnp.maximum(x, 0.0)),), [a], [], [(a.shape[1], BF16)], tr=_tile(a.shape[0], 264))[0]


def _relu2_bwd(a, du, name):
    return _rowwise(name, lambda r0, x, d: (2.0 * jnp.maximum(x, 0.0) * d,), [a, du], [], [(a.shape[1], BF16)],
                    tr=_tile(a.shape[0], 264))[0]


def _loss_and_grad(h, target, name):
    t, w = h.shape
    nb = t // CHUNK

    def body(h_ref, t_ref, s_ref, dh_ref):
        i = pl.program_id(0)

        @pl.when(i == 0)
        def _():
            s_ref[...] = jnp.zeros_like(s_ref)
            dh_ref[...] = jnp.zeros_like(dh_ref)

        @pl.when(i > 0)
        def _():
            err = h_ref[...] - t_ref[...]
            s_ref[...] += jnp.sum(err * err)
            dh_ref[...] = err * (1.0 / w)

    return pl.pallas_call(
        body, name=name, grid=(nb,),
        in_specs=[pl.BlockSpec((CHUNK, w), lambda i: (i, 0)), pl.BlockSpec((CHUNK, w), lambda i: (jnp.maximum(i - 1, 0), 0))],
        out_specs=[_full_spec((1, LANES), 1), pl.BlockSpec((CHUNK, w), lambda i: (i, 0))],
        out_shape=[jax.ShapeDtypeStruct((1, LANES), F32), jax.ShapeDtypeStruct((t, w), F32)],
        compiler_params=_params(("arbitrary",)),
    )(h, target)


def _mlp_fwd(h, p, l):
    hn = _prenorm(h, p["mlp_pre_g"][l], "mlp_prenorm")
    a = _mm(hn, p["w_up"][l], "nn", "mlp_up")
    u = _relu2(a, "mlp_relu2")
    d = _mm(u, p["w_down"][l], "nn", "mlp_down")
    h2 = _add_postnorm(h, [d], p["mlp_post_g"][l], "mlp_postnorm")
    return h2, (h, hn, a, u, d)


def _mlp_bwd(dh, saved, p, l, grads):
    h, hn, a, u, d = saved
    dd, grads["mlp_post_g"][l] = _postnorm_bwd(d, p["mlp_post_g"][l], dh, "mlp_postnorm_bwd")
    grads["w_down"][l] = _mm(u, dd, "tn", "mlp_down_dw")
    du = _mm(dd, p["w_down"][l], "nt", "mlp_down_dx")
    da = _relu2_bwd(a, du, "mlp_relu2_bwd")
    grads["w_up"][l] = _mm(hn, da, "tn", "mlp_up_dw", out_chip_major=True)
    dhn = _mm(da, p["w_up"][l], "nt", "mlp_up_dx")
    dh, grads["mlp_pre_g"][l] = _prenorm_bwd_add(h, p["mlp_pre_g"][l], [dhn], dh, "mlp_prenorm_bwd")
    return dh


def _dot(a, b, mode):
    return lax.dot_general(a.astype(BF16), b.astype(BF16), _DIMS[mode], preferred_element_type=F32)


@jax.custom_vjp
def _bnn(a, b):
    return _dot(a, b, "nn")


_bnn.defvjp(lambda a, b: (_dot(a, b, "nn"), (a, b)), lambda r, ct: (_dot(ct, r[1], "nt"), _dot(r[0], ct, "tn")))


@jax.custom_vjp
def _bnt(a, b):
    return _dot(a, b, "nt")


_bnt.defvjp(lambda a, b: (_dot(a, b, "nt"), (a, b)), lambda r, ct: (_dot(ct, r[1], "nn"), _dot(ct, r[0], "tn")))


@jax.custom_vjp
def _btn(a, b):
    return _dot(a, b, "tn")


_btn.defvjp(lambda a, b: (_dot(a, b, "tn"), (a, b)), lambda r, ct: (_dot(r[1], ct, "nt"), _dot(r[0], ct, "nn")))


CONV_K = 4
HALO = 8


def _conv_fwd(x, w, b, name, cw, c0=0):
    t, c = x.shape[0], w.shape[1]
    tr = _tile(t, 528)
    hb = tr // HALO

    def body(x_ref, halo_ref, w_ref, b_ref, o_ref, ext):
        i = pl.program_id(1)
        ext[pl.ds(0, HALO), :] = jnp.where(i > 0, halo_ref[...], 0.0)
        ext[pl.ds(HALO, tr), :] = x_ref[...]
        acc = jnp.broadcast_to(b_ref[...], (tr, cw))
        for k in range(CONV_K):
            acc = acc + w_ref[pl.ds(k, 1), :] * ext[pl.ds(HALO - (CONV_K - 1) + k, tr), :]
        o_ref[...] = acc

    return pl.pallas_call(
        body, name=name, grid=(c // cw, t // tr),
        in_specs=[pl.BlockSpec((tr, cw), lambda j, i: (i, c0 + j)),
                  pl.BlockSpec((HALO, cw), lambda j, i: (jnp.maximum(i * hb - 1, 0), c0 + j)),
                  pl.BlockSpec((CONV_K, cw), lambda j, i: (0, j)), pl.BlockSpec((1, cw), lambda j, i: (0, j))],
        out_specs=pl.BlockSpec((tr, cw), lambda j, i: (i, j)),
        out_shape=jax.ShapeDtypeStruct((t, c), F32),
        scratch_shapes=[pltpu.VMEM((tr + HALO, cw), F32)],
        compiler_params=_params(("parallel", "parallel")),
    )(x, x, w, b)


def _conv_bwd(x, w, dy, name, cw, c0=0):
    t, c = x.shape[0], w.shape[1]
    tr = _tile(t, 528)
    hb = tr // HALO
    nb = t // tr

    def body(x_ref, xh_ref, w_ref, dy_ref, dyh_ref, dx_ref, dw_ref, db_ref, xe, de):
        c = cw
        i = pl.program_id(1)
        xe[pl.ds(0, HALO), :] = jnp.where(i > 0, xh_ref[...], 0.0)
        xe[pl.ds(HALO, tr), :] = x_ref[...]
        de[pl.ds(0, tr), :] = dy_ref[...]
        de[pl.ds(tr, HALO), :] = jnp.where(i < nb - 1, dyh_ref[...], 0.0)
        dy = dy_ref[...]
        acc = jnp.zeros((tr, c), F32)
        dw = jnp.zeros((CONV_K, c), F32)
        rows = lax.broadcasted_iota(jnp.int32, (CONV_K, 1), 0)
        for k in range(CONV_K):
            acc = acc + w_ref[pl.ds(k, 1), :] * de[pl.ds(CONV_K - 1 - k, tr), :]
            dwk = jnp.sum(dy * xe[pl.ds(HALO - (CONV_K - 1) + k, tr), :], axis=0, keepdims=True)
            dw = dw + jnp.where(rows == k, dwk, 0.0)
        dx_ref[...] = jnp.where(_valid(i * tr, tr), acc, 0.0)
        db = jnp.sum(dy, axis=0, keepdims=True)

        @pl.when(i == 0)
        def _():
            dw_ref[...] = dw
            db_ref[...] = db

        @pl.when(i > 0)
        def _():
            dw_ref[...] += dw
            db_ref[...] += db

    row = pl.BlockSpec((tr, cw), lambda j, i: (i, j))
    return pl.pallas_call(
        body, name=name, grid=(c // cw, nb),
        in_specs=[pl.BlockSpec((tr, cw), lambda j, i: (i, c0 + j)),
                  pl.BlockSpec((HALO, cw), lambda j, i: (jnp.maximum(i * hb - 1, 0), c0 + j)),
                  pl.BlockSpec((CONV_K, cw), lambda j, i: (0, j)),
                  row, pl.BlockSpec((HALO, cw), lambda j, i: (jnp.minimum((i + 1) * hb, t // HALO - 1), j))],
        out_specs=[row, pl.BlockSpec((CONV_K, cw), lambda j, i: (0, j)), pl.BlockSpec((1, cw), lambda j, i: (0, j))],
        out_shape=[jax.ShapeDtypeStruct((t, c), F32), jax.ShapeDtypeStruct((CONV_K, c), F32), jax.ShapeDtypeStruct((1, c), F32)],
        scratch_shapes=[pltpu.VMEM((tr + HALO, cw), F32), pltpu.VMEM((tr + HALO, cw), F32)],
        compiler_params=_params(("parallel", "arbitrary")),
    )(x, x, w, dy, dy)


SUB = 8


def _lru_scan(a, u, name):
    t, c = a.shape
    tr = _tile(t, 528)

    def body(a_ref, u_ref, o_ref, carry):
        @pl.when(pl.program_id(0) == 0)
        def _():
            carry[...] = jnp.zeros_like(carry)

        rows = lax.broadcasted_iota(jnp.int32, (SUB, 1), 0)

        def step(k, cin):
            r = pl.multiple_of(k * SUB, SUB)
            av, uv = a_ref[pl.ds(r, SUB), :], u_ref[pl.ds(r, SUB), :]
            for d in (1, 2, 4):
                m = rows >= d
                uv = uv + av * jnp.where(m, pltpu.roll(uv, d, 0), 0.0)
                av = av * jnp.where(m, pltpu.roll(av, d, 0), 1.0)
            hv = uv + av * cin
            o_ref[pl.ds(r, SUB), :] = hv
            return jnp.broadcast_to(hv[SUB - 1:SUB, :], (SUB, c))

        carry[...] = lax.fori_loop(0, tr // SUB, step, carry[...])

    row = pl.BlockSpec((tr, c), lambda i: (i, 0))
    return pl.pallas_call(
        body, name=name, grid=(t // tr,), in_specs=[row, row], out_specs=row,
        out_shape=jax.ShapeDtypeStruct((t, c), F32), scratch_shapes=[pltpu.VMEM((SUB, c), F32)],
        compiler_params=_params(("arbitrary",)),
    )(a, u)


def _lru_scan_bwd(a, hs, dy, name):
    t, c = a.shape
    tr = _tile(t, 528)
    nb, nt = t // tr, tr // SUB

    def body(a_ref, h_ref, hh_ref, dy_ref, du_ref, da_ref, gcar, acar):
        i = pl.program_id(0)

        @pl.when(i == 0)
        def _():
            gcar[...] = jnp.zeros_like(gcar)
            acar[...] = jnp.zeros_like(acar)

        rows = lax.broadcasted_iota(jnp.int32, (SUB, 1), 0)
        hhalo = jnp.where(i < nb - 1, hh_ref[...], 0.0)

        def step(kk, car):
            gin, a_next_first = car
            k = nt - 1 - kk
            r = pl.multiple_of(k * SUB, SUB)
            av, hv, dv = a_ref[pl.ds(r, SUB), :], h_ref[pl.ds(r, SUB), :], dy_ref[pl.ds(r, SUB), :]
            rp = pl.multiple_of(jnp.maximum(k - 1, 0) * SUB, SUB)
            hp = jnp.where(k > 0, h_ref[pl.ds(rp, SUB), :], hhalo)
            cv = jnp.where(rows < SUB - 1, pltpu.roll(av, SUB - 1, 0), a_next_first)
            gv = dv
            for d in (1, 2, 4):
                m = rows < SUB - d
                gv = gv + cv * jnp.where(m, pltpu.roll(gv, SUB - d, 0), 0.0)
                cv = cv * jnp.where(m, pltpu.roll(cv, SUB - d, 0), 1.0)
            gv = gv + cv * gin
            hprev = jnp.where(rows >= 1, pltpu.roll(hv, 1, 0), jnp.broadcast_to(hp[SUB - 1:SUB, :], (SUB, c)))
            du_ref[pl.ds(r, SUB), :] = gv
            da_ref[pl.ds(r, SUB), :] = gv * hprev
            return jnp.broadcast_to(gv[0:1, :], (SUB, c)), jnp.broadcast_to(av[0:1, :], (SUB, c))

        g, af = lax.fori_loop(0, nt, step, (gcar[...], acar[...]))
        gcar[...] = g
        acar[...] = af

    hb = tr // SUB
    row = pl.BlockSpec((tr, c), lambda i: (nb - 1 - i, 0))
    halo = pl.BlockSpec((SUB, c), lambda i: (jnp.maximum((nb - 1 - i) * hb - 1, 0), 0))
    return pl.pallas_call(
        body, name=name, grid=(nb,), in_specs=[row, row, halo, row], out_specs=[row, row],
        out_shape=[jax.ShapeDtypeStruct((t, c), F32)] * 2,
        scratch_shapes=[pltpu.VMEM((SUB, c), F32), pltpu.VMEM((SUB, c), F32)],
        compiler_params=_params(("arbitrary",)),
    )(a, hs, hs, dy)


def _lru_gates(row0, xr, wa, ba, wi, bi, lam):
    r = jax.nn.sigmoid(_bnn(xr, wa) + ba)
    i = jax.nn.sigmoid(_bnn(xr, wi) + bi)
    log_a = -LRU_C * r * _softplus(-lam)
    u = jnp.sqrt(_neg_expm1(2.0 * log_a)) * (i * xr)
    return jnp.exp(log_a), jnp.where(_valid(row0, xr.shape[0]), u, 0.0)


def _lru_gate_out(row0, hs, yw):
    return (hs * jax.nn.gelu(yw),)


def _rglru_fwd(h, p, l, o):
    hn = _prenorm(h, p["mix_pre_g"][l], "rg_prenorm")
    xw = _mm(hn, p["rg_w_x"][o], "nn", "rg_in_x")
    yw = _mm(hn, p["rg_w_y"][o], "nn", "rg_in_y")
    xr = _conv_fwd(xw, p["rg_conv_w"][o], p["rg_conv_b"][o], "rg_conv", cw=LRU_WIDTH // 2)
    gp = [p["rg_w_a"][o], p["rg_b_a"][o], p["rg_w_i"][o], p["rg_b_i"][o], p["rg_lambda"][o]]
    a, u = _rowwise("rg_gates", _lru_gates, [xr], gp, [(LRU_WIDTH, F32)] * 2, ncol=LRU_BLOCKS)
    hs = _lru_scan(a, u, "rg_scan")
    hg = _rowwise("rg_gate_out", _lru_gate_out, [hs, yw], [], [(LRU_WIDTH, BF16)])[0]
    m = _mm(hg, p["rg_w_out"][o], "nn", "rg_out")
    h2 = _add_postnorm(h, [m], p["mix_post_g"][l], "rg_postnorm")
    return h2, (h, hn, xw, yw, xr, a, hs, hg, m)


def _rglru_bwd(dh, saved, p, l, o, grads):
    h, hn, xw, yw, xr, a, hs, hg, m = saved
    dm, grads["mix_post_g"][l] = _postnorm_bwd(m, p["mix_post_g"][l], dh, "rg_postnorm_bwd")
    grads["rg_w_out"][o] = _mm(hg, dm, "tn", "rg_out_dw")
    dhg = _mm(dm, p["rg_w_out"][o], "nt", "rg_out_dx")
    (dhs, dyw), _ = _rowwise_vjp("rg_gate_out_bwd", _lru_gate_out, [hs, yw], [], [dhg])
    du, da = _lru_scan_bwd(a, hs, dhs, "rg_scan_bwd")
    gp = [p["rg_w_a"][o], p["rg_b_a"][o], p["rg_w_i"][o], p["rg_b_i"][o], p["rg_lambda"][o]]
    (dxr,), gg = _rowwise_vjp("rg_gates_bwd", _lru_gates, [xr], gp, [da, du], ncol=LRU_BLOCKS)
    grads["rg_w_a"][o], grads["rg_b_a"][o], grads["rg_w_i"][o], grads["rg_b_i"][o], grads["rg_lambda"][o] = gg
    dxw, grads["rg_conv_w"][o], grads["rg_conv_b"][o] = _conv_bwd(xw, p["rg_conv_w"][o], dxr, "rg_conv_bwd", cw=LRU_WIDTH // 2)
    grads["rg_w_x"][o] = _mm(hn, dxw, "tn", "rg_in_x_dw")
    grads["rg_w_y"][o] = _mm(hn, dyw, "tn", "rg_in_y_dw")
    dhx = _mm(dxw, p["rg_w_x"][o], "nt", "rg_in_x_dx")
    dhy = _mm(dyw, p["rg_w_y"][o], "nt", "rg_in_y_dx")
    dh, grads["mix_pre_g"][l] = _prenorm_bwd_add(h, p["mix_pre_g"][l], [dhx, dhy], dh, "rg_prenorm_bwd")
    return dh


SSD_GW = SSD_D_INNER // SSD_GROUPS
SSD_GH = SSD_HEADS // SSD_GROUPS
XACT_B = SSD_D_INNER // SSD_STATE
XACT_C = XACT_B + SSD_GROUPS


def _hp(a, b, dims=_DIMS["nn"]):
    return lax.dot_general(a, b, dims, precision=lax.Precision.HIGHEST, preferred_element_type=F32)


def _ssd_chunk(xs, bm, cm, dt, da, ht, g):
    l = CHUNK
    ri = lax.broadcasted_iota(jnp.int32, (l, l), 0)
    ci = lax.broadcasted_iota(jnp.int32, (l, l), 1)
    causal = ri >= ci
    tri = causal.astype(F32)
    hr = lax.broadcasted_iota(jnp.int32, (LANES, SSD_GW), 0)
    hc = lax.broadcasted_iota(jnp.int32, (LANES, SSD_GW), 1)
    expand = (hr == g * SSD_GH + hc // SSD_HEAD_DIM).astype(F32)
    acs = _hp(tri, da)
    acs_t = _hp(da, tri, (((0,), (1,)), ((), ())))
    acs_e = _hp(acs, expand)
    x = xs * _hp(dt, expand)
    gmat = _bnt(cm, bm)
    lane = lax.broadcasted_iota(jnp.int32, (1, LANES), 1)
    sub = lax.broadcasted_iota(jnp.int32, (LANES, 1), 0)
    colhead = lax.broadcasted_iota(jnp.int32, (1, SSD_GW), 1) // SSD_HEAD_DIM
    y = _bnn(cm, ht) * jnp.exp(acs_e)
    for k in range(SSD_GH):
        hh = g * SSD_GH + k
        col = jnp.sum(jnp.where(lane == hh, acs, 0.0), axis=1, keepdims=True)
        row = jnp.sum(jnp.where(sub == hh, acs_t, 0.0), axis=0, keepdims=True)
        decay = jnp.exp(jnp.where(causal, col - row, -1e30))
        y = y + _bnn(gmat * decay, jnp.where(colhead == k, x, 0.0))
    last = lax.broadcasted_iota(jnp.int32, (l, 1), 0) == l - 1
    a_last = jnp.sum(jnp.where(last, acs_e, 0.0), axis=0, keepdims=True)
    st = _btn(bm, x * jnp.exp(a_last - acs_e))
    return y, ht * jnp.exp(a_last) + st


def _ssd_specs(nc, rev):
    def cc(c):
        return nc - 1 - c if rev else c

    return [pl.BlockSpec((CHUNK, SSD_GW), lambda c, g: (cc(c), g)),
            pl.BlockSpec((CHUNK, SSD_STATE), lambda c, g: (cc(c), XACT_B + g)),
            pl.BlockSpec((CHUNK, SSD_STATE), lambda c, g: (cc(c), XACT_C + g)),
            pl.BlockSpec((CHUNK, LANES), lambda c, g: (cc(c), 0)),
            pl.BlockSpec((CHUNK, LANES), lambda c, g: (cc(c), 0))]


def _ssd_scan(xact, dt, da, name):
    t = xact.shape[0]
    nc = t // CHUNK

    def body(xs_ref, b_ref, c_ref, dt_ref, da_ref, y_ref, hs_ref, state):
        c, g = pl.program_id(0), pl.program_id(1)

        @pl.when(c == 0)
        def _():
            state[g] = jnp.zeros((SSD_STATE, SSD_GW), F32)

        ht = state[g]
        hs_ref[0] = ht
        y, ht2 = _ssd_chunk(xs_ref[...], b_ref[...], c_ref[...], dt_ref[...], da_ref[...], ht, g)
        y_ref[...] = y
        state[g] = ht2

    return pl.pallas_call(
        body, name=name, grid=(nc, SSD_GROUPS), in_specs=_ssd_specs(nc, False),
        out_specs=[pl.BlockSpec((CHUNK, SSD_GW), lambda c, g: (c, g)),
                   pl.BlockSpec((1, SSD_STATE, SSD_GW), lambda c, g: (c * SSD_GROUPS + g, 0, 0))],
        out_shape=[jax.ShapeDtypeStruct((t, SSD_D_INNER), F32), jax.ShapeDtypeStruct((nc * SSD_GROUPS, SSD_STATE, SSD_GW), F32)],
        scratch_shapes=[pltpu.VMEM((SSD_GROUPS, SSD_STATE, SSD_GW), F32)],
        compiler_params=_params(("arbitrary", "arbitrary")),
    )(xact, xact, xact, dt, da)


def _ssd_scan_bwd(xact, dt, da, hsave, dy, dxskip, name):
    t = xact.shape[0]
    nc = t // CHUNK

    def body(xs_ref, b_ref, c_ref, dt_ref, da_ref, hs_ref, dy_ref, sk_ref, dxs_ref, db_ref, dc_ref, ddt_ref, dda_ref, dstate):
        c, g = pl.program_id(0), pl.program_id(1)

        @pl.when(c == 0)
        def _():
            dstate[g] = jnp.zeros((SSD_STATE, SSD_GW), F32)

        _, vjp = jax.vjp(lambda *a: _ssd_chunk(*a, g), xs_ref[...], b_ref[...], c_ref[...], dt_ref[...], da_ref[...], hs_ref[0])
        dxs, dbm, dcm, ddt, dda, dht = vjp((dy_ref[...], dstate[g]))
        dxs_ref[...] = dxs + sk_ref[...]
        db_ref[...] = dbm
        dc_ref[...] = dcm
        dstate[g] = dht

        @pl.when(g == 0)
        def _():
            ddt_ref[...] = ddt
            dda_ref[...] = dda

        @pl.when(g > 0)
        def _():
            ddt_ref[...] += ddt
            dda_ref[...] += dda

    grp = pl.BlockSpec((CHUNK, SSD_GW), lambda c, g: (nc - 1 - c, g))
    st = pl.BlockSpec((CHUNK, SSD_STATE), lambda c, g: (nc - 1 - c, g))
    hd = pl.BlockSpec((CHUNK, LANES), lambda c, g: (nc - 1 - c, 0))
    return pl.pallas_call(
        body, name=name, grid=(nc, SSD_GROUPS),
        in_specs=_ssd_specs(nc, True) + [pl.BlockSpec((1, SSD_STATE, SSD_GW), lambda c, g: ((nc - 1 - c) * SSD_GROUPS + g, 0, 0)), grp, grp],
        out_specs=[grp, st, st, hd, hd],
        out_shape=[jax.ShapeDtypeStruct((t, SSD_D_INNER), F32), jax.ShapeDtypeStruct((t, SSD_GROUPS * SSD_STATE), F32),
                   jax.ShapeDtypeStruct((t, SSD_GROUPS * SSD_STATE), F32), jax.ShapeDtypeStruct((t, LANES), F32),
                   jax.ShapeDtypeStruct((t, LANES), F32)],
        scratch_shapes=[pltpu.VMEM((SSD_GROUPS, SSD_STATE, SSD_GW), F32)],
        compiler_params=_params(("arbitrary", "arbitrary")),
    )(xact, xact, xact, dt, da, hsave, dy, dxskip)


def _ssd_act(row0, xc):
    return (jnp.where(_valid(row0, xc.shape[0]), jax.nn.silu(xc), 0.0),)


def _ssd_dt(row0, dtraw, dt_bias, a_log):
    dt = jnp.where(_valid(row0, dtraw.shape[0]), _softplus(dtraw + dt_bias), 0.0)
    return dt, dt * -jnp.exp(a_log)


def _ssd_post(row0, y, xs, z, d_skip, norm_g):
    hr = lax.broadcasted_iota(jnp.int32, (LANES, SSD_D_INNER), 0)
    hc = lax.broadcasted_iota(jnp.int32, (LANES, SSD_D_INNER), 1)
    expand = (hr == hc // SSD_HEAD_DIM).astype(F32)
    d_e = jnp.sum(_hp(jnp.broadcast_to(d_skip, (SUB, LANES)), expand), axis=0, keepdims=True) * (1.0 / SUB)
    return (_rms((y + xs * d_e) * jax.nn.silu(z), norm_g),)


ROPE_LO, ROPE_MID, ROPE_HI = MLA_NOPE, MLA_NOPE + MLA_ROPE // 2, MLA_NOPE + MLA_ROPE
ATT_SCALE = (MLA_NOPE + MLA_ROPE) ** -0.5


def _slot_lane(width):
    return lax.broadcasted_iota(jnp.int32, (1, width), 1) % LANES


def _swap_halves(x):
    width = x.shape[1]
    lane = _slot_lane(width)
    sw = jnp.where(lane < ROPE_MID, pltpu.roll(x, width - MLA_ROPE // 2, 1), pltpu.roll(x, MLA_ROPE // 2, 1))
    return jnp.where((lane >= ROPE_LO) & (lane < ROPE_HI), sw, 0.0)


def _rope(x, cos, sin):
    n = x.shape[1] // LANES
    return x * jnp.tile(cos, (1, n)) + _swap_halves(x) * jnp.tile(sin, (1, n))


def _rope_t(dy, cos, sin):
    n = dy.shape[1] // LANES
    return dy * jnp.tile(cos, (1, n)) + _swap_halves(dy * jnp.tile(sin, (1, n)))


ATT_SCALE2 = ATT_SCALE * float(np.log2(np.e))
MASKED = -1e30
ATT_STRIP = 64


def _att_mask(i, j, blk):
    rowid = i * blk + lax.broadcasted_iota(jnp.int32, (blk, 1), 0)
    colid = j * blk + lax.broadcasted_iota(jnp.int32, (1, blk), 1)
    return (colid <= rowid) & (colid >= PAD)


def _att_bias(blk):
    r = jnp.arange(blk)[:, None]
    c = jnp.arange(blk)[None, :]
    zero = jnp.zeros((blk, blk), F32)
    first = jnp.where(c >= PAD, 0.0, MASKED) + zero
    diag = jnp.where(c <= r, 0.0, MASKED).astype(F32)
    return jnp.stack([zero, first, diag, jnp.minimum(first, diag), zero + MASKED])


def _att_bias_index(j, i):
    return jnp.where(j > i, 4, jnp.where(j == 0, 1, 0) + jnp.where(j == i, 2, 0))


def _key_slots(row0, kv, kr):
    width = kv.shape[1]
    return jnp.where(_slot_lane(width) < MLA_NOPE, kv, jnp.tile(kr, (1, width // LANES))), kv


def _attn_fwd(qr, km, vb, name):
    t = qr.shape[0]
    blk = _tile(t, 384, LANES)
    nq = t // blk

    bias = _att_bias(blk)

    def body(q_ref, k_ref, v_ref, b_ref, o_ref, s0, s1, p0, p1):
        i = pl.program_id(1)
        lane = lax.broadcasted_iota(jnp.int32, (1, LANES), 1)
        qb = q_ref[...]

        def rows(j):
            return pl.ds(pl.multiple_of(jnp.clip(j, 0, i) * blk, blk), blk)

        def scores(j):
            return lax.dot_general(qb, k_ref[rows(j), :], _DIMS["nt"], preferred_element_type=F32) + b_ref[_att_bias_index(j, i)]

        def half(j, car, s_cur, s_nxt, p_cur, p_prv):
            m, l, acc, al_prev = car
            s_nxt[...] = scores(j + 1)
            acc2 = al_prev * acc + lax.dot_general(p_prv[...], v_ref[rows(j - 1), :], _DIMS["nn"], preferred_element_type=F32)
            m2 = jnp.maximum(m, jnp.max(s_cur[...], axis=1, keepdims=True))
            al = jnp.exp2((m - m2) * ATT_SCALE2)
            pm = jnp.exp2(s_cur[...] * ATT_SCALE2 - m2 * ATT_SCALE2)
            p_cur[...] = pm.astype(BF16)
            return m2, al * l + jnp.sum(pm, axis=1, keepdims=True), acc2, al

        def step(jj, car):
            car = half(2 * jj, car, s0, s1, p0, p1)
            return half(2 * jj + 1, car, s1, s0, p1, p0)

        s0[...] = scores(0)
        p1[...] = jnp.zeros((blk, blk), BF16)
        car = (jnp.full((blk, 1), MASKED, F32), jnp.zeros((blk, 1), F32), jnp.zeros((blk, LANES), F32), jnp.ones((blk, 1), F32))
        steps = i // 2 + 1
        m, l, acc, al_last = lax.fori_loop(0, steps, step, car)
        acc = al_last * acc + lax.dot_general(p1[...], v_ref[rows(2 * steps - 1), :], _DIMS["nn"], preferred_element_type=F32)
        out = jnp.where(lane >= MLA_NOPE, acc / l, m * ATT_SCALE + jnp.log(l))
        o_ref[...] = jnp.where(_valid(i * blk, blk), out, 0.0)

    seq_h = pl.BlockSpec((t, LANES), lambda h, i: (0, h))
    return pl.pallas_call(
        body, name=name, grid=(MLA_HEADS, nq),
        in_specs=[pl.BlockSpec((blk, LANES), lambda h, i: (i, h)), seq_h, seq_h, _full_spec(bias.shape, 2)],
        out_specs=pl.BlockSpec((blk, LANES), lambda h, i: (i, h)),
        out_shape=jax.ShapeDtypeStruct((t, MLA_HEADS * LANES), F32),
        scratch_shapes=[pltpu.VMEM((blk, blk), F32)] * 2 + [pltpu.VMEM((blk, blk), BF16)] * 2,
        compiler_params=_params(("parallel", "parallel")),
    )(qr, km, vb, bias)


def _attn_bwd(qr, km, vb, o, do, name):
    t = qr.shape[0]
    blk = _tile(t, 384, LANES)
    nq = t // blk

    bias = _att_bias(blk)
    log2e = float(np.log2(np.e))

    def body(q_ref, o_ref, do_ref, k_ref, v_ref, b_ref, dq_ref, dkv_ref, dkr_ref, s0, s1, dp0, dp1, p0, p1, ds0, ds1):
        h, j = pl.program_id(0), pl.program_id(1)
        lane = lax.broadcasted_iota(jnp.int32, (1, LANES), 1)

        @pl.when(j == 0)
        def _():
            dq_ref[...] = jnp.zeros_like(dq_ref)

        @pl.when((h == 0) & (j == 0))
        def _():
            dkr_ref[...] = jnp.zeros_like(dkr_ref)

        kmat, vmat = k_ref[...], v_ref[...]

        def rows(i):
            return pl.ds(pl.multiple_of(jnp.clip(i, j, nq - 1) * blk, blk), blk)

        def first_stage(i, s_buf, dp_buf):
            bidx = jnp.where(i >= nq, 4, _att_bias_index(j, i))
            s_buf[...] = lax.dot_general(q_ref[rows(i), :], kmat, _DIMS["nt"], preferred_element_type=F32) + b_ref[bidx]
            dp_buf[...] = lax.dot_general(do_ref[rows(i), :].astype(BF16), vmat, _DIMS["nt"], preferred_element_type=F32)

        def last_stage(i, car, p_buf, ds_buf):
            dk, dv = car
            r = rows(i)
            dv = dv + lax.dot_general(p_buf[...], do_ref[r, :].astype(BF16), _DIMS["tn"], preferred_element_type=F32)
            dk = dk + lax.dot_general(ds_buf[...], q_ref[r, :], _DIMS["tn"], preferred_element_type=F32)
            dq_ref[r, :] += lax.dot_general(ds_buf[...], kmat, _DIMS["nn"], preferred_element_type=F32)
            return dk, dv

        def half(i, car, s_cur, dp_cur, p_cur, ds_cur, s_nxt, dp_nxt, p_prv, ds_prv):
            first_stage(i + 1, s_nxt, dp_nxt)
            car = last_stage(i - 1, car, p_prv, ds_prv)
            r = rows(i)
            ob, dob = o_ref[r, :], do_ref[r, :]
            delta = jnp.sum(dob * ob, axis=1, keepdims=True)
            pm = jnp.exp2(s_cur[...] * ATT_SCALE2 - ob[:, 0:1] * log2e)
            p_cur[...] = pm.astype(BF16)
            ds_cur[...] = (pm * (dp_cur[...] - delta) * ATT_SCALE).astype(BF16)
            return car

        def step(tt, car):
            i = j + 2 * tt
            car = half(i, car, s0, dp0, p0, ds0, s1, dp1, p1, ds1)
            return half(i + 1, car, s1, dp1, p1, ds1, s0, dp0, p0, ds0)

        first_stage(j, s0, dp0)
        p1[...] = jnp.zeros((blk, blk), BF16)
        ds1[...] = jnp.zeros((blk, blk), BF16)
        zero = jnp.zeros((blk, LANES), F32)
        steps = (nq - j + 1) // 2
        car = lax.fori_loop(0, steps, step, (zero, zero))
        dk, dv = last_stage(j + 2 * steps - 1, car, p1, ds1)
        dkv_ref[...] = jnp.where(lane < MLA_NOPE, dk, dv)
        dkr_ref[rows(j), :] += jnp.where(lane >= MLA_NOPE, dk, 0.0)

    seq_h = pl.BlockSpec((t, LANES), lambda h, j: (0, h))
    blk_h = pl.BlockSpec((blk, LANES), lambda h, j: (j, h))
    return pl.pallas_call(
        body, name=name, grid=(MLA_HEADS, nq),
        in_specs=[seq_h, seq_h, seq_h, blk_h, blk_h, _full_spec(bias.shape, 2)],
        out_specs=[seq_h, blk_h, pl.BlockSpec((t, LANES), lambda h, j: (0, 0))],
        out_shape=[jax.ShapeDtypeStruct((t, MLA_HEADS * LANES), F32), jax.ShapeDtypeStruct((t, MLA_HEADS * LANES), F32),
                   jax.ShapeDtypeStruct((t, LANES), F32)],
        scratch_shapes=[pltpu.VMEM((blk, blk), F32)] * 4 + [pltpu.VMEM((blk, blk), BF16)] * 4,
        compiler_params=_params(("arbitrary", "arbitrary")),
    )(qr, o, do, km, vb, bias)


def _rms_rows(row0, x, g):
    return (_rms(x, g),)


def _ssdmla_fwd(h, p, l, e, cos, sin):
    hn = _prenorm(h, p["mix_pre_g"][l], "sm_prenorm")
    proj = _mm(hn, p["w_in"][e], "nn", "sm_in")
    xc = _conv_fwd(proj, p["ssd_conv_w"][e], p["ssd_conv_b"][e], "ssd_conv", cw=SSD_GW, c0=PROJ_XBC // SSD_GW)
    xact = _rowwise("ssd_act", _ssd_act, [xc], [], [(SSD_CONV_CH, F32)])[0]
    dt, da = _rowwise("ssd_dt", _ssd_dt, [(proj, LANES, PROJ_DT // LANES)], [p["ssd_dt_bias"][e], p["ssd_a_log"][e]],
                      [(LANES, F32)] * 2)
    y, hsave = _ssd_scan(xact, dt, da, "ssd_scan")
    y_ssd = _rowwise("ssd_post", _ssd_post, [y, (xact, SSD_D_INNER, 0), (proj, SSD_D_INNER, 0)],
                     [p["ssd_d"][e], p["ssd_norm_g"][e]], [(SSD_D_INNER, BF16)])[0]
    cqn = _prenorm((proj, MLA_Q_RANK, PROJ_CQ // MLA_Q_RANK), p["mla_q_norm_g"][e], "mla_qnorm")
    ckvn = _prenorm((proj, MLA_KV_RANK, PROJ_CKV // MLA_KV_RANK), p["mla_kv_norm_g"][e], "mla_kvnorm")
    q = _mm(cqn, p["mla_w_q_up"][e], "nn", "mla_q_up")
    kv = _mm(ckvn, p["mla_w_kv_up"][e], "nn", "mla_kv_up")
    kr = _rowwise("mla_krope", lambda r0, x, c, s: (_rope(x, c, s),), [(proj, LANES, PROJ_KR // LANES), cos, sin], [],
                  [(LANES, F32)])[0]
    slots, tr = MLA_HEADS * LANES, _tile(h.shape[0], 264, 16)
    qr = _rowwise("mla_q_rope", lambda r0, a, c, s: (_rope(a, c, s),), [q, cos, sin], [], [(slots, BF16)], tr=tr)[0]
    km, vb = _rowwise("mla_key_slots", _key_slots, [kv, kr], [], [(slots, BF16)] * 2, tr=tr)
    o = _attn_fwd(qr, km, vb, "mla_attn")
    m1 = _mm(y_ssd, p["w_out_ssd"][e], "nn", "sm_out_ssd")
    m = _mm(o, p["w_out_att"][e], "nn", "sm_out_att", add=m1)
    h2 = _add_postnorm(h, [m], p["mix_post_g"][l], "sm_postnorm")
    return h2, (h, hn, proj, xc, xact, dt, da, y, hsave, y_ssd, cqn, ckvn, qr, km, vb, o, m)


def _ssdmla_bwd(dh, saved, p, l, e, cos, sin, grads):
    h, hn, proj, xc, xact, dt, da, y, hsave, y_ssd, cqn, ckvn, qr, km, vb, o, m = saved
    dm, grads["mix_post_g"][l] = _postnorm_bwd(m, p["mix_post_g"][l], dh, "sm_postnorm_bwd")
    grads["w_out_ssd"][e] = _mm(y_ssd, dm, "tn", "sm_out_ssd_dw")
    grads["w_out_att"][e] = _mm(o, dm, "tn", "sm_out_att_dw")
    dy_ssd = _mm(dm, p["w_out_ssd"][e], "nt", "sm_out_ssd_dx")
    do = _mm(dm, p["w_out_att"][e], "nt", "sm_out_att_dx")
    dqr, dkv, dkr = _attn_bwd(qr, km, vb, o, do, "mla_attn_bwd")
    dq = _rowwise("mla_q_rope_bwd", lambda r0, a, c, s: (_rope_t(a, c, s),), [dqr, cos, sin], [], [(MLA_HEADS * LANES, F32)],
                  tr=_tile(h.shape[0], 264, 16))[0]
    dkr_raw = _rowwise("mla_krope_bwd", lambda r0, d, c, s: (_rope_t(d, c, s),), [dkr, cos, sin], [], [(LANES, F32)])[0]
    grads["mla_w_q_up"][e] = _mm(cqn, dq, "tn", "mla_q_up_dw")
    dcqn = _mm(dq, p["mla_w_q_up"][e], "nt", "mla_q_up_dx")
    (dcq,), (grads["mla_q_norm_g"][e],) = _rowwise_vjp(
        "mla_qnorm_bwd", _rms_rows, [(proj, MLA_Q_RANK, PROJ_CQ // MLA_Q_RANK)], [p["mla_q_norm_g"][e]], [dcqn])
    grads["mla_w_kv_up"][e] = _mm(ckvn, dkv, "tn", "mla_kv_up_dw")
    dckvn = _mm(dkv, p["mla_w_kv_up"][e], "nt", "mla_kv_up_dx")
    (dckv,), (grads["mla_kv_norm_g"][e],) = _rowwise_vjp(
        "mla_kvnorm_bwd", _rms_rows, [(proj, MLA_KV_RANK, PROJ_CKV // MLA_KV_RANK)], [p["mla_kv_norm_g"][e]], [dckvn])
    (dy, dxskip, dz), (grads["ssd_d"][e], grads["ssd_norm_g"][e]) = _rowwise_vjp(
        "ssd_post_bwd", _ssd_post, [y, (xact, SSD_D_INNER, 0), (proj, SSD_D_INNER, 0)], [p["ssd_d"][e], p["ssd_norm_g"][e]], [dy_ssd])
    dxs, db, dc, ddt, dda = _ssd_scan_bwd(xact, dt, da, hsave, dy, dxskip, "ssd_scan_bwd")
    dxact = jnp.concatenate([dxs, db, dc], axis=1)
    (dxc,), _ = _rowwise_vjp("ssd_act_bwd", _ssd_act, [xc], [], [dxact])
    dxbc, grads["ssd_conv_w"][e], grads["ssd_conv_b"][e] = _conv_bwd(
        proj, p["ssd_conv_w"][e], dxc, "ssd_conv_bwd", cw=SSD_GW, c0=PROJ_XBC // SSD_GW)
    (ddtraw,), (grads["ssd_dt_bias"][e], grads["ssd_a_log"][e]) = _rowwise_vjp(
        "ssd_dt_bwd", _ssd_dt, [(proj, LANES, PROJ_DT // LANES)], [p["ssd_dt_bias"][e], p["ssd_a_log"][e]], [ddt, dda])
    dproj = jnp.concatenate([dz, dxbc, ddtraw, dcq, dckv, dkr_raw], axis=1)
    grads["w_in"][e] = _mm(hn, dproj, "tn", "sm_in_dw")
    dhn = _mm(dproj, p["w_in"][e], "nt", "sm_in_dx")
    dh, grads["mix_pre_g"][l] = _prenorm_bwd_add(h, p["mix_pre_g"][l], [dhn], dh, "sm_prenorm_bwd")
    return dh


GAINS = ("mix_pre_g", "mix_post_g", "mlp_pre_g", "mlp_post_g", "ssd_norm_g", "mla_q_norm_g", "mla_kv_norm_g", "ssd_conv_b", "rg_conv_b")
HEAD_VECS = ("ssd_dt_bias", "ssd_a_log", "ssd_d")
LRU_VECS = ("rg_b_a", "rg_b_i", "rg_lambda")
IN_DT_END = SSD_D_INNER + SSD_CONV_CH + SSD_HEADS
IN_KR = IN_DT_END + MLA_Q_RANK + MLA_KV_RANK


def _layout_params(w):
    p = {k: w[k][:, None, :] for k in GAINS}
    for k in HEAD_VECS:
        p[k] = jnp.pad(w[k], ((0, 0), (0, LANES - SSD_HEADS)))[:, None, :]
    for k in LRU_VECS:
        p[k] = w[k].reshape(-1, LRU_BLOCKS, 1, LRU_BLOCK)
    for k in ("w_up", "w_down", "mla_w_kv_up", "rg_w_x", "rg_w_y", "rg_w_out"):
        p[k] = w[k] if isinstance(w[k], list) else w[k].astype(BF16)
    for k in ("ssd_conv_w", "rg_conv_w", "rg_w_a", "rg_w_i"):
        p[k] = w[k]
    wi = w["w_in"]

    def zcols(n):
        return jnp.zeros(wi.shape[:2] + (n,), wi.dtype)

    p["w_in"] = jnp.concatenate([wi[..., :IN_DT_END], zcols(PROJ_CQ - IN_DT_END), wi[..., IN_DT_END:IN_KR], zcols(ROPE_LO),
                                 wi[..., IN_KR:], zcols(LANES - ROPE_HI)], axis=-1).astype(BF16)
    wq = w["mla_w_q_up"].reshape(-1, MLA_Q_RANK, MLA_HEADS, MLA_NOPE + MLA_ROPE)
    p["mla_w_q_up"] = jnp.pad(wq, ((0, 0), (0, 0), (0, 0), (0, LANES - MLA_NOPE - MLA_ROPE))).reshape(-1, MLA_Q_RANK, MLA_HEADS * LANES).astype(BF16)
    wo = w["w_out_ab"]
    p["w_out_ssd"] = wo[:, :SSD_D_INNER].astype(BF16)
    wa = wo[:, SSD_D_INNER:].reshape(-1, MLA_HEADS, MLA_V, D_MODEL)
    p["w_out_att"] = jnp.pad(wa, ((0, 0), (0, 0), (LANES - MLA_V, 0), (0, 0))).reshape(-1, MLA_HEADS * LANES, D_MODEL).astype(BF16)
    return p


def _rope_tables(t):
    pos = (jnp.arange(t) - PAD).astype(F32)
    inv = ROPE_BASE ** (-jnp.arange(0, MLA_ROPE, 2, dtype=F32) / MLA_ROPE)
    ang = pos[:, None] * inv[None, :]
    c, s = jnp.cos(ang), jnp.sin(ang)
    one, zero = jnp.ones((t, MLA_NOPE), F32), jnp.zeros((t, MLA_NOPE), F32)
    tail = LANES - ROPE_HI
    return (jnp.concatenate([one, c, c, one[:, :tail]], axis=1), jnp.concatenate([zero, -s, s, zero[:, :tail]], axis=1))


GRAD_KEYS = GAINS + HEAD_VECS + LRU_VECS + ("w_up", "w_down", "mla_w_kv_up", "rg_w_x", "rg_w_y", "rg_w_out", "ssd_conv_w",
                                            "rg_conv_w", "rg_w_a", "rg_w_i", "w_in", "mla_w_q_up", "w_out_ssd", "w_out_att")


def _device_step(x, meta, target, p):
    t = PAD + N_META + x.shape[0]
    cos, sin = _rope_tables(t)
    h = jnp.concatenate([jnp.zeros((PAD, D_MODEL), F32), meta, x], axis=0)
    n_even, n_odd = (DEPTH + 1) // 2, DEPTH // 2
    saved = []
    for l in range(DEPTH):
        if l % 2 == 0:
            h, sm = _ssdmla_fwd(h, p, l, l // 2, cos, sin)
        else:
            h, sm = _rglru_fwd(h, p, l, l // 2)
        h, sp = _mlp_fwd(h, p, l)
        saved.append((sm, sp))
    sq, dh = _loss_and_grad(h, target, "loss")
    per_layer = {"mix_pre_g": DEPTH, "mix_post_g": DEPTH, "mlp_pre_g": DEPTH, "mlp_post_g": DEPTH, "w_up": DEPTH, "w_down": DEPTH}
    grads = {k: [None] * per_layer.get(k, n_odd if k.startswith("rg_") else n_even) for k in GRAD_KEYS}
    for l in reversed(range(DEPTH)):
        sm, sp = saved[l]
        dh = _mlp_bwd(dh, sp, p, l, grads)
        if l % 2 == 0:
            dh = _ssdmla_bwd(dh, sm, p, l, l // 2, cos, sin, grads)
        else:
            dh = _rglru_bwd(dh, sm, p, l, l // 2, grads)
    return sq, dh, grads


MESH = pl.DeviceIdType.MESH
ANY = pl.BlockSpec(memory_space=pl.ANY)


def _mesh_pos():
    return lax.axis_index("x"), lax.axis_index("y"), lax.axis_index("c")


def _other_chips(x, y):
    return [(1 - x, y), (x, 1 - y), (1 - x, 1 - y)]


def _remote(src, dst, send_sems, recv_sems, k, to):
    return pltpu.make_async_remote_copy(src_ref=src, dst_ref=dst, send_sem=send_sems.at[k], recv_sem=recv_sems.at[k],
                                        device_id=to, device_id_type=MESH)


def _gather_chips(srcs, name):
    n = len(srcs)
    nc = N_CHIPS - 1

    def body(*refs):
        src_refs, out_refs = refs[:n], refs[n:2 * n]
        send_sems, recv_sems, local_sems, own_sems = refs[2 * n:]
        x, y, c = _mesh_pos()
        sib = (x, y, 1 - c)
        me = 2 * x + y
        chips = _other_chips(x, y)

        def half(ref, hc):
            h = ref.shape[0] // 2
            return ref.at[pl.ds(hc * h, h)]

        mine = [_remote(s, o.at[me], local_sems, own_sems, t, sib) for t, (s, o) in enumerate(zip(src_refs, out_refs))]
        for cp in mine:
            cp.start()
        first = [_remote(half(s, c), half(o.at[me], c), send_sems, recv_sems, 2 * nc * t + j, (cx, cy, c))
                 for t, (s, o) in enumerate(zip(src_refs, out_refs)) for j, (cx, cy) in enumerate(chips)]
        for cp in first:
            cp.start()
        passed = []
        for t, (s, o) in enumerate(zip(src_refs, out_refs)):
            for j, (cx, cy) in enumerate(chips):
                slot = half(o.at[2 * cx + cy], c)
                _remote(half(s, c), slot, send_sems, recv_sems, 2 * nc * t + j, (cx, cy, c)).wait_recv()
                passed.append(_remote(slot, slot, send_sems, recv_sems, 2 * nc * t + nc + j, sib))
                passed[-1].start()
        for t, (s, o) in enumerate(zip(src_refs, out_refs)):
            for j, (cx, cy) in enumerate(chips):
                _remote(half(s, c), half(o.at[2 * cx + cy], 1 - c), send_sems, recv_sems, 2 * nc * t + nc + j, sib).wait_recv()
        for cp in mine:
            cp.wait_recv()
        for cp in first + passed + mine:
            cp.wait_send()

    return pl.pallas_call(
        body, name=name, in_specs=[ANY] * n, out_specs=[ANY] * n,
        out_shape=[jax.ShapeDtypeStruct((N_CHIPS,) + s.shape, s.dtype) for s in srcs],
        scratch_shapes=[pltpu.SemaphoreType.DMA((2 * nc * n,)), pltpu.SemaphoreType.DMA((2 * nc * n,)),
                        pltpu.SemaphoreType.DMA((n,)), pltpu.SemaphoreType.DMA((n,))],
    )(*srcs)


def _pair_exchange(gs, name):
    n = len(gs)

    def body(*refs):
        g_refs, o_refs, send_sems, recv_sems = refs[:n], refs[n:2 * n], refs[2 * n], refs[2 * n + 1]
        x, y, c = _mesh_pos()
        sib = (x, y, 1 - c)
        cps = [_remote(g.at[pl.ds(0, N_CHIPS), 1 - c], o, send_sems, recv_sems, t, sib) for t, (g, o) in enumerate(zip(g_refs, o_refs))]
        for cp in cps:
            cp.start()
        for cp in cps:
            cp.wait_recv()
        for cp in cps:
            cp.wait_send()

    return pl.pallas_call(
        body, name=name, in_specs=[ANY] * n, out_specs=[ANY] * n,
        out_shape=[jax.ShapeDtypeStruct((g.shape[0],) + g.shape[2:], g.dtype) for g in gs],
        scratch_shapes=[pltpu.SemaphoreType.DMA((n,)), pltpu.SemaphoreType.DMA((n,))],
    )(*gs)


def _chip_exchange(ps, groups, name):
    n, ng = len(ps), len(groups)
    nc = N_CHIPS - 1

    def body(*refs):
        p_refs, q_refs = refs[:n], refs[n:n + ng]
        send_sems, recv_sems = refs[n + ng:]
        x, y, c = _mesh_pos()
        chips = _other_chips(x, y)
        slots = [(t, q, li) for q, idx in zip(q_refs, groups) for li, t in enumerate(idx)]
        sends = [_remote(p_refs[t].at[2 * cx + cy], q.at[j, li], send_sems, recv_sems, nc * t + j, (cx, cy, c))
                 for t, q, li in slots for j, (cx, cy) in enumerate(chips)]
        for cp in sends:
            cp.start()
        for cp in sends:
            cp.wait_recv()
        for cp in sends:
            cp.wait_send()

    return pl.pallas_call(
        body, name=name, in_specs=[ANY] * n, out_specs=[ANY] * ng,
        out_shape=[jax.ShapeDtypeStruct((nc, len(idx)) + ps[idx[0]].shape[1:], ps[idx[0]].dtype) for idx in groups],
        scratch_shapes=[pltpu.SemaphoreType.DMA((nc * n,)), pltpu.SemaphoreType.DMA((nc * n,))],
    )(*ps)


def _pair_share(fs, name):
    n = len(fs)

    def body(*refs):
        o_refs = refs[n:2 * n]
        send_sems, recv_sems = refs[2 * n:]
        x, y, c = _mesh_pos()
        sib = (x, y, 1 - c)
        cps = [_remote(o.at[pl.ds(0, o.shape[0]), c], o.at[pl.ds(0, o.shape[0]), c], send_sems, recv_sems, t, sib)
               for t, o in enumerate(o_refs)]
        for cp in cps:
            cp.start()
        for t, o in enumerate(o_refs):
            _remote(o.at[pl.ds(0, o.shape[0]), c], o.at[pl.ds(0, o.shape[0]), 1 - c], send_sems, recv_sems, t, sib).wait_recv()
        for cp in cps:
            cp.wait_send()

    return pl.pallas_call(
        body, name=name, in_specs=[ANY] * n, out_specs=[ANY] * n, input_output_aliases={t: t for t in range(n)},
        out_shape=[jax.ShapeDtypeStruct(f.shape, f.dtype) for f in fs],
        scratch_shapes=[pltpu.SemaphoreType.DMA((n,)), pltpu.SemaphoreType.DMA((n,))],
    )(*fs)


SUM_BLOCK = 512 * 1024


def _sum_pair(g, ra, c, name):
    n, _, h, w = g.shape
    tr = _tile(h, max(16, SUM_BLOCK // w), 16)

    def body(c_ref, g_ref, r_ref, o_ref):
        o_ref[...] = (g_ref[0] + r_ref[...]).astype(o_ref.dtype)

    return pl.pallas_call(
        body, name=name,
        grid_spec=pltpu.PrefetchScalarGridSpec(
            num_scalar_prefetch=1, grid=(n, h // tr),
            in_specs=[pl.BlockSpec((1, 1, tr, w), lambda s, i, cr: (s, cr[0], i, 0)), pl.BlockSpec((1, tr, w), lambda s, i, cr: (s, i, 0))],
            out_specs=pl.BlockSpec((1, tr, w), lambda s, i, cr: (s, i, 0))),
        out_shape=jax.ShapeDtypeStruct((n, h, w), BF16),
        compiler_params=_params(("parallel", "parallel")),
    )(c.reshape(1).astype(jnp.int32), g, ra)


def _sum_chips(ps, q, pos, name):
    nc, nl, h, w = q.shape
    tr = _tile(h, max(16, SUM_BLOCK // (w * nl)), 16)

    def body(x_ref, y_ref, c_ref, *refs):
        q_ref, o_ref = refs[nl], refs[nl + 1]
        for l in range(nl):
            acc = refs[l][0].astype(F32)
            for j in range(nc):
                acc = acc + q_ref[j, l].astype(F32)
            o_ref[l] = acc

    return pl.pallas_call(
        body, name=name,
        grid_spec=pltpu.PrefetchScalarGridSpec(
            num_scalar_prefetch=3, grid=(h // tr,),
            in_specs=[pl.BlockSpec((1, tr, w), lambda i, x, y, c: (2 * x[0] + y[0], i, 0))] * nl
            + [pl.BlockSpec((nc, nl, tr, w), lambda i, x, y, c: (0, 0, i, 0))],
            out_specs=pl.BlockSpec((nl, None, tr, w), lambda i, x, y, c: (0, c[0], i, 0))),
        out_shape=jax.ShapeDtypeStruct((nl, 2, h, w), F32),
        compiler_params=_params(("parallel",)),
    )(*pos, *ps, q)


def _adamw(g, w, m, v, name):
    def f(r0, gg, ww, mm, vv):
        m2 = ADAM_B1 * mm + (1.0 - ADAM_B1) * gg
        v2 = ADAM_B2 * vv + (1.0 - ADAM_B2) * jnp.square(gg)
        m_hat = m2 / (1.0 - ADAM_B1 ** ADAM_STEP)
        v_hat = v2 / (1.0 - ADAM_B2 ** ADAM_STEP)
        return gg, -ADAM_LR * (m_hat / (jnp.sqrt(v_hat) + ADAM_EPS) + ADAM_WD * ww), m2, v2

    return _rowwise(name, f, [g, w, m, v], [], [(g.shape[1], F32)] * 4, tr=_tile(g.shape[0], 512))


WEIGHTS = (
    ("meta_tokens", (N_META, D_MODEL), 1), ("mix_pre_g", (DEPTH, D_MODEL), None), ("mix_post_g", (DEPTH, D_MODEL), None),
    ("mlp_pre_g", (DEPTH, D_MODEL), None), ("mlp_post_g", (DEPTH, D_MODEL), None), ("w_up", (DEPTH, D_MODEL, D_FF), 2),
    ("w_down", (DEPTH, D_FF, D_MODEL), 1), ("w_in", (2, D_MODEL, 3248), 2), ("ssd_conv_w", (2, CONV_K, SSD_CONV_CH), 2),
    ("ssd_conv_b", (2, SSD_CONV_CH), None), ("ssd_dt_bias", (2, SSD_HEADS), None), ("ssd_a_log", (2, SSD_HEADS), None),
    ("ssd_d", (2, SSD_HEADS), None), ("ssd_norm_g", (2, SSD_D_INNER), None), ("mla_q_norm_g", (2, MLA_Q_RANK), None),
    ("mla_w_q_up", (2, MLA_Q_RANK, MLA_HEADS * (MLA_NOPE + MLA_ROPE)), 2), ("mla_kv_norm_g", (2, MLA_KV_RANK), None),
    ("mla_w_kv_up", (2, MLA_KV_RANK, MLA_HEADS * (MLA_NOPE + MLA_V)), 2), ("w_out_ab", (2, SSD_D_INNER + MLA_HEADS * MLA_V, D_MODEL), 1),
    ("rg_w_x", (2, D_MODEL, LRU_WIDTH), 2), ("rg_w_y", (2, D_MODEL, LRU_WIDTH), 2), ("rg_conv_w", (2, CONV_K, LRU_WIDTH), 2),
    ("rg_conv_b", (2, LRU_WIDTH), 1), ("rg_w_a", (2, LRU_BLOCKS, LRU_BLOCK, LRU_BLOCK), None), ("rg_b_a", (2, LRU_WIDTH), 1),
    ("rg_w_i", (2, LRU_BLOCKS, LRU_BLOCK, LRU_BLOCK), None), ("rg_b_i", (2, LRU_WIDTH), 1), ("rg_lambda", (2, LRU_WIDTH), 1),
    ("rg_w_out", (2, LRU_WIDTH, D_MODEL), 1),
)
BIG = {"w_up": "col", "w_down": "row", "w_in": "col", "mla_w_q_up": "col", "mla_w_kv_up": "col", "w_out_ab": "row",
       "rg_w_x": "col", "rg_w_y": "col", "rg_w_out": "row"}
DIRECT = ("w_up", "w_down")
FLAT_QUANTUM = 2 * 16 * LANES
TABLE = {name: (shape, d) for name, shape, d in WEIGHTS}
SMALL_SHARDED = tuple(name for name, _, d in WEIGHTS if d is not None and name not in BIG)
REPLICATED = tuple(name for name, _, d in WEIGHTS if d is None)


def _chips_to_full(a, kind):
    if kind == "col":
        return jnp.moveaxis(a, 0, 2).reshape(a.shape[1], a.shape[2], -1)
    return jnp.moveaxis(a, 0, 1).reshape(a.shape[1], -1, a.shape[3])


def _full_to_chips(g, kind):
    if kind == "col":
        return jnp.moveaxis(g.reshape(g.shape[0], N_CHIPS, -1), 1, 0)
    return g.reshape(N_CHIPS, -1, g.shape[1])


def _chips_to_full_1(pc, kind):
    return jnp.moveaxis(pc, 0, 1).reshape(pc.shape[1], -1) if kind == "col" else pc.reshape(-1, pc.shape[2])


def _shard_shape(shape, d):
    return shape[:d] + (shape[d] // N_CHIPS,) + shape[d + 1:]


def _shard_major(full, d):
    s = full.shape
    return jnp.moveaxis(full.reshape(s[:d] + (N_CHIPS, s[d] // N_CHIPS) + s[d + 1:]), d, 0).reshape(N_CHIPS, -1)


def _from_shard_major(a, shape, d):
    ss = _shard_shape(shape, d)
    return jnp.moveaxis(a.reshape((N_CHIPS,) + ss), 0, d).reshape(shape)


def _pad_cols(a, quantum):
    n = a.shape[-1]
    return jnp.pad(a, [(0, 0)] * (a.ndim - 1) + [(0, -n % quantum)])


def _gather_weights(w):
    small = _pad_cols(jnp.concatenate([w[n].reshape(-1) for n in SMALL_SHARDED]), FLAT_QUANTUM).reshape(2, -1, LANES)
    outs = _gather_chips([w[n].astype(BF16) for n in BIG] + [small], "gather_weights")
    full = {n: w[n] for n in REPLICATED}
    for name, a in zip(BIG, outs):
        if name in DIRECT:
            full[name] = [Gathered(a, BIG[name], l) for l in range(a.shape[1])]
        else:
            full[name] = _chips_to_full(a, BIG[name])
    got, off = outs[-1].reshape(N_CHIPS, -1), 0
    for name in SMALL_SHARDED:
        shape, d = TABLE[name]
        n = int(np.prod(_shard_shape(shape, d)))
        full[name] = _from_shard_major(got[:, off:off + n], shape, d)
        off += n
    return full


def _big_pieces(g):
    def w_in(a):
        return jnp.concatenate([a[:, :IN_DT_END], a[:, PROJ_CQ:PROJ_KR], a[:, PROJ_KR + ROPE_LO:PROJ_KR + ROPE_HI]], axis=1)

    def q_up(a):
        return a.reshape(MLA_Q_RANK, MLA_HEADS, LANES)[:, :, :MLA_NOPE + MLA_ROPE].reshape(MLA_Q_RANK, -1)

    def out_ab(s, a):
        return jnp.concatenate([s, a.reshape(MLA_HEADS, LANES, D_MODEL)[:, LANES - MLA_V:, :].reshape(-1, D_MODEL)], axis=0)

    full = {"w_down": g["w_down"], "w_in": [w_in(a) for a in g["w_in"]], "mla_w_q_up": [q_up(a) for a in g["mla_w_q_up"]],
            "mla_w_kv_up": g["mla_w_kv_up"], "w_out_ab": [out_ab(s, a) for s, a in zip(g["w_out_ssd"], g["w_out_att"])],
            "rg_w_x": g["rg_w_x"], "rg_w_y": g["rg_w_y"], "rg_w_out": g["rg_w_out"]}
    return {name: (g[name] if name == "w_up" else [_full_to_chips(a, BIG[name]) for a in full[name]]) for name in BIG}


def _small_grads(g, dh):
    out = {k: jnp.stack(g[k])[:, 0, :] for k in GAINS}
    for k in HEAD_VECS:
        out[k] = jnp.stack(g[k])[:, 0, :SSD_HEADS]
    for k in LRU_VECS:
        out[k] = jnp.stack(g[k]).reshape(-1, LRU_WIDTH)
    for k in ("ssd_conv_w", "rg_conv_w", "rg_w_a", "rg_w_i"):
        out[k] = jnp.stack(g[k])
    out["meta_tokens"] = dh[PAD:PAD + N_META]
    return out


def _natural_grads(g, dh):
    out = _small_grads(g, dh)
    for name, pcs in _big_pieces(g).items():
        out[name] = jnp.stack([_chips_to_full_1(pc, BIG[name]) for pc in pcs])
    return out


def _reduce_grads(g, dh, c):
    big, small = _big_pieces(g), _small_grads(g, dh)
    pieces, groups = [], []
    for name in BIG:
        groups.append(list(range(len(pieces), len(pieces) + len(big[name]))))
        pieces += [pc.reshape(N_CHIPS, 2, pc.shape[1] // 2, pc.shape[2]) for pc in big[name]]
    sharded = jnp.concatenate([_shard_major(small[n], TABLE[n][1]) for n in SMALL_SHARDED], axis=1)
    rep = _pad_cols(jnp.concatenate([small[n].reshape(-1) for n in REPLICATED]), N_CHIPS * FLAT_QUANTUM)
    n_sh, n_rep = sharded.shape[1], rep.shape[0] // N_CHIPS
    flat = _pad_cols(jnp.concatenate([sharded, rep.reshape(N_CHIPS, n_rep)], axis=1), FLAT_QUANTUM)
    groups.append([len(pieces)])
    pieces.append(flat.reshape(N_CHIPS, 2, -1, LANES))
    ras = _pair_exchange(pieces, "grads_pair_exchange")
    ps = [_sum_pair(a, ra, c, "grads_pair_sum") for a, ra in zip(pieces, ras)]
    qs = _chip_exchange(ps, groups, "grads_chip_exchange")
    pos = [lax.axis_index(a).reshape(1).astype(jnp.int32) for a in ("x", "y", "c")]
    outs = _pair_share([_sum_chips([ps[t] for t in idx], q, pos, "grads_chip_sum") for idx, q in zip(groups, qs)], "grads_pair_share")
    out = {name: o.reshape(o.shape[0], -1, o.shape[3]) for name, o in zip(BIG, outs)}
    f = outs[-1].reshape(-1)
    rep_all = _gather_chips([f[n_sh:n_sh + n_rep].reshape(2, -1, LANES)], "grads_gather_replicated")[0].reshape(-1)
    off = 0
    for name in SMALL_SHARDED:
        ss = _shard_shape(*TABLE[name])
        n = int(np.prod(ss))
        out[name] = f[off:off + n].reshape(ss)
        off += n
    off = 0
    for name in REPLICATED:
        shape = TABLE[name][0]
        n = int(np.prod(shape))
        out[name] = rep_all[off:off + n].reshape(shape)
        off += n
    return out


def kernel(x, meta_tokens, mix_pre_g, mix_post_g, mlp_pre_g, mlp_post_g, w_up, w_down, w_in, ssd_conv_w, ssd_conv_b, ssd_dt_bias, ssd_a_log, ssd_d, ssd_norm_g, mla_q_norm_g, mla_w_q_up, mla_kv_norm_g, mla_w_kv_up, w_out_ab, rg_w_x, rg_w_y, rg_conv_w, rg_conv_b, rg_w_a, rg_b_a, rg_w_i, rg_b_i, rg_lambda, rg_w_out, loss_target, m_meta_tokens, m_mix_pre_g, m_mix_post_g, m_mlp_pre_g, m_mlp_post_g, m_w_up, m_w_down, m_w_in, m_ssd_conv_w, m_ssd_conv_b, m_ssd_dt_bias, m_ssd_a_log, m_ssd_d, m_ssd_norm_g, m_mla_q_norm_g, m_mla_w_q_up, m_mla_kv_norm_g, m_mla_w_kv_up, m_w_out_ab, m_rg_w_x, m_rg_w_y, m_rg_conv_w, m_rg_conv_b, m_rg_w_a, m_rg_b_a, m_rg_w_i, m_rg_b_i, m_rg_lambda, m_rg_w_out, v_meta_tokens, v_mix_pre_g, v_mix_post_g, v_mlp_pre_g, v_mlp_post_g, v_w_up, v_w_down, v_w_in, v_ssd_conv_w, v_ssd_conv_b, v_ssd_dt_bias, v_ssd_a_log, v_ssd_d, v_ssd_norm_g, v_mla_q_norm_g, v_mla_w_q_up, v_mla_kv_norm_g, v_mla_w_kv_up, v_w_out_ab, v_rg_w_x, v_rg_w_y, v_rg_conv_w, v_rg_conv_b, v_rg_w_a, v_rg_b_a, v_rg_w_i, v_rg_b_i, v_rg_lambda, v_rg_w_out):
    names = [n for n, _, _ in WEIGHTS]
    w = dict(zip(names, (meta_tokens, mix_pre_g, mix_post_g, mlp_pre_g, mlp_post_g, w_up, w_down, w_in, ssd_conv_w, ssd_conv_b, ssd_dt_bias, ssd_a_log, ssd_d, ssd_norm_g, mla_q_norm_g, mla_w_q_up, mla_kv_norm_g, mla_w_kv_up, w_out_ab, rg_w_x, rg_w_y, rg_conv_w, rg_conv_b, rg_w_a, rg_b_a, rg_w_i, rg_b_i, rg_lambda, rg_w_out)))
    m = dict(zip(names, (m_meta_tokens, m_mix_pre_g, m_mix_post_g, m_mlp_pre_g, m_mlp_post_g, m_w_up, m_w_down, m_w_in, m_ssd_conv_w, m_ssd_conv_b, m_ssd_dt_bias, m_ssd_a_log, m_ssd_d, m_ssd_norm_g, m_mla_q_norm_g, m_mla_w_q_up, m_mla_kv_norm_g, m_mla_w_kv_up, m_w_out_ab, m_rg_w_x, m_rg_w_y, m_rg_conv_w, m_rg_conv_b, m_rg_w_a, m_rg_b_a, m_rg_w_i, m_rg_b_i, m_rg_lambda, m_rg_w_out)))
    v = dict(zip(names, (v_meta_tokens, v_mix_pre_g, v_mix_post_g, v_mlp_pre_g, v_mlp_post_g, v_w_up, v_w_down, v_w_in, v_ssd_conv_w, v_ssd_conv_b, v_ssd_dt_bias, v_ssd_a_log, v_ssd_d, v_ssd_norm_g, v_mla_q_norm_g, v_mla_w_q_up, v_mla_kv_norm_g, v_mla_w_kv_up, v_w_out_ab, v_rg_w_x, v_rg_w_y, v_rg_conv_w, v_rg_conv_b, v_rg_w_a, v_rg_b_a, v_rg_w_i, v_rg_b_i, v_rg_lambda, v_rg_w_out)))
    full = _gather_weights(w)
    p = _layout_params({k: a for k, a in full.items() if k != "meta_tokens"})
    sq, dh, grads = _device_step(x[0], full["meta_tokens"], loss_target[0], p)
    loss = lax.psum(0.5 * sq[0, 0] / D_MODEL, ("x", "y", "c"))
    g = _reduce_grads(grads, dh, lax.axis_index("c"))
    grad, delta, new_m, new_v = {}, {}, {}, {}
    for name in names:
        shape = g[name].shape
        two_d = (int(np.prod(shape[:-1])), shape[-1])
        res = _adamw(g[name].reshape(two_d), w[name].reshape(two_d), m[name].reshape(two_d), v[name].reshape(two_d), "adamw")
        grad[name], delta[name], new_m[name], new_v[name] = (r.reshape(shape) for r in res)
    grad_x = dh[PAD + N_META:][None]
    return (loss, grad_x, *[grad[n] for n in names], *[delta[n] for n in names], *[new_m[n] for n in names], *[new_v[n] for n in names])
```

```python
import functools

import jax
import jax.numpy as jnp
import numpy as np
from jax import lax
from jax.experimental import pallas as pl
from jax.experimental.pallas import tpu as pltpu

F32 = jnp.float32
BF16 = jnp.bfloat16

D_MODEL = 1024
DEPTH = 4
N_META = 16
CHUNK = 128
PAD = CHUNK - N_META
EPS = 1e-6
SSD_HEADS = 16
SSD_HEAD_DIM = 64
SSD_D_INNER = SSD_HEADS * SSD_HEAD_DIM
SSD_GROUPS = 2
SSD_STATE = 128
SSD_CONV_CH = SSD_D_INNER + 2 * SSD_GROUPS * SSD_STATE
MLA_HEADS = 16
MLA_NOPE = 64
MLA_ROPE = 32
MLA_V = 64
MLA_Q_RANK = 384
MLA_KV_RANK = 256
ROPE_BASE = 10000.0
LRU_WIDTH = 1280
LRU_BLOCKS = 10
LRU_BLOCK = 128
LRU_C = 8.0
D_FF = 4 * D_MODEL
ADAM_LR, ADAM_B1, ADAM_B2, ADAM_EPS, ADAM_WD, ADAM_STEP = 0.001, 0.9, 0.999, 1e-08, 0.01, 10

LANES = 128
VMEM_LIMIT = 56 * 1024 * 1024
HEAD_SLOT = 128
PROJ_Z, PROJ_XBC, PROJ_DT, PROJ_CQ, PROJ_CKV, PROJ_KR = 0, 1024, 2560, 2688, 3072, 3328
PROJ_W = 3456


def _tile(n, cap, mult=8):
    for t in range(min(n, cap), 0, -1):
        if n % t == 0 and t % mult == 0:
            return t
    return n


def _params(sem):
    return pltpu.CompilerParams(dimension_semantics=sem, vmem_limit_bytes=VMEM_LIMIT)


def _full_spec(shape, ngrid):
    nd = len(shape)
    if ngrid == 1:
        return pl.BlockSpec(shape, lambda i: (0,) * nd)
    if ngrid == 2:
        return pl.BlockSpec(shape, lambda i, j: (0,) * nd)
    return pl.BlockSpec(shape, lambda i, j, k: (0,) * nd)


_DIMS = {"nn": (((1,), (0,)), ((), ())), "nt": (((1,), (1,)), ((), ())), "tn": (((0,), (0,)), ((), ()))}


class Gathered:
    def __init__(self, arr, kind, layer):
        self.arr, self.kind, self.layer = arr, kind, layer
        _, _, r, c = arr.shape
        self.shape = (r, N_CHIPS * c) if kind == "col" else (N_CHIPS * r, c)


N_CHIPS = 4


def _mm(a, b, mode, name, out_dtype=F32, add=None, out_chip_major=False, extra=(), post=None, out_dtypes=None):
    if mode == "nn":
        (m, kc), (_, n) = a.shape, b.shape
    elif mode == "nt":
        (m, kc), (n, _) = a.shape, b.shape
    else:
        (kc, m), (_, n) = a.shape, b.shape
    tm = _tile(m, 1024, LANES) if mode == "tn" else _tile(m, 1056, 16)
    tn = _tile(n // N_CHIPS if out_chip_major else n, 1280, LANES)
    tk = _tile(kc, 1024 if mode != "tn" else 1408, LANES)
    nk = kc // tk
    if mode == "tn":
        a_spec = pl.BlockSpec((tk, tm), lambda i, j, k: (k, i))
    else:
        a_spec = pl.BlockSpec((tm, tk), lambda i, j, k: (i, k))
    b_arr = b
    if isinstance(b, Gathered):
        b_arr, layer = b.arr, b.layer
        sr, sc = b.arr.shape[2:]
        br, bc = (tk, tn) if mode == "nn" else (tn, tk)
        assert mode in ("nn", "nt") and sr % br == 0 and sc % bc == 0

        def b_map(i, j, k):
            r, c = (k, j) if mode == "nn" else (j, k)
            if b.kind == "col":
                return ((c * bc) // sc, layer, r, ((c * bc) % sc) // bc)
            return ((r * br) // sr, layer, ((r * br) % sr) // br, c)

        b_spec = pl.BlockSpec((None, None, br, bc), b_map)
    elif mode == "nt":
        b_spec = pl.BlockSpec((tn, tk), lambda i, j, k: (j, k))
    else:
        b_spec = pl.BlockSpec((tk, tn), lambda i, j, k: (k, j))
    dims = _DIMS[mode]
    if out_chip_major:
        ns = n // N_CHIPS
        o_spec = pl.BlockSpec((None, tm, tn), lambda i, j, k: ((j * tn) // ns, i, ((j * tn) % ns) // tn))
        o_shape = jax.ShapeDtypeStruct((N_CHIPS, m, ns), out_dtype)
    else:
        o_spec = pl.BlockSpec((tm, tn), lambda i, j, k: (i, j))
        o_shape = jax.ShapeDtypeStruct((m, n), out_dtype)
    extra = list(extra) + ([add] if add is not None else [])
    if add is not None:
        post = lambda v, x: (v + x,)
    nx = len(extra)
    out_dtypes = out_dtypes or [out_dtype]
    no = len(out_dtypes)

    def body(a_ref, b_ref, *rest):
        o_refs, acc = rest[nx:nx + no], rest[nx + no:]
        p = lax.dot_general(a_ref[...].astype(BF16), b_ref[...].astype(BF16), dims, preferred_element_type=F32)

        def emit(v):
            res = post(v, *[r[...] for r in rest[:nx]]) if post else (v,)
            for o_ref, r in zip(o_refs, res):
                o_ref[...] = r.astype(o_ref.dtype)

        if nk == 1:
            emit(p)
        else:
            k = pl.program_id(2)

            @pl.when(k == 0)
            def _():
                acc[0][...] = p

            @pl.when(k > 0)
            def _():
                acc[0][...] += p

            @pl.when(k == nk - 1)
            def _():
                emit(acc[0][...])

    res = pl.pallas_call(
        body, name=name, grid=(m // tm, n // tn, nk),
        in_specs=[a_spec, b_spec] + [o_spec] * nx, out_specs=[o_spec] * no,
        out_shape=[jax.ShapeDtypeStruct(o_shape.shape, dt) for dt in out_dtypes],
        scratch_shapes=[pltpu.VMEM((tm, tn), F32)] if nk > 1 else [],
        compiler_params=_params(("parallel", "parallel", "arbitrary")),
    )(a, b_arr, *extra)
    return res[0] if no == 1 else res


def _rowarg(r):
    return r if isinstance(r, tuple) else (r, r.shape[1], 0)


def _rowspec(r, tr, ncol):
    _, w, cb = r
    if ncol > 1:
        return pl.BlockSpec((tr, w // ncol), lambda j, i: (i, j))
    return pl.BlockSpec((tr, w), lambda j, i: (i, cb))


def _rowwise(name, f, rows, params, outs, tr=None, ncol=1):
    rows = [_rowarg(r) for r in rows]
    t = rows[0][0].shape[0]
    tr = tr or _tile(t, 528)
    nr, npm = len(rows), len(params)

    def body(*refs):
        vals = [r[...] for r in refs[:nr]] + [(p[0] if ncol > 1 else p[...]) for p in refs[nr:nr + npm]]
        res = f(pl.program_id(1) * tr, *vals)
        for o_ref, v in zip(refs[nr + npm:], res):
            o_ref[...] = v.astype(o_ref.dtype)

    def pspec(p):
        if ncol > 1:
            return pl.BlockSpec((1,) + p.shape[1:], lambda j, i, n=p.ndim: (j,) + (0,) * (n - 1))
        return _full_spec(p.shape, 2)

    return pl.pallas_call(
        body, name=name, grid=(ncol, t // tr),
        in_specs=[_rowspec(r, tr, ncol) for r in rows] + [pspec(p) for p in params],
        out_specs=[pl.BlockSpec((tr, w // ncol), lambda j, i: (i, j)) for w, _ in outs],
        out_shape=[jax.ShapeDtypeStruct((t, w), dt) for w, dt in outs],
        compiler_params=_params(("parallel", "parallel")),
    )(*[r[0] for r in rows], *params)


def _rowwise_vjp(name, f, rows, params, cts, tr=None, ncol=1, row_dtypes=None):
    rows = [_rowarg(r) for r in rows]
    cts = [_rowarg(c) for c in cts]
    t = rows[0][0].shape[0]
    tr = tr or _tile(t, 528)
    nr, npm, nc = len(rows), len(params), len(cts)
    row_dtypes = row_dtypes or [F32] * nr

    def body(*refs):
        i = pl.program_id(1)
        vals = [r[...] for r in refs[:nr]] + [(p[0] if ncol > 1 else p[...]) for p in refs[nr:nr + npm]]
        ct = tuple(c[...].astype(F32) for c in refs[nr + npm:nr + npm + nc])
        _, vjp = jax.vjp(lambda *a: tuple(f(i * tr, *a)), *vals)
        g = vjp(ct)
        outs = refs[nr + npm + nc:]
        for o_ref, v in zip(outs[:nr], g[:nr]):
            o_ref[...] = v.astype(o_ref.dtype)
        pg = [(v[None] if ncol > 1 else v) for v in g[nr:]]

        @pl.when(i == 0)
        def _():
            for o_ref, v in zip(outs[nr:], pg):
                o_ref[...] = v

        @pl.when(i > 0)
        def _():
            for o_ref, v in zip(outs[nr:], pg):
                o_ref[...] += v

    def pspec(p):
        if ncol > 1:
            return pl.BlockSpec((1,) + p.shape[1:], lambda j, i, n=p.ndim: (j,) + (0,) * (n - 1))
        return _full_spec(p.shape, 2)

    res = pl.pallas_call(
        body, name=name, grid=(ncol, t // tr),
        in_specs=[_rowspec(r, tr, ncol) for r in rows] + [pspec(p) for p in params] + [_rowspec(c, tr, ncol) for c in cts],
        out_specs=[pl.BlockSpec((tr, w // ncol), lambda j, i: (i, j)) for _, w, _ in rows] + [pspec(p) for p in params],
        out_shape=[jax.ShapeDtypeStruct((t, w), dt) for (_, w, _), dt in zip(rows, row_dtypes)]
        + [jax.ShapeDtypeStruct(p.shape, F32) for p in params],
        compiler_params=_params(("parallel", "arbitrary")),
    )(*[r[0] for r in rows], *params, *[c[0] for c in cts])
    return res[:nr], res[nr:]


def _valid(row0, tr):
    return (row0 + lax.broadcasted_iota(jnp.int32, (tr, 1), 0)) >= PAD


def _rms(x, g):
    return x * lax.rsqrt(jnp.mean(x * x, axis=-1, keepdims=True) + EPS) * g


def _softplus(x):
    return jnp.where(x < -15.0, jnp.exp(x), jnp.maximum(x, 0.0) + jnp.log(1.0 + jnp.exp(-jnp.abs(x))))


def _neg_expm1(z):
    return jnp.where(z > -0.01, -z * (1.0 + z * (0.5 + z * (1.0 / 6.0))), 1.0 - jnp.exp(z))


def _prenorm(h, g, name):
    return _rowwise(name, lambda r0, x, gg: (_rms(x, gg),), [h], [g], [(_rowarg(h)[1], BF16)])[0]


def _add_postnorm(h, ms, g, name):
    def f(r0, x, *rest):
        return (x + _rms(functools.reduce(jnp.add, rest[:-1]), rest[-1]),)

    return _rowwise(name, f, [h] + list(ms), [g], [(h.shape[1], F32)])[0]


def _postnorm_bwd(m, g, dh, name):
    (dm,), (dg,) = _rowwise_vjp(name, lambda r0, mm, gg: (_rms(mm, gg),), [m], [g], [dh])
    return dm, dg


def _prenorm_bwd_add(h, g, dhns, dh, name):
    t, w = h.shape
    tr = _tile(t, 528)
    nd = len(dhns)

    def body(h_ref, g_ref, *refs):
        dh_ref, o_ref, dg_ref = refs[nd:]
        i = pl.program_id(0)
        _, vjp = jax.vjp(_rms, h_ref[...], g_ref[...])
        dhn = refs[0][...].astype(F32)
        for r in refs[1:nd]:
            dhn = dhn + r[...].astype(F32)
        dx, dg = vjp(dhn)
        o_ref[...] = dh_ref[...] + dx

        @pl.when(i == 0)
        def _():
            dg_ref[...] = dg

        @pl.when(i > 0)
        def _():
            dg_ref[...] += dg

    row = pl.BlockSpec((tr, w), lambda i: (i, 0))
    return pl.pallas_call(
        body, name=name, grid=(t // tr,), in_specs=[row, _full_spec(g.shape, 1)] + [row] * (nd + 1),
        out_specs=[row, _full_spec(g.shape, 1)],
        out_shape=[jax.ShapeDtypeStruct((t, w), F32), jax.ShapeDtypeStruct(g.shape, F32)],
        compiler_params=_params(("arbitrary",)),
    )(h, g, *dhns, dh)


def _loss_and_grad(h, target, name):
    t, w = h.shape
    nb = t // CHUNK

    def body(h_ref, t_ref, s_ref, dh_ref):
        i = pl.program_id(0)

        @pl.when(i == 0)
        def _():
            s_ref[...] = jnp.zeros_like(s_ref)
            dh_ref[...] = jnp.zeros_like(dh_ref)

        @pl.when(i > 0)
        def _():
            err = h_ref[...] - t_ref[...]
            s_ref[...] += jnp.sum(err * err)
            dh_ref[...] = err * (1.0 / w)

    return pl.pallas_call(
        body, name=name, grid=(nb,),
        in_specs=[pl.BlockSpec((CHUNK, w), lambda i: (i, 0)), pl.BlockSpec((CHUNK, w), lambda i: (jnp.maximum(i - 1, 0), 0))],
        out_specs=[_full_spec((1, LANES), 1), pl.BlockSpec((CHUNK, w), lambda i: (i, 0))],
        out_shape=[jax.ShapeDtypeStruct((1, LANES), F32), jax.ShapeDtypeStruct((t, w), F32)],
        compiler_params=_params(("arbitrary",)),
    )(h, target)


def _mlp_fwd(h, p, l):
    hn = _prenorm(h, p["mlp_pre_g"][l], "mlp_prenorm")
    a, u = _mm(hn, p["w_up"][l], "nn", "mlp_up", post=lambda v: (v, jnp.square(jnp.maximum(v, 0.0))), out_dtypes=[F32, BF16])
    d = _mm(u, p["w_down"][l], "nn", "mlp_down")
    h2 = _add_postnorm(h, [d], p["mlp_post_g"][l], "mlp_postnorm")
    return h2, (h, hn, a, u, d)


def _mlp_bwd(dh, saved, p, l, grads):
    h, hn, a, u, d = saved
    dd, grads["mlp_post_g"][l] = _postnorm_bwd(d, p["mlp_post_g"][l], dh, "mlp_postnorm_bwd")
    grads["w_down"][l] = _mm(u, dd, "tn", "mlp_down_dw")
    da = _mm(dd, p["w_down"][l], "nt", "mlp_down_dx", extra=[a], post=lambda v, x: (2.0 * jnp.maximum(x, 0.0) * v,),
             out_dtypes=[BF16])
    grads["w_up"][l] = _mm(hn, da, "tn", "mlp_up_dw", out_chip_major=True)
    dhn = _mm(da, p["w_up"][l], "nt", "mlp_up_dx")
    dh, grads["mlp_pre_g"][l] = _prenorm_bwd_add(h, p["mlp_pre_g"][l], [dhn], dh, "mlp_prenorm_bwd")
    return dh


def _dot(a, b, mode):
    return lax.dot_general(a.astype(BF16), b.astype(BF16), _DIMS[mode], preferred_element_type=F32)


@jax.custom_vjp
def _bnn(a, b):
    return _dot(a, b, "nn")


_bnn.defvjp(lambda a, b: (_dot(a, b, "nn"), (a, b)), lambda r, ct: (_dot(ct, r[1], "nt"), _dot(r[0], ct, "tn")))


@jax.custom_vjp
def _bnt(a, b):
    return _dot(a, b, "nt")


_bnt.defvjp(lambda a, b: (_dot(a, b, "nt"), (a, b)), lambda r, ct: (_dot(ct, r[1], "nn"), _dot(ct, r[0], "tn")))


@jax.custom_vjp
def _btn(a, b):
    return _dot(a, b, "tn")


_btn.defvjp(lambda a, b: (_dot(a, b, "tn"), (a, b)), lambda r, ct: (_dot(r[1], ct, "nt"), _dot(r[0], ct, "nn")))


CONV_K = 4
HALO = 8


def _conv_fwd(x, w, b, name, cw, c0=0):
    t, c = x.shape[0], w.shape[1]
    tr = _tile(t, 528)
    hb = tr // HALO

    def body(x_ref, halo_ref, w_ref, b_ref, o_ref, ext):
        i = pl.program_id(1)
        ext[pl.ds(0, HALO), :] = jnp.where(i > 0, halo_ref[...], 0.0)
        ext[pl.ds(HALO, tr), :] = x_ref[...]
        acc = jnp.broadcast_to(b_ref[...], (tr, cw))
        for k in range(CONV_K):
            acc = acc + w_ref[pl.ds(k, 1), :] * ext[pl.ds(HALO - (CONV_K - 1) + k, tr), :]
        o_ref[...] = acc

    return pl.pallas_call(
        body, name=name, grid=(c // cw, t // tr),
        in_specs=[pl.BlockSpec((tr, cw), lambda j, i: (i, c0 + j)),
                  pl.BlockSpec((HALO, cw), lambda j, i: (jnp.maximum(i * hb - 1, 0), c0 + j)),
                  pl.BlockSpec((CONV_K, cw), lambda j, i: (0, j)), pl.BlockSpec((1, cw), lambda j, i: (0, j))],
        out_specs=pl.BlockSpec((tr, cw), lambda j, i: (i, j)),
        out_shape=jax.ShapeDtypeStruct((t, c), F32),
        scratch_shapes=[pltpu.VMEM((tr + HALO, cw), F32)],
        compiler_params=_params(("parallel", "parallel")),
    )(x, x, w, b)


def _conv_bwd(x, w, dy, name, cw, c0=0):
    t, c = x.shape[0], w.shape[1]
    tr = _tile(t, 528)
    hb = tr // HALO
    nb = t // tr

    def body(x_ref, xh_ref, w_ref, dy_ref, dyh_ref, dx_ref, dw_ref, db_ref, xe, de):
        c = cw
        i = pl.program_id(1)
        xe[pl.ds(0, HALO), :] = jnp.where(i > 0, xh_ref[...], 0.0)
        xe[pl.ds(HALO, tr), :] = x_ref[...]
        de[pl.ds(0, tr), :] = dy_ref[...]
        de[pl.ds(tr, HALO), :] = jnp.where(i < nb - 1, dyh_ref[...], 0.0)
        dy = dy_ref[...]
        acc = jnp.zeros((tr, c), F32)
        dw = jnp.zeros((CONV_K, c), F32)
        rows = lax.broadcasted_iota(jnp.int32, (CONV_K, 1), 0)
        for k in range(CONV_K):
            acc = acc + w_ref[pl.ds(k, 1), :] * de[pl.ds(CONV_K - 1 - k, tr), :]
            dwk = jnp.sum(dy * xe[pl.ds(HALO - (CONV_K - 1) + k, tr), :], axis=0, keepdims=True)
            dw = dw + jnp.where(rows == k, dwk, 0.0)
        dx_ref[...] = jnp.where(_valid(i * tr, tr), acc, 0.0)
        db = jnp.sum(dy, axis=0, keepdims=True)

        @pl.when(i == 0)
        def _():
            dw_ref[...] = dw
            db_ref[...] = db

        @pl.when(i > 0)
        def _():
            dw_ref[...] += dw
            db_ref[...] += db

    row = pl.BlockSpec((tr, cw), lambda j, i: (i, j))
    return pl.pallas_call(
        body, name=name, grid=(c // cw, nb),
        in_specs=[pl.BlockSpec((tr, cw), lambda j, i: (i, c0 + j)),
                  pl.BlockSpec((HALO, cw), lambda j, i: (jnp.maximum(i * hb - 1, 0), c0 + j)),
                  pl.BlockSpec((CONV_K, cw), lambda j, i: (0, j)),
                  row, pl.BlockSpec((HALO, cw), lambda j, i: (jnp.minimum((i + 1) * hb, t // HALO - 1), j))],
        out_specs=[row, pl.BlockSpec((CONV_K, cw), lambda j, i: (0, j)), pl.BlockSpec((1, cw), lambda j, i: (0, j))],
        out_shape=[jax.ShapeDtypeStruct((t, c), F32), jax.ShapeDtypeStruct((CONV_K, c), F32), jax.ShapeDtypeStruct((1, c), F32)],
        scratch_shapes=[pltpu.VMEM((tr + HALO, cw), F32), pltpu.VMEM((tr + HALO, cw), F32)],
        compiler_params=_params(("parallel", "arbitrary")),
    )(x, x, w, dy, dy)


SUB = 8


def _lru_scan(a, u, name):
    t, c = a.shape
    tr = _tile(t, 528)

    def body(a_ref, u_ref, o_ref, carry):
        @pl.when(pl.program_id(0) == 0)
        def _():
            carry[...] = jnp.zeros_like(carry)

        rows = lax.broadcasted_iota(jnp.int32, (SUB, 1), 0)

        def step(k, cin):
            r = pl.multiple_of(k * SUB, SUB)
            av, uv = a_ref[pl.ds(r, SUB), :], u_ref[pl.ds(r, SUB), :]
            for d in (1, 2, 4):
                m = rows >= d
                uv = uv + av * jnp.where(m, pltpu.roll(uv, d, 0), 0.0)
                av = av * jnp.where(m, pltpu.roll(av, d, 0), 1.0)
            hv = uv + av * cin
            o_ref[pl.ds(r, SUB), :] = hv
            return jnp.broadcast_to(hv[SUB - 1:SUB, :], (SUB, c))

        carry[...] = lax.fori_loop(0, tr // SUB, step, carry[...])

    row = pl.BlockSpec((tr, c), lambda i: (i, 0))
    return pl.pallas_call(
        body, name=name, grid=(t // tr,), in_specs=[row, row], out_specs=row,
        out_shape=jax.ShapeDtypeStruct((t, c), F32), scratch_shapes=[pltpu.VMEM((SUB, c), F32)],
        compiler_params=_params(("arbitrary",)),
    )(a, u)


def _lru_scan_bwd(a, hs, dy, name):
    t, c = a.shape
    tr = _tile(t, 528)
    nb, nt = t // tr, tr // SUB

    def body(a_ref, h_ref, hh_ref, dy_ref, du_ref, da_ref, gcar, acar):
        i = pl.program_id(0)

        @pl.when(i == 0)
        def _():
            gcar[...] = jnp.zeros_like(gcar)
            acar[...] = jnp.zeros_like(acar)

        rows = lax.broadcasted_iota(jnp.int32, (SUB, 1), 0)
        hhalo = jnp.where(i < nb - 1, hh_ref[...], 0.0)

        def step(kk, car):
            gin, a_next_first = car
            k = nt - 1 - kk
            r = pl.multiple_of(k * SUB, SUB)
            av, hv, dv = a_ref[pl.ds(r, SUB), :], h_ref[pl.ds(r, SUB), :], dy_ref[pl.ds(r, SUB), :]
            rp = pl.multiple_of(jnp.maximum(k - 1, 0) * SUB, SUB)
            hp = jnp.where(k > 0, h_ref[pl.ds(rp, SUB), :], hhalo)
            cv = jnp.where(rows < SUB - 1, pltpu.roll(av, SUB - 1, 0), a_next_first)
            gv = dv
            for d in (1, 2, 4):
                m = rows < SUB - d
                gv = gv + cv * jnp.where(m, pltpu.roll(gv, SUB - d, 0), 0.0)
                cv = cv * jnp.where(m, pltpu.roll(cv, SUB - d, 0), 1.0)
            gv = gv + cv * gin
            hprev = jnp.where(rows >= 1, pltpu.roll(hv, 1, 0), jnp.broadcast_to(hp[SUB - 1:SUB, :], (SUB, c)))
            du_ref[pl.ds(r, SUB), :] = gv
            da_ref[pl.ds(r, SUB), :] = gv * hprev
            return jnp.broadcast_to(gv[0:1, :], (SUB, c)), jnp.broadcast_to(av[0:1, :], (SUB, c))

        g, af = lax.fori_loop(0, nt, step, (gcar[...], acar[...]))
        gcar[...] = g
        acar[...] = af

    hb = tr // SUB
    row = pl.BlockSpec((tr, c), lambda i: (nb - 1 - i, 0))
    halo = pl.BlockSpec((SUB, c), lambda i: (jnp.maximum((nb - 1 - i) * hb - 1, 0), 0))
    return pl.pallas_call(
        body, name=name, grid=(nb,), in_specs=[row, row, halo, row], out_specs=[row, row],
        out_shape=[jax.ShapeDtypeStruct((t, c), F32)] * 2,
        scratch_shapes=[pltpu.VMEM((SUB, c), F32), pltpu.VMEM((SUB, c), F32)],
        compiler_params=_params(("arbitrary",)),
    )(a, hs, hs, dy)


def _lru_gates(row0, xr, wa, ba, wi, bi, lam):
    r = jax.nn.sigmoid(_bnn(xr, wa) + ba)
    i = jax.nn.sigmoid(_bnn(xr, wi) + bi)
    log_a = -LRU_C * r * _softplus(-lam)
    u = jnp.sqrt(_neg_expm1(2.0 * log_a)) * (i * xr)
    return jnp.exp(log_a), jnp.where(_valid(row0, xr.shape[0]), u, 0.0)


def _lru_gate_out(row0, hs, yw):
    return (hs * jax.nn.gelu(yw),)


def _rglru_fwd(h, p, l, o):
    hn = _prenorm(h, p["mix_pre_g"][l], "rg_prenorm")
    xw = _mm(hn, p["rg_w_x"][o], "nn", "rg_in_x")
    yw = _mm(hn, p["rg_w_y"][o], "nn", "rg_in_y")
    xr = _conv_fwd(xw, p["rg_conv_w"][o], p["rg_conv_b"][o], "rg_conv", cw=LRU_WIDTH // 2)
    gp = [p["rg_w_a"][o], p["rg_b_a"][o], p["rg_w_i"][o], p["rg_b_i"][o], p["rg_lambda"][o]]
    a, u = _rowwise("rg_gates", _lru_gates, [xr], gp, [(LRU_WIDTH, F32)] * 2, ncol=LRU_BLOCKS)
    hs = _lru_scan(a, u, "rg_scan")
    hg = _rowwise("rg_gate_out", _lru_gate_out, [hs, yw], [], [(LRU_WIDTH, BF16)])[0]
    m = _mm(hg, p["rg_w_out"][o], "nn", "rg_out")
    h2 = _add_postnorm(h, [m], p["mix_post_g"][l], "rg_postnorm")
    return h2, (h, hn, xw, yw, xr, a, hs, hg, m)


def _rglru_bwd(dh, saved, p, l, o, grads):
    h, hn, xw, yw, xr, a, hs, hg, m = saved
    dm, grads["mix_post_g"][l] = _postnorm_bwd(m, p["mix_post_g"][l], dh, "rg_postnorm_bwd")
    grads["rg_w_out"][o] = _mm(hg, dm, "tn", "rg_out_dw")
    dhg = _mm(dm, p["rg_w_out"][o], "nt", "rg_out_dx")
    (dhs, dyw), _ = _rowwise_vjp("rg_gate_out_bwd", _lru_gate_out, [hs, yw], [], [dhg])
    du, da = _lru_scan_bwd(a, hs, dhs, "rg_scan_bwd")
    gp = [p["rg_w_a"][o], p["rg_b_a"][o], p["rg_w_i"][o], p["rg_b_i"][o], p["rg_lambda"][o]]
    (dxr,), gg = _rowwise_vjp("rg_gates_bwd", _lru_gates, [xr], gp, [da, du], ncol=LRU_BLOCKS)
    grads["rg_w_a"][o], grads["rg_b_a"][o], grads["rg_w_i"][o], grads["rg_b_i"][o], grads["rg_lambda"][o] = gg
    dxw, grads["rg_conv_w"][o], grads["rg_conv_b"][o] = _conv_bwd(xw, p["rg_conv_w"][o], dxr, "rg_conv_bwd", cw=LRU_WIDTH // 2)
    grads["rg_w_x"][o] = _mm(hn, dxw, "tn", "rg_in_x_dw")
    grads["rg_w_y"][o] = _mm(hn, dyw, "tn", "rg_in_y_dw")
    dhx = _mm(dxw, p["rg_w_x"][o], "nt", "rg_in_x_dx")
    dhy = _mm(dyw, p["rg_w_y"][o], "nt", "rg_in_y_dx")
    dh, grads["mix_pre_g"][l] = _prenorm_bwd_add(h, p["mix_pre_g"][l], [dhx, dhy], dh, "rg_prenorm_bwd")
    return dh


SSD_GW = SSD_D_INNER // SSD_GROUPS
SSD_GH = SSD_HEADS // SSD_GROUPS
XACT_B = SSD_D_INNER // SSD_STATE
XACT_C = XACT_B + SSD_GROUPS


def _hp(a, b, dims=_DIMS["nn"]):
    return lax.dot_general(a, b, dims, precision=lax.Precision.HIGHEST, preferred_element_type=F32)


def _ssd_chunk(xs, bm, cm, dt, da, ht, g):
    l = CHUNK
    ri = lax.broadcasted_iota(jnp.int32, (l, l), 0)
    ci = lax.broadcasted_iota(jnp.int32, (l, l), 1)
    causal = ri >= ci
    tri = causal.astype(F32)
    hr = lax.broadcasted_iota(jnp.int32, (LANES, SSD_GW), 0)
    hc = lax.broadcasted_iota(jnp.int32, (LANES, SSD_GW), 1)
    expand = (hr == g * SSD_GH + hc // SSD_HEAD_DIM).astype(F32)
    acs = _hp(tri, da)
    acs_t = _hp(da, tri, (((0,), (1,)), ((), ())))
    acs_e = _hp(acs, expand)
    x = xs * _hp(dt, expand)
    gmat = _bnt(cm, bm)
    lane = lax.broadcasted_iota(jnp.int32, (1, LANES), 1)
    sub = lax.broadcasted_iota(jnp.int32, (LANES, 1), 0)
    colhead = lax.broadcasted_iota(jnp.int32, (1, SSD_GW), 1) // SSD_HEAD_DIM
    y = _bnn(cm, ht) * jnp.exp(acs_e)
    for k in range(SSD_GH):
        hh = g * SSD_GH + k
        col = jnp.sum(jnp.where(lane == hh, acs, 0.0), axis=1, keepdims=True)
        row = jnp.sum(jnp.where(sub == hh, acs_t, 0.0), axis=0, keepdims=True)
        decay = jnp.exp(jnp.where(causal, col - row, -1e30))
        y = y + _bnn(gmat * decay, jnp.where(colhead == k, x, 0.0))
    last = lax.broadcasted_iota(jnp.int32, (l, 1), 0) == l - 1
    a_last = jnp.sum(jnp.where(last, acs_e, 0.0), axis=0, keepdims=True)
    st = _btn(bm, x * jnp.exp(a_last - acs_e))
    return y, ht * jnp.exp(a_last) + st


def _ssd_specs(nc, rev):
    def cc(c):
        return nc - 1 - c if rev else c

    return [pl.BlockSpec((CHUNK, SSD_GW), lambda c, g: (cc(c), g)),
            pl.BlockSpec((CHUNK, SSD_STATE), lambda c, g: (cc(c), XACT_B + g)),
            pl.BlockSpec((CHUNK, SSD_STATE), lambda c, g: (cc(c), XACT_C + g)),
            pl.BlockSpec((CHUNK, LANES), lambda c, g: (cc(c), 0)),
            pl.BlockSpec((CHUNK, LANES), lambda c, g: (cc(c), 0))]


def _ssd_scan(xact, dt, da, name):
    t = xact.shape[0]
    nc = t // CHUNK

    def body(xs_ref, b_ref, c_ref, dt_ref, da_ref, y_ref, hs_ref, state):
        c, g = pl.program_id(0), pl.program_id(1)

        @pl.when(c == 0)
        def _():
            state[g] = jnp.zeros((SSD_STATE, SSD_GW), F32)

        ht = state[g]
        hs_ref[0] = ht
        y, ht2 = _ssd_chunk(xs_ref[...], b_ref[...], c_ref[...], dt_ref[...], da_ref[...], ht, g)
        y_ref[...] = y
        state[g] = ht2

    return pl.pallas_call(
        body, name=name, grid=(nc, SSD_GROUPS), in_specs=_ssd_specs(nc, False),
        out_specs=[pl.BlockSpec((CHUNK, SSD_GW), lambda c, g: (c, g)),
                   pl.BlockSpec((1, SSD_STATE, SSD_GW), lambda c, g: (c * SSD_GROUPS + g, 0, 0))],
        out_shape=[jax.ShapeDtypeStruct((t, SSD_D_INNER), F32), jax.ShapeDtypeStruct((nc * SSD_GROUPS, SSD_STATE, SSD_GW), F32)],
        scratch_shapes=[pltpu.VMEM((SSD_GROUPS, SSD_STATE, SSD_GW), F32)],
        compiler_params=_params(("arbitrary", "arbitrary")),
    )(xact, xact, xact, dt, da)


def _ssd_scan_bwd(xact, dt, da, hsave, dy, dxskip, name):
    t = xact.shape[0]
    nc = t // CHUNK

    def body(xs_ref, b_ref, c_ref, dt_ref, da_ref, hs_ref, dy_ref, sk_ref, dxs_ref, db_ref, dc_ref, ddt_ref, dda_ref, dstate):
        c, g = pl.program_id(0), pl.program_id(1)

        @pl.when(c == 0)
        def _():
            dstate[g] = jnp.zeros((SSD_STATE, SSD_GW), F32)

        _, vjp = jax.vjp(lambda *a: _ssd_chunk(*a, g), xs_ref[...], b_ref[...], c_ref[...], dt_ref[...], da_ref[...], hs_ref[0])
        dxs, dbm, dcm, ddt, dda, dht = vjp((dy_ref[...], dstate[g]))
        dxs_ref[...] = dxs + sk_ref[...]
        db_ref[...] = dbm
        dc_ref[...] = dcm
        dstate[g] = dht

        @pl.when(g == 0)
        def _():
            ddt_ref[...] = ddt
            dda_ref[...] = dda

        @pl.when(g > 0)
        def _():
            ddt_ref[...] += ddt
            dda_ref[...] += dda

    grp = pl.BlockSpec((CHUNK, SSD_GW), lambda c, g: (nc - 1 - c, g))
    st = pl.BlockSpec((CHUNK, SSD_STATE), lambda c, g: (nc - 1 - c, g))
    hd = pl.BlockSpec((CHUNK, LANES), lambda c, g: (nc - 1 - c, 0))
    return pl.pallas_call(
        body, name=name, grid=(nc, SSD_GROUPS),
        in_specs=_ssd_specs(nc, True) + [pl.BlockSpec((1, SSD_STATE, SSD_GW), lambda c, g: ((nc - 1 - c) * SSD_GROUPS + g, 0, 0)), grp, grp],
        out_specs=[grp, st, st, hd, hd],
        out_shape=[jax.ShapeDtypeStruct((t, SSD_D_INNER), F32), jax.ShapeDtypeStruct((t, SSD_GROUPS * SSD_STATE), F32),
                   jax.ShapeDtypeStruct((t, SSD_GROUPS * SSD_STATE), F32), jax.ShapeDtypeStruct((t, LANES), F32),
                   jax.ShapeDtypeStruct((t, LANES), F32)],
        scratch_shapes=[pltpu.VMEM((SSD_GROUPS, SSD_STATE, SSD_GW), F32)],
        compiler_params=_params(("arbitrary", "arbitrary")),
    )(xact, xact, xact, dt, da, hsave, dy, dxskip)


def _ssd_act(row0, xc):
    return (jnp.where(_valid(row0, xc.shape[0]), jax.nn.silu(xc), 0.0),)


def _ssd_dt(row0, dtraw, dt_bias, a_log):
    dt = jnp.where(_valid(row0, dtraw.shape[0]), _softplus(dtraw + dt_bias), 0.0)
    return dt, dt * -jnp.exp(a_log)


def _ssd_post(row0, y, xs, z, d_skip, norm_g):
    hr = lax.broadcasted_iota(jnp.int32, (LANES, SSD_D_INNER), 0)
    hc = lax.broadcasted_iota(jnp.int32, (LANES, SSD_D_INNER), 1)
    expand = (hr == hc // SSD_HEAD_DIM).astype(F32)
    d_e = jnp.sum(_hp(jnp.broadcast_to(d_skip, (SUB, LANES)), expand), axis=0, keepdims=True) * (1.0 / SUB)
    return (_rms((y + xs * d_e) * jax.nn.silu(z), norm_g),)


ROPE_LO, ROPE_MID, ROPE_HI = MLA_NOPE, MLA_NOPE + MLA_ROPE // 2, MLA_NOPE + MLA_ROPE
ATT_SCALE = (MLA_NOPE + MLA_ROPE) ** -0.5


def _slot_lane(width):
    return lax.broadcasted_iota(jnp.int32, (1, width), 1) % LANES


def _swap_halves(x):
    width = x.shape[1]
    lane = _slot_lane(width)
    sw = jnp.where(lane < ROPE_MID, pltpu.roll(x, width - MLA_ROPE // 2, 1), pltpu.roll(x, MLA_ROPE // 2, 1))
    return jnp.where((lane >= ROPE_LO) & (lane < ROPE_HI), sw, 0.0)


def _rope(x, cos, sin):
    n = x.shape[1] // LANES
    return x * jnp.tile(cos, (1, n)) + _swap_halves(x) * jnp.tile(sin, (1, n))


def _rope_t(dy, cos, sin):
    n = dy.shape[1] // LANES
    return dy * jnp.tile(cos, (1, n)) + _swap_halves(dy * jnp.tile(sin, (1, n)))


ATT_SCALE2 = ATT_SCALE * float(np.log2(np.e))
MASKED = -1e30
ATT_STRIP = 64


def _att_mask(i, j, blk):
    rowid = i * blk + lax.broadcasted_iota(jnp.int32, (blk, 1), 0)
    colid = j * blk + lax.broadcasted_iota(jnp.int32, (1, blk), 1)
    return (colid <= rowid) & (colid >= PAD)


def _att_bias(blk):
    r = jnp.arange(blk)[:, None]
    c = jnp.arange(blk)[None, :]
    zero = jnp.zeros((blk, blk), F32)
    first = jnp.where(c >= PAD, 0.0, MASKED) + zero
    diag = jnp.where(c <= r, 0.0, MASKED).astype(F32)
    return jnp.stack([zero, first, diag, jnp.minimum(first, diag), zero + MASKED])


def _att_bias_index(j, i):
    return jnp.where(j > i, 4, jnp.where(j == 0, 1, 0) + jnp.where(j == i, 2, 0))


def _key_slots(row0, kv, kr):
    width = kv.shape[1]
    return jnp.where(_slot_lane(width) < MLA_NOPE, kv, jnp.tile(kr, (1, width // LANES))), kv


def _attn_fwd(qr, km, vb, name, carried=None):
    t = qr.shape[0]
    blk = _tile(t, 384, LANES)
    nq = t // blk

    bias = _att_bias(blk)

    def body(q_ref, k_ref, v_ref, b_ref, o_ref, s0, s1, p0, p1):
        i = pl.program_id(1)
        lane = lax.broadcasted_iota(jnp.int32, (1, LANES), 1)
        qb = q_ref[...]

        def rows(j):
            return pl.ds(pl.multiple_of(jnp.clip(j, 0, i) * blk, blk), blk)

        def scores(j):
            return lax.dot_general(qb, k_ref[rows(j), :], _DIMS["nt"], preferred_element_type=F32) + b_ref[_att_bias_index(j, i)]

        def half(j, car, s_cur, s_nxt, p_cur, p_prv):
            m, l, acc, al_prev = car
            s_nxt[...] = scores(j + 1)
            acc2 = al_prev * acc + lax.dot_general(p_prv[...], v_ref[rows(j - 1), :], _DIMS["nn"], preferred_element_type=F32)
            m2 = jnp.maximum(m, jnp.max(s_cur[...], axis=1, keepdims=True))
            al = jnp.exp2((m - m2) * ATT_SCALE2)
            pm = jnp.exp2(s_cur[...] * ATT_SCALE2 - m2 * ATT_SCALE2)
            p_cur[...] = pm.astype(BF16)
            return m2, al * l + jnp.sum(pm, axis=1, keepdims=True), acc2, al

        def step(jj, car):
            car = half(2 * jj, car, s0, s1, p0, p1)
            return half(2 * jj + 1, car, s1, s0, p1, p0)

        s0[...] = scores(0)
        p1[...] = jnp.zeros((blk, blk), BF16)
        car = (jnp.full((blk, 1), MASKED, F32), jnp.zeros((blk, 1), F32), jnp.zeros((blk, LANES), F32), jnp.ones((blk, 1), F32))
        steps = i // 2 + 1
        m, l, acc, al_last = lax.fori_loop(0, steps, step, car)
        acc = al_last * acc + lax.dot_general(p1[...], v_ref[rows(2 * steps - 1), :], _DIMS["nn"], preferred_element_type=F32)
        out = jnp.where(lane >= MLA_NOPE, acc / l, m * ATT_SCALE + jnp.log(l))
        o_ref[...] = jnp.where(_valid(i * blk, blk), out, 0.0)

    seq_h = pl.BlockSpec((t, LANES), lambda h, i: (0, h))
    (o,), carried_out = _carry_call(
        body, name, (MLA_HEADS, nq),
        [pl.BlockSpec((blk, LANES), lambda h, i: (i, h)), seq_h, seq_h, _full_spec(bias.shape, 2)],
        [pl.BlockSpec((blk, LANES), lambda h, i: (i, h))], [jax.ShapeDtypeStruct((t, MLA_HEADS * LANES), F32)],
        [pltpu.VMEM((blk, blk), F32)] * 2 + [pltpu.VMEM((blk, blk), BF16)] * 2, (qr, km, vb, bias), carried)
    return o, carried_out


def _attn_bwd(qr, km, vb, o, do, name, carried=None):
    t = qr.shape[0]
    blk = _tile(t, 384, LANES)
    nq = t // blk

    bias = _att_bias(blk)
    log2e = float(np.log2(np.e))

    def body(q_ref, o_ref, do_ref, k_ref, v_ref, b_ref, dq_ref, dkv_ref, dkr_ref, s0, s1, dp0, dp1, p0, p1, ds0, ds1):
        h, j = pl.program_id(0), pl.program_id(1)
        lane = lax.broadcasted_iota(jnp.int32, (1, LANES), 1)

        @pl.when(j == 0)
        def _():
            dq_ref[...] = jnp.zeros_like(dq_ref)

        @pl.when((h == 0) & (j == 0))
        def _():
            dkr_ref[...] = jnp.zeros_like(dkr_ref)

        kmat, vmat = k_ref[...], v_ref[...]

        def rows(i):
            return pl.ds(pl.multiple_of(jnp.clip(i, j, nq - 1) * blk, blk), blk)

        def first_stage(i, s_buf, dp_buf):
            bidx = jnp.where(i >= nq, 4, _att_bias_index(j, i))
            s_buf[...] = lax.dot_general(q_ref[rows(i), :], kmat, _DIMS["nt"], preferred_element_type=F32) + b_ref[bidx]
            dp_buf[...] = lax.dot_general(do_ref[rows(i), :].astype(BF16), vmat, _DIMS["nt"], preferred_element_type=F32)

        def last_stage(i, car, p_buf, ds_buf):
            dk, dv = car
            r = rows(i)
            dv = dv + lax.dot_general(p_buf[...], do_ref[r, :].astype(BF16), _DIMS["tn"], preferred_element_type=F32)
            dk = dk + lax.dot_general(ds_buf[...], q_ref[r, :], _DIMS["tn"], preferred_element_type=F32)
            dq_ref[r, :] += lax.dot_general(ds_buf[...], kmat, _DIMS["nn"], preferred_element_type=F32)
            return dk, dv

        def half(i, car, s_cur, dp_cur, p_cur, ds_cur, s_nxt, dp_nxt, p_prv, ds_prv):
            first_stage(i + 1, s_nxt, dp_nxt)
            car = last_stage(i - 1, car, p_prv, ds_prv)
            r = rows(i)
            ob, dob = o_ref[r, :], do_ref[r, :]
            delta = jnp.sum(dob * ob, axis=1, keepdims=True)
            pm = jnp.exp2(s_cur[...] * ATT_SCALE2 - ob[:, 0:1] * log2e)
            p_cur[...] = pm.astype(BF16)
            ds_cur[...] = (pm * (dp_cur[...] - delta) * ATT_SCALE).astype(BF16)
            return car

        def step(tt, car):
            i = j + 2 * tt
            car = half(i, car, s0, dp0, p0, ds0, s1, dp1, p1, ds1)
            return half(i + 1, car, s1, dp1, p1, ds1, s0, dp0, p0, ds0)

        first_stage(j, s0, dp0)
        p1[...] = jnp.zeros((blk, blk), BF16)
        ds1[...] = jnp.zeros((blk, blk), BF16)
        zero = jnp.zeros((blk, LANES), F32)
        steps = (nq - j + 1) // 2
        car = lax.fori_loop(0, steps, step, (zero, zero))
        dk, dv = last_stage(j + 2 * steps - 1, car, p1, ds1)
        dkv_ref[...] = jnp.where(lane < MLA_NOPE, dk, dv)
        dkr_ref[rows(j), :] += jnp.where(lane >= MLA_NOPE, dk, 0.0)

    seq_h = pl.BlockSpec((t, LANES), lambda h, j: (0, h))
    blk_h = pl.BlockSpec((blk, LANES), lambda h, j: (j, h))
    return _carry_call(
        body, name, (MLA_HEADS, nq), [seq_h, seq_h, seq_h, blk_h, blk_h, _full_spec(bias.shape, 2)],
        [seq_h, blk_h, pl.BlockSpec((t, LANES), lambda h, j: (0, 0))],
        [jax.ShapeDtypeStruct((t, MLA_HEADS * LANES), F32), jax.ShapeDtypeStruct((t, MLA_HEADS * LANES), F32),
         jax.ShapeDtypeStruct((t, LANES), F32)],
        [pltpu.VMEM((blk, blk), F32)] * 4 + [pltpu.VMEM((blk, blk), BF16)] * 4, (qr, o, do, km, vb, bias), carried)


def _rms_rows(row0, x, g):
    return (_rms(x, g),)


def _ssdmla_fwd(h, p, l, e, cos, sin, carried=None):
    hn = _prenorm(h, p["mix_pre_g"][l], "sm_prenorm")
    proj = _mm(hn, p["w_in"][e], "nn", "sm_in")
    xc = _conv_fwd(proj, p["ssd_conv_w"][e], p["ssd_conv_b"][e], "ssd_conv", cw=SSD_GW, c0=PROJ_XBC // SSD_GW)
    xact = _rowwise("ssd_act", _ssd_act, [xc], [], [(SSD_CONV_CH, F32)])[0]
    dt, da = _rowwise("ssd_dt", _ssd_dt, [(proj, LANES, PROJ_DT // LANES)], [p["ssd_dt_bias"][e], p["ssd_a_log"][e]],
                      [(LANES, F32)] * 2)
    y, hsave = _ssd_scan(xact, dt, da, "ssd_scan")
    y_ssd = _rowwise("ssd_post", _ssd_post, [y, (xact, SSD_D_INNER, 0), (proj, SSD_D_INNER, 0)],
                     [p["ssd_d"][e], p["ssd_norm_g"][e]], [(SSD_D_INNER, BF16)])[0]
    cqn = _prenorm((proj, MLA_Q_RANK, PROJ_CQ // MLA_Q_RANK), p["mla_q_norm_g"][e], "mla_qnorm")
    ckvn = _prenorm((proj, MLA_KV_RANK, PROJ_CKV // MLA_KV_RANK), p["mla_kv_norm_g"][e], "mla_kvnorm")
    q = _mm(cqn, p["mla_w_q_up"][e], "nn", "mla_q_up")
    kv = _mm(ckvn, p["mla_w_kv_up"][e], "nn", "mla_kv_up")
    kr = _rowwise("mla_krope", lambda r0, x, c, s: (_rope(x, c, s),), [(proj, LANES, PROJ_KR // LANES), cos, sin], [],
                  [(LANES, F32)])[0]
    slots, tr = MLA_HEADS * LANES, _tile(h.shape[0], 264, 16)
    qr = _rowwise("mla_q_rope", lambda r0, a, c, s: (_rope(a, c, s),), [q, cos, sin], [], [(slots, BF16)], tr=tr)[0]
    km, vb = _rowwise("mla_key_slots", _key_slots, [kv, kr], [], [(slots, BF16)] * 2, tr=tr)
    o, carried_out = _attn_fwd(qr, km, vb, "mla_attn", carried)
    m1 = _mm(y_ssd, p["w_out_ssd"][e], "nn", "sm_out_ssd")
    m = _mm(o, p["w_out_att"][e], "nn", "sm_out_att", add=m1)
    h2 = _add_postnorm(h, [m], p["mix_post_g"][l], "sm_postnorm")
    return h2, (h, hn, proj, xc, xact, dt, da, y, hsave, y_ssd, cqn, ckvn, qr, km, vb, o, m), carried_out


def _ssdmla_bwd(dh, saved, p, l, e, cos, sin, grads, carried=None):
    h, hn, proj, xc, xact, dt, da, y, hsave, y_ssd, cqn, ckvn, qr, km, vb, o, m = saved
    dm, grads["mix_post_g"][l] = _postnorm_bwd(m, p["mix_post_g"][l], dh, "sm_postnorm_bwd")
    grads["w_out_ssd"][e] = _mm(y_ssd, dm, "tn", "sm_out_ssd_dw")
    grads["w_out_att"][e] = _mm(o, dm, "tn", "sm_out_att_dw")
    dy_ssd = _mm(dm, p["w_out_ssd"][e], "nt", "sm_out_ssd_dx")
    do = _mm(dm, p["w_out_att"][e], "nt", "sm_out_att_dx")
    (dqr, dkv, dkr), carried_out = _attn_bwd(qr, km, vb, o, do, "mla_attn_bwd", carried)
    dq = _rowwise("mla_q_rope_bwd", lambda r0, a, c, s: (_rope_t(a, c, s),), [dqr, cos, sin], [], [(MLA_HEADS * LANES, F32)],
                  tr=_tile(h.shape[0], 264, 16))[0]
    dkr_raw = _rowwise("mla_krope_bwd", lambda r0, d, c, s: (_rope_t(d, c, s),), [dkr, cos, sin], [], [(LANES, F32)])[0]
    grads["mla_w_q_up"][e] = _mm(cqn, dq, "tn", "mla_q_up_dw")
    dcqn = _mm(dq, p["mla_w_q_up"][e], "nt", "mla_q_up_dx")
    (dcq,), (grads["mla_q_norm_g"][e],) = _rowwise_vjp(
        "mla_qnorm_bwd", _rms_rows, [(proj, MLA_Q_RANK, PROJ_CQ // MLA_Q_RANK)], [p["mla_q_norm_g"][e]], [dcqn])
    grads["mla_w_kv_up"][e] = _mm(ckvn, dkv, "tn", "mla_kv_up_dw")
    dckvn = _mm(dkv, p["mla_w_kv_up"][e], "nt", "mla_kv_up_dx")
    (dckv,), (grads["mla_kv_norm_g"][e],) = _rowwise_vjp(
        "mla_kvnorm_bwd", _rms_rows, [(proj, MLA_KV_RANK, PROJ_CKV // MLA_KV_RANK)], [p["mla_kv_norm_g"][e]], [dckvn])
    (dy, dxskip, dz), (grads["ssd_d"][e], grads["ssd_norm_g"][e]) = _rowwise_vjp(
        "ssd_post_bwd", _ssd_post, [y, (xact, SSD_D_INNER, 0), (proj, SSD_D_INNER, 0)], [p["ssd_d"][e], p["ssd_norm_g"][e]], [dy_ssd])
    dxs, db, dc, ddt, dda = _ssd_scan_bwd(xact, dt, da, hsave, dy, dxskip, "ssd_scan_bwd")
    dxact = jnp.concatenate([dxs, db, dc], axis=1)
    (dxc,), _ = _rowwise_vjp("ssd_act_bwd", _ssd_act, [xc], [], [dxact])
    dxbc, grads["ssd_conv_w"][e], grads["ssd_conv_b"][e] = _conv_bwd(
        proj, p["ssd_conv_w"][e], dxc, "ssd_conv_bwd", cw=SSD_GW, c0=PROJ_XBC // SSD_GW)
    (ddtraw,), (grads["ssd_dt_bias"][e], grads["ssd_a_log"][e]) = _rowwise_vjp(
        "ssd_dt_bwd", _ssd_dt, [(proj, LANES, PROJ_DT // LANES)], [p["ssd_dt_bias"][e], p["ssd_a_log"][e]], [ddt, dda])
    dproj = jnp.concatenate([dz, dxbc, ddtraw, dcq, dckv, dkr_raw], axis=1)
    grads["w_in"][e] = _mm(hn, dproj, "tn", "sm_in_dw")
    dhn = _mm(dproj, p["w_in"][e], "nt", "sm_in_dx")
    dh, grads["mix_pre_g"][l] = _prenorm_bwd_add(h, p["mix_pre_g"][l], [dhn], dh, "sm_prenorm_bwd")
    return dh, carried_out


GAINS = ("mix_pre_g", "mix_post_g", "mlp_pre_g", "mlp_post_g", "ssd_norm_g", "mla_q_norm_g", "mla_kv_norm_g", "ssd_conv_b", "rg_conv_b")
HEAD_VECS = ("ssd_dt_bias", "ssd_a_log", "ssd_d")
LRU_VECS = ("rg_b_a", "rg_b_i", "rg_lambda")
IN_DT_END = SSD_D_INNER + SSD_CONV_CH + SSD_HEADS
IN_KR = IN_DT_END + MLA_Q_RANK + MLA_KV_RANK


def _each(a, f):
    layers = a if isinstance(a, list) else [a[i] for i in range(a.shape[0])]
    return [None if x is None else f(x) for x in layers]


def _layout_params(w):
    p = {k: _each(w[k], lambda a: a[None, :]) for k in GAINS}
    for k in HEAD_VECS:
        p[k] = _each(w[k], lambda a: jnp.pad(a, (0, LANES - SSD_HEADS))[None, :])
    for k in LRU_VECS:
        p[k] = _each(w[k], lambda a: a.reshape(LRU_BLOCKS, 1, LRU_BLOCK))
    for k in ("w_up", "w_down", "mla_w_kv_up", "rg_w_x", "rg_w_y", "rg_w_out"):
        p[k] = _each(w[k], lambda a: a if isinstance(a, Gathered) else a.astype(BF16))
    for k in ("ssd_conv_w", "rg_conv_w", "rg_w_a", "rg_w_i"):
        p[k] = _each(w[k], lambda a: a)

    def w_in(a):
        def zcols(n):
            return jnp.zeros((a.shape[0], n), a.dtype)

        return jnp.concatenate([a[:, :IN_DT_END], zcols(PROJ_CQ - IN_DT_END), a[:, IN_DT_END:IN_KR], zcols(ROPE_LO),
                                a[:, IN_KR:], zcols(LANES - ROPE_HI)], axis=1).astype(BF16)

    def q_up(a):
        a = a.reshape(MLA_Q_RANK, MLA_HEADS, MLA_NOPE + MLA_ROPE)
        return jnp.pad(a, ((0, 0), (0, 0), (0, LANES - MLA_NOPE - MLA_ROPE))).reshape(MLA_Q_RANK, MLA_HEADS * LANES).astype(BF16)

    def out_att(a):
        a = a[SSD_D_INNER:].reshape(MLA_HEADS, MLA_V, D_MODEL)
        return jnp.pad(a, ((0, 0), (LANES - MLA_V, 0), (0, 0))).reshape(MLA_HEADS * LANES, D_MODEL).astype(BF16)

    p["w_in"] = _each(w["w_in"], w_in)
    p["mla_w_q_up"] = _each(w["mla_w_q_up"], q_up)
    p["w_out_ssd"] = _each(w["w_out_ab"], lambda a: a[:SSD_D_INNER].astype(BF16))
    p["w_out_att"] = _each(w["w_out_ab"], out_att)
    return p


def _rope_tables(t):
    pos = (jnp.arange(t) - PAD).astype(F32)
    inv = ROPE_BASE ** (-jnp.arange(0, MLA_ROPE, 2, dtype=F32) / MLA_ROPE)
    ang = pos[:, None] * inv[None, :]
    c, s = jnp.cos(ang), jnp.sin(ang)
    one, zero = jnp.ones((t, MLA_NOPE), F32), jnp.zeros((t, MLA_NOPE), F32)
    tail = LANES - ROPE_HI
    return (jnp.concatenate([one, c, c, one[:, :tail]], axis=1), jnp.concatenate([zero, -s, s, zero[:, :tail]], axis=1))


GRAD_KEYS = GAINS + HEAD_VECS + LRU_VECS + ("w_up", "w_down", "mla_w_kv_up", "rg_w_x", "rg_w_y", "rg_w_out", "ssd_conv_w",
                                            "rg_conv_w", "rg_w_a", "rg_w_i", "w_in", "mla_w_q_up", "w_out_ssd", "w_out_att")


def _device_step(x, meta, target, p, hooks=None):
    t = PAD + N_META + x.shape[0]
    cos, sin = _rope_tables(t)
    h = jnp.concatenate([jnp.zeros((PAD, D_MODEL), F32), meta, x], axis=0)
    n_even, n_odd = (DEPTH + 1) // 2, DEPTH // 2
    saved = []
    for l in range(DEPTH):
        if l % 2 == 0:
            carried = hooks.forward_exchange() if hooks and l == 0 else None
            h, sm, arrived = _ssdmla_fwd(h, p, l, l // 2, cos, sin, carried)
            if carried is not None:
                p = hooks.after_forward_exchange(arrived)
        else:
            h, sm = _rglru_fwd(h, p, l, l // 2)
        h, sp = _mlp_fwd(h, p, l)
        saved.append((sm, sp))
    sq, dh = _loss_and_grad(h, target, "loss")
    per_layer = {"mix_pre_g": DEPTH, "mix_post_g": DEPTH, "mlp_pre_g": DEPTH, "mlp_post_g": DEPTH, "w_up": DEPTH, "w_down": DEPTH}
    grads = {k: [None] * per_layer.get(k, n_odd if k.startswith("rg_") else n_even) for k in GRAD_KEYS}
    for l in reversed(range(DEPTH)):
        sm, sp = saved[l]
        dh = _mlp_bwd(dh, sp, p, l, grads)
        if l % 2 == 0:
            carried = hooks.backward_exchange(grads) if hooks and l == 0 else None
            dh, arrived = _ssdmla_bwd(dh, sm, p, l, l // 2, cos, sin, grads, carried)
            if carried is not None:
                hooks.after_backward_exchange(arrived)
        else:
            dh = _rglru_bwd(dh, sm, p, l, l // 2, grads)
    return sq, dh, grads


MESH = pl.DeviceIdType.MESH
ANY = pl.BlockSpec(memory_space=pl.ANY)


def _mesh_pos():
    return lax.axis_index("x"), lax.axis_index("y"), lax.axis_index("c")


def _other_chips(x, y):
    return [(1 - x, y), (x, 1 - y), (1 - x, 1 - y)]


def _remote(src, dst, send_sems, recv_sems, k, to):
    return pltpu.make_async_remote_copy(src_ref=src, dst_ref=dst, send_sem=send_sems.at[k], recv_sem=recv_sems.at[k],
                                        device_id=to, device_id_type=MESH)


class Exchange:
    def __init__(self, ins, outs, aliases, n_sems, plan):
        self.ins, self.outs, self.aliases, self.n_sems, self.plan = list(ins), list(outs), dict(aliases), n_sems, plan


def _sems(n):
    return [pltpu.SemaphoreType.DMA((n,)), pltpu.SemaphoreType.DMA((n,))]


def _run_exchange(name, ex):
    ni, no = len(ex.ins), len(ex.outs)

    def body(*refs):
        sends = ex.plan(refs[:ni], refs[ni:ni + no], refs[-2], refs[-1], False)
        for cp in sends:
            cp.start()
        for cp in ex.plan(refs[:ni], refs[ni:ni + no], refs[-2], refs[-1], True):
            cp.wait_recv()
        for cp in sends:
            cp.wait_send()

    return pl.pallas_call(body, name=name, in_specs=[ANY] * ni, out_specs=[ANY] * no, out_shape=ex.outs,
                          input_output_aliases=ex.aliases, scratch_shapes=_sems(ex.n_sems))(*ex.ins)


def _carry_call(body, name, grid, in_specs, out_specs, out_shape, scratch_shapes, args, ex):
    if ex is None:
        res = pl.pallas_call(body, name=name, grid=grid, in_specs=in_specs, out_specs=out_specs, out_shape=out_shape,
                             scratch_shapes=scratch_shapes, compiler_params=_params(("arbitrary",) * len(grid)))(*args)
        return res, None
    ni, no, ns, xi, xo = len(in_specs), len(out_specs), len(scratch_shapes), len(ex.ins), len(ex.outs)

    def wrapped(*refs):
        ins, xin = refs[:ni], refs[ni:ni + xi]
        outs, xout = refs[ni + xi:ni + xi + no], refs[ni + xi + no:ni + xi + no + xo]
        scr, send_sems, recv_sems = refs[ni + xi + no + xo:-2], refs[-2], refs[-1]
        pid = [pl.program_id(d) for d in range(len(grid))]
        first = functools.reduce(jnp.logical_and, [p == 0 for p in pid])
        last = functools.reduce(jnp.logical_and, [p == g - 1 for p, g in zip(pid, grid)])

        @pl.when(first)
        def _():
            for cp in ex.plan(xin, xout, send_sems, recv_sems, False):
                cp.start()

        body(*ins, *outs, *scr)

        @pl.when(last)
        def _():
            for cp in ex.plan(xin, xout, send_sems, recv_sems, True):
                cp.wait_recv()
            for cp in ex.plan(xin, xout, send_sems, recv_sems, False):
                cp.wait_send()

    res = pl.pallas_call(
        wrapped, name=name, grid=grid, in_specs=list(in_specs) + [ANY] * xi, out_specs=list(out_specs) + [ANY] * xo,
        out_shape=list(out_shape) + ex.outs, scratch_shapes=list(scratch_shapes) + _sems(ex.n_sems),
        input_output_aliases={ni + i: no + o for i, o in ex.aliases.items()},
        compiler_params=_params(("arbitrary",) * len(grid)))(*args, *ex.ins)
    return res[:no], res[no:]


def _gather_ici(srcs, bufs, ranges):
    n = len(srcs)

    def plan(in_refs, out_refs, ss, rs, arrivals):
        x, y, c = _mesh_pos()
        cps = []
        for t, (l0, nl) in enumerate(ranges):
            if nl:
                s, o, lr = in_refs[t], out_refs[t], pl.ds(l0, nl)
                for j, (cx, cy) in enumerate(_other_chips(x, y)):
                    chip = 2 * cx + cy if arrivals else 2 * x + y
                    cps.append(_remote(s.at[lr, c], o.at[chip, lr, c], ss, rs, (N_CHIPS - 1) * t + j, (cx, cy, c)))
        return cps

    outs = [jax.ShapeDtypeStruct((N_CHIPS,) + s.shape, s.dtype) for s in srcs]
    if bufs is None:
        return Exchange(srcs, outs, {}, (N_CHIPS - 1) * n, plan)
    return Exchange(list(srcs) + list(bufs), outs, {n + t: t for t in range(n)}, (N_CHIPS - 1) * n, plan)


def _gather_d2d(srcs, bufs, ranges):
    n = len(srcs)

    def plan(in_refs, out_refs, ss, rs, arrivals):
        x, y, c = _mesh_pos()
        sib, me = (x, y, 1 - c), 2 * x + y
        cps = []
        for t, (l0, nl) in enumerate(ranges):
            if nl:
                s, o, lr = in_refs[t], out_refs[t], pl.ds(l0, nl)
                for j, (cx, cy) in enumerate(_other_chips(x, y)):
                    slot = o.at[2 * cx + cy, lr, c]
                    cps.append(_remote(slot, o.at[2 * cx + cy, lr, 1 - c] if arrivals else slot, ss, rs, N_CHIPS * t + j, sib))
                cps.append(_remote(s.at[lr], o.at[me, lr], ss, rs, N_CHIPS * t + N_CHIPS - 1, sib))
        return cps

    outs = [jax.ShapeDtypeStruct(b.shape, b.dtype) for b in bufs]
    return Exchange(list(srcs) + list(bufs), outs, {n + t: t for t in range(n)}, N_CHIPS * n, plan)


def _gather_chips(srcs, name):
    ranges = [(0, s.shape[0]) for s in srcs]
    bufs = _run_exchange(name + "_ici", _gather_ici(srcs, None, ranges))
    return _run_exchange(name + "_d2d", _gather_d2d(srcs, bufs, ranges))


def _pair_exchange(gs):
    def plan(in_refs, out_refs, ss, rs, arrivals):
        x, y, c = _mesh_pos()
        return [_remote(g.at[pl.ds(0, N_CHIPS), 1 - c], o, ss, rs, t, (x, y, 1 - c)) for t, (g, o) in enumerate(zip(in_refs, out_refs))]

    return Exchange(gs, [jax.ShapeDtypeStruct((g.shape[0],) + g.shape[2:], g.dtype) for g in gs], {}, len(gs), plan)


def _chip_exchange(ps, slots, qs, q_shapes):
    n = len(ps)
    kept = [g for g, q in enumerate(qs) if q is not None]

    def plan(in_refs, out_refs, ss, rs, arrivals):
        x, y, c = _mesh_pos()
        return [_remote(in_refs[t].at[2 * cx + cy], out_refs[g].at[j, li], ss, rs, (N_CHIPS - 1) * t + j, (cx, cy, c))
                for t, (g, li) in enumerate(slots) for j, (cx, cy) in enumerate(_other_chips(x, y))]

    return Exchange(list(ps) + [qs[g] for g in kept], q_shapes, {n + i: g for i, g in enumerate(kept)}, (N_CHIPS - 1) * n, plan)


def _pair_share(fs):
    def plan(in_refs, out_refs, ss, rs, arrivals):
        x, y, c = _mesh_pos()
        return [_remote(o.at[pl.ds(0, o.shape[0]), c], o.at[pl.ds(0, o.shape[0]), 1 - c if arrivals else c], ss, rs, t, (x, y, 1 - c))
                for t, o in enumerate(out_refs)]

    return Exchange(fs, [jax.ShapeDtypeStruct(f.shape, f.dtype) for f in fs], {t: t for t in range(len(fs))}, len(fs), plan)


SUM_BLOCK = 512 * 1024


def _sum_pair(g, ra, c, name):
    n, _, h, w = g.shape
    tr = _tile(h, max(16, SUM_BLOCK // w), 16)

    def body(c_ref, g_ref, r_ref, o_ref):
        o_ref[...] = (g_ref[0] + r_ref[...]).astype(o_ref.dtype)

    return pl.pallas_call(
        body, name=name,
        grid_spec=pltpu.PrefetchScalarGridSpec(
            num_scalar_prefetch=1, grid=(n, h // tr),
            in_specs=[pl.BlockSpec((1, 1, tr, w), lambda s, i, cr: (s, cr[0], i, 0)), pl.BlockSpec((1, tr, w), lambda s, i, cr: (s, i, 0))],
            out_specs=pl.BlockSpec((1, tr, w), lambda s, i, cr: (s, i, 0))),
        out_shape=jax.ShapeDtypeStruct((n, h, w), BF16),
        compiler_params=_params(("parallel", "parallel")),
    )(c.reshape(1).astype(jnp.int32), g, ra)


def _sum_chips(ps, q, pos, name):
    nc, nl, h, w = q.shape
    tr = _tile(h, max(16, SUM_BLOCK // (w * nl)), 16)

    def body(x_ref, y_ref, c_ref, *refs):
        q_ref, o_ref = refs[nl], refs[nl + 1]
        for l in range(nl):
            acc = refs[l][0].astype(F32)
            for j in range(nc):
                acc = acc + q_ref[j, l].astype(F32)
            o_ref[l] = acc

    return pl.pallas_call(
        body, name=name,
        grid_spec=pltpu.PrefetchScalarGridSpec(
            num_scalar_prefetch=3, grid=(h // tr,),
            in_specs=[pl.BlockSpec((1, tr, w), lambda i, x, y, c: (2 * x[0] + y[0], i, 0))] * nl
            + [pl.BlockSpec((nc, nl, tr, w), lambda i, x, y, c: (0, 0, i, 0))],
            out_specs=pl.BlockSpec((nl, None, tr, w), lambda i, x, y, c: (0, c[0], i, 0))),
        out_shape=jax.ShapeDtypeStruct((nl, 2, h, w), F32),
        compiler_params=_params(("parallel",)),
    )(*pos, *ps, q)


def _adamw(g, w, m, v, name):
    def f(r0, gg, ww, mm, vv):
        m2 = ADAM_B1 * mm + (1.0 - ADAM_B1) * gg
        v2 = ADAM_B2 * vv + (1.0 - ADAM_B2) * jnp.square(gg)
        m_hat = m2 / (1.0 - ADAM_B1 ** ADAM_STEP)
        v_hat = v2 / (1.0 - ADAM_B2 ** ADAM_STEP)
        return gg, -ADAM_LR * (m_hat / (jnp.sqrt(v_hat) + ADAM_EPS) + ADAM_WD * ww), m2, v2

    return _rowwise(name, f, [g, w, m, v], [], [(g.shape[1], F32)] * 4, tr=_tile(g.shape[0], 512))


WEIGHTS = (
    ("meta_tokens", (N_META, D_MODEL), 1), ("mix_pre_g", (DEPTH, D_MODEL), None), ("mix_post_g", (DEPTH, D_MODEL), None),
    ("mlp_pre_g", (DEPTH, D_MODEL), None), ("mlp_post_g", (DEPTH, D_MODEL), None), ("w_up", (DEPTH, D_MODEL, D_FF), 2),
    ("w_down", (DEPTH, D_FF, D_MODEL), 1), ("w_in", (2, D_MODEL, 3248), 2), ("ssd_conv_w", (2, CONV_K, SSD_CONV_CH), 2),
    ("ssd_conv_b", (2, SSD_CONV_CH), None), ("ssd_dt_bias", (2, SSD_HEADS), None), ("ssd_a_log", (2, SSD_HEADS), None),
    ("ssd_d", (2, SSD_HEADS), None), ("ssd_norm_g", (2, SSD_D_INNER), None), ("mla_q_norm_g", (2, MLA_Q_RANK), None),
    ("mla_w_q_up", (2, MLA_Q_RANK, MLA_HEADS * (MLA_NOPE + MLA_ROPE)), 2), ("mla_kv_norm_g", (2, MLA_KV_RANK), None),
    ("mla_w_kv_up", (2, MLA_KV_RANK, MLA_HEADS * (MLA_NOPE + MLA_V)), 2), ("w_out_ab", (2, SSD_D_INNER + MLA_HEADS * MLA_V, D_MODEL), 1),
    ("rg_w_x", (2, D_MODEL, LRU_WIDTH), 2), ("rg_w_y", (2, D_MODEL, LRU_WIDTH), 2), ("rg_conv_w", (2, CONV_K, LRU_WIDTH), 2),
    ("rg_conv_b", (2, LRU_WIDTH), 1), ("rg_w_a", (2, LRU_BLOCKS, LRU_BLOCK, LRU_BLOCK), None), ("rg_b_a", (2, LRU_WIDTH), 1),
    ("rg_w_i", (2, LRU_BLOCKS, LRU_BLOCK, LRU_BLOCK), None), ("rg_b_i", (2, LRU_WIDTH), 1), ("rg_lambda", (2, LRU_WIDTH), 1),
    ("rg_w_out", (2, LRU_WIDTH, D_MODEL), 1),
)
BIG = {"w_up": "col", "w_down": "row", "w_in": "col", "mla_w_q_up": "col", "mla_w_kv_up": "col", "w_out_ab": "row",
       "rg_w_x": "col", "rg_w_y": "col", "rg_w_out": "row"}
DIRECT = ("w_up", "w_down")
FLAT_QUANTUM = 2 * 16 * LANES
TABLE = {name: (shape, d) for name, shape, d in WEIGHTS}
SMALL_SHARDED = tuple(name for name, _, d in WEIGHTS if d is not None and name not in BIG)
REPLICATED = tuple(name for name, _, d in WEIGHTS if d is None)


def _chips_to_full(a, kind):
    if kind == "col":
        return jnp.moveaxis(a, 0, 2).reshape(a.shape[1], a.shape[2], -1)
    return jnp.moveaxis(a, 0, 1).reshape(a.shape[1], -1, a.shape[3])


def _full_to_chips(g, kind):
    if kind == "col":
        return jnp.moveaxis(g.reshape(g.shape[0], N_CHIPS, -1), 1, 0)
    return g.reshape(N_CHIPS, -1, g.shape[1])


def _chips_to_full_1(pc, kind):
    return jnp.moveaxis(pc, 0, 1).reshape(pc.shape[1], -1) if kind == "col" else pc.reshape(-1, pc.shape[2])


def _shard_shape(shape, d):
    return shape[:d] + (shape[d] // N_CHIPS,) + shape[d + 1:]


def _shard_major(full, d):
    s = full.shape
    return jnp.moveaxis(full.reshape(s[:d] + (N_CHIPS, s[d] // N_CHIPS) + s[d + 1:]), d, 0).reshape(N_CHIPS, -1)


def _from_shard_major(a, shape, d):
    ss = _shard_shape(shape, d)
    return jnp.moveaxis(a.reshape((N_CHIPS,) + ss), 0, d).reshape(shape)


def _pad_cols(a, quantum):
    n = a.shape[-1]
    return jnp.pad(a, [(0, 0)] * (a.ndim - 1) + [(0, -n % quantum)])


def _big_pieces(g):
    def w_in(a):
        return jnp.concatenate([a[:, :IN_DT_END], a[:, PROJ_CQ:PROJ_KR], a[:, PROJ_KR + ROPE_LO:PROJ_KR + ROPE_HI]], axis=1)

    def q_up(a):
        return a.reshape(MLA_Q_RANK, MLA_HEADS, LANES)[:, :, :MLA_NOPE + MLA_ROPE].reshape(MLA_Q_RANK, -1)

    def out_ab(sa):
        s, a = sa
        return jnp.concatenate([s, a.reshape(MLA_HEADS, LANES, D_MODEL)[:, LANES - MLA_V:, :].reshape(-1, D_MODEL)], axis=0)

    ident = lambda a: a
    full = {"w_down": _each(g["w_down"], ident), "w_in": _each(g["w_in"], w_in), "mla_w_q_up": _each(g["mla_w_q_up"], q_up),
            "mla_w_kv_up": _each(g["mla_w_kv_up"], ident),
            "w_out_ab": _each([None if s is None or a is None else (s, a) for s, a in zip(g["w_out_ssd"], g["w_out_att"])], out_ab),
            "rg_w_x": _each(g["rg_w_x"], ident), "rg_w_y": _each(g["rg_w_y"], ident), "rg_w_out": _each(g["rg_w_out"], ident)}
    return {name: (list(g[name]) if name == "w_up" else _each(full[name], lambda a, k=BIG[name]: _full_to_chips(a, k))) for name in BIG}


def _small_grads(g, dh):
    out = {k: jnp.stack(g[k])[:, 0, :] for k in GAINS}
    for k in HEAD_VECS:
        out[k] = jnp.stack(g[k])[:, 0, :SSD_HEADS]
    for k in LRU_VECS:
        out[k] = jnp.stack(g[k]).reshape(-1, LRU_WIDTH)
    for k in ("ssd_conv_w", "rg_conv_w", "rg_w_a", "rg_w_i"):
        out[k] = jnp.stack(g[k])
    out["meta_tokens"] = dh[PAD:PAD + N_META]
    return out


def _natural_grads(g, dh):
    out = _small_grads(g, dh)
    for name, pcs in _big_pieces(g).items():
        out[name] = jnp.stack([_chips_to_full_1(pc, BIG[name]) for pc in pcs])
    return out


class StepExchanges:
    def __init__(self, w):
        self.w = w
        self.c = lax.axis_index("c")
        small = _pad_cols(jnp.concatenate([w[n].reshape(-1) for n in SMALL_SHARDED]), FLAT_QUANTUM).reshape(1, 2, -1, LANES)
        self.srcs = [self._halves(w[n].astype(BF16)) for n in BIG] + [small]
        first = {n: (0, 0 if n.startswith("rg_") else 1) for n in BIG}
        self.first = [first[n] for n in BIG] + [(0, 1)]
        self.rest = [(nl, TABLE[n][0][0] - nl) for n, (_, nl) in zip(BIG, self.first)] + [(0, 0)]
        bufs = _run_exchange("gather_first_ici", _gather_ici(self.srcs, None, self.first))
        self.bufs = _run_exchange("gather_first_d2d", _gather_d2d(self.srcs, bufs, self.first))

    @staticmethod
    def _halves(a):
        return a.reshape(a.shape[0], 2, a.shape[1] // 2, a.shape[2])

    def params(self, ranges):
        w = self.w
        full = {n: w[n] for n in REPLICATED}
        for name, buf, (l0, nl) in zip(BIG, self.bufs, ranges):
            a = buf.reshape(buf.shape[:2] + (-1, buf.shape[4]))
            have = range(l0, l0 + nl)
            if name in DIRECT:
                full[name] = [Gathered(a, BIG[name], l) if l in have else None for l in range(a.shape[1])]
            else:
                full[name] = [_chips_to_full(a[:, l:l + 1], BIG[name])[0] if l in have else None for l in range(a.shape[1])]
        got, off = self.bufs[-1].reshape(N_CHIPS, -1), 0
        for name in SMALL_SHARDED:
            shape, d = TABLE[name]
            n = int(np.prod(_shard_shape(shape, d)))
            full[name] = _from_shard_major(got[:, off:off + n], shape, d)
            off += n
        self.meta = full.pop("meta_tokens")
        return _layout_params(full)

    def forward_exchange(self):
        return _gather_ici(self.srcs, self.bufs, self.rest)

    def after_forward_exchange(self, arrived):
        self.bufs = _run_exchange("gather_rest_d2d", _gather_d2d(self.srcs, arrived, self.rest))
        return self.params([(0, TABLE[n][0][0]) for n in BIG])

    def _pair_sums(self, pieces, tag):
        keys = list(pieces)
        ras = _run_exchange("grads_pair_exchange_" + tag, _pair_exchange([pieces[k] for k in keys]))
        return {k: _sum_pair(pieces[k], ra, self.c, "grads_pair_sum") for k, ra in zip(keys, ras)}

    def _q_shapes(self):
        return [jax.ShapeDtypeStruct((N_CHIPS - 1, s.shape[0]) + s.shape[2:], BF16) for s in self.srcs[:-1]]

    def backward_exchange(self, grads):
        big = _big_pieces(grads)
        pieces = {(g, l): pc.reshape(N_CHIPS, 2, pc.shape[1] // 2, pc.shape[2])
                  for g, name in enumerate(BIG) for l, pc in enumerate(big[name]) if pc is not None}
        self.ps = self._pair_sums(pieces, "early")
        self.early = list(self.ps)
        return _chip_exchange([self.ps[k] for k in self.early], self.early, [None] * len(BIG), self._q_shapes())

    def after_backward_exchange(self, arrived):
        self.qs = list(arrived)

    def finish(self, grads, dh):
        big, small = _big_pieces(grads), _small_grads(grads, dh)
        pieces = {(g, l): pc.reshape(N_CHIPS, 2, pc.shape[1] // 2, pc.shape[2])
                  for g, name in enumerate(BIG) for l, pc in enumerate(big[name]) if (g, l) not in self.ps}
        sharded = jnp.concatenate([_shard_major(small[n], TABLE[n][1]) for n in SMALL_SHARDED], axis=1)
        rep = _pad_cols(jnp.concatenate([small[n].reshape(-1) for n in REPLICATED]), N_CHIPS * FLAT_QUANTUM)
        n_sh, n_rep = sharded.shape[1], rep.shape[0] // N_CHIPS
        flat = _pad_cols(jnp.concatenate([sharded, rep.reshape(N_CHIPS, n_rep)], axis=1), FLAT_QUANTUM)
        pieces[(len(BIG), 0)] = flat.reshape(N_CHIPS, 2, -1, LANES)
        late = self._pair_sums(pieces, "late")
        self.ps.update(late)
        keys = list(late)
        small_q = jax.ShapeDtypeStruct((N_CHIPS - 1, 1) + late[(len(BIG), 0)].shape[1:], BF16)
        qs = _run_exchange("grads_chip_exchange_late",
                           _chip_exchange([late[k] for k in keys], keys, self.qs + [None], self._q_shapes() + [small_q]))
        pos = [lax.axis_index(a).reshape(1).astype(jnp.int32) for a in ("x", "y", "c")]
        sums = [_sum_chips([self.ps[(g, l)] for l in range(q.shape[1])], q, pos, "grads_chip_sum") for g, q in enumerate(qs)]
        outs = _run_exchange("grads_pair_share", _pair_share(sums))
        out = {name: o.reshape(o.shape[0], -1, o.shape[3]) for name, o in zip(BIG, outs)}
        f = outs[-1].reshape(-1)
        rep_all = _gather_chips([f[n_sh:n_sh + n_rep].reshape(1, 2, -1, LANES)], "grads_gather_replicated")[0].reshape(-1)
        off = 0
        for name in SMALL_SHARDED:
            ss = _shard_shape(*TABLE[name])
            n = int(np.prod(ss))
            out[name] = f[off:off + n].reshape(ss)
            off += n
        off = 0
        for name in REPLICATED:
            shape = TABLE[name][0]
            n = int(np.prod(shape))
            out[name] = rep_all[off:off + n].reshape(shape)
            off += n
        return out


def kernel(x, meta_tokens, mix_pre_g, mix_post_g, mlp_pre_g, mlp_post_g, w_up, w_down, w_in, ssd_conv_w, ssd_conv_b, ssd_dt_bias, ssd_a_log, ssd_d, ssd_norm_g, mla_q_norm_g, mla_w_q_up, mla_kv_norm_g, mla_w_kv_up, w_out_ab, rg_w_x, rg_w_y, rg_conv_w, rg_conv_b, rg_w_a, rg_b_a, rg_w_i, rg_b_i, rg_lambda, rg_w_out, loss_target, m_meta_tokens, m_mix_pre_g, m_mix_post_g, m_mlp_pre_g, m_mlp_post_g, m_w_up, m_w_down, m_w_in, m_ssd_conv_w, m_ssd_conv_b, m_ssd_dt_bias, m_ssd_a_log, m_ssd_d, m_ssd_norm_g, m_mla_q_norm_g, m_mla_w_q_up, m_mla_kv_norm_g, m_mla_w_kv_up, m_w_out_ab, m_rg_w_x, m_rg_w_y, m_rg_conv_w, m_rg_conv_b, m_rg_w_a, m_rg_b_a, m_rg_w_i, m_rg_b_i, m_rg_lambda, m_rg_w_out, v_meta_tokens, v_mix_pre_g, v_mix_post_g, v_mlp_pre_g, v_mlp_post_g, v_w_up, v_w_down, v_w_in, v_ssd_conv_w, v_ssd_conv_b, v_ssd_dt_bias, v_ssd_a_log, v_ssd_d, v_ssd_norm_g, v_mla_q_norm_g, v_mla_w_q_up, v_mla_kv_norm_g, v_mla_w_kv_up, v_w_out_ab, v_rg_w_x, v_rg_w_y, v_rg_conv_w, v_rg_conv_b, v_rg_w_a, v_rg_b_a, v_rg_w_i, v_rg_b_i, v_rg_lambda, v_rg_w_out):
    names = [n for n, _, _ in WEIGHTS]
    w = dict(zip(names, (meta_tokens, mix_pre_g, mix_post_g, mlp_pre_g, mlp_post_g, w_up, w_down, w_in, ssd_conv_w, ssd_conv_b, ssd_dt_bias, ssd_a_log, ssd_d, ssd_norm_g, mla_q_norm_g, mla_w_q_up, mla_kv_norm_g, mla_w_kv_up, w_out_ab, rg_w_x, rg_w_y, rg_conv_w, rg_conv_b, rg_w_a, rg_b_a, rg_w_i, rg_b_i, rg_lambda, rg_w_out)))
    m = dict(zip(names, (m_meta_tokens, m_mix_pre_g, m_mix_post_g, m_mlp_pre_g, m_mlp_post_g, m_w_up, m_w_down, m_w_in, m_ssd_conv_w, m_ssd_conv_b, m_ssd_dt_bias, m_ssd_a_log, m_ssd_d, m_ssd_norm_g, m_mla_q_norm_g, m_mla_w_q_up, m_mla_kv_norm_g, m_mla_w_kv_up, m_w_out_ab, m_rg_w_x, m_rg_w_y, m_rg_conv_w, m_rg_conv_b, m_rg_w_a, m_rg_b_a, m_rg_w_i, m_rg_b_i, m_rg_lambda, m_rg_w_out)))
    v = dict(zip(names, (v_meta_tokens, v_mix_pre_g, v_mix_post_g, v_mlp_pre_g, v_mlp_post_g, v_w_up, v_w_down, v_w_in, v_ssd_conv_w, v_ssd_conv_b, v_ssd_dt_bias, v_ssd_a_log, v_ssd_d, v_ssd_norm_g, v_mla_q_norm_g, v_mla_w_q_up, v_mla_kv_norm_g, v_mla_w_kv_up, v_w_out_ab, v_rg_w_x, v_rg_w_y, v_rg_conv_w, v_rg_conv_b, v_rg_w_a, v_rg_b_a, v_rg_w_i, v_rg_b_i, v_rg_lambda, v_rg_w_out)))
    ex = StepExchanges(w)
    p = ex.params(ex.first)
    sq, dh, grads = _device_step(x[0], ex.meta, loss_target[0], p, hooks=ex)
    loss = lax.psum(0.5 * sq[0, 0] / D_MODEL, ("x", "y", "c"))
    g = ex.finish(grads, dh)
    grad, delta, new_m, new_v = {}, {}, {}, {}
    for name in names:
        shape = g[name].shape
        two_d = (int(np.prod(shape[:-1])), shape[-1])
        res = _adamw(g[name].reshape(two_d), w[name].reshape(two_d), m[name].reshape(two_d), v[name].reshape(two_d), "adamw")
        grad[name], delta[name], new_m[name], new_v[name] = (r.reshape(shape) for r in res)
    grad_x = dh[PAD + N_META:][None]
    return (loss, grad_x, *[grad[n] for n in names], *[delta[n] for n in names], *[new_m[n] for n in names], *[new_v[n] for n in names])
```

```python
import functools

import jax
import jax.numpy as jnp
import numpy as np
from jax import lax
from jax.experimental import pallas as pl
from jax.experimental.pallas import tpu as pltpu

F32 = jnp.float32
BF16 = jnp.bfloat16

D_MODEL = 1024
DEPTH = 4
N_META = 16
CHUNK = 128
PAD = CHUNK - N_META
EPS = 1e-6
SSD_HEADS = 16
SSD_HEAD_DIM = 64
SSD_D_INNER = SSD_HEADS * SSD_HEAD_DIM
SSD_GROUPS = 2
SSD_STATE = 128
SSD_CONV_CH = SSD_D_INNER + 2 * SSD_GROUPS * SSD_STATE
MLA_HEADS = 16
MLA_NOPE = 64
MLA_ROPE = 32
MLA_V = 64
MLA_Q_RANK = 384
MLA_KV_RANK = 256
ROPE_BASE = 10000.0
LRU_WIDTH = 1280
LRU_BLOCKS = 10
LRU_BLOCK = 128
LRU_C = 8.0
D_FF = 4 * D_MODEL
ADAM_LR, ADAM_B1, ADAM_B2, ADAM_EPS, ADAM_WD, ADAM_STEP = 0.001, 0.9, 0.999, 1e-08, 0.01, 10

LANES = 128
VMEM_LIMIT = 56 * 1024 * 1024
HEAD_SLOT = 128
PROJ_Z, PROJ_XBC, PROJ_DT, PROJ_CQ, PROJ_CKV, PROJ_KR = 0, 1024, 2560, 2688, 3072, 3328
PROJ_W = 3456


def _tile(n, cap, mult=8):
    for t in range(min(n, cap), 0, -1):
        if n % t == 0 and t % mult == 0:
            return t
    return n


def _params(sem):
    return pltpu.CompilerParams(dimension_semantics=sem, vmem_limit_bytes=VMEM_LIMIT)


def _full_spec(shape, ngrid):
    nd = len(shape)
    if ngrid == 1:
        return pl.BlockSpec(shape, lambda i: (0,) * nd)
    if ngrid == 2:
        return pl.BlockSpec(shape, lambda i, j: (0,) * nd)
    return pl.BlockSpec(shape, lambda i, j, k: (0,) * nd)


_DIMS = {"nn": (((1,), (0,)), ((), ())), "nt": (((1,), (1,)), ((), ())), "tn": (((0,), (0,)), ((), ()))}


class Gathered:
    def __init__(self, arr, kind, layer):
        self.arr, self.kind, self.layer = arr, kind, layer
        _, _, r, c = arr.shape
        self.shape = (r, N_CHIPS * c) if kind == "col" else (N_CHIPS * r, c)


N_CHIPS = 4


def _mm(a, b, mode, name, out_dtype=F32, add=None, out_chip_major=False, extra=(), post=None, out_dtypes=None):
    if mode == "nn":
        (m, kc), (_, n) = a.shape, b.shape
    elif mode == "nt":
        (m, kc), (n, _) = a.shape, b.shape
    else:
        (kc, m), (_, n) = a.shape, b.shape
    tm = _tile(m, 1024, LANES) if mode == "tn" else _tile(m, 1056, 16)
    tn = _tile(n // N_CHIPS if out_chip_major else n, 1280, LANES)
    tk = _tile(kc, 1024 if mode != "tn" else 1408, LANES)
    nk = kc // tk
    if mode == "tn":
        a_spec = pl.BlockSpec((tk, tm), lambda i, j, k: (k, i))
    else:
        a_spec = pl.BlockSpec((tm, tk), lambda i, j, k: (i, k))
    b_arr = b
    if isinstance(b, Gathered):
        b_arr, layer = b.arr, b.layer
        sr, sc = b.arr.shape[2:]
        br, bc = (tk, tn) if mode == "nn" else (tn, tk)
        assert mode in ("nn", "nt") and sr % br == 0 and sc % bc == 0

        def b_map(i, j, k):
            r, c = (k, j) if mode == "nn" else (j, k)
            if b.kind == "col":
                return ((c * bc) // sc, layer, r, ((c * bc) % sc) // bc)
            return ((r * br) // sr, layer, ((r * br) % sr) // br, c)

        b_spec = pl.BlockSpec((None, None, br, bc), b_map)
    elif mode == "nt":
        b_spec = pl.BlockSpec((tn, tk), lambda i, j, k: (j, k))
    else:
        b_spec = pl.BlockSpec((tk, tn), lambda i, j, k: (k, j))
    dims = _DIMS[mode]
    if out_chip_major:
        ns = n // N_CHIPS
        o_spec = pl.BlockSpec((None, tm, tn), lambda i, j, k: ((j * tn) // ns, i, ((j * tn) % ns) // tn))
        o_shape = jax.ShapeDtypeStruct((N_CHIPS, m, ns), out_dtype)
    else:
        o_spec = pl.BlockSpec((tm, tn), lambda i, j, k: (i, j))
        o_shape = jax.ShapeDtypeStruct((m, n), out_dtype)
    extra = list(extra) + ([add] if add is not None else [])
    if add is not None:
        post = lambda v, x: (v + x,)
    nx = len(extra)
    out_dtypes = out_dtypes or [out_dtype]
    no = len(out_dtypes)

    def body(a_ref, b_ref, *rest):
        o_refs, acc = rest[nx:nx + no], rest[nx + no:]
        p = lax.dot_general(a_ref[...].astype(BF16), b_ref[...].astype(BF16), dims, preferred_element_type=F32)

        def emit(v):
            res = post(v, *[r[...] for r in rest[:nx]]) if post else (v,)
            for o_ref, r in zip(o_refs, res):
                o_ref[...] = r.astype(o_ref.dtype)

        if nk == 1:
            emit(p)
        else:
            k = pl.program_id(2)

            @pl.when(k == 0)
            def _():
                acc[0][...] = p

            @pl.when(k > 0)
            def _():
                acc[0][...] += p

            @pl.when(k == nk - 1)
            def _():
                emit(acc[0][...])

    res = pl.pallas_call(
        body, name=name, grid=(m // tm, n // tn, nk),
        in_specs=[a_spec, b_spec] + [o_spec] * nx, out_specs=[o_spec] * no,
        out_shape=[jax.ShapeDtypeStruct(o_shape.shape, dt) for dt in out_dtypes],
        scratch_shapes=[pltpu.VMEM((tm, tn), F32)] if nk > 1 else [],
        compiler_params=_params(("parallel", "parallel", "arbitrary")),
    )(a, b_arr, *extra)
    return res[0] if no == 1 else res


def _rowarg(r):
    return r if isinstance(r, tuple) else (r, r.shape[1], 0)


def _rowspec(r, tr, ncol):
    _, w, cb = r
    if ncol > 1:
        return pl.BlockSpec((tr, w // ncol), lambda j, i: (i, j))
    return pl.BlockSpec((tr, w), lambda j, i: (i, cb))


def _rowwise(name, f, rows, params, outs, tr=None, ncol=1):
    rows = [_rowarg(r) for r in rows]
    t = rows[0][0].shape[0]
    tr = tr or _tile(t, 528)
    nr, npm = len(rows), len(params)

    def body(*refs):
        vals = [r[...] for r in refs[:nr]] + [(p[0] if ncol > 1 else p[...]) for p in refs[nr:nr + npm]]
        res = f(pl.program_id(1) * tr, *vals)
        for o_ref, v in zip(refs[nr + npm:], res):
            o_ref[...] = v.astype(o_ref.dtype)

    def pspec(p):
        if ncol > 1:
            return pl.BlockSpec((1,) + p.shape[1:], lambda j, i, n=p.ndim: (j,) + (0,) * (n - 1))
        return _full_spec(p.shape, 2)

    return pl.pallas_call(
        body, name=name, grid=(ncol, t // tr),
        in_specs=[_rowspec(r, tr, ncol) for r in rows] + [pspec(p) for p in params],
        out_specs=[pl.BlockSpec((tr, w // ncol), lambda j, i: (i, j)) for w, _ in outs],
        out_shape=[jax.ShapeDtypeStruct((t, w), dt) for w, dt in outs],
        compiler_params=_params(("parallel", "parallel")),
    )(*[r[0] for r in rows], *params)


def _rowwise_vjp(name, f, rows, params, cts, tr=None, ncol=1, row_dtypes=None):
    rows = [_rowarg(r) for r in rows]
    cts = [_rowarg(c) for c in cts]
    t = rows[0][0].shape[0]
    tr = tr or _tile(t, 528)
    nr, npm, nc = len(rows), len(params), len(cts)
    row_dtypes = row_dtypes or [F32] * nr

    def body(*refs):
        i = pl.program_id(1)
        vals = [r[...] for r in refs[:nr]] + [(p[0] if ncol > 1 else p[...]) for p in refs[nr:nr + npm]]
        ct = tuple(c[...].astype(F32) for c in refs[nr + npm:nr + npm + nc])
        _, vjp = jax.vjp(lambda *a: tuple(f(i * tr, *a)), *vals)
        g = vjp(ct)
        outs = refs[nr + npm + nc:]
        for o_ref, v in zip(outs[:nr], g[:nr]):
            o_ref[...] = v.astype(o_ref.dtype)
        pg = [(v[None] if ncol > 1 else v) for v in g[nr:]]

        @pl.when(i == 0)
        def _():
            for o_ref, v in zip(outs[nr:], pg):
                o_ref[...] = v

        @pl.when(i > 0)
        def _():
            for o_ref, v in zip(outs[nr:], pg):
                o_ref[...] += v

    def pspec(p):
        if ncol > 1:
            return pl.BlockSpec((1,) + p.shape[1:], lambda j, i, n=p.ndim: (j,) + (0,) * (n - 1))
        return _full_spec(p.shape, 2)

    res = pl.pallas_call(
        body, name=name, grid=(ncol, t // tr),
        in_specs=[_rowspec(r, tr, ncol) for r in rows] + [pspec(p) for p in params] + [_rowspec(c, tr, ncol) for c in cts],
        out_specs=[pl.BlockSpec((tr, w // ncol), lambda j, i: (i, j)) for _, w, _ in rows] + [pspec(p) for p in params],
        out_shape=[jax.ShapeDtypeStruct((t, w), dt) for (_, w, _), dt in zip(rows, row_dtypes)]
        + [jax.ShapeDtypeStruct(p.shape, F32) for p in params],
        compiler_params=_params(("parallel", "arbitrary")),
    )(*[r[0] for r in rows], *params, *[c[0] for c in cts])
    return res[:nr], res[nr:]


def _valid(row0, tr):
    return (row0 + lax.broadcasted_iota(jnp.int32, (tr, 1), 0)) >= PAD


def _rms(x, g):
    return x * lax.rsqrt(jnp.mean(x * x, axis=-1, keepdims=True) + EPS) * g


def _softplus(x):
    return jnp.where(x < -15.0, jnp.exp(x), jnp.maximum(x, 0.0) + jnp.log(1.0 + jnp.exp(-jnp.abs(x))))


def _neg_expm1(z):
    return jnp.where(z > -0.01, -z * (1.0 + z * (0.5 + z * (1.0 / 6.0))), 1.0 - jnp.exp(z))


def _prenorm(h, g, name):
    return _rowwise(name, lambda r0, x, gg: (_rms(x, gg),), [h], [g], [(_rowarg(h)[1], BF16)])[0]


def _add_postnorm(h, ms, g, name):
    def f(r0, x, *rest):
        return (x + _rms(functools.reduce(jnp.add, rest[:-1]), rest[-1]),)

    return _rowwise(name, f, [h] + list(ms), [g], [(h.shape[1], F32)])[0]


def _postnorm_bwd(m, g, dh, name):
    (dm,), (dg,) = _rowwise_vjp(name, lambda r0, mm, gg: (_rms(mm, gg),), [m], [g], [dh])
    return dm, dg


def _prenorm_bwd_add(h, g, dhns, dh, name):
    t, w = h.shape
    tr = _tile(t, 528)
    nd = len(dhns)

    def body(h_ref, g_ref, *refs):
        dh_ref, o_ref, dg_ref = refs[nd:]
        i = pl.program_id(0)
        _, vjp = jax.vjp(_rms, h_ref[...], g_ref[...])
        dhn = refs[0][...].astype(F32)
        for r in refs[1:nd]:
            dhn = dhn + r[...].astype(F32)
        dx, dg = vjp(dhn)
        o_ref[...] = dh_ref[...] + dx

        @pl.when(i == 0)
        def _():
            dg_ref[...] = dg

        @pl.when(i > 0)
        def _():
            dg_ref[...] += dg

    row = pl.BlockSpec((tr, w), lambda i: (i, 0))
    return pl.pallas_call(
        body, name=name, grid=(t // tr,), in_specs=[row, _full_spec(g.shape, 1)] + [row] * (nd + 1),
        out_specs=[row, _full_spec(g.shape, 1)],
        out_shape=[jax.ShapeDtypeStruct((t, w), F32), jax.ShapeDtypeStruct(g.shape, F32)],
        compiler_params=_params(("arbitrary",)),
    )(h, g, *dhns, dh)


def _loss_and_grad(h, target, name):
    t, w = h.shape
    nb = t // CHUNK

    def body(h_ref, t_ref, s_ref, dh_ref):
        i = pl.program_id(0)

        @pl.when(i == 0)
        def _():
            s_ref[...] = jnp.zeros_like(s_ref)
            dh_ref[...] = jnp.zeros_like(dh_ref)

        @pl.when(i > 0)
        def _():
            err = h_ref[...] - t_ref[...]
            s_ref[...] += jnp.sum(err * err)
            dh_ref[...] = err * (1.0 / w)

    return pl.pallas_call(
        body, name=name, grid=(nb,),
        in_specs=[pl.BlockSpec((CHUNK, w), lambda i: (i, 0)), pl.BlockSpec((CHUNK, w), lambda i: (jnp.maximum(i - 1, 0), 0))],
        out_specs=[_full_spec((1, LANES), 1), pl.BlockSpec((CHUNK, w), lambda i: (i, 0))],
        out_shape=[jax.ShapeDtypeStruct((1, LANES), F32), jax.ShapeDtypeStruct((t, w), F32)],
        compiler_params=_params(("arbitrary",)),
    )(h, target)


def _mlp_fwd(h, p, l):
    hn = _prenorm(h, p["mlp_pre_g"][l], "mlp_prenorm")
    a, u = _mm(hn, p["w_up"][l], "nn", "mlp_up", post=lambda v: (v, jnp.square(jnp.maximum(v, 0.0))), out_dtypes=[F32, BF16])
    d = _mm(u, p["w_down"][l], "nn", "mlp_down")
    h2 = _add_postnorm(h, [d], p["mlp_post_g"][l], "mlp_postnorm")
    return h2, (h, hn, a, u, d)


def _mlp_bwd(dh, saved, p, l, grads):
    h, hn, a, u, d = saved
    dd, grads["mlp_post_g"][l] = _postnorm_bwd(d, p["mlp_post_g"][l], dh, "mlp_postnorm_bwd")
    grads["w_down"][l] = _mm(u, dd, "tn", "mlp_down_dw")
    da = _mm(dd, p["w_down"][l], "nt", "mlp_down_dx", extra=[a], post=lambda v, x: (2.0 * jnp.maximum(x, 0.0) * v,),
             out_dtypes=[BF16])
    grads["w_up"][l] = _mm(hn, da, "tn", "mlp_up_dw", out_chip_major=True)
    dhn = _mm(da, p["w_up"][l], "nt", "mlp_up_dx")
    dh, grads["mlp_pre_g"][l] = _prenorm_bwd_add(h, p["mlp_pre_g"][l], [dhn], dh, "mlp_prenorm_bwd")
    return dh


def _dot(a, b, mode):
    return lax.dot_general(a.astype(BF16), b.astype(BF16), _DIMS[mode], preferred_element_type=F32)


@jax.custom_vjp
def _bnn(a, b):
    return _dot(a, b, "nn")


_bnn.defvjp(lambda a, b: (_dot(a, b, "nn"), (a, b)), lambda r, ct: (_dot(ct, r[1], "nt"), _dot(r[0], ct, "tn")))


@jax.custom_vjp
def _bnt(a, b):
    return _dot(a, b, "nt")


_bnt.defvjp(lambda a, b: (_dot(a, b, "nt"), (a, b)), lambda r, ct: (_dot(ct, r[1], "nn"), _dot(ct, r[0], "tn")))


@jax.custom_vjp
def _btn(a, b):
    return _dot(a, b, "tn")


_btn.defvjp(lambda a, b: (_dot(a, b, "tn"), (a, b)), lambda r, ct: (_dot(r[1], ct, "nt"), _dot(r[0], ct, "nn")))


CONV_K = 4
HALO = 8


def _conv_fwd(x, w, b, name, cw, c0=0):
    t, c = x.shape[0], w.shape[1]
    tr = _tile(t, 528)
    hb = tr // HALO

    def body(x_ref, halo_ref, w_ref, b_ref, o_ref, ext):
        i = pl.program_id(1)
        ext[pl.ds(0, HALO), :] = jnp.where(i > 0, halo_ref[...], 0.0)
        ext[pl.ds(HALO, tr), :] = x_ref[...]
        acc = jnp.broadcast_to(b_ref[...], (tr, cw))
        for k in range(CONV_K):
            acc = acc + w_ref[pl.ds(k, 1), :] * ext[pl.ds(HALO - (CONV_K - 1) + k, tr), :]
        o_ref[...] = acc

    return pl.pallas_call(
        body, name=name, grid=(c // cw, t // tr),
        in_specs=[pl.BlockSpec((tr, cw), lambda j, i: (i, c0 + j)),
                  pl.BlockSpec((HALO, cw), lambda j, i: (jnp.maximum(i * hb - 1, 0), c0 + j)),
                  pl.BlockSpec((CONV_K, cw), lambda j, i: (0, j)), pl.BlockSpec((1, cw), lambda j, i: (0, j))],
        out_specs=pl.BlockSpec((tr, cw), lambda j, i: (i, j)),
        out_shape=jax.ShapeDtypeStruct((t, c), F32),
        scratch_shapes=[pltpu.VMEM((tr + HALO, cw), F32)],
        compiler_params=_params(("parallel", "parallel")),
    )(x, x, w, b)


def _conv_bwd(x, w, dy, name, cw, c0=0):
    t, c = x.shape[0], w.shape[1]
    tr = _tile(t, 528)
    hb = tr // HALO
    nb = t // tr

    def body(x_ref, xh_ref, w_ref, dy_ref, dyh_ref, dx_ref, dw_ref, db_ref, xe, de):
        c = cw
        i = pl.program_id(1)
        xe[pl.ds(0, HALO), :] = jnp.where(i > 0, xh_ref[...], 0.0)
        xe[pl.ds(HALO, tr), :] = x_ref[...]
        de[pl.ds(0, tr), :] = dy_ref[...]
        de[pl.ds(tr, HALO), :] = jnp.where(i < nb - 1, dyh_ref[...], 0.0)
        dy = dy_ref[...]
        acc = jnp.zeros((tr, c), F32)
        dw = jnp.zeros((CONV_K, c), F32)
        rows = lax.broadcasted_iota(jnp.int32, (CONV_K, 1), 0)
        for k in range(CONV_K):
            acc = acc + w_ref[pl.ds(k, 1), :] * de[pl.ds(CONV_K - 1 - k, tr), :]
            dwk = jnp.sum(dy * xe[pl.ds(HALO - (CONV_K - 1) + k, tr), :], axis=0, keepdims=True)
            dw = dw + jnp.where(rows == k, dwk, 0.0)
        dx_ref[...] = jnp.where(_valid(i * tr, tr), acc, 0.0)
        db = jnp.sum(dy, axis=0, keepdims=True)

        @pl.when(i == 0)
        def _():
            dw_ref[...] = dw
            db_ref[...] = db

        @pl.when(i > 0)
        def _():
            dw_ref[...] += dw
            db_ref[...] += db

    row = pl.BlockSpec((tr, cw), lambda j, i: (i, j))
    return pl.pallas_call(
        body, name=name, grid=(c // cw, nb),
        in_specs=[pl.BlockSpec((tr, cw), lambda j, i: (i, c0 + j)),
                  pl.BlockSpec((HALO, cw), lambda j, i: (jnp.maximum(i * hb - 1, 0), c0 + j)),
                  pl.BlockSpec((CONV_K, cw), lambda j, i: (0, j)),
                  row, pl.BlockSpec((HALO, cw), lambda j, i: (jnp.minimum((i + 1) * hb, t // HALO - 1), j))],
        out_specs=[row, pl.BlockSpec((CONV_K, cw), lambda j, i: (0, j)), pl.BlockSpec((1, cw), lambda j, i: (0, j))],
        out_shape=[jax.ShapeDtypeStruct((t, c), F32), jax.ShapeDtypeStruct((CONV_K, c), F32), jax.ShapeDtypeStruct((1, c), F32)],
        scratch_shapes=[pltpu.VMEM((tr + HALO, cw), F32), pltpu.VMEM((tr + HALO, cw), F32)],
        compiler_params=_params(("parallel", "arbitrary")),
    )(x, x, w, dy, dy)


SUB = 8


def _lru_scan(a, u, name):
    t, c = a.shape
    tr = _tile(t, 528)

    def body(a_ref, u_ref, o_ref, carry):
        @pl.when(pl.program_id(0) == 0)
        def _():
            carry[...] = jnp.zeros_like(carry)

        rows = lax.broadcasted_iota(jnp.int32, (SUB, 1), 0)

        def step(k, cin):
            r = pl.multiple_of(k * SUB, SUB)
            av, uv = a_ref[pl.ds(r, SUB), :], u_ref[pl.ds(r, SUB), :]
            for d in (1, 2, 4):
                m = rows >= d
                uv = uv + av * jnp.where(m, pltpu.roll(uv, d, 0), 0.0)
                av = av * jnp.where(m, pltpu.roll(av, d, 0), 1.0)
            hv = uv + av * cin
            o_ref[pl.ds(r, SUB), :] = hv
            return jnp.broadcast_to(hv[SUB - 1:SUB, :], (SUB, c))

        carry[...] = lax.fori_loop(0, tr // SUB, step, carry[...])

    row = pl.BlockSpec((tr, c), lambda i: (i, 0))
    return pl.pallas_call(
        body, name=name, grid=(t // tr,), in_specs=[row, row], out_specs=row,
        out_shape=jax.ShapeDtypeStruct((t, c), F32), scratch_shapes=[pltpu.VMEM((SUB, c), F32)],
        compiler_params=_params(("arbitrary",)),
    )(a, u)


def _lru_scan_bwd(a, hs, dy, name):
    t, c = a.shape
    tr = _tile(t, 528)
    nb, nt = t // tr, tr // SUB

    def body(a_ref, h_ref, hh_ref, dy_ref, du_ref, da_ref, gcar, acar):
        i = pl.program_id(0)

        @pl.when(i == 0)
        def _():
            gcar[...] = jnp.zeros_like(gcar)
            acar[...] = jnp.zeros_like(acar)

        rows = lax.broadcasted_iota(jnp.int32, (SUB, 1), 0)
        hhalo = jnp.where(i < nb - 1, hh_ref[...], 0.0)

        def step(kk, car):
            gin, a_next_first = car
            k = nt - 1 - kk
            r = pl.multiple_of(k * SUB, SUB)
            av, hv, dv = a_ref[pl.ds(r, SUB), :], h_ref[pl.ds(r, SUB), :], dy_ref[pl.ds(r, SUB), :]
            rp = pl.multiple_of(jnp.maximum(k - 1, 0) * SUB, SUB)
            hp = jnp.where(k > 0, h_ref[pl.ds(rp, SUB), :], hhalo)
            cv = jnp.where(rows < SUB - 1, pltpu.roll(av, SUB - 1, 0), a_next_first)
            gv = dv
            for d in (1, 2, 4):
                m = rows < SUB - d
                gv = gv + cv * jnp.where(m, pltpu.roll(gv, SUB - d, 0), 0.0)
                cv = cv * jnp.where(m, pltpu.roll(cv, SUB - d, 0), 1.0)
            gv = gv + cv * gin
            hprev = jnp.where(rows >= 1, pltpu.roll(hv, 1, 0), jnp.broadcast_to(hp[SUB - 1:SUB, :], (SUB, c)))
            du_ref[pl.ds(r, SUB), :] = gv
            da_ref[pl.ds(r, SUB), :] = gv * hprev
            return jnp.broadcast_to(gv[0:1, :], (SUB, c)), jnp.broadcast_to(av[0:1, :], (SUB, c))

        g, af = lax.fori_loop(0, nt, step, (gcar[...], acar[...]))
        gcar[...] = g
        acar[...] = af

    hb = tr // SUB
    row = pl.BlockSpec((tr, c), lambda i: (nb - 1 - i, 0))
    halo = pl.BlockSpec((SUB, c), lambda i: (jnp.maximum((nb - 1 - i) * hb - 1, 0), 0))
    return pl.pallas_call(
        body, name=name, grid=(nb,), in_specs=[row, row, halo, row], out_specs=[row, row],
        out_shape=[jax.ShapeDtypeStruct((t, c), F32)] * 2,
        scratch_shapes=[pltpu.VMEM((SUB, c), F32), pltpu.VMEM((SUB, c), F32)],
        compiler_params=_params(("arbitrary",)),
    )(a, hs, hs, dy)


def _lru_gates(row0, xr, wa, ba, wi, bi, lam):
    r = jax.nn.sigmoid(_bnn(xr, wa) + ba)
    i = jax.nn.sigmoid(_bnn(xr, wi) + bi)
    log_a = -LRU_C * r * _softplus(-lam)
    u = jnp.sqrt(_neg_expm1(2.0 * log_a)) * (i * xr)
    return jnp.exp(log_a), jnp.where(_valid(row0, xr.shape[0]), u, 0.0)


def _lru_gate_out(row0, hs, yw):
    return (hs * jax.nn.gelu(yw),)


def _rglru_fwd(h, p, l, o):
    hn = _prenorm(h, p["mix_pre_g"][l], "rg_prenorm")
    xw = _mm(hn, p["rg_w_x"][o], "nn", "rg_in_x")
    yw = _mm(hn, p["rg_w_y"][o], "nn", "rg_in_y")
    xr = _conv_fwd(xw, p["rg_conv_w"][o], p["rg_conv_b"][o], "rg_conv", cw=LRU_WIDTH // 2)
    gp = [p["rg_w_a"][o], p["rg_b_a"][o], p["rg_w_i"][o], p["rg_b_i"][o], p["rg_lambda"][o]]
    a, u = _rowwise("rg_gates", _lru_gates, [xr], gp, [(LRU_WIDTH, F32)] * 2, ncol=LRU_BLOCKS, tr=_tile(h.shape[0], 1056))
    hs = _lru_scan(a, u, "rg_scan")
    hg = _rowwise("rg_gate_out", _lru_gate_out, [hs, yw], [], [(LRU_WIDTH, BF16)])[0]
    m = _mm(hg, p["rg_w_out"][o], "nn", "rg_out")
    h2 = _add_postnorm(h, [m], p["mix_post_g"][l], "rg_postnorm")
    return h2, (h, hn, xw, yw, xr, a, hs, hg, m)


def _rglru_bwd(dh, saved, p, l, o, grads):
    h, hn, xw, yw, xr, a, hs, hg, m = saved
    dm, grads["mix_post_g"][l] = _postnorm_bwd(m, p["mix_post_g"][l], dh, "rg_postnorm_bwd")
    grads["rg_w_out"][o] = _mm(hg, dm, "tn", "rg_out_dw")
    dhg = _mm(dm, p["rg_w_out"][o], "nt", "rg_out_dx")
    (dhs, dyw), _ = _rowwise_vjp("rg_gate_out_bwd", _lru_gate_out, [hs, yw], [], [dhg])
    du, da = _lru_scan_bwd(a, hs, dhs, "rg_scan_bwd")
    gp = [p["rg_w_a"][o], p["rg_b_a"][o], p["rg_w_i"][o], p["rg_b_i"][o], p["rg_lambda"][o]]
    (dxr,), gg = _rowwise_vjp("rg_gates_bwd", _lru_gates, [xr], gp, [da, du], ncol=LRU_BLOCKS, tr=_tile(h.shape[0], 1056))
    grads["rg_w_a"][o], grads["rg_b_a"][o], grads["rg_w_i"][o], grads["rg_b_i"][o], grads["rg_lambda"][o] = gg
    dxw, grads["rg_conv_w"][o], grads["rg_conv_b"][o] = _conv_bwd(xw, p["rg_conv_w"][o], dxr, "rg_conv_bwd", cw=LRU_WIDTH // 2)
    grads["rg_w_x"][o] = _mm(hn, dxw, "tn", "rg_in_x_dw")
    grads["rg_w_y"][o] = _mm(hn, dyw, "tn", "rg_in_y_dw")
    dhx = _mm(dxw, p["rg_w_x"][o], "nt", "rg_in_x_dx")
    dhy = _mm(dyw, p["rg_w_y"][o], "nt", "rg_in_y_dx")
    dh, grads["mix_pre_g"][l] = _prenorm_bwd_add(h, p["mix_pre_g"][l], [dhx, dhy], dh, "rg_prenorm_bwd")
    return dh


SSD_GW = SSD_D_INNER // SSD_GROUPS
SSD_GH = SSD_HEADS // SSD_GROUPS
XACT_B = SSD_D_INNER // SSD_STATE
XACT_C = XACT_B + SSD_GROUPS


def _hp(a, b, dims=_DIMS["nn"]):
    return lax.dot_general(a, b, dims, precision=lax.Precision.HIGHEST, preferred_element_type=F32)


def _split_dot(a, e, mode, parts):
    eb = e.astype(BF16)
    out, rest = None, a
    for _ in range(parts):
        term = rest.astype(BF16)
        rest = rest - term.astype(F32)
        if mode in ("nn", "nt"):
            prod = lax.dot_general(term, eb, _DIMS[mode], preferred_element_type=F32)
        else:
            prod = lax.dot_general(eb, term, _DIMS["nn" if mode == "left" else "tn"], preferred_element_type=F32)
        out = prod if out is None else out + prod
    return out


@jax.custom_vjp
def _select_nn(a, e):
    return _split_dot(a, e, "nn", 3)


_select_nn.defvjp(lambda a, e: (_split_dot(a, e, "nn", 3), e), lambda e, ct: (_split_dot(ct, e, "nt", 2), jnp.zeros_like(e)))


@jax.custom_vjp
def _select_left(e, a):
    return _split_dot(a, e, "left", 3)


_select_left.defvjp(lambda e, a: (_split_dot(a, e, "left", 3), e),
                    lambda e, ct: (jnp.zeros_like(e), _split_dot(ct, e, "left_t", 2)))


def _ssd_chunk(xs, bm, cm, dt, da, ht, g):
    l = CHUNK
    ri = lax.broadcasted_iota(jnp.int32, (l, l), 0)
    ci = lax.broadcasted_iota(jnp.int32, (l, l), 1)
    causal = ri >= ci
    tri = causal.astype(F32)
    hr = lax.broadcasted_iota(jnp.int32, (LANES, SSD_GW), 0)
    hc = lax.broadcasted_iota(jnp.int32, (LANES, SSD_GW), 1)
    expand = (hr == g * SSD_GH + hc // SSD_HEAD_DIM).astype(F32)
    acs = _select_left(tri, da)
    acs_t = acs.T
    acs_e = _select_nn(acs, expand)
    x = xs * _select_nn(dt, expand)
    gmat = _bnt(cm, bm)
    lane = lax.broadcasted_iota(jnp.int32, (1, LANES), 1)
    sub = lax.broadcasted_iota(jnp.int32, (LANES, 1), 0)
    colhead = lax.broadcasted_iota(jnp.int32, (1, SSD_GW), 1) // SSD_HEAD_DIM
    y = _bnn(cm, ht) * jnp.exp(acs_e)
    for k in range(SSD_GH):
        hh = g * SSD_GH + k
        col = jnp.sum(jnp.where(lane == hh, acs, 0.0), axis=1, keepdims=True)
        row = jnp.sum(jnp.where(sub == hh, acs_t, 0.0), axis=0, keepdims=True)
        decay = jnp.exp(jnp.where(causal, col - row, -1e30))
        y = y + _bnn(gmat * decay, jnp.where(colhead == k, x, 0.0))
    last = lax.broadcasted_iota(jnp.int32, (l, 1), 0) == l - 1
    a_last = jnp.sum(jnp.where(last, acs_e, 0.0), axis=0, keepdims=True)
    st = _btn(bm, x * jnp.exp(a_last - acs_e))
    return y, ht * jnp.exp(a_last) + st


def _ssd_specs(nc, rev):
    def cc(c):
        return nc - 1 - c if rev else c

    return [pl.BlockSpec((CHUNK, SSD_GW), lambda c, g: (cc(c), g)),
            pl.BlockSpec((CHUNK, SSD_STATE), lambda c, g: (cc(c), XACT_B + g)),
            pl.BlockSpec((CHUNK, SSD_STATE), lambda c, g: (cc(c), XACT_C + g)),
            pl.BlockSpec((CHUNK, LANES), lambda c, g: (cc(c), 0)),
            pl.BlockSpec((CHUNK, LANES), lambda c, g: (cc(c), 0))]


def _ssd_scan(xact, dt, da, name):
    t = xact.shape[0]
    nc = t // CHUNK

    def body(xs_ref, b_ref, c_ref, dt_ref, da_ref, y_ref, hs_ref, state):
        c, g = pl.program_id(0), pl.program_id(1)

        @pl.when(c == 0)
        def _():
            state[g] = jnp.zeros((SSD_STATE, SSD_GW), F32)

        ht = state[g]
        hs_ref[0] = ht
        y, ht2 = _ssd_chunk(xs_ref[...], b_ref[...], c_ref[...], dt_ref[...], da_ref[...], ht, g)
        y_ref[...] = y
        state[g] = ht2

    return pl.pallas_call(
        body, name=name, grid=(nc, SSD_GROUPS), in_specs=_ssd_specs(nc, False),
        out_specs=[pl.BlockSpec((CHUNK, SSD_GW), lambda c, g: (c, g)),
                   pl.BlockSpec((1, SSD_STATE, SSD_GW), lambda c, g: (c * SSD_GROUPS + g, 0, 0))],
        out_shape=[jax.ShapeDtypeStruct((t, SSD_D_INNER), F32), jax.ShapeDtypeStruct((nc * SSD_GROUPS, SSD_STATE, SSD_GW), F32)],
        scratch_shapes=[pltpu.VMEM((SSD_GROUPS, SSD_STATE, SSD_GW), F32)],
        compiler_params=_params(("arbitrary", "arbitrary")),
    )(xact, xact, xact, dt, da)


def _ssd_scan_bwd(xact, dt, da, hsave, dy, dxskip, name):
    t = xact.shape[0]
    nc = t // CHUNK

    def body(xs_ref, b_ref, c_ref, dt_ref, da_ref, hs_ref, dy_ref, sk_ref, dxs_ref, db_ref, dc_ref, ddt_ref, dda_ref, dstate):
        c, g = pl.program_id(0), pl.program_id(1)

        @pl.when(c == 0)
        def _():
            dstate[g] = jnp.zeros((SSD_STATE, SSD_GW), F32)

        _, vjp = jax.vjp(lambda *a: _ssd_chunk(*a, g), xs_ref[...], b_ref[...], c_ref[...], dt_ref[...], da_ref[...], hs_ref[0])
        dxs, dbm, dcm, ddt, dda, dht = vjp((dy_ref[...], dstate[g]))
        dxs_ref[...] = dxs + sk_ref[...]
        db_ref[...] = dbm
        dc_ref[...] = dcm
        dstate[g] = dht

        @pl.when(g == 0)
        def _():
            ddt_ref[...] = ddt
            dda_ref[...] = dda

        @pl.when(g > 0)
        def _():
            ddt_ref[...] += ddt
            dda_ref[...] += dda

    grp = pl.BlockSpec((CHUNK, SSD_GW), lambda c, g: (nc - 1 - c, g))
    st = pl.BlockSpec((CHUNK, SSD_STATE), lambda c, g: (nc - 1 - c, g))
    hd = pl.BlockSpec((CHUNK, LANES), lambda c, g: (nc - 1 - c, 0))
    return pl.pallas_call(
        body, name=name, grid=(nc, SSD_GROUPS),
        in_specs=_ssd_specs(nc, True) + [pl.BlockSpec((1, SSD_STATE, SSD_GW), lambda c, g: ((nc - 1 - c) * SSD_GROUPS + g, 0, 0)), grp, grp],
        out_specs=[grp, st, st, hd, hd],
        out_shape=[jax.ShapeDtypeStruct((t, SSD_D_INNER), F32), jax.ShapeDtypeStruct((t, SSD_GROUPS * SSD_STATE), F32),
                   jax.ShapeDtypeStruct((t, SSD_GROUPS * SSD_STATE), F32), jax.ShapeDtypeStruct((t, LANES), F32),
                   jax.ShapeDtypeStruct((t, LANES), F32)],
        scratch_shapes=[pltpu.VMEM((SSD_GROUPS, SSD_STATE, SSD_GW), F32)],
        compiler_params=_params(("arbitrary", "arbitrary")),
    )(xact, xact, xact, dt, da, hsave, dy, dxskip)


def _ssd_act(row0, xc):
    return (jnp.where(_valid(row0, xc.shape[0]), jax.nn.silu(xc), 0.0),)


def _ssd_dt(row0, dtraw, dt_bias, a_log):
    dt = jnp.where(_valid(row0, dtraw.shape[0]), _softplus(dtraw + dt_bias), 0.0)
    return dt, dt * -jnp.exp(a_log)


def _ssd_post(row0, y, xs, z, d_skip, norm_g):
    hr = lax.broadcasted_iota(jnp.int32, (LANES, SSD_D_INNER), 0)
    hc = lax.broadcasted_iota(jnp.int32, (LANES, SSD_D_INNER), 1)
    expand = (hr == hc // SSD_HEAD_DIM).astype(F32)
    d_e = jnp.sum(_hp(jnp.broadcast_to(d_skip, (SUB, LANES)), expand), axis=0, keepdims=True) * (1.0 / SUB)
    return (_rms((y + xs * d_e) * jax.nn.silu(z), norm_g),)


ROPE_LO, ROPE_MID, ROPE_HI = MLA_NOPE, MLA_NOPE + MLA_ROPE // 2, MLA_NOPE + MLA_ROPE
ATT_SCALE = (MLA_NOPE + MLA_ROPE) ** -0.5


def _slot_lane(width):
    return lax.broadcasted_iota(jnp.int32, (1, width), 1) % LANES


def _swap_halves(x):
    width = x.shape[1]
    lane = _slot_lane(width)
    sw = jnp.where(lane < ROPE_MID, pltpu.roll(x, width - MLA_ROPE // 2, 1), pltpu.roll(x, MLA_ROPE // 2, 1))
    return jnp.where((lane >= ROPE_LO) & (lane < ROPE_HI), sw, 0.0)


def _rope(x, cos, sin):
    n = x.shape[1] // LANES
    return x * jnp.tile(cos, (1, n)) + _swap_halves(x) * jnp.tile(sin, (1, n))


def _rope_t(dy, cos, sin):
    n = dy.shape[1] // LANES
    return dy * jnp.tile(cos, (1, n)) + _swap_halves(dy * jnp.tile(sin, (1, n)))


ATT_SCALE2 = ATT_SCALE * float(np.log2(np.e))
MASKED = -1e30
ATT_STRIP = 64


def _att_mask(i, j, blk):
    rowid = i * blk + lax.broadcasted_iota(jnp.int32, (blk, 1), 0)
    colid = j * blk + lax.broadcasted_iota(jnp.int32, (1, blk), 1)
    return (colid <= rowid) & (colid >= PAD)


def _att_bias(blk):
    r = jnp.arange(blk)[:, None]
    c = jnp.arange(blk)[None, :]
    zero = jnp.zeros((blk, blk), F32)
    first = jnp.where(c >= PAD, 0.0, MASKED) + zero
    diag = jnp.where(c <= r, 0.0, MASKED).astype(F32)
    return jnp.stack([zero, first, diag, jnp.minimum(first, diag), zero + MASKED])


def _att_bias_index(j, i):
    return jnp.where(j > i, 4, jnp.where(j == 0, 1, 0) + jnp.where(j == i, 2, 0))


def _key_slots(row0, kv, kr):
    width = kv.shape[1]
    return jnp.where(_slot_lane(width) < MLA_NOPE, kv, jnp.tile(kr, (1, width // LANES))), kv


def _attn_fwd(qr, km, vb, name, carried=None):
    t = qr.shape[0]
    blk = _tile(t, 384, LANES)
    nq = t // blk

    bias = _att_bias(blk)

    def body(q_ref, k_ref, v_ref, b_ref, o_ref, s0, s1, p0, p1):
        i = pl.program_id(1)
        lane = lax.broadcasted_iota(jnp.int32, (1, LANES), 1)
        qb = q_ref[...]

        def rows(j):
            return pl.ds(pl.multiple_of(jnp.clip(j, 0, i) * blk, blk), blk)

        def scores(j):
            return lax.dot_general(qb, k_ref[rows(j), :], _DIMS["nt"], preferred_element_type=F32) + b_ref[_att_bias_index(j, i)]

        def half(j, car, s_cur, s_nxt, p_cur, p_prv):
            m, l, acc, al_prev = car
            s_nxt[...] = scores(j + 1)
            acc2 = al_prev * acc + lax.dot_general(p_prv[...], v_ref[rows(j - 1), :], _DIMS["nn"], preferred_element_type=F32)
            m2 = jnp.maximum(m, jnp.max(s_cur[...], axis=1, keepdims=True))
            al = jnp.exp2((m - m2) * ATT_SCALE2)
            pm = jnp.exp2(s_cur[...] * ATT_SCALE2 - m2 * ATT_SCALE2)
            p_cur[...] = pm.astype(BF16)
            return m2, al * l + jnp.sum(pm, axis=1, keepdims=True), acc2, al

        def step(jj, car):
            car = half(2 * jj, car, s0, s1, p0, p1)
            return half(2 * jj + 1, car, s1, s0, p1, p0)

        s0[...] = scores(0)
        p1[...] = jnp.zeros((blk, blk), BF16)
        car = (jnp.full((blk, 1), MASKED, F32), jnp.zeros((blk, 1), F32), jnp.zeros((blk, LANES), F32), jnp.ones((blk, 1), F32))
        steps = i // 2 + 1
        m, l, acc, al_last = lax.fori_loop(0, steps, step, car)
        acc = al_last * acc + lax.dot_general(p1[...], v_ref[rows(2 * steps - 1), :], _DIMS["nn"], preferred_element_type=F32)
        out = jnp.where(lane >= MLA_NOPE, acc / l, m * ATT_SCALE + jnp.log(l))
        o_ref[...] = jnp.where(_valid(i * blk, blk), out, 0.0)

    seq_h = pl.BlockSpec((t, LANES), lambda h, i: (0, h))
    (o,), carried_out = _carry_call(
        body, name, (MLA_HEADS, nq),
        [pl.BlockSpec((blk, LANES), lambda h, i: (i, h)), seq_h, seq_h, _full_spec(bias.shape, 2)],
        [pl.BlockSpec((blk, LANES), lambda h, i: (i, h))], [jax.ShapeDtypeStruct((t, MLA_HEADS * LANES), F32)],
        [pltpu.VMEM((blk, blk), F32)] * 2 + [pltpu.VMEM((blk, blk), BF16)] * 2, (qr, km, vb, bias), carried)
    return o, carried_out


def _attn_bwd(qr, km, vb, o, do, name, carried=None):
    t = qr.shape[0]
    blk = _tile(t, 384, LANES)
    nq = t // blk

    bias = _att_bias(blk)
    log2e = float(np.log2(np.e))

    def body(q_ref, o_ref, do_ref, k_ref, v_ref, b_ref, dq_ref, dkv_ref, dkr_ref, s0, s1, dp0, dp1, p0, p1, ds0, ds1):
        h, j = pl.program_id(0), pl.program_id(1)
        lane = lax.broadcasted_iota(jnp.int32, (1, LANES), 1)

        @pl.when(j == 0)
        def _():
            dq_ref[...] = jnp.zeros_like(dq_ref)

        @pl.when((h == 0) & (j == 0))
        def _():
            dkr_ref[...] = jnp.zeros_like(dkr_ref)

        kmat, vmat = k_ref[...], v_ref[...]

        def rows(i):
            return pl.ds(pl.multiple_of(jnp.clip(i, j, nq - 1) * blk, blk), blk)

        def first_stage(i, s_buf, dp_buf):
            bidx = jnp.where(i >= nq, 4, _att_bias_index(j, i))
            s_buf[...] = lax.dot_general(q_ref[rows(i), :], kmat, _DIMS["nt"], preferred_element_type=F32) + b_ref[bidx]
            dp_buf[...] = lax.dot_general(do_ref[rows(i), :].astype(BF16), vmat, _DIMS["nt"], preferred_element_type=F32)

        def last_stage(i, car, p_buf, ds_buf):
            dk, dv = car
            r = rows(i)
            dv = dv + lax.dot_general(p_buf[...], do_ref[r, :].astype(BF16), _DIMS["tn"], preferred_element_type=F32)
            dk = dk + lax.dot_general(ds_buf[...], q_ref[r, :], _DIMS["tn"], preferred_element_type=F32)
            dq_ref[r, :] += lax.dot_general(ds_buf[...], kmat, _DIMS["nn"], preferred_element_type=F32)
            return dk, dv

        def half(i, car, s_cur, dp_cur, p_cur, ds_cur, s_nxt, dp_nxt, p_prv, ds_prv):
            first_stage(i + 1, s_nxt, dp_nxt)
            car = last_stage(i - 1, car, p_prv, ds_prv)
            r = rows(i)
            ob, dob = o_ref[r, :], do_ref[r, :]
            delta = jnp.sum(dob * ob, axis=1, keepdims=True)
            pm = jnp.exp2(s_cur[...] * ATT_SCALE2 - ob[:, 0:1] * log2e)
            p_cur[...] = pm.astype(BF16)
            ds_cur[...] = (pm * (dp_cur[...] - delta) * ATT_SCALE).astype(BF16)
            return car

        def step(tt, car):
            i = j + 2 * tt
            car = half(i, car, s0, dp0, p0, ds0, s1, dp1, p1, ds1)
            return half(i + 1, car, s1, dp1, p1, ds1, s0, dp0, p0, ds0)

        first_stage(j, s0, dp0)
        p1[...] = jnp.zeros((blk, blk), BF16)
        ds1[...] = jnp.zeros((blk, blk), BF16)
        zero = jnp.zeros((blk, LANES), F32)
        steps = (nq - j + 1) // 2
        car = lax.fori_loop(0, steps, step, (zero, zero))
        dk, dv = last_stage(j + 2 * steps - 1, car, p1, ds1)
        dkv_ref[...] = jnp.where(lane < MLA_NOPE, dk, dv)
        dkr_ref[rows(j), :] += jnp.where(lane >= MLA_NOPE, dk, 0.0)

    seq_h = pl.BlockSpec((t, LANES), lambda h, j: (0, h))
    blk_h = pl.BlockSpec((blk, LANES), lambda h, j: (j, h))
    return _carry_call(
        body, name, (MLA_HEADS, nq), [seq_h, seq_h, seq_h, blk_h, blk_h, _full_spec(bias.shape, 2)],
        [seq_h, blk_h, pl.BlockSpec((t, LANES), lambda h, j: (0, 0))],
        [jax.ShapeDtypeStruct((t, MLA_HEADS * LANES), F32), jax.ShapeDtypeStruct((t, MLA_HEADS * LANES), F32),
         jax.ShapeDtypeStruct((t, LANES), F32)],
        [pltpu.VMEM((blk, blk), F32)] * 4 + [pltpu.VMEM((blk, blk), BF16)] * 4, (qr, o, do, km, vb, bias), carried)


def _rms_rows(row0, x, g):
    return (_rms(x, g),)


def _ssdmla_fwd(h, p, l, e, cos, sin, carried=None):
    hn = _prenorm(h, p["mix_pre_g"][l], "sm_prenorm")
    proj = _mm(hn, p["w_in"][e], "nn", "sm_in")
    xc = _conv_fwd(proj, p["ssd_conv_w"][e], p["ssd_conv_b"][e], "ssd_conv", cw=SSD_GW, c0=PROJ_XBC // SSD_GW)
    xact = _rowwise("ssd_act", _ssd_act, [xc], [], [(SSD_CONV_CH, F32)])[0]
    dt, da = _rowwise("ssd_dt", _ssd_dt, [(proj, LANES, PROJ_DT // LANES)], [p["ssd_dt_bias"][e], p["ssd_a_log"][e]],
                      [(LANES, F32)] * 2)
    y, hsave = _ssd_scan(xact, dt, da, "ssd_scan")
    y_ssd = _rowwise("ssd_post", _ssd_post, [y, (xact, SSD_D_INNER, 0), (proj, SSD_D_INNER, 0)],
                     [p["ssd_d"][e], p["ssd_norm_g"][e]], [(SSD_D_INNER, BF16)])[0]
    cqn = _prenorm((proj, MLA_Q_RANK, PROJ_CQ // MLA_Q_RANK), p["mla_q_norm_g"][e], "mla_qnorm")
    ckvn = _prenorm((proj, MLA_KV_RANK, PROJ_CKV // MLA_KV_RANK), p["mla_kv_norm_g"][e], "mla_kvnorm")
    q = _mm(cqn, p["mla_w_q_up"][e], "nn", "mla_q_up")
    kv = _mm(ckvn, p["mla_w_kv_up"][e], "nn", "mla_kv_up")
    kr = _rowwise("mla_krope", lambda r0, x, c, s: (_rope(x, c, s),), [(proj, LANES, PROJ_KR // LANES), cos, sin], [],
                  [(LANES, F32)])[0]
    slots, tr = MLA_HEADS * LANES, _tile(h.shape[0], 264, 16)
    qr = _rowwise("mla_q_rope", lambda r0, a, c, s: (_rope(a, c, s),), [q, cos, sin], [], [(slots, BF16)], tr=tr)[0]
    km, vb = _rowwise("mla_key_slots", _key_slots, [kv, kr], [], [(slots, BF16)] * 2, tr=tr)
    o, carried_out = _attn_fwd(qr, km, vb, "mla_attn", carried)
    m1 = _mm(y_ssd, p["w_out_ssd"][e], "nn", "sm_out_ssd")
    m = _mm(o, p["w_out_att"][e], "nn", "sm_out_att", add=m1)
    h2 = _add_postnorm(h, [m], p["mix_post_g"][l], "sm_postnorm")
    return h2, (h, hn, proj, xc, xact, dt, da, y, hsave, y_ssd, cqn, ckvn, qr, km, vb, o, m), carried_out


def _ssdmla_bwd(dh, saved, p, l, e, cos, sin, grads, carried=None):
    h, hn, proj, xc, xact, dt, da, y, hsave, y_ssd, cqn, ckvn, qr, km, vb, o, m = saved
    dm, grads["mix_post_g"][l] = _postnorm_bwd(m, p["mix_post_g"][l], dh, "sm_postnorm_bwd")
    grads["w_out_ssd"][e] = _mm(y_ssd, dm, "tn", "sm_out_ssd_dw")
    grads["w_out_att"][e] = _mm(o, dm, "tn", "sm_out_att_dw")
    dy_ssd = _mm(dm, p["w_out_ssd"][e], "nt", "sm_out_ssd_dx")
    do = _mm(dm, p["w_out_att"][e], "nt", "sm_out_att_dx")
    (dqr, dkv, dkr), carried_out = _attn_bwd(qr, km, vb, o, do, "mla_attn_bwd", carried)
    dq = _rowwise("mla_q_rope_bwd", lambda r0, a, c, s: (_rope_t(a, c, s),), [dqr, cos, sin], [], [(MLA_HEADS * LANES, F32)],
                  tr=_tile(h.shape[0], 264, 16))[0]
    dkr_raw = _rowwise("mla_krope_bwd", lambda r0, d, c, s: (_rope_t(d, c, s),), [dkr, cos, sin], [], [(LANES, F32)])[0]
    grads["mla_w_q_up"][e] = _mm(cqn, dq, "tn", "mla_q_up_dw")
    dcqn = _mm(dq, p["mla_w_q_up"][e], "nt", "mla_q_up_dx")
    (dcq,), (grads["mla_q_norm_g"][e],) = _rowwise_vjp(
        "mla_qnorm_bwd", _rms_rows, [(proj, MLA_Q_RANK, PROJ_CQ // MLA_Q_RANK)], [p["mla_q_norm_g"][e]], [dcqn])
    grads["mla_w_kv_up"][e] = _mm(ckvn, dkv, "tn", "mla_kv_up_dw")
    dckvn = _mm(dkv, p["mla_w_kv_up"][e], "nt", "mla_kv_up_dx")
    (dckv,), (grads["mla_kv_norm_g"][e],) = _rowwise_vjp(
        "mla_kvnorm_bwd", _rms_rows, [(proj, MLA_KV_RANK, PROJ_CKV // MLA_KV_RANK)], [p["mla_kv_norm_g"][e]], [dckvn])
    (dy, dxskip, dz), (grads["ssd_d"][e], grads["ssd_norm_g"][e]) = _rowwise_vjp(
        "ssd_post_bwd", _ssd_post, [y, (xact, SSD_D_INNER, 0), (proj, SSD_D_INNER, 0)], [p["ssd_d"][e], p["ssd_norm_g"][e]], [dy_ssd])
    dxs, db, dc, ddt, dda = _ssd_scan_bwd(xact, dt, da, hsave, dy, dxskip, "ssd_scan_bwd")
    dxact = jnp.concatenate([dxs, db, dc], axis=1)
    (dxc,), _ = _rowwise_vjp("ssd_act_bwd", _ssd_act, [xc], [], [dxact])
    dxbc, grads["ssd_conv_w"][e], grads["ssd_conv_b"][e] = _conv_bwd(
        proj, p["ssd_conv_w"][e], dxc, "ssd_conv_bwd", cw=SSD_GW, c0=PROJ_XBC // SSD_GW)
    (ddtraw,), (grads["ssd_dt_bias"][e], grads["ssd_a_log"][e]) = _rowwise_vjp(
        "ssd_dt_bwd", _ssd_dt, [(proj, LANES, PROJ_DT // LANES)], [p["ssd_dt_bias"][e], p["ssd_a_log"][e]], [ddt, dda])
    dproj = jnp.concatenate([dz, dxbc, ddtraw, dcq, dckv, dkr_raw], axis=1)
    grads["w_in"][e] = _mm(hn, dproj, "tn", "sm_in_dw")
    dhn = _mm(dproj, p["w_in"][e], "nt", "sm_in_dx")
    dh, grads["mix_pre_g"][l] = _prenorm_bwd_add(h, p["mix_pre_g"][l], [dhn], dh, "sm_prenorm_bwd")
    return dh, carried_out


GAINS = ("mix_pre_g", "mix_post_g", "mlp_pre_g", "mlp_post_g", "ssd_norm_g", "mla_q_norm_g", "mla_kv_norm_g", "ssd_conv_b", "rg_conv_b")
HEAD_VECS = ("ssd_dt_bias", "ssd_a_log", "ssd_d")
LRU_VECS = ("rg_b_a", "rg_b_i", "rg_lambda")
IN_DT_END = SSD_D_INNER + SSD_CONV_CH + SSD_HEADS
IN_KR = IN_DT_END + MLA_Q_RANK + MLA_KV_RANK


def _each(a, f):
    layers = a if isinstance(a, list) else [a[i] for i in range(a.shape[0])]
    return [None if x is None else f(x) for x in layers]


def _layout_params(w):
    p = {k: _each(w[k], lambda a: a[None, :]) for k in GAINS}
    for k in HEAD_VECS:
        p[k] = _each(w[k], lambda a: jnp.pad(a, (0, LANES - SSD_HEADS))[None, :])
    for k in LRU_VECS:
        p[k] = _each(w[k], lambda a: a.reshape(LRU_BLOCKS, 1, LRU_BLOCK))
    for k in ("w_up", "w_down", "mla_w_kv_up", "rg_w_x", "rg_w_y", "rg_w_out"):
        p[k] = _each(w[k], lambda a: a if isinstance(a, Gathered) else a.astype(BF16))
    for k in ("ssd_conv_w", "rg_conv_w", "rg_w_a", "rg_w_i"):
        p[k] = _each(w[k], lambda a: a)

    def w_in(a):
        def zcols(n):
            return jnp.zeros((a.shape[0], n), a.dtype)

        return jnp.concatenate([a[:, :IN_DT_END], zcols(PROJ_CQ - IN_DT_END), a[:, IN_DT_END:IN_KR], zcols(ROPE_LO),
                                a[:, IN_KR:], zcols(LANES - ROPE_HI)], axis=1).astype(BF16)

    def q_up(a):
        a = a.reshape(MLA_Q_RANK, MLA_HEADS, MLA_NOPE + MLA_ROPE)
        return jnp.pad(a, ((0, 0), (0, 0), (0, LANES - MLA_NOPE - MLA_ROPE))).reshape(MLA_Q_RANK, MLA_HEADS * LANES).astype(BF16)

    def out_att(a):
        a = a[SSD_D_INNER:].reshape(MLA_HEADS, MLA_V, D_MODEL)
        return jnp.pad(a, ((0, 0), (LANES - MLA_V, 0), (0, 0))).reshape(MLA_HEADS * LANES, D_MODEL).astype(BF16)

    p["w_in"] = _each(w["w_in"], w_in)
    p["mla_w_q_up"] = _each(w["mla_w_q_up"], q_up)
    p["w_out_ssd"] = _each(w["w_out_ab"], lambda a: a[:SSD_D_INNER].astype(BF16))
    p["w_out_att"] = _each(w["w_out_ab"], out_att)
    return p


def _rope_tables(t):
    pos = (jnp.arange(t) - PAD).astype(F32)
    inv = ROPE_BASE ** (-jnp.arange(0, MLA_ROPE, 2, dtype=F32) / MLA_ROPE)
    ang = pos[:, None] * inv[None, :]
    c, s = jnp.cos(ang), jnp.sin(ang)
    one, zero = jnp.ones((t, MLA_NOPE), F32), jnp.zeros((t, MLA_NOPE), F32)
    tail = LANES - ROPE_HI
    return (jnp.concatenate([one, c, c, one[:, :tail]], axis=1), jnp.concatenate([zero, -s, s, zero[:, :tail]], axis=1))


GRAD_KEYS = GAINS + HEAD_VECS + LRU_VECS + ("w_up", "w_down", "mla_w_kv_up", "rg_w_x", "rg_w_y", "rg_w_out", "ssd_conv_w",
                                            "rg_conv_w", "rg_w_a", "rg_w_i", "w_in", "mla_w_q_up", "w_out_ssd", "w_out_att")


def _device_step(x, meta, target, p, hooks=None):
    t = PAD + N_META + x.shape[0]
    cos, sin = _rope_tables(t)
    h = jnp.concatenate([jnp.zeros((PAD, D_MODEL), F32), meta, x], axis=0)
    n_even, n_odd = (DEPTH + 1) // 2, DEPTH // 2
    saved = []
    for l in range(DEPTH):
        if l % 2 == 0:
            carried = hooks.forward_exchange() if hooks and l == 0 else None
            h, sm, arrived = _ssdmla_fwd(h, p, l, l // 2, cos, sin, carried)
            if carried is not None:
                p = hooks.after_forward_exchange(arrived)
        else:
            h, sm = _rglru_fwd(h, p, l, l // 2)
        h, sp = _mlp_fwd(h, p, l)
        saved.append((sm, sp))
    sq, dh = _loss_and_grad(h, target, "loss")
    per_layer = {"mix_pre_g": DEPTH, "mix_post_g": DEPTH, "mlp_pre_g": DEPTH, "mlp_post_g": DEPTH, "w_up": DEPTH, "w_down": DEPTH}
    grads = {k: [None] * per_layer.get(k, n_odd if k.startswith("rg_") else n_even) for k in GRAD_KEYS}
    for l in reversed(range(DEPTH)):
        sm, sp = saved[l]
        dh = _mlp_bwd(dh, sp, p, l, grads)
        if l % 2 == 0:
            carried = hooks.backward_exchange(grads, l) if hooks else None
            dh, arrived = _ssdmla_bwd(dh, sm, p, l, l // 2, cos, sin, grads, carried)
            if carried is not None:
                hooks.after_backward_exchange(arrived, l)
        else:
            dh = _rglru_bwd(dh, sm, p, l, l // 2, grads)
    return sq, dh, grads


MESH = pl.DeviceIdType.MESH
ANY = pl.BlockSpec(memory_space=pl.ANY)


def _mesh_pos():
    return lax.axis_index("x"), lax.axis_index("y"), lax.axis_index("c")


def _other_chips(x, y):
    return [(1 - x, y), (x, 1 - y), (1 - x, 1 - y)]


def _remote(src, dst, send_sems, recv_sems, k, to):
    return pltpu.make_async_remote_copy(src_ref=src, dst_ref=dst, send_sem=send_sems.at[k], recv_sem=recv_sems.at[k],
                                        device_id=to, device_id_type=MESH)


class Exchange:
    def __init__(self, ins, outs, aliases, n_sems, plan):
        self.ins, self.outs, self.aliases, self.n_sems, self.plan = list(ins), list(outs), dict(aliases), n_sems, plan


def _sems(n):
    return [pltpu.SemaphoreType.DMA((n,)), pltpu.SemaphoreType.DMA((n,))]


def _run_exchange(name, ex):
    ni, no = len(ex.ins), len(ex.outs)

    def body(*refs):
        sends = ex.plan(refs[:ni], refs[ni:ni + no], refs[-2], refs[-1], False)
        for cp in sends:
            cp.start()
        for cp in ex.plan(refs[:ni], refs[ni:ni + no], refs[-2], refs[-1], True):
            cp.wait_recv()
        for cp in sends:
            cp.wait_send()

    return pl.pallas_call(body, name=name, in_specs=[ANY] * ni, out_specs=[ANY] * no, out_shape=ex.outs,
                          input_output_aliases=ex.aliases, scratch_shapes=_sems(ex.n_sems))(*ex.ins)


def _carry_call(body, name, grid, in_specs, out_specs, out_shape, scratch_shapes, args, ex):
    if ex is None:
        res = pl.pallas_call(body, name=name, grid=grid, in_specs=in_specs, out_specs=out_specs, out_shape=out_shape,
                             scratch_shapes=scratch_shapes, compiler_params=_params(("arbitrary",) * len(grid)))(*args)
        return res, None
    ni, no, ns, xi, xo = len(in_specs), len(out_specs), len(scratch_shapes), len(ex.ins), len(ex.outs)

    def wrapped(*refs):
        ins, xin = refs[:ni], refs[ni:ni + xi]
        outs, xout = refs[ni + xi:ni + xi + no], refs[ni + xi + no:ni + xi + no + xo]
        scr, send_sems, recv_sems = refs[ni + xi + no + xo:-2], refs[-2], refs[-1]
        pid = [pl.program_id(d) for d in range(len(grid))]
        first = functools.reduce(jnp.logical_and, [p == 0 for p in pid])
        last = functools.reduce(jnp.logical_and, [p == g - 1 for p, g in zip(pid, grid)])

        @pl.when(first)
        def _():
            for cp in ex.plan(xin, xout, send_sems, recv_sems, False):
                cp.start()

        body(*ins, *outs, *scr)

        @pl.when(last)
        def _():
            for cp in ex.plan(xin, xout, send_sems, recv_sems, True):
                cp.wait_recv()
            for cp in ex.plan(xin, xout, send_sems, recv_sems, False):
                cp.wait_send()

    res = pl.pallas_call(
        wrapped, name=name, grid=grid, in_specs=list(in_specs) + [ANY] * xi, out_specs=list(out_specs) + [ANY] * xo,
        out_shape=list(out_shape) + ex.outs, scratch_shapes=list(scratch_shapes) + _sems(ex.n_sems),
        input_output_aliases={ni + i: no + o for i, o in ex.aliases.items()},
        compiler_params=_params(("arbitrary",) * len(grid)))(*args, *ex.ins)
    return res[:no], res[no:]


def _gather_ici(srcs, bufs, ranges):
    n = len(srcs)

    def plan(in_refs, out_refs, ss, rs, arrivals):
        x, y, c = _mesh_pos()
        cps = []
        for t, (l0, nl) in enumerate(ranges):
            if nl:
                s, o, lr = in_refs[t], out_refs[t], pl.ds(l0, nl)
                for j, (cx, cy) in enumerate(_other_chips(x, y)):
                    chip = 2 * cx + cy if arrivals else 2 * x + y
                    cps.append(_remote(s.at[lr, c], o.at[chip, lr, c], ss, rs, (N_CHIPS - 1) * t + j, (cx, cy, c)))
        return cps

    outs = [jax.ShapeDtypeStruct((N_CHIPS,) + s.shape, s.dtype) for s in srcs]
    if bufs is None:
        return Exchange(srcs, outs, {}, (N_CHIPS - 1) * n, plan)
    return Exchange(list(srcs) + list(bufs), outs, {n + t: t for t in range(n)}, (N_CHIPS - 1) * n, plan)


def _gather_d2d(srcs, bufs, ranges):
    n = len(srcs)

    def plan(in_refs, out_refs, ss, rs, arrivals):
        x, y, c = _mesh_pos()
        sib, me = (x, y, 1 - c), 2 * x + y
        cps = []
        for t, (l0, nl) in enumerate(ranges):
            if nl:
                s, o, lr = in_refs[t], out_refs[t], pl.ds(l0, nl)
                for j, (cx, cy) in enumerate(_other_chips(x, y)):
                    slot = o.at[2 * cx + cy, lr, c]
                    cps.append(_remote(slot, o.at[2 * cx + cy, lr, 1 - c] if arrivals else slot, ss, rs, N_CHIPS * t + j, sib))
                cps.append(_remote(s.at[lr], o.at[me, lr], ss, rs, N_CHIPS * t + N_CHIPS - 1, sib))
        return cps

    outs = [jax.ShapeDtypeStruct(b.shape, b.dtype) for b in bufs]
    return Exchange(list(srcs) + list(bufs), outs, {n + t: t for t in range(n)}, N_CHIPS * n, plan)


def _gather_chips(srcs, name):
    ranges = [(0, s.shape[0]) for s in srcs]
    bufs = _run_exchange(name + "_ici", _gather_ici(srcs, None, ranges))
    return _run_exchange(name + "_d2d", _gather_d2d(srcs, bufs, ranges))


def _pair_exchange(gs):
    def plan(in_refs, out_refs, ss, rs, arrivals):
        x, y, c = _mesh_pos()
        return [_remote(g.at[pl.ds(0, N_CHIPS), 1 - c], o, ss, rs, t, (x, y, 1 - c)) for t, (g, o) in enumerate(zip(in_refs, out_refs))]

    return Exchange(gs, [jax.ShapeDtypeStruct((g.shape[0],) + g.shape[2:], g.dtype) for g in gs], {}, len(gs), plan)


def _chip_exchange(ps, slots, qs, q_shapes):
    n = len(ps)
    kept = [g for g, q in enumerate(qs) if q is not None]

    def plan(in_refs, out_refs, ss, rs, arrivals):
        x, y, c = _mesh_pos()
        return [_remote(in_refs[t].at[2 * cx + cy], out_refs[g].at[j, li], ss, rs, (N_CHIPS - 1) * t + j, (cx, cy, c))
                for t, (g, li) in enumerate(slots) for j, (cx, cy) in enumerate(_other_chips(x, y))]

    return Exchange(list(ps) + [qs[g] for g in kept], q_shapes, {n + i: g for i, g in enumerate(kept)}, (N_CHIPS - 1) * n, plan)


def _pair_share(fs):
    def plan(in_refs, out_refs, ss, rs, arrivals):
        x, y, c = _mesh_pos()
        return [_remote(o.at[pl.ds(0, o.shape[0]), c], o.at[pl.ds(0, o.shape[0]), 1 - c if arrivals else c], ss, rs, t, (x, y, 1 - c))
                for t, o in enumerate(out_refs)]

    return Exchange(fs, [jax.ShapeDtypeStruct(f.shape, f.dtype) for f in fs], {t: t for t in range(len(fs))}, len(fs), plan)


SUM_BLOCK = 512 * 1024


def _sum_pair(g, ra, c, name):
    n, _, h, w = g.shape
    tr = _tile(h, max(16, SUM_BLOCK // w), 16)

    def body(c_ref, g_ref, r_ref, o_ref):
        o_ref[...] = (g_ref[0] + r_ref[...]).astype(o_ref.dtype)

    return pl.pallas_call(
        body, name=name,
        grid_spec=pltpu.PrefetchScalarGridSpec(
            num_scalar_prefetch=1, grid=(n, h // tr),
            in_specs=[pl.BlockSpec((1, 1, tr, w), lambda s, i, cr: (s, cr[0], i, 0)), pl.BlockSpec((1, tr, w), lambda s, i, cr: (s, i, 0))],
            out_specs=pl.BlockSpec((1, tr, w), lambda s, i, cr: (s, i, 0))),
        out_shape=jax.ShapeDtypeStruct((n, h, w), BF16),
        compiler_params=_params(("parallel", "parallel")),
    )(c.reshape(1).astype(jnp.int32), g, ra)


def _sum_chips(ps, q, pos, name):
    nc, nl, h, w = q.shape
    tr = _tile(h, max(16, SUM_BLOCK // (w * nl)), 16)

    def body(x_ref, y_ref, c_ref, *refs):
        q_ref, o_ref = refs[nl], refs[nl + 1]
        for l in range(nl):
            acc = refs[l][0].astype(F32)
            for j in range(nc):
                acc = acc + q_ref[j, l].astype(F32)
            o_ref[l] = acc

    return pl.pallas_call(
        body, name=name,
        grid_spec=pltpu.PrefetchScalarGridSpec(
            num_scalar_prefetch=3, grid=(h // tr,),
            in_specs=[pl.BlockSpec((1, tr, w), lambda i, x, y, c: (2 * x[0] + y[0], i, 0))] * nl
            + [pl.BlockSpec((nc, nl, tr, w), lambda i, x, y, c: (0, 0, i, 0))],
            out_specs=pl.BlockSpec((nl, None, tr, w), lambda i, x, y, c: (0, c[0], i, 0))),
        out_shape=jax.ShapeDtypeStruct((nl, 2, h, w), F32),
        compiler_params=_params(("parallel",)),
    )(*pos, *ps, q)


def _adamw(g, w, m, v, name):
    def f(r0, gg, ww, mm, vv):
        m2 = ADAM_B1 * mm + (1.0 - ADAM_B1) * gg
        v2 = ADAM_B2 * vv + (1.0 - ADAM_B2) * jnp.square(gg)
        m_hat = m2 / (1.0 - ADAM_B1 ** ADAM_STEP)
        v_hat = v2 / (1.0 - ADAM_B2 ** ADAM_STEP)
        return gg, -ADAM_LR * (m_hat / (jnp.sqrt(v_hat) + ADAM_EPS) + ADAM_WD * ww), m2, v2

    return _rowwise(name, f, [g, w, m, v], [], [(g.shape[1], F32)] * 4, tr=_tile(g.shape[0], 512))


WEIGHTS = (
    ("meta_tokens", (N_META, D_MODEL), 1), ("mix_pre_g", (DEPTH, D_MODEL), None), ("mix_post_g", (DEPTH, D_MODEL), None),
    ("mlp_pre_g", (DEPTH, D_MODEL), None), ("mlp_post_g", (DEPTH, D_MODEL), None), ("w_up", (DEPTH, D_MODEL, D_FF), 2),
    ("w_down", (DEPTH, D_FF, D_MODEL), 1), ("w_in", (2, D_MODEL, 3248), 2), ("ssd_conv_w", (2, CONV_K, SSD_CONV_CH), 2),
    ("ssd_conv_b", (2, SSD_CONV_CH), None), ("ssd_dt_bias", (2, SSD_HEADS), None), ("ssd_a_log", (2, SSD_HEADS), None),
    ("ssd_d", (2, SSD_HEADS), None), ("ssd_norm_g", (2, SSD_D_INNER), None), ("mla_q_norm_g", (2, MLA_Q_RANK), None),
    ("mla_w_q_up", (2, MLA_Q_RANK, MLA_HEADS * (MLA_NOPE + MLA_ROPE)), 2), ("mla_kv_norm_g", (2, MLA_KV_RANK), None),
    ("mla_w_kv_up", (2, MLA_KV_RANK, MLA_HEADS * (MLA_NOPE + MLA_V)), 2), ("w_out_ab", (2, SSD_D_INNER + MLA_HEADS * MLA_V, D_MODEL), 1),
    ("rg_w_x", (2, D_MODEL, LRU_WIDTH), 2), ("rg_w_y", (2, D_MODEL, LRU_WIDTH), 2), ("rg_conv_w", (2, CONV_K, LRU_WIDTH), 2),
    ("rg_conv_b", (2, LRU_WIDTH), 1), ("rg_w_a", (2, LRU_BLOCKS, LRU_BLOCK, LRU_BLOCK), None), ("rg_b_a", (2, LRU_WIDTH), 1),
    ("rg_w_i", (2, LRU_BLOCKS, LRU_BLOCK, LRU_BLOCK), None), ("rg_b_i", (2, LRU_WIDTH), 1), ("rg_lambda", (2, LRU_WIDTH), 1),
    ("rg_w_out", (2, LRU_WIDTH, D_MODEL), 1),
)
BIG = {"w_up": "col", "w_down": "row", "w_in": "col", "mla_w_q_up": "col", "mla_w_kv_up": "col", "w_out_ab": "row",
       "rg_w_x": "col", "rg_w_y": "col", "rg_w_out": "row"}
DIRECT = ("w_up", "w_down")
FLAT_QUANTUM = 2 * 16 * LANES
TABLE = {name: (shape, d) for name, shape, d in WEIGHTS}
SMALL_SHARDED = tuple(name for name, _, d in WEIGHTS if d is not None and name not in BIG)
REPLICATED = tuple(name for name, _, d in WEIGHTS if d is None)


def _chips_to_full(a, kind):
    if kind == "col":
        return jnp.moveaxis(a, 0, 2).reshape(a.shape[1], a.shape[2], -1)
    return jnp.moveaxis(a, 0, 1).reshape(a.shape[1], -1, a.shape[3])


def _full_to_chips(g, kind):
    if kind == "col":
        return jnp.moveaxis(g.reshape(g.shape[0], N_CHIPS, -1), 1, 0)
    return g.reshape(N_CHIPS, -1, g.shape[1])


def _chips_to_full_1(pc, kind):
    return jnp.moveaxis(pc, 0, 1).reshape(pc.shape[1], -1) if kind == "col" else pc.reshape(-1, pc.shape[2])


def _shard_shape(shape, d):
    return shape[:d] + (shape[d] // N_CHIPS,) + shape[d + 1:]


def _shard_major(full, d):
    s = full.shape
    return jnp.moveaxis(full.reshape(s[:d] + (N_CHIPS, s[d] // N_CHIPS) + s[d + 1:]), d, 0).reshape(N_CHIPS, -1)


def _from_shard_major(a, shape, d):
    ss = _shard_shape(shape, d)
    return jnp.moveaxis(a.reshape((N_CHIPS,) + ss), 0, d).reshape(shape)


def _pad_cols(a, quantum):
    n = a.shape[-1]
    return jnp.pad(a, [(0, 0)] * (a.ndim - 1) + [(0, -n % quantum)])


def _big_pieces(g):
    def w_in(a):
        return jnp.concatenate([a[:, :IN_DT_END], a[:, PROJ_CQ:PROJ_KR], a[:, PROJ_KR + ROPE_LO:PROJ_KR + ROPE_HI]], axis=1)

    def q_up(a):
        return a.reshape(MLA_Q_RANK, MLA_HEADS, LANES)[:, :, :MLA_NOPE + MLA_ROPE].reshape(MLA_Q_RANK, -1)

    def out_ab(sa):
        s, a = sa
        return jnp.concatenate([s, a.reshape(MLA_HEADS, LANES, D_MODEL)[:, LANES - MLA_V:, :].reshape(-1, D_MODEL)], axis=0)

    ident = lambda a: a
    full = {"w_down": _each(g["w_down"], ident), "w_in": _each(g["w_in"], w_in), "mla_w_q_up": _each(g["mla_w_q_up"], q_up),
            "mla_w_kv_up": _each(g["mla_w_kv_up"], ident),
            "w_out_ab": _each([None if s is None or a is None else (s, a) for s, a in zip(g["w_out_ssd"], g["w_out_att"])], out_ab),
            "rg_w_x": _each(g["rg_w_x"], ident), "rg_w_y": _each(g["rg_w_y"], ident), "rg_w_out": _each(g["rg_w_out"], ident)}
    return {name: (list(g[name]) if name == "w_up" else _each(full[name], lambda a, k=BIG[name]: _full_to_chips(a, k))) for name in BIG}


def _small_grads(g, dh):
    out = {k: jnp.stack(g[k])[:, 0, :] for k in GAINS}
    for k in HEAD_VECS:
        out[k] = jnp.stack(g[k])[:, 0, :SSD_HEADS]
    for k in LRU_VECS:
        out[k] = jnp.stack(g[k]).reshape(-1, LRU_WIDTH)
    for k in ("ssd_conv_w", "rg_conv_w", "rg_w_a", "rg_w_i"):
        out[k] = jnp.stack(g[k])
    out["meta_tokens"] = dh[PAD:PAD + N_META]
    return out


def _natural_grads(g, dh):
    out = _small_grads(g, dh)
    for name, pcs in _big_pieces(g).items():
        out[name] = jnp.stack([_chips_to_full_1(pc, BIG[name]) for pc in pcs])
    return out


class StepExchanges:
    def __init__(self, w):
        self.w = w
        self.c = lax.axis_index("c")
        self.riding, self.ras = {}, {}
        small = _pad_cols(jnp.concatenate([w[n].reshape(-1) for n in SMALL_SHARDED]), FLAT_QUANTUM).reshape(1, 2, -1, LANES)
        self.srcs = [self._halves(w[n].astype(BF16)) for n in BIG] + [small]
        first = {n: (0, 1 if n in ("w_in", "mla_w_q_up", "mla_w_kv_up", "w_out_ab") else 0) for n in BIG}
        self.first = [first[n] for n in BIG] + [(0, 1)]
        self.rest = [(nl, TABLE[n][0][0] - nl) for n, (_, nl) in zip(BIG, self.first)] + [(0, 0)]
        bufs = _run_exchange("gather_first_ici", _gather_ici(self.srcs, None, self.first))
        self.bufs = _run_exchange("gather_first_d2d", _gather_d2d(self.srcs, bufs, self.first))

    @staticmethod
    def _halves(a):
        return a.reshape(a.shape[0], 2, a.shape[1] // 2, a.shape[2])

    def params(self, ranges):
        w = self.w
        full = {n: w[n] for n in REPLICATED}
        for name, buf, (l0, nl) in zip(BIG, self.bufs, ranges):
            a = buf.reshape(buf.shape[:2] + (-1, buf.shape[4]))
            have = range(l0, l0 + nl)
            if name in DIRECT:
                full[name] = [Gathered(a, BIG[name], l) if l in have else None for l in range(a.shape[1])]
            else:
                full[name] = [_chips_to_full(a[:, l:l + 1], BIG[name])[0] if l in have else None for l in range(a.shape[1])]
        got, off = self.bufs[-1].reshape(N_CHIPS, -1), 0
        for name in SMALL_SHARDED:
            shape, d = TABLE[name]
            n = int(np.prod(_shard_shape(shape, d)))
            full[name] = _from_shard_major(got[:, off:off + n], shape, d)
            off += n
        self.meta = full.pop("meta_tokens")
        return _layout_params(full)

    def forward_exchange(self):
        return _gather_ici(self.srcs, self.bufs, self.rest)

    def after_forward_exchange(self, arrived):
        self.bufs = _run_exchange("gather_rest_d2d", _gather_d2d(self.srcs, arrived, self.rest))
        return self.params([(0, TABLE[n][0][0]) for n in BIG])

    def _pair_sums(self, pieces, tag):
        keys = list(pieces)
        ras = _run_exchange("grads_pair_exchange_" + tag, _pair_exchange([pieces[k] for k in keys]))
        return {k: _sum_pair(pieces[k], ra, self.c, "grads_pair_sum") for k, ra in zip(keys, ras)}

    def _q_shapes(self):
        return [jax.ShapeDtypeStruct((N_CHIPS - 1, s.shape[0]) + s.shape[2:], BF16) for s in self.srcs[:-1]]

    def backward_exchange(self, grads, layer):
        big = _big_pieces(grads)
        pieces = {(g, l): pc.reshape(N_CHIPS, 2, pc.shape[1] // 2, pc.shape[2]) for g, name in enumerate(BIG)
                  for l, pc in enumerate(big[name]) if pc is not None and (g, l) not in self.riding}
        if layer > 0:
            self.riding = pieces
            return _pair_exchange(list(pieces.values()))
        self.ps = {k: _sum_pair(self.riding[k], ra, self.c, "grads_pair_sum") for k, ra in self.ras.items()}
        self.ps.update(self._pair_sums(pieces, "early"))
        self.early = list(self.ps)
        return _chip_exchange([self.ps[k] for k in self.early], self.early, [None] * len(BIG), self._q_shapes())

    def after_backward_exchange(self, arrived, layer):
        if layer > 0:
            self.ras = dict(zip(self.riding, arrived))
        else:
            self.qs = list(arrived)

    def finish(self, grads, dh):
        big, small = _big_pieces(grads), _small_grads(grads, dh)
        pieces = {(g, l): pc.reshape(N_CHIPS, 2, pc.shape[1] // 2, pc.shape[2])
                  for g, name in enumerate(BIG) for l, pc in enumerate(big[name]) if (g, l) not in self.ps}
        sharded = jnp.concatenate([_shard_major(small[n], TABLE[n][1]) for n in SMALL_SHARDED], axis=1)
        rep = _pad_cols(jnp.concatenate([small[n].reshape(-1) for n in REPLICATED]), N_CHIPS * FLAT_QUANTUM)
        n_sh, n_rep = sharded.shape[1], rep.shape[0] // N_CHIPS
        flat = _pad_cols(jnp.concatenate([sharded, rep.reshape(N_CHIPS, n_rep)], axis=1), FLAT_QUANTUM)
        pieces[(len(BIG), 0)] = flat.reshape(N_CHIPS, 2, -1, LANES)
        late = self._pair_sums(pieces, "late")
        self.ps.update(late)
        keys = list(late)
        small_q = jax.ShapeDtypeStruct((N_CHIPS - 1, 1) + late[(len(BIG), 0)].shape[1:], BF16)
        qs = _run_exchange("grads_chip_exchange_late",
                           _chip_exchange([late[k] for k in keys], keys, self.qs + [None], self._q_shapes() + [small_q]))
        pos = [lax.axis_index(a).reshape(1).astype(jnp.int32) for a in ("x", "y", "c")]
        sums = [_sum_chips([self.ps[(g, l)] for l in range(q.shape[1])], q, pos, "grads_chip_sum") for g, q in enumerate(qs)]
        outs = _run_exchange("grads_pair_share", _pair_share(sums))
        out = {name: o.reshape(o.shape[0], -1, o.shape[3]) for name, o in zip(BIG, outs)}
        f = outs[-1].reshape(-1)
        rep_all = _gather_chips([f[n_sh:n_sh + n_rep].reshape(1, 2, -1, LANES)], "grads_gather_replicated")[0].reshape(-1)
        off = 0
        for name in SMALL_SHARDED:
            ss = _shard_shape(*TABLE[name])
            n = int(np.prod(ss))
            out[name] = f[off:off + n].reshape(ss)
            off += n
        off = 0
        for name in REPLICATED:
            shape = TABLE[name][0]
            n = int(np.prod(shape))
            out[name] = rep_all[off:off + n].reshape(shape)
            off += n
        return out


def kernel(x, meta_tokens, mix_pre_g, mix_post_g, mlp_pre_g, mlp_post_g, w_up, w_down, w_in, ssd_conv_w, ssd_conv_b, ssd_dt_bias, ssd_a_log, ssd_d, ssd_norm_g, mla_q_norm_g, mla_w_q_up, mla_kv_norm_g, mla_w_kv_up, w_out_ab, rg_w_x, rg_w_y, rg_conv_w, rg_conv_b, rg_w_a, rg_b_a, rg_w_i, rg_b_i, rg_lambda, rg_w_out, loss_target, m_meta_tokens, m_mix_pre_g, m_mix_post_g, m_mlp_pre_g, m_mlp_post_g, m_w_up, m_w_down, m_w_in, m_ssd_conv_w, m_ssd_conv_b, m_ssd_dt_bias, m_ssd_a_log, m_ssd_d, m_ssd_norm_g, m_mla_q_norm_g, m_mla_w_q_up, m_mla_kv_norm_g, m_mla_w_kv_up, m_w_out_ab, m_rg_w_x, m_rg_w_y, m_rg_conv_w, m_rg_conv_b, m_rg_w_a, m_rg_b_a, m_rg_w_i, m_rg_b_i, m_rg_lambda, m_rg_w_out, v_meta_tokens, v_mix_pre_g, v_mix_post_g, v_mlp_pre_g, v_mlp_post_g, v_w_up, v_w_down, v_w_in, v_ssd_conv_w, v_ssd_conv_b, v_ssd_dt_bias, v_ssd_a_log, v_ssd_d, v_ssd_norm_g, v_mla_q_norm_g, v_mla_w_q_up, v_mla_kv_norm_g, v_mla_w_kv_up, v_w_out_ab, v_rg_w_x, v_rg_w_y, v_rg_conv_w, v_rg_conv_b, v_rg_w_a, v_rg_b_a, v_rg_w_i, v_rg_b_i, v_rg_lambda, v_rg_w_out):
    names = [n for n, _, _ in WEIGHTS]
    w = dict(zip(names, (meta_tokens, mix_pre_g, mix_post_g, mlp_pre_g, mlp_post_g, w_up, w_down, w_in, ssd_conv_w, ssd_conv_b, ssd_dt_bias, ssd_a_log, ssd_d, ssd_norm_g, mla_q_norm_g, mla_w_q_up, mla_kv_norm_g, mla_w_kv_up, w_out_ab, rg_w_x, rg_w_y, rg_conv_w, rg_conv_b, rg_w_a, rg_b_a, rg_w_i, rg_b_i, rg_lambda, rg_w_out)))
    m = dict(zip(names, (m_meta_tokens, m_mix_pre_g, m_mix_post_g, m_mlp_pre_g, m_mlp_post_g, m_w_up, m_w_down, m_w_in, m_ssd_conv_w, m_ssd_conv_b, m_ssd_dt_bias, m_ssd_a_log, m_ssd_d, m_ssd_norm_g, m_mla_q_norm_g, m_mla_w_q_up, m_mla_kv_norm_g, m_mla_w_kv_up, m_w_out_ab, m_rg_w_x, m_rg_w_y, m_rg_conv_w, m_rg_conv_b, m_rg_w_a, m_rg_b_a, m_rg_w_i, m_rg_b_i, m_rg_lambda, m_rg_w_out)))
    v = dict(zip(names, (v_meta_tokens, v_mix_pre_g, v_mix_post_g, v_mlp_pre_g, v_mlp_post_g, v_w_up, v_w_down, v_w_in, v_ssd_conv_w, v_ssd_conv_b, v_ssd_dt_bias, v_ssd_a_log, v_ssd_d, v_ssd_norm_g, v_mla_q_norm_g, v_mla_w_q_up, v_mla_kv_norm_g, v_mla_w_kv_up, v_w_out_ab, v_rg_w_x, v_rg_w_y, v_rg_conv_w, v_rg_conv_b, v_rg_w_a, v_rg_b_a, v_rg_w_i, v_rg_b_i, v_rg_lambda, v_rg_w_out)))
    ex = StepExchanges(w)
    p = ex.params(ex.first)
    sq, dh, grads = _device_step(x[0], ex.meta, loss_target[0], p, hooks=ex)
    loss = lax.psum(0.5 * sq[0, 0] / D_MODEL, ("x", "y", "c"))
    g = ex.finish(grads, dh)
    grad, delta, new_m, new_v = {}, {}, {}, {}
    for name in names:
        shape = g[name].shape
        two_d = (int(np.prod(shape[:-1])), shape[-1])
        res = _adamw(g[name].reshape(two_d), w[name].reshape(two_d), m[name].reshape(two_d), v[name].reshape(two_d), "adamw")
        grad[name], delta[name], new_m[name], new_v[name] = (r.reshape(shape) for r in res)
    grad_x = dh[PAD + N_META:][None]
    return (loss, grad_x, *[grad[n] for n in names], *[delta[n] for n in names], *[new_m[n] for n in names], *[new_v[n] for n in names])
```

```python
import functools

import jax
import jax.numpy as jnp
import numpy as np
from jax import lax
from jax.experimental import pallas as pl
from jax.experimental.pallas import tpu as pltpu

F32 = jnp.float32
BF16 = jnp.bfloat16

D_MODEL = 1024
DEPTH = 4
N_META = 16
CHUNK = 128
PAD = CHUNK - N_META
EPS = 1e-6
SSD_HEADS = 16
SSD_HEAD_DIM = 64
SSD_D_INNER = SSD_HEADS * SSD_HEAD_DIM
SSD_GROUPS = 2
SSD_STATE = 128
SSD_CONV_CH = SSD_D_INNER + 2 * SSD_GROUPS * SSD_STATE
MLA_HEADS = 16
MLA_NOPE = 64
MLA_ROPE = 32
MLA_V = 64
MLA_Q_RANK = 384
MLA_KV_RANK = 256
ROPE_BASE = 10000.0
LRU_WIDTH = 1280
LRU_BLOCKS = 10
LRU_BLOCK = 128
LRU_C = 8.0
D_FF = 4 * D_MODEL
ADAM_LR, ADAM_B1, ADAM_B2, ADAM_EPS, ADAM_WD, ADAM_STEP = 0.001, 0.9, 0.999, 1e-08, 0.01, 10

LANES = 128
VMEM_LIMIT = 56 * 1024 * 1024
HEAD_SLOT = 128
PROJ_Z, PROJ_XBC, PROJ_DT, PROJ_CQ, PROJ_CKV, PROJ_KR = 0, 1024, 2560, 2688, 3072, 3328
PROJ_W = 3456


def _tile(n, cap, mult=8):
    for t in range(min(n, cap), 0, -1):
        if n % t == 0 and t % mult == 0:
            return t
    return n


def _params(sem):
    return pltpu.CompilerParams(dimension_semantics=sem, vmem_limit_bytes=VMEM_LIMIT)


def _full_spec(shape, ngrid):
    nd = len(shape)
    if ngrid == 1:
        return pl.BlockSpec(shape, lambda i: (0,) * nd)
    if ngrid == 2:
        return pl.BlockSpec(shape, lambda i, j: (0,) * nd)
    return pl.BlockSpec(shape, lambda i, j, k: (0,) * nd)


_DIMS = {"nn": (((1,), (0,)), ((), ())), "nt": (((1,), (1,)), ((), ())), "tn": (((0,), (0,)), ((), ()))}


class Gathered:
    def __init__(self, arr, kind, layer):
        self.arr, self.kind, self.layer = arr, kind, layer
        _, _, r, c = arr.shape
        self.shape = (r, N_CHIPS * c) if kind == "col" else (N_CHIPS * r, c)


N_CHIPS = 4


def _mm(a, b, mode, name, out_dtype=F32, add=None, out_chip_major=False, extra=(), post=None, out_dtypes=None):
    if mode == "nn":
        (m, kc), (_, n) = a.shape, b.shape
    elif mode == "nt":
        (m, kc), (n, _) = a.shape, b.shape
    else:
        (kc, m), (_, n) = a.shape, b.shape
    tm = _tile(m, 1024, LANES) if mode == "tn" else _tile(m, 1056, 16)
    tn = _tile(n // N_CHIPS if out_chip_major else n, 1280, LANES)
    tk = _tile(kc, 1024 if mode != "tn" else 1408, LANES)
    nk = kc // tk
    if mode == "tn":
        a_spec = pl.BlockSpec((tk, tm), lambda i, j, k: (k, i))
    else:
        a_spec = pl.BlockSpec((tm, tk), lambda i, j, k: (i, k))
    b_arr = b
    if isinstance(b, Gathered):
        b_arr, layer = b.arr, b.layer
        sr, sc = b.arr.shape[2:]
        br, bc = (tk, tn) if mode == "nn" else (tn, tk)
        assert mode in ("nn", "nt") and sr % br == 0 and sc % bc == 0

        def b_map(i, j, k):
            r, c = (k, j) if mode == "nn" else (j, k)
            if b.kind == "col":
                return ((c * bc) // sc, layer, r, ((c * bc) % sc) // bc)
            return ((r * br) // sr, layer, ((r * br) % sr) // br, c)

        b_spec = pl.BlockSpec((None, None, br, bc), b_map)
    elif mode == "nt":
        b_spec = pl.BlockSpec((tn, tk), lambda i, j, k: (j, k))
    else:
        b_spec = pl.BlockSpec((tk, tn), lambda i, j, k: (k, j))
    dims = _DIMS[mode]
    if out_chip_major:
        ns = n // N_CHIPS
        o_spec = pl.BlockSpec((None, tm, tn), lambda i, j, k: ((j * tn) // ns, i, ((j * tn) % ns) // tn))
        o_shape = jax.ShapeDtypeStruct((N_CHIPS, m, ns), out_dtype)
    else:
        o_spec = pl.BlockSpec((tm, tn), lambda i, j, k: (i, j))
        o_shape = jax.ShapeDtypeStruct((m, n), out_dtype)
    extra = list(extra) + ([add] if add is not None else [])
    if add is not None:
        post = lambda v, x: (v + x,)
    nx = len(extra)
    out_dtypes = out_dtypes or [out_dtype]
    no = len(out_dtypes)

    def body(a_ref, b_ref, *rest):
        o_refs, acc = rest[nx:nx + no], rest[nx + no:]
        p = lax.dot_general(a_ref[...].astype(BF16), b_ref[...].astype(BF16), dims, preferred_element_type=F32)

        def emit(v):
            res = post(v, *[r[...] for r in rest[:nx]]) if post else (v,)
            for o_ref, r in zip(o_refs, res):
                o_ref[...] = r.astype(o_ref.dtype)

        if nk == 1:
            emit(p)
        else:
            k = pl.program_id(2)

            @pl.when(k == 0)
            def _():
                acc[0][...] = p

            @pl.when(k > 0)
            def _():
                acc[0][...] += p

            @pl.when(k == nk - 1)
            def _():
                emit(acc[0][...])

    res = pl.pallas_call(
        body, name=name, grid=(m // tm, n // tn, nk),
        in_specs=[a_spec, b_spec] + [o_spec] * nx, out_specs=[o_spec] * no,
        out_shape=[jax.ShapeDtypeStruct(o_shape.shape, dt) for dt in out_dtypes],
        scratch_shapes=[pltpu.VMEM((tm, tn), F32)] if nk > 1 else [],
        compiler_params=_params(("parallel", "parallel", "arbitrary")),
    )(a, b_arr, *extra)
    return res[0] if no == 1 else res


def _rowarg(r):
    return r if isinstance(r, tuple) else (r, r.shape[1], 0)


def _rowspec(r, tr, ncol):
    _, w, cb = r
    if ncol > 1:
        return pl.BlockSpec((tr, w // ncol), lambda j, i: (i, j))
    return pl.BlockSpec((tr, w), lambda j, i: (i, cb))


def _rowwise(name, f, rows, params, outs, tr=None, ncol=1):
    rows = [_rowarg(r) for r in rows]
    t = rows[0][0].shape[0]
    tr = tr or _tile(t, 528)
    nr, npm = len(rows), len(params)

    def body(*refs):
        vals = [r[...] for r in refs[:nr]] + [(p[0] if ncol > 1 else p[...]) for p in refs[nr:nr + npm]]
        res = f(pl.program_id(1) * tr, *vals)
        for o_ref, v in zip(refs[nr + npm:], res):
            o_ref[...] = v.astype(o_ref.dtype)

    def pspec(p):
        if ncol > 1:
            return pl.BlockSpec((1,) + p.shape[1:], lambda j, i, n=p.ndim: (j,) + (0,) * (n - 1))
        return _full_spec(p.shape, 2)

    return pl.pallas_call(
        body, name=name, grid=(ncol, t // tr),
        in_specs=[_rowspec(r, tr, ncol) for r in rows] + [pspec(p) for p in params],
        out_specs=[pl.BlockSpec((tr, w // ncol), lambda j, i: (i, j)) for w, _ in outs],
        out_shape=[jax.ShapeDtypeStruct((t, w), dt) for w, dt in outs],
        compiler_params=_params(("parallel", "parallel")),
    )(*[r[0] for r in rows], *params)


def _rowwise_vjp(name, f, rows, params, cts, tr=None, ncol=1, row_dtypes=None):
    rows = [_rowarg(r) for r in rows]
    cts = [_rowarg(c) for c in cts]
    t = rows[0][0].shape[0]
    tr = tr or _tile(t, 528)
    nr, npm, nc = len(rows), len(params), len(cts)
    row_dtypes = row_dtypes or [F32] * nr

    def body(*refs):
        i = pl.program_id(1)
        vals = [r[...] for r in refs[:nr]] + [(p[0] if ncol > 1 else p[...]) for p in refs[nr:nr + npm]]
        ct = tuple(c[...].astype(F32) for c in refs[nr + npm:nr + npm + nc])
        _, vjp = jax.vjp(lambda *a: tuple(f(i * tr, *a)), *vals)
        g = vjp(ct)
        outs = refs[nr + npm + nc:]
        for o_ref, v in zip(outs[:nr], g[:nr]):
            o_ref[...] = v.astype(o_ref.dtype)
        pg = [(v[None] if ncol > 1 else v) for v in g[nr:]]

        @pl.when(i == 0)
        def _():
            for o_ref, v in zip(outs[nr:], pg):
                o_ref[...] = v

        @pl.when(i > 0)
        def _():
            for o_ref, v in zip(outs[nr:], pg):
                o_ref[...] += v

    def pspec(p):
        if ncol > 1:
            return pl.BlockSpec((1,) + p.shape[1:], lambda j, i, n=p.ndim: (j,) + (0,) * (n - 1))
        return _full_spec(p.shape, 2)

    res = pl.pallas_call(
        body, name=name, grid=(ncol, t // tr),
        in_specs=[_rowspec(r, tr, ncol) for r in rows] + [pspec(p) for p in params] + [_rowspec(c, tr, ncol) for c in cts],
        out_specs=[pl.BlockSpec((tr, w // ncol), lambda j, i: (i, j)) for _, w, _ in rows] + [pspec(p) for p in params],
        out_shape=[jax.ShapeDtypeStruct((t, w), dt) for (_, w, _), dt in zip(rows, row_dtypes)]
        + [jax.ShapeDtypeStruct(p.shape, F32) for p in params],
        compiler_params=_params(("parallel", "arbitrary")),
    )(*[r[0] for r in rows], *params, *[c[0] for c in cts])
    return res[:nr], res[nr:]


def _valid(row0, tr):
    return (row0 + lax.broadcasted_iota(jnp.int32, (tr, 1), 0)) >= PAD


def _rms(x, g):
    return x * lax.rsqrt(jnp.mean(x * x, axis=-1, keepdims=True) + EPS) * g


def _softplus(x):
    return jnp.where(x < -15.0, jnp.exp(x), jnp.maximum(x, 0.0) + jnp.log(1.0 + jnp.exp(-jnp.abs(x))))


def _neg_expm1(z):
    return jnp.where(z > -0.01, -z * (1.0 + z * (0.5 + z * (1.0 / 6.0))), 1.0 - jnp.exp(z))


def _prenorm(h, g, name):
    return _rowwise(name, lambda r0, x, gg: (_rms(x, gg),), [h], [g], [(_rowarg(h)[1], BF16)])[0]


def _add_postnorm(h, ms, g, name):
    def f(r0, x, *rest):
        return (x + _rms(functools.reduce(jnp.add, rest[:-1]), rest[-1]),)

    return _rowwise(name, f, [h] + list(ms), [g], [(h.shape[1], F32)])[0]


def _postnorm_bwd(m, g, dh, name):
    (dm,), (dg,) = _rowwise_vjp(name, lambda r0, mm, gg: (_rms(mm, gg),), [m], [g], [dh])
    return dm, dg


def _prenorm_bwd_add(h, g, dhns, dh, name):
    t, w = h.shape
    tr = _tile(t, 528)
    nd = len(dhns)

    def body(h_ref, g_ref, *refs):
        dh_ref, o_ref, dg_ref = refs[nd:]
        i = pl.program_id(0)
        _, vjp = jax.vjp(_rms, h_ref[...], g_ref[...])
        dhn = refs[0][...].astype(F32)
        for r in refs[1:nd]:
            dhn = dhn + r[...].astype(F32)
        dx, dg = vjp(dhn)
        o_ref[...] = dh_ref[...] + dx

        @pl.when(i == 0)
        def _():
            dg_ref[...] = dg

        @pl.when(i > 0)
        def _():
            dg_ref[...] += dg

    row = pl.BlockSpec((tr, w), lambda i: (i, 0))
    return pl.pallas_call(
        body, name=name, grid=(t // tr,), in_specs=[row, _full_spec(g.shape, 1)] + [row] * (nd + 1),
        out_specs=[row, _full_spec(g.shape, 1)],
        out_shape=[jax.ShapeDtypeStruct((t, w), F32), jax.ShapeDtypeStruct(g.shape, F32)],
        compiler_params=_params(("arbitrary",)),
    )(h, g, *dhns, dh)


def _loss_and_grad(h, target, name):
    t, w = h.shape
    nb = t // CHUNK

    def body(h_ref, t_ref, s_ref, dh_ref):
        i = pl.program_id(0)

        @pl.when(i == 0)
        def _():
            s_ref[...] = jnp.zeros_like(s_ref)
            dh_ref[...] = jnp.zeros_like(dh_ref)

        @pl.when(i > 0)
        def _():
            err = h_ref[...] - t_ref[...]
            s_ref[...] += jnp.sum(err * err)
            dh_ref[...] = err * (1.0 / w)

    return pl.pallas_call(
        body, name=name, grid=(nb,),
        in_specs=[pl.BlockSpec((CHUNK, w), lambda i: (i, 0)), pl.BlockSpec((CHUNK, w), lambda i: (jnp.maximum(i - 1, 0), 0))],
        out_specs=[_full_spec((1, LANES), 1), pl.BlockSpec((CHUNK, w), lambda i: (i, 0))],
        out_shape=[jax.ShapeDtypeStruct((1, LANES), F32), jax.ShapeDtypeStruct((t, w), F32)],
        compiler_params=_params(("arbitrary",)),
    )(h, target)


def _mlp_fwd(h, p, l):
    hn = _prenorm(h, p["mlp_pre_g"][l], "mlp_prenorm")
    a, u = _mm(hn, p["w_up"][l], "nn", "mlp_up", post=lambda v: (v, jnp.square(jnp.maximum(v, 0.0))), out_dtypes=[F32, BF16])
    d = _mm(u, p["w_down"][l], "nn", "mlp_down")
    h2 = _add_postnorm(h, [d], p["mlp_post_g"][l], "mlp_postnorm")
    return h2, (h, hn, a, u, d)


def _mlp_bwd(dh, saved, p, l, grads):
    h, hn, a, u, d = saved
    dd, grads["mlp_post_g"][l] = _postnorm_bwd(d, p["mlp_post_g"][l], dh, "mlp_postnorm_bwd")
    grads["w_down"][l] = _mm(u, dd, "tn", "mlp_down_dw")
    da = _mm(dd, p["w_down"][l], "nt", "mlp_down_dx", extra=[a], post=lambda v, x: (2.0 * jnp.maximum(x, 0.0) * v,),
             out_dtypes=[BF16])
    grads["w_up"][l] = _mm(hn, da, "tn", "mlp_up_dw", out_chip_major=True)
    dhn = _mm(da, p["w_up"][l], "nt", "mlp_up_dx")
    dh, grads["mlp_pre_g"][l] = _prenorm_bwd_add(h, p["mlp_pre_g"][l], [dhn], dh, "mlp_prenorm_bwd")
    return dh


def _dot(a, b, mode):
    return lax.dot_general(a.astype(BF16), b.astype(BF16), _DIMS[mode], preferred_element_type=F32)


@jax.custom_vjp
def _bnn(a, b):
    return _dot(a, b, "nn")


_bnn.defvjp(lambda a, b: (_dot(a, b, "nn"), (a, b)), lambda r, ct: (_dot(ct, r[1], "nt"), _dot(r[0], ct, "tn")))


@jax.custom_vjp
def _bnt(a, b):
    return _dot(a, b, "nt")


_bnt.defvjp(lambda a, b: (_dot(a, b, "nt"), (a, b)), lambda r, ct: (_dot(ct, r[1], "nn"), _dot(ct, r[0], "tn")))


@jax.custom_vjp
def _btn(a, b):
    return _dot(a, b, "tn")


_btn.defvjp(lambda a, b: (_dot(a, b, "tn"), (a, b)), lambda r, ct: (_dot(r[1], ct, "nt"), _dot(r[0], ct, "nn")))


CONV_K = 4
HALO = 8


def _conv_fwd(x, w, b, name, cw, c0=0):
    t, c = x.shape[0], w.shape[1]
    tr = _tile(t, 528)
    hb = tr // HALO

    def body(x_ref, halo_ref, w_ref, b_ref, o_ref, ext):
        i = pl.program_id(1)
        ext[pl.ds(0, HALO), :] = jnp.where(i > 0, halo_ref[...], 0.0)
        ext[pl.ds(HALO, tr), :] = x_ref[...]
        acc = jnp.broadcast_to(b_ref[...], (tr, cw))
        for k in range(CONV_K):
            acc = acc + w_ref[pl.ds(k, 1), :] * ext[pl.ds(HALO - (CONV_K - 1) + k, tr), :]
        o_ref[...] = acc

    return pl.pallas_call(
        body, name=name, grid=(c // cw, t // tr),
        in_specs=[pl.BlockSpec((tr, cw), lambda j, i: (i, c0 + j)),
                  pl.BlockSpec((HALO, cw), lambda j, i: (jnp.maximum(i * hb - 1, 0), c0 + j)),
                  pl.BlockSpec((CONV_K, cw), lambda j, i: (0, j)), pl.BlockSpec((1, cw), lambda j, i: (0, j))],
        out_specs=pl.BlockSpec((tr, cw), lambda j, i: (i, j)),
        out_shape=jax.ShapeDtypeStruct((t, c), F32),
        scratch_shapes=[pltpu.VMEM((tr + HALO, cw), F32)],
        compiler_params=_params(("parallel", "parallel")),
    )(x, x, w, b)


def _conv_bwd(x, w, dy, name, cw, c0=0):
    t, c = x.shape[0], w.shape[1]
    tr = _tile(t, 528)
    hb = tr // HALO
    nb = t // tr

    def body(x_ref, xh_ref, w_ref, dy_ref, dyh_ref, dx_ref, dw_ref, db_ref, xe, de):
        c = cw
        i = pl.program_id(1)
        xe[pl.ds(0, HALO), :] = jnp.where(i > 0, xh_ref[...], 0.0)
        xe[pl.ds(HALO, tr), :] = x_ref[...]
        de[pl.ds(0, tr), :] = dy_ref[...]
        de[pl.ds(tr, HALO), :] = jnp.where(i < nb - 1, dyh_ref[...], 0.0)
        dy = dy_ref[...]
        acc = jnp.zeros((tr, c), F32)
        dw = jnp.zeros((CONV_K, c), F32)
        rows = lax.broadcasted_iota(jnp.int32, (CONV_K, 1), 0)
        for k in range(CONV_K):
            acc = acc + w_ref[pl.ds(k, 1), :] * de[pl.ds(CONV_K - 1 - k, tr), :]
            dwk = jnp.sum(dy * xe[pl.ds(HALO - (CONV_K - 1) + k, tr), :], axis=0, keepdims=True)
            dw = dw + jnp.where(rows == k, dwk, 0.0)
        dx_ref[...] = jnp.where(_valid(i * tr, tr), acc, 0.0)
        db = jnp.sum(dy, axis=0, keepdims=True)

        @pl.when(i == 0)
        def _():
            dw_ref[...] = dw
            db_ref[...] = db

        @pl.when(i > 0)
        def _():
            dw_ref[...] += dw
            db_ref[...] += db

    row = pl.BlockSpec((tr, cw), lambda j, i: (i, j))
    return pl.pallas_call(
        body, name=name, grid=(c // cw, nb),
        in_specs=[pl.BlockSpec((tr, cw), lambda j, i: (i, c0 + j)),
                  pl.BlockSpec((HALO, cw), lambda j, i: (jnp.maximum(i * hb - 1, 0), c0 + j)),
                  pl.BlockSpec((CONV_K, cw), lambda j, i: (0, j)),
                  row, pl.BlockSpec((HALO, cw), lambda j, i: (jnp.minimum((i + 1) * hb, t // HALO - 1), j))],
        out_specs=[row, pl.BlockSpec((CONV_K, cw), lambda j, i: (0, j)), pl.BlockSpec((1, cw), lambda j, i: (0, j))],
        out_shape=[jax.ShapeDtypeStruct((t, c), F32), jax.ShapeDtypeStruct((CONV_K, c), F32), jax.ShapeDtypeStruct((1, c), F32)],
        scratch_shapes=[pltpu.VMEM((tr + HALO, cw), F32), pltpu.VMEM((tr + HALO, cw), F32)],
        compiler_params=_params(("parallel", "arbitrary")),
    )(x, x, w, dy, dy)


SUB = 8


def _lru_scan(a, u, name):
    t, c = a.shape
    tr = _tile(t, 528)

    def body(a_ref, u_ref, o_ref, carry):
        @pl.when(pl.program_id(0) == 0)
        def _():
            carry[...] = jnp.zeros_like(carry)

        rows = lax.broadcasted_iota(jnp.int32, (SUB, 1), 0)

        def step(k, cin):
            r = pl.multiple_of(k * SUB, SUB)
            av, uv = a_ref[pl.ds(r, SUB), :], u_ref[pl.ds(r, SUB), :]
            for d in (1, 2, 4):
                m = rows >= d
                uv = uv + av * jnp.where(m, pltpu.roll(uv, d, 0), 0.0)
                av = av * jnp.where(m, pltpu.roll(av, d, 0), 1.0)
            hv = uv + av * cin
            o_ref[pl.ds(r, SUB), :] = hv
            return jnp.broadcast_to(hv[SUB - 1:SUB, :], (SUB, c))

        carry[...] = lax.fori_loop(0, tr // SUB, step, carry[...])

    row = pl.BlockSpec((tr, c), lambda i: (i, 0))
    return pl.pallas_call(
        body, name=name, grid=(t // tr,), in_specs=[row, row], out_specs=row,
        out_shape=jax.ShapeDtypeStruct((t, c), F32), scratch_shapes=[pltpu.VMEM((SUB, c), F32)],
        compiler_params=_params(("arbitrary",)),
    )(a, u)


def _lru_scan_bwd(a, hs, dy, name):
    t, c = a.shape
    tr = _tile(t, 528)
    nb, nt = t // tr, tr // SUB

    def body(a_ref, h_ref, hh_ref, dy_ref, du_ref, da_ref, gcar, acar):
        i = pl.program_id(0)

        @pl.when(i == 0)
        def _():
            gcar[...] = jnp.zeros_like(gcar)
            acar[...] = jnp.zeros_like(acar)

        rows = lax.broadcasted_iota(jnp.int32, (SUB, 1), 0)
        hhalo = jnp.where(i < nb - 1, hh_ref[...], 0.0)

        def step(kk, car):
            gin, a_next_first = car
            k = nt - 1 - kk
            r = pl.multiple_of(k * SUB, SUB)
            av, hv, dv = a_ref[pl.ds(r, SUB), :], h_ref[pl.ds(r, SUB), :], dy_ref[pl.ds(r, SUB), :]
            rp = pl.multiple_of(jnp.maximum(k - 1, 0) * SUB, SUB)
            hp = jnp.where(k > 0, h_ref[pl.ds(rp, SUB), :], hhalo)
            cv = jnp.where(rows < SUB - 1, pltpu.roll(av, SUB - 1, 0), a_next_first)
            gv = dv
            for d in (1, 2, 4):
                m = rows < SUB - d
                gv = gv + cv * jnp.where(m, pltpu.roll(gv, SUB - d, 0), 0.0)
                cv = cv * jnp.where(m, pltpu.roll(cv, SUB - d, 0), 1.0)
            gv = gv + cv * gin
            hprev = jnp.where(rows >= 1, pltpu.roll(hv, 1, 0), jnp.broadcast_to(hp[SUB - 1:SUB, :], (SUB, c)))
            du_ref[pl.ds(r, SUB), :] = gv
            da_ref[pl.ds(r, SUB), :] = gv * hprev
            return jnp.broadcast_to(gv[0:1, :], (SUB, c)), jnp.broadcast_to(av[0:1, :], (SUB, c))

        g, af = lax.fori_loop(0, nt, step, (gcar[...], acar[...]))
        gcar[...] = g
        acar[...] = af

    hb = tr // SUB
    row = pl.BlockSpec((tr, c), lambda i: (nb - 1 - i, 0))
    halo = pl.BlockSpec((SUB, c), lambda i: (jnp.maximum((nb - 1 - i) * hb - 1, 0), 0))
    return pl.pallas_call(
        body, name=name, grid=(nb,), in_specs=[row, row, halo, row], out_specs=[row, row],
        out_shape=[jax.ShapeDtypeStruct((t, c), F32)] * 2,
        scratch_shapes=[pltpu.VMEM((SUB, c), F32), pltpu.VMEM((SUB, c), F32)],
        compiler_params=_params(("arbitrary",)),
    )(a, hs, hs, dy)


def _lru_gates(row0, xr, wa, ba, wi, bi, lam):
    r = jax.nn.sigmoid(_bnn(xr, wa) + ba)
    i = jax.nn.sigmoid(_bnn(xr, wi) + bi)
    log_a = -LRU_C * r * _softplus(-lam)
    u = jnp.sqrt(_neg_expm1(2.0 * log_a)) * (i * xr)
    return jnp.exp(log_a), jnp.where(_valid(row0, xr.shape[0]), u, 0.0)


def _lru_gate_out(row0, hs, yw):
    return (hs * jax.nn.gelu(yw),)


def _rglru_fwd(h, p, l, o):
    hn = _prenorm(h, p["mix_pre_g"][l], "rg_prenorm")
    xw = _mm(hn, p["rg_w_x"][o], "nn", "rg_in_x")
    yw = _mm(hn, p["rg_w_y"][o], "nn", "rg_in_y")
    xr = _conv_fwd(xw, p["rg_conv_w"][o], p["rg_conv_b"][o], "rg_conv", cw=LRU_WIDTH // 2)
    gp = [p["rg_w_a"][o], p["rg_b_a"][o], p["rg_w_i"][o], p["rg_b_i"][o], p["rg_lambda"][o]]
    a, u = _rowwise("rg_gates", _lru_gates, [xr], gp, [(LRU_WIDTH, F32)] * 2, ncol=LRU_BLOCKS, tr=_tile(h.shape[0], 1056))
    hs = _lru_scan(a, u, "rg_scan")
    hg = _rowwise("rg_gate_out", _lru_gate_out, [hs, yw], [], [(LRU_WIDTH, BF16)])[0]
    m = _mm(hg, p["rg_w_out"][o], "nn", "rg_out")
    h2 = _add_postnorm(h, [m], p["mix_post_g"][l], "rg_postnorm")
    return h2, (h, hn, xw, yw, xr, a, hs, hg, m)


def _rglru_bwd(dh, saved, p, l, o, grads):
    h, hn, xw, yw, xr, a, hs, hg, m = saved
    dm, grads["mix_post_g"][l] = _postnorm_bwd(m, p["mix_post_g"][l], dh, "rg_postnorm_bwd")
    grads["rg_w_out"][o] = _mm(hg, dm, "tn", "rg_out_dw")
    dhg = _mm(dm, p["rg_w_out"][o], "nt", "rg_out_dx")
    (dhs, dyw), _ = _rowwise_vjp("rg_gate_out_bwd", _lru_gate_out, [hs, yw], [], [dhg])
    du, da = _lru_scan_bwd(a, hs, dhs, "rg_scan_bwd")
    gp = [p["rg_w_a"][o], p["rg_b_a"][o], p["rg_w_i"][o], p["rg_b_i"][o], p["rg_lambda"][o]]
    (dxr,), gg = _rowwise_vjp("rg_gates_bwd", _lru_gates, [xr], gp, [da, du], ncol=LRU_BLOCKS, tr=_tile(h.shape[0], 1056))
    grads["rg_w_a"][o], grads["rg_b_a"][o], grads["rg_w_i"][o], grads["rg_b_i"][o], grads["rg_lambda"][o] = gg
    dxw, grads["rg_conv_w"][o], grads["rg_conv_b"][o] = _conv_bwd(xw, p["rg_conv_w"][o], dxr, "rg_conv_bwd", cw=LRU_WIDTH // 2)
    grads["rg_w_x"][o] = _mm(hn, dxw, "tn", "rg_in_x_dw")
    grads["rg_w_y"][o] = _mm(hn, dyw, "tn", "rg_in_y_dw")
    dhx = _mm(dxw, p["rg_w_x"][o], "nt", "rg_in_x_dx")
    dhy = _mm(dyw, p["rg_w_y"][o], "nt", "rg_in_y_dx")
    dh, grads["mix_pre_g"][l] = _prenorm_bwd_add(h, p["mix_pre_g"][l], [dhx, dhy], dh, "rg_prenorm_bwd")
    return dh


SSD_GW = SSD_D_INNER // SSD_GROUPS
SSD_GH = SSD_HEADS // SSD_GROUPS
XACT_B = SSD_D_INNER // SSD_STATE
XACT_C = XACT_B + SSD_GROUPS


def _hp(a, b, dims=_DIMS["nn"]):
    return lax.dot_general(a, b, dims, precision=lax.Precision.HIGHEST, preferred_element_type=F32)


def _split_dot(a, e, mode, parts):
    eb = e.astype(BF16)
    out, rest = None, a
    for _ in range(parts):
        term = rest.astype(BF16)
        rest = rest - term.astype(F32)
        if mode in ("nn", "nt"):
            prod = lax.dot_general(term, eb, _DIMS[mode], preferred_element_type=F32)
        else:
            prod = lax.dot_general(eb, term, _DIMS["nn" if mode == "left" else "tn"], preferred_element_type=F32)
        out = prod if out is None else out + prod
    return out


@jax.custom_vjp
def _select_nn(a, e):
    return _split_dot(a, e, "nn", 3)


_select_nn.defvjp(lambda a, e: (_split_dot(a, e, "nn", 3), e), lambda e, ct: (_split_dot(ct, e, "nt", 2), jnp.zeros_like(e)))


@jax.custom_vjp
def _select_left(e, a):
    return _split_dot(a, e, "left", 3)


_select_left.defvjp(lambda e, a: (_split_dot(a, e, "left", 3), e),
                    lambda e, ct: (jnp.zeros_like(e), _split_dot(ct, e, "left_t", 2)))


def _ssd_chunk(xs, bm, cm, dt, da, ht, g):
    l = CHUNK
    ri = lax.broadcasted_iota(jnp.int32, (l, l), 0)
    ci = lax.broadcasted_iota(jnp.int32, (l, l), 1)
    causal = ri >= ci
    tri = causal.astype(F32)
    hr = lax.broadcasted_iota(jnp.int32, (LANES, SSD_GW), 0)
    hc = lax.broadcasted_iota(jnp.int32, (LANES, SSD_GW), 1)
    expand = (hr == g * SSD_GH + hc // SSD_HEAD_DIM).astype(F32)
    acs = _select_left(tri, da)
    acs_t = acs.T
    acs_e = _select_nn(acs, expand)
    x = xs * _select_nn(dt, expand)
    gmat = _bnt(cm, bm)
    lane = lax.broadcasted_iota(jnp.int32, (1, LANES), 1)
    sub = lax.broadcasted_iota(jnp.int32, (LANES, 1), 0)
    colhead = lax.broadcasted_iota(jnp.int32, (1, SSD_GW), 1) // SSD_HEAD_DIM
    y = _bnn(cm, ht) * jnp.exp(acs_e)
    for k in range(SSD_GH):
        hh = g * SSD_GH + k
        col = jnp.sum(jnp.where(lane == hh, acs, 0.0), axis=1, keepdims=True)
        row = jnp.sum(jnp.where(sub == hh, acs_t, 0.0), axis=0, keepdims=True)
        decay = jnp.exp(jnp.where(causal, col - row, -1e30))
        y = y + _bnn(gmat * decay, jnp.where(colhead == k, x, 0.0))
    last = lax.broadcasted_iota(jnp.int32, (l, 1), 0) == l - 1
    a_last = jnp.sum(jnp.where(last, acs_e, 0.0), axis=0, keepdims=True)
    st = _btn(bm, x * jnp.exp(a_last - acs_e))
    return y, ht * jnp.exp(a_last) + st


def _ssd_specs(nc, rev):
    def cc(c):
        return nc - 1 - c if rev else c

    return [pl.BlockSpec((CHUNK, SSD_GW), lambda c, g: (cc(c), g)),
            pl.BlockSpec((CHUNK, SSD_STATE), lambda c, g: (cc(c), XACT_B + g)),
            pl.BlockSpec((CHUNK, SSD_STATE), lambda c, g: (cc(c), XACT_C + g)),
            pl.BlockSpec((CHUNK, LANES), lambda c, g: (cc(c), 0)),
            pl.BlockSpec((CHUNK, LANES), lambda c, g: (cc(c), 0))]


def _ssd_scan(xact, dt, da, name):
    t = xact.shape[0]
    nc = t // CHUNK

    def body(xs_ref, b_ref, c_ref, dt_ref, da_ref, y_ref, hs_ref, state):
        c, g = pl.program_id(0), pl.program_id(1)

        @pl.when(c == 0)
        def _():
            state[g] = jnp.zeros((SSD_STATE, SSD_GW), F32)

        ht = state[g]
        hs_ref[0] = ht
        y, ht2 = _ssd_chunk(xs_ref[...], b_ref[...], c_ref[...], dt_ref[...], da_ref[...], ht, g)
        y_ref[...] = y
        state[g] = ht2

    return pl.pallas_call(
        body, name=name, grid=(nc, SSD_GROUPS), in_specs=_ssd_specs(nc, False),
        out_specs=[pl.BlockSpec((CHUNK, SSD_GW), lambda c, g: (c, g)),
                   pl.BlockSpec((1, SSD_STATE, SSD_GW), lambda c, g: (c * SSD_GROUPS + g, 0, 0))],
        out_shape=[jax.ShapeDtypeStruct((t, SSD_D_INNER), F32), jax.ShapeDtypeStruct((nc * SSD_GROUPS, SSD_STATE, SSD_GW), F32)],
        scratch_shapes=[pltpu.VMEM((SSD_GROUPS, SSD_STATE, SSD_GW), F32)],
        compiler_params=_params(("arbitrary", "arbitrary")),
    )(xact, xact, xact, dt, da)


def _ssd_scan_bwd(xact, dt, da, hsave, dy, dxskip, name):
    t = xact.shape[0]
    nc = t // CHUNK

    def body(xs_ref, b_ref, c_ref, dt_ref, da_ref, hs_ref, dy_ref, sk_ref, dxs_ref, db_ref, dc_ref, ddt_ref, dda_ref, dstate):
        c, g = pl.program_id(0), pl.program_id(1)

        @pl.when(c == 0)
        def _():
            dstate[g] = jnp.zeros((SSD_STATE, SSD_GW), F32)

        _, vjp = jax.vjp(lambda *a: _ssd_chunk(*a, g), xs_ref[...], b_ref[...], c_ref[...], dt_ref[...], da_ref[...], hs_ref[0])
        dxs, dbm, dcm, ddt, dda, dht = vjp((dy_ref[...], dstate[g]))
        dxs_ref[...] = dxs + sk_ref[...]
        db_ref[...] = dbm
        dc_ref[...] = dcm
        dstate[g] = dht

        @pl.when(g == 0)
        def _():
            ddt_ref[...] = ddt
            dda_ref[...] = dda

        @pl.when(g > 0)
        def _():
            ddt_ref[...] += ddt
            dda_ref[...] += dda

    grp = pl.BlockSpec((CHUNK, SSD_GW), lambda c, g: (nc - 1 - c, g))
    st = pl.BlockSpec((CHUNK, SSD_STATE), lambda c, g: (nc - 1 - c, g))
    hd = pl.BlockSpec((CHUNK, LANES), lambda c, g: (nc - 1 - c, 0))
    return pl.pallas_call(
        body, name=name, grid=(nc, SSD_GROUPS),
        in_specs=_ssd_specs(nc, True) + [pl.BlockSpec((1, SSD_STATE, SSD_GW), lambda c, g: ((nc - 1 - c) * SSD_GROUPS + g, 0, 0)), grp, grp],
        out_specs=[grp, st, st, hd, hd],
        out_shape=[jax.ShapeDtypeStruct((t, SSD_D_INNER), F32), jax.ShapeDtypeStruct((t, SSD_GROUPS * SSD_STATE), F32),
                   jax.ShapeDtypeStruct((t, SSD_GROUPS * SSD_STATE), F32), jax.ShapeDtypeStruct((t, LANES), F32),
                   jax.ShapeDtypeStruct((t, LANES), F32)],
        scratch_shapes=[pltpu.VMEM((SSD_GROUPS, SSD_STATE, SSD_GW), F32)],
        compiler_params=_params(("arbitrary", "arbitrary")),
    )(xact, xact, xact, dt, da, hsave, dy, dxskip)


def _ssd_act(row0, xc):
    return (jnp.where(_valid(row0, xc.shape[0]), jax.nn.silu(xc), 0.0),)


def _ssd_dt(row0, dtraw, dt_bias, a_log):
    dt = jnp.where(_valid(row0, dtraw.shape[0]), _softplus(dtraw + dt_bias), 0.0)
    return dt, dt * -jnp.exp(a_log)


def _ssd_post(row0, y, xs, z, d_skip, norm_g):
    hr = lax.broadcasted_iota(jnp.int32, (LANES, SSD_D_INNER), 0)
    hc = lax.broadcasted_iota(jnp.int32, (LANES, SSD_D_INNER), 1)
    expand = (hr == hc // SSD_HEAD_DIM).astype(F32)
    d_e = jnp.sum(_hp(jnp.broadcast_to(d_skip, (SUB, LANES)), expand), axis=0, keepdims=True) * (1.0 / SUB)
    return (_rms((y + xs * d_e) * jax.nn.silu(z), norm_g),)


ROPE_LO, ROPE_MID, ROPE_HI = MLA_NOPE, MLA_NOPE + MLA_ROPE // 2, MLA_NOPE + MLA_ROPE
ATT_SCALE = (MLA_NOPE + MLA_ROPE) ** -0.5


def _slot_lane(width):
    return lax.broadcasted_iota(jnp.int32, (1, width), 1) % LANES


def _swap_halves(x):
    width = x.shape[1]
    lane = _slot_lane(width)
    sw = jnp.where(lane < ROPE_MID, pltpu.roll(x, width - MLA_ROPE // 2, 1), pltpu.roll(x, MLA_ROPE // 2, 1))
    return jnp.where((lane >= ROPE_LO) & (lane < ROPE_HI), sw, 0.0)


def _rope(x, cos, sin):
    n = x.shape[1] // LANES
    return x * jnp.tile(cos, (1, n)) + _swap_halves(x) * jnp.tile(sin, (1, n))


def _rope_t(dy, cos, sin):
    n = dy.shape[1] // LANES
    return dy * jnp.tile(cos, (1, n)) + _swap_halves(dy * jnp.tile(sin, (1, n)))


ATT_SCALE2 = ATT_SCALE * float(np.log2(np.e))
MASKED = -1e30
ATT_STRIP = 64


def _att_mask(i, j, blk):
    rowid = i * blk + lax.broadcasted_iota(jnp.int32, (blk, 1), 0)
    colid = j * blk + lax.broadcasted_iota(jnp.int32, (1, blk), 1)
    return (colid <= rowid) & (colid >= PAD)


def _att_bias(blk):
    r = jnp.arange(blk)[:, None]
    c = jnp.arange(blk)[None, :]
    zero = jnp.zeros((blk, blk), F32)
    first = jnp.where(c >= PAD, 0.0, MASKED) + zero
    diag = jnp.where(c <= r, 0.0, MASKED).astype(F32)
    return jnp.stack([zero, first, diag, jnp.minimum(first, diag)])


def _att_bias_index(j, i):
    return jnp.where(j == 0, 1, 0) + jnp.where(j == i, 2, 0)


def _key_slots(row0, kv, kr):
    width = kv.shape[1]
    return jnp.where(_slot_lane(width) < MLA_NOPE, kv, jnp.tile(kr, (1, width // LANES))), kv


def _attn_fwd(qr, km, vb, name, carried=None):
    t = qr.shape[0]
    blk = _tile(t, 384, LANES)
    nq = t // blk

    bias = _att_bias(blk)

    def body(q_ref, k_ref, v_ref, b_ref, o_ref, s0, s1, p0, p1, m_s, l_s, al_s, acc_s):
        i = pl.program_id(1)
        lane = lax.broadcasted_iota(jnp.int32, (1, LANES), 1)
        qb = q_ref[...]

        def rows(j):
            return pl.ds(pl.multiple_of(jnp.clip(j, 0, i) * blk, blk), blk)

        def scores(j, s_buf):
            jc = jnp.minimum(j, i)
            s_buf[...] = lax.dot_general(qb, k_ref[rows(jc), :], _DIMS["nt"], preferred_element_type=F32) + b_ref[_att_bias_index(jc, i)]

        def add_pv(j, p_buf):
            acc_s[...] = al_s[...] * acc_s[...] + lax.dot_general(p_buf[...], v_ref[rows(j), :], _DIMS["nn"], preferred_element_type=F32)

        def softmax(s_buf, p_buf):
            m = m_s[...]
            m2 = jnp.maximum(m, jnp.max(s_buf[...], axis=1, keepdims=True))
            al = jnp.exp2((m - m2) * ATT_SCALE2)
            pm = jnp.exp2(s_buf[...] * ATT_SCALE2 - m2 * ATT_SCALE2)
            p_buf[...] = pm.astype(BF16)
            m_s[...] = m2
            l_s[...] = al * l_s[...] + jnp.sum(pm, axis=1, keepdims=True)
            al_s[...] = al

        m_s[...] = jnp.full((blk, 1), MASKED, F32)
        l_s[...] = jnp.zeros((blk, 1), F32)
        acc_s[...] = jnp.zeros((blk, LANES), F32)
        scores(0, s0)
        scores(1, s1)
        softmax(s0, p0)

        def step(tt, carry):
            j = 2 * tt + 1
            scores(j + 1, s0)
            add_pv(j - 1, p0)
            softmax(s1, p1)
            scores(j + 2, s1)
            add_pv(j, p1)
            softmax(s0, p0)
            return carry

        lax.fori_loop(0, i // 2, step, 0)

        @pl.when(i % 2 == 1)
        def _():
            add_pv(i - 1, p0)
            softmax(s1, p1)
            add_pv(i, p1)

        @pl.when(i % 2 == 0)
        def _():
            add_pv(i, p0)

        out = jnp.where(lane >= MLA_NOPE, acc_s[...] / l_s[...], m_s[...] * ATT_SCALE + jnp.log(l_s[...]))
        o_ref[...] = jnp.where(_valid(i * blk, blk), out, 0.0)

    seq_h = pl.BlockSpec((t, LANES), lambda h, i: (0, h))
    (o,), carried_out = _carry_call(
        body, name, (MLA_HEADS, nq),
        [pl.BlockSpec((blk, LANES), lambda h, i: (i, h)), seq_h, seq_h, _full_spec(bias.shape, 2)],
        [pl.BlockSpec((blk, LANES), lambda h, i: (i, h))], [jax.ShapeDtypeStruct((t, MLA_HEADS * LANES), F32)],
        [pltpu.VMEM((blk, blk), F32)] * 2 + [pltpu.VMEM((blk, blk), BF16)] * 2 + [pltpu.VMEM((blk, 1), F32)] * 3
        + [pltpu.VMEM((blk, LANES), F32)], (qr, km, vb, bias), carried)
    return o, carried_out


def _attn_bwd(qr, km, vb, o, do, name, carried=None):
    t = qr.shape[0]
    blk = _tile(t, 384, LANES)
    nq = t // blk

    bias = _att_bias(blk)
    log2e = float(np.log2(np.e))

    def body(q_ref, o_ref, do_ref, k_ref, v_ref, b_ref, dq_ref, dkv_ref, dkr_ref, s0, s1, dp0, dp1, p0, p1, ds0, ds1, dk_s, dv_s):
        h, j = pl.program_id(0), pl.program_id(1)
        lane = lax.broadcasted_iota(jnp.int32, (1, LANES), 1)

        @pl.when(j == 0)
        def _():
            dq_ref[...] = jnp.zeros_like(dq_ref)

        @pl.when((h == 0) & (j == 0))
        def _():
            dkr_ref[...] = jnp.zeros_like(dkr_ref)

        kmat, vmat = k_ref[...], v_ref[...]

        def rows(i):
            return pl.ds(pl.multiple_of(jnp.clip(i, j, nq - 1) * blk, blk), blk)

        def first_stage(i, s_buf, dp_buf):
            ic = jnp.minimum(i, nq - 1)
            s_buf[...] = lax.dot_general(q_ref[rows(ic), :], kmat, _DIMS["nt"], preferred_element_type=F32) + b_ref[_att_bias_index(j, ic)]
            dp_buf[...] = lax.dot_general(do_ref[rows(ic), :].astype(BF16), vmat, _DIMS["nt"], preferred_element_type=F32)

        def middle_stage(i, s_buf, dp_buf, p_buf, ds_buf):
            r = rows(i)
            ob, dob = o_ref[r, :], do_ref[r, :]
            delta = jnp.sum(dob * ob, axis=1, keepdims=True)
            pm = jnp.exp2(s_buf[...] * ATT_SCALE2 - ob[:, 0:1] * log2e)
            p_buf[...] = pm.astype(BF16)
            ds_buf[...] = (pm * (dp_buf[...] - delta) * ATT_SCALE).astype(BF16)

        def last_stage(i, p_buf, ds_buf):
            r = rows(i)
            dv_s[...] += lax.dot_general(p_buf[...], do_ref[r, :].astype(BF16), _DIMS["tn"], preferred_element_type=F32)
            dk_s[...] += lax.dot_general(ds_buf[...], q_ref[r, :], _DIMS["tn"], preferred_element_type=F32)
            dq_ref[r, :] += lax.dot_general(ds_buf[...], kmat, _DIMS["nn"], preferred_element_type=F32)

        n = nq - j
        dk_s[...] = jnp.zeros((blk, LANES), F32)
        dv_s[...] = jnp.zeros((blk, LANES), F32)
        first_stage(j, s0, dp0)
        first_stage(j + 1, s1, dp1)
        middle_stage(j, s0, dp0, p0, ds0)

        def step(tt, carry):
            i = j + 2 * tt + 1
            first_stage(i + 1, s0, dp0)
            last_stage(i - 1, p0, ds0)
            middle_stage(i, s1, dp1, p1, ds1)
            first_stage(i + 2, s1, dp1)
            last_stage(i, p1, ds1)
            middle_stage(i + 1, s0, dp0, p0, ds0)
            return carry

        lax.fori_loop(0, (n - 1) // 2, step, 0)

        @pl.when(n % 2 == 0)
        def _():
            last_stage(nq - 2, p0, ds0)
            middle_stage(nq - 1, s1, dp1, p1, ds1)
            last_stage(nq - 1, p1, ds1)

        @pl.when(n % 2 == 1)
        def _():
            last_stage(nq - 1, p0, ds0)

        dk = dk_s[...]
        dkv_ref[...] = jnp.where(lane < MLA_NOPE, dk, dv_s[...])
        dkr_ref[rows(j), :] += jnp.where(lane >= MLA_NOPE, dk, 0.0)

    seq_h = pl.BlockSpec((t, LANES), lambda h, j: (0, h))
    blk_h = pl.BlockSpec((blk, LANES), lambda h, j: (j, h))
    return _carry_call(
        body, name, (MLA_HEADS, nq), [seq_h, seq_h, seq_h, blk_h, blk_h, _full_spec(bias.shape, 2)],
        [seq_h, blk_h, pl.BlockSpec((t, LANES), lambda h, j: (0, 0))],
        [jax.ShapeDtypeStruct((t, MLA_HEADS * LANES), F32), jax.ShapeDtypeStruct((t, MLA_HEADS * LANES), F32),
         jax.ShapeDtypeStruct((t, LANES), F32)],
        [pltpu.VMEM((blk, blk), F32)] * 4 + [pltpu.VMEM((blk, blk), BF16)] * 4 + [pltpu.VMEM((blk, LANES), F32)] * 2,
        (qr, o, do, km, vb, bias), carried)


def _rms_rows(row0, x, g):
    return (_rms(x, g),)


def _ssdmla_fwd(h, p, l, e, cos, sin, carried=None):
    hn = _prenorm(h, p["mix_pre_g"][l], "sm_prenorm")
    proj = _mm(hn, p["w_in"][e], "nn", "sm_in")
    xc = _conv_fwd(proj, p["ssd_conv_w"][e], p["ssd_conv_b"][e], "ssd_conv", cw=SSD_GW, c0=PROJ_XBC // SSD_GW)
    xact = _rowwise("ssd_act", _ssd_act, [xc], [], [(SSD_CONV_CH, F32)])[0]
    dt, da = _rowwise("ssd_dt", _ssd_dt, [(proj, LANES, PROJ_DT // LANES)], [p["ssd_dt_bias"][e], p["ssd_a_log"][e]],
                      [(LANES, F32)] * 2)
    y, hsave = _ssd_scan(xact, dt, da, "ssd_scan")
    y_ssd = _rowwise("ssd_post", _ssd_post, [y, (xact, SSD_D_INNER, 0), (proj, SSD_D_INNER, 0)],
                     [p["ssd_d"][e], p["ssd_norm_g"][e]], [(SSD_D_INNER, BF16)])[0]
    cqn = _prenorm((proj, MLA_Q_RANK, PROJ_CQ // MLA_Q_RANK), p["mla_q_norm_g"][e], "mla_qnorm")
    ckvn = _prenorm((proj, MLA_KV_RANK, PROJ_CKV // MLA_KV_RANK), p["mla_kv_norm_g"][e], "mla_kvnorm")
    q = _mm(cqn, p["mla_w_q_up"][e], "nn", "mla_q_up")
    kv = _mm(ckvn, p["mla_w_kv_up"][e], "nn", "mla_kv_up")
    kr = _rowwise("mla_krope", lambda r0, x, c, s: (_rope(x, c, s),), [(proj, LANES, PROJ_KR // LANES), cos, sin], [],
                  [(LANES, F32)])[0]
    slots, tr = MLA_HEADS * LANES, _tile(h.shape[0], 264, 16)
    qr = _rowwise("mla_q_rope", lambda r0, a, c, s: (_rope(a, c, s),), [q, cos, sin], [], [(slots, BF16)], tr=tr)[0]
    km, vb = _rowwise("mla_key_slots", _key_slots, [kv, kr], [], [(slots, BF16)] * 2, tr=tr)
    o, carried_out = _attn_fwd(qr, km, vb, "mla_attn", carried)
    m1 = _mm(y_ssd, p["w_out_ssd"][e], "nn", "sm_out_ssd")
    m = _mm(o, p["w_out_att"][e], "nn", "sm_out_att", add=m1)
    h2 = _add_postnorm(h, [m], p["mix_post_g"][l], "sm_postnorm")
    return h2, (h, hn, proj, xc, xact, dt, da, y, hsave, y_ssd, cqn, ckvn, qr, km, vb, o, m), carried_out


def _ssdmla_bwd(dh, saved, p, l, e, cos, sin, grads, carried=None):
    h, hn, proj, xc, xact, dt, da, y, hsave, y_ssd, cqn, ckvn, qr, km, vb, o, m = saved
    dm, grads["mix_post_g"][l] = _postnorm_bwd(m, p["mix_post_g"][l], dh, "sm_postnorm_bwd")
    grads["w_out_ssd"][e] = _mm(y_ssd, dm, "tn", "sm_out_ssd_dw")
    grads["w_out_att"][e] = _mm(o, dm, "tn", "sm_out_att_dw")
    dy_ssd = _mm(dm, p["w_out_ssd"][e], "nt", "sm_out_ssd_dx")
    do = _mm(dm, p["w_out_att"][e], "nt", "sm_out_att_dx")
    (dqr, dkv, dkr), carried_out = _attn_bwd(qr, km, vb, o, do, "mla_attn_bwd", carried)
    dq = _rowwise("mla_q_rope_bwd", lambda r0, a, c, s: (_rope_t(a, c, s),), [dqr, cos, sin], [], [(MLA_HEADS * LANES, F32)],
                  tr=_tile(h.shape[0], 264, 16))[0]
    dkr_raw = _rowwise("mla_krope_bwd", lambda r0, d, c, s: (_rope_t(d, c, s),), [dkr, cos, sin], [], [(LANES, F32)])[0]
    grads["mla_w_q_up"][e] = _mm(cqn, dq, "tn", "mla_q_up_dw")
    dcqn = _mm(dq, p["mla_w_q_up"][e], "nt", "mla_q_up_dx")
    (dcq,), (grads["mla_q_norm_g"][e],) = _rowwise_vjp(
        "mla_qnorm_bwd", _rms_rows, [(proj, MLA_Q_RANK, PROJ_CQ // MLA_Q_RANK)], [p["mla_q_norm_g"][e]], [dcqn])
    grads["mla_w_kv_up"][e] = _mm(ckvn, dkv, "tn", "mla_kv_up_dw")
    dckvn = _mm(dkv, p["mla_w_kv_up"][e], "nt", "mla_kv_up_dx")
    (dckv,), (grads["mla_kv_norm_g"][e],) = _rowwise_vjp(
        "mla_kvnorm_bwd", _rms_rows, [(proj, MLA_KV_RANK, PROJ_CKV // MLA_KV_RANK)], [p["mla_kv_norm_g"][e]], [dckvn])
    (dy, dxskip, dz), (grads["ssd_d"][e], grads["ssd_norm_g"][e]) = _rowwise_vjp(
        "ssd_post_bwd", _ssd_post, [y, (xact, SSD_D_INNER, 0), (proj, SSD_D_INNER, 0)], [p["ssd_d"][e], p["ssd_norm_g"][e]], [dy_ssd])
    dxs, db, dc, ddt, dda = _ssd_scan_bwd(xact, dt, da, hsave, dy, dxskip, "ssd_scan_bwd")
    dxact = jnp.concatenate([dxs, db, dc], axis=1)
    (dxc,), _ = _rowwise_vjp("ssd_act_bwd", _ssd_act, [xc], [], [dxact])
    dxbc, grads["ssd_conv_w"][e], grads["ssd_conv_b"][e] = _conv_bwd(
        proj, p["ssd_conv_w"][e], dxc, "ssd_conv_bwd", cw=SSD_GW, c0=PROJ_XBC // SSD_GW)
    (ddtraw,), (grads["ssd_dt_bias"][e], grads["ssd_a_log"][e]) = _rowwise_vjp(
        "ssd_dt_bwd", _ssd_dt, [(proj, LANES, PROJ_DT // LANES)], [p["ssd_dt_bias"][e], p["ssd_a_log"][e]], [ddt, dda])
    dproj = jnp.concatenate([dz, dxbc, ddtraw, dcq, dckv, dkr_raw], axis=1)
    grads["w_in"][e] = _mm(hn, dproj, "tn", "sm_in_dw")
    dhn = _mm(dproj, p["w_in"][e], "nt", "sm_in_dx")
    dh, grads["mix_pre_g"][l] = _prenorm_bwd_add(h, p["mix_pre_g"][l], [dhn], dh, "sm_prenorm_bwd")
    return dh, carried_out


GAINS = ("mix_pre_g", "mix_post_g", "mlp_pre_g", "mlp_post_g", "ssd_norm_g", "mla_q_norm_g", "mla_kv_norm_g", "ssd_conv_b", "rg_conv_b")
HEAD_VECS = ("ssd_dt_bias", "ssd_a_log", "ssd_d")
LRU_VECS = ("rg_b_a", "rg_b_i", "rg_lambda")
IN_DT_END = SSD_D_INNER + SSD_CONV_CH + SSD_HEADS
IN_KR = IN_DT_END + MLA_Q_RANK + MLA_KV_RANK


def _each(a, f):
    layers = a if isinstance(a, list) else [a[i] for i in range(a.shape[0])]
    return [None if x is None else f(x) for x in layers]


def _layout_params(w):
    p = {k: _each(w[k], lambda a: a[None, :]) for k in GAINS}
    for k in HEAD_VECS:
        p[k] = _each(w[k], lambda a: jnp.pad(a, (0, LANES - SSD_HEADS))[None, :])
    for k in LRU_VECS:
        p[k] = _each(w[k], lambda a: a.reshape(LRU_BLOCKS, 1, LRU_BLOCK))
    for k in ("w_up", "w_down", "mla_w_kv_up", "rg_w_x", "rg_w_y", "rg_w_out"):
        p[k] = _each(w[k], lambda a: a if isinstance(a, Gathered) else a.astype(BF16))
    for k in ("ssd_conv_w", "rg_conv_w", "rg_w_a", "rg_w_i"):
        p[k] = _each(w[k], lambda a: a)

    def w_in(a):
        def zcols(n):
            return jnp.zeros((a.shape[0], n), a.dtype)

        return jnp.concatenate([a[:, :IN_DT_END], zcols(PROJ_CQ - IN_DT_END), a[:, IN_DT_END:IN_KR], zcols(ROPE_LO),
                                a[:, IN_KR:], zcols(LANES - ROPE_HI)], axis=1).astype(BF16)

    def q_up(a):
        a = a.reshape(MLA_Q_RANK, MLA_HEADS, MLA_NOPE + MLA_ROPE)
        return jnp.pad(a, ((0, 0), (0, 0), (0, LANES - MLA_NOPE - MLA_ROPE))).reshape(MLA_Q_RANK, MLA_HEADS * LANES).astype(BF16)

    def out_att(a):
        a = a[SSD_D_INNER:].reshape(MLA_HEADS, MLA_V, D_MODEL)
        return jnp.pad(a, ((0, 0), (LANES - MLA_V, 0), (0, 0))).reshape(MLA_HEADS * LANES, D_MODEL).astype(BF16)

    p["w_in"] = _each(w["w_in"], w_in)
    p["mla_w_q_up"] = _each(w["mla_w_q_up"], q_up)
    p["w_out_ssd"] = _each(w["w_out_ab"], lambda a: a[:SSD_D_INNER].astype(BF16))
    p["w_out_att"] = _each(w["w_out_ab"], out_att)
    return p


def _rope_tables(t):
    pos = (jnp.arange(t) - PAD).astype(F32)
    inv = ROPE_BASE ** (-jnp.arange(0, MLA_ROPE, 2, dtype=F32) / MLA_ROPE)
    ang = pos[:, None] * inv[None, :]
    c, s = jnp.cos(ang), jnp.sin(ang)
    one, zero = jnp.ones((t, MLA_NOPE), F32), jnp.zeros((t, MLA_NOPE), F32)
    tail = LANES - ROPE_HI
    return (jnp.concatenate([one, c, c, one[:, :tail]], axis=1), jnp.concatenate([zero, -s, s, zero[:, :tail]], axis=1))


GRAD_KEYS = GAINS + HEAD_VECS + LRU_VECS + ("w_up", "w_down", "mla_w_kv_up", "rg_w_x", "rg_w_y", "rg_w_out", "ssd_conv_w",
                                            "rg_conv_w", "rg_w_a", "rg_w_i", "w_in", "mla_w_q_up", "w_out_ssd", "w_out_att")


def _device_step(x, meta, target, p, hooks=None):
    t = PAD + N_META + x.shape[0]
    cos, sin = _rope_tables(t)
    h = jnp.concatenate([jnp.zeros((PAD, D_MODEL), F32), meta, x], axis=0)
    n_even, n_odd = (DEPTH + 1) // 2, DEPTH // 2
    saved = []
    for l in range(DEPTH):
        if l % 2 == 0:
            carried = hooks.forward_exchange() if hooks and l == 0 else None
            h, sm, arrived = _ssdmla_fwd(h, p, l, l // 2, cos, sin, carried)
            if carried is not None:
                p = hooks.after_forward_exchange(arrived)
        else:
            h, sm = _rglru_fwd(h, p, l, l // 2)
        h, sp = _mlp_fwd(h, p, l)
        saved.append((sm, sp))
    sq, dh = _loss_and_grad(h, target, "loss")
    per_layer = {"mix_pre_g": DEPTH, "mix_post_g": DEPTH, "mlp_pre_g": DEPTH, "mlp_post_g": DEPTH, "w_up": DEPTH, "w_down": DEPTH}
    grads = {k: [None] * per_layer.get(k, n_odd if k.startswith("rg_") else n_even) for k in GRAD_KEYS}
    for l in reversed(range(DEPTH)):
        sm, sp = saved[l]
        dh = _mlp_bwd(dh, sp, p, l, grads)
        if l % 2 == 0:
            carried = hooks.backward_exchange(grads, l) if hooks else None
            dh, arrived = _ssdmla_bwd(dh, sm, p, l, l // 2, cos, sin, grads, carried)
            if carried is not None:
                hooks.after_backward_exchange(arrived, l)
        else:
            dh = _rglru_bwd(dh, sm, p, l, l // 2, grads)
    return sq, dh, grads


MESH = pl.DeviceIdType.MESH
ANY = pl.BlockSpec(memory_space=pl.ANY)


def _mesh_pos():
    return lax.axis_index("x"), lax.axis_index("y"), lax.axis_index("c")


def _other_chips(x, y):
    return [(1 - x, y), (x, 1 - y), (1 - x, 1 - y)]


def _remote(src, dst, send_sems, recv_sems, k, to):
    return pltpu.make_async_remote_copy(src_ref=src, dst_ref=dst, send_sem=send_sems.at[k], recv_sem=recv_sems.at[k],
                                        device_id=to, device_id_type=MESH)


class Exchange:
    def __init__(self, ins, outs, aliases, n_sems, plan):
        self.ins, self.outs, self.aliases, self.n_sems, self.plan = list(ins), list(outs), dict(aliases), n_sems, plan


def _sems(n):
    return [pltpu.SemaphoreType.DMA((n,)), pltpu.SemaphoreType.DMA((n,))]


def _run_exchange(name, ex):
    ni, no = len(ex.ins), len(ex.outs)

    def body(*refs):
        sends = ex.plan(refs[:ni], refs[ni:ni + no], refs[-2], refs[-1], False)
        for cp in sends:
            cp.start()
        for cp in ex.plan(refs[:ni], refs[ni:ni + no], refs[-2], refs[-1], True):
            cp.wait_recv()
        for cp in sends:
            cp.wait_send()

    return pl.pallas_call(body, name=name, in_specs=[ANY] * ni, out_specs=[ANY] * no, out_shape=ex.outs,
                          input_output_aliases=ex.aliases, scratch_shapes=_sems(ex.n_sems))(*ex.ins)


def _carry_call(body, name, grid, in_specs, out_specs, out_shape, scratch_shapes, args, ex):
    if ex is None:
        res = pl.pallas_call(body, name=name, grid=grid, in_specs=in_specs, out_specs=out_specs, out_shape=out_shape,
                             scratch_shapes=scratch_shapes, compiler_params=_params(("arbitrary",) * len(grid)))(*args)
        return res, None
    ni, no, ns, xi, xo = len(in_specs), len(out_specs), len(scratch_shapes), len(ex.ins), len(ex.outs)

    def wrapped(*refs):
        ins, xin = refs[:ni], refs[ni:ni + xi]
        outs, xout = refs[ni + xi:ni + xi + no], refs[ni + xi + no:ni + xi + no + xo]
        scr, send_sems, recv_sems = refs[ni + xi + no + xo:-2], refs[-2], refs[-1]
        pid = [pl.program_id(d) for d in range(len(grid))]
        first = functools.reduce(jnp.logical_and, [p == 0 for p in pid])
        last = functools.reduce(jnp.logical_and, [p == g - 1 for p, g in zip(pid, grid)])

        @pl.when(first)
        def _():
            for cp in ex.plan(xin, xout, send_sems, recv_sems, False):
                cp.start()

        body(*ins, *outs, *scr)

        @pl.when(last)
        def _():
            for cp in ex.plan(xin, xout, send_sems, recv_sems, True):
                cp.wait_recv()
            for cp in ex.plan(xin, xout, send_sems, recv_sems, False):
                cp.wait_send()

    res = pl.pallas_call(
        wrapped, name=name, grid=grid, in_specs=list(in_specs) + [ANY] * xi, out_specs=list(out_specs) + [ANY] * xo,
        out_shape=list(out_shape) + ex.outs, scratch_shapes=list(scratch_shapes) + _sems(ex.n_sems),
        input_output_aliases={ni + i: no + o for i, o in ex.aliases.items()},
        compiler_params=_params(("arbitrary",) * len(grid)))(*args, *ex.ins)
    return res[:no], res[no:]


def _gather_ici(srcs, bufs, ranges):
    n = len(srcs)

    def plan(in_refs, out_refs, ss, rs, arrivals):
        x, y, c = _mesh_pos()
        cps = []
        for t, (l0, nl) in enumerate(ranges):
            if nl:
                s, o, lr = in_refs[t], out_refs[t], pl.ds(l0, nl)
                for j, (cx, cy) in enumerate(_other_chips(x, y)):
                    chip = 2 * cx + cy if arrivals else 2 * x + y
                    cps.append(_remote(s.at[lr, c], o.at[chip, lr, c], ss, rs, (N_CHIPS - 1) * t + j, (cx, cy, c)))
        return cps

    outs = [jax.ShapeDtypeStruct((N_CHIPS,) + s.shape, s.dtype) for s in srcs]
    if bufs is None:
        return Exchange(srcs, outs, {}, (N_CHIPS - 1) * n, plan)
    return Exchange(list(srcs) + list(bufs), outs, {n + t: t for t in range(n)}, (N_CHIPS - 1) * n, plan)


def _gather_d2d(srcs, bufs, ranges):
    n = len(srcs)

    def plan(in_refs, out_refs, ss, rs, arrivals):
        x, y, c = _mesh_pos()
        sib, me = (x, y, 1 - c), 2 * x + y
        cps = []
        for t, (l0, nl) in enumerate(ranges):
            if nl:
                s, o, lr = in_refs[t], out_refs[t], pl.ds(l0, nl)
                for j, (cx, cy) in enumerate(_other_chips(x, y)):
                    slot = o.at[2 * cx + cy, lr, c]
                    cps.append(_remote(slot, o.at[2 * cx + cy, lr, 1 - c] if arrivals else slot, ss, rs, N_CHIPS * t + j, sib))
                cps.append(_remote(s.at[lr], o.at[me, lr], ss, rs, N_CHIPS * t + N_CHIPS - 1, sib))
        return cps

    outs = [jax.ShapeDtypeStruct(b.shape, b.dtype) for b in bufs]
    return Exchange(list(srcs) + list(bufs), outs, {n + t: t for t in range(n)}, N_CHIPS * n, plan)


def _gather_chips(srcs, name):
    ranges = [(0, s.shape[0]) for s in srcs]
    bufs = _run_exchange(name + "_ici", _gather_ici(srcs, None, ranges))
    return _run_exchange(name + "_d2d", _gather_d2d(srcs, bufs, ranges))


def _pair_exchange(gs):
    def plan(in_refs, out_refs, ss, rs, arrivals):
        x, y, c = _mesh_pos()
        return [_remote(g.at[pl.ds(0, N_CHIPS), 1 - c], o, ss, rs, t, (x, y, 1 - c)) for t, (g, o) in enumerate(zip(in_refs, out_refs))]

    return Exchange(gs, [jax.ShapeDtypeStruct((g.shape[0],) + g.shape[2:], g.dtype) for g in gs], {}, len(gs), plan)


def _chip_exchange(ps, slots, qs, q_shapes):
    n = len(ps)
    kept = [g for g, q in enumerate(qs) if q is not None]

    def plan(in_refs, out_refs, ss, rs, arrivals):
        x, y, c = _mesh_pos()
        return [_remote(in_refs[t].at[2 * cx + cy], out_refs[g].at[j, li], ss, rs, (N_CHIPS - 1) * t + j, (cx, cy, c))
                for t, (g, li) in enumerate(slots) for j, (cx, cy) in enumerate(_other_chips(x, y))]

    return Exchange(list(ps) + [qs[g] for g in kept], q_shapes, {n + i: g for i, g in enumerate(kept)}, (N_CHIPS - 1) * n, plan)


def _pair_share(fs):
    def plan(in_refs, out_refs, ss, rs, arrivals):
        x, y, c = _mesh_pos()
        return [_remote(o.at[pl.ds(0, o.shape[0]), c], o.at[pl.ds(0, o.shape[0]), 1 - c if arrivals else c], ss, rs, t, (x, y, 1 - c))
                for t, o in enumerate(out_refs)]

    return Exchange(fs, [jax.ShapeDtypeStruct(f.shape, f.dtype) for f in fs], {t: t for t in range(len(fs))}, len(fs), plan)


SUM_BLOCK = 512 * 1024


def _sum_pair(g, ra, c, name):
    n, _, h, w = g.shape
    tr = _tile(h, max(16, SUM_BLOCK // w), 16)

    def body(c_ref, g_ref, r_ref, o_ref):
        o_ref[...] = (g_ref[0] + r_ref[...]).astype(o_ref.dtype)

    return pl.pallas_call(
        body, name=name,
        grid_spec=pltpu.PrefetchScalarGridSpec(
            num_scalar_prefetch=1, grid=(n, h // tr),
            in_specs=[pl.BlockSpec((1, 1, tr, w), lambda s, i, cr: (s, cr[0], i, 0)), pl.BlockSpec((1, tr, w), lambda s, i, cr: (s, i, 0))],
            out_specs=pl.BlockSpec((1, tr, w), lambda s, i, cr: (s, i, 0))),
        out_shape=jax.ShapeDtypeStruct((n, h, w), BF16),
        compiler_params=_params(("parallel", "parallel")),
    )(c.reshape(1).astype(jnp.int32), g, ra)


def _sum_chips(ps, q, pos, name):
    nc, nl, h, w = q.shape
    tr = _tile(h, max(16, SUM_BLOCK // (w * nl)), 16)

    def body(x_ref, y_ref, c_ref, *refs):
        q_ref, o_ref = refs[nl], refs[nl + 1]
        for l in range(nl):
            acc = refs[l][0].astype(F32)
            for j in range(nc):
                acc = acc + q_ref[j, l].astype(F32)
            o_ref[l] = acc

    return pl.pallas_call(
        body, name=name,
        grid_spec=pltpu.PrefetchScalarGridSpec(
            num_scalar_prefetch=3, grid=(h // tr,),
            in_specs=[pl.BlockSpec((1, tr, w), lambda i, x, y, c: (2 * x[0] + y[0], i, 0))] * nl
            + [pl.BlockSpec((nc, nl, tr, w), lambda i, x, y, c: (0, 0, i, 0))],
            out_specs=pl.BlockSpec((nl, None, tr, w), lambda i, x, y, c: (0, c[0], i, 0))),
        out_shape=jax.ShapeDtypeStruct((nl, 2, h, w), F32),
        compiler_params=_params(("parallel",)),
    )(*pos, *ps, q)


def _adamw(g, w, m, v, name):
    def f(r0, gg, ww, mm, vv):
        m2 = ADAM_B1 * mm + (1.0 - ADAM_B1) * gg
        v2 = ADAM_B2 * vv + (1.0 - ADAM_B2) * jnp.square(gg)
        m_hat = m2 / (1.0 - ADAM_B1 ** ADAM_STEP)
        v_hat = v2 / (1.0 - ADAM_B2 ** ADAM_STEP)
        return gg, -ADAM_LR * (m_hat / (jnp.sqrt(v_hat) + ADAM_EPS) + ADAM_WD * ww), m2, v2

    return _rowwise(name, f, [g, w, m, v], [], [(g.shape[1], F32)] * 4, tr=_tile(g.shape[0], 512))


WEIGHTS = (
    ("meta_tokens", (N_META, D_MODEL), 1), ("mix_pre_g", (DEPTH, D_MODEL), None), ("mix_post_g", (DEPTH, D_MODEL), None),
    ("mlp_pre_g", (DEPTH, D_MODEL), None), ("mlp_post_g", (DEPTH, D_MODEL), None), ("w_up", (DEPTH, D_MODEL, D_FF), 2),
    ("w_down", (DEPTH, D_FF, D_MODEL), 1), ("w_in", (2, D_MODEL, 3248), 2), ("ssd_conv_w", (2, CONV_K, SSD_CONV_CH), 2),
    ("ssd_conv_b", (2, SSD_CONV_CH), None), ("ssd_dt_bias", (2, SSD_HEADS), None), ("ssd_a_log", (2, SSD_HEADS), None),
    ("ssd_d", (2, SSD_HEADS), None), ("ssd_norm_g", (2, SSD_D_INNER), None), ("mla_q_norm_g", (2, MLA_Q_RANK), None),
    ("mla_w_q_up", (2, MLA_Q_RANK, MLA_HEADS * (MLA_NOPE + MLA_ROPE)), 2), ("mla_kv_norm_g", (2, MLA_KV_RANK), None),
    ("mla_w_kv_up", (2, MLA_KV_RANK, MLA_HEADS * (MLA_NOPE + MLA_V)), 2), ("w_out_ab", (2, SSD_D_INNER + MLA_HEADS * MLA_V, D_MODEL), 1),
    ("rg_w_x", (2, D_MODEL, LRU_WIDTH), 2), ("rg_w_y", (2, D_MODEL, LRU_WIDTH), 2), ("rg_conv_w", (2, CONV_K, LRU_WIDTH), 2),
    ("rg_conv_b", (2, LRU_WIDTH), 1), ("rg_w_a", (2, LRU_BLOCKS, LRU_BLOCK, LRU_BLOCK), None), ("rg_b_a", (2, LRU_WIDTH), 1),
    ("rg_w_i", (2, LRU_BLOCKS, LRU_BLOCK, LRU_BLOCK), None), ("rg_b_i", (2, LRU_WIDTH), 1), ("rg_lambda", (2, LRU_WIDTH), 1),
    ("rg_w_out", (2, LRU_WIDTH, D_MODEL), 1),
)
BIG = {"w_up": "col", "w_down": "row", "w_in": "col", "mla_w_q_up": "col", "mla_w_kv_up": "col", "w_out_ab": "row",
       "rg_w_x": "col", "rg_w_y": "col", "rg_w_out": "row"}
DIRECT = ("w_up", "w_down")
FLAT_QUANTUM = 2 * 16 * LANES
TABLE = {name: (shape, d) for name, shape, d in WEIGHTS}
SMALL_SHARDED = tuple(name for name, _, d in WEIGHTS if d is not None and name not in BIG)
REPLICATED = tuple(name for name, _, d in WEIGHTS if d is None)


def _chips_to_full(a, kind):
    if kind == "col":
        return jnp.moveaxis(a, 0, 2).reshape(a.shape[1], a.shape[2], -1)
    return jnp.moveaxis(a, 0, 1).reshape(a.shape[1], -1, a.shape[3])


def _full_to_chips(g, kind):
    if kind == "col":
        return jnp.moveaxis(g.reshape(g.shape[0], N_CHIPS, -1), 1, 0)
    return g.reshape(N_CHIPS, -1, g.shape[1])


def _chips_to_full_1(pc, kind):
    return jnp.moveaxis(pc, 0, 1).reshape(pc.shape[1], -1) if kind == "col" else pc.reshape(-1, pc.shape[2])


def _shard_shape(shape, d):
    return shape[:d] + (shape[d] // N_CHIPS,) + shape[d + 1:]


def _shard_major(full, d):
    s = full.shape
    return jnp.moveaxis(full.reshape(s[:d] + (N_CHIPS, s[d] // N_CHIPS) + s[d + 1:]), d, 0).reshape(N_CHIPS, -1)


def _from_shard_major(a, shape, d):
    ss = _shard_shape(shape, d)
    return jnp.moveaxis(a.reshape((N_CHIPS,) + ss), 0, d).reshape(shape)


def _pad_cols(a, quantum):
    n = a.shape[-1]
    return jnp.pad(a, [(0, 0)] * (a.ndim - 1) + [(0, -n % quantum)])


def _big_pieces(g):
    def w_in(a):
        return jnp.concatenate([a[:, :IN_DT_END], a[:, PROJ_CQ:PROJ_KR], a[:, PROJ_KR + ROPE_LO:PROJ_KR + ROPE_HI]], axis=1)

    def q_up(a):
        return a.reshape(MLA_Q_RANK, MLA_HEADS, LANES)[:, :, :MLA_NOPE + MLA_ROPE].reshape(MLA_Q_RANK, -1)

    def out_ab(sa):
        s, a = sa
        return jnp.concatenate([s, a.reshape(MLA_HEADS, LANES, D_MODEL)[:, LANES - MLA_V:, :].reshape(-1, D_MODEL)], axis=0)

    ident = lambda a: a
    full = {"w_down": _each(g["w_down"], ident), "w_in": _each(g["w_in"], w_in), "mla_w_q_up": _each(g["mla_w_q_up"], q_up),
            "mla_w_kv_up": _each(g["mla_w_kv_up"], ident),
            "w_out_ab": _each([None if s is None or a is None else (s, a) for s, a in zip(g["w_out_ssd"], g["w_out_att"])], out_ab),
            "rg_w_x": _each(g["rg_w_x"], ident), "rg_w_y": _each(g["rg_w_y"], ident), "rg_w_out": _each(g["rg_w_out"], ident)}
    return {name: (list(g[name]) if name == "w_up" else _each(full[name], lambda a, k=BIG[name]: _full_to_chips(a, k))) for name in BIG}


def _small_grads(g, dh):
    out = {k: jnp.stack(g[k])[:, 0, :] for k in GAINS}
    for k in HEAD_VECS:
        out[k] = jnp.stack(g[k])[:, 0, :SSD_HEADS]
    for k in LRU_VECS:
        out[k] = jnp.stack(g[k]).reshape(-1, LRU_WIDTH)
    for k in ("ssd_conv_w", "rg_conv_w", "rg_w_a", "rg_w_i"):
        out[k] = jnp.stack(g[k])
    out["meta_tokens"] = dh[PAD:PAD + N_META]
    return out


def _natural_grads(g, dh):
    out = _small_grads(g, dh)
    for name, pcs in _big_pieces(g).items():
        out[name] = jnp.stack([_chips_to_full_1(pc, BIG[name]) for pc in pcs])
    return out


class StepExchanges:
    def __init__(self, w):
        self.w = w
        self.c = lax.axis_index("c")
        self.riding, self.ras = {}, {}
        small = _pad_cols(jnp.concatenate([w[n].reshape(-1) for n in SMALL_SHARDED]), FLAT_QUANTUM).reshape(1, 2, -1, LANES)
        self.srcs = [self._halves(w[n].astype(BF16)) for n in BIG] + [small]
        first = {n: (0, 1 if n in ("w_in", "mla_w_q_up", "mla_w_kv_up", "w_out_ab") else 0) for n in BIG}
        self.first = [first[n] for n in BIG] + [(0, 1)]
        self.rest = [(nl, TABLE[n][0][0] - nl) for n, (_, nl) in zip(BIG, self.first)] + [(0, 0)]
        bufs = _run_exchange("gather_first_ici", _gather_ici(self.srcs, None, self.first))
        self.bufs = _run_exchange("gather_first_d2d", _gather_d2d(self.srcs, bufs, self.first))

    @staticmethod
    def _halves(a):
        return a.reshape(a.shape[0], 2, a.shape[1] // 2, a.shape[2])

    def params(self, ranges):
        w = self.w
        full = {n: w[n] for n in REPLICATED}
        for name, buf, (l0, nl) in zip(BIG, self.bufs, ranges):
            a = buf.reshape(buf.shape[:2] + (-1, buf.shape[4]))
            have = range(l0, l0 + nl)
            if name in DIRECT:
                full[name] = [Gathered(a, BIG[name], l) if l in have else None for l in range(a.shape[1])]
            else:
                full[name] = [_chips_to_full(a[:, l:l + 1], BIG[name])[0] if l in have else None for l in range(a.shape[1])]
        got, off = self.bufs[-1].reshape(N_CHIPS, -1), 0
        for name in SMALL_SHARDED:
            shape, d = TABLE[name]
            n = int(np.prod(_shard_shape(shape, d)))
            full[name] = _from_shard_major(got[:, off:off + n], shape, d)
            off += n
        self.meta = full.pop("meta_tokens")
        return _layout_params(full)

    def forward_exchange(self):
        return _gather_ici(self.srcs, self.bufs, self.rest)

    def after_forward_exchange(self, arrived):
        self.bufs = _run_exchange("gather_rest_d2d", _gather_d2d(self.srcs, arrived, self.rest))
        return self.params([(0, TABLE[n][0][0]) for n in BIG])

    def _pair_sums(self, pieces, tag):
        keys = list(pieces)
        ras = _run_exchange("grads_pair_exchange_" + tag, _pair_exchange([pieces[k] for k in keys]))
        return {k: _sum_pair(pieces[k], ra, self.c, "grads_pair_sum") for k, ra in zip(keys, ras)}

    def _q_shapes(self):
        return [jax.ShapeDtypeStruct((N_CHIPS - 1, s.shape[0]) + s.shape[2:], BF16) for s in self.srcs[:-1]]

    def backward_exchange(self, grads, layer):
        big = _big_pieces(grads)
        pieces = {(g, l): pc.reshape(N_CHIPS, 2, pc.shape[1] // 2, pc.shape[2]) for g, name in enumerate(BIG)
                  for l, pc in enumerate(big[name]) if pc is not None and (g, l) not in self.riding}
        if layer > 0:
            self.riding = pieces
            return _pair_exchange(list(pieces.values()))
        self.ps = {k: _sum_pair(self.riding[k], ra, self.c, "grads_pair_sum") for k, ra in self.ras.items()}
        self.ps.update(self._pair_sums(pieces, "early"))
        self.early = list(self.ps)
        return _chip_exchange([self.ps[k] for k in self.early], self.early, [None] * len(BIG), self._q_shapes())

    def after_backward_exchange(self, arrived, layer):
        if layer > 0:
            self.ras = dict(zip(self.riding, arrived))
        else:
            self.qs = list(arrived)

    def finish(self, grads, dh):
        big, small = _big_pieces(grads), _small_grads(grads, dh)
        pieces = {(g, l): pc.reshape(N_CHIPS, 2, pc.shape[1] // 2, pc.shape[2])
                  for g, name in enumerate(BIG) for l, pc in enumerate(big[name]) if (g, l) not in self.ps}
        sharded = jnp.concatenate([_shard_major(small[n], TABLE[n][1]) for n in SMALL_SHARDED], axis=1)
        rep = _pad_cols(jnp.concatenate([small[n].reshape(-1) for n in REPLICATED]), N_CHIPS * FLAT_QUANTUM)
        n_sh, n_rep = sharded.shape[1], rep.shape[0] // N_CHIPS
        flat = _pad_cols(jnp.concatenate([sharded, rep.reshape(N_CHIPS, n_rep)], axis=1), FLAT_QUANTUM)
        pieces[(len(BIG), 0)] = flat.reshape(N_CHIPS, 2, -1, LANES)
        late = self._pair_sums(pieces, "late")
        self.ps.update(late)
        keys = list(late)
        small_q = jax.ShapeDtypeStruct((N_CHIPS - 1, 1) + late[(len(BIG), 0)].shape[1:], BF16)
        qs = _run_exchange("grads_chip_exchange_late",
                           _chip_exchange([late[k] for k in keys], keys, self.qs + [None], self._q_shapes() + [small_q]))
        pos = [lax.axis_index(a).reshape(1).astype(jnp.int32) for a in ("x", "y", "c")]
        sums = [_sum_chips([self.ps[(g, l)] for l in range(q.shape[1])], q, pos, "grads_chip_sum") for g, q in enumerate(qs)]
        outs = _run_exchange("grads_pair_share", _pair_share(sums))
        out = {name: o.reshape(o.shape[0], -1, o.shape[3]) for name, o in zip(BIG, outs)}
        f = outs[-1].reshape(-1)
        rep_all = _gather_chips([f[n_sh:n_sh + n_rep].reshape(1, 2, -1, LANES)], "grads_gather_replicated")[0].reshape(-1)
        off = 0
        for name in SMALL_SHARDED:
            ss = _shard_shape(*TABLE[name])
            n = int(np.prod(ss))
            out[name] = f[off:off + n].reshape(ss)
            off += n
        off = 0
        for name in REPLICATED:
            shape = TABLE[name][0]
            n = int(np.prod(shape))
            out[name] = rep_all[off:off + n].reshape(shape)
            off += n
        return out


def kernel(x, meta_tokens, mix_pre_g, mix_post_g, mlp_pre_g, mlp_post_g, w_up, w_down, w_in, ssd_conv_w, ssd_conv_b, ssd_dt_bias, ssd_a_log, ssd_d, ssd_norm_g, mla_q_norm_g, mla_w_q_up, mla_kv_norm_g, mla_w_kv_up, w_out_ab, rg_w_x, rg_w_y, rg_conv_w, rg_conv_b, rg_w_a, rg_b_a, rg_w_i, rg_b_i, rg_lambda, rg_w_out, loss_target, m_meta_tokens, m_mix_pre_g, m_mix_post_g, m_mlp_pre_g, m_mlp_post_g, m_w_up, m_w_down, m_w_in, m_ssd_conv_w, m_ssd_conv_b, m_ssd_dt_bias, m_ssd_a_log, m_ssd_d, m_ssd_norm_g, m_mla_q_norm_g, m_mla_w_q_up, m_mla_kv_norm_g, m_mla_w_kv_up, m_w_out_ab, m_rg_w_x, m_rg_w_y, m_rg_conv_w, m_rg_conv_b, m_rg_w_a, m_rg_b_a, m_rg_w_i, m_rg_b_i, m_rg_lambda, m_rg_w_out, v_meta_tokens, v_mix_pre_g, v_mix_post_g, v_mlp_pre_g, v_mlp_post_g, v_w_up, v_w_down, v_w_in, v_ssd_conv_w, v_ssd_conv_b, v_ssd_dt_bias, v_ssd_a_log, v_ssd_d, v_ssd_norm_g, v_mla_q_norm_g, v_mla_w_q_up, v_mla_kv_norm_g, v_mla_w_kv_up, v_w_out_ab, v_rg_w_x, v_rg_w_y, v_rg_conv_w, v_rg_conv_b, v_rg_w_a, v_rg_b_a, v_rg_w_i, v_rg_b_i, v_rg_lambda, v_rg_w_out):
    names = [n for n, _, _ in WEIGHTS]
    w = dict(zip(names, (meta_tokens, mix_pre_g, mix_post_g, mlp_pre_g, mlp_post_g, w_up, w_down, w_in, ssd_conv_w, ssd_conv_b, ssd_dt_bias, ssd_a_log, ssd_d, ssd_norm_g, mla_q_norm_g, mla_w_q_up, mla_kv_norm_g, mla_w_kv_up, w_out_ab, rg_w_x, rg_w_y, rg_conv_w, rg_conv_b, rg_w_a, rg_b_a, rg_w_i, rg_b_i, rg_lambda, rg_w_out)))
    m = dict(zip(names, (m_meta_tokens, m_mix_pre_g, m_mix_post_g, m_mlp_pre_g, m_mlp_post_g, m_w_up, m_w_down, m_w_in, m_ssd_conv_w, m_ssd_conv_b, m_ssd_dt_bias, m_ssd_a_log, m_ssd_d, m_ssd_norm_g, m_mla_q_norm_g, m_mla_w_q_up, m_mla_kv_norm_g, m_mla_w_kv_up, m_w_out_ab, m_rg_w_x, m_rg_w_y, m_rg_conv_w, m_rg_conv_b, m_rg_w_a, m_rg_b_a, m_rg_w_i, m_rg_b_i, m_rg_lambda, m_rg_w_out)))
    v = dict(zip(names, (v_meta_tokens, v_mix_pre_g, v_mix_post_g, v_mlp_pre_g, v_mlp_post_g, v_w_up, v_w_down, v_w_in, v_ssd_conv_w, v_ssd_conv_b, v_ssd_dt_bias, v_ssd_a_log, v_ssd_d, v_ssd_norm_g, v_mla_q_norm_g, v_mla_w_q_up, v_mla_kv_norm_g, v_mla_w_kv_up, v_w_out_ab, v_rg_w_x, v_rg_w_y, v_rg_conv_w, v_rg_conv_b, v_rg_w_a, v_rg_b_a, v_rg_w_i, v_rg_b_i, v_rg_lambda, v_rg_w_out)))
    ex = StepExchanges(w)
    p = ex.params(ex.first)
    sq, dh, grads = _device_step(x[0], ex.meta, loss_target[0], p, hooks=ex)
    loss = lax.psum(0.5 * sq[0, 0] / D_MODEL, ("x", "y", "c"))
    g = ex.finish(grads, dh)
    grad, delta, new_m, new_v = {}, {}, {}, {}
    for name in names:
        shape = g[name].shape
        two_d = (int(np.prod(shape[:-1])), shape[-1])
        res = _adamw(g[name].reshape(two_d), w[name].reshape(two_d), m[name].reshape(two_d), v[name].reshape(two_d), "adamw")
        grad[name], delta[name], new_m[name], new_v[name] = (r.reshape(shape) for r in res)
    grad_x = dh[PAD + N_META:][None]
    return (loss, grad_x, *[grad[n] for n in names], *[delta[n] for n in names], *[new_m[n] for n in names], *[new_v[n] for n in names])
```

```python
import functools

import jax
import jax.numpy as jnp
import numpy as np
from jax import lax
from jax.experimental import pallas as pl
from jax.experimental.pallas import tpu as pltpu

F32 = jnp.float32
BF16 = jnp.bfloat16

D_MODEL = 1024
DEPTH = 4
N_META = 16
CHUNK = 128
PAD = CHUNK - N_META
EPS = 1e-6
SSD_HEADS = 16
SSD_HEAD_DIM = 64
SSD_D_INNER = SSD_HEADS * SSD_HEAD_DIM
SSD_GROUPS = 2
SSD_STATE = 128
SSD_CONV_CH = SSD_D_INNER + 2 * SSD_GROUPS * SSD_STATE
MLA_HEADS = 16
MLA_NOPE = 64
MLA_ROPE = 32
MLA_V = 64
MLA_Q_RANK = 384
MLA_KV_RANK = 256
ROPE_BASE = 10000.0
LRU_WIDTH = 1280
LRU_BLOCKS = 10
LRU_BLOCK = 128
LRU_C = 8.0
D_FF = 4 * D_MODEL
ADAM_LR, ADAM_B1, ADAM_B2, ADAM_EPS, ADAM_WD, ADAM_STEP = 0.001, 0.9, 0.999, 1e-08, 0.01, 10

LANES = 128
VMEM_LIMIT = 56 * 1024 * 1024
HEAD_SLOT = 128
PROJ_Z, PROJ_XBC, PROJ_DT, PROJ_CQ, PROJ_CKV, PROJ_KR = 0, 1024, 2560, 2688, 3072, 3328
PROJ_W = 3456


def _tile(n, cap, mult=8):
    for t in range(min(n, cap), 0, -1):
        if n % t == 0 and t % mult == 0:
            return t
    return n


def _params(sem):
    return pltpu.CompilerParams(dimension_semantics=sem, vmem_limit_bytes=VMEM_LIMIT)


def _full_spec(shape, ngrid):
    nd = len(shape)
    if ngrid == 1:
        return pl.BlockSpec(shape, lambda i: (0,) * nd)
    if ngrid == 2:
        return pl.BlockSpec(shape, lambda i, j: (0,) * nd)
    return pl.BlockSpec(shape, lambda i, j, k: (0,) * nd)


_DIMS = {"nn": (((1,), (0,)), ((), ())), "nt": (((1,), (1,)), ((), ())), "tn": (((0,), (0,)), ((), ()))}


class Gathered:
    def __init__(self, arr, kind, layer):
        self.arr, self.kind, self.layer = arr, kind, layer
        _, _, r, c = arr.shape
        self.shape = (r, N_CHIPS * c) if kind == "col" else (N_CHIPS * r, c)


N_CHIPS = 4


def _mm(a, b, mode, name, out_dtype=F32, add=None, out_chip_major=False, extra=(), vecs=(), post=None, out_dtypes=None):
    if mode == "nn":
        (m, kc), (_, n) = a.shape, b.shape
    elif mode == "nt":
        (m, kc), (n, _) = a.shape, b.shape
    else:
        (kc, m), (_, n) = a.shape, b.shape
    tm = _tile(m, 1024, LANES) if mode == "tn" else _tile(m, 1056, 16)
    tn = _tile(n // N_CHIPS if out_chip_major else n, 1280, LANES)
    tk = _tile(kc, 1024 if mode != "tn" else 1408, LANES)
    nk = kc // tk
    if mode == "tn":
        a_spec = pl.BlockSpec((tk, tm), lambda i, j, k: (k, i))
    else:
        a_spec = pl.BlockSpec((tm, tk), lambda i, j, k: (i, k))
    b_arr = b
    if isinstance(b, Gathered):
        b_arr, layer = b.arr, b.layer
        sr, sc = b.arr.shape[2:]
        br, bc = (tk, tn) if mode == "nn" else (tn, tk)
        assert mode in ("nn", "nt") and sr % br == 0 and sc % bc == 0

        def b_map(i, j, k):
            r, c = (k, j) if mode == "nn" else (j, k)
            if b.kind == "col":
                return ((c * bc) // sc, layer, r, ((c * bc) % sc) // bc)
            return ((r * br) // sr, layer, ((r * br) % sr) // br, c)

        b_spec = pl.BlockSpec((None, None, br, bc), b_map)
    elif mode == "nt":
        b_spec = pl.BlockSpec((tn, tk), lambda i, j, k: (j, k))
    else:
        b_spec = pl.BlockSpec((tk, tn), lambda i, j, k: (k, j))
    dims = _DIMS[mode]
    if out_chip_major:
        ns = n // N_CHIPS
        o_spec = pl.BlockSpec((None, tm, tn), lambda i, j, k: ((j * tn) // ns, i, ((j * tn) % ns) // tn))
        o_shape = jax.ShapeDtypeStruct((N_CHIPS, m, ns), out_dtype)
    else:
        o_spec = pl.BlockSpec((tm, tn), lambda i, j, k: (i, j))
        o_shape = jax.ShapeDtypeStruct((m, n), out_dtype)
    extra = list(extra) + ([add] if add is not None else [])
    if add is not None:
        post = lambda v, x: (v + x,)
    vecs = list(vecs)
    nx = len(extra) + len(vecs)
    out_dtypes = out_dtypes or [out_dtype]
    no = len(out_dtypes)

    def body(a_ref, b_ref, *rest):
        o_refs, acc = rest[nx:nx + no], rest[nx + no:]
        p = lax.dot_general(a_ref[...].astype(BF16), b_ref[...].astype(BF16), dims, preferred_element_type=F32)

        def emit(v):
            res = post(v, *[r[...] for r in rest[:nx]]) if post else (v,)
            for o_ref, r in zip(o_refs, res):
                o_ref[...] = r.astype(o_ref.dtype)

        if nk == 1:
            emit(p)
        else:
            k = pl.program_id(2)

            @pl.when(k == 0)
            def _():
                acc[0][...] = p

            @pl.when(k > 0)
            def _():
                acc[0][...] += p

            @pl.when(k == nk - 1)
            def _():
                emit(acc[0][...])

    res = pl.pallas_call(
        body, name=name, grid=(m // tm, n // tn, nk),
        in_specs=[a_spec, b_spec] + [o_spec] * len(extra) + [pl.BlockSpec((1, tn), lambda i, j, k: (0, j))] * len(vecs),
        out_specs=[o_spec] * no,
        out_shape=[jax.ShapeDtypeStruct(o_shape.shape, dt) for dt in out_dtypes],
        scratch_shapes=[pltpu.VMEM((tm, tn), F32)] if nk > 1 else [],
        compiler_params=_params(("parallel", "parallel", "arbitrary")),
    )(a, b_arr, *extra, *vecs)
    return res[0] if no == 1 else res


def _rowarg(r):
    return r if isinstance(r, tuple) else (r, r.shape[1], 0)


def _rowspec(r, tr, ncol):
    _, w, cb = r
    if ncol > 1:
        return pl.BlockSpec((tr, w // ncol), lambda j, i: (i, j))
    return pl.BlockSpec((tr, w), lambda j, i: (i, cb))


def _rowwise(name, f, rows, params, outs, tr=None, ncol=1):
    rows = [_rowarg(r) for r in rows]
    t = rows[0][0].shape[0]
    tr = tr or _tile(t, 528)
    nr, npm = len(rows), len(params)

    def body(*refs):
        vals = [r[...] for r in refs[:nr]] + [(p[0] if ncol > 1 else p[...]) for p in refs[nr:nr + npm]]
        res = f(pl.program_id(1) * tr, *vals)
        for o_ref, v in zip(refs[nr + npm:], res):
            o_ref[...] = v.astype(o_ref.dtype)

    def pspec(p):
        if ncol > 1:
            return pl.BlockSpec((1,) + p.shape[1:], lambda j, i, n=p.ndim: (j,) + (0,) * (n - 1))
        return _full_spec(p.shape, 2)

    return pl.pallas_call(
        body, name=name, grid=(ncol, t // tr),
        in_specs=[_rowspec(r, tr, ncol) for r in rows] + [pspec(p) for p in params],
        out_specs=[pl.BlockSpec((tr, w // ncol), lambda j, i: (i, j)) for w, _ in outs],
        out_shape=[jax.ShapeDtypeStruct((t, w), dt) for w, dt in outs],
        compiler_params=_params(("parallel", "parallel")),
    )(*[r[0] for r in rows], *params)


def _rowwise_vjp(name, f, rows, params, cts, tr=None, ncol=1, row_dtypes=None):
    rows = [_rowarg(r) for r in rows]
    cts = [_rowarg(c) for c in cts]
    t = rows[0][0].shape[0]
    tr = tr or _tile(t, 528)
    nr, npm, nc = len(rows), len(params), len(cts)
    row_dtypes = row_dtypes or [F32] * nr

    def body(*refs):
        i = pl.program_id(1)
        vals = [r[...] for r in refs[:nr]] + [(p[0] if ncol > 1 else p[...]) for p in refs[nr:nr + npm]]
        ct = tuple(c[...].astype(F32) for c in refs[nr + npm:nr + npm + nc])
        _, vjp = jax.vjp(lambda *a: tuple(f(i * tr, *a)), *vals)
        g = vjp(ct)
        outs = refs[nr + npm + nc:]
        for o_ref, v in zip(outs[:nr], g[:nr]):
            o_ref[...] = v.astype(o_ref.dtype)
        pg = [(v[None] if ncol > 1 else v) for v in g[nr:]]

        @pl.when(i == 0)
        def _():
            for o_ref, v in zip(outs[nr:], pg):
                o_ref[...] = v

        @pl.when(i > 0)
        def _():
            for o_ref, v in zip(outs[nr:], pg):
                o_ref[...] += v

    def pspec(p):
        if ncol > 1:
            return pl.BlockSpec((1,) + p.shape[1:], lambda j, i, n=p.ndim: (j,) + (0,) * (n - 1))
        return _full_spec(p.shape, 2)

    res = pl.pallas_call(
        body, name=name, grid=(ncol, t // tr),
        in_specs=[_rowspec(r, tr, ncol) for r in rows] + [pspec(p) for p in params] + [_rowspec(c, tr, ncol) for c in cts],
        out_specs=[pl.BlockSpec((tr, w // ncol), lambda j, i: (i, j)) for _, w, _ in rows] + [pspec(p) for p in params],
        out_shape=[jax.ShapeDtypeStruct((t, w), dt) for (_, w, _), dt in zip(rows, row_dtypes)]
        + [jax.ShapeDtypeStruct(p.shape, F32) for p in params],
        compiler_params=_params(("parallel", "arbitrary")),
    )(*[r[0] for r in rows], *params, *[c[0] for c in cts])
    return res[:nr], res[nr:]


def _valid(row0, tr):
    return (row0 + lax.broadcasted_iota(jnp.int32, (tr, 1), 0)) >= PAD


def _rms(x, g):
    return x * lax.rsqrt(jnp.mean(x * x, axis=-1, keepdims=True) + EPS) * g


def _softplus(x):
    return jnp.where(x < -15.0, jnp.exp(x), jnp.maximum(x, 0.0) + jnp.log(1.0 + jnp.exp(-jnp.abs(x))))


def _neg_expm1(z):
    return jnp.where(z > -0.01, -z * (1.0 + z * (0.5 + z * (1.0 / 6.0))), 1.0 - jnp.exp(z))


def _prenorm(h, g, name):
    return _rowwise(name, lambda r0, x, gg: (_rms(x, gg),), [h], [g], [(_rowarg(h)[1], BF16)])[0]


def _post_residual(m, h, g):
    assert m.shape[1] == D_MODEL
    return m, h + _rms(m, g)


def _postnorm_bwd(m, g, dh, name):
    (dm,), (dg,) = _rowwise_vjp(name, lambda r0, mm, gg: (_rms(mm, gg),), [m], [g], [dh])
    return dm, dg


def _prenorm_bwd_add(h, g, dhns, dh, name):
    t, w = h.shape
    tr = _tile(t, 528)
    nd = len(dhns)

    def body(h_ref, g_ref, *refs):
        dh_ref, o_ref, dg_ref = refs[nd:]
        i = pl.program_id(0)
        _, vjp = jax.vjp(_rms, h_ref[...], g_ref[...])
        dhn = refs[0][...].astype(F32)
        for r in refs[1:nd]:
            dhn = dhn + r[...].astype(F32)
        dx, dg = vjp(dhn)
        o_ref[...] = dh_ref[...] + dx

        @pl.when(i == 0)
        def _():
            dg_ref[...] = dg

        @pl.when(i > 0)
        def _():
            dg_ref[...] += dg

    row = pl.BlockSpec((tr, w), lambda i: (i, 0))
    return pl.pallas_call(
        body, name=name, grid=(t // tr,), in_specs=[row, _full_spec(g.shape, 1)] + [row] * (nd + 1),
        out_specs=[row, _full_spec(g.shape, 1)],
        out_shape=[jax.ShapeDtypeStruct((t, w), F32), jax.ShapeDtypeStruct(g.shape, F32)],
        compiler_params=_params(("arbitrary",)),
    )(h, g, *dhns, dh)


def _loss_and_grad(h, target, name):
    t, w = h.shape
    nb = t // CHUNK

    def body(h_ref, t_ref, s_ref, dh_ref):
        i = pl.program_id(0)

        @pl.when(i == 0)
        def _():
            s_ref[...] = jnp.zeros_like(s_ref)
            dh_ref[...] = jnp.zeros_like(dh_ref)

        @pl.when(i > 0)
        def _():
            err = h_ref[...] - t_ref[...]
            s_ref[...] += jnp.sum(err * err)
            dh_ref[...] = err * (1.0 / w)

    return pl.pallas_call(
        body, name=name, grid=(nb,),
        in_specs=[pl.BlockSpec((CHUNK, w), lambda i: (i, 0)), pl.BlockSpec((CHUNK, w), lambda i: (jnp.maximum(i - 1, 0), 0))],
        out_specs=[_full_spec((1, LANES), 1), pl.BlockSpec((CHUNK, w), lambda i: (i, 0))],
        out_shape=[jax.ShapeDtypeStruct((1, LANES), F32), jax.ShapeDtypeStruct((t, w), F32)],
        compiler_params=_params(("arbitrary",)),
    )(h, target)


def _mlp_fwd(h, p, l):
    hn = _prenorm(h, p["mlp_pre_g"][l], "mlp_prenorm")
    a, u = _mm(hn, p["w_up"][l], "nn", "mlp_up", post=lambda v: (v, jnp.square(jnp.maximum(v, 0.0))), out_dtypes=[F32, BF16])
    d, h2 = _mm(u, p["w_down"][l], "nn", "mlp_down", extra=[h], vecs=[p["mlp_post_g"][l]], post=_post_residual, out_dtypes=[F32, F32])
    return h2, (h, hn, a, u, d)


def _mlp_bwd(dh, saved, p, l, grads):
    h, hn, a, u, d = saved
    dd, grads["mlp_post_g"][l] = _postnorm_bwd(d, p["mlp_post_g"][l], dh, "mlp_postnorm_bwd")
    grads["w_down"][l] = _mm(u, dd, "tn", "mlp_down_dw")
    da = _mm(dd, p["w_down"][l], "nt", "mlp_down_dx", extra=[a], post=lambda v, x: (2.0 * jnp.maximum(x, 0.0) * v,),
             out_dtypes=[BF16])
    grads["w_up"][l] = _mm(hn, da, "tn", "mlp_up_dw", out_chip_major=True)
    dhn = _mm(da, p["w_up"][l], "nt", "mlp_up_dx")
    dh, grads["mlp_pre_g"][l] = _prenorm_bwd_add(h, p["mlp_pre_g"][l], [dhn], dh, "mlp_prenorm_bwd")
    return dh


def _dot(a, b, mode):
    return lax.dot_general(a.astype(BF16), b.astype(BF16), _DIMS[mode], preferred_element_type=F32)


@jax.custom_vjp
def _bnn(a, b):
    return _dot(a, b, "nn")


_bnn.defvjp(lambda a, b: (_dot(a, b, "nn"), (a, b)), lambda r, ct: (_dot(ct, r[1], "nt"), _dot(r[0], ct, "tn")))


@jax.custom_vjp
def _bnt(a, b):
    return _dot(a, b, "nt")


_bnt.defvjp(lambda a, b: (_dot(a, b, "nt"), (a, b)), lambda r, ct: (_dot(ct, r[1], "nn"), _dot(ct, r[0], "tn")))


@jax.custom_vjp
def _btn(a, b):
    return _dot(a, b, "tn")


_btn.defvjp(lambda a, b: (_dot(a, b, "tn"), (a, b)), lambda r, ct: (_dot(r[1], ct, "nt"), _dot(r[0], ct, "nn")))


CONV_K = 4
HALO = 8


def _conv_fwd(x, w, b, name, cw, c0=0):
    t, c = x.shape[0], w.shape[1]
    tr = _tile(t, 528)
    hb = tr // HALO

    def body(x_ref, halo_ref, w_ref, b_ref, o_ref, ext):
        i = pl.program_id(1)
        ext[pl.ds(0, HALO), :] = jnp.where(i > 0, halo_ref[...], 0.0)
        ext[pl.ds(HALO, tr), :] = x_ref[...]
        acc = jnp.broadcast_to(b_ref[...], (tr, cw))
        for k in range(CONV_K):
            acc = acc + w_ref[pl.ds(k, 1), :] * ext[pl.ds(HALO - (CONV_K - 1) + k, tr), :]
        o_ref[...] = acc

    return pl.pallas_call(
        body, name=name, grid=(c // cw, t // tr),
        in_specs=[pl.BlockSpec((tr, cw), lambda j, i: (i, c0 + j)),
                  pl.BlockSpec((HALO, cw), lambda j, i: (jnp.maximum(i * hb - 1, 0), c0 + j)),
                  pl.BlockSpec((CONV_K, cw), lambda j, i: (0, j)), pl.BlockSpec((1, cw), lambda j, i: (0, j))],
        out_specs=pl.BlockSpec((tr, cw), lambda j, i: (i, j)),
        out_shape=jax.ShapeDtypeStruct((t, c), F32),
        scratch_shapes=[pltpu.VMEM((tr + HALO, cw), F32)],
        compiler_params=_params(("parallel", "parallel")),
    )(x, x, w, b)


def _conv_bwd(x, w, dy, name, cw, c0=0):
    t, c = x.shape[0], w.shape[1]
    tr = _tile(t, 528)
    hb = tr // HALO
    nb = t // tr

    def body(x_ref, xh_ref, w_ref, dy_ref, dyh_ref, dx_ref, dw_ref, db_ref, xe, de):
        c = cw
        i = pl.program_id(1)
        xe[pl.ds(0, HALO), :] = jnp.where(i > 0, xh_ref[...], 0.0)
        xe[pl.ds(HALO, tr), :] = x_ref[...]
        de[pl.ds(0, tr), :] = dy_ref[...]
        de[pl.ds(tr, HALO), :] = jnp.where(i < nb - 1, dyh_ref[...], 0.0)
        dy = dy_ref[...]
        acc = jnp.zeros((tr, c), F32)
        dw = jnp.zeros((CONV_K, c), F32)
        rows = lax.broadcasted_iota(jnp.int32, (CONV_K, 1), 0)
        for k in range(CONV_K):
            acc = acc + w_ref[pl.ds(k, 1), :] * de[pl.ds(CONV_K - 1 - k, tr), :]
            dwk = jnp.sum(dy * xe[pl.ds(HALO - (CONV_K - 1) + k, tr), :], axis=0, keepdims=True)
            dw = dw + jnp.where(rows == k, dwk, 0.0)
        dx_ref[...] = jnp.where(_valid(i * tr, tr), acc, 0.0)
        db = jnp.sum(dy, axis=0, keepdims=True)

        @pl.when(i == 0)
        def _():
            dw_ref[...] = dw
            db_ref[...] = db

        @pl.when(i > 0)
        def _():
            dw_ref[...] += dw
            db_ref[...] += db

    row = pl.BlockSpec((tr, cw), lambda j, i: (i, j))
    return pl.pallas_call(
        body, name=name, grid=(c // cw, nb),
        in_specs=[pl.BlockSpec((tr, cw), lambda j, i: (i, c0 + j)),
                  pl.BlockSpec((HALO, cw), lambda j, i: (jnp.maximum(i * hb - 1, 0), c0 + j)),
                  pl.BlockSpec((CONV_K, cw), lambda j, i: (0, j)),
                  row, pl.BlockSpec((HALO, cw), lambda j, i: (jnp.minimum((i + 1) * hb, t // HALO - 1), j))],
        out_specs=[row, pl.BlockSpec((CONV_K, cw), lambda j, i: (0, j)), pl.BlockSpec((1, cw), lambda j, i: (0, j))],
        out_shape=[jax.ShapeDtypeStruct((t, c), F32), jax.ShapeDtypeStruct((CONV_K, c), F32), jax.ShapeDtypeStruct((1, c), F32)],
        scratch_shapes=[pltpu.VMEM((tr + HALO, cw), F32), pltpu.VMEM((tr + HALO, cw), F32)],
        compiler_params=_params(("parallel", "arbitrary")),
    )(x, x, w, dy, dy)


SUB = 8


def _lru_scan(a, u, name):
    t, c = a.shape
    tr = _tile(t, 528)

    def body(a_ref, u_ref, o_ref, carry):
        @pl.when(pl.program_id(0) == 0)
        def _():
            carry[...] = jnp.zeros_like(carry)

        rows = lax.broadcasted_iota(jnp.int32, (SUB, 1), 0)

        def step(k, cin):
            r = pl.multiple_of(k * SUB, SUB)
            av, uv = a_ref[pl.ds(r, SUB), :], u_ref[pl.ds(r, SUB), :]
            for d in (1, 2, 4):
                m = rows >= d
                uv = uv + av * jnp.where(m, pltpu.roll(uv, d, 0), 0.0)
                av = av * jnp.where(m, pltpu.roll(av, d, 0), 1.0)
            hv = uv + av * cin
            o_ref[pl.ds(r, SUB), :] = hv
            return jnp.broadcast_to(hv[SUB - 1:SUB, :], (SUB, c))

        carry[...] = lax.fori_loop(0, tr // SUB, step, carry[...])

    row = pl.BlockSpec((tr, c), lambda i: (i, 0))
    return pl.pallas_call(
        body, name=name, grid=(t // tr,), in_specs=[row, row], out_specs=row,
        out_shape=jax.ShapeDtypeStruct((t, c), F32), scratch_shapes=[pltpu.VMEM((SUB, c), F32)],
        compiler_params=_params(("arbitrary",)),
    )(a, u)


def _lru_scan_bwd(a, hs, dy, name):
    t, c = a.shape
    tr = _tile(t, 528)
    nb, nt = t // tr, tr // SUB

    def body(a_ref, h_ref, hh_ref, dy_ref, du_ref, da_ref, gcar, acar):
        i = pl.program_id(0)

        @pl.when(i == 0)
        def _():
            gcar[...] = jnp.zeros_like(gcar)
            acar[...] = jnp.zeros_like(acar)

        rows = lax.broadcasted_iota(jnp.int32, (SUB, 1), 0)
        hhalo = jnp.where(i < nb - 1, hh_ref[...], 0.0)

        def step(kk, car):
            gin, a_next_first = car
            k = nt - 1 - kk
            r = pl.multiple_of(k * SUB, SUB)
            av, hv, dv = a_ref[pl.ds(r, SUB), :], h_ref[pl.ds(r, SUB), :], dy_ref[pl.ds(r, SUB), :]
            rp = pl.multiple_of(jnp.maximum(k - 1, 0) * SUB, SUB)
            hp = jnp.where(k > 0, h_ref[pl.ds(rp, SUB), :], hhalo)
            cv = jnp.where(rows < SUB - 1, pltpu.roll(av, SUB - 1, 0), a_next_first)
            gv = dv
            for d in (1, 2, 4):
                m = rows < SUB - d
                gv = gv + cv * jnp.where(m, pltpu.roll(gv, SUB - d, 0), 0.0)
                cv = cv * jnp.where(m, pltpu.roll(cv, SUB - d, 0), 1.0)
            gv = gv + cv * gin
            hprev = jnp.where(rows >= 1, pltpu.roll(hv, 1, 0), jnp.broadcast_to(hp[SUB - 1:SUB, :], (SUB, c)))
            du_ref[pl.ds(r, SUB), :] = gv
            da_ref[pl.ds(r, SUB), :] = gv * hprev
            return jnp.broadcast_to(gv[0:1, :], (SUB, c)), jnp.broadcast_to(av[0:1, :], (SUB, c))

        g, af = lax.fori_loop(0, nt, step, (gcar[...], acar[...]))
        gcar[...] = g
        acar[...] = af

    hb = tr // SUB
    row = pl.BlockSpec((tr, c), lambda i: (nb - 1 - i, 0))
    halo = pl.BlockSpec((SUB, c), lambda i: (jnp.maximum((nb - 1 - i) * hb - 1, 0), 0))
    return pl.pallas_call(
        body, name=name, grid=(nb,), in_specs=[row, row, halo, row], out_specs=[row, row],
        out_shape=[jax.ShapeDtypeStruct((t, c), F32)] * 2,
        scratch_shapes=[pltpu.VMEM((SUB, c), F32), pltpu.VMEM((SUB, c), F32)],
        compiler_params=_params(("arbitrary",)),
    )(a, hs, hs, dy)


def _lru_gates(row0, xr, wa, ba, wi, bi, lam):
    r = jax.nn.sigmoid(_bnn(xr, wa) + ba)
    i = jax.nn.sigmoid(_bnn(xr, wi) + bi)
    log_a = -LRU_C * r * _softplus(-lam)
    u = jnp.sqrt(_neg_expm1(2.0 * log_a)) * (i * xr)
    return jnp.exp(log_a), jnp.where(_valid(row0, xr.shape[0]), u, 0.0)


def _lru_gate_out(row0, hs, yw):
    return (hs * jax.nn.gelu(yw),)


def _rglru_fwd(h, p, l, o):
    hn = _prenorm(h, p["mix_pre_g"][l], "rg_prenorm")
    xw = _mm(hn, p["rg_w_x"][o], "nn", "rg_in_x")
    yw = _mm(hn, p["rg_w_y"][o], "nn", "rg_in_y")
    xr = _conv_fwd(xw, p["rg_conv_w"][o], p["rg_conv_b"][o], "rg_conv", cw=LRU_WIDTH // 2)
    gp = [p["rg_w_a"][o], p["rg_b_a"][o], p["rg_w_i"][o], p["rg_b_i"][o], p["rg_lambda"][o]]
    a, u = _rowwise("rg_gates", _lru_gates, [xr], gp, [(LRU_WIDTH, F32)] * 2, ncol=LRU_BLOCKS, tr=_tile(h.shape[0], 1056))
    hs = _lru_scan(a, u, "rg_scan")
    hg = _rowwise("rg_gate_out", _lru_gate_out, [hs, yw], [], [(LRU_WIDTH, BF16)])[0]
    m, h2 = _mm(hg, p["rg_w_out"][o], "nn", "rg_out", extra=[h], vecs=[p["mix_post_g"][l]], post=_post_residual, out_dtypes=[F32, F32])
    return h2, (h, hn, xw, yw, xr, a, hs, hg, m)


def _rglru_bwd(dh, saved, p, l, o, grads):
    h, hn, xw, yw, xr, a, hs, hg, m = saved
    dm, grads["mix_post_g"][l] = _postnorm_bwd(m, p["mix_post_g"][l], dh, "rg_postnorm_bwd")
    grads["rg_w_out"][o] = _mm(hg, dm, "tn", "rg_out_dw")
    dhg = _mm(dm, p["rg_w_out"][o], "nt", "rg_out_dx")
    (dhs, dyw), _ = _rowwise_vjp("rg_gate_out_bwd", _lru_gate_out, [hs, yw], [], [dhg])
    du, da = _lru_scan_bwd(a, hs, dhs, "rg_scan_bwd")
    gp = [p["rg_w_a"][o], p["rg_b_a"][o], p["rg_w_i"][o], p["rg_b_i"][o], p["rg_lambda"][o]]
    (dxr,), gg = _rowwise_vjp("rg_gates_bwd", _lru_gates, [xr], gp, [da, du], ncol=LRU_BLOCKS, tr=_tile(h.shape[0], 1056))
    grads["rg_w_a"][o], grads["rg_b_a"][o], grads["rg_w_i"][o], grads["rg_b_i"][o], grads["rg_lambda"][o] = gg
    dxw, grads["rg_conv_w"][o], grads["rg_conv_b"][o] = _conv_bwd(xw, p["rg_conv_w"][o], dxr, "rg_conv_bwd", cw=LRU_WIDTH // 2)
    grads["rg_w_x"][o] = _mm(hn, dxw, "tn", "rg_in_x_dw")
    grads["rg_w_y"][o] = _mm(hn, dyw, "tn", "rg_in_y_dw")
    dhx = _mm(dxw, p["rg_w_x"][o], "nt", "rg_in_x_dx")
    dhy = _mm(dyw, p["rg_w_y"][o], "nt", "rg_in_y_dx")
    dh, grads["mix_pre_g"][l] = _prenorm_bwd_add(h, p["mix_pre_g"][l], [dhx, dhy], dh, "rg_prenorm_bwd")
    return dh


SSD_GW = SSD_D_INNER // SSD_GROUPS
SSD_GH = SSD_HEADS // SSD_GROUPS
XACT_B = SSD_D_INNER // SSD_STATE
XACT_C = XACT_B + SSD_GROUPS


def _hp(a, b, dims=_DIMS["nn"]):
    return lax.dot_general(a, b, dims, precision=lax.Precision.HIGHEST, preferred_element_type=F32)


def _split_dot(a, e, mode, parts):
    eb = e.astype(BF16)
    out, rest = None, a
    for _ in range(parts):
        term = rest.astype(BF16)
        rest = rest - term.astype(F32)
        if mode in ("nn", "nt"):
            prod = lax.dot_general(term, eb, _DIMS[mode], preferred_element_type=F32)
        else:
            prod = lax.dot_general(eb, term, _DIMS["nn" if mode == "left" else "tn"], preferred_element_type=F32)
        out = prod if out is None else out + prod
    return out


@jax.custom_vjp
def _select_nn(a, e):
    return _split_dot(a, e, "nn", 3)


_select_nn.defvjp(lambda a, e: (_split_dot(a, e, "nn", 3), e), lambda e, ct: (_split_dot(ct, e, "nt", 2), jnp.zeros_like(e)))


@jax.custom_vjp
def _select_left(e, a):
    return _split_dot(a, e, "left", 3)


_select_left.defvjp(lambda e, a: (_split_dot(a, e, "left", 3), e),
                    lambda e, ct: (jnp.zeros_like(e), _split_dot(ct, e, "left_t", 2)))


def _ssd_chunk(xs, bm, cm, dt, da, ht, g):
    l = CHUNK
    ri = lax.broadcasted_iota(jnp.int32, (l, l), 0)
    ci = lax.broadcasted_iota(jnp.int32, (l, l), 1)
    causal = ri >= ci
    tri = causal.astype(F32)
    hr = lax.broadcasted_iota(jnp.int32, (LANES, SSD_GW), 0)
    hc = lax.broadcasted_iota(jnp.int32, (LANES, SSD_GW), 1)
    expand = (hr == g * SSD_GH + hc // SSD_HEAD_DIM).astype(F32)
    acs = _select_left(tri, da)
    acs_t = acs.T
    acs_e = _select_nn(acs, expand)
    x = xs * _select_nn(dt, expand)
    gmat = _bnt(cm, bm)
    lane = lax.broadcasted_iota(jnp.int32, (1, LANES), 1)
    sub = lax.broadcasted_iota(jnp.int32, (LANES, 1), 0)
    colhead = lax.broadcasted_iota(jnp.int32, (1, SSD_GW), 1) // SSD_HEAD_DIM
    y = _bnn(cm, ht) * jnp.exp(acs_e)
    for k in range(SSD_GH):
        hh = g * SSD_GH + k
        col = jnp.sum(jnp.where(lane == hh, acs, 0.0), axis=1, keepdims=True)
        row = jnp.sum(jnp.where(sub == hh, acs_t, 0.0), axis=0, keepdims=True)
        decay = jnp.exp(jnp.where(causal, col - row, -1e30))
        y = y + _bnn(gmat * decay, jnp.where(colhead == k, x, 0.0))
    last = lax.broadcasted_iota(jnp.int32, (l, 1), 0) == l - 1
    a_last = jnp.sum(jnp.where(last, acs_e, 0.0), axis=0, keepdims=True)
    st = _btn(bm, x * jnp.exp(a_last - acs_e))
    return y, ht * jnp.exp(a_last) + st


def _ssd_specs(nc, rev):
    def cc(c):
        return nc - 1 - c if rev else c

    return [pl.BlockSpec((CHUNK, SSD_GW), lambda c, g: (cc(c), g)),
            pl.BlockSpec((CHUNK, SSD_STATE), lambda c, g: (cc(c), XACT_B + g)),
            pl.BlockSpec((CHUNK, SSD_STATE), lambda c, g: (cc(c), XACT_C + g)),
            pl.BlockSpec((CHUNK, LANES), lambda c, g: (cc(c), 0)),
            pl.BlockSpec((CHUNK, LANES), lambda c, g: (cc(c), 0))]


def _ssd_scan(xact, dt, da, name):
    t = xact.shape[0]
    nc = t // CHUNK

    def body(xs_ref, b_ref, c_ref, dt_ref, da_ref, y_ref, hs_ref, state):
        c, g = pl.program_id(0), pl.program_id(1)

        @pl.when(c == 0)
        def _():
            state[g] = jnp.zeros((SSD_STATE, SSD_GW), F32)

        ht = state[g]
        hs_ref[0] = ht
        y, ht2 = _ssd_chunk(xs_ref[...], b_ref[...], c_ref[...], dt_ref[...], da_ref[...], ht, g)
        y_ref[...] = y
        state[g] = ht2

    return pl.pallas_call(
        body, name=name, grid=(nc, SSD_GROUPS), in_specs=_ssd_specs(nc, False),
        out_specs=[pl.BlockSpec((CHUNK, SSD_GW), lambda c, g: (c, g)),
                   pl.BlockSpec((1, SSD_STATE, SSD_GW), lambda c, g: (c * SSD_GROUPS + g, 0, 0))],
        out_shape=[jax.ShapeDtypeStruct((t, SSD_D_INNER), F32), jax.ShapeDtypeStruct((nc * SSD_GROUPS, SSD_STATE, SSD_GW), F32)],
        scratch_shapes=[pltpu.VMEM((SSD_GROUPS, SSD_STATE, SSD_GW), F32)],
        compiler_params=_params(("arbitrary", "arbitrary")),
    )(xact, xact, xact, dt, da)


def _ssd_scan_bwd(xact, dt, da, hsave, dy, dxskip, name):
    t = xact.shape[0]
    nc = t // CHUNK

    def body(xs_ref, b_ref, c_ref, dt_ref, da_ref, hs_ref, dy_ref, sk_ref, dxs_ref, db_ref, dc_ref, ddt_ref, dda_ref, dstate):
        c, g = pl.program_id(0), pl.program_id(1)

        @pl.when(c == 0)
        def _():
            dstate[g] = jnp.zeros((SSD_STATE, SSD_GW), F32)

        _, vjp = jax.vjp(lambda *a: _ssd_chunk(*a, g), xs_ref[...], b_ref[...], c_ref[...], dt_ref[...], da_ref[...], hs_ref[0])
        dxs, dbm, dcm, ddt, dda, dht = vjp((dy_ref[...], dstate[g]))
        dxs_ref[...] = dxs + sk_ref[...]
        db_ref[...] = dbm
        dc_ref[...] = dcm
        dstate[g] = dht

        @pl.when(g == 0)
        def _():
            ddt_ref[...] = ddt
            dda_ref[...] = dda

        @pl.when(g > 0)
        def _():
            ddt_ref[...] += ddt
            dda_ref[...] += dda

    grp = pl.BlockSpec((CHUNK, SSD_GW), lambda c, g: (nc - 1 - c, g))
    st = pl.BlockSpec((CHUNK, SSD_STATE), lambda c, g: (nc - 1 - c, g))
    hd = pl.BlockSpec((CHUNK, LANES), lambda c, g: (nc - 1 - c, 0))
    return pl.pallas_call(
        body, name=name, grid=(nc, SSD_GROUPS),
        in_specs=_ssd_specs(nc, True) + [pl.BlockSpec((1, SSD_STATE, SSD_GW), lambda c, g: ((nc - 1 - c) * SSD_GROUPS + g, 0, 0)), grp, grp],
        out_specs=[grp, st, st, hd, hd],
        out_shape=[jax.ShapeDtypeStruct((t, SSD_D_INNER), F32), jax.ShapeDtypeStruct((t, SSD_GROUPS * SSD_STATE), F32),
                   jax.ShapeDtypeStruct((t, SSD_GROUPS * SSD_STATE), F32), jax.ShapeDtypeStruct((t, LANES), F32),
                   jax.ShapeDtypeStruct((t, LANES), F32)],
        scratch_shapes=[pltpu.VMEM((SSD_GROUPS, SSD_STATE, SSD_GW), F32)],
        compiler_params=_params(("arbitrary", "arbitrary")),
    )(xact, xact, xact, dt, da, hsave, dy, dxskip)


def _ssd_act(row0, xc):
    return (jnp.where(_valid(row0, xc.shape[0]), jax.nn.silu(xc), 0.0),)


def _ssd_dt(row0, dtraw, dt_bias, a_log):
    dt = jnp.where(_valid(row0, dtraw.shape[0]), _softplus(dtraw + dt_bias), 0.0)
    return dt, dt * -jnp.exp(a_log)


def _ssd_post(row0, y, xs, z, d_skip, norm_g):
    hr = lax.broadcasted_iota(jnp.int32, (LANES, SSD_D_INNER), 0)
    hc = lax.broadcasted_iota(jnp.int32, (LANES, SSD_D_INNER), 1)
    expand = (hr == hc // SSD_HEAD_DIM).astype(F32)
    d_e = jnp.sum(_hp(jnp.broadcast_to(d_skip, (SUB, LANES)), expand), axis=0, keepdims=True) * (1.0 / SUB)
    return (_rms((y + xs * d_e) * jax.nn.silu(z), norm_g),)


ROPE_LO, ROPE_MID, ROPE_HI = MLA_NOPE, MLA_NOPE + MLA_ROPE // 2, MLA_NOPE + MLA_ROPE
ATT_SCALE = (MLA_NOPE + MLA_ROPE) ** -0.5


def _slot_lane(width):
    return lax.broadcasted_iota(jnp.int32, (1, width), 1) % LANES


def _swap_halves(x):
    width = x.shape[1]
    lane = _slot_lane(width)
    sw = jnp.where(lane < ROPE_MID, pltpu.roll(x, width - MLA_ROPE // 2, 1), pltpu.roll(x, MLA_ROPE // 2, 1))
    return jnp.where((lane >= ROPE_LO) & (lane < ROPE_HI), sw, 0.0)


def _rope(x, cos, sin):
    n = x.shape[1] // LANES
    return x * jnp.tile(cos, (1, n)) + _swap_halves(x) * jnp.tile(sin, (1, n))


def _rope_t(dy, cos, sin):
    n = dy.shape[1] // LANES
    return dy * jnp.tile(cos, (1, n)) + _swap_halves(dy * jnp.tile(sin, (1, n)))


ATT_SCALE2 = ATT_SCALE * float(np.log2(np.e))
MASKED = -1e30
ATT_STRIP = 64


def _att_mask(i, j, blk):
    rowid = i * blk + lax.broadcasted_iota(jnp.int32, (blk, 1), 0)
    colid = j * blk + lax.broadcasted_iota(jnp.int32, (1, blk), 1)
    return (colid <= rowid) & (colid >= PAD)


def _att_bias(blk):
    r = jnp.arange(blk)[:, None]
    c = jnp.arange(blk)[None, :]
    zero = jnp.zeros((blk, blk), F32)
    first = jnp.where(c >= PAD, 0.0, MASKED) + zero
    diag = jnp.where(c <= r, 0.0, MASKED).astype(F32)
    return jnp.stack([zero, first, diag, jnp.minimum(first, diag), zero + MASKED])


def _att_bias_index(j, i):
    return jnp.where(j > i, 4, jnp.where(j == 0, 1, 0) + jnp.where(j == i, 2, 0))


def _key_slots(row0, kv, kr):
    width = kv.shape[1]
    return jnp.where(_slot_lane(width) < MLA_NOPE, kv, jnp.tile(kr, (1, width // LANES))), kv


def _attn_fwd(qr, km, vb, name, carried=None):
    t = qr.shape[0]
    blk = _tile(t, 384, LANES)
    nq = t // blk

    bias = _att_bias(blk)

    def body(q_ref, k_ref, v_ref, b_ref, o_ref, s0, s1, p0, p1):
        i = pl.program_id(1)
        lane = lax.broadcasted_iota(jnp.int32, (1, LANES), 1)
        qb = q_ref[...]

        def rows(j):
            return pl.ds(pl.multiple_of(jnp.clip(j, 0, i) * blk, blk), blk)

        def scores(j):
            return lax.dot_general(qb, k_ref[rows(j), :], _DIMS["nt"], preferred_element_type=F32) + b_ref[_att_bias_index(j, i)]

        def half(j, car, s_cur, s_nxt, p_cur, p_prv):
            m, l, acc, al_prev = car
            s_nxt[...] = scores(j + 1)
            acc2 = al_prev * acc + lax.dot_general(p_prv[...], v_ref[rows(j - 1), :], _DIMS["nn"], preferred_element_type=F32)
            m2 = jnp.maximum(m, jnp.max(s_cur[...], axis=1, keepdims=True))
            al = jnp.exp2((m - m2) * ATT_SCALE2)
            pm = jnp.exp2(s_cur[...] * ATT_SCALE2 - m2 * ATT_SCALE2)
            p_cur[...] = pm.astype(BF16)
            return m2, al * l + jnp.sum(pm, axis=1, keepdims=True), acc2, al

        def step(jj, car):
            car = half(2 * jj, car, s0, s1, p0, p1)
            return half(2 * jj + 1, car, s1, s0, p1, p0)

        s0[...] = scores(0)
        p1[...] = jnp.zeros((blk, blk), BF16)
        car = (jnp.full((blk, 1), MASKED, F32), jnp.zeros((blk, 1), F32), jnp.zeros((blk, LANES), F32), jnp.ones((blk, 1), F32))
        steps = i // 2 + 1
        m, l, acc, al_last = lax.fori_loop(0, steps, step, car)
        acc = al_last * acc + lax.dot_general(p1[...], v_ref[rows(2 * steps - 1), :], _DIMS["nn"], preferred_element_type=F32)
        out = jnp.where(lane >= MLA_NOPE, acc / l, m * ATT_SCALE + jnp.log(l))
        o_ref[...] = jnp.where(_valid(i * blk, blk), out, 0.0)

    seq_h = pl.BlockSpec((t, LANES), lambda h, i: (0, h))
    (o,), carried_out = _carry_call(
        body, name, (MLA_HEADS, nq),
        [pl.BlockSpec((blk, LANES), lambda h, i: (i, h)), seq_h, seq_h, _full_spec(bias.shape, 2)],
        [pl.BlockSpec((blk, LANES), lambda h, i: (i, h))], [jax.ShapeDtypeStruct((t, MLA_HEADS * LANES), F32)],
        [pltpu.VMEM((blk, blk), F32)] * 2 + [pltpu.VMEM((blk, blk), BF16)] * 2, (qr, km, vb, bias), carried)
    return o, carried_out


def _attn_bwd(qr, km, vb, o, do, name, carried=None):
    t = qr.shape[0]
    blk = _tile(t, 384, LANES)
    nq = t // blk

    bias = _att_bias(blk)
    log2e = float(np.log2(np.e))

    def body(q_ref, o_ref, do_ref, k_ref, v_ref, b_ref, dq_ref, dkv_ref, dkr_ref, s0, s1, dp0, dp1, p0, p1, ds0, ds1, dk_s, dv_s):
        h, j = pl.program_id(0), pl.program_id(1)
        lane = lax.broadcasted_iota(jnp.int32, (1, LANES), 1)

        @pl.when(j == 0)
        def _():
            dq_ref[...] = jnp.zeros_like(dq_ref)

        @pl.when((h == 0) & (j == 0))
        def _():
            dkr_ref[...] = jnp.zeros_like(dkr_ref)

        kmat, vmat = k_ref[...], v_ref[...]

        def rows(i):
            return pl.ds(pl.multiple_of(jnp.clip(i, j, nq - 1) * blk, blk), blk)

        def first_stage(i, s_buf, dp_buf):
            ic = jnp.minimum(i, nq - 1)
            s_buf[...] = lax.dot_general(q_ref[rows(ic), :], kmat, _DIMS["nt"], preferred_element_type=F32) + b_ref[_att_bias_index(j, ic)]
            dp_buf[...] = lax.dot_general(do_ref[rows(ic), :].astype(BF16), vmat, _DIMS["nt"], preferred_element_type=F32)

        def middle_stage(i, s_buf, dp_buf, p_buf, ds_buf):
            r = rows(i)
            ob, dob = o_ref[r, :], do_ref[r, :]
            delta = jnp.sum(dob * ob, axis=1, keepdims=True)
            pm = jnp.exp2(s_buf[...] * ATT_SCALE2 - ob[:, 0:1] * log2e)
            p_buf[...] = pm.astype(BF16)
            ds_buf[...] = (pm * (dp_buf[...] - delta) * ATT_SCALE).astype(BF16)

        def last_stage(i, p_buf, ds_buf):
            r = rows(i)
            dv_s[...] += lax.dot_general(p_buf[...], do_ref[r, :].astype(BF16), _DIMS["tn"], preferred_element_type=F32)
            dk_s[...] += lax.dot_general(ds_buf[...], q_ref[r, :], _DIMS["tn"], preferred_element_type=F32)
            dq_ref[r, :] += lax.dot_general(ds_buf[...], kmat, _DIMS["nn"], preferred_element_type=F32)

        n = nq - j
        dk_s[...] = jnp.zeros((blk, LANES), F32)
        dv_s[...] = jnp.zeros((blk, LANES), F32)
        first_stage(j, s0, dp0)
        first_stage(j + 1, s1, dp1)
        middle_stage(j, s0, dp0, p0, ds0)

        def step(tt, carry):
            i = j + 2 * tt + 1
            first_stage(i + 1, s0, dp0)
            last_stage(i - 1, p0, ds0)
            middle_stage(i, s1, dp1, p1, ds1)
            first_stage(i + 2, s1, dp1)
            last_stage(i, p1, ds1)
            middle_stage(i + 1, s0, dp0, p0, ds0)
            return carry

        lax.fori_loop(0, (n - 1) // 2, step, 0)

        @pl.when(n % 2 == 0)
        def _():
            last_stage(nq - 2, p0, ds0)
            middle_stage(nq - 1, s1, dp1, p1, ds1)
            last_stage(nq - 1, p1, ds1)

        @pl.when(n % 2 == 1)
        def _():
            last_stage(nq - 1, p0, ds0)

        dk = dk_s[...]
        dkv_ref[...] = jnp.where(lane < MLA_NOPE, dk, dv_s[...])
        dkr_ref[rows(j), :] += jnp.where(lane >= MLA_NOPE, dk, 0.0)

    seq_h = pl.BlockSpec((t, LANES), lambda h, j: (0, h))
    blk_h = pl.BlockSpec((blk, LANES), lambda h, j: (j, h))
    return _carry_call(
        body, name, (MLA_HEADS, nq), [seq_h, seq_h, seq_h, blk_h, blk_h, _full_spec(bias.shape, 2)],
        [seq_h, blk_h, pl.BlockSpec((t, LANES), lambda h, j: (0, 0))],
        [jax.ShapeDtypeStruct((t, MLA_HEADS * LANES), F32), jax.ShapeDtypeStruct((t, MLA_HEADS * LANES), F32),
         jax.ShapeDtypeStruct((t, LANES), F32)],
        [pltpu.VMEM((blk, blk), F32)] * 4 + [pltpu.VMEM((blk, blk), BF16)] * 4 + [pltpu.VMEM((blk, LANES), F32)] * 2,
        (qr, o, do, km, vb, bias), carried)


def _rms_rows(row0, x, g):
    return (_rms(x, g),)


def _ssdmla_fwd(h, p, l, e, cos, sin, carried=None):
    hn = _prenorm(h, p["mix_pre_g"][l], "sm_prenorm")
    proj = _mm(hn, p["w_in"][e], "nn", "sm_in")
    xc = _conv_fwd(proj, p["ssd_conv_w"][e], p["ssd_conv_b"][e], "ssd_conv", cw=SSD_GW, c0=PROJ_XBC // SSD_GW)
    xact = _rowwise("ssd_act", _ssd_act, [xc], [], [(SSD_CONV_CH, F32)])[0]
    dt, da = _rowwise("ssd_dt", _ssd_dt, [(proj, LANES, PROJ_DT // LANES)], [p["ssd_dt_bias"][e], p["ssd_a_log"][e]],
                      [(LANES, F32)] * 2)
    y, hsave = _ssd_scan(xact, dt, da, "ssd_scan")
    y_ssd = _rowwise("ssd_post", _ssd_post, [y, (xact, SSD_D_INNER, 0), (proj, SSD_D_INNER, 0)],
                     [p["ssd_d"][e], p["ssd_norm_g"][e]], [(SSD_D_INNER, BF16)])[0]
    cqn = _prenorm((proj, MLA_Q_RANK, PROJ_CQ // MLA_Q_RANK), p["mla_q_norm_g"][e], "mla_qnorm")
    ckvn = _prenorm((proj, MLA_KV_RANK, PROJ_CKV // MLA_KV_RANK), p["mla_kv_norm_g"][e], "mla_kvnorm")
    q = _mm(cqn, p["mla_w_q_up"][e], "nn", "mla_q_up")
    kv = _mm(ckvn, p["mla_w_kv_up"][e], "nn", "mla_kv_up")
    kr = _rowwise("mla_krope", lambda r0, x, c, s: (_rope(x, c, s),), [(proj, LANES, PROJ_KR // LANES), cos, sin], [],
                  [(LANES, F32)])[0]
    slots, tr = MLA_HEADS * LANES, _tile(h.shape[0], 264, 16)
    qr = _rowwise("mla_q_rope", lambda r0, a, c, s: (_rope(a, c, s),), [q, cos, sin], [], [(slots, BF16)], tr=tr)[0]
    km, vb = _rowwise("mla_key_slots", _key_slots, [kv, kr], [], [(slots, BF16)] * 2, tr=tr)
    o, carried_out = _attn_fwd(qr, km, vb, "mla_attn", carried)
    m1 = _mm(y_ssd, p["w_out_ssd"][e], "nn", "sm_out_ssd")
    m, h2 = _mm(o, p["w_out_att"][e], "nn", "sm_out_att", extra=[m1, h], vecs=[p["mix_post_g"][l]],
                post=lambda v, m1b, hb, g: _post_residual(v + m1b, hb, g), out_dtypes=[F32, F32])
    return h2, (h, hn, proj, xc, xact, dt, da, y, hsave, y_ssd, cqn, ckvn, qr, km, vb, o, m), carried_out


def _ssdmla_bwd(dh, saved, p, l, e, cos, sin, grads, carried=None):
    h, hn, proj, xc, xact, dt, da, y, hsave, y_ssd, cqn, ckvn, qr, km, vb, o, m = saved
    dm, grads["mix_post_g"][l] = _postnorm_bwd(m, p["mix_post_g"][l], dh, "sm_postnorm_bwd")
    grads["w_out_ssd"][e] = _mm(y_ssd, dm, "tn", "sm_out_ssd_dw")
    grads["w_out_att"][e] = _mm(o, dm, "tn", "sm_out_att_dw")
    dy_ssd = _mm(dm, p["w_out_ssd"][e], "nt", "sm_out_ssd_dx")
    do = _mm(dm, p["w_out_att"][e], "nt", "sm_out_att_dx")
    (dqr, dkv, dkr), carried_out = _attn_bwd(qr, km, vb, o, do, "mla_attn_bwd", carried)
    dq = _rowwise("mla_q_rope_bwd", lambda r0, a, c, s: (_rope_t(a, c, s),), [dqr, cos, sin], [], [(MLA_HEADS * LANES, F32)],
                  tr=_tile(h.shape[0], 264, 16))[0]
    dkr_raw = _rowwise("mla_krope_bwd", lambda r0, d, c, s: (_rope_t(d, c, s),), [dkr, cos, sin], [], [(LANES, F32)])[0]
    grads["mla_w_q_up"][e] = _mm(cqn, dq, "tn", "mla_q_up_dw")
    dcqn = _mm(dq, p["mla_w_q_up"][e], "nt", "mla_q_up_dx")
    (dcq,), (grads["mla_q_norm_g"][e],) = _rowwise_vjp(
        "mla_qnorm_bwd", _rms_rows, [(proj, MLA_Q_RANK, PROJ_CQ // MLA_Q_RANK)], [p["mla_q_norm_g"][e]], [dcqn])
    grads["mla_w_kv_up"][e] = _mm(ckvn, dkv, "tn", "mla_kv_up_dw")
    dckvn = _mm(dkv, p["mla_w_kv_up"][e], "nt", "mla_kv_up_dx")
    (dckv,), (grads["mla_kv_norm_g"][e],) = _rowwise_vjp(
        "mla_kvnorm_bwd", _rms_rows, [(proj, MLA_KV_RANK, PROJ_CKV // MLA_KV_RANK)], [p["mla_kv_norm_g"][e]], [dckvn])
    (dy, dxskip, dz), (grads["ssd_d"][e], grads["ssd_norm_g"][e]) = _rowwise_vjp(
        "ssd_post_bwd", _ssd_post, [y, (xact, SSD_D_INNER, 0), (proj, SSD_D_INNER, 0)], [p["ssd_d"][e], p["ssd_norm_g"][e]], [dy_ssd])
    dxs, db, dc, ddt, dda = _ssd_scan_bwd(xact, dt, da, hsave, dy, dxskip, "ssd_scan_bwd")
    dxact = jnp.concatenate([dxs, db, dc], axis=1)
    (dxc,), _ = _rowwise_vjp("ssd_act_bwd", _ssd_act, [xc], [], [dxact])
    dxbc, grads["ssd_conv_w"][e], grads["ssd_conv_b"][e] = _conv_bwd(
        proj, p["ssd_conv_w"][e], dxc, "ssd_conv_bwd", cw=SSD_GW, c0=PROJ_XBC // SSD_GW)
    (ddtraw,), (grads["ssd_dt_bias"][e], grads["ssd_a_log"][e]) = _rowwise_vjp(
        "ssd_dt_bwd", _ssd_dt, [(proj, LANES, PROJ_DT // LANES)], [p["ssd_dt_bias"][e], p["ssd_a_log"][e]], [ddt, dda])
    dproj = jnp.concatenate([dz, dxbc, ddtraw, dcq, dckv, dkr_raw], axis=1)
    grads["w_in"][e] = _mm(hn, dproj, "tn", "sm_in_dw")
    dhn = _mm(dproj, p["w_in"][e], "nt", "sm_in_dx")
    dh, grads["mix_pre_g"][l] = _prenorm_bwd_add(h, p["mix_pre_g"][l], [dhn], dh, "sm_prenorm_bwd")
    return dh, carried_out


GAINS = ("mix_pre_g", "mix_post_g", "mlp_pre_g", "mlp_post_g", "ssd_norm_g", "mla_q_norm_g", "mla_kv_norm_g", "ssd_conv_b", "rg_conv_b")
HEAD_VECS = ("ssd_dt_bias", "ssd_a_log", "ssd_d")
LRU_VECS = ("rg_b_a", "rg_b_i", "rg_lambda")
IN_DT_END = SSD_D_INNER + SSD_CONV_CH + SSD_HEADS
IN_KR = IN_DT_END + MLA_Q_RANK + MLA_KV_RANK


def _each(a, f):
    layers = a if isinstance(a, list) else [a[i] for i in range(a.shape[0])]
    return [None if x is None else f(x) for x in layers]


def _layout_params(w):
    p = {k: _each(w[k], lambda a: a[None, :]) for k in GAINS}
    for k in HEAD_VECS:
        p[k] = _each(w[k], lambda a: jnp.pad(a, (0, LANES - SSD_HEADS))[None, :])
    for k in LRU_VECS:
        p[k] = _each(w[k], lambda a: a.reshape(LRU_BLOCKS, 1, LRU_BLOCK))
    for k in ("w_up", "w_down", "mla_w_kv_up", "rg_w_x", "rg_w_y", "rg_w_out"):
        p[k] = _each(w[k], lambda a: a if isinstance(a, Gathered) else a.astype(BF16))
    for k in ("ssd_conv_w", "rg_conv_w", "rg_w_a", "rg_w_i"):
        p[k] = _each(w[k], lambda a: a)

    def w_in(a):
        def zcols(n):
            return jnp.zeros((a.shape[0], n), a.dtype)

        return jnp.concatenate([a[:, :IN_DT_END], zcols(PROJ_CQ - IN_DT_END), a[:, IN_DT_END:IN_KR], zcols(ROPE_LO),
                                a[:, IN_KR:], zcols(LANES - ROPE_HI)], axis=1).astype(BF16)

    def q_up(a):
        a = a.reshape(MLA_Q_RANK, MLA_HEADS, MLA_NOPE + MLA_ROPE)
        return jnp.pad(a, ((0, 0), (0, 0), (0, LANES - MLA_NOPE - MLA_ROPE))).reshape(MLA_Q_RANK, MLA_HEADS * LANES).astype(BF16)

    def out_att(a):
        a = a[SSD_D_INNER:].reshape(MLA_HEADS, MLA_V, D_MODEL)
        return jnp.pad(a, ((0, 0), (LANES - MLA_V, 0), (0, 0))).reshape(MLA_HEADS * LANES, D_MODEL).astype(BF16)

    p["w_in"] = _each(w["w_in"], w_in)
    p["mla_w_q_up"] = _each(w["mla_w_q_up"], q_up)
    p["w_out_ssd"] = _each(w["w_out_ab"], lambda a: a[:SSD_D_INNER].astype(BF16))
    p["w_out_att"] = _each(w["w_out_ab"], out_att)
    return p


def _rope_tables(t):
    pos = (jnp.arange(t) - PAD).astype(F32)
    inv = ROPE_BASE ** (-jnp.arange(0, MLA_ROPE, 2, dtype=F32) / MLA_ROPE)
    ang = pos[:, None] * inv[None, :]
    c, s = jnp.cos(ang), jnp.sin(ang)
    one, zero = jnp.ones((t, MLA_NOPE), F32), jnp.zeros((t, MLA_NOPE), F32)
    tail = LANES - ROPE_HI
    return (jnp.concatenate([one, c, c, one[:, :tail]], axis=1), jnp.concatenate([zero, -s, s, zero[:, :tail]], axis=1))


GRAD_KEYS = GAINS + HEAD_VECS + LRU_VECS + ("w_up", "w_down", "mla_w_kv_up", "rg_w_x", "rg_w_y", "rg_w_out", "ssd_conv_w",
                                            "rg_conv_w", "rg_w_a", "rg_w_i", "w_in", "mla_w_q_up", "w_out_ssd", "w_out_att")


def _device_step(x, meta, target, p, hooks=None):
    t = PAD + N_META + x.shape[0]
    cos, sin = _rope_tables(t)
    h = jnp.concatenate([jnp.zeros((PAD, D_MODEL), F32), meta, x], axis=0)
    n_even, n_odd = (DEPTH + 1) // 2, DEPTH // 2
    saved = []
    for l in range(DEPTH):
        if l % 2 == 0:
            carried = hooks.forward_exchange() if hooks and l == 0 else None
            h, sm, arrived = _ssdmla_fwd(h, p, l, l // 2, cos, sin, carried)
            if carried is not None:
                p = hooks.after_forward_exchange(arrived)
        else:
            h, sm = _rglru_fwd(h, p, l, l // 2)
        h, sp = _mlp_fwd(h, p, l)
        saved.append((sm, sp))
    sq, dh = _loss_and_grad(h, target, "loss")
    per_layer = {"mix_pre_g": DEPTH, "mix_post_g": DEPTH, "mlp_pre_g": DEPTH, "mlp_post_g": DEPTH, "w_up": DEPTH, "w_down": DEPTH}
    grads = {k: [None] * per_layer.get(k, n_odd if k.startswith("rg_") else n_even) for k in GRAD_KEYS}
    for l in reversed(range(DEPTH)):
        sm, sp = saved[l]
        dh = _mlp_bwd(dh, sp, p, l, grads)
        if l % 2 == 0:
            carried = hooks.backward_exchange(grads, l) if hooks else None
            dh, arrived = _ssdmla_bwd(dh, sm, p, l, l // 2, cos, sin, grads, carried)
            if carried is not None:
                hooks.after_backward_exchange(arrived, l)
        else:
            dh = _rglru_bwd(dh, sm, p, l, l // 2, grads)
    return sq, dh, grads


MESH = pl.DeviceIdType.MESH
ANY = pl.BlockSpec(memory_space=pl.ANY)


def _mesh_pos():
    return lax.axis_index("x"), lax.axis_index("y"), lax.axis_index("c")


def _other_chips(x, y):
    return [(1 - x, y), (x, 1 - y), (1 - x, 1 - y)]


def _remote(src, dst, send_sems, recv_sems, k, to):
    return pltpu.make_async_remote_copy(src_ref=src, dst_ref=dst, send_sem=send_sems.at[k], recv_sem=recv_sems.at[k],
                                        device_id=to, device_id_type=MESH)


class Exchange:
    def __init__(self, ins, outs, aliases, n_sems, plan):
        self.ins, self.outs, self.aliases, self.n_sems, self.plan = list(ins), list(outs), dict(aliases), n_sems, plan


def _sems(n):
    return [pltpu.SemaphoreType.DMA((n,)), pltpu.SemaphoreType.DMA((n,))]


def _run_exchange(name, ex):
    ni, no = len(ex.ins), len(ex.outs)

    def body(*refs):
        sends = ex.plan(refs[:ni], refs[ni:ni + no], refs[-2], refs[-1], False)
        for cp in sends:
            cp.start()
        for cp in ex.plan(refs[:ni], refs[ni:ni + no], refs[-2], refs[-1], True):
            cp.wait_recv()
        for cp in sends:
            cp.wait_send()

    return pl.pallas_call(body, name=name, in_specs=[ANY] * ni, out_specs=[ANY] * no, out_shape=ex.outs,
                          input_output_aliases=ex.aliases, scratch_shapes=_sems(ex.n_sems))(*ex.ins)


def _carry_call(body, name, grid, in_specs, out_specs, out_shape, scratch_shapes, args, ex):
    if ex is None:
        res = pl.pallas_call(body, name=name, grid=grid, in_specs=in_specs, out_specs=out_specs, out_shape=out_shape,
                             scratch_shapes=scratch_shapes, compiler_params=_params(("arbitrary",) * len(grid)))(*args)
        return res, None
    ni, no, ns, xi, xo = len(in_specs), len(out_specs), len(scratch_shapes), len(ex.ins), len(ex.outs)

    def wrapped(*refs):
        ins, xin = refs[:ni], refs[ni:ni + xi]
        outs, xout = refs[ni + xi:ni + xi + no], refs[ni + xi + no:ni + xi + no + xo]
        scr, send_sems, recv_sems = refs[ni + xi + no + xo:-2], refs[-2], refs[-1]
        pid = [pl.program_id(d) for d in range(len(grid))]
        first = functools.reduce(jnp.logical_and, [p == 0 for p in pid])
        last = functools.reduce(jnp.logical_and, [p == g - 1 for p, g in zip(pid, grid)])

        @pl.when(first)
        def _():
            for cp in ex.plan(xin, xout, send_sems, recv_sems, False):
                cp.start()

        body(*ins, *outs, *scr)

        @pl.when(last)
        def _():
            for cp in ex.plan(xin, xout, send_sems, recv_sems, True):
                cp.wait_recv()
            for cp in ex.plan(xin, xout, send_sems, recv_sems, False):
                cp.wait_send()

    res = pl.pallas_call(
        wrapped, name=name, grid=grid, in_specs=list(in_specs) + [ANY] * xi, out_specs=list(out_specs) + [ANY] * xo,
        out_shape=list(out_shape) + ex.outs, scratch_shapes=list(scratch_shapes) + _sems(ex.n_sems),
        input_output_aliases={ni + i: no + o for i, o in ex.aliases.items()},
        compiler_params=_params(("arbitrary",) * len(grid)))(*args, *ex.ins)
    return res[:no], res[no:]


def _gather_ici(srcs, bufs, ranges):
    n = len(srcs)

    def plan(in_refs, out_refs, ss, rs, arrivals):
        x, y, c = _mesh_pos()
        cps = []
        for t, (l0, nl) in enumerate(ranges):
            if nl:
                s, o, lr = in_refs[t], out_refs[t], pl.ds(l0, nl)
                for j, (cx, cy) in enumerate(_other_chips(x, y)):
                    chip = 2 * cx + cy if arrivals else 2 * x + y
                    cps.append(_remote(s.at[lr, c], o.at[chip, lr, c], ss, rs, (N_CHIPS - 1) * t + j, (cx, cy, c)))
        return cps

    outs = [jax.ShapeDtypeStruct((N_CHIPS,) + s.shape, s.dtype) for s in srcs]
    if bufs is None:
        return Exchange(srcs, outs, {}, (N_CHIPS - 1) * n, plan)
    return Exchange(list(srcs) + list(bufs), outs, {n + t: t for t in range(n)}, (N_CHIPS - 1) * n, plan)


def _gather_d2d(srcs, bufs, ranges):
    n = len(srcs)

    def plan(in_refs, out_refs, ss, rs, arrivals):
        x, y, c = _mesh_pos()
        sib, me = (x, y, 1 - c), 2 * x + y
        cps = []
        for t, (l0, nl) in enumerate(ranges):
            if nl:
                s, o, lr = in_refs[t], out_refs[t], pl.ds(l0, nl)
                for j, (cx, cy) in enumerate(_other_chips(x, y)):
                    slot = o.at[2 * cx + cy, lr, c]
                    cps.append(_remote(slot, o.at[2 * cx + cy, lr, 1 - c] if arrivals else slot, ss, rs, N_CHIPS * t + j, sib))
                cps.append(_remote(s.at[lr], o.at[me, lr], ss, rs, N_CHIPS * t + N_CHIPS - 1, sib))
        return cps

    outs = [jax.ShapeDtypeStruct(b.shape, b.dtype) for b in bufs]
    return Exchange(list(srcs) + list(bufs), outs, {n + t: t for t in range(n)}, N_CHIPS * n, plan)


def _gather_chips(srcs, name):
    ranges = [(0, s.shape[0]) for s in srcs]
    bufs = _run_exchange(name + "_ici", _gather_ici(srcs, None, ranges))
    return _run_exchange(name + "_d2d", _gather_d2d(srcs, bufs, ranges))


def _pair_exchange(gs):
    def plan(in_refs, out_refs, ss, rs, arrivals):
        x, y, c = _mesh_pos()
        return [_remote(g.at[pl.ds(0, N_CHIPS), 1 - c], o, ss, rs, t, (x, y, 1 - c)) for t, (g, o) in enumerate(zip(in_refs, out_refs))]

    return Exchange(gs, [jax.ShapeDtypeStruct((g.shape[0],) + g.shape[2:], g.dtype) for g in gs], {}, len(gs), plan)


def _chip_exchange(ps, slots, qs, q_shapes):
    n = len(ps)
    kept = [g for g, q in enumerate(qs) if q is not None]

    def plan(in_refs, out_refs, ss, rs, arrivals):
        x, y, c = _mesh_pos()
        return [_remote(in_refs[t].at[2 * cx + cy], out_refs[g].at[j, li], ss, rs, (N_CHIPS - 1) * t + j, (cx, cy, c))
                for t, (g, li) in enumerate(slots) for j, (cx, cy) in enumerate(_other_chips(x, y))]

    return Exchange(list(ps) + [qs[g] for g in kept], q_shapes, {n + i: g for i, g in enumerate(kept)}, (N_CHIPS - 1) * n, plan)


def _pair_share(fs):
    def plan(in_refs, out_refs, ss, rs, arrivals):
        x, y, c = _mesh_pos()
        return [_remote(o.at[pl.ds(0, o.shape[0]), c], o.at[pl.ds(0, o.shape[0]), 1 - c if arrivals else c], ss, rs, t, (x, y, 1 - c))
                for t, o in enumerate(out_refs)]

    return Exchange(fs, [jax.ShapeDtypeStruct(f.shape, f.dtype) for f in fs], {t: t for t in range(len(fs))}, len(fs), plan)


SUM_BLOCK = 512 * 1024


def _sum_pair(g, ra, c, name):
    n, _, h, w = g.shape
    tr = _tile(h, max(16, SUM_BLOCK // w), 16)

    def body(c_ref, g_ref, r_ref, o_ref):
        o_ref[...] = (g_ref[0] + r_ref[...]).astype(o_ref.dtype)

    return pl.pallas_call(
        body, name=name,
        grid_spec=pltpu.PrefetchScalarGridSpec(
            num_scalar_prefetch=1, grid=(n, h // tr),
            in_specs=[pl.BlockSpec((1, 1, tr, w), lambda s, i, cr: (s, cr[0], i, 0)), pl.BlockSpec((1, tr, w), lambda s, i, cr: (s, i, 0))],
            out_specs=pl.BlockSpec((1, tr, w), lambda s, i, cr: (s, i, 0))),
        out_shape=jax.ShapeDtypeStruct((n, h, w), BF16),
        compiler_params=_params(("parallel", "parallel")),
    )(c.reshape(1).astype(jnp.int32), g, ra)


def _sum_chips(ps, q, pos, name):
    nc, nl, h, w = q.shape
    tr = _tile(h, max(16, SUM_BLOCK // (w * nl)), 16)

    def body(x_ref, y_ref, c_ref, *refs):
        q_ref, o_ref = refs[nl], refs[nl + 1]
        for l in range(nl):
            acc = refs[l][0].astype(F32)
            for j in range(nc):
                acc = acc + q_ref[j, l].astype(F32)
            o_ref[l] = acc

    return pl.pallas_call(
        body, name=name,
        grid_spec=pltpu.PrefetchScalarGridSpec(
            num_scalar_prefetch=3, grid=(h // tr,),
            in_specs=[pl.BlockSpec((1, tr, w), lambda i, x, y, c: (2 * x[0] + y[0], i, 0))] * nl
            + [pl.BlockSpec((nc, nl, tr, w), lambda i, x, y, c: (0, 0, i, 0))],
            out_specs=pl.BlockSpec((nl, None, tr, w), lambda i, x, y, c: (0, c[0], i, 0))),
        out_shape=jax.ShapeDtypeStruct((nl, 2, h, w), F32),
        compiler_params=_params(("parallel",)),
    )(*pos, *ps, q)


def _adamw(g, w, m, v, name):
    def f(r0, gg, ww, mm, vv):
        m2 = ADAM_B1 * mm + (1.0 - ADAM_B1) * gg
        v2 = ADAM_B2 * vv + (1.0 - ADAM_B2) * jnp.square(gg)
        m_hat = m2 / (1.0 - ADAM_B1 ** ADAM_STEP)
        v_hat = v2 / (1.0 - ADAM_B2 ** ADAM_STEP)
        return gg, -ADAM_LR * (m_hat / (jnp.sqrt(v_hat) + ADAM_EPS) + ADAM_WD * ww), m2, v2

    return _rowwise(name, f, [g, w, m, v], [], [(g.shape[1], F32)] * 4, tr=_tile(g.shape[0], 512))


WEIGHTS = (
    ("meta_tokens", (N_META, D_MODEL), 1), ("mix_pre_g", (DEPTH, D_MODEL), None), ("mix_post_g", (DEPTH, D_MODEL), None),
    ("mlp_pre_g", (DEPTH, D_MODEL), None), ("mlp_post_g", (DEPTH, D_MODEL), None), ("w_up", (DEPTH, D_MODEL, D_FF), 2),
    ("w_down", (DEPTH, D_FF, D_MODEL), 1), ("w_in", (2, D_MODEL, 3248), 2), ("ssd_conv_w", (2, CONV_K, SSD_CONV_CH), 2),
    ("ssd_conv_b", (2, SSD_CONV_CH), None), ("ssd_dt_bias", (2, SSD_HEADS), None), ("ssd_a_log", (2, SSD_HEADS), None),
    ("ssd_d", (2, SSD_HEADS), None), ("ssd_norm_g", (2, SSD_D_INNER), None), ("mla_q_norm_g", (2, MLA_Q_RANK), None),
    ("mla_w_q_up", (2, MLA_Q_RANK, MLA_HEADS * (MLA_NOPE + MLA_ROPE)), 2), ("mla_kv_norm_g", (2, MLA_KV_RANK), None),
    ("mla_w_kv_up", (2, MLA_KV_RANK, MLA_HEADS * (MLA_NOPE + MLA_V)), 2), ("w_out_ab", (2, SSD_D_INNER + MLA_HEADS * MLA_V, D_MODEL), 1),
    ("rg_w_x", (2, D_MODEL, LRU_WIDTH), 2), ("rg_w_y", (2, D_MODEL, LRU_WIDTH), 2), ("rg_conv_w", (2, CONV_K, LRU_WIDTH), 2),
    ("rg_conv_b", (2, LRU_WIDTH), 1), ("rg_w_a", (2, LRU_BLOCKS, LRU_BLOCK, LRU_BLOCK), None), ("rg_b_a", (2, LRU_WIDTH), 1),
    ("rg_w_i", (2, LRU_BLOCKS, LRU_BLOCK, LRU_BLOCK), None), ("rg_b_i", (2, LRU_WIDTH), 1), ("rg_lambda", (2, LRU_WIDTH), 1),
    ("rg_w_out", (2, LRU_WIDTH, D_MODEL), 1),
)
BIG = {"w_up": "col", "w_down": "row", "w_in": "col", "mla_w_q_up": "col", "mla_w_kv_up": "col", "w_out_ab": "row",
       "rg_w_x": "col", "rg_w_y": "col", "rg_w_out": "row"}
DIRECT = ("w_up", "w_down")
FLAT_QUANTUM = 2 * 16 * LANES
TABLE = {name: (shape, d) for name, shape, d in WEIGHTS}
SMALL_SHARDED = tuple(name for name, _, d in WEIGHTS if d is not None and name not in BIG)
REPLICATED = tuple(name for name, _, d in WEIGHTS if d is None)


def _chips_to_full(a, kind):
    if kind == "col":
        return jnp.moveaxis(a, 0, 2).reshape(a.shape[1], a.shape[2], -1)
    return jnp.moveaxis(a, 0, 1).reshape(a.shape[1], -1, a.shape[3])


def _full_to_chips(g, kind):
    if kind == "col":
        return jnp.moveaxis(g.reshape(g.shape[0], N_CHIPS, -1), 1, 0)
    return g.reshape(N_CHIPS, -1, g.shape[1])


def _chips_to_full_1(pc, kind):
    return jnp.moveaxis(pc, 0, 1).reshape(pc.shape[1], -1) if kind == "col" else pc.reshape(-1, pc.shape[2])


def _shard_shape(shape, d):
    return shape[:d] + (shape[d] // N_CHIPS,) + shape[d + 1:]


def _shard_major(full, d):
    s = full.shape
    return jnp.moveaxis(full.reshape(s[:d] + (N_CHIPS, s[d] // N_CHIPS) + s[d + 1:]), d, 0).reshape(N_CHIPS, -1)


def _from_shard_major(a, shape, d):
    ss = _shard_shape(shape, d)
    return jnp.moveaxis(a.reshape((N_CHIPS,) + ss), 0, d).reshape(shape)


def _pad_cols(a, quantum):
    n = a.shape[-1]
    return jnp.pad(a, [(0, 0)] * (a.ndim - 1) + [(0, -n % quantum)])


def _big_pieces(g):
    def w_in(a):
        return jnp.concatenate([a[:, :IN_DT_END], a[:, PROJ_CQ:PROJ_KR], a[:, PROJ_KR + ROPE_LO:PROJ_KR + ROPE_HI]], axis=1)

    def q_up(a):
        return a.reshape(MLA_Q_RANK, MLA_HEADS, LANES)[:, :, :MLA_NOPE + MLA_ROPE].reshape(MLA_Q_RANK, -1)

    def out_ab(sa):
        s, a = sa
        return jnp.concatenate([s, a.reshape(MLA_HEADS, LANES, D_MODEL)[:, LANES - MLA_V:, :].reshape(-1, D_MODEL)], axis=0)

    ident = lambda a: a
    full = {"w_down": _each(g["w_down"], ident), "w_in": _each(g["w_in"], w_in), "mla_w_q_up": _each(g["mla_w_q_up"], q_up),
            "mla_w_kv_up": _each(g["mla_w_kv_up"], ident),
            "w_out_ab": _each([None if s is None or a is None else (s, a) for s, a in zip(g["w_out_ssd"], g["w_out_att"])], out_ab),
            "rg_w_x": _each(g["rg_w_x"], ident), "rg_w_y": _each(g["rg_w_y"], ident), "rg_w_out": _each(g["rg_w_out"], ident)}
    return {name: (list(g[name]) if name == "w_up" else _each(full[name], lambda a, k=BIG[name]: _full_to_chips(a, k))) for name in BIG}


def _small_grads(g, dh):
    out = {k: jnp.stack(g[k])[:, 0, :] for k in GAINS}
    for k in HEAD_VECS:
        out[k] = jnp.stack(g[k])[:, 0, :SSD_HEADS]
    for k in LRU_VECS:
        out[k] = jnp.stack(g[k]).reshape(-1, LRU_WIDTH)
    for k in ("ssd_conv_w", "rg_conv_w", "rg_w_a", "rg_w_i"):
        out[k] = jnp.stack(g[k])
    out["meta_tokens"] = dh[PAD:PAD + N_META]
    return out


def _natural_grads(g, dh):
    out = _small_grads(g, dh)
    for name, pcs in _big_pieces(g).items():
        out[name] = jnp.stack([_chips_to_full_1(pc, BIG[name]) for pc in pcs])
    return out


class StepExchanges:
    def __init__(self, w):
        self.w = w
        self.c = lax.axis_index("c")
        self.riding, self.ras = {}, {}
        small = _pad_cols(jnp.concatenate([w[n].reshape(-1) for n in SMALL_SHARDED]), FLAT_QUANTUM).reshape(1, 2, -1, LANES)
        self.srcs = [self._halves(w[n].astype(BF16)) for n in BIG] + [small]
        first = {n: (0, 1 if n in ("w_in", "mla_w_q_up", "mla_w_kv_up", "w_out_ab") else 0) for n in BIG}
        self.first = [first[n] for n in BIG] + [(0, 1)]
        self.rest = [(nl, TABLE[n][0][0] - nl) for n, (_, nl) in zip(BIG, self.first)] + [(0, 0)]
        bufs = _run_exchange("gather_first_ici", _gather_ici(self.srcs, None, self.first))
        self.bufs = _run_exchange("gather_first_d2d", _gather_d2d(self.srcs, bufs, self.first))

    @staticmethod
    def _halves(a):
        return a.reshape(a.shape[0], 2, a.shape[1] // 2, a.shape[2])

    def params(self, ranges):
        w = self.w
        full = {n: w[n] for n in REPLICATED}
        for name, buf, (l0, nl) in zip(BIG, self.bufs, ranges):
            a = buf.reshape(buf.shape[:2] + (-1, buf.shape[4]))
            have = range(l0, l0 + nl)
            if name in DIRECT:
                full[name] = [Gathered(a, BIG[name], l) if l in have else None for l in range(a.shape[1])]
            else:
                full[name] = [_chips_to_full(a[:, l:l + 1], BIG[name])[0] if l in have else None for l in range(a.shape[1])]
        got, off = self.bufs[-1].reshape(N_CHIPS, -1), 0
        for name in SMALL_SHARDED:
            shape, d = TABLE[name]
            n = int(np.prod(_shard_shape(shape, d)))
            full[name] = _from_shard_major(got[:, off:off + n], shape, d)
            off += n
        self.meta = full.pop("meta_tokens")
        return _layout_params(full)

    def forward_exchange(self):
        return _gather_ici(self.srcs, self.bufs, self.rest)

    def after_forward_exchange(self, arrived):
        self.bufs = _run_exchange("gather_rest_d2d", _gather_d2d(self.srcs, arrived, self.rest))
        return self.params([(0, TABLE[n][0][0]) for n in BIG])

    def _pair_sums(self, pieces, tag):
        keys = list(pieces)
        ras = _run_exchange("grads_pair_exchange_" + tag, _pair_exchange([pieces[k] for k in keys]))
        return {k: _sum_pair(pieces[k], ra, self.c, "grads_pair_sum") for k, ra in zip(keys, ras)}

    def _q_shapes(self):
        return [jax.ShapeDtypeStruct((N_CHIPS - 1, s.shape[0]) + s.shape[2:], BF16) for s in self.srcs[:-1]]

    def backward_exchange(self, grads, layer):
        big = _big_pieces(grads)
        pieces = {(g, l): pc.reshape(N_CHIPS, 2, pc.shape[1] // 2, pc.shape[2]) for g, name in enumerate(BIG)
                  for l, pc in enumerate(big[name]) if pc is not None and (g, l) not in self.riding}
        if layer > 0:
            self.riding = pieces
            return _pair_exchange(list(pieces.values()))
        self.ps = {k: _sum_pair(self.riding[k], ra, self.c, "grads_pair_sum") for k, ra in self.ras.items()}
        self.ps.update(self._pair_sums(pieces, "early"))
        self.early = list(self.ps)
        return _chip_exchange([self.ps[k] for k in self.early], self.early, [None] * len(BIG), self._q_shapes())

    def after_backward_exchange(self, arrived, layer):
        if layer > 0:
            self.ras = dict(zip(self.riding, arrived))
        else:
            self.qs = list(arrived)

    def finish(self, grads, dh):
        big, small = _big_pieces(grads), _small_grads(grads, dh)
        pieces = {(g, l): pc.reshape(N_CHIPS, 2, pc.shape[1] // 2, pc.shape[2])
                  for g, name in enumerate(BIG) for l, pc in enumerate(big[name]) if (g, l) not in self.ps}
        sharded = jnp.concatenate([_shard_major(small[n], TABLE[n][1]) for n in SMALL_SHARDED], axis=1)
        rep = _pad_cols(jnp.concatenate([small[n].reshape(-1) for n in REPLICATED]), N_CHIPS * FLAT_QUANTUM)
        n_sh, n_rep = sharded.shape[1], rep.shape[0] // N_CHIPS
        flat = _pad_cols(jnp.concatenate([sharded, rep.reshape(N_CHIPS, n_rep)], axis=1), FLAT_QUANTUM)
        pieces[(len(BIG), 0)] = flat.reshape(N_CHIPS, 2, -1, LANES)
        late = self._pair_sums(pieces, "late")
        self.ps.update(late)
        keys = list(late)
        small_q = jax.ShapeDtypeStruct((N_CHIPS - 1, 1) + late[(len(BIG), 0)].shape[1:], BF16)
        qs = _run_exchange("grads_chip_exchange_late",
                           _chip_exchange([late[k] for k in keys], keys, self.qs + [None], self._q_shapes() + [small_q]))
        pos = [lax.axis_index(a).reshape(1).astype(jnp.int32) for a in ("x", "y", "c")]
        sums = [_sum_chips([self.ps[(g, l)] for l in range(q.shape[1])], q, pos, "grads_chip_sum") for g, q in enumerate(qs)]
        outs = _run_exchange("grads_pair_share", _pair_share(sums))
        out = {name: o.reshape(o.shape[0], -1, o.shape[3]) for name, o in zip(BIG, outs)}
        f = outs[-1].reshape(-1)
        rep_all = _gather_chips([f[n_sh:n_sh + n_rep].reshape(1, 2, -1, LANES)], "grads_gather_replicated")[0].reshape(-1)
        off = 0
        for name in SMALL_SHARDED:
            ss = _shard_shape(*TABLE[name])
            n = int(np.prod(ss))
            out[name] = f[off:off + n].reshape(ss)
            off += n
        off = 0
        for name in REPLICATED:
            shape = TABLE[name][0]
            n = int(np.prod(shape))
            out[name] = rep_all[off:off + n].reshape(shape)
            off += n
        return out


def kernel(x, meta_tokens, mix_pre_g, mix_post_g, mlp_pre_g, mlp_post_g, w_up, w_down, w_in, ssd_conv_w, ssd_conv_b, ssd_dt_bias, ssd_a_log, ssd_d, ssd_norm_g, mla_q_norm_g, mla_w_q_up, mla_kv_norm_g, mla_w_kv_up, w_out_ab, rg_w_x, rg_w_y, rg_conv_w, rg_conv_b, rg_w_a, rg_b_a, rg_w_i, rg_b_i, rg_lambda, rg_w_out, loss_target, m_meta_tokens, m_mix_pre_g, m_mix_post_g, m_mlp_pre_g, m_mlp_post_g, m_w_up, m_w_down, m_w_in, m_ssd_conv_w, m_ssd_conv_b, m_ssd_dt_bias, m_ssd_a_log, m_ssd_d, m_ssd_norm_g, m_mla_q_norm_g, m_mla_w_q_up, m_mla_kv_norm_g, m_mla_w_kv_up, m_w_out_ab, m_rg_w_x, m_rg_w_y, m_rg_conv_w, m_rg_conv_b, m_rg_w_a, m_rg_b_a, m_rg_w_i, m_rg_b_i, m_rg_lambda, m_rg_w_out, v_meta_tokens, v_mix_pre_g, v_mix_post_g, v_mlp_pre_g, v_mlp_post_g, v_w_up, v_w_down, v_w_in, v_ssd_conv_w, v_ssd_conv_b, v_ssd_dt_bias, v_ssd_a_log, v_ssd_d, v_ssd_norm_g, v_mla_q_norm_g, v_mla_w_q_up, v_mla_kv_norm_g, v_mla_w_kv_up, v_w_out_ab, v_rg_w_x, v_rg_w_y, v_rg_conv_w, v_rg_conv_b, v_rg_w_a, v_rg_b_a, v_rg_w_i, v_rg_b_i, v_rg_lambda, v_rg_w_out):
    names = [n for n, _, _ in WEIGHTS]
    w = dict(zip(names, (meta_tokens, mix_pre_g, mix_post_g, mlp_pre_g, mlp_post_g, w_up, w_down, w_in, ssd_conv_w, ssd_conv_b, ssd_dt_bias, ssd_a_log, ssd_d, ssd_norm_g, mla_q_norm_g, mla_w_q_up, mla_kv_norm_g, mla_w_kv_up, w_out_ab, rg_w_x, rg_w_y, rg_conv_w, rg_conv_b, rg_w_a, rg_b_a, rg_w_i, rg_b_i, rg_lambda, rg_w_out)))
    m = dict(zip(names, (m_meta_tokens, m_mix_pre_g, m_mix_post_g, m_mlp_pre_g, m_mlp_post_g, m_w_up, m_w_down, m_w_in, m_ssd_conv_w, m_ssd_conv_b, m_ssd_dt_bias, m_ssd_a_log, m_ssd_d, m_ssd_norm_g, m_mla_q_norm_g, m_mla_w_q_up, m_mla_kv_norm_g, m_mla_w_kv_up, m_w_out_ab, m_rg_w_x, m_rg_w_y, m_rg_conv_w, m_rg_conv_b, m_rg_w_a, m_rg_b_a, m_rg_w_i, m_rg_b_i, m_rg_lambda, m_rg_w_out)))
    v = dict(zip(names, (v_meta_tokens, v_mix_pre_g, v_mix_post_g, v_mlp_pre_g, v_mlp_post_g, v_w_up, v_w_down, v_w_in, v_ssd_conv_w, v_ssd_conv_b, v_ssd_dt_bias, v_ssd_a_log, v_ssd_d, v_ssd_norm_g, v_mla_q_norm_g, v_mla_w_q_up, v_mla_kv_norm_g, v_mla_w_kv_up, v_w_out_ab, v_rg_w_x, v_rg_w_y, v_rg_conv_w, v_rg_conv_b, v_rg_w_a, v_rg_b_a, v_rg_w_i, v_rg_b_i, v_rg_lambda, v_rg_w_out)))
    ex = StepExchanges(w)
    p = ex.params(ex.first)
    sq, dh, grads = _device_step(x[0], ex.meta, loss_target[0], p, hooks=ex)
    loss = lax.psum(0.5 * sq[0, 0] / D_MODEL, ("x", "y", "c"))
    g = ex.finish(grads, dh)
    grad, delta, new_m, new_v = {}, {}, {}, {}
    for name in names:
        shape = g[name].shape
        two_d = (int(np.prod(shape[:-1])), shape[-1])
        res = _adamw(g[name].reshape(two_d), w[name].reshape(two_d), m[name].reshape(two_d), v[name].reshape(two_d), "adamw")
        grad[name], delta[name], new_m[name], new_v[name] = (r.reshape(shape) for r in res)
    grad_x = dh[PAD + N_META:][None]
    return (loss, grad_x, *[grad[n] for n in names], *[delta[n] for n in names], *[new_m[n] for n in names], *[new_v[n] for n in names])
```

```python
import functools

import jax
import jax.numpy as jnp
import numpy as np
from jax import lax
from jax.experimental import pallas as pl
from jax.experimental.pallas import tpu as pltpu

F32 = jnp.float32
BF16 = jnp.bfloat16

D_MODEL = 1024
DEPTH = 4
N_META = 16
CHUNK = 128
PAD = CHUNK - N_META
EPS = 1e-6
SSD_HEADS = 16
SSD_HEAD_DIM = 64
SSD_D_INNER = SSD_HEADS * SSD_HEAD_DIM
SSD_GROUPS = 2
SSD_STATE = 128
SSD_CONV_CH = SSD_D_INNER + 2 * SSD_GROUPS * SSD_STATE
MLA_HEADS = 16
MLA_NOPE = 64
MLA_ROPE = 32
MLA_V = 64
MLA_Q_RANK = 384
MLA_KV_RANK = 256
ROPE_BASE = 10000.0
LRU_WIDTH = 1280
LRU_BLOCKS = 10
LRU_BLOCK = 128
LRU_C = 8.0
D_FF = 4 * D_MODEL
ADAM_LR, ADAM_B1, ADAM_B2, ADAM_EPS, ADAM_WD, ADAM_STEP = 0.001, 0.9, 0.999, 1e-08, 0.01, 10

LANES = 128
VMEM_LIMIT = 56 * 1024 * 1024
MM_VMEM_BUDGET = 40 * 1024 * 1024
HEAD_SLOT = 128
PROJ_Z, PROJ_XBC, PROJ_DT, PROJ_CQ, PROJ_CKV, PROJ_KR = 0, 1024, 2560, 2688, 3072, 3328
PROJ_W = 3456


def _tile(n, cap, mult=8):
    for t in range(min(n, cap), 0, -1):
        if n % t == 0 and t % mult == 0:
            return t
    return n


def _params(sem):
    return pltpu.CompilerParams(dimension_semantics=sem, vmem_limit_bytes=VMEM_LIMIT)


def _full_spec(shape, ngrid):
    nd = len(shape)
    if ngrid == 1:
        return pl.BlockSpec(shape, lambda i: (0,) * nd)
    if ngrid == 2:
        return pl.BlockSpec(shape, lambda i, j: (0,) * nd)
    return pl.BlockSpec(shape, lambda i, j, k: (0,) * nd)


_DIMS = {"nn": (((1,), (0,)), ((), ())), "nt": (((1,), (1,)), ((), ())), "tn": (((0,), (0,)), ((), ()))}


class Gathered:
    def __init__(self, arr, kind, layer):
        self.arr, self.kind, self.layer = arr, kind, layer
        _, _, r, c = arr.shape
        self.shape = (r, N_CHIPS * c) if kind == "col" else (N_CHIPS * r, c)


N_CHIPS = 4


def _mm(a, b, mode, name, out_dtype=F32, add=None, out_chip_major=False, extra=(), vecs=(), post=None, out_dtypes=None):
    if mode == "nn":
        (m, kc), (_, n) = a.shape, b.shape
    elif mode == "nt":
        (m, kc), (n, _) = a.shape, b.shape
    else:
        (kc, m), (_, n) = a.shape, b.shape
    n_tile = n // N_CHIPS if out_chip_major else n
    across = isinstance(b, Gathered) and (mode, b.kind) in (("nn", "row"), ("nt", "col"))
    if mode == "tn":
        tm, tk = _tile(m, 1024, LANES), kc
        fits = [c for c in (1280, 1152, 1024, 768, 640, 512, 384, 256, 128) if n_tile % c == 0 and
                2 * kc * (tm * a.dtype.itemsize + c * b.dtype.itemsize) + 2 * tm * c * 4 <= MM_VMEM_BUDGET]
        tn = fits[0] if fits else _tile(n_tile, 1280, LANES)
        if not fits:
            tk = _tile(kc, 1408, LANES)
    else:
        tn = _tile(n_tile, 1280, LANES)
        tk = _tile(kc, 4096, LANES)
        tm = _tile(m, 1056 if tk <= 1024 else 528, 16)
    nk = kc // tk
    if mode == "tn":
        a_spec = pl.BlockSpec((tk, tm), lambda i, j, k: (k, i))
    else:
        a_spec = pl.BlockSpec((tm, tk), lambda i, j, k: (i, k))
    b_arrs = [b]
    if isinstance(b, Gathered):
        layer = b.layer
        sr, sc = b.arr.shape[2:]
        if across:
            assert nk == 1 and kc == N_CHIPS * (sr if b.kind == "row" else sc)
            b_arrs = [b.arr] * N_CHIPS
            if b.kind == "row":
                b_specs = [pl.BlockSpec((None, None, sr, tn), lambda i, j, k, s=s: (s, layer, 0, j)) for s in range(N_CHIPS)]
            else:
                b_specs = [pl.BlockSpec((None, None, tn, sc), lambda i, j, k, s=s: (s, layer, j, 0)) for s in range(N_CHIPS)]
        else:
            b_arrs = [b.arr]
            br, bc = (tk, tn) if mode == "nn" else (tn, tk)
            assert mode in ("nn", "nt") and sr % br == 0 and sc % bc == 0

            def b_map(i, j, k):
                r, c = (k, j) if mode == "nn" else (j, k)
                if b.kind == "col":
                    return ((c * bc) // sc, layer, r, ((c * bc) % sc) // bc)
                return ((r * br) // sr, layer, ((r * br) % sr) // br, c)

            b_specs = [pl.BlockSpec((None, None, br, bc), b_map)]
    elif mode == "nt":
        b_specs = [pl.BlockSpec((tn, tk), lambda i, j, k: (j, k))]
    else:
        b_specs = [pl.BlockSpec((tk, tn), lambda i, j, k: (k, j))]
    nb = len(b_arrs)
    dims = _DIMS[mode]
    if out_chip_major:
        ns = n // N_CHIPS
        o_spec = pl.BlockSpec((None, tm, tn), lambda i, j, k: ((j * tn) // ns, i, ((j * tn) % ns) // tn))
        o_shape = jax.ShapeDtypeStruct((N_CHIPS, m, ns), out_dtype)
    else:
        o_spec = pl.BlockSpec((tm, tn), lambda i, j, k: (i, j))
        o_shape = jax.ShapeDtypeStruct((m, n), out_dtype)
    extra = list(extra) + ([add] if add is not None else [])
    if add is not None:
        post = lambda v, x: (v + x,)
    vecs = list(vecs)
    nx = len(extra) + len(vecs)
    out_dtypes = out_dtypes or [out_dtype]
    no = len(out_dtypes)

    def body(a_ref, *rest):
        b_refs, rest = rest[:nb], rest[nb:]
        o_refs, acc = rest[nx:nx + no], rest[nx + no:]
        if across:
            w = kc // N_CHIPS
            p = functools.reduce(jnp.add, [
                lax.dot_general(a_ref[:, s * w:(s + 1) * w].astype(BF16), b_refs[s][...].astype(BF16), dims, preferred_element_type=F32)
                for s in range(N_CHIPS)])
        else:
            p = lax.dot_general(a_ref[...].astype(BF16), b_refs[0][...].astype(BF16), dims, preferred_element_type=F32)

        def emit(v):
            res = post(v, *[r[...] for r in rest[:nx]]) if post else (v,)
            for o_ref, r in zip(o_refs, res):
                o_ref[...] = r.astype(o_ref.dtype)

        if nk == 1:
            emit(p)
        else:
            k = pl.program_id(2)

            @pl.when(k == 0)
            def _():
                acc[0][...] = p

            @pl.when(k > 0)
            def _():
                acc[0][...] += p

            @pl.when(k == nk - 1)
            def _():
                emit(acc[0][...])

    res = pl.pallas_call(
        body, name=name, grid=(m // tm, n // tn, nk),
        in_specs=[a_spec] + b_specs + [o_spec] * len(extra) + [pl.BlockSpec((1, tn), lambda i, j, k: (0, j))] * len(vecs),
        out_specs=[o_spec] * no,
        out_shape=[jax.ShapeDtypeStruct(o_shape.shape, dt) for dt in out_dtypes],
        scratch_shapes=[pltpu.VMEM((tm, tn), F32)] if nk > 1 else [],
        compiler_params=_params(("parallel", "parallel", "arbitrary")),
    )(a, *b_arrs, *extra, *vecs)
    return res[0] if no == 1 else res


def _rowarg(r):
    return r if isinstance(r, tuple) else (r, r.shape[1], 0)


def _rowspec(r, tr, ncol):
    _, w, cb = r
    if ncol > 1:
        return pl.BlockSpec((tr, w // ncol), lambda j, i: (i, j))
    return pl.BlockSpec((tr, w), lambda j, i: (i, cb))


def _rowwise(name, f, rows, params, outs, tr=None, ncol=1):
    rows = [_rowarg(r) for r in rows]
    t = rows[0][0].shape[0]
    tr = tr or _tile(t, 528)
    nr, npm = len(rows), len(params)

    def body(*refs):
        vals = [r[...] for r in refs[:nr]] + [(p[0] if ncol > 1 else p[...]) for p in refs[nr:nr + npm]]
        res = f(pl.program_id(1) * tr, *vals)
        for o_ref, v in zip(refs[nr + npm:], res):
            o_ref[...] = v.astype(o_ref.dtype)

    def pspec(p):
        if ncol > 1:
            return pl.BlockSpec((1,) + p.shape[1:], lambda j, i, n=p.ndim: (j,) + (0,) * (n - 1))
        return _full_spec(p.shape, 2)

    return pl.pallas_call(
        body, name=name, grid=(ncol, t // tr),
        in_specs=[_rowspec(r, tr, ncol) for r in rows] + [pspec(p) for p in params],
        out_specs=[pl.BlockSpec((tr, w // ncol), lambda j, i: (i, j)) for w, _ in outs],
        out_shape=[jax.ShapeDtypeStruct((t, w), dt) for w, dt in outs],
        compiler_params=_params(("parallel", "parallel")),
    )(*[r[0] for r in rows], *params)


def _rowwise_vjp(name, f, rows, params, cts, tr=None, ncol=1, row_dtypes=None):
    rows = [_rowarg(r) for r in rows]
    cts = [_rowarg(c) for c in cts]
    t = rows[0][0].shape[0]
    tr = tr or _tile(t, 528)
    nr, npm, nc = len(rows), len(params), len(cts)
    row_dtypes = row_dtypes or [F32] * nr

    def body(*refs):
        i = pl.program_id(1)
        vals = [r[...] for r in refs[:nr]] + [(p[0] if ncol > 1 else p[...]) for p in refs[nr:nr + npm]]
        ct = tuple(c[...].astype(F32) for c in refs[nr + npm:nr + npm + nc])
        _, vjp = jax.vjp(lambda *a: tuple(f(i * tr, *a)), *vals)
        g = vjp(ct)
        outs = refs[nr + npm + nc:]
        for o_ref, v in zip(outs[:nr], g[:nr]):
            o_ref[...] = v.astype(o_ref.dtype)
        pg = [(v[None] if ncol > 1 else v) for v in g[nr:]]

        @pl.when(i == 0)
        def _():
            for o_ref, v in zip(outs[nr:], pg):
                o_ref[...] = v

        @pl.when(i > 0)
        def _():
            for o_ref, v in zip(outs[nr:], pg):
                o_ref[...] += v

    def pspec(p):
        if ncol > 1:
            return pl.BlockSpec((1,) + p.shape[1:], lambda j, i, n=p.ndim: (j,) + (0,) * (n - 1))
        return _full_spec(p.shape, 2)

    res = pl.pallas_call(
        body, name=name, grid=(ncol, t // tr),
        in_specs=[_rowspec(r, tr, ncol) for r in rows] + [pspec(p) for p in params] + [_rowspec(c, tr, ncol) for c in cts],
        out_specs=[pl.BlockSpec((tr, w // ncol), lambda j, i: (i, j)) for _, w, _ in rows] + [pspec(p) for p in params],
        out_shape=[jax.ShapeDtypeStruct((t, w), dt) for (_, w, _), dt in zip(rows, row_dtypes)]
        + [jax.ShapeDtypeStruct(p.shape, F32) for p in params],
        compiler_params=_params(("parallel", "arbitrary")),
    )(*[r[0] for r in rows], *params, *[c[0] for c in cts])
    return res[:nr], res[nr:]


def _valid(row0, tr):
    return (row0 + lax.broadcasted_iota(jnp.int32, (tr, 1), 0)) >= PAD


def _rms(x, g):
    return x * lax.rsqrt(jnp.mean(x * x, axis=-1, keepdims=True) + EPS) * g


def _softplus(x):
    return jnp.where(x < -15.0, jnp.exp(x), jnp.maximum(x, 0.0) + jnp.log(1.0 + jnp.exp(-jnp.abs(x))))


def _neg_expm1(z):
    return jnp.where(z > -0.01, -z * (1.0 + z * (0.5 + z * (1.0 / 6.0))), 1.0 - jnp.exp(z))


def _prenorm(h, g, name):
    return _rowwise(name, lambda r0, x, gg: (_rms(x, gg),), [h], [g], [(_rowarg(h)[1], BF16)])[0]


def _post_residual(m, h, g):
    assert m.shape[1] == D_MODEL
    return m, h + _rms(m, g)


def _postnorm_bwd(m, g, dh, name):
    (dm,), (dg,) = _rowwise_vjp(name, lambda r0, mm, gg: (_rms(mm, gg),), [m], [g], [dh])
    return dm, dg


def _prenorm_bwd_add(h, g, dhns, dh, name):
    t, w = h.shape
    tr = _tile(t, 528)
    nd = len(dhns)

    def body(h_ref, g_ref, *refs):
        dh_ref, o_ref, dg_ref = refs[nd:]
        i = pl.program_id(0)
        _, vjp = jax.vjp(_rms, h_ref[...], g_ref[...])
        dhn = refs[0][...].astype(F32)
        for r in refs[1:nd]:
            dhn = dhn + r[...].astype(F32)
        dx, dg = vjp(dhn)
        o_ref[...] = dh_ref[...] + dx

        @pl.when(i == 0)
        def _():
            dg_ref[...] = dg

        @pl.when(i > 0)
        def _():
            dg_ref[...] += dg

    row = pl.BlockSpec((tr, w), lambda i: (i, 0))
    return pl.pallas_call(
        body, name=name, grid=(t // tr,), in_specs=[row, _full_spec(g.shape, 1)] + [row] * (nd + 1),
        out_specs=[row, _full_spec(g.shape, 1)],
        out_shape=[jax.ShapeDtypeStruct((t, w), F32), jax.ShapeDtypeStruct(g.shape, F32)],
        compiler_params=_params(("arbitrary",)),
    )(h, g, *dhns, dh)


def _loss_and_grad(h, target, name):
    t, w = h.shape
    nb = t // CHUNK

    def body(h_ref, t_ref, s_ref, dh_ref):
        i = pl.program_id(0)

        @pl.when(i == 0)
        def _():
            s_ref[...] = jnp.zeros_like(s_ref)
            dh_ref[...] = jnp.zeros_like(dh_ref)

        @pl.when(i > 0)
        def _():
            err = h_ref[...] - t_ref[...]
            s_ref[...] += jnp.sum(err * err)
            dh_ref[...] = err * (1.0 / w)

    return pl.pallas_call(
        body, name=name, grid=(nb,),
        in_specs=[pl.BlockSpec((CHUNK, w), lambda i: (i, 0)), pl.BlockSpec((CHUNK, w), lambda i: (jnp.maximum(i - 1, 0), 0))],
        out_specs=[_full_spec((1, LANES), 1), pl.BlockSpec((CHUNK, w), lambda i: (i, 0))],
        out_shape=[jax.ShapeDtypeStruct((1, LANES), F32), jax.ShapeDtypeStruct((t, w), F32)],
        compiler_params=_params(("arbitrary",)),
    )(h, target)


def _mlp_fwd(h, p, l):
    hn = _prenorm(h, p["mlp_pre_g"][l], "mlp_prenorm")
    a, u = _mm(hn, p["w_up"][l], "nn", "mlp_up", post=lambda v: (v, jnp.square(jnp.maximum(v, 0.0))), out_dtypes=[F32, BF16])
    d, h2 = _mm(u, p["w_down"][l], "nn", "mlp_down", extra=[h], vecs=[p["mlp_post_g"][l]], post=_post_residual, out_dtypes=[F32, F32])
    return h2, (h, hn, a, u, d)


def _mlp_bwd(dh, saved, p, l, grads):
    h, hn, a, u, d = saved
    dd, grads["mlp_post_g"][l] = _postnorm_bwd(d, p["mlp_post_g"][l], dh, "mlp_postnorm_bwd")
    grads["w_down"][l] = _mm(u, dd, "tn", "mlp_down_dw")
    da = _mm(dd, p["w_down"][l], "nt", "mlp_down_dx", extra=[a], post=lambda v, x: (2.0 * jnp.maximum(x, 0.0) * v,),
             out_dtypes=[BF16])
    grads["w_up"][l] = _mm(hn, da, "tn", "mlp_up_dw", out_chip_major=True)
    dhn = _mm(da, p["w_up"][l], "nt", "mlp_up_dx")
    dh, grads["mlp_pre_g"][l] = _prenorm_bwd_add(h, p["mlp_pre_g"][l], [dhn], dh, "mlp_prenorm_bwd")
    return dh


def _dot(a, b, mode):
    return lax.dot_general(a.astype(BF16), b.astype(BF16), _DIMS[mode], preferred_element_type=F32)


@jax.custom_vjp
def _bnn(a, b):
    return _dot(a, b, "nn")


_bnn.defvjp(lambda a, b: (_dot(a, b, "nn"), (a, b)), lambda r, ct: (_dot(ct, r[1], "nt"), _dot(r[0], ct, "tn")))


@jax.custom_vjp
def _bnt(a, b):
    return _dot(a, b, "nt")


_bnt.defvjp(lambda a, b: (_dot(a, b, "nt"), (a, b)), lambda r, ct: (_dot(ct, r[1], "nn"), _dot(ct, r[0], "tn")))


@jax.custom_vjp
def _btn(a, b):
    return _dot(a, b, "tn")


_btn.defvjp(lambda a, b: (_dot(a, b, "tn"), (a, b)), lambda r, ct: (_dot(r[1], ct, "nt"), _dot(r[0], ct, "nn")))


CONV_K = 4
HALO = 8


def _conv_fwd(x, w, b, name, cw, c0=0):
    t, c = x.shape[0], w.shape[1]
    tr = _tile(t, 528)
    hb = tr // HALO

    def body(x_ref, halo_ref, w_ref, b_ref, o_ref, ext):
        i = pl.program_id(1)
        ext[pl.ds(0, HALO), :] = jnp.where(i > 0, halo_ref[...], 0.0)
        ext[pl.ds(HALO, tr), :] = x_ref[...]
        acc = jnp.broadcast_to(b_ref[...], (tr, cw))
        for k in range(CONV_K):
            acc = acc + w_ref[pl.ds(k, 1), :] * ext[pl.ds(HALO - (CONV_K - 1) + k, tr), :]
        o_ref[...] = acc

    return pl.pallas_call(
        body, name=name, grid=(c // cw, t // tr),
        in_specs=[pl.BlockSpec((tr, cw), lambda j, i: (i, c0 + j)),
                  pl.BlockSpec((HALO, cw), lambda j, i: (jnp.maximum(i * hb - 1, 0), c0 + j)),
                  pl.BlockSpec((CONV_K, cw), lambda j, i: (0, j)), pl.BlockSpec((1, cw), lambda j, i: (0, j))],
        out_specs=pl.BlockSpec((tr, cw), lambda j, i: (i, j)),
        out_shape=jax.ShapeDtypeStruct((t, c), F32),
        scratch_shapes=[pltpu.VMEM((tr + HALO, cw), F32)],
        compiler_params=_params(("parallel", "parallel")),
    )(x, x, w, b)


def _conv_bwd(x, w, dy, name, cw, c0=0):
    t, c = x.shape[0], w.shape[1]
    tr = _tile(t, 528)
    hb = tr // HALO
    nb = t // tr

    def body(x_ref, xh_ref, w_ref, dy_ref, dyh_ref, dx_ref, dw_ref, db_ref, xe, de):
        c = cw
        i = pl.program_id(1)
        xe[pl.ds(0, HALO), :] = jnp.where(i > 0, xh_ref[...], 0.0)
        xe[pl.ds(HALO, tr), :] = x_ref[...]
        de[pl.ds(0, tr), :] = dy_ref[...]
        de[pl.ds(tr, HALO), :] = jnp.where(i < nb - 1, dyh_ref[...], 0.0)
        dy = dy_ref[...]
        acc = jnp.zeros((tr, c), F32)
        dw = jnp.zeros((CONV_K, c), F32)
        rows = lax.broadcasted_iota(jnp.int32, (CONV_K, 1), 0)
        for k in range(CONV_K):
            acc = acc + w_ref[pl.ds(k, 1), :] * de[pl.ds(CONV_K - 1 - k, tr), :]
            dwk = jnp.sum(dy * xe[pl.ds(HALO - (CONV_K - 1) + k, tr), :], axis=0, keepdims=True)
            dw = dw + jnp.where(rows == k, dwk, 0.0)
        dx_ref[...] = jnp.where(_valid(i * tr, tr), acc, 0.0)
        db = jnp.sum(dy, axis=0, keepdims=True)

        @pl.when(i == 0)
        def _():
            dw_ref[...] = dw
            db_ref[...] = db

        @pl.when(i > 0)
        def _():
            dw_ref[...] += dw
            db_ref[...] += db

    row = pl.BlockSpec((tr, cw), lambda j, i: (i, j))
    return pl.pallas_call(
        body, name=name, grid=(c // cw, nb),
        in_specs=[pl.BlockSpec((tr, cw), lambda j, i: (i, c0 + j)),
                  pl.BlockSpec((HALO, cw), lambda j, i: (jnp.maximum(i * hb - 1, 0), c0 + j)),
                  pl.BlockSpec((CONV_K, cw), lambda j, i: (0, j)),
                  row, pl.BlockSpec((HALO, cw), lambda j, i: (jnp.minimum((i + 1) * hb, t // HALO - 1), j))],
        out_specs=[row, pl.BlockSpec((CONV_K, cw), lambda j, i: (0, j)), pl.BlockSpec((1, cw), lambda j, i: (0, j))],
        out_shape=[jax.ShapeDtypeStruct((t, c), F32), jax.ShapeDtypeStruct((CONV_K, c), F32), jax.ShapeDtypeStruct((1, c), F32)],
        scratch_shapes=[pltpu.VMEM((tr + HALO, cw), F32), pltpu.VMEM((tr + HALO, cw), F32)],
        compiler_params=_params(("parallel", "arbitrary")),
    )(x, x, w, dy, dy)


SUB = 8


def _lru_scan(a, u, name):
    t, c = a.shape
    tr = _tile(t, 528)

    def body(a_ref, u_ref, o_ref, carry):
        @pl.when(pl.program_id(0) == 0)
        def _():
            carry[...] = jnp.zeros_like(carry)

        rows = lax.broadcasted_iota(jnp.int32, (SUB, 1), 0)

        def step(k, cin):
            r = pl.multiple_of(k * SUB, SUB)
            av, uv = a_ref[pl.ds(r, SUB), :], u_ref[pl.ds(r, SUB), :]
            for d in (1, 2, 4):
                m = rows >= d
                uv = uv + av * jnp.where(m, pltpu.roll(uv, d, 0), 0.0)
                av = av * jnp.where(m, pltpu.roll(av, d, 0), 1.0)
            hv = uv + av * cin
            o_ref[pl.ds(r, SUB), :] = hv
            return jnp.broadcast_to(hv[SUB - 1:SUB, :], (SUB, c))

        carry[...] = lax.fori_loop(0, tr // SUB, step, carry[...])

    row = pl.BlockSpec((tr, c), lambda i: (i, 0))
    return pl.pallas_call(
        body, name=name, grid=(t // tr,), in_specs=[row, row], out_specs=row,
        out_shape=jax.ShapeDtypeStruct((t, c), F32), scratch_shapes=[pltpu.VMEM((SUB, c), F32)],
        compiler_params=_params(("arbitrary",)),
    )(a, u)


def _lru_scan_bwd(a, hs, dy, name):
    t, c = a.shape
    tr = _tile(t, 528)
    nb, nt = t // tr, tr // SUB

    def body(a_ref, h_ref, hh_ref, dy_ref, du_ref, da_ref, gcar, acar):
        i = pl.program_id(0)

        @pl.when(i == 0)
        def _():
            gcar[...] = jnp.zeros_like(gcar)
            acar[...] = jnp.zeros_like(acar)

        rows = lax.broadcasted_iota(jnp.int32, (SUB, 1), 0)
        hhalo = jnp.where(i < nb - 1, hh_ref[...], 0.0)

        def step(kk, car):
            gin, a_next_first = car
            k = nt - 1 - kk
            r = pl.multiple_of(k * SUB, SUB)
            av, hv, dv = a_ref[pl.ds(r, SUB), :], h_ref[pl.ds(r, SUB), :], dy_ref[pl.ds(r, SUB), :]
            rp = pl.multiple_of(jnp.maximum(k - 1, 0) * SUB, SUB)
            hp = jnp.where(k > 0, h_ref[pl.ds(rp, SUB), :], hhalo)
            cv = jnp.where(rows < SUB - 1, pltpu.roll(av, SUB - 1, 0), a_next_first)
            gv = dv
            for d in (1, 2, 4):
                m = rows < SUB - d
                gv = gv + cv * jnp.where(m, pltpu.roll(gv, SUB - d, 0), 0.0)
                cv = cv * jnp.where(m, pltpu.roll(cv, SUB - d, 0), 1.0)
            gv = gv + cv * gin
            hprev = jnp.where(rows >= 1, pltpu.roll(hv, 1, 0), jnp.broadcast_to(hp[SUB - 1:SUB, :], (SUB, c)))
            du_ref[pl.ds(r, SUB), :] = gv
            da_ref[pl.ds(r, SUB), :] = gv * hprev
            return jnp.broadcast_to(gv[0:1, :], (SUB, c)), jnp.broadcast_to(av[0:1, :], (SUB, c))

        g, af = lax.fori_loop(0, nt, step, (gcar[...], acar[...]))
        gcar[...] = g
        acar[...] = af

    hb = tr // SUB
    row = pl.BlockSpec((tr, c), lambda i: (nb - 1 - i, 0))
    halo = pl.BlockSpec((SUB, c), lambda i: (jnp.maximum((nb - 1 - i) * hb - 1, 0), 0))
    return pl.pallas_call(
        body, name=name, grid=(nb,), in_specs=[row, row, halo, row], out_specs=[row, row],
        out_shape=[jax.ShapeDtypeStruct((t, c), F32)] * 2,
        scratch_shapes=[pltpu.VMEM((SUB, c), F32), pltpu.VMEM((SUB, c), F32)],
        compiler_params=_params(("arbitrary",)),
    )(a, hs, hs, dy)


def _lru_gates(row0, xr, wa, ba, wi, bi, lam):
    r = jax.nn.sigmoid(_bnn(xr, wa) + ba)
    i = jax.nn.sigmoid(_bnn(xr, wi) + bi)
    log_a = -LRU_C * r * _softplus(-lam)
    u = jnp.sqrt(_neg_expm1(2.0 * log_a)) * (i * xr)
    return jnp.exp(log_a), jnp.where(_valid(row0, xr.shape[0]), u, 0.0)


def _lru_gate_out(row0, hs, yw):
    return (hs * jax.nn.gelu(yw),)


def _rglru_fwd(h, p, l, o):
    hn = _prenorm(h, p["mix_pre_g"][l], "rg_prenorm")
    xw = _mm(hn, p["rg_w_x"][o], "nn", "rg_in_x")
    yw = _mm(hn, p["rg_w_y"][o], "nn", "rg_in_y")
    xr = _conv_fwd(xw, p["rg_conv_w"][o], p["rg_conv_b"][o], "rg_conv", cw=LRU_WIDTH // 2)
    gp = [p["rg_w_a"][o], p["rg_b_a"][o], p["rg_w_i"][o], p["rg_b_i"][o], p["rg_lambda"][o]]
    a, u = _rowwise("rg_gates", _lru_gates, [xr], gp, [(LRU_WIDTH, F32)] * 2, ncol=LRU_BLOCKS, tr=_tile(h.shape[0], 1056))
    hs = _lru_scan(a, u, "rg_scan")
    hg = _rowwise("rg_gate_out", _lru_gate_out, [hs, yw], [], [(LRU_WIDTH, BF16)])[0]
    m, h2 = _mm(hg, p["rg_w_out"][o], "nn", "rg_out", extra=[h], vecs=[p["mix_post_g"][l]], post=_post_residual, out_dtypes=[F32, F32])
    return h2, (h, hn, xw, yw, xr, a, hs, hg, m)


def _rglru_bwd(dh, saved, p, l, o, grads):
    h, hn, xw, yw, xr, a, hs, hg, m = saved
    dm, grads["mix_post_g"][l] = _postnorm_bwd(m, p["mix_post_g"][l], dh, "rg_postnorm_bwd")
    grads["rg_w_out"][o] = _mm(hg, dm, "tn", "rg_out_dw")
    dhg = _mm(dm, p["rg_w_out"][o], "nt", "rg_out_dx")
    (dhs, dyw), _ = _rowwise_vjp("rg_gate_out_bwd", _lru_gate_out, [hs, yw], [], [dhg])
    du, da = _lru_scan_bwd(a, hs, dhs, "rg_scan_bwd")
    gp = [p["rg_w_a"][o], p["rg_b_a"][o], p["rg_w_i"][o], p["rg_b_i"][o], p["rg_lambda"][o]]
    (dxr,), gg = _rowwise_vjp("rg_gates_bwd", _lru_gates, [xr], gp, [da, du], ncol=LRU_BLOCKS, tr=_tile(h.shape[0], 1056))
    grads["rg_w_a"][o], grads["rg_b_a"][o], grads["rg_w_i"][o], grads["rg_b_i"][o], grads["rg_lambda"][o] = gg
    dxw, grads["rg_conv_w"][o], grads["rg_conv_b"][o] = _conv_bwd(xw, p["rg_conv_w"][o], dxr, "rg_conv_bwd", cw=LRU_WIDTH // 2)
    grads["rg_w_x"][o] = _mm(hn, dxw, "tn", "rg_in_x_dw")
    grads["rg_w_y"][o] = _mm(hn, dyw, "tn", "rg_in_y_dw")
    dhx = _mm(dxw, p["rg_w_x"][o], "nt", "rg_in_x_dx")
    dhy = _mm(dyw, p["rg_w_y"][o], "nt", "rg_in_y_dx")
    dh, grads["mix_pre_g"][l] = _prenorm_bwd_add(h, p["mix_pre_g"][l], [dhx, dhy], dh, "rg_prenorm_bwd")
    return dh


SSD_GW = SSD_D_INNER // SSD_GROUPS
SSD_GH = SSD_HEADS // SSD_GROUPS
XACT_B = SSD_D_INNER // SSD_STATE
XACT_C = XACT_B + SSD_GROUPS


def _hp(a, b, dims=_DIMS["nn"]):
    return lax.dot_general(a, b, dims, precision=lax.Precision.HIGHEST, preferred_element_type=F32)


def _split_dot(a, e, mode, parts):
    eb = e.astype(BF16)
    out, rest = None, a
    for _ in range(parts):
        term = rest.astype(BF16)
        rest = rest - term.astype(F32)
        if mode in ("nn", "nt"):
            prod = lax.dot_general(term, eb, _DIMS[mode], preferred_element_type=F32)
        else:
            prod = lax.dot_general(eb, term, _DIMS["nn" if mode == "left" else "tn"], preferred_element_type=F32)
        out = prod if out is None else out + prod
    return out


@jax.custom_vjp
def _select_nn(a, e):
    return _split_dot(a, e, "nn", 3)


_select_nn.defvjp(lambda a, e: (_split_dot(a, e, "nn", 3), e), lambda e, ct: (_split_dot(ct, e, "nt", 2), jnp.zeros_like(e)))


@jax.custom_vjp
def _select_left(e, a):
    return _split_dot(a, e, "left", 3)


_select_left.defvjp(lambda e, a: (_split_dot(a, e, "left", 3), e),
                    lambda e, ct: (jnp.zeros_like(e), _split_dot(ct, e, "left_t", 2)))


def _ssd_chunk(xs, bm, cm, dt, da, ht, g):
    l = CHUNK
    ri = lax.broadcasted_iota(jnp.int32, (l, l), 0)
    ci = lax.broadcasted_iota(jnp.int32, (l, l), 1)
    causal = ri >= ci
    tri = causal.astype(F32)
    hr = lax.broadcasted_iota(jnp.int32, (LANES, SSD_GW), 0)
    hc = lax.broadcasted_iota(jnp.int32, (LANES, SSD_GW), 1)
    expand = (hr == g * SSD_GH + hc // SSD_HEAD_DIM).astype(F32)
    acs = _select_left(tri, da)
    acs_t = acs.T
    acs_e = _select_nn(acs, expand)
    x = xs * _select_nn(dt, expand)
    gmat = _bnt(cm, bm)
    lane = lax.broadcasted_iota(jnp.int32, (1, LANES), 1)
    sub = lax.broadcasted_iota(jnp.int32, (LANES, 1), 0)
    colhead = lax.broadcasted_iota(jnp.int32, (1, SSD_GW), 1) // SSD_HEAD_DIM
    y = _bnn(cm, ht) * jnp.exp(acs_e)
    for k in range(SSD_GH):
        hh = g * SSD_GH + k
        col = jnp.sum(jnp.where(lane == hh, acs, 0.0), axis=1, keepdims=True)
        row = jnp.sum(jnp.where(sub == hh, acs_t, 0.0), axis=0, keepdims=True)
        decay = jnp.exp(jnp.where(causal, col - row, -1e30))
        y = y + _bnn(gmat * decay, jnp.where(colhead == k, x, 0.0))
    last = lax.broadcasted_iota(jnp.int32, (l, 1), 0) == l - 1
    a_last = jnp.sum(jnp.where(last, acs_e, 0.0), axis=0, keepdims=True)
    st = _btn(bm, x * jnp.exp(a_last - acs_e))
    return y, ht * jnp.exp(a_last) + st


def _ssd_specs(nc, rev):
    def cc(c):
        return nc - 1 - c if rev else c

    return [pl.BlockSpec((CHUNK, SSD_GW), lambda c, g: (cc(c), g)),
            pl.BlockSpec((CHUNK, SSD_STATE), lambda c, g: (cc(c), XACT_B + g)),
            pl.BlockSpec((CHUNK, SSD_STATE), lambda c, g: (cc(c), XACT_C + g)),
            pl.BlockSpec((CHUNK, LANES), lambda c, g: (cc(c), 0)),
            pl.BlockSpec((CHUNK, LANES), lambda c, g: (cc(c), 0))]


def _ssd_scan(xact, dt, da, name):
    t = xact.shape[0]
    nc = t // CHUNK

    def body(xs_ref, b_ref, c_ref, dt_ref, da_ref, y_ref, hs_ref, state):
        c, g = pl.program_id(0), pl.program_id(1)

        @pl.when(c == 0)
        def _():
            state[g] = jnp.zeros((SSD_STATE, SSD_GW), F32)

        ht = state[g]
        hs_ref[0] = ht
        y, ht2 = _ssd_chunk(xs_ref[...], b_ref[...], c_ref[...], dt_ref[...], da_ref[...], ht, g)
        y_ref[...] = y
        state[g] = ht2

    return pl.pallas_call(
        body, name=name, grid=(nc, SSD_GROUPS), in_specs=_ssd_specs(nc, False),
        out_specs=[pl.BlockSpec((CHUNK, SSD_GW), lambda c, g: (c, g)),
                   pl.BlockSpec((1, SSD_STATE, SSD_GW), lambda c, g: (c * SSD_GROUPS + g, 0, 0))],
        out_shape=[jax.ShapeDtypeStruct((t, SSD_D_INNER), F32), jax.ShapeDtypeStruct((nc * SSD_GROUPS, SSD_STATE, SSD_GW), F32)],
        scratch_shapes=[pltpu.VMEM((SSD_GROUPS, SSD_STATE, SSD_GW), F32)],
        compiler_params=_params(("arbitrary", "arbitrary")),
    )(xact, xact, xact, dt, da)


def _ssd_scan_bwd(xact, dt, da, hsave, dy, dxskip, name):
    t = xact.shape[0]
    nc = t // CHUNK

    def body(xs_ref, b_ref, c_ref, dt_ref, da_ref, hs_ref, dy_ref, sk_ref, dxs_ref, db_ref, dc_ref, ddt_ref, dda_ref, dstate):
        c, g = pl.program_id(0), pl.program_id(1)

        @pl.when(c == 0)
        def _():
            dstate[g] = jnp.zeros((SSD_STATE, SSD_GW), F32)

        _, vjp = jax.vjp(lambda *a: _ssd_chunk(*a, g), xs_ref[...], b_ref[...], c_ref[...], dt_ref[...], da_ref[...], hs_ref[0])
        dxs, dbm, dcm, ddt, dda, dht = vjp((dy_ref[...], dstate[g]))
        dxs_ref[...] = dxs + sk_ref[...]
        db_ref[...] = dbm
        dc_ref[...] = dcm
        dstate[g] = dht

        @pl.when(g == 0)
        def _():
            ddt_ref[...] = ddt
            dda_ref[...] = dda

        @pl.when(g > 0)
        def _():
            ddt_ref[...] += ddt
            dda_ref[...] += dda

    grp = pl.BlockSpec((CHUNK, SSD_GW), lambda c, g: (nc - 1 - c, g))
    st = pl.BlockSpec((CHUNK, SSD_STATE), lambda c, g: (nc - 1 - c, g))
    hd = pl.BlockSpec((CHUNK, LANES), lambda c, g: (nc - 1 - c, 0))
    return pl.pallas_call(
        body, name=name, grid=(nc, SSD_GROUPS),
        in_specs=_ssd_specs(nc, True) + [pl.BlockSpec((1, SSD_STATE, SSD_GW), lambda c, g: ((nc - 1 - c) * SSD_GROUPS + g, 0, 0)), grp, grp],
        out_specs=[grp, st, st, hd, hd],
        out_shape=[jax.ShapeDtypeStruct((t, SSD_D_INNER), F32), jax.ShapeDtypeStruct((t, SSD_GROUPS * SSD_STATE), F32),
                   jax.ShapeDtypeStruct((t, SSD_GROUPS * SSD_STATE), F32), jax.ShapeDtypeStruct((t, LANES), F32),
                   jax.ShapeDtypeStruct((t, LANES), F32)],
        scratch_shapes=[pltpu.VMEM((SSD_GROUPS, SSD_STATE, SSD_GW), F32)],
        compiler_params=_params(("arbitrary", "arbitrary")),
    )(xact, xact, xact, dt, da, hsave, dy, dxskip)


def _ssd_act(row0, xc):
    return (jnp.where(_valid(row0, xc.shape[0]), jax.nn.silu(xc), 0.0),)


def _ssd_dt(row0, dtraw, dt_bias, a_log):
    dt = jnp.where(_valid(row0, dtraw.shape[0]), _softplus(dtraw + dt_bias), 0.0)
    return dt, dt * -jnp.exp(a_log)


def _ssd_post(row0, y, xs, z, d_skip, norm_g):
    hr = lax.broadcasted_iota(jnp.int32, (LANES, SSD_D_INNER), 0)
    hc = lax.broadcasted_iota(jnp.int32, (LANES, SSD_D_INNER), 1)
    expand = (hr == hc // SSD_HEAD_DIM).astype(F32)
    d_e = jnp.sum(_hp(jnp.broadcast_to(d_skip, (SUB, LANES)), expand), axis=0, keepdims=True) * (1.0 / SUB)
    return (_rms((y + xs * d_e) * jax.nn.silu(z), norm_g),)


ROPE_LO, ROPE_MID, ROPE_HI = MLA_NOPE, MLA_NOPE + MLA_ROPE // 2, MLA_NOPE + MLA_ROPE
ATT_SCALE = (MLA_NOPE + MLA_ROPE) ** -0.5


def _slot_lane(width):
    return lax.broadcasted_iota(jnp.int32, (1, width), 1) % LANES


def _swap_halves(x):
    width = x.shape[1]
    lane = _slot_lane(width)
    sw = jnp.where(lane < ROPE_MID, pltpu.roll(x, width - MLA_ROPE // 2, 1), pltpu.roll(x, MLA_ROPE // 2, 1))
    return jnp.where((lane >= ROPE_LO) & (lane < ROPE_HI), sw, 0.0)


def _rope(x, cos, sin):
    n = x.shape[1] // LANES
    return x * jnp.tile(cos, (1, n)) + _swap_halves(x) * jnp.tile(sin, (1, n))


def _rope_t(dy, cos, sin):
    n = dy.shape[1] // LANES
    return dy * jnp.tile(cos, (1, n)) + _swap_halves(dy * jnp.tile(sin, (1, n)))


ATT_SCALE2 = ATT_SCALE * float(np.log2(np.e))
MASKED = -1e30
ATT_STRIP = 64


def _att_mask(i, j, blk):
    rowid = i * blk + lax.broadcasted_iota(jnp.int32, (blk, 1), 0)
    colid = j * blk + lax.broadcasted_iota(jnp.int32, (1, blk), 1)
    return (colid <= rowid) & (colid >= PAD)


def _att_bias(blk):
    r = jnp.arange(blk)[:, None]
    c = jnp.arange(blk)[None, :]
    zero = jnp.zeros((blk, blk), F32)
    first = jnp.where(c >= PAD, 0.0, MASKED) + zero
    diag = jnp.where(c <= r, 0.0, MASKED).astype(F32)
    return jnp.stack([zero, first, diag, jnp.minimum(first, diag), zero + MASKED])


def _att_bias_index(j, i):
    return jnp.where(j > i, 4, jnp.where(j == 0, 1, 0) + jnp.where(j == i, 2, 0))


def _key_slots(row0, kv, kr):
    width = kv.shape[1]
    return jnp.where(_slot_lane(width) < MLA_NOPE, kv, jnp.tile(kr, (1, width // LANES))), kv


def _attn_fwd(qr, km, vb, name, carried=None):
    t = qr.shape[0]
    blk = _tile(t, 384, LANES)
    nq = t // blk

    bias = _att_bias(blk)

    def body(q_ref, k_ref, v_ref, b_ref, o_ref, s0, s1, p0, p1):
        i = pl.program_id(1)
        lane = lax.broadcasted_iota(jnp.int32, (1, LANES), 1)
        qb = q_ref[...]

        def rows(j):
            return pl.ds(pl.multiple_of(jnp.clip(j, 0, i) * blk, blk), blk)

        def scores(j):
            return lax.dot_general(qb, k_ref[rows(j), :], _DIMS["nt"], preferred_element_type=F32) + b_ref[_att_bias_index(j, i)]

        def half(j, car, s_cur, s_nxt, p_cur, p_prv):
            m, l, acc, al_prev = car
            s_nxt[...] = scores(j + 1)
            acc2 = al_prev * acc + lax.dot_general(p_prv[...], v_ref[rows(j - 1), :], _DIMS["nn"], preferred_element_type=F32)
            m2 = jnp.maximum(m, jnp.max(s_cur[...], axis=1, keepdims=True))
            al = jnp.exp2((m - m2) * ATT_SCALE2)
            pm = jnp.exp2(s_cur[...] * ATT_SCALE2 - m2 * ATT_SCALE2)
            p_cur[...] = pm.astype(BF16)
            return m2, al * l + jnp.sum(pm, axis=1, keepdims=True), acc2, al

        def step(jj, car):
            car = half(2 * jj, car, s0, s1, p0, p1)
            return half(2 * jj + 1, car, s1, s0, p1, p0)

        s0[...] = scores(0)
        p1[...] = jnp.zeros((blk, blk), BF16)
        car = (jnp.full((blk, 1), MASKED, F32), jnp.zeros((blk, 1), F32), jnp.zeros((blk, LANES), F32), jnp.ones((blk, 1), F32))
        steps = i // 2 + 1
        m, l, acc, al_last = lax.fori_loop(0, steps, step, car)
        acc = al_last * acc + lax.dot_general(p1[...], v_ref[rows(2 * steps - 1), :], _DIMS["nn"], preferred_element_type=F32)
        out = jnp.where(lane >= MLA_NOPE, acc / l, m * ATT_SCALE + jnp.log(l))
        o_ref[...] = jnp.where(_valid(i * blk, blk), out, 0.0)

    seq_h = pl.BlockSpec((t, LANES), lambda h, i: (0, h))
    (o,), carried_out = _carry_call(
        body, name, (MLA_HEADS, nq),
        [pl.BlockSpec((blk, LANES), lambda h, i: (i, h)), seq_h, seq_h, _full_spec(bias.shape, 2)],
        [pl.BlockSpec((blk, LANES), lambda h, i: (i, h))], [jax.ShapeDtypeStruct((t, MLA_HEADS * LANES), F32)],
        [pltpu.VMEM((blk, blk), F32)] * 2 + [pltpu.VMEM((blk, blk), BF16)] * 2, (qr, km, vb, bias), carried)
    return o, carried_out


def _attn_bwd(qr, km, vb, o, do, name, carried=None):
    t = qr.shape[0]
    blk = _tile(t, 384, LANES)
    nq = t // blk

    bias = _att_bias(blk)
    log2e = float(np.log2(np.e))

    def body(q_ref, o_ref, do_ref, k_ref, v_ref, b_ref, dq_ref, dkv_ref, dkr_ref, s0, s1, dp0, dp1, p0, p1, ds0, ds1, dk_s, dv_s):
        h, j = pl.program_id(0), pl.program_id(1)
        lane = lax.broadcasted_iota(jnp.int32, (1, LANES), 1)

        @pl.when(j == 0)
        def _():
            dq_ref[...] = jnp.zeros_like(dq_ref)

        @pl.when((h == 0) & (j == 0))
        def _():
            dkr_ref[...] = jnp.zeros_like(dkr_ref)

        kmat, vmat = k_ref[...], v_ref[...]

        def rows(i):
            return pl.ds(pl.multiple_of(jnp.clip(i, j, nq - 1) * blk, blk), blk)

        def first_stage(i, s_buf, dp_buf):
            ic = jnp.minimum(i, nq - 1)
            s_buf[...] = lax.dot_general(q_ref[rows(ic), :], kmat, _DIMS["nt"], preferred_element_type=F32) + b_ref[_att_bias_index(j, ic)]
            dp_buf[...] = lax.dot_general(do_ref[rows(ic), :].astype(BF16), vmat, _DIMS["nt"], preferred_element_type=F32)

        def middle_stage(i, s_buf, dp_buf, p_buf, ds_buf):
            r = rows(i)
            ob, dob = o_ref[r, :], do_ref[r, :]
            delta = jnp.sum(dob * ob, axis=1, keepdims=True)
            pm = jnp.exp2(s_buf[...] * ATT_SCALE2 - ob[:, 0:1] * log2e)
            p_buf[...] = pm.astype(BF16)
            ds_buf[...] = (pm * (dp_buf[...] - delta) * ATT_SCALE).astype(BF16)

        def last_stage(i, p_buf, ds_buf):
            r = rows(i)
            dv_s[...] += lax.dot_general(p_buf[...], do_ref[r, :].astype(BF16), _DIMS["tn"], preferred_element_type=F32)
            dk_s[...] += lax.dot_general(ds_buf[...], q_ref[r, :], _DIMS["tn"], preferred_element_type=F32)
            dq_ref[r, :] += lax.dot_general(ds_buf[...], kmat, _DIMS["nn"], preferred_element_type=F32)

        n = nq - j
        dk_s[...] = jnp.zeros((blk, LANES), F32)
        dv_s[...] = jnp.zeros((blk, LANES), F32)
        first_stage(j, s0, dp0)
        first_stage(j + 1, s1, dp1)
        middle_stage(j, s0, dp0, p0, ds0)

        def step(tt, carry):
            i = j + 2 * tt + 1
            first_stage(i + 1, s0, dp0)
            last_stage(i - 1, p0, ds0)
            middle_stage(i, s1, dp1, p1, ds1)
            first_stage(i + 2, s1, dp1)
            last_stage(i, p1, ds1)
            middle_stage(i + 1, s0, dp0, p0, ds0)
            return carry

        lax.fori_loop(0, (n - 1) // 2, step, 0)

        @pl.when(n % 2 == 0)
        def _():
            last_stage(nq - 2, p0, ds0)
            middle_stage(nq - 1, s1, dp1, p1, ds1)
            last_stage(nq - 1, p1, ds1)

        @pl.when(n % 2 == 1)
        def _():
            last_stage(nq - 1, p0, ds0)

        dk = dk_s[...]
        dkv_ref[...] = jnp.where(lane < MLA_NOPE, dk, dv_s[...])
        dkr_ref[rows(j), :] += jnp.where(lane >= MLA_NOPE, dk, 0.0)

    seq_h = pl.BlockSpec((t, LANES), lambda h, j: (0, h))
    blk_h = pl.BlockSpec((blk, LANES), lambda h, j: (j, h))
    return _carry_call(
        body, name, (MLA_HEADS, nq), [seq_h, seq_h, seq_h, blk_h, blk_h, _full_spec(bias.shape, 2)],
        [seq_h, blk_h, pl.BlockSpec((t, LANES), lambda h, j: (0, 0))],
        [jax.ShapeDtypeStruct((t, MLA_HEADS * LANES), F32), jax.ShapeDtypeStruct((t, MLA_HEADS * LANES), F32),
         jax.ShapeDtypeStruct((t, LANES), F32)],
        [pltpu.VMEM((blk, blk), F32)] * 4 + [pltpu.VMEM((blk, blk), BF16)] * 4 + [pltpu.VMEM((blk, LANES), F32)] * 2,
        (qr, o, do, km, vb, bias), carried)


def _rms_rows(row0, x, g):
    return (_rms(x, g),)


def _ssdmla_fwd(h, p, l, e, cos, sin, carried=None):
    hn = _prenorm(h, p["mix_pre_g"][l], "sm_prenorm")
    proj = _mm(hn, p["w_in"][e], "nn", "sm_in")
    xc = _conv_fwd(proj, p["ssd_conv_w"][e], p["ssd_conv_b"][e], "ssd_conv", cw=SSD_GW, c0=PROJ_XBC // SSD_GW)
    xact = _rowwise("ssd_act", _ssd_act, [xc], [], [(SSD_CONV_CH, F32)])[0]
    dt, da = _rowwise("ssd_dt", _ssd_dt, [(proj, LANES, PROJ_DT // LANES)], [p["ssd_dt_bias"][e], p["ssd_a_log"][e]],
                      [(LANES, F32)] * 2)
    y, hsave = _ssd_scan(xact, dt, da, "ssd_scan")
    y_ssd = _rowwise("ssd_post", _ssd_post, [y, (xact, SSD_D_INNER, 0), (proj, SSD_D_INNER, 0)],
                     [p["ssd_d"][e], p["ssd_norm_g"][e]], [(SSD_D_INNER, BF16)])[0]
    cqn = _prenorm((proj, MLA_Q_RANK, PROJ_CQ // MLA_Q_RANK), p["mla_q_norm_g"][e], "mla_qnorm")
    ckvn = _prenorm((proj, MLA_KV_RANK, PROJ_CKV // MLA_KV_RANK), p["mla_kv_norm_g"][e], "mla_kvnorm")
    q = _mm(cqn, p["mla_w_q_up"][e], "nn", "mla_q_up")
    kv = _mm(ckvn, p["mla_w_kv_up"][e], "nn", "mla_kv_up")
    kr = _rowwise("mla_krope", lambda r0, x, c, s: (_rope(x, c, s),), [(proj, LANES, PROJ_KR // LANES), cos, sin], [],
                  [(LANES, F32)])[0]
    slots, tr = MLA_HEADS * LANES, _tile(h.shape[0], 264, 16)
    qr = _rowwise("mla_q_rope", lambda r0, a, c, s: (_rope(a, c, s),), [q, cos, sin], [], [(slots, BF16)], tr=tr)[0]
    km, vb = _rowwise("mla_key_slots", _key_slots, [kv, kr], [], [(slots, BF16)] * 2, tr=tr)
    o, carried_out = _attn_fwd(qr, km, vb, "mla_attn", carried)
    m1 = _mm(y_ssd, p["w_out_ssd"][e], "nn", "sm_out_ssd")
    m, h2 = _mm(o, p["w_out_att"][e], "nn", "sm_out_att", extra=[m1, h], vecs=[p["mix_post_g"][l]],
                post=lambda v, m1b, hb, g: _post_residual(v + m1b, hb, g), out_dtypes=[F32, F32])
    return h2, (h, hn, proj, xc, xact, dt, da, y, hsave, y_ssd, cqn, ckvn, qr, km, vb, o, m), carried_out


def _ssdmla_bwd(dh, saved, p, l, e, cos, sin, grads, carried=None):
    h, hn, proj, xc, xact, dt, da, y, hsave, y_ssd, cqn, ckvn, qr, km, vb, o, m = saved
    dm, grads["mix_post_g"][l] = _postnorm_bwd(m, p["mix_post_g"][l], dh, "sm_postnorm_bwd")
    grads["w_out_ssd"][e] = _mm(y_ssd, dm, "tn", "sm_out_ssd_dw")
    grads["w_out_att"][e] = _mm(o, dm, "tn", "sm_out_att_dw")
    dy_ssd = _mm(dm, p["w_out_ssd"][e], "nt", "sm_out_ssd_dx")
    do = _mm(dm, p["w_out_att"][e], "nt", "sm_out_att_dx")
    (dqr, dkv, dkr), carried_out = _attn_bwd(qr, km, vb, o, do, "mla_attn_bwd", carried)
    dq = _rowwise("mla_q_rope_bwd", lambda r0, a, c, s: (_rope_t(a, c, s),), [dqr, cos, sin], [], [(MLA_HEADS * LANES, F32)],
                  tr=_tile(h.shape[0], 264, 16))[0]
    dkr_raw = _rowwise("mla_krope_bwd", lambda r0, d, c, s: (_rope_t(d, c, s),), [dkr, cos, sin], [], [(LANES, F32)])[0]
    grads["mla_w_q_up"][e] = _mm(cqn, dq, "tn", "mla_q_up_dw")
    dcqn = _mm(dq, p["mla_w_q_up"][e], "nt", "mla_q_up_dx")
    (dcq,), (grads["mla_q_norm_g"][e],) = _rowwise_vjp(
        "mla_qnorm_bwd", _rms_rows, [(proj, MLA_Q_RANK, PROJ_CQ // MLA_Q_RANK)], [p["mla_q_norm_g"][e]], [dcqn])
    grads["mla_w_kv_up"][e] = _mm(ckvn, dkv, "tn", "mla_kv_up_dw")
    dckvn = _mm(dkv, p["mla_w_kv_up"][e], "nt", "mla_kv_up_dx")
    (dckv,), (grads["mla_kv_norm_g"][e],) = _rowwise_vjp(
        "mla_kvnorm_bwd", _rms_rows, [(proj, MLA_KV_RANK, PROJ_CKV // MLA_KV_RANK)], [p["mla_kv_norm_g"][e]], [dckvn])
    (dy, dxskip, dz), (grads["ssd_d"][e], grads["ssd_norm_g"][e]) = _rowwise_vjp(
        "ssd_post_bwd", _ssd_post, [y, (xact, SSD_D_INNER, 0), (proj, SSD_D_INNER, 0)], [p["ssd_d"][e], p["ssd_norm_g"][e]], [dy_ssd])
    dxs, db, dc, ddt, dda = _ssd_scan_bwd(xact, dt, da, hsave, dy, dxskip, "ssd_scan_bwd")
    dxact = jnp.concatenate([dxs, db, dc], axis=1)
    (dxc,), _ = _rowwise_vjp("ssd_act_bwd", _ssd_act, [xc], [], [dxact])
    dxbc, grads["ssd_conv_w"][e], grads["ssd_conv_b"][e] = _conv_bwd(
        proj, p["ssd_conv_w"][e], dxc, "ssd_conv_bwd", cw=SSD_GW, c0=PROJ_XBC // SSD_GW)
    (ddtraw,), (grads["ssd_dt_bias"][e], grads["ssd_a_log"][e]) = _rowwise_vjp(
        "ssd_dt_bwd", _ssd_dt, [(proj, LANES, PROJ_DT // LANES)], [p["ssd_dt_bias"][e], p["ssd_a_log"][e]], [ddt, dda])
    dproj = jnp.concatenate([dz, dxbc, ddtraw, dcq, dckv, dkr_raw], axis=1)
    grads["w_in"][e] = _mm(hn, dproj, "tn", "sm_in_dw")
    dhn = _mm(dproj, p["w_in"][e], "nt", "sm_in_dx")
    dh, grads["mix_pre_g"][l] = _prenorm_bwd_add(h, p["mix_pre_g"][l], [dhn], dh, "sm_prenorm_bwd")
    return dh, carried_out


GAINS = ("mix_pre_g", "mix_post_g", "mlp_pre_g", "mlp_post_g", "ssd_norm_g", "mla_q_norm_g", "mla_kv_norm_g", "ssd_conv_b", "rg_conv_b")
HEAD_VECS = ("ssd_dt_bias", "ssd_a_log", "ssd_d")
LRU_VECS = ("rg_b_a", "rg_b_i", "rg_lambda")
IN_DT_END = SSD_D_INNER + SSD_CONV_CH + SSD_HEADS
IN_KR = IN_DT_END + MLA_Q_RANK + MLA_KV_RANK


def _each(a, f):
    layers = a if isinstance(a, list) else [a[i] for i in range(a.shape[0])]
    return [None if x is None else f(x) for x in layers]


def _layout_params(w):
    p = {k: _each(w[k], lambda a: a[None, :]) for k in GAINS}
    for k in HEAD_VECS:
        p[k] = _each(w[k], lambda a: jnp.pad(a, (0, LANES - SSD_HEADS))[None, :])
    for k in LRU_VECS:
        p[k] = _each(w[k], lambda a: a.reshape(LRU_BLOCKS, 1, LRU_BLOCK))
    for k in ("w_up", "w_down", "mla_w_kv_up", "rg_w_x", "rg_w_y", "rg_w_out"):
        p[k] = _each(w[k], lambda a: a if isinstance(a, Gathered) else a.astype(BF16))
    for k in ("ssd_conv_w", "rg_conv_w", "rg_w_a", "rg_w_i"):
        p[k] = _each(w[k], lambda a: a)

    def w_in(a):
        def zcols(n):
            return jnp.zeros((a.shape[0], n), a.dtype)

        return jnp.concatenate([a[:, :IN_DT_END], zcols(PROJ_CQ - IN_DT_END), a[:, IN_DT_END:IN_KR], zcols(ROPE_LO),
                                a[:, IN_KR:], zcols(LANES - ROPE_HI)], axis=1).astype(BF16)

    def q_up(a):
        a = a.reshape(MLA_Q_RANK, MLA_HEADS, MLA_NOPE + MLA_ROPE)
        return jnp.pad(a, ((0, 0), (0, 0), (0, LANES - MLA_NOPE - MLA_ROPE))).reshape(MLA_Q_RANK, MLA_HEADS * LANES).astype(BF16)

    def out_att(a):
        a = a[SSD_D_INNER:].reshape(MLA_HEADS, MLA_V, D_MODEL)
        return jnp.pad(a, ((0, 0), (LANES - MLA_V, 0), (0, 0))).reshape(MLA_HEADS * LANES, D_MODEL).astype(BF16)

    p["w_in"] = _each(w["w_in"], w_in)
    p["mla_w_q_up"] = _each(w["mla_w_q_up"], q_up)
    p["w_out_ssd"] = _each(w["w_out_ab"], lambda a: a[:SSD_D_INNER].astype(BF16))
    p["w_out_att"] = _each(w["w_out_ab"], out_att)
    return p


def _rope_tables(t):
    pos = (jnp.arange(t) - PAD).astype(F32)
    inv = ROPE_BASE ** (-jnp.arange(0, MLA_ROPE, 2, dtype=F32) / MLA_ROPE)
    ang = pos[:, None] * inv[None, :]
    c, s = jnp.cos(ang), jnp.sin(ang)
    one, zero = jnp.ones((t, MLA_NOPE), F32), jnp.zeros((t, MLA_NOPE), F32)
    tail = LANES - ROPE_HI
    return (jnp.concatenate([one, c, c, one[:, :tail]], axis=1), jnp.concatenate([zero, -s, s, zero[:, :tail]], axis=1))


GRAD_KEYS = GAINS + HEAD_VECS + LRU_VECS + ("w_up", "w_down", "mla_w_kv_up", "rg_w_x", "rg_w_y", "rg_w_out", "ssd_conv_w",
                                            "rg_conv_w", "rg_w_a", "rg_w_i", "w_in", "mla_w_q_up", "w_out_ssd", "w_out_att")


def _device_step(x, meta, target, p, hooks=None):
    t = PAD + N_META + x.shape[0]
    cos, sin = _rope_tables(t)
    h = jnp.concatenate([jnp.zeros((PAD, D_MODEL), F32), meta, x], axis=0)
    n_even, n_odd = (DEPTH + 1) // 2, DEPTH // 2
    saved = []
    for l in range(DEPTH):
        if l % 2 == 0:
            carried = hooks.forward_exchange() if hooks and l == 0 else None
            h, sm, arrived = _ssdmla_fwd(h, p, l, l // 2, cos, sin, carried)
            if carried is not None:
                p = hooks.after_forward_exchange(arrived)
        else:
            h, sm = _rglru_fwd(h, p, l, l // 2)
        h, sp = _mlp_fwd(h, p, l)
        saved.append((sm, sp))
    sq, dh = _loss_and_grad(h, target, "loss")
    per_layer = {"mix_pre_g": DEPTH, "mix_post_g": DEPTH, "mlp_pre_g": DEPTH, "mlp_post_g": DEPTH, "w_up": DEPTH, "w_down": DEPTH}
    grads = {k: [None] * per_layer.get(k, n_odd if k.startswith("rg_") else n_even) for k in GRAD_KEYS}
    for l in reversed(range(DEPTH)):
        sm, sp = saved[l]
        dh = _mlp_bwd(dh, sp, p, l, grads)
        if l % 2 == 0:
            carried = hooks.backward_exchange(grads, l) if hooks else None
            dh, arrived = _ssdmla_bwd(dh, sm, p, l, l // 2, cos, sin, grads, carried)
            if carried is not None:
                hooks.after_backward_exchange(arrived, l)
        else:
            dh = _rglru_bwd(dh, sm, p, l, l // 2, grads)
    return sq, dh, grads


MESH = pl.DeviceIdType.MESH
ANY = pl.BlockSpec(memory_space=pl.ANY)


def _mesh_pos():
    return lax.axis_index("x"), lax.axis_index("y"), lax.axis_index("c")


def _other_chips(x, y):
    return [(1 - x, y), (x, 1 - y), (1 - x, 1 - y)]


def _remote(src, dst, send_sems, recv_sems, k, to):
    return pltpu.make_async_remote_copy(src_ref=src, dst_ref=dst, send_sem=send_sems.at[k], recv_sem=recv_sems.at[k],
                                        device_id=to, device_id_type=MESH)


class Exchange:
    def __init__(self, ins, outs, aliases, n_sems, plan):
        self.ins, self.outs, self.aliases, self.n_sems, self.plan = list(ins), list(outs), dict(aliases), n_sems, plan


def _sems(n):
    return [pltpu.SemaphoreType.DMA((n,)), pltpu.SemaphoreType.DMA((n,))]


def _run_exchange(name, ex):
    ni, no = len(ex.ins), len(ex.outs)

    def body(*refs):
        sends = ex.plan(refs[:ni], refs[ni:ni + no], refs[-2], refs[-1], False)
        for cp in sends:
            cp.start()
        for cp in ex.plan(refs[:ni], refs[ni:ni + no], refs[-2], refs[-1], True):
            cp.wait_recv()
        for cp in sends:
            cp.wait_send()

    return pl.pallas_call(body, name=name, in_specs=[ANY] * ni, out_specs=[ANY] * no, out_shape=ex.outs,
                          input_output_aliases=ex.aliases, scratch_shapes=_sems(ex.n_sems))(*ex.ins)


def _carry_call(body, name, grid, in_specs, out_specs, out_shape, scratch_shapes, args, ex):
    if ex is None:
        res = pl.pallas_call(body, name=name, grid=grid, in_specs=in_specs, out_specs=out_specs, out_shape=out_shape,
                             scratch_shapes=scratch_shapes, compiler_params=_params(("arbitrary",) * len(grid)))(*args)
        return res, None
    ni, no, ns, xi, xo = len(in_specs), len(out_specs), len(scratch_shapes), len(ex.ins), len(ex.outs)

    def wrapped(*refs):
        ins, xin = refs[:ni], refs[ni:ni + xi]
        outs, xout = refs[ni + xi:ni + xi + no], refs[ni + xi + no:ni + xi + no + xo]
        scr, send_sems, recv_sems = refs[ni + xi + no + xo:-2], refs[-2], refs[-1]
        pid = [pl.program_id(d) for d in range(len(grid))]
        first = functools.reduce(jnp.logical_and, [p == 0 for p in pid])
        last = functools.reduce(jnp.logical_and, [p == g - 1 for p, g in zip(pid, grid)])

        @pl.when(first)
        def _():
            for cp in ex.plan(xin, xout, send_sems, recv_sems, False):
                cp.start()

        body(*ins, *outs, *scr)

        @pl.when(last)
        def _():
            for cp in ex.plan(xin, xout, send_sems, recv_sems, True):
                cp.wait_recv()
            for cp in ex.plan(xin, xout, send_sems, recv_sems, False):
                cp.wait_send()

    res = pl.pallas_call(
        wrapped, name=name, grid=grid, in_specs=list(in_specs) + [ANY] * xi, out_specs=list(out_specs) + [ANY] * xo,
        out_shape=list(out_shape) + ex.outs, scratch_shapes=list(scratch_shapes) + _sems(ex.n_sems),
        input_output_aliases={ni + i: no + o for i, o in ex.aliases.items()},
        compiler_params=_params(("arbitrary",) * len(grid)))(*args, *ex.ins)
    return res[:no], res[no:]


def _gather_ici(srcs, bufs, ranges):
    n = len(srcs)

    def plan(in_refs, out_refs, ss, rs, arrivals):
        x, y, c = _mesh_pos()
        cps = []
        for t, (l0, nl) in enumerate(ranges):
            if nl:
                s, o, lr = in_refs[t], out_refs[t], pl.ds(l0, nl)
                for j, (cx, cy) in enumerate(_other_chips(x, y)):
                    chip = 2 * cx + cy if arrivals else 2 * x + y
                    cps.append(_remote(s.at[lr, c], o.at[chip, lr, c], ss, rs, (N_CHIPS - 1) * t + j, (cx, cy, c)))
        return cps

    outs = [jax.ShapeDtypeStruct((N_CHIPS,) + s.shape, s.dtype) for s in srcs]
    if bufs is None:
        return Exchange(srcs, outs, {}, (N_CHIPS - 1) * n, plan)
    return Exchange(list(srcs) + list(bufs), outs, {n + t: t for t in range(n)}, (N_CHIPS - 1) * n, plan)


def _gather_d2d(srcs, bufs, ranges):
    n = len(srcs)

    def plan(in_refs, out_refs, ss, rs, arrivals):
        x, y, c = _mesh_pos()
        sib, me = (x, y, 1 - c), 2 * x + y
        cps = []
        for t, (l0, nl) in enumerate(ranges):
            if nl:
                s, o, lr = in_refs[t], out_refs[t], pl.ds(l0, nl)
                for j, (cx, cy) in enumerate(_other_chips(x, y)):
                    slot = o.at[2 * cx + cy, lr, c]
                    cps.append(_remote(slot, o.at[2 * cx + cy, lr, 1 - c] if arrivals else slot, ss, rs, N_CHIPS * t + j, sib))
                cps.append(_remote(s.at[lr], o.at[me, lr], ss, rs, N_CHIPS * t + N_CHIPS - 1, sib))
        return cps

    outs = [jax.ShapeDtypeStruct(b.shape, b.dtype) for b in bufs]
    return Exchange(list(srcs) + list(bufs), outs, {n + t: t for t in range(n)}, N_CHIPS * n, plan)


def _gather_chips(srcs, name):
    ranges = [(0, s.shape[0]) for s in srcs]
    bufs = _run_exchange(name + "_ici", _gather_ici(srcs, None, ranges))
    return _run_exchange(name + "_d2d", _gather_d2d(srcs, bufs, ranges))


def _pair_exchange(gs):
    def plan(in_refs, out_refs, ss, rs, arrivals):
        x, y, c = _mesh_pos()
        return [_remote(g.at[pl.ds(0, N_CHIPS), 1 - c], o, ss, rs, t, (x, y, 1 - c)) for t, (g, o) in enumerate(zip(in_refs, out_refs))]

    return Exchange(gs, [jax.ShapeDtypeStruct((g.shape[0],) + g.shape[2:], g.dtype) for g in gs], {}, len(gs), plan)


def _chip_exchange(ps, slots, qs, q_shapes):
    n = len(ps)
    kept = [g for g, q in enumerate(qs) if q is not None]

    def plan(in_refs, out_refs, ss, rs, arrivals):
        x, y, c = _mesh_pos()
        return [_remote(in_refs[t].at[2 * cx + cy], out_refs[g].at[j, li], ss, rs, (N_CHIPS - 1) * t + j, (cx, cy, c))
                for t, (g, li) in enumerate(slots) for j, (cx, cy) in enumerate(_other_chips(x, y))]

    return Exchange(list(ps) + [qs[g] for g in kept], q_shapes, {n + i: g for i, g in enumerate(kept)}, (N_CHIPS - 1) * n, plan)


def _pair_share(fs):
    def plan(in_refs, out_refs, ss, rs, arrivals):
        x, y, c = _mesh_pos()
        return [_remote(o.at[pl.ds(0, o.shape[0]), c], o.at[pl.ds(0, o.shape[0]), 1 - c if arrivals else c], ss, rs, t, (x, y, 1 - c))
                for t, o in enumerate(out_refs)]

    return Exchange(fs, [jax.ShapeDtypeStruct(f.shape, f.dtype) for f in fs], {t: t for t in range(len(fs))}, len(fs), plan)


SUM_BLOCK = 512 * 1024


def _sum_pair(g, ra, c, name):
    n, _, h, w = g.shape
    tr = _tile(h, max(16, SUM_BLOCK // w), 16)

    def body(c_ref, g_ref, r_ref, o_ref):
        o_ref[...] = (g_ref[0] + r_ref[...]).astype(o_ref.dtype)

    return pl.pallas_call(
        body, name=name,
        grid_spec=pltpu.PrefetchScalarGridSpec(
            num_scalar_prefetch=1, grid=(n, h // tr),
            in_specs=[pl.BlockSpec((1, 1, tr, w), lambda s, i, cr: (s, cr[0], i, 0)), pl.BlockSpec((1, tr, w), lambda s, i, cr: (s, i, 0))],
            out_specs=pl.BlockSpec((1, tr, w), lambda s, i, cr: (s, i, 0))),
        out_shape=jax.ShapeDtypeStruct((n, h, w), BF16),
        compiler_params=_params(("parallel", "parallel")),
    )(c.reshape(1).astype(jnp.int32), g, ra)


def _sum_chips(ps, q, pos, name):
    nc, nl, h, w = q.shape
    tr = _tile(h, max(16, SUM_BLOCK // (w * nl)), 16)

    def body(x_ref, y_ref, c_ref, *refs):
        q_ref, o_ref = refs[nl], refs[nl + 1]
        for l in range(nl):
            acc = refs[l][0].astype(F32)
            for j in range(nc):
                acc = acc + q_ref[j, l].astype(F32)
            o_ref[l] = acc

    return pl.pallas_call(
        body, name=name,
        grid_spec=pltpu.PrefetchScalarGridSpec(
            num_scalar_prefetch=3, grid=(h // tr,),
            in_specs=[pl.BlockSpec((1, tr, w), lambda i, x, y, c: (2 * x[0] + y[0], i, 0))] * nl
            + [pl.BlockSpec((nc, nl, tr, w), lambda i, x, y, c: (0, 0, i, 0))],
            out_specs=pl.BlockSpec((nl, None, tr, w), lambda i, x, y, c: (0, c[0], i, 0))),
        out_shape=jax.ShapeDtypeStruct((nl, 2, h, w), F32),
        compiler_params=_params(("parallel",)),
    )(*pos, *ps, q)


def _adamw(g, w, m, v, name):
    def f(r0, gg, ww, mm, vv):
        m2 = ADAM_B1 * mm + (1.0 - ADAM_B1) * gg
        v2 = ADAM_B2 * vv + (1.0 - ADAM_B2) * jnp.square(gg)
        m_hat = m2 / (1.0 - ADAM_B1 ** ADAM_STEP)
        v_hat = v2 / (1.0 - ADAM_B2 ** ADAM_STEP)
        return gg, -ADAM_LR * (m_hat / (jnp.sqrt(v_hat) + ADAM_EPS) + ADAM_WD * ww), m2, v2

    return _rowwise(name, f, [g, w, m, v], [], [(g.shape[1], F32)] * 4, tr=_tile(g.shape[0], 512))


WEIGHTS = (
    ("meta_tokens", (N_META, D_MODEL), 1), ("mix_pre_g", (DEPTH, D_MODEL), None), ("mix_post_g", (DEPTH, D_MODEL), None),
    ("mlp_pre_g", (DEPTH, D_MODEL), None), ("mlp_post_g", (DEPTH, D_MODEL), None), ("w_up", (DEPTH, D_MODEL, D_FF), 2),
    ("w_down", (DEPTH, D_FF, D_MODEL), 1), ("w_in", (2, D_MODEL, 3248), 2), ("ssd_conv_w", (2, CONV_K, SSD_CONV_CH), 2),
    ("ssd_conv_b", (2, SSD_CONV_CH), None), ("ssd_dt_bias", (2, SSD_HEADS), None), ("ssd_a_log", (2, SSD_HEADS), None),
    ("ssd_d", (2, SSD_HEADS), None), ("ssd_norm_g", (2, SSD_D_INNER), None), ("mla_q_norm_g", (2, MLA_Q_RANK), None),
    ("mla_w_q_up", (2, MLA_Q_RANK, MLA_HEADS * (MLA_NOPE + MLA_ROPE)), 2), ("mla_kv_norm_g", (2, MLA_KV_RANK), None),
    ("mla_w_kv_up", (2, MLA_KV_RANK, MLA_HEADS * (MLA_NOPE + MLA_V)), 2), ("w_out_ab", (2, SSD_D_INNER + MLA_HEADS * MLA_V, D_MODEL), 1),
    ("rg_w_x", (2, D_MODEL, LRU_WIDTH), 2), ("rg_w_y", (2, D_MODEL, LRU_WIDTH), 2), ("rg_conv_w", (2, CONV_K, LRU_WIDTH), 2),
    ("rg_conv_b", (2, LRU_WIDTH), 1), ("rg_w_a", (2, LRU_BLOCKS, LRU_BLOCK, LRU_BLOCK), None), ("rg_b_a", (2, LRU_WIDTH), 1),
    ("rg_w_i", (2, LRU_BLOCKS, LRU_BLOCK, LRU_BLOCK), None), ("rg_b_i", (2, LRU_WIDTH), 1), ("rg_lambda", (2, LRU_WIDTH), 1),
    ("rg_w_out", (2, LRU_WIDTH, D_MODEL), 1),
)
BIG = {"w_up": "col", "w_down": "row", "w_in": "col", "mla_w_q_up": "col", "mla_w_kv_up": "col", "w_out_ab": "row",
       "rg_w_x": "col", "rg_w_y": "col", "rg_w_out": "row"}
DIRECT = ("w_up", "w_down")
FLAT_QUANTUM = 2 * 16 * LANES
TABLE = {name: (shape, d) for name, shape, d in WEIGHTS}
SMALL_SHARDED = tuple(name for name, _, d in WEIGHTS if d is not None and name not in BIG)
REPLICATED = tuple(name for name, _, d in WEIGHTS if d is None)


def _chips_to_full(a, kind):
    if kind == "col":
        return jnp.moveaxis(a, 0, 2).reshape(a.shape[1], a.shape[2], -1)
    return jnp.moveaxis(a, 0, 1).reshape(a.shape[1], -1, a.shape[3])


def _full_to_chips(g, kind):
    if kind == "col":
        return jnp.moveaxis(g.reshape(g.shape[0], N_CHIPS, -1), 1, 0)
    return g.reshape(N_CHIPS, -1, g.shape[1])


def _chips_to_full_1(pc, kind):
    return jnp.moveaxis(pc, 0, 1).reshape(pc.shape[1], -1) if kind == "col" else pc.reshape(-1, pc.shape[2])


def _shard_shape(shape, d):
    return shape[:d] + (shape[d] // N_CHIPS,) + shape[d + 1:]


def _shard_major(full, d):
    s = full.shape
    return jnp.moveaxis(full.reshape(s[:d] + (N_CHIPS, s[d] // N_CHIPS) + s[d + 1:]), d, 0).reshape(N_CHIPS, -1)


def _from_shard_major(a, shape, d):
    ss = _shard_shape(shape, d)
    return jnp.moveaxis(a.reshape((N_CHIPS,) + ss), 0, d).reshape(shape)


def _pad_cols(a, quantum):
    n = a.shape[-1]
    return jnp.pad(a, [(0, 0)] * (a.ndim - 1) + [(0, -n % quantum)])


def _big_pieces(g):
    def w_in(a):
        return jnp.concatenate([a[:, :IN_DT_END], a[:, PROJ_CQ:PROJ_KR], a[:, PROJ_KR + ROPE_LO:PROJ_KR + ROPE_HI]], axis=1)

    def q_up(a):
        return a.reshape(MLA_Q_RANK, MLA_HEADS, LANES)[:, :, :MLA_NOPE + MLA_ROPE].reshape(MLA_Q_RANK, -1)

    def out_ab(sa):
        s, a = sa
        return jnp.concatenate([s, a.reshape(MLA_HEADS, LANES, D_MODEL)[:, LANES - MLA_V:, :].reshape(-1, D_MODEL)], axis=0)

    ident = lambda a: a
    full = {"w_down": _each(g["w_down"], ident), "w_in": _each(g["w_in"], w_in), "mla_w_q_up": _each(g["mla_w_q_up"], q_up),
            "mla_w_kv_up": _each(g["mla_w_kv_up"], ident),
            "w_out_ab": _each([None if s is None or a is None else (s, a) for s, a in zip(g["w_out_ssd"], g["w_out_att"])], out_ab),
            "rg_w_x": _each(g["rg_w_x"], ident), "rg_w_y": _each(g["rg_w_y"], ident), "rg_w_out": _each(g["rg_w_out"], ident)}
    return {name: (list(g[name]) if name == "w_up" else _each(full[name], lambda a, k=BIG[name]: _full_to_chips(a, k))) for name in BIG}


def _small_grads(g, dh):
    out = {k: jnp.stack(g[k])[:, 0, :] for k in GAINS}
    for k in HEAD_VECS:
        out[k] = jnp.stack(g[k])[:, 0, :SSD_HEADS]
    for k in LRU_VECS:
        out[k] = jnp.stack(g[k]).reshape(-1, LRU_WIDTH)
    for k in ("ssd_conv_w", "rg_conv_w", "rg_w_a", "rg_w_i"):
        out[k] = jnp.stack(g[k])
    out["meta_tokens"] = dh[PAD:PAD + N_META]
    return out


def _natural_grads(g, dh):
    out = _small_grads(g, dh)
    for name, pcs in _big_pieces(g).items():
        out[name] = jnp.stack([_chips_to_full_1(pc, BIG[name]) for pc in pcs])
    return out


class StepExchanges:
    def __init__(self, w):
        self.w = w
        self.c = lax.axis_index("c")
        self.riding, self.ras = {}, {}
        small = _pad_cols(jnp.concatenate([w[n].reshape(-1) for n in SMALL_SHARDED]), FLAT_QUANTUM).reshape(1, 2, -1, LANES)
        self.srcs = [self._halves(w[n].astype(BF16)) for n in BIG] + [small]
        first = {n: (0, 1 if n in ("w_in", "mla_w_q_up", "mla_w_kv_up", "w_out_ab") else 0) for n in BIG}
        self.first = [first[n] for n in BIG] + [(0, 1)]
        self.rest = [(nl, TABLE[n][0][0] - nl) for n, (_, nl) in zip(BIG, self.first)] + [(0, 0)]
        bufs = _run_exchange("gather_first_ici", _gather_ici(self.srcs, None, self.first))
        self.bufs = _run_exchange("gather_first_d2d", _gather_d2d(self.srcs, bufs, self.first))

    @staticmethod
    def _halves(a):
        return a.reshape(a.shape[0], 2, a.shape[1] // 2, a.shape[2])

    def params(self, ranges):
        w = self.w
        full = {n: w[n] for n in REPLICATED}
        for name, buf, (l0, nl) in zip(BIG, self.bufs, ranges):
            a = buf.reshape(buf.shape[:2] + (-1, buf.shape[4]))
            have = range(l0, l0 + nl)
            if name in DIRECT:
                full[name] = [Gathered(a, BIG[name], l) if l in have else None for l in range(a.shape[1])]
            else:
                full[name] = [_chips_to_full(a[:, l:l + 1], BIG[name])[0] if l in have else None for l in range(a.shape[1])]
        got, off = self.bufs[-1].reshape(N_CHIPS, -1), 0
        for name in SMALL_SHARDED:
            shape, d = TABLE[name]
            n = int(np.prod(_shard_shape(shape, d)))
            full[name] = _from_shard_major(got[:, off:off + n], shape, d)
            off += n
        self.meta = full.pop("meta_tokens")
        return _layout_params(full)

    def forward_exchange(self):
        return _gather_ici(self.srcs, self.bufs, self.rest)

    def after_forward_exchange(self, arrived):
        self.bufs = _run_exchange("gather_rest_d2d", _gather_d2d(self.srcs, arrived, self.rest))
        return self.params([(0, TABLE[n][0][0]) for n in BIG])

    def _pair_sums(self, pieces, tag):
        keys = list(pieces)
        ras = _run_exchange("grads_pair_exchange_" + tag, _pair_exchange([pieces[k] for k in keys]))
        return {k: _sum_pair(pieces[k], ra, self.c, "grads_pair_sum") for k, ra in zip(keys, ras)}

    def _q_shapes(self):
        return [jax.ShapeDtypeStruct((N_CHIPS - 1, s.shape[0]) + s.shape[2:], BF16) for s in self.srcs[:-1]]

    def backward_exchange(self, grads, layer):
        big = _big_pieces(grads)
        pieces = {(g, l): pc.reshape(N_CHIPS, 2, pc.shape[1] // 2, pc.shape[2]) for g, name in enumerate(BIG)
                  for l, pc in enumerate(big[name]) if pc is not None and (g, l) not in self.riding}
        if layer > 0:
            self.riding = pieces
            return _pair_exchange(list(pieces.values()))
        self.ps = {k: _sum_pair(self.riding[k], ra, self.c, "grads_pair_sum") for k, ra in self.ras.items()}
        self.ps.update(self._pair_sums(pieces, "early"))
        self.early = list(self.ps)
        return _chip_exchange([self.ps[k] for k in self.early], self.early, [None] * len(BIG), self._q_shapes())

    def after_backward_exchange(self, arrived, layer):
        if layer > 0:
            self.ras = dict(zip(self.riding, arrived))
        else:
            self.qs = list(arrived)

    def finish(self, grads, dh):
        big, small = _big_pieces(grads), _small_grads(grads, dh)
        pieces = {(g, l): pc.reshape(N_CHIPS, 2, pc.shape[1] // 2, pc.shape[2])
                  for g, name in enumerate(BIG) for l, pc in enumerate(big[name]) if (g, l) not in self.ps}
        sharded = jnp.concatenate([_shard_major(small[n], TABLE[n][1]) for n in SMALL_SHARDED], axis=1)
        rep = _pad_cols(jnp.concatenate([small[n].reshape(-1) for n in REPLICATED]), N_CHIPS * FLAT_QUANTUM)
        n_sh, n_rep = sharded.shape[1], rep.shape[0] // N_CHIPS
        flat = _pad_cols(jnp.concatenate([sharded, rep.reshape(N_CHIPS, n_rep)], axis=1), FLAT_QUANTUM)
        pieces[(len(BIG), 0)] = flat.reshape(N_CHIPS, 2, -1, LANES)
        late = self._pair_sums(pieces, "late")
        self.ps.update(late)
        keys = list(late)
        small_q = jax.ShapeDtypeStruct((N_CHIPS - 1, 1) + late[(len(BIG), 0)].shape[1:], BF16)
        qs = _run_exchange("grads_chip_exchange_late",
                           _chip_exchange([late[k] for k in keys], keys, self.qs + [None], self._q_shapes() + [small_q]))
        pos = [lax.axis_index(a).reshape(1).astype(jnp.int32) for a in ("x", "y", "c")]
        sums = [_sum_chips([self.ps[(g, l)] for l in range(q.shape[1])], q, pos, "grads_chip_sum") for g, q in enumerate(qs)]
        outs = _run_exchange("grads_pair_share", _pair_share(sums))
        out = {name: o.reshape(o.shape[0], -1, o.shape[3]) for name, o in zip(BIG, outs)}
        f = outs[-1].reshape(-1)
        rep_all = _gather_chips([f[n_sh:n_sh + n_rep].reshape(1, 2, -1, LANES)], "grads_gather_replicated")[0].reshape(-1)
        off = 0
        for name in SMALL_SHARDED:
            ss = _shard_shape(*TABLE[name])
            n = int(np.prod(ss))
            out[name] = f[off:off + n].reshape(ss)
            off += n
        off = 0
        for name in REPLICATED:
            shape = TABLE[name][0]
            n = int(np.prod(shape))
            out[name] = rep_all[off:off + n].reshape(shape)
            off += n
        return out


def kernel(x, meta_tokens, mix_pre_g, mix_post_g, mlp_pre_g, mlp_post_g, w_up, w_down, w_in, ssd_conv_w, ssd_conv_b, ssd_dt_bias, ssd_a_log, ssd_d, ssd_norm_g, mla_q_norm_g, mla_w_q_up, mla_kv_norm_g, mla_w_kv_up, w_out_ab, rg_w_x, rg_w_y, rg_conv_w, rg_conv_b, rg_w_a, rg_b_a, rg_w_i, rg_b_i, rg_lambda, rg_w_out, loss_target, m_meta_tokens, m_mix_pre_g, m_mix_post_g, m_mlp_pre_g, m_mlp_post_g, m_w_up, m_w_down, m_w_in, m_ssd_conv_w, m_ssd_conv_b, m_ssd_dt_bias, m_ssd_a_log, m_ssd_d, m_ssd_norm_g, m_mla_q_norm_g, m_mla_w_q_up, m_mla_kv_norm_g, m_mla_w_kv_up, m_w_out_ab, m_rg_w_x, m_rg_w_y, m_rg_conv_w, m_rg_conv_b, m_rg_w_a, m_rg_b_a, m_rg_w_i, m_rg_b_i, m_rg_lambda, m_rg_w_out, v_meta_tokens, v_mix_pre_g, v_mix_post_g, v_mlp_pre_g, v_mlp_post_g, v_w_up, v_w_down, v_w_in, v_ssd_conv_w, v_ssd_conv_b, v_ssd_dt_bias, v_ssd_a_log, v_ssd_d, v_ssd_norm_g, v_mla_q_norm_g, v_mla_w_q_up, v_mla_kv_norm_g, v_mla_w_kv_up, v_w_out_ab, v_rg_w_x, v_rg_w_y, v_rg_conv_w, v_rg_conv_b, v_rg_w_a, v_rg_b_a, v_rg_w_i, v_rg_b_i, v_rg_lambda, v_rg_w_out):
    names = [n for n, _, _ in WEIGHTS]
    w = dict(zip(names, (meta_tokens, mix_pre_g, mix_post_g, mlp_pre_g, mlp_post_g, w_up, w_down, w_in, ssd_conv_w, ssd_conv_b, ssd_dt_bias, ssd_a_log, ssd_d, ssd_norm_g, mla_q_norm_g, mla_w_q_up, mla_kv_norm_g, mla_w_kv_up, w_out_ab, rg_w_x, rg_w_y, rg_conv_w, rg_conv_b, rg_w_a, rg_b_a, rg_w_i, rg_b_i, rg_lambda, rg_w_out)))
    m = dict(zip(names, (m_meta_tokens, m_mix_pre_g, m_mix_post_g, m_mlp_pre_g, m_mlp_post_g, m_w_up, m_w_down, m_w_in, m_ssd_conv_w, m_ssd_conv_b, m_ssd_dt_bias, m_ssd_a_log, m_ssd_d, m_ssd_norm_g, m_mla_q_norm_g, m_mla_w_q_up, m_mla_kv_norm_g, m_mla_w_kv_up, m_w_out_ab, m_rg_w_x, m_rg_w_y, m_rg_conv_w, m_rg_conv_b, m_rg_w_a, m_rg_b_a, m_rg_w_i, m_rg_b_i, m_rg_lambda, m_rg_w_out)))
    v = dict(zip(names, (v_meta_tokens, v_mix_pre_g, v_mix_post_g, v_mlp_pre_g, v_mlp_post_g, v_w_up, v_w_down, v_w_in, v_ssd_conv_w, v_ssd_conv_b, v_ssd_dt_bias, v_ssd_a_log, v_ssd_d, v_ssd_norm_g, v_mla_q_norm_g, v_mla_w_q_up, v_mla_kv_norm_g, v_mla_w_kv_up, v_w_out_ab, v_rg_w_x, v_rg_w_y, v_rg_conv_w, v_rg_conv_b, v_rg_w_a, v_rg_b_a, v_rg_w_i, v_rg_b_i, v_rg_lambda, v_rg_w_out)))
    ex = StepExchanges(w)
    p = ex.params(ex.first)
    sq, dh, grads = _device_step(x[0], ex.meta, loss_target[0], p, hooks=ex)
    loss = lax.psum(0.5 * sq[0, 0] / D_MODEL, ("x", "y", "c"))
    g = ex.finish(grads, dh)
    grad, delta, new_m, new_v = {}, {}, {}, {}
    for name in names:
        shape = g[name].shape
        two_d = (int(np.prod(shape[:-1])), shape[-1])
        res = _adamw(g[name].reshape(two_d), w[name].reshape(two_d), m[name].reshape(two_d), v[name].reshape(two_d), "adamw")
        grad[name], delta[name], new_m[name], new_v[name] = (r.reshape(shape) for r in res)
    grad_x = dh[PAD + N_META:][None]
    return (loss, grad_x, *[grad[n] for n in names], *[delta[n] for n in names], *[new_m[n] for n in names], *[new_v[n] for n in names])
```

```python
import functools

import jax
import jax.numpy as jnp
import numpy as np
from jax import lax
from jax.experimental import pallas as pl
from jax.experimental.pallas import tpu as pltpu

F32 = jnp.float32
BF16 = jnp.bfloat16

D_MODEL = 1024
DEPTH = 4
N_META = 16
CHUNK = 128
PAD = CHUNK - N_META
EPS = 1e-6
SSD_HEADS = 16
SSD_HEAD_DIM = 64
SSD_D_INNER = SSD_HEADS * SSD_HEAD_DIM
SSD_GROUPS = 2
SSD_STATE = 128
SSD_CONV_CH = SSD_D_INNER + 2 * SSD_GROUPS * SSD_STATE
MLA_HEADS = 16
MLA_NOPE = 64
MLA_ROPE = 32
MLA_V = 64
MLA_Q_RANK = 384
MLA_KV_RANK = 256
ROPE_BASE = 10000.0
LRU_WIDTH = 1280
LRU_BLOCKS = 10
LRU_BLOCK = 128
LRU_C = 8.0
D_FF = 4 * D_MODEL
ADAM_LR, ADAM_B1, ADAM_B2, ADAM_EPS, ADAM_WD, ADAM_STEP = 0.001, 0.9, 0.999, 1e-08, 0.01, 10

LANES = 128
VMEM_LIMIT = 56 * 1024 * 1024
MM_VMEM_BUDGET = 40 * 1024 * 1024
HEAD_SLOT = 128
PROJ_Z, PROJ_XBC, PROJ_DT, PROJ_CQ, PROJ_CKV, PROJ_KR = 0, 1024, 2560, 2688, 3072, 3328
PROJ_W = 3456


def _tile(n, cap, mult=8):
    for t in range(min(n, cap), 0, -1):
        if n % t == 0 and t % mult == 0:
            return t
    return n


def _params(sem):
    return pltpu.CompilerParams(dimension_semantics=sem, vmem_limit_bytes=VMEM_LIMIT)


def _full_spec(shape, ngrid):
    nd = len(shape)
    if ngrid == 1:
        return pl.BlockSpec(shape, lambda i: (0,) * nd)
    if ngrid == 2:
        return pl.BlockSpec(shape, lambda i, j: (0,) * nd)
    return pl.BlockSpec(shape, lambda i, j, k: (0,) * nd)


_DIMS = {"nn": (((1,), (0,)), ((), ())), "nt": (((1,), (1,)), ((), ())), "tn": (((0,), (0,)), ((), ()))}


class Gathered:
    def __init__(self, arr, kind, layer):
        self.arr, self.kind, self.layer = arr, kind, layer
        _, _, r, c = arr.shape
        self.shape = (r, N_CHIPS * c) if kind == "col" else (N_CHIPS * r, c)


N_CHIPS = 4


def _mm(a, b, mode, name, out_dtype=F32, add=None, out_chip_major=False, extra=(), vecs=(), post=None, out_dtypes=None):
    if mode == "nn":
        (m, kc), (_, n) = a.shape, b.shape
    elif mode == "nt":
        (m, kc), (n, _) = a.shape, b.shape
    else:
        (kc, m), (_, n) = a.shape, b.shape
    n_tile = n // N_CHIPS if out_chip_major else n
    across = isinstance(b, Gathered) and (mode, b.kind) in (("nn", "row"), ("nt", "col"))
    if mode == "tn":
        tm, tk = _tile(m, 1024, LANES), kc
        fits = [c for c in (1280, 1152, 1024, 768, 640, 512) if n_tile % c == 0 and
                2 * kc * (tm * a.dtype.itemsize + c * b.dtype.itemsize) + 2 * tm * c * 4 <= MM_VMEM_BUDGET]
        tn = fits[0] if fits else _tile(n_tile, 1280, LANES)
        if not fits:
            tk = _tile(kc, 1408, LANES)
    else:
        tn = _tile(n_tile, 1280, LANES)
        tk = _tile(kc, 4096, LANES)
        tm = _tile(m, 1056 if tk <= 1024 else 528, 16)
    nk = kc // tk
    if mode == "tn":
        a_spec = pl.BlockSpec((tk, tm), lambda i, j, k: (k, i))
    else:
        a_spec = pl.BlockSpec((tm, tk), lambda i, j, k: (i, k))
    b_arrs = [b]
    if isinstance(b, Gathered):
        layer = b.layer
        sr, sc = b.arr.shape[2:]
        if across:
            assert nk == 1 and kc == N_CHIPS * (sr if b.kind == "row" else sc)
            b_arrs = [b.arr] * N_CHIPS
            if b.kind == "row":
                b_specs = [pl.BlockSpec((None, None, sr, tn), lambda i, j, k, s=s: (s, layer, 0, j)) for s in range(N_CHIPS)]
            else:
                b_specs = [pl.BlockSpec((None, None, tn, sc), lambda i, j, k, s=s: (s, layer, j, 0)) for s in range(N_CHIPS)]
        else:
            b_arrs = [b.arr]
            br, bc = (tk, tn) if mode == "nn" else (tn, tk)
            assert mode in ("nn", "nt") and sr % br == 0 and sc % bc == 0

            def b_map(i, j, k):
                r, c = (k, j) if mode == "nn" else (j, k)
                if b.kind == "col":
                    return ((c * bc) // sc, layer, r, ((c * bc) % sc) // bc)
                return ((r * br) // sr, layer, ((r * br) % sr) // br, c)

            b_specs = [pl.BlockSpec((None, None, br, bc), b_map)]
    elif mode == "nt":
        b_specs = [pl.BlockSpec((tn, tk), lambda i, j, k: (j, k))]
    else:
        b_specs = [pl.BlockSpec((tk, tn), lambda i, j, k: (k, j))]
    nb = len(b_arrs)
    dims = _DIMS[mode]
    if out_chip_major:
        ns = n // N_CHIPS
        o_spec = pl.BlockSpec((None, tm, tn), lambda i, j, k: ((j * tn) // ns, i, ((j * tn) % ns) // tn))
        o_shape = jax.ShapeDtypeStruct((N_CHIPS, m, ns), out_dtype)
    else:
        o_spec = pl.BlockSpec((tm, tn), lambda i, j, k: (i, j))
        o_shape = jax.ShapeDtypeStruct((m, n), out_dtype)
    extra = list(extra) + ([add] if add is not None else [])
    if add is not None:
        post = lambda v, x: (v + x,)
    vecs = list(vecs)
    nx = len(extra) + len(vecs)
    out_dtypes = out_dtypes or [out_dtype]
    no = len(out_dtypes)

    def body(a_ref, *rest):
        b_refs, rest = rest[:nb], rest[nb:]
        o_refs, acc = rest[nx:nx + no], rest[nx + no:]
        if across:
            w = kc // N_CHIPS
            p = functools.reduce(jnp.add, [
                lax.dot_general(a_ref[:, s * w:(s + 1) * w].astype(BF16), b_refs[s][...].astype(BF16), dims, preferred_element_type=F32)
                for s in range(N_CHIPS)])
        else:
            p = lax.dot_general(a_ref[...].astype(BF16), b_refs[0][...].astype(BF16), dims, preferred_element_type=F32)

        def emit(v):
            res = post(v, *[r[...] for r in rest[:nx]]) if post else (v,)
            for o_ref, r in zip(o_refs, res):
                o_ref[...] = r.astype(o_ref.dtype)

        if nk == 1:
            emit(p)
        else:
            k = pl.program_id(2)

            @pl.when(k == 0)
            def _():
                acc[0][...] = p

            @pl.when(k > 0)
            def _():
                acc[0][...] += p

            @pl.when(k == nk - 1)
            def _():
                emit(acc[0][...])

    res = pl.pallas_call(
        body, name=name, grid=(m // tm, n // tn, nk),
        in_specs=[a_spec] + b_specs + [o_spec] * len(extra) + [pl.BlockSpec((1, tn), lambda i, j, k: (0, j))] * len(vecs),
        out_specs=[o_spec] * no,
        out_shape=[jax.ShapeDtypeStruct(o_shape.shape, dt) for dt in out_dtypes],
        scratch_shapes=[pltpu.VMEM((tm, tn), F32)] if nk > 1 else [],
        compiler_params=_params(("parallel", "parallel", "arbitrary")),
    )(a, *b_arrs, *extra, *vecs)
    return res[0] if no == 1 else res


def _rowarg(r):
    return r if isinstance(r, tuple) else (r, r.shape[1], 0)


def _rowspec(r, tr, ncol):
    _, w, cb = r
    if ncol > 1:
        return pl.BlockSpec((tr, w // ncol), lambda j, i: (i, j))
    return pl.BlockSpec((tr, w), lambda j, i: (i, cb))


def _rowwise(name, f, rows, params, outs, tr=None, ncol=1):
    rows = [_rowarg(r) for r in rows]
    t = rows[0][0].shape[0]
    tr = tr or _tile(t, 528)
    nr, npm = len(rows), len(params)

    def body(*refs):
        vals = [r[...] for r in refs[:nr]] + [(p[0] if ncol > 1 else p[...]) for p in refs[nr:nr + npm]]
        res = f(pl.program_id(1) * tr, *vals)
        for o_ref, v in zip(refs[nr + npm:], res):
            o_ref[...] = v.astype(o_ref.dtype)

    def pspec(p):
        if ncol > 1:
            return pl.BlockSpec((1,) + p.shape[1:], lambda j, i, n=p.ndim: (j,) + (0,) * (n - 1))
        return _full_spec(p.shape, 2)

    return pl.pallas_call(
        body, name=name, grid=(ncol, t // tr),
        in_specs=[_rowspec(r, tr, ncol) for r in rows] + [pspec(p) for p in params],
        out_specs=[pl.BlockSpec((tr, w // ncol), lambda j, i: (i, j)) for w, _ in outs],
        out_shape=[jax.ShapeDtypeStruct((t, w), dt) for w, dt in outs],
        compiler_params=_params(("parallel", "parallel")),
    )(*[r[0] for r in rows], *params)


def _rowwise_vjp(name, f, rows, params, cts, tr=None, ncol=1, row_dtypes=None):
    rows = [_rowarg(r) for r in rows]
    cts = [_rowarg(c) for c in cts]
    t = rows[0][0].shape[0]
    tr = tr or _tile(t, 528)
    nr, npm, nc = len(rows), len(params), len(cts)
    row_dtypes = row_dtypes or [F32] * nr

    def body(*refs):
        i = pl.program_id(1)
        vals = [r[...] for r in refs[:nr]] + [(p[0] if ncol > 1 else p[...]) for p in refs[nr:nr + npm]]
        ct = tuple(c[...].astype(F32) for c in refs[nr + npm:nr + npm + nc])
        _, vjp = jax.vjp(lambda *a: tuple(f(i * tr, *a)), *vals)
        g = vjp(ct)
        outs = refs[nr + npm + nc:]
        for o_ref, v in zip(outs[:nr], g[:nr]):
            o_ref[...] = v.astype(o_ref.dtype)
        pg = [(v[None] if ncol > 1 else v) for v in g[nr:]]

        @pl.when(i == 0)
        def _():
            for o_ref, v in zip(outs[nr:], pg):
                o_ref[...] = v

        @pl.when(i > 0)
        def _():
            for o_ref, v in zip(outs[nr:], pg):
                o_ref[...] += v

    def pspec(p):
        if ncol > 1:
            return pl.BlockSpec((1,) + p.shape[1:], lambda j, i, n=p.ndim: (j,) + (0,) * (n - 1))
        return _full_spec(p.shape, 2)

    res = pl.pallas_call(
        body, name=name, grid=(ncol, t // tr),
        in_specs=[_rowspec(r, tr, ncol) for r in rows] + [pspec(p) for p in params] + [_rowspec(c, tr, ncol) for c in cts],
        out_specs=[pl.BlockSpec((tr, w // ncol), lambda j, i: (i, j)) for _, w, _ in rows] + [pspec(p) for p in params],
        out_shape=[jax.ShapeDtypeStruct((t, w), dt) for (_, w, _), dt in zip(rows, row_dtypes)]
        + [jax.ShapeDtypeStruct(p.shape, F32) for p in params],
        compiler_params=_params(("parallel", "arbitrary")),
    )(*[r[0] for r in rows], *params, *[c[0] for c in cts])
    return res[:nr], res[nr:]


def _valid(row0, tr):
    return (row0 + lax.broadcasted_iota(jnp.int32, (tr, 1), 0)) >= PAD


def _rms(x, g):
    return x * lax.rsqrt(jnp.mean(x * x, axis=-1, keepdims=True) + EPS) * g


def _softplus(x):
    return jnp.where(x < -15.0, jnp.exp(x), jnp.maximum(x, 0.0) + jnp.log(1.0 + jnp.exp(-jnp.abs(x))))


def _neg_expm1(z):
    return jnp.where(z > -0.01, -z * (1.0 + z * (0.5 + z * (1.0 / 6.0))), 1.0 - jnp.exp(z))


def _prenorm(h, g, name):
    return _rowwise(name, lambda r0, x, gg: (_rms(x, gg),), [h], [g], [(_rowarg(h)[1], BF16)])[0]


def _post_residual(m, h, g):
    assert m.shape[1] == D_MODEL
    return m, h + _rms(m, g)


def _postnorm_bwd(m, g, dh, name):
    (dm,), (dg,) = _rowwise_vjp(name, lambda r0, mm, gg: (_rms(mm, gg),), [m], [g], [dh], row_dtypes=[BF16])
    return dm, dg


def _prenorm_bwd_add(h, g, dhns, dh, name):
    t, w = h.shape
    tr = _tile(t, 528)
    nd = len(dhns)

    def body(h_ref, g_ref, *refs):
        dh_ref, o_ref, dg_ref = refs[nd:]
        i = pl.program_id(0)
        _, vjp = jax.vjp(_rms, h_ref[...], g_ref[...])
        dhn = refs[0][...].astype(F32)
        for r in refs[1:nd]:
            dhn = dhn + r[...].astype(F32)
        dx, dg = vjp(dhn)
        o_ref[...] = dh_ref[...] + dx

        @pl.when(i == 0)
        def _():
            dg_ref[...] = dg

        @pl.when(i > 0)
        def _():
            dg_ref[...] += dg

    row = pl.BlockSpec((tr, w), lambda i: (i, 0))
    return pl.pallas_call(
        body, name=name, grid=(t // tr,), in_specs=[row, _full_spec(g.shape, 1)] + [row] * (nd + 1),
        out_specs=[row, _full_spec(g.shape, 1)],
        out_shape=[jax.ShapeDtypeStruct((t, w), F32), jax.ShapeDtypeStruct(g.shape, F32)],
        compiler_params=_params(("arbitrary",)),
    )(h, g, *dhns, dh)


def _loss_and_grad(h, target, name):
    t, w = h.shape
    nb = t // CHUNK

    def body(h_ref, t_ref, s_ref, dh_ref):
        i = pl.program_id(0)

        @pl.when(i == 0)
        def _():
            s_ref[...] = jnp.zeros_like(s_ref)
            dh_ref[...] = jnp.zeros_like(dh_ref)

        @pl.when(i > 0)
        def _():
            err = h_ref[...] - t_ref[...]
            s_ref[...] += jnp.sum(err * err)
            dh_ref[...] = err * (1.0 / w)

    return pl.pallas_call(
        body, name=name, grid=(nb,),
        in_specs=[pl.BlockSpec((CHUNK, w), lambda i: (i, 0)), pl.BlockSpec((CHUNK, w), lambda i: (jnp.maximum(i - 1, 0), 0))],
        out_specs=[_full_spec((1, LANES), 1), pl.BlockSpec((CHUNK, w), lambda i: (i, 0))],
        out_shape=[jax.ShapeDtypeStruct((1, LANES), F32), jax.ShapeDtypeStruct((t, w), F32)],
        compiler_params=_params(("arbitrary",)),
    )(h, target)


def _mlp_fwd(h, p, l):
    hn = _prenorm(h, p["mlp_pre_g"][l], "mlp_prenorm")
    a, u = _mm(hn, p["w_up"][l], "nn", "mlp_up", post=lambda v: (v, jnp.square(jnp.maximum(v, 0.0))), out_dtypes=[BF16, BF16])
    d, h2 = _mm(u, p["w_down"][l], "nn", "mlp_down", extra=[h], vecs=[p["mlp_post_g"][l]], post=_post_residual, out_dtypes=[F32, F32])
    return h2, (h, hn, a, u, d)


def _mlp_bwd(dh, saved, p, l, grads):
    h, hn, a, u, d = saved
    dd, grads["mlp_post_g"][l] = _postnorm_bwd(d, p["mlp_post_g"][l], dh, "mlp_postnorm_bwd")
    grads["w_down"][l] = _mm(u, dd, "tn", "mlp_down_dw")
    da = _mm(dd, p["w_down"][l], "nt", "mlp_down_dx", extra=[a], post=lambda v, x: (2.0 * jnp.maximum(x.astype(F32), 0.0) * v,),
             out_dtypes=[BF16])
    grads["w_up"][l] = _mm(hn, da, "tn", "mlp_up_dw", out_chip_major=True)
    dhn = _mm(da, p["w_up"][l], "nt", "mlp_up_dx")
    dh, grads["mlp_pre_g"][l] = _prenorm_bwd_add(h, p["mlp_pre_g"][l], [dhn], dh, "mlp_prenorm_bwd")
    return dh


def _dot(a, b, mode):
    return lax.dot_general(a.astype(BF16), b.astype(BF16), _DIMS[mode], preferred_element_type=F32)


@jax.custom_vjp
def _bnn(a, b):
    return _dot(a, b, "nn")


_bnn.defvjp(lambda a, b: (_dot(a, b, "nn"), (a, b)), lambda r, ct: (_dot(ct, r[1], "nt"), _dot(r[0], ct, "tn")))


@jax.custom_vjp
def _bnt(a, b):
    return _dot(a, b, "nt")


_bnt.defvjp(lambda a, b: (_dot(a, b, "nt"), (a, b)), lambda r, ct: (_dot(ct, r[1], "nn"), _dot(ct, r[0], "tn")))


@jax.custom_vjp
def _btn(a, b):
    return _dot(a, b, "tn")


_btn.defvjp(lambda a, b: (_dot(a, b, "tn"), (a, b)), lambda r, ct: (_dot(r[1], ct, "nt"), _dot(r[0], ct, "nn")))


CONV_K = 4
HALO = 8


def _conv_fwd(x, w, b, name, cw, c0=0):
    t, c = x.shape[0], w.shape[1]
    tr = _tile(t, 528)
    hb = tr // HALO

    def body(x_ref, halo_ref, w_ref, b_ref, o_ref, ext):
        i = pl.program_id(1)
        ext[pl.ds(0, HALO), :] = jnp.where(i > 0, halo_ref[...], 0.0)
        ext[pl.ds(HALO, tr), :] = x_ref[...]
        acc = jnp.broadcast_to(b_ref[...], (tr, cw))
        for k in range(CONV_K):
            acc = acc + w_ref[pl.ds(k, 1), :] * ext[pl.ds(HALO - (CONV_K - 1) + k, tr), :]
        o_ref[...] = acc

    return pl.pallas_call(
        body, name=name, grid=(c // cw, t // tr),
        in_specs=[pl.BlockSpec((tr, cw), lambda j, i: (i, c0 + j)),
                  pl.BlockSpec((HALO, cw), lambda j, i: (jnp.maximum(i * hb - 1, 0), c0 + j)),
                  pl.BlockSpec((CONV_K, cw), lambda j, i: (0, j)), pl.BlockSpec((1, cw), lambda j, i: (0, j))],
        out_specs=pl.BlockSpec((tr, cw), lambda j, i: (i, j)),
        out_shape=jax.ShapeDtypeStruct((t, c), F32),
        scratch_shapes=[pltpu.VMEM((tr + HALO, cw), F32)],
        compiler_params=_params(("parallel", "parallel")),
    )(x, x, w, b)


def _conv_bwd(x, w, dy, name, cw, c0=0):
    t, c = x.shape[0], w.shape[1]
    tr = _tile(t, 528)
    hb = tr // HALO
    nb = t // tr

    def body(x_ref, xh_ref, w_ref, dy_ref, dyh_ref, dx_ref, dw_ref, db_ref, xe, de):
        c = cw
        i = pl.program_id(1)
        xe[pl.ds(0, HALO), :] = jnp.where(i > 0, xh_ref[...], 0.0)
        xe[pl.ds(HALO, tr), :] = x_ref[...]
        de[pl.ds(0, tr), :] = dy_ref[...]
        de[pl.ds(tr, HALO), :] = jnp.where(i < nb - 1, dyh_ref[...], 0.0)
        dy = dy_ref[...]
        acc = jnp.zeros((tr, c), F32)
        dw = jnp.zeros((CONV_K, c), F32)
        rows = lax.broadcasted_iota(jnp.int32, (CONV_K, 1), 0)
        for k in range(CONV_K):
            acc = acc + w_ref[pl.ds(k, 1), :] * de[pl.ds(CONV_K - 1 - k, tr), :]
            dwk = jnp.sum(dy * xe[pl.ds(HALO - (CONV_K - 1) + k, tr), :], axis=0, keepdims=True)
            dw = dw + jnp.where(rows == k, dwk, 0.0)
        dx_ref[...] = jnp.where(_valid(i * tr, tr), acc, 0.0).astype(dx_ref.dtype)
        db = jnp.sum(dy, axis=0, keepdims=True)

        @pl.when(i == 0)
        def _():
            dw_ref[...] = dw
            db_ref[...] = db

        @pl.when(i > 0)
        def _():
            dw_ref[...] += dw
            db_ref[...] += db

    row = pl.BlockSpec((tr, cw), lambda j, i: (i, j))
    return pl.pallas_call(
        body, name=name, grid=(c // cw, nb),
        in_specs=[pl.BlockSpec((tr, cw), lambda j, i: (i, c0 + j)),
                  pl.BlockSpec((HALO, cw), lambda j, i: (jnp.maximum(i * hb - 1, 0), c0 + j)),
                  pl.BlockSpec((CONV_K, cw), lambda j, i: (0, j)),
                  row, pl.BlockSpec((HALO, cw), lambda j, i: (jnp.minimum((i + 1) * hb, t // HALO - 1), j))],
        out_specs=[row, pl.BlockSpec((CONV_K, cw), lambda j, i: (0, j)), pl.BlockSpec((1, cw), lambda j, i: (0, j))],
        out_shape=[jax.ShapeDtypeStruct((t, c), BF16), jax.ShapeDtypeStruct((CONV_K, c), F32), jax.ShapeDtypeStruct((1, c), F32)],
        scratch_shapes=[pltpu.VMEM((tr + HALO, cw), F32), pltpu.VMEM((tr + HALO, cw), F32)],
        compiler_params=_params(("parallel", "arbitrary")),
    )(x, x, w, dy, dy)


SUB = 8


def _lru_scan(a, u, name):
    t, c = a.shape
    tr = _tile(t, 528)

    def body(a_ref, u_ref, o_ref, carry):
        @pl.when(pl.program_id(0) == 0)
        def _():
            carry[...] = jnp.zeros_like(carry)

        rows = lax.broadcasted_iota(jnp.int32, (SUB, 1), 0)

        def step(k, cin):
            r = pl.multiple_of(k * SUB, SUB)
            av, uv = a_ref[pl.ds(r, SUB), :], u_ref[pl.ds(r, SUB), :]
            for d in (1, 2, 4):
                m = rows >= d
                uv = uv + av * jnp.where(m, pltpu.roll(uv, d, 0), 0.0)
                av = av * jnp.where(m, pltpu.roll(av, d, 0), 1.0)
            hv = uv + av * cin
            o_ref[pl.ds(r, SUB), :] = hv
            return jnp.broadcast_to(hv[SUB - 1:SUB, :], (SUB, c))

        carry[...] = lax.fori_loop(0, tr // SUB, step, carry[...])

    row = pl.BlockSpec((tr, c), lambda i: (i, 0))
    return pl.pallas_call(
        body, name=name, grid=(t // tr,), in_specs=[row, row], out_specs=row,
        out_shape=jax.ShapeDtypeStruct((t, c), F32), scratch_shapes=[pltpu.VMEM((SUB, c), F32)],
        compiler_params=_params(("arbitrary",)),
    )(a, u)


def _lru_scan_bwd(a, hs, dy, name):
    t, c = a.shape
    tr = _tile(t, 528)
    nb, nt = t // tr, tr // SUB

    def body(a_ref, h_ref, hh_ref, dy_ref, du_ref, da_ref, gcar, acar):
        i = pl.program_id(0)

        @pl.when(i == 0)
        def _():
            gcar[...] = jnp.zeros_like(gcar)
            acar[...] = jnp.zeros_like(acar)

        rows = lax.broadcasted_iota(jnp.int32, (SUB, 1), 0)
        hhalo = jnp.where(i < nb - 1, hh_ref[...], 0.0)

        def step(kk, car):
            gin, a_next_first = car
            k = nt - 1 - kk
            r = pl.multiple_of(k * SUB, SUB)
            av, hv, dv = a_ref[pl.ds(r, SUB), :], h_ref[pl.ds(r, SUB), :], dy_ref[pl.ds(r, SUB), :]
            rp = pl.multiple_of(jnp.maximum(k - 1, 0) * SUB, SUB)
            hp = jnp.where(k > 0, h_ref[pl.ds(rp, SUB), :], hhalo)
            cv = jnp.where(rows < SUB - 1, pltpu.roll(av, SUB - 1, 0), a_next_first)
            gv = dv
            for d in (1, 2, 4):
                m = rows < SUB - d
                gv = gv + cv * jnp.where(m, pltpu.roll(gv, SUB - d, 0), 0.0)
                cv = cv * jnp.where(m, pltpu.roll(cv, SUB - d, 0), 1.0)
            gv = gv + cv * gin
            hprev = jnp.where(rows >= 1, pltpu.roll(hv, 1, 0), jnp.broadcast_to(hp[SUB - 1:SUB, :], (SUB, c)))
            du_ref[pl.ds(r, SUB), :] = gv
            da_ref[pl.ds(r, SUB), :] = gv * hprev
            return jnp.broadcast_to(gv[0:1, :], (SUB, c)), jnp.broadcast_to(av[0:1, :], (SUB, c))

        g, af = lax.fori_loop(0, nt, step, (gcar[...], acar[...]))
        gcar[...] = g
        acar[...] = af

    hb = tr // SUB
    row = pl.BlockSpec((tr, c), lambda i: (nb - 1 - i, 0))
    halo = pl.BlockSpec((SUB, c), lambda i: (jnp.maximum((nb - 1 - i) * hb - 1, 0), 0))
    return pl.pallas_call(
        body, name=name, grid=(nb,), in_specs=[row, row, halo, row], out_specs=[row, row],
        out_shape=[jax.ShapeDtypeStruct((t, c), F32)] * 2,
        scratch_shapes=[pltpu.VMEM((SUB, c), F32), pltpu.VMEM((SUB, c), F32)],
        compiler_params=_params(("arbitrary",)),
    )(a, hs, hs, dy)


def _lru_gates(row0, xr, wa, ba, wi, bi, lam):
    r = jax.nn.sigmoid(_bnn(xr, wa) + ba)
    i = jax.nn.sigmoid(_bnn(xr, wi) + bi)
    log_a = -LRU_C * r * _softplus(-lam)
    u = jnp.sqrt(_neg_expm1(2.0 * log_a)) * (i * xr)
    return jnp.exp(log_a), jnp.where(_valid(row0, xr.shape[0]), u, 0.0)


def _lru_gate_out(row0, hs, yw):
    return (hs * jax.nn.gelu(yw),)


def _rglru_fwd(h, p, l, o):
    hn = _prenorm(h, p["mix_pre_g"][l], "rg_prenorm")
    xw = _mm(hn, p["rg_w_x"][o], "nn", "rg_in_x")
    yw = _mm(hn, p["rg_w_y"][o], "nn", "rg_in_y")
    xr = _conv_fwd(xw, p["rg_conv_w"][o], p["rg_conv_b"][o], "rg_conv", cw=LRU_WIDTH // 2)
    gp = [p["rg_w_a"][o], p["rg_b_a"][o], p["rg_w_i"][o], p["rg_b_i"][o], p["rg_lambda"][o]]
    a, u = _rowwise("rg_gates", _lru_gates, [xr], gp, [(LRU_WIDTH, F32)] * 2, ncol=LRU_BLOCKS, tr=_tile(h.shape[0], 1056))
    hs = _lru_scan(a, u, "rg_scan")
    hg = _rowwise("rg_gate_out", _lru_gate_out, [hs, yw], [], [(LRU_WIDTH, BF16)])[0]
    m, h2 = _mm(hg, p["rg_w_out"][o], "nn", "rg_out", extra=[h], vecs=[p["mix_post_g"][l]], post=_post_residual, out_dtypes=[F32, F32])
    return h2, (h, hn, xw, yw, xr, a, hs, hg, m)


def _rglru_bwd(dh, saved, p, l, o, grads):
    h, hn, xw, yw, xr, a, hs, hg, m = saved
    dm, grads["mix_post_g"][l] = _postnorm_bwd(m, p["mix_post_g"][l], dh, "rg_postnorm_bwd")
    grads["rg_w_out"][o] = _mm(hg, dm, "tn", "rg_out_dw")
    dhg = _mm(dm, p["rg_w_out"][o], "nt", "rg_out_dx")
    (dhs, dyw), _ = _rowwise_vjp("rg_gate_out_bwd", _lru_gate_out, [hs, yw], [], [dhg], row_dtypes=[F32, BF16])
    du, da = _lru_scan_bwd(a, hs, dhs, "rg_scan_bwd")
    gp = [p["rg_w_a"][o], p["rg_b_a"][o], p["rg_w_i"][o], p["rg_b_i"][o], p["rg_lambda"][o]]
    (dxr,), gg = _rowwise_vjp("rg_gates_bwd", _lru_gates, [xr], gp, [da, du], ncol=LRU_BLOCKS, tr=_tile(h.shape[0], 1056))
    grads["rg_w_a"][o], grads["rg_b_a"][o], grads["rg_w_i"][o], grads["rg_b_i"][o], grads["rg_lambda"][o] = gg
    dxw, grads["rg_conv_w"][o], grads["rg_conv_b"][o] = _conv_bwd(xw, p["rg_conv_w"][o], dxr, "rg_conv_bwd", cw=LRU_WIDTH // 2)
    grads["rg_w_x"][o] = _mm(hn, dxw, "tn", "rg_in_x_dw")
    grads["rg_w_y"][o] = _mm(hn, dyw, "tn", "rg_in_y_dw")
    dhx = _mm(dxw, p["rg_w_x"][o], "nt", "rg_in_x_dx")
    dhy = _mm(dyw, p["rg_w_y"][o], "nt", "rg_in_y_dx")
    dh, grads["mix_pre_g"][l] = _prenorm_bwd_add(h, p["mix_pre_g"][l], [dhx, dhy], dh, "rg_prenorm_bwd")
    return dh


SSD_GW = SSD_D_INNER // SSD_GROUPS
SSD_GH = SSD_HEADS // SSD_GROUPS
XACT_B = SSD_D_INNER // SSD_STATE
XACT_C = XACT_B + SSD_GROUPS


def _hp(a, b, dims=_DIMS["nn"]):
    return lax.dot_general(a, b, dims, precision=lax.Precision.HIGHEST, preferred_element_type=F32)


def _split_dot(a, e, mode, parts):
    eb = e.astype(BF16)
    out, rest = None, a
    for _ in range(parts):
        term = rest.astype(BF16)
        rest = rest - term.astype(F32)
        if mode in ("nn", "nt"):
            prod = lax.dot_general(term, eb, _DIMS[mode], preferred_element_type=F32)
        else:
            prod = lax.dot_general(eb, term, _DIMS["nn" if mode == "left" else "tn"], preferred_element_type=F32)
        out = prod if out is None else out + prod
    return out


@jax.custom_vjp
def _select_nn(a, e):
    return _split_dot(a, e, "nn", 3)


_select_nn.defvjp(lambda a, e: (_split_dot(a, e, "nn", 3), e), lambda e, ct: (_split_dot(ct, e, "nt", 2), jnp.zeros_like(e)))


@jax.custom_vjp
def _select_left(e, a):
    return _split_dot(a, e, "left", 3)


_select_left.defvjp(lambda e, a: (_split_dot(a, e, "left", 3), e),
                    lambda e, ct: (jnp.zeros_like(e), _split_dot(ct, e, "left_t", 2)))


def _ssd_chunk(xs, bm, cm, dt, da, ht, g):
    l = CHUNK
    ri = lax.broadcasted_iota(jnp.int32, (l, l), 0)
    ci = lax.broadcasted_iota(jnp.int32, (l, l), 1)
    causal = ri >= ci
    tri = causal.astype(F32)
    hr = lax.broadcasted_iota(jnp.int32, (LANES, SSD_GW), 0)
    hc = lax.broadcasted_iota(jnp.int32, (LANES, SSD_GW), 1)
    expand = (hr == g * SSD_GH + hc // SSD_HEAD_DIM).astype(F32)
    acs = _select_left(tri, da)
    acs_t = acs.T
    acs_e = _select_nn(acs, expand)
    x = xs * _select_nn(dt, expand)
    gmat = _bnt(cm, bm)
    lane = lax.broadcasted_iota(jnp.int32, (1, LANES), 1)
    sub = lax.broadcasted_iota(jnp.int32, (LANES, 1), 0)
    colhead = lax.broadcasted_iota(jnp.int32, (1, SSD_GW), 1) // SSD_HEAD_DIM
    y = _bnn(cm, ht) * jnp.exp(acs_e)
    for k in range(SSD_GH):
        hh = g * SSD_GH + k
        col = jnp.sum(jnp.where(lane == hh, acs, 0.0), axis=1, keepdims=True)
        row = jnp.sum(jnp.where(sub == hh, acs_t, 0.0), axis=0, keepdims=True)
        decay = jnp.exp(jnp.where(causal, col - row, -1e30))
        y = y + _bnn(gmat * decay, jnp.where(colhead == k, x, 0.0))
    last = lax.broadcasted_iota(jnp.int32, (l, 1), 0) == l - 1
    a_last = jnp.sum(jnp.where(last, acs_e, 0.0), axis=0, keepdims=True)
    st = _btn(bm, x * jnp.exp(a_last - acs_e))
    return y, ht * jnp.exp(a_last) + st


def _ssd_specs(nc, rev):
    def cc(c):
        return nc - 1 - c if rev else c

    return [pl.BlockSpec((CHUNK, SSD_GW), lambda c, g: (cc(c), g)),
            pl.BlockSpec((CHUNK, SSD_STATE), lambda c, g: (cc(c), XACT_B + g)),
            pl.BlockSpec((CHUNK, SSD_STATE), lambda c, g: (cc(c), XACT_C + g)),
            pl.BlockSpec((CHUNK, LANES), lambda c, g: (cc(c), 0)),
            pl.BlockSpec((CHUNK, LANES), lambda c, g: (cc(c), 0))]


def _ssd_scan(xact, dt, da, name):
    t = xact.shape[0]
    nc = t // CHUNK

    def body(xs_ref, b_ref, c_ref, dt_ref, da_ref, y_ref, hs_ref, state):
        c, g = pl.program_id(0), pl.program_id(1)

        @pl.when(c == 0)
        def _():
            state[g] = jnp.zeros((SSD_STATE, SSD_GW), F32)

        ht = state[g]
        hs_ref[0] = ht
        y, ht2 = _ssd_chunk(xs_ref[...], b_ref[...], c_ref[...], dt_ref[...], da_ref[...], ht, g)
        y_ref[...] = y
        state[g] = ht2

    return pl.pallas_call(
        body, name=name, grid=(nc, SSD_GROUPS), in_specs=_ssd_specs(nc, False),
        out_specs=[pl.BlockSpec((CHUNK, SSD_GW), lambda c, g: (c, g)),
                   pl.BlockSpec((1, SSD_STATE, SSD_GW), lambda c, g: (c * SSD_GROUPS + g, 0, 0))],
        out_shape=[jax.ShapeDtypeStruct((t, SSD_D_INNER), F32), jax.ShapeDtypeStruct((nc * SSD_GROUPS, SSD_STATE, SSD_GW), F32)],
        scratch_shapes=[pltpu.VMEM((SSD_GROUPS, SSD_STATE, SSD_GW), F32)],
        compiler_params=_params(("arbitrary", "arbitrary")),
    )(xact, xact, xact, dt, da)


def _ssd_scan_bwd(xact, dt, da, hsave, dy, dxskip, name):
    t = xact.shape[0]
    nc = t // CHUNK

    def body(xs_ref, b_ref, c_ref, dt_ref, da_ref, hs_ref, dy_ref, sk_ref, dxs_ref, db_ref, dc_ref, ddt_ref, dda_ref, dstate):
        c, g = pl.program_id(0), pl.program_id(1)

        @pl.when(c == 0)
        def _():
            dstate[g] = jnp.zeros((SSD_STATE, SSD_GW), F32)

        _, vjp = jax.vjp(lambda *a: _ssd_chunk(*a, g), xs_ref[...], b_ref[...], c_ref[...], dt_ref[...], da_ref[...], hs_ref[0])
        dxs, dbm, dcm, ddt, dda, dht = vjp((dy_ref[...], dstate[g]))
        dxs_ref[...] = dxs + sk_ref[...]
        db_ref[...] = dbm
        dc_ref[...] = dcm
        dstate[g] = dht

        @pl.when(g == 0)
        def _():
            ddt_ref[...] = ddt
            dda_ref[...] = dda

        @pl.when(g > 0)
        def _():
            ddt_ref[...] += ddt
            dda_ref[...] += dda

    grp = pl.BlockSpec((CHUNK, SSD_GW), lambda c, g: (nc - 1 - c, g))
    st = pl.BlockSpec((CHUNK, SSD_STATE), lambda c, g: (nc - 1 - c, g))
    hd = pl.BlockSpec((CHUNK, LANES), lambda c, g: (nc - 1 - c, 0))
    return pl.pallas_call(
        body, name=name, grid=(nc, SSD_GROUPS),
        in_specs=_ssd_specs(nc, True) + [pl.BlockSpec((1, SSD_STATE, SSD_GW), lambda c, g: ((nc - 1 - c) * SSD_GROUPS + g, 0, 0)), grp, grp],
        out_specs=[grp, st, st, hd, hd],
        out_shape=[jax.ShapeDtypeStruct((t, SSD_D_INNER), F32), jax.ShapeDtypeStruct((t, SSD_GROUPS * SSD_STATE), F32),
                   jax.ShapeDtypeStruct((t, SSD_GROUPS * SSD_STATE), F32), jax.ShapeDtypeStruct((t, LANES), F32),
                   jax.ShapeDtypeStruct((t, LANES), F32)],
        scratch_shapes=[pltpu.VMEM((SSD_GROUPS, SSD_STATE, SSD_GW), F32)],
        compiler_params=_params(("arbitrary", "arbitrary")),
    )(xact, xact, xact, dt, da, hsave, dy, dxskip)


def _ssd_act(row0, xc):
    return (jnp.where(_valid(row0, xc.shape[0]), jax.nn.silu(xc), 0.0),)


def _ssd_dt(row0, dtraw, dt_bias, a_log):
    dt = jnp.where(_valid(row0, dtraw.shape[0]), _softplus(dtraw + dt_bias), 0.0)
    return dt, dt * -jnp.exp(a_log)


def _ssd_post(row0, y, xs, z, d_skip, norm_g):
    hr = lax.broadcasted_iota(jnp.int32, (LANES, SSD_D_INNER), 0)
    hc = lax.broadcasted_iota(jnp.int32, (LANES, SSD_D_INNER), 1)
    expand = (hr == hc // SSD_HEAD_DIM).astype(F32)
    d_e = jnp.sum(_hp(jnp.broadcast_to(d_skip, (SUB, LANES)), expand), axis=0, keepdims=True) * (1.0 / SUB)
    return (_rms((y + xs * d_e) * jax.nn.silu(z), norm_g),)


ROPE_LO, ROPE_MID, ROPE_HI = MLA_NOPE, MLA_NOPE + MLA_ROPE // 2, MLA_NOPE + MLA_ROPE
ATT_SCALE = (MLA_NOPE + MLA_ROPE) ** -0.5


def _slot_lane(width):
    return lax.broadcasted_iota(jnp.int32, (1, width), 1) % LANES


def _swap_halves(x):
    width = x.shape[1]
    lane = _slot_lane(width)
    sw = jnp.where(lane < ROPE_MID, pltpu.roll(x, width - MLA_ROPE // 2, 1), pltpu.roll(x, MLA_ROPE // 2, 1))
    return jnp.where((lane >= ROPE_LO) & (lane < ROPE_HI), sw, 0.0)


def _rope(x, cos, sin):
    n = x.shape[1] // LANES
    return x * jnp.tile(cos, (1, n)) + _swap_halves(x) * jnp.tile(sin, (1, n))


def _rope_t(dy, cos, sin):
    n = dy.shape[1] // LANES
    return dy * jnp.tile(cos, (1, n)) + _swap_halves(dy * jnp.tile(sin, (1, n)))


ATT_SCALE2 = ATT_SCALE * float(np.log2(np.e))
MASKED = -1e30
ATT_STRIP = 64


def _att_mask(i, j, blk):
    rowid = i * blk + lax.broadcasted_iota(jnp.int32, (blk, 1), 0)
    colid = j * blk + lax.broadcasted_iota(jnp.int32, (1, blk), 1)
    return (colid <= rowid) & (colid >= PAD)


def _att_bias(blk):
    r = jnp.arange(blk)[:, None]
    c = jnp.arange(blk)[None, :]
    zero = jnp.zeros((blk, blk), F32)
    first = jnp.where(c >= PAD, 0.0, MASKED) + zero
    diag = jnp.where(c <= r, 0.0, MASKED).astype(F32)
    return jnp.stack([zero, first, diag, jnp.minimum(first, diag), zero + MASKED])


def _att_bias_index(j, i):
    return jnp.where(j > i, 4, jnp.where(j == 0, 1, 0) + jnp.where(j == i, 2, 0))


def _key_slots(row0, kv, kr):
    width = kv.shape[1]
    return jnp.where(_slot_lane(width) < MLA_NOPE, kv, jnp.tile(kr, (1, width // LANES))), kv


def _attn_fwd(qr, km, vb, name, carried=None):
    t = qr.shape[0]
    blk = _tile(t, 384, LANES)
    nq = t // blk

    bias = _att_bias(blk)

    def body(q_ref, k_ref, v_ref, b_ref, o_ref, s0, s1, p0, p1):
        i = pl.program_id(1)
        lane = lax.broadcasted_iota(jnp.int32, (1, LANES), 1)
        qb = q_ref[...]

        def rows(j):
            return pl.ds(pl.multiple_of(jnp.clip(j, 0, i) * blk, blk), blk)

        def scores(j):
            return lax.dot_general(qb, k_ref[rows(j), :], _DIMS["nt"], preferred_element_type=F32) + b_ref[_att_bias_index(j, i)]

        def half(j, car, s_cur, s_nxt, p_cur, p_prv):
            m, l, acc, al_prev = car
            s_nxt[...] = scores(j + 1)
            acc2 = al_prev * acc + lax.dot_general(p_prv[...], v_ref[rows(j - 1), :], _DIMS["nn"], preferred_element_type=F32)
            m2 = jnp.maximum(m, jnp.max(s_cur[...], axis=1, keepdims=True))
            al = jnp.exp2((m - m2) * ATT_SCALE2)
            pm = jnp.exp2(s_cur[...] * ATT_SCALE2 - m2 * ATT_SCALE2)
            p_cur[...] = pm.astype(BF16)
            return m2, al * l + jnp.sum(pm, axis=1, keepdims=True), acc2, al

        def step(jj, car):
            car = half(2 * jj, car, s0, s1, p0, p1)
            return half(2 * jj + 1, car, s1, s0, p1, p0)

        s0[...] = scores(0)
        p1[...] = jnp.zeros((blk, blk), BF16)
        car = (jnp.full((blk, 1), MASKED, F32), jnp.zeros((blk, 1), F32), jnp.zeros((blk, LANES), F32), jnp.ones((blk, 1), F32))
        steps = i // 2 + 1
        m, l, acc, al_last = lax.fori_loop(0, steps, step, car)
        acc = al_last * acc + lax.dot_general(p1[...], v_ref[rows(2 * steps - 1), :], _DIMS["nn"], preferred_element_type=F32)
        out = jnp.where(lane >= MLA_NOPE, acc / l, m * ATT_SCALE + jnp.log(l))
        o_ref[...] = jnp.where(_valid(i * blk, blk), out, 0.0)

    seq_h = pl.BlockSpec((t, LANES), lambda h, i: (0, h))
    (o,), carried_out = _carry_call(
        body, name, (MLA_HEADS, nq),
        [pl.BlockSpec((blk, LANES), lambda h, i: (i, h)), seq_h, seq_h, _full_spec(bias.shape, 2)],
        [pl.BlockSpec((blk, LANES), lambda h, i: (i, h))], [jax.ShapeDtypeStruct((t, MLA_HEADS * LANES), F32)],
        [pltpu.VMEM((blk, blk), F32)] * 2 + [pltpu.VMEM((blk, blk), BF16)] * 2, (qr, km, vb, bias), carried)
    return o, carried_out


def _attn_bwd(qr, km, vb, o, do, name, carried=None):
    t = qr.shape[0]
    blk = _tile(t, 384, LANES)
    nq = t // blk

    bias = _att_bias(blk)
    log2e = float(np.log2(np.e))

    def body(q_ref, o_ref, do_ref, k_ref, v_ref, b_ref, dq_ref, dkv_ref, dkr_ref, s0, s1, dp0, dp1, p0, p1, ds0, ds1, dk_s, dv_s):
        h, j = pl.program_id(0), pl.program_id(1)
        lane = lax.broadcasted_iota(jnp.int32, (1, LANES), 1)

        @pl.when(j == 0)
        def _():
            dq_ref[...] = jnp.zeros_like(dq_ref)

        @pl.when((h == 0) & (j == 0))
        def _():
            dkr_ref[...] = jnp.zeros_like(dkr_ref)

        kmat, vmat = k_ref[...], v_ref[...]

        def rows(i):
            return pl.ds(pl.multiple_of(jnp.clip(i, j, nq - 1) * blk, blk), blk)

        def first_stage(i, s_buf, dp_buf):
            ic = jnp.minimum(i, nq - 1)
            s_buf[...] = lax.dot_general(q_ref[rows(ic), :], kmat, _DIMS["nt"], preferred_element_type=F32) + b_ref[_att_bias_index(j, ic)]
            dp_buf[...] = lax.dot_general(do_ref[rows(ic), :].astype(BF16), vmat, _DIMS["nt"], preferred_element_type=F32)

        def middle_stage(i, s_buf, dp_buf, p_buf, ds_buf):
            r = rows(i)
            ob, dob = o_ref[r, :], do_ref[r, :]
            delta = jnp.sum(dob * ob, axis=1, keepdims=True)
            pm = jnp.exp2(s_buf[...] * ATT_SCALE2 - ob[:, 0:1] * log2e)
            p_buf[...] = pm.astype(BF16)
            ds_buf[...] = (pm * (dp_buf[...] - delta) * ATT_SCALE).astype(BF16)

        def last_stage(i, p_buf, ds_buf):
            r = rows(i)
            dv_s[...] += lax.dot_general(p_buf[...], do_ref[r, :].astype(BF16), _DIMS["tn"], preferred_element_type=F32)
            dk_s[...] += lax.dot_general(ds_buf[...], q_ref[r, :], _DIMS["tn"], preferred_element_type=F32)
            dq_ref[r, :] += lax.dot_general(ds_buf[...], kmat, _DIMS["nn"], preferred_element_type=F32)

        n = nq - j
        dk_s[...] = jnp.zeros((blk, LANES), F32)
        dv_s[...] = jnp.zeros((blk, LANES), F32)
        first_stage(j, s0, dp0)
        first_stage(j + 1, s1, dp1)
        middle_stage(j, s0, dp0, p0, ds0)

        def step(tt, carry):
            i = j + 2 * tt + 1
            first_stage(i + 1, s0, dp0)
            last_stage(i - 1, p0, ds0)
            middle_stage(i, s1, dp1, p1, ds1)
            first_stage(i + 2, s1, dp1)
            last_stage(i, p1, ds1)
            middle_stage(i + 1, s0, dp0, p0, ds0)
            return carry

        lax.fori_loop(0, (n - 1) // 2, step, 0)

        @pl.when(n % 2 == 0)
        def _():
            last_stage(nq - 2, p0, ds0)
            middle_stage(nq - 1, s1, dp1, p1, ds1)
            last_stage(nq - 1, p1, ds1)

        @pl.when(n % 2 == 1)
        def _():
            last_stage(nq - 1, p0, ds0)

        dk = dk_s[...]
        dkv_ref[...] = jnp.where(lane < MLA_NOPE, dk, dv_s[...]).astype(dkv_ref.dtype)
        dkr_ref[rows(j), :] += jnp.where(lane >= MLA_NOPE, dk, 0.0)

    seq_h = pl.BlockSpec((t, LANES), lambda h, j: (0, h))
    blk_h = pl.BlockSpec((blk, LANES), lambda h, j: (j, h))
    return _carry_call(
        body, name, (MLA_HEADS, nq), [seq_h, seq_h, seq_h, blk_h, blk_h, _full_spec(bias.shape, 2)],
        [seq_h, blk_h, pl.BlockSpec((t, LANES), lambda h, j: (0, 0))],
        [jax.ShapeDtypeStruct((t, MLA_HEADS * LANES), F32), jax.ShapeDtypeStruct((t, MLA_HEADS * LANES), BF16),
         jax.ShapeDtypeStruct((t, LANES), F32)],
        [pltpu.VMEM((blk, blk), F32)] * 4 + [pltpu.VMEM((blk, blk), BF16)] * 4 + [pltpu.VMEM((blk, LANES), F32)] * 2,
        (qr, o, do, km, vb, bias), carried)


def _rms_rows(row0, x, g):
    return (_rms(x, g),)


def _ssdmla_fwd(h, p, l, e, cos, sin, carried=None):
    hn = _prenorm(h, p["mix_pre_g"][l], "sm_prenorm")
    proj = _mm(hn, p["w_in"][e], "nn", "sm_in")
    xc = _conv_fwd(proj, p["ssd_conv_w"][e], p["ssd_conv_b"][e], "ssd_conv", cw=SSD_GW, c0=PROJ_XBC // SSD_GW)
    xact = _rowwise("ssd_act", _ssd_act, [xc], [], [(SSD_CONV_CH, F32)])[0]
    dt, da = _rowwise("ssd_dt", _ssd_dt, [(proj, LANES, PROJ_DT // LANES)], [p["ssd_dt_bias"][e], p["ssd_a_log"][e]],
                      [(LANES, F32)] * 2)
    y, hsave = _ssd_scan(xact, dt, da, "ssd_scan")
    y_ssd = _rowwise("ssd_post", _ssd_post, [y, (xact, SSD_D_INNER, 0), (proj, SSD_D_INNER, 0)],
                     [p["ssd_d"][e], p["ssd_norm_g"][e]], [(SSD_D_INNER, BF16)])[0]
    cqn = _prenorm((proj, MLA_Q_RANK, PROJ_CQ // MLA_Q_RANK), p["mla_q_norm_g"][e], "mla_qnorm")
    ckvn = _prenorm((proj, MLA_KV_RANK, PROJ_CKV // MLA_KV_RANK), p["mla_kv_norm_g"][e], "mla_kvnorm")
    q = _mm(cqn, p["mla_w_q_up"][e], "nn", "mla_q_up")
    kv = _mm(ckvn, p["mla_w_kv_up"][e], "nn", "mla_kv_up")
    kr = _rowwise("mla_krope", lambda r0, x, c, s: (_rope(x, c, s),), [(proj, LANES, PROJ_KR // LANES), cos, sin], [],
                  [(LANES, F32)])[0]
    slots, tr = MLA_HEADS * LANES, _tile(h.shape[0], 264, 16)
    qr = _rowwise("mla_q_rope", lambda r0, a, c, s: (_rope(a, c, s),), [q, cos, sin], [], [(slots, BF16)], tr=tr)[0]
    km, vb = _rowwise("mla_key_slots", _key_slots, [kv, kr], [], [(slots, BF16)] * 2, tr=tr)
    o, carried_out = _attn_fwd(qr, km, vb, "mla_attn", carried)
    m1 = _mm(y_ssd, p["w_out_ssd"][e], "nn", "sm_out_ssd")
    m, h2 = _mm(o, p["w_out_att"][e], "nn", "sm_out_att", extra=[m1, h], vecs=[p["mix_post_g"][l]],
                post=lambda v, m1b, hb, g: _post_residual(v + m1b, hb, g), out_dtypes=[F32, F32])
    return h2, (h, hn, proj, xc, xact, dt, da, y, hsave, y_ssd, cqn, ckvn, qr, km, vb, o, m), carried_out


def _ssdmla_bwd(dh, saved, p, l, e, cos, sin, grads, carried=None):
    h, hn, proj, xc, xact, dt, da, y, hsave, y_ssd, cqn, ckvn, qr, km, vb, o, m = saved
    dm, grads["mix_post_g"][l] = _postnorm_bwd(m, p["mix_post_g"][l], dh, "sm_postnorm_bwd")
    grads["w_out_ssd"][e] = _mm(y_ssd, dm, "tn", "sm_out_ssd_dw")
    grads["w_out_att"][e] = _mm(o, dm, "tn", "sm_out_att_dw")
    dy_ssd = _mm(dm, p["w_out_ssd"][e], "nt", "sm_out_ssd_dx")
    do = _mm(dm, p["w_out_att"][e], "nt", "sm_out_att_dx")
    (dqr, dkv, dkr), carried_out = _attn_bwd(qr, km, vb, o, do, "mla_attn_bwd", carried)
    dq = _rowwise("mla_q_rope_bwd", lambda r0, a, c, s: (_rope_t(a, c, s),), [dqr, cos, sin], [], [(MLA_HEADS * LANES, BF16)],
                  tr=_tile(h.shape[0], 264, 16))[0]
    dkr_raw = _rowwise("mla_krope_bwd", lambda r0, d, c, s: (_rope_t(d, c, s),), [dkr, cos, sin], [], [(LANES, F32)])[0]
    grads["mla_w_q_up"][e] = _mm(cqn, dq, "tn", "mla_q_up_dw")
    dcqn = _mm(dq, p["mla_w_q_up"][e], "nt", "mla_q_up_dx")
    (dcq,), (grads["mla_q_norm_g"][e],) = _rowwise_vjp(
        "mla_qnorm_bwd", _rms_rows, [(proj, MLA_Q_RANK, PROJ_CQ // MLA_Q_RANK)], [p["mla_q_norm_g"][e]], [dcqn])
    grads["mla_w_kv_up"][e] = _mm(ckvn, dkv, "tn", "mla_kv_up_dw")
    dckvn = _mm(dkv, p["mla_w_kv_up"][e], "nt", "mla_kv_up_dx")
    (dckv,), (grads["mla_kv_norm_g"][e],) = _rowwise_vjp(
        "mla_kvnorm_bwd", _rms_rows, [(proj, MLA_KV_RANK, PROJ_CKV // MLA_KV_RANK)], [p["mla_kv_norm_g"][e]], [dckvn])
    (dy, dxskip, dz), (grads["ssd_d"][e], grads["ssd_norm_g"][e]) = _rowwise_vjp(
        "ssd_post_bwd", _ssd_post, [y, (xact, SSD_D_INNER, 0), (proj, SSD_D_INNER, 0)], [p["ssd_d"][e], p["ssd_norm_g"][e]], [dy_ssd])
    dxs, db, dc, ddt, dda = _ssd_scan_bwd(xact, dt, da, hsave, dy, dxskip, "ssd_scan_bwd")
    dxact = jnp.concatenate([dxs, db, dc], axis=1)
    (dxc,), _ = _rowwise_vjp("ssd_act_bwd", _ssd_act, [xc], [], [dxact])
    dxbc, grads["ssd_conv_w"][e], grads["ssd_conv_b"][e] = _conv_bwd(
        proj, p["ssd_conv_w"][e], dxc, "ssd_conv_bwd", cw=SSD_GW, c0=PROJ_XBC // SSD_GW)
    (ddtraw,), (grads["ssd_dt_bias"][e], grads["ssd_a_log"][e]) = _rowwise_vjp(
        "ssd_dt_bwd", _ssd_dt, [(proj, LANES, PROJ_DT // LANES)], [p["ssd_dt_bias"][e], p["ssd_a_log"][e]], [ddt, dda])
    dproj = jnp.concatenate([dz, dxbc, ddtraw, dcq, dckv, dkr_raw], axis=1).astype(BF16)
    grads["w_in"][e] = _mm(hn, dproj, "tn", "sm_in_dw")
    dhn = _mm(dproj, p["w_in"][e], "nt", "sm_in_dx")
    dh, grads["mix_pre_g"][l] = _prenorm_bwd_add(h, p["mix_pre_g"][l], [dhn], dh, "sm_prenorm_bwd")
    return dh, carried_out


GAINS = ("mix_pre_g", "mix_post_g", "mlp_pre_g", "mlp_post_g", "ssd_norm_g", "mla_q_norm_g", "mla_kv_norm_g", "ssd_conv_b", "rg_conv_b")
HEAD_VECS = ("ssd_dt_bias", "ssd_a_log", "ssd_d")
LRU_VECS = ("rg_b_a", "rg_b_i", "rg_lambda")
IN_DT_END = SSD_D_INNER + SSD_CONV_CH + SSD_HEADS
IN_KR = IN_DT_END + MLA_Q_RANK + MLA_KV_RANK


def _each(a, f):
    layers = a if isinstance(a, list) else [a[i] for i in range(a.shape[0])]
    return [None if x is None else f(x) for x in layers]


def _layout_params(w):
    p = {k: _each(w[k], lambda a: a[None, :]) for k in GAINS}
    for k in HEAD_VECS:
        p[k] = _each(w[k], lambda a: jnp.pad(a, (0, LANES - SSD_HEADS))[None, :])
    for k in LRU_VECS:
        p[k] = _each(w[k], lambda a: a.reshape(LRU_BLOCKS, 1, LRU_BLOCK))
    for k in ("w_up", "w_down", "mla_w_kv_up", "rg_w_x", "rg_w_y", "rg_w_out"):
        p[k] = _each(w[k], lambda a: a if isinstance(a, Gathered) else a.astype(BF16))
    for k in ("ssd_conv_w", "rg_conv_w", "rg_w_a", "rg_w_i"):
        p[k] = _each(w[k], lambda a: a)

    def w_in(a):
        def zcols(n):
            return jnp.zeros((a.shape[0], n), a.dtype)

        return jnp.concatenate([a[:, :IN_DT_END], zcols(PROJ_CQ - IN_DT_END), a[:, IN_DT_END:IN_KR], zcols(ROPE_LO),
                                a[:, IN_KR:], zcols(LANES - ROPE_HI)], axis=1).astype(BF16)

    def q_up(a):
        a = a.reshape(MLA_Q_RANK, MLA_HEADS, MLA_NOPE + MLA_ROPE)
        return jnp.pad(a, ((0, 0), (0, 0), (0, LANES - MLA_NOPE - MLA_ROPE))).reshape(MLA_Q_RANK, MLA_HEADS * LANES).astype(BF16)

    def out_att(a):
        a = a[SSD_D_INNER:].reshape(MLA_HEADS, MLA_V, D_MODEL)
        return jnp.pad(a, ((0, 0), (LANES - MLA_V, 0), (0, 0))).reshape(MLA_HEADS * LANES, D_MODEL).astype(BF16)

    p["w_in"] = _each(w["w_in"], w_in)
    p["mla_w_q_up"] = _each(w["mla_w_q_up"], q_up)
    p["w_out_ssd"] = _each(w["w_out_ab"], lambda a: a[:SSD_D_INNER].astype(BF16))
    p["w_out_att"] = _each(w["w_out_ab"], out_att)
    return p


def _rope_tables(t):
    pos = (jnp.arange(t) - PAD).astype(F32)
    inv = ROPE_BASE ** (-jnp.arange(0, MLA_ROPE, 2, dtype=F32) / MLA_ROPE)
    ang = pos[:, None] * inv[None, :]
    c, s = jnp.cos(ang), jnp.sin(ang)
    one, zero = jnp.ones((t, MLA_NOPE), F32), jnp.zeros((t, MLA_NOPE), F32)
    tail = LANES - ROPE_HI
    return (jnp.concatenate([one, c, c, one[:, :tail]], axis=1), jnp.concatenate([zero, -s, s, zero[:, :tail]], axis=1))


GRAD_KEYS = GAINS + HEAD_VECS + LRU_VECS + ("w_up", "w_down", "mla_w_kv_up", "rg_w_x", "rg_w_y", "rg_w_out", "ssd_conv_w",
                                            "rg_conv_w", "rg_w_a", "rg_w_i", "w_in", "mla_w_q_up", "w_out_ssd", "w_out_att")


def _device_step(x, meta, target, p, hooks=None):
    t = PAD + N_META + x.shape[0]
    cos, sin = _rope_tables(t)
    h = jnp.concatenate([jnp.zeros((PAD, D_MODEL), F32), meta, x], axis=0)
    n_even, n_odd = (DEPTH + 1) // 2, DEPTH // 2
    saved = []
    for l in range(DEPTH):
        if l % 2 == 0:
            carried = hooks.forward_exchange() if hooks and l == 0 else None
            h, sm, arrived = _ssdmla_fwd(h, p, l, l // 2, cos, sin, carried)
            if carried is not None:
                p = hooks.after_forward_exchange(arrived)
        else:
            h, sm = _rglru_fwd(h, p, l, l // 2)
        h, sp = _mlp_fwd(h, p, l)
        saved.append((sm, sp))
    sq, dh = _loss_and_grad(h, target, "loss")
    per_layer = {"mix_pre_g": DEPTH, "mix_post_g": DEPTH, "mlp_pre_g": DEPTH, "mlp_post_g": DEPTH, "w_up": DEPTH, "w_down": DEPTH}
    grads = {k: [None] * per_layer.get(k, n_odd if k.startswith("rg_") else n_even) for k in GRAD_KEYS}
    for l in reversed(range(DEPTH)):
        sm, sp = saved[l]
        dh = _mlp_bwd(dh, sp, p, l, grads)
        if l % 2 == 0:
            carried = hooks.backward_exchange(grads, l) if hooks else None
            dh, arrived = _ssdmla_bwd(dh, sm, p, l, l // 2, cos, sin, grads, carried)
            if carried is not None:
                hooks.after_backward_exchange(arrived, l)
        else:
            dh = _rglru_bwd(dh, sm, p, l, l // 2, grads)
    return sq, dh, grads


MESH = pl.DeviceIdType.MESH
ANY = pl.BlockSpec(memory_space=pl.ANY)


def _mesh_pos():
    return lax.axis_index("x"), lax.axis_index("y"), lax.axis_index("c")


def _other_chips(x, y):
    return [(1 - x, y), (x, 1 - y), (1 - x, 1 - y)]


def _remote(src, dst, send_sems, recv_sems, k, to):
    return pltpu.make_async_remote_copy(src_ref=src, dst_ref=dst, send_sem=send_sems.at[k], recv_sem=recv_sems.at[k],
                                        device_id=to, device_id_type=MESH)


class Exchange:
    def __init__(self, ins, outs, aliases, n_sems, plan):
        self.ins, self.outs, self.aliases, self.n_sems, self.plan = list(ins), list(outs), dict(aliases), n_sems, plan


def _sems(n):
    return [pltpu.SemaphoreType.DMA((n,)), pltpu.SemaphoreType.DMA((n,))]


def _run_exchange(name, ex):
    ni, no = len(ex.ins), len(ex.outs)

    def body(*refs):
        sends = ex.plan(refs[:ni], refs[ni:ni + no], refs[-2], refs[-1], False)
        for cp in sends:
            cp.start()
        for cp in ex.plan(refs[:ni], refs[ni:ni + no], refs[-2], refs[-1], True):
            cp.wait_recv()
        for cp in sends:
            cp.wait_send()

    return pl.pallas_call(body, name=name, in_specs=[ANY] * ni, out_specs=[ANY] * no, out_shape=ex.outs,
                          input_output_aliases=ex.aliases, scratch_shapes=_sems(ex.n_sems))(*ex.ins)


def _carry_call(body, name, grid, in_specs, out_specs, out_shape, scratch_shapes, args, ex):
    if ex is None:
        res = pl.pallas_call(body, name=name, grid=grid, in_specs=in_specs, out_specs=out_specs, out_shape=out_shape,
                             scratch_shapes=scratch_shapes, compiler_params=_params(("arbitrary",) * len(grid)))(*args)
        return res, None
    ni, no, ns, xi, xo = len(in_specs), len(out_specs), len(scratch_shapes), len(ex.ins), len(ex.outs)

    def wrapped(*refs):
        ins, xin = refs[:ni], refs[ni:ni + xi]
        outs, xout = refs[ni + xi:ni + xi + no], refs[ni + xi + no:ni + xi + no + xo]
        scr, send_sems, recv_sems = refs[ni + xi + no + xo:-2], refs[-2], refs[-1]
        pid = [pl.program_id(d) for d in range(len(grid))]
        first = functools.reduce(jnp.logical_and, [p == 0 for p in pid])
        last = functools.reduce(jnp.logical_and, [p == g - 1 for p, g in zip(pid, grid)])

        @pl.when(first)
        def _():
            for cp in ex.plan(xin, xout, send_sems, recv_sems, False):
                cp.start()

        body(*ins, *outs, *scr)

        @pl.when(last)
        def _():
            for cp in ex.plan(xin, xout, send_sems, recv_sems, True):
                cp.wait_recv()
            for cp in ex.plan(xin, xout, send_sems, recv_sems, False):
                cp.wait_send()

    res = pl.pallas_call(
        wrapped, name=name, grid=grid, in_specs=list(in_specs) + [ANY] * xi, out_specs=list(out_specs) + [ANY] * xo,
        out_shape=list(out_shape) + ex.outs, scratch_shapes=list(scratch_shapes) + _sems(ex.n_sems),
        input_output_aliases={ni + i: no + o for i, o in ex.aliases.items()},
        compiler_params=_params(("arbitrary",) * len(grid)))(*args, *ex.ins)
    return res[:no], res[no:]


def _gather_ici(srcs, bufs, ranges):
    n = len(srcs)

    def plan(in_refs, out_refs, ss, rs, arrivals):
        x, y, c = _mesh_pos()
        cps = []
        for t, (l0, nl) in enumerate(ranges):
            if nl:
                s, o, lr = in_refs[t], out_refs[t], pl.ds(l0, nl)
                for j, (cx, cy) in enumerate(_other_chips(x, y)):
                    chip = 2 * cx + cy if arrivals else 2 * x + y
                    cps.append(_remote(s.at[lr, c], o.at[chip, lr, c], ss, rs, (N_CHIPS - 1) * t + j, (cx, cy, c)))
        return cps

    outs = [jax.ShapeDtypeStruct((N_CHIPS,) + s.shape, s.dtype) for s in srcs]
    if bufs is None:
        return Exchange(srcs, outs, {}, (N_CHIPS - 1) * n, plan)
    return Exchange(list(srcs) + list(bufs), outs, {n + t: t for t in range(n)}, (N_CHIPS - 1) * n, plan)


def _gather_d2d(srcs, bufs, ranges):
    n = len(srcs)

    def plan(in_refs, out_refs, ss, rs, arrivals):
        x, y, c = _mesh_pos()
        sib, me = (x, y, 1 - c), 2 * x + y
        cps = []
        for t, (l0, nl) in enumerate(ranges):
            if nl:
                s, o, lr = in_refs[t], out_refs[t], pl.ds(l0, nl)
                for j, (cx, cy) in enumerate(_other_chips(x, y)):
                    slot = o.at[2 * cx + cy, lr, c]
                    cps.append(_remote(slot, o.at[2 * cx + cy, lr, 1 - c] if arrivals else slot, ss, rs, N_CHIPS * t + j, sib))
                cps.append(_remote(s.at[lr], o.at[me, lr], ss, rs, N_CHIPS * t + N_CHIPS - 1, sib))
        return cps

    outs = [jax.ShapeDtypeStruct(b.shape, b.dtype) for b in bufs]
    return Exchange(list(srcs) + list(bufs), outs, {n + t: t for t in range(n)}, N_CHIPS * n, plan)


def _gather_chips(srcs, name):
    ranges = [(0, s.shape[0]) for s in srcs]
    bufs = _run_exchange(name + "_ici", _gather_ici(srcs, None, ranges))
    return _run_exchange(name + "_d2d", _gather_d2d(srcs, bufs, ranges))


def _pair_exchange(gs):
    def plan(in_refs, out_refs, ss, rs, arrivals):
        x, y, c = _mesh_pos()
        return [_remote(g.at[pl.ds(0, N_CHIPS), 1 - c], o, ss, rs, t, (x, y, 1 - c)) for t, (g, o) in enumerate(zip(in_refs, out_refs))]

    return Exchange(gs, [jax.ShapeDtypeStruct((g.shape[0],) + g.shape[2:], g.dtype) for g in gs], {}, len(gs), plan)


def _chip_exchange(ps, slots, qs, q_shapes):
    n = len(ps)
    kept = [g for g, q in enumerate(qs) if q is not None]

    def plan(in_refs, out_refs, ss, rs, arrivals):
        x, y, c = _mesh_pos()
        return [_remote(in_refs[t].at[2 * cx + cy], out_refs[g].at[j, li], ss, rs, (N_CHIPS - 1) * t + j, (cx, cy, c))
                for t, (g, li) in enumerate(slots) for j, (cx, cy) in enumerate(_other_chips(x, y))]

    return Exchange(list(ps) + [qs[g] for g in kept], q_shapes, {n + i: g for i, g in enumerate(kept)}, (N_CHIPS - 1) * n, plan)


def _pair_share(fs):
    def plan(in_refs, out_refs, ss, rs, arrivals):
        x, y, c = _mesh_pos()
        return [_remote(o.at[pl.ds(0, o.shape[0]), c], o.at[pl.ds(0, o.shape[0]), 1 - c if arrivals else c], ss, rs, t, (x, y, 1 - c))
                for t, o in enumerate(out_refs)]

    return Exchange(fs, [jax.ShapeDtypeStruct(f.shape, f.dtype) for f in fs], {t: t for t in range(len(fs))}, len(fs), plan)


SUM_BLOCK = 512 * 1024


def _sum_pair(g, ra, c, name):
    n, _, h, w = g.shape
    tr = _tile(h, max(16, SUM_BLOCK // w), 16)

    def body(c_ref, g_ref, r_ref, o_ref):
        o_ref[...] = (g_ref[0] + r_ref[...]).astype(o_ref.dtype)

    return pl.pallas_call(
        body, name=name,
        grid_spec=pltpu.PrefetchScalarGridSpec(
            num_scalar_prefetch=1, grid=(n, h // tr),
            in_specs=[pl.BlockSpec((1, 1, tr, w), lambda s, i, cr: (s, cr[0], i, 0)), pl.BlockSpec((1, tr, w), lambda s, i, cr: (s, i, 0))],
            out_specs=pl.BlockSpec((1, tr, w), lambda s, i, cr: (s, i, 0))),
        out_shape=jax.ShapeDtypeStruct((n, h, w), BF16),
        compiler_params=_params(("parallel", "parallel")),
    )(c.reshape(1).astype(jnp.int32), g, ra)


def _sum_chips(ps, q, pos, name):
    nc, nl, h, w = q.shape
    tr = _tile(h, max(16, SUM_BLOCK // (w * nl)), 16)

    def body(x_ref, y_ref, c_ref, *refs):
        q_ref, o_ref = refs[nl], refs[nl + 1]
        for l in range(nl):
            acc = refs[l][0].astype(F32)
            for j in range(nc):
                acc = acc + q_ref[j, l].astype(F32)
            o_ref[l] = acc

    return pl.pallas_call(
        body, name=name,
        grid_spec=pltpu.PrefetchScalarGridSpec(
            num_scalar_prefetch=3, grid=(h // tr,),
            in_specs=[pl.BlockSpec((1, tr, w), lambda i, x, y, c: (2 * x[0] + y[0], i, 0))] * nl
            + [pl.BlockSpec((nc, nl, tr, w), lambda i, x, y, c: (0, 0, i, 0))],
            out_specs=pl.BlockSpec((nl, None, tr, w), lambda i, x, y, c: (0, c[0], i, 0))),
        out_shape=jax.ShapeDtypeStruct((nl, 2, h, w), F32),
        compiler_params=_params(("parallel",)),
    )(*pos, *ps, q)


def _adamw(g, w, m, v, name):
    def f(r0, gg, ww, mm, vv):
        m2 = ADAM_B1 * mm + (1.0 - ADAM_B1) * gg
        v2 = ADAM_B2 * vv + (1.0 - ADAM_B2) * jnp.square(gg)
        m_hat = m2 / (1.0 - ADAM_B1 ** ADAM_STEP)
        v_hat = v2 / (1.0 - ADAM_B2 ** ADAM_STEP)
        return gg, -ADAM_LR * (m_hat / (jnp.sqrt(v_hat) + ADAM_EPS) + ADAM_WD * ww), m2, v2

    return _rowwise(name, f, [g, w, m, v], [], [(g.shape[1], F32)] * 4, tr=_tile(g.shape[0], 512))


WEIGHTS = (
    ("meta_tokens", (N_META, D_MODEL), 1), ("mix_pre_g", (DEPTH, D_MODEL), None), ("mix_post_g", (DEPTH, D_MODEL), None),
    ("mlp_pre_g", (DEPTH, D_MODEL), None), ("mlp_post_g", (DEPTH, D_MODEL), None), ("w_up", (DEPTH, D_MODEL, D_FF), 2),
    ("w_down", (DEPTH, D_FF, D_MODEL), 1), ("w_in", (2, D_MODEL, 3248), 2), ("ssd_conv_w", (2, CONV_K, SSD_CONV_CH), 2),
    ("ssd_conv_b", (2, SSD_CONV_CH), None), ("ssd_dt_bias", (2, SSD_HEADS), None), ("ssd_a_log", (2, SSD_HEADS), None),
    ("ssd_d", (2, SSD_HEADS), None), ("ssd_norm_g", (2, SSD_D_INNER), None), ("mla_q_norm_g", (2, MLA_Q_RANK), None),
    ("mla_w_q_up", (2, MLA_Q_RANK, MLA_HEADS * (MLA_NOPE + MLA_ROPE)), 2), ("mla_kv_norm_g", (2, MLA_KV_RANK), None),
    ("mla_w_kv_up", (2, MLA_KV_RANK, MLA_HEADS * (MLA_NOPE + MLA_V)), 2), ("w_out_ab", (2, SSD_D_INNER + MLA_HEADS * MLA_V, D_MODEL), 1),
    ("rg_w_x", (2, D_MODEL, LRU_WIDTH), 2), ("rg_w_y", (2, D_MODEL, LRU_WIDTH), 2), ("rg_conv_w", (2, CONV_K, LRU_WIDTH), 2),
    ("rg_conv_b", (2, LRU_WIDTH), 1), ("rg_w_a", (2, LRU_BLOCKS, LRU_BLOCK, LRU_BLOCK), None), ("rg_b_a", (2, LRU_WIDTH), 1),
    ("rg_w_i", (2, LRU_BLOCKS, LRU_BLOCK, LRU_BLOCK), None), ("rg_b_i", (2, LRU_WIDTH), 1), ("rg_lambda", (2, LRU_WIDTH), 1),
    ("rg_w_out", (2, LRU_WIDTH, D_MODEL), 1),
)
BIG = {"w_up": "col", "w_down": "row", "w_in": "col", "mla_w_q_up": "col", "mla_w_kv_up": "col", "w_out_ab": "row",
       "rg_w_x": "col", "rg_w_y": "col", "rg_w_out": "row"}
DIRECT = ("w_up", "w_down")
FLAT_QUANTUM = 2 * 16 * LANES
TABLE = {name: (shape, d) for name, shape, d in WEIGHTS}
SMALL_SHARDED = tuple(name for name, _, d in WEIGHTS if d is not None and name not in BIG)
REPLICATED = tuple(name for name, _, d in WEIGHTS if d is None)


def _chips_to_full(a, kind):
    if kind == "col":
        return jnp.moveaxis(a, 0, 2).reshape(a.shape[1], a.shape[2], -1)
    return jnp.moveaxis(a, 0, 1).reshape(a.shape[1], -1, a.shape[3])


def _full_to_chips(g, kind):
    if kind == "col":
        return jnp.moveaxis(g.reshape(g.shape[0], N_CHIPS, -1), 1, 0)
    return g.reshape(N_CHIPS, -1, g.shape[1])


def _chips_to_full_1(pc, kind):
    return jnp.moveaxis(pc, 0, 1).reshape(pc.shape[1], -1) if kind == "col" else pc.reshape(-1, pc.shape[2])


def _shard_shape(shape, d):
    return shape[:d] + (shape[d] // N_CHIPS,) + shape[d + 1:]


def _shard_major(full, d):
    s = full.shape
    return jnp.moveaxis(full.reshape(s[:d] + (N_CHIPS, s[d] // N_CHIPS) + s[d + 1:]), d, 0).reshape(N_CHIPS, -1)


def _from_shard_major(a, shape, d):
    ss = _shard_shape(shape, d)
    return jnp.moveaxis(a.reshape((N_CHIPS,) + ss), 0, d).reshape(shape)


def _pad_cols(a, quantum):
    n = a.shape[-1]
    return jnp.pad(a, [(0, 0)] * (a.ndim - 1) + [(0, -n % quantum)])


def _big_pieces(g):
    def w_in(a):
        return jnp.concatenate([a[:, :IN_DT_END], a[:, PROJ_CQ:PROJ_KR], a[:, PROJ_KR + ROPE_LO:PROJ_KR + ROPE_HI]], axis=1)

    def q_up(a):
        return a.reshape(MLA_Q_RANK, MLA_HEADS, LANES)[:, :, :MLA_NOPE + MLA_ROPE].reshape(MLA_Q_RANK, -1)

    def out_ab(sa):
        s, a = sa
        return jnp.concatenate([s, a.reshape(MLA_HEADS, LANES, D_MODEL)[:, LANES - MLA_V:, :].reshape(-1, D_MODEL)], axis=0)

    ident = lambda a: a
    full = {"w_down": _each(g["w_down"], ident), "w_in": _each(g["w_in"], w_in), "mla_w_q_up": _each(g["mla_w_q_up"], q_up),
            "mla_w_kv_up": _each(g["mla_w_kv_up"], ident),
            "w_out_ab": _each([None if s is None or a is None else (s, a) for s, a in zip(g["w_out_ssd"], g["w_out_att"])], out_ab),
            "rg_w_x": _each(g["rg_w_x"], ident), "rg_w_y": _each(g["rg_w_y"], ident), "rg_w_out": _each(g["rg_w_out"], ident)}
    return {name: (list(g[name]) if name == "w_up" else _each(full[name], lambda a, k=BIG[name]: _full_to_chips(a, k))) for name in BIG}


def _small_grads(g, dh):
    out = {k: jnp.stack(g[k])[:, 0, :] for k in GAINS}
    for k in HEAD_VECS:
        out[k] = jnp.stack(g[k])[:, 0, :SSD_HEADS]
    for k in LRU_VECS:
        out[k] = jnp.stack(g[k]).reshape(-1, LRU_WIDTH)
    for k in ("ssd_conv_w", "rg_conv_w", "rg_w_a", "rg_w_i"):
        out[k] = jnp.stack(g[k])
    out["meta_tokens"] = dh[PAD:PAD + N_META]
    return out


def _natural_grads(g, dh):
    out = _small_grads(g, dh)
    for name, pcs in _big_pieces(g).items():
        out[name] = jnp.stack([_chips_to_full_1(pc, BIG[name]) for pc in pcs])
    return out


class StepExchanges:
    def __init__(self, w):
        self.w = w
        self.c = lax.axis_index("c")
        self.riding, self.ras = {}, {}
        small = _pad_cols(jnp.concatenate([w[n].reshape(-1) for n in SMALL_SHARDED]), FLAT_QUANTUM).reshape(1, 2, -1, LANES)
        self.srcs = [self._halves(w[n].astype(BF16)) for n in BIG] + [small]
        first = {n: (0, 1 if n in ("w_in", "mla_w_q_up", "mla_w_kv_up", "w_out_ab") else 0) for n in BIG}
        self.first = [first[n] for n in BIG] + [(0, 1)]
        self.rest = [(nl, TABLE[n][0][0] - nl) for n, (_, nl) in zip(BIG, self.first)] + [(0, 0)]
        bufs = _run_exchange("gather_first_ici", _gather_ici(self.srcs, None, self.first))
        self.bufs = _run_exchange("gather_first_d2d", _gather_d2d(self.srcs, bufs, self.first))

    @staticmethod
    def _halves(a):
        return a.reshape(a.shape[0], 2, a.shape[1] // 2, a.shape[2])

    def params(self, ranges):
        w = self.w
        full = {n: w[n] for n in REPLICATED}
        for name, buf, (l0, nl) in zip(BIG, self.bufs, ranges):
            a = buf.reshape(buf.shape[:2] + (-1, buf.shape[4]))
            have = range(l0, l0 + nl)
            if name in DIRECT:
                full[name] = [Gathered(a, BIG[name], l) if l in have else None for l in range(a.shape[1])]
            else:
                full[name] = [_chips_to_full(a[:, l:l + 1], BIG[name])[0] if l in have else None for l in range(a.shape[1])]
        got, off = self.bufs[-1].reshape(N_CHIPS, -1), 0
        for name in SMALL_SHARDED:
            shape, d = TABLE[name]
            n = int(np.prod(_shard_shape(shape, d)))
            full[name] = _from_shard_major(got[:, off:off + n], shape, d)
            off += n
        self.meta = full.pop("meta_tokens")
        return _layout_params(full)

    def forward_exchange(self):
        return _gather_ici(self.srcs, self.bufs, self.rest)

    def after_forward_exchange(self, arrived):
        self.bufs = _run_exchange("gather_rest_d2d", _gather_d2d(self.srcs, arrived, self.rest))
        return self.params([(0, TABLE[n][0][0]) for n in BIG])

    def _pair_sums(self, pieces, tag):
        keys = list(pieces)
        ras = _run_exchange("grads_pair_exchange_" + tag, _pair_exchange([pieces[k] for k in keys]))
        return {k: _sum_pair(pieces[k], ra, self.c, "grads_pair_sum") for k, ra in zip(keys, ras)}

    def _q_shapes(self):
        return [jax.ShapeDtypeStruct((N_CHIPS - 1, s.shape[0]) + s.shape[2:], BF16) for s in self.srcs[:-1]]

    def backward_exchange(self, grads, layer):
        big = _big_pieces(grads)
        pieces = {(g, l): pc.reshape(N_CHIPS, 2, pc.shape[1] // 2, pc.shape[2]) for g, name in enumerate(BIG)
                  for l, pc in enumerate(big[name]) if pc is not None and (g, l) not in self.riding}
        if layer > 0:
            self.riding = pieces
            return _pair_exchange(list(pieces.values()))
        self.ps = {k: _sum_pair(self.riding[k], ra, self.c, "grads_pair_sum") for k, ra in self.ras.items()}
        self.ps.update(self._pair_sums(pieces, "early"))
        self.early = list(self.ps)
        return _chip_exchange([self.ps[k] for k in self.early], self.early, [None] * len(BIG), self._q_shapes())

    def after_backward_exchange(self, arrived, layer):
        if layer > 0:
            self.ras = dict(zip(self.riding, arrived))
        else:
            self.qs = list(arrived)

    def finish(self, grads, dh):
        big, small = _big_pieces(grads), _small_grads(grads, dh)
        pieces = {(g, l): pc.reshape(N_CHIPS, 2, pc.shape[1] // 2, pc.shape[2])
                  for g, name in enumerate(BIG) for l, pc in enumerate(big[name]) if (g, l) not in self.ps}
        sharded = jnp.concatenate([_shard_major(small[n], TABLE[n][1]) for n in SMALL_SHARDED], axis=1)
        rep = _pad_cols(jnp.concatenate([small[n].reshape(-1) for n in REPLICATED]), N_CHIPS * FLAT_QUANTUM)
        n_sh, n_rep = sharded.shape[1], rep.shape[0] // N_CHIPS
        flat = _pad_cols(jnp.concatenate([sharded, rep.reshape(N_CHIPS, n_rep)], axis=1), FLAT_QUANTUM)
        pieces[(len(BIG), 0)] = flat.reshape(N_CHIPS, 2, -1, LANES)
        late = self._pair_sums(pieces, "late")
        self.ps.update(late)
        keys = list(late)
        small_q = jax.ShapeDtypeStruct((N_CHIPS - 1, 1) + late[(len(BIG), 0)].shape[1:], BF16)
        qs = _run_exchange("grads_chip_exchange_late",
                           _chip_exchange([late[k] for k in keys], keys, self.qs + [None], self._q_shapes() + [small_q]))
        pos = [lax.axis_index(a).reshape(1).astype(jnp.int32) for a in ("x", "y", "c")]
        sums = [_sum_chips([self.ps[(g, l)] for l in range(q.shape[1])], q, pos, "grads_chip_sum") for g, q in enumerate(qs)]
        outs = _run_exchange("grads_pair_share", _pair_share(sums))
        out = {name: o.reshape(o.shape[0], -1, o.shape[3]) for name, o in zip(BIG, outs)}
        f = outs[-1].reshape(-1)
        rep_all = _gather_chips([f[n_sh:n_sh + n_rep].reshape(1, 2, -1, LANES)], "grads_gather_replicated")[0].reshape(-1)
        off = 0
        for name in SMALL_SHARDED:
            ss = _shard_shape(*TABLE[name])
            n = int(np.prod(ss))
            out[name] = f[off:off + n].reshape(ss)
            off += n
        off = 0
        for name in REPLICATED:
            shape = TABLE[name][0]
            n = int(np.prod(shape))
            out[name] = rep_all[off:off + n].reshape(shape)
            off += n
        return out


def kernel(x, meta_tokens, mix_pre_g, mix_post_g, mlp_pre_g, mlp_post_g, w_up, w_down, w_in, ssd_conv_w, ssd_conv_b, ssd_dt_bias, ssd_a_log, ssd_d, ssd_norm_g, mla_q_norm_g, mla_w_q_up, mla_kv_norm_g, mla_w_kv_up, w_out_ab, rg_w_x, rg_w_y, rg_conv_w, rg_conv_b, rg_w_a, rg_b_a, rg_w_i, rg_b_i, rg_lambda, rg_w_out, loss_target, m_meta_tokens, m_mix_pre_g, m_mix_post_g, m_mlp_pre_g, m_mlp_post_g, m_w_up, m_w_down, m_w_in, m_ssd_conv_w, m_ssd_conv_b, m_ssd_dt_bias, m_ssd_a_log, m_ssd_d, m_ssd_norm_g, m_mla_q_norm_g, m_mla_w_q_up, m_mla_kv_norm_g, m_mla_w_kv_up, m_w_out_ab, m_rg_w_x, m_rg_w_y, m_rg_conv_w, m_rg_conv_b, m_rg_w_a, m_rg_b_a, m_rg_w_i, m_rg_b_i, m_rg_lambda, m_rg_w_out, v_meta_tokens, v_mix_pre_g, v_mix_post_g, v_mlp_pre_g, v_mlp_post_g, v_w_up, v_w_down, v_w_in, v_ssd_conv_w, v_ssd_conv_b, v_ssd_dt_bias, v_ssd_a_log, v_ssd_d, v_ssd_norm_g, v_mla_q_norm_g, v_mla_w_q_up, v_mla_kv_norm_g, v_mla_w_kv_up, v_w_out_ab, v_rg_w_x, v_rg_w_y, v_rg_conv_w, v_rg_conv_b, v_rg_w_a, v_rg_b_a, v_rg_w_i, v_rg_b_i, v_rg_lambda, v_rg_w_out):
    names = [n for n, _, _ in WEIGHTS]
    w = dict(zip(names, (meta_tokens, mix_pre_g, mix_post_g, mlp_pre_g, mlp_post_g, w_up, w_down, w_in, ssd_conv_w, ssd_conv_b, ssd_dt_bias, ssd_a_log, ssd_d, ssd_norm_g, mla_q_norm_g, mla_w_q_up, mla_kv_norm_g, mla_w_kv_up, w_out_ab, rg_w_x, rg_w_y, rg_conv_w, rg_conv_b, rg_w_a, rg_b_a, rg_w_i, rg_b_i, rg_lambda, rg_w_out)))
    m = dict(zip(names, (m_meta_tokens, m_mix_pre_g, m_mix_post_g, m_mlp_pre_g, m_mlp_post_g, m_w_up, m_w_down, m_w_in, m_ssd_conv_w, m_ssd_conv_b, m_ssd_dt_bias, m_ssd_a_log, m_ssd_d, m_ssd_norm_g, m_mla_q_norm_g, m_mla_w_q_up, m_mla_kv_norm_g, m_mla_w_kv_up, m_w_out_ab, m_rg_w_x, m_rg_w_y, m_rg_conv_w, m_rg_conv_b, m_rg_w_a, m_rg_b_a, m_rg_w_i, m_rg_b_i, m_rg_lambda, m_rg_w_out)))
    v = dict(zip(names, (v_meta_tokens, v_mix_pre_g, v_mix_post_g, v_mlp_pre_g, v_mlp_post_g, v_w_up, v_w_down, v_w_in, v_ssd_conv_w, v_ssd_conv_b, v_ssd_dt_bias, v_ssd_a_log, v_ssd_d, v_ssd_norm_g, v_mla_q_norm_g, v_mla_w_q_up, v_mla_kv_norm_g, v_mla_w_kv_up, v_w_out_ab, v_rg_w_x, v_rg_w_y, v_rg_conv_w, v_rg_conv_b, v_rg_w_a, v_rg_b_a, v_rg_w_i, v_rg_b_i, v_rg_lambda, v_rg_w_out)))
    ex = StepExchanges(w)
    p = ex.params(ex.first)
    sq, dh, grads = _device_step(x[0], ex.meta, loss_target[0], p, hooks=ex)
    loss = lax.psum(0.5 * sq[0, 0] / D_MODEL, ("x", "y", "c"))
    g = ex.finish(grads, dh)
    grad, delta, new_m, new_v = {}, {}, {}, {}
    for name in names:
        shape = g[name].shape
        two_d = (int(np.prod(shape[:-1])), shape[-1])
        res = _adamw(g[name].reshape(two_d), w[name].reshape(two_d), m[name].reshape(two_d), v[name].reshape(two_d), "adamw")
        grad[name], delta[name], new_m[name], new_v[name] = (r.reshape(shape) for r in res)
    grad_x = dh[PAD + N_META:][None]
    return (loss, grad_x, *[grad[n] for n in names], *[delta[n] for n in names], *[new_m[n] for n in names], *[new_v[n] for n in names])
```

```python
import functools

import jax
import jax.numpy as jnp
import numpy as np
from jax import lax
from jax.experimental import pallas as pl
from jax.experimental.pallas import tpu as pltpu

F32 = jnp.float32
BF16 = jnp.bfloat16

D_MODEL = 1024
DEPTH = 4
N_META = 16
CHUNK = 128
PAD = CHUNK - N_META
EPS = 1e-6
SSD_HEADS = 16
SSD_HEAD_DIM = 64
SSD_D_INNER = SSD_HEADS * SSD_HEAD_DIM
SSD_GROUPS = 2
SSD_STATE = 128
SSD_CONV_CH = SSD_D_INNER + 2 * SSD_GROUPS * SSD_STATE
MLA_HEADS = 16
MLA_NOPE = 64
MLA_ROPE = 32
MLA_V = 64
MLA_Q_RANK = 384
MLA_KV_RANK = 256
ROPE_BASE = 10000.0
LRU_WIDTH = 1280
LRU_BLOCKS = 10
LRU_BLOCK = 128
LRU_C = 8.0
D_FF = 4 * D_MODEL
ADAM_LR, ADAM_B1, ADAM_B2, ADAM_EPS, ADAM_WD, ADAM_STEP = 0.001, 0.9, 0.999, 1e-08, 0.01, 10

LANES = 128
VMEM_LIMIT = 56 * 1024 * 1024
MM_VMEM_BUDGET = 40 * 1024 * 1024
PROJ_Z, PROJ_XBC, PROJ_DT, PROJ_CQ, PROJ_CKV, PROJ_KR = 0, 1024, 2560, 2688, 3072, 3328
PROJ_W = 3456


def _tile(n, cap, mult=8):
    for t in range(min(n, cap), 0, -1):
        if n % t == 0 and t % mult == 0:
            return t
    return n


def _params(sem):
    return pltpu.CompilerParams(dimension_semantics=sem, vmem_limit_bytes=VMEM_LIMIT)


def _full_spec(shape, ngrid):
    nd = len(shape)
    if ngrid == 1:
        return pl.BlockSpec(shape, lambda i: (0,) * nd)
    if ngrid == 2:
        return pl.BlockSpec(shape, lambda i, j: (0,) * nd)
    return pl.BlockSpec(shape, lambda i, j, k: (0,) * nd)


_DIMS = {"nn": (((1,), (0,)), ((), ())), "nt": (((1,), (1,)), ((), ())), "tn": (((0,), (0,)), ((), ()))}


class Gathered:
    def __init__(self, arr, kind, layer):
        self.arr, self.kind, self.layer = arr, kind, layer
        _, _, r, c = arr.shape
        self.shape = (r, N_CHIPS * c) if kind == "col" else (N_CHIPS * r, c)


N_CHIPS = 4


def _mm(a, b, mode, name, out_dtype=F32, add=None, out_chip_major=False, extra=(), vecs=(), slots=(), post=None, out_dtypes=None):
    if mode == "nn":
        (m, kc), (_, n) = a.shape, b.shape
    elif mode == "nt":
        (m, kc), (n, _) = a.shape, b.shape
    else:
        (kc, m), (_, n) = a.shape, b.shape
    n_tile = n // N_CHIPS if out_chip_major else n
    across = isinstance(b, Gathered) and (mode, b.kind) in (("nn", "row"), ("nt", "col"))
    if mode == "tn":
        tm, tk = _tile(m, 1024, LANES), kc
        fits = [c for c in (1280, 1152, 1024, 768, 640, 512) if n_tile % c == 0 and
                2 * kc * (tm * a.dtype.itemsize + c * b.dtype.itemsize) + 2 * tm * c * 4 <= MM_VMEM_BUDGET]
        tn = fits[0] if fits else _tile(n_tile, 1280, LANES)
        if not fits:
            tk = _tile(kc, 1408, LANES)
    else:
        tn = _tile(n_tile, 1280, LANES)
        tk = _tile(kc, 4096, LANES)
        tm = _tile(m, 1056 if tk <= 1024 else 528, 16)
    nk = kc // tk
    if mode == "tn":
        a_spec = pl.BlockSpec((tk, tm), lambda i, j, k: (k, i))
    else:
        a_spec = pl.BlockSpec((tm, tk), lambda i, j, k: (i, k))
    b_arrs = [b]
    if isinstance(b, Gathered):
        layer = b.layer
        sr, sc = b.arr.shape[2:]
        if across:
            assert nk == 1 and kc == N_CHIPS * (sr if b.kind == "row" else sc)
            b_arrs = [b.arr] * N_CHIPS
            if b.kind == "row":
                b_specs = [pl.BlockSpec((None, None, sr, tn), lambda i, j, k, s=s: (s, layer, 0, j)) for s in range(N_CHIPS)]
            else:
                b_specs = [pl.BlockSpec((None, None, tn, sc), lambda i, j, k, s=s: (s, layer, j, 0)) for s in range(N_CHIPS)]
        else:
            b_arrs = [b.arr]
            br, bc = (tk, tn) if mode == "nn" else (tn, tk)
            assert mode in ("nn", "nt") and sr % br == 0 and sc % bc == 0

            def b_map(i, j, k):
                r, c = (k, j) if mode == "nn" else (j, k)
                if b.kind == "col":
                    return ((c * bc) // sc, layer, r, ((c * bc) % sc) // bc)
                return ((r * br) // sr, layer, ((r * br) % sr) // br, c)

            b_specs = [pl.BlockSpec((None, None, br, bc), b_map)]
    elif mode == "nt":
        b_specs = [pl.BlockSpec((tn, tk), lambda i, j, k: (j, k))]
    else:
        b_specs = [pl.BlockSpec((tk, tn), lambda i, j, k: (k, j))]
    nb = len(b_arrs)
    dims = _DIMS[mode]
    if out_chip_major:
        ns = n // N_CHIPS
        o_spec = pl.BlockSpec((None, tm, tn), lambda i, j, k: ((j * tn) // ns, i, ((j * tn) % ns) // tn))
        o_shape = jax.ShapeDtypeStruct((N_CHIPS, m, ns), out_dtype)
    else:
        o_spec = pl.BlockSpec((tm, tn), lambda i, j, k: (i, j))
        o_shape = jax.ShapeDtypeStruct((m, n), out_dtype)
    extra = list(extra) + ([add] if add is not None else [])
    if add is not None:
        post = lambda v, x: (v + x,)
    vecs, slots = list(vecs), list(slots)
    nx = len(extra) + len(vecs) + len(slots)
    out_dtypes = out_dtypes or [out_dtype]
    no = len(out_dtypes)

    def body(a_ref, *rest):
        b_refs, rest = rest[:nb], rest[nb:]
        o_refs, acc = rest[nx:nx + no], rest[nx + no:]
        if across:
            w = kc // N_CHIPS
            p = functools.reduce(jnp.add, [
                lax.dot_general(a_ref[:, s * w:(s + 1) * w].astype(BF16), b_refs[s][...].astype(BF16), dims, preferred_element_type=F32)
                for s in range(N_CHIPS)])
        else:
            p = lax.dot_general(a_ref[...].astype(BF16), b_refs[0][...].astype(BF16), dims, preferred_element_type=F32)

        def emit(v):
            res = post(v, *[r[...] for r in rest[:nx]]) if post else (v,)
            for o_ref, r in zip(o_refs, res):
                o_ref[...] = r.astype(o_ref.dtype)

        if nk == 1:
            emit(p)
        else:
            k = pl.program_id(2)

            @pl.when(k == 0)
            def _():
                acc[0][...] = p

            @pl.when(k > 0)
            def _():
                acc[0][...] += p

            @pl.when(k == nk - 1)
            def _():
                emit(acc[0][...])

    res = pl.pallas_call(
        body, name=name, grid=(m // tm, n // tn, nk),
        in_specs=[a_spec] + b_specs + [o_spec] * len(extra) + [pl.BlockSpec((1, tn), lambda i, j, k: (0, j))] * len(vecs)
        + [pl.BlockSpec((tm, LANES), lambda i, j, k: (i, 0))] * len(slots),
        out_specs=[o_spec] * no,
        out_shape=[jax.ShapeDtypeStruct(o_shape.shape, dt) for dt in out_dtypes],
        scratch_shapes=[pltpu.VMEM((tm, tn), F32)] if nk > 1 else [],
        compiler_params=_params(("parallel", "parallel", "arbitrary")),
    )(a, *b_arrs, *extra, *vecs, *slots)
    return res[0] if no == 1 else res


def _rowarg(r):
    return r if isinstance(r, tuple) else (r, r.shape[1], 0)


def _rowspec(r, tr, ncol):
    _, w, cb = r
    if ncol > 1:
        return pl.BlockSpec((tr, w // ncol), lambda j, i: (i, j))
    return pl.BlockSpec((tr, w), lambda j, i: (i, cb))


def _rowwise(name, f, rows, params, outs, tr=None, ncol=1):
    rows = [_rowarg(r) for r in rows]
    t = rows[0][0].shape[0]
    tr = tr or _tile(t, 528)
    nr, npm = len(rows), len(params)

    def body(*refs):
        vals = [r[...] for r in refs[:nr]] + [(p[0] if ncol > 1 else p[...]) for p in refs[nr:nr + npm]]
        res = f(pl.program_id(1) * tr, *vals)
        for o_ref, v in zip(refs[nr + npm:], res):
            o_ref[...] = v.astype(o_ref.dtype)

    def pspec(p):
        if ncol > 1:
            return pl.BlockSpec((1,) + p.shape[1:], lambda j, i, n=p.ndim: (j,) + (0,) * (n - 1))
        return _full_spec(p.shape, 2)

    return pl.pallas_call(
        body, name=name, grid=(ncol, t // tr),
        in_specs=[_rowspec(r, tr, ncol) for r in rows] + [pspec(p) for p in params],
        out_specs=[pl.BlockSpec((tr, w // ncol), lambda j, i: (i, j)) for w, _ in outs],
        out_shape=[jax.ShapeDtypeStruct((t, w), dt) for w, dt in outs],
        compiler_params=_params(("parallel", "parallel")),
    )(*[r[0] for r in rows], *params)


def _rowwise_vjp(name, f, rows, params, cts, tr=None, ncol=1, row_dtypes=None):
    rows = [_rowarg(r) for r in rows]
    cts = [_rowarg(c) for c in cts]
    t = rows[0][0].shape[0]
    tr = tr or _tile(t, 528)
    nr, npm, nc = len(rows), len(params), len(cts)
    row_dtypes = row_dtypes or [F32] * nr

    def body(*refs):
        i = pl.program_id(1)
        vals = [r[...] for r in refs[:nr]] + [(p[0] if ncol > 1 else p[...]) for p in refs[nr:nr + npm]]
        ct = tuple(c[...].astype(F32) for c in refs[nr + npm:nr + npm + nc])
        _, vjp = jax.vjp(lambda *a: tuple(f(i * tr, *a)), *vals)
        g = vjp(ct)
        outs = refs[nr + npm + nc:]
        for o_ref, v in zip(outs[:nr], g[:nr]):
            o_ref[...] = v.astype(o_ref.dtype)
        pg = [(v[None] if ncol > 1 else v) for v in g[nr:]]

        @pl.when(i == 0)
        def _():
            for o_ref, v in zip(outs[nr:], pg):
                o_ref[...] = v

        @pl.when(i > 0)
        def _():
            for o_ref, v in zip(outs[nr:], pg):
                o_ref[...] += v

    def pspec(p):
        if ncol > 1:
            return pl.BlockSpec((1,) + p.shape[1:], lambda j, i, n=p.ndim: (j,) + (0,) * (n - 1))
        return _full_spec(p.shape, 2)

    res = pl.pallas_call(
        body, name=name, grid=(ncol, t // tr),
        in_specs=[_rowspec(r, tr, ncol) for r in rows] + [pspec(p) for p in params] + [_rowspec(c, tr, ncol) for c in cts],
        out_specs=[pl.BlockSpec((tr, w // ncol), lambda j, i: (i, j)) for _, w, _ in rows] + [pspec(p) for p in params],
        out_shape=[jax.ShapeDtypeStruct((t, w), dt) for (_, w, _), dt in zip(rows, row_dtypes)]
        + [jax.ShapeDtypeStruct(p.shape, F32) for p in params],
        compiler_params=_params(("parallel", "arbitrary")),
    )(*[r[0] for r in rows], *params, *[c[0] for c in cts])
    return res[:nr], res[nr:]


def _valid(row0, tr):
    return (row0 + lax.broadcasted_iota(jnp.int32, (tr, 1), 0)) >= PAD


def _rms(x, g):
    return x * lax.rsqrt(jnp.mean(x * x, axis=-1, keepdims=True) + EPS) * g


def _softplus(x):
    return jnp.where(x < -15.0, jnp.exp(x), jnp.maximum(x, 0.0) + jnp.log(1.0 + jnp.exp(-jnp.abs(x))))


def _neg_expm1(z):
    return jnp.where(z > -0.01, -z * (1.0 + z * (0.5 + z * (1.0 / 6.0))), 1.0 - jnp.exp(z))


def _prenorm(h, g, name):
    return _rowwise(name, lambda r0, x, gg: (_rms(x, gg),), [h], [g], [(_rowarg(h)[1], BF16)])[0]


def _post_residual(m, h, g):
    assert m.shape[1] == D_MODEL
    return m, h + _rms(m, g)


def _postnorm_bwd(m, g, dh, name):
    (dm,), (dg,) = _rowwise_vjp(name, lambda r0, mm, gg: (_rms(mm, gg),), [m], [g], [dh], row_dtypes=[BF16])
    return dm, dg


def _prenorm_bwd_add(h, g, dhns, dh, name):
    t, w = h.shape
    tr = _tile(t, 528)
    nd = len(dhns)

    def body(h_ref, g_ref, *refs):
        dh_ref, o_ref, dg_ref = refs[nd:]
        i = pl.program_id(0)
        _, vjp = jax.vjp(_rms, h_ref[...], g_ref[...])
        dhn = refs[0][...].astype(F32)
        for r in refs[1:nd]:
            dhn = dhn + r[...].astype(F32)
        dx, dg = vjp(dhn)
        o_ref[...] = dh_ref[...] + dx

        @pl.when(i == 0)
        def _():
            dg_ref[...] = dg

        @pl.when(i > 0)
        def _():
            dg_ref[...] += dg

    row = pl.BlockSpec((tr, w), lambda i: (i, 0))
    return pl.pallas_call(
        body, name=name, grid=(t // tr,), in_specs=[row, _full_spec(g.shape, 1)] + [row] * (nd + 1),
        out_specs=[row, _full_spec(g.shape, 1)],
        out_shape=[jax.ShapeDtypeStruct((t, w), F32), jax.ShapeDtypeStruct(g.shape, F32)],
        compiler_params=_params(("arbitrary",)),
    )(h, g, *dhns, dh)


def _loss_and_grad(h, target, name):
    t, w = h.shape
    nb = t // CHUNK

    def body(h_ref, t_ref, s_ref, dh_ref):
        i = pl.program_id(0)

        @pl.when(i == 0)
        def _():
            s_ref[...] = jnp.zeros_like(s_ref)
            dh_ref[...] = jnp.zeros_like(dh_ref)

        @pl.when(i > 0)
        def _():
            err = h_ref[...] - t_ref[...]
            s_ref[...] += jnp.sum(err * err)
            dh_ref[...] = err * (1.0 / w)

    return pl.pallas_call(
        body, name=name, grid=(nb,),
        in_specs=[pl.BlockSpec((CHUNK, w), lambda i: (i, 0)), pl.BlockSpec((CHUNK, w), lambda i: (jnp.maximum(i - 1, 0), 0))],
        out_specs=[_full_spec((1, LANES), 1), pl.BlockSpec((CHUNK, w), lambda i: (i, 0))],
        out_shape=[jax.ShapeDtypeStruct((1, LANES), F32), jax.ShapeDtypeStruct((t, w), F32)],
        compiler_params=_params(("arbitrary",)),
    )(h, target)


def _mlp_fwd(h, p, l):
    hn = _prenorm(h, p["mlp_pre_g"][l], "mlp_prenorm")
    a, u = _mm(hn, p["w_up"][l], "nn", "mlp_up", post=lambda v: (v, jnp.square(jnp.maximum(v, 0.0))), out_dtypes=[BF16, BF16])
    d, h2 = _mm(u, p["w_down"][l], "nn", "mlp_down", extra=[h], vecs=[p["mlp_post_g"][l]], post=_post_residual, out_dtypes=[F32, F32])
    return h2, (h, hn, a, u, d)


def _mlp_bwd(dh, saved, p, l, grads):
    h, hn, a, u, d = saved
    dd, grads["mlp_post_g"][l] = _postnorm_bwd(d, p["mlp_post_g"][l], dh, "mlp_postnorm_bwd")
    grads["w_down"][l] = _mm(u, dd, "tn", "mlp_down_dw")
    da = _mm(dd, p["w_down"][l], "nt", "mlp_down_dx", extra=[a], post=lambda v, x: (2.0 * jnp.maximum(x.astype(F32), 0.0) * v,),
             out_dtypes=[BF16])
    grads["w_up"][l] = _mm(hn, da, "tn", "mlp_up_dw", out_chip_major=True)
    dhn = _mm(da, p["w_up"][l], "nt", "mlp_up_dx")
    dh, grads["mlp_pre_g"][l] = _prenorm_bwd_add(h, p["mlp_pre_g"][l], [dhn], dh, "mlp_prenorm_bwd")
    return dh


def _dot(a, b, mode):
    return lax.dot_general(a.astype(BF16), b.astype(BF16), _DIMS[mode], preferred_element_type=F32)


@jax.custom_vjp
def _bnn(a, b):
    return _dot(a, b, "nn")


_bnn.defvjp(lambda a, b: (_dot(a, b, "nn"), (a, b)), lambda r, ct: (_dot(ct, r[1], "nt"), _dot(r[0], ct, "tn")))


@jax.custom_vjp
def _bnt(a, b):
    return _dot(a, b, "nt")


_bnt.defvjp(lambda a, b: (_dot(a, b, "nt"), (a, b)), lambda r, ct: (_dot(ct, r[1], "nn"), _dot(ct, r[0], "tn")))


@jax.custom_vjp
def _btn(a, b):
    return _dot(a, b, "tn")


_btn.defvjp(lambda a, b: (_dot(a, b, "tn"), (a, b)), lambda r, ct: (_dot(r[1], ct, "nt"), _dot(r[0], ct, "nn")))


CONV_K = 4
HALO = 8


def _conv_fwd(x, w, b, name, cw, c0=0):
    t, c = x.shape[0], w.shape[1]
    tr = _tile(t, 528)
    hb = tr // HALO

    def body(x_ref, halo_ref, w_ref, b_ref, o_ref, ext):
        i = pl.program_id(1)
        ext[pl.ds(0, HALO), :] = jnp.where(i > 0, halo_ref[...], 0.0)
        ext[pl.ds(HALO, tr), :] = x_ref[...]
        acc = jnp.broadcast_to(b_ref[...], (tr, cw))
        for k in range(CONV_K):
            acc = acc + w_ref[pl.ds(k, 1), :] * ext[pl.ds(HALO - (CONV_K - 1) + k, tr), :]
        o_ref[...] = acc

    return pl.pallas_call(
        body, name=name, grid=(c // cw, t // tr),
        in_specs=[pl.BlockSpec((tr, cw), lambda j, i: (i, c0 + j)),
                  pl.BlockSpec((HALO, cw), lambda j, i: (jnp.maximum(i * hb - 1, 0), c0 + j)),
                  pl.BlockSpec((CONV_K, cw), lambda j, i: (0, j)), pl.BlockSpec((1, cw), lambda j, i: (0, j))],
        out_specs=pl.BlockSpec((tr, cw), lambda j, i: (i, j)),
        out_shape=jax.ShapeDtypeStruct((t, c), F32),
        scratch_shapes=[pltpu.VMEM((tr + HALO, cw), F32)],
        compiler_params=_params(("parallel", "parallel")),
    )(x, x, w, b)


def _conv_bwd(x, w, dy, name, cw, c0=0):
    t, c = x.shape[0], w.shape[1]
    tr = _tile(t, 528)
    hb = tr // HALO
    nb = t // tr

    def body(x_ref, xh_ref, w_ref, dy_ref, dyh_ref, dx_ref, dw_ref, db_ref, xe, de):
        c = cw
        i = pl.program_id(1)
        xe[pl.ds(0, HALO), :] = jnp.where(i > 0, xh_ref[...], 0.0)
        xe[pl.ds(HALO, tr), :] = x_ref[...]
        de[pl.ds(0, tr), :] = dy_ref[...]
        de[pl.ds(tr, HALO), :] = jnp.where(i < nb - 1, dyh_ref[...], 0.0)
        dy = dy_ref[...]
        acc = jnp.zeros((tr, c), F32)
        dw = jnp.zeros((CONV_K, c), F32)
        rows = lax.broadcasted_iota(jnp.int32, (CONV_K, 1), 0)
        for k in range(CONV_K):
            acc = acc + w_ref[pl.ds(k, 1), :] * de[pl.ds(CONV_K - 1 - k, tr), :]
            dwk = jnp.sum(dy * xe[pl.ds(HALO - (CONV_K - 1) + k, tr), :], axis=0, keepdims=True)
            dw = dw + jnp.where(rows == k, dwk, 0.0)
        dx_ref[...] = jnp.where(_valid(i * tr, tr), acc, 0.0).astype(dx_ref.dtype)
        db = jnp.sum(dy, axis=0, keepdims=True)

        @pl.when(i == 0)
        def _():
            dw_ref[...] = dw
            db_ref[...] = db

        @pl.when(i > 0)
        def _():
            dw_ref[...] += dw
            db_ref[...] += db

    row = pl.BlockSpec((tr, cw), lambda j, i: (i, j))
    return pl.pallas_call(
        body, name=name, grid=(c // cw, nb),
        in_specs=[pl.BlockSpec((tr, cw), lambda j, i: (i, c0 + j)),
                  pl.BlockSpec((HALO, cw), lambda j, i: (jnp.maximum(i * hb - 1, 0), c0 + j)),
                  pl.BlockSpec((CONV_K, cw), lambda j, i: (0, j)),
                  row, pl.BlockSpec((HALO, cw), lambda j, i: (jnp.minimum((i + 1) * hb, t // HALO - 1), j))],
        out_specs=[row, pl.BlockSpec((CONV_K, cw), lambda j, i: (0, j)), pl.BlockSpec((1, cw), lambda j, i: (0, j))],
        out_shape=[jax.ShapeDtypeStruct((t, c), BF16), jax.ShapeDtypeStruct((CONV_K, c), F32), jax.ShapeDtypeStruct((1, c), F32)],
        scratch_shapes=[pltpu.VMEM((tr + HALO, cw), F32), pltpu.VMEM((tr + HALO, cw), F32)],
        compiler_params=_params(("parallel", "arbitrary")),
    )(x, x, w, dy, dy)


SUB = 8


def _lru_scan(a, u, name):
    t, c = a.shape
    tr = _tile(t, 528)

    def body(a_ref, u_ref, o_ref, carry):
        @pl.when(pl.program_id(0) == 0)
        def _():
            carry[...] = jnp.zeros_like(carry)

        rows = lax.broadcasted_iota(jnp.int32, (SUB, 1), 0)

        def step(k, cin):
            r = pl.multiple_of(k * SUB, SUB)
            av, uv = a_ref[pl.ds(r, SUB), :], u_ref[pl.ds(r, SUB), :]
            for d in (1, 2, 4):
                m = rows >= d
                uv = uv + av * jnp.where(m, pltpu.roll(uv, d, 0), 0.0)
                av = av * jnp.where(m, pltpu.roll(av, d, 0), 1.0)
            hv = uv + av * cin
            o_ref[pl.ds(r, SUB), :] = hv
            return jnp.broadcast_to(hv[SUB - 1:SUB, :], (SUB, c))

        carry[...] = lax.fori_loop(0, tr // SUB, step, carry[...])

    row = pl.BlockSpec((tr, c), lambda i: (i, 0))
    return pl.pallas_call(
        body, name=name, grid=(t // tr,), in_specs=[row, row], out_specs=row,
        out_shape=jax.ShapeDtypeStruct((t, c), F32), scratch_shapes=[pltpu.VMEM((SUB, c), F32)],
        compiler_params=_params(("arbitrary",)),
    )(a, u)


def _lru_scan_bwd(a, hs, dy, name):
    t, c = a.shape
    tr = _tile(t, 528)
    nb, nt = t // tr, tr // SUB

    def body(a_ref, h_ref, hh_ref, dy_ref, du_ref, da_ref, gcar, acar):
        i = pl.program_id(0)

        @pl.when(i == 0)
        def _():
            gcar[...] = jnp.zeros_like(gcar)
            acar[...] = jnp.zeros_like(acar)

        rows = lax.broadcasted_iota(jnp.int32, (SUB, 1), 0)
        hhalo = jnp.where(i < nb - 1, hh_ref[...], 0.0)

        def step(kk, car):
            gin, a_next_first = car
            k = nt - 1 - kk
            r = pl.multiple_of(k * SUB, SUB)
            av, hv, dv = a_ref[pl.ds(r, SUB), :], h_ref[pl.ds(r, SUB), :], dy_ref[pl.ds(r, SUB), :]
            rp = pl.multiple_of(jnp.maximum(k - 1, 0) * SUB, SUB)
            hp = jnp.where(k > 0, h_ref[pl.ds(rp, SUB), :], hhalo)
            cv = jnp.where(rows < SUB - 1, pltpu.roll(av, SUB - 1, 0), a_next_first)
            gv = dv
            for d in (1, 2, 4):
                m = rows < SUB - d
                gv = gv + cv * jnp.where(m, pltpu.roll(gv, SUB - d, 0), 0.0)
                cv = cv * jnp.where(m, pltpu.roll(cv, SUB - d, 0), 1.0)
            gv = gv + cv * gin
            hprev = jnp.where(rows >= 1, pltpu.roll(hv, 1, 0), jnp.broadcast_to(hp[SUB - 1:SUB, :], (SUB, c)))
            du_ref[pl.ds(r, SUB), :] = gv
            da_ref[pl.ds(r, SUB), :] = gv * hprev
            return jnp.broadcast_to(gv[0:1, :], (SUB, c)), jnp.broadcast_to(av[0:1, :], (SUB, c))

        g, af = lax.fori_loop(0, nt, step, (gcar[...], acar[...]))
        gcar[...] = g
        acar[...] = af

    hb = tr // SUB
    row = pl.BlockSpec((tr, c), lambda i: (nb - 1 - i, 0))
    halo = pl.BlockSpec((SUB, c), lambda i: (jnp.maximum((nb - 1 - i) * hb - 1, 0), 0))
    return pl.pallas_call(
        body, name=name, grid=(nb,), in_specs=[row, row, halo, row], out_specs=[row, row],
        out_shape=[jax.ShapeDtypeStruct((t, c), F32)] * 2,
        scratch_shapes=[pltpu.VMEM((SUB, c), F32), pltpu.VMEM((SUB, c), F32)],
        compiler_params=_params(("arbitrary",)),
    )(a, hs, hs, dy)


def _lru_gates(row0, xr, wa, ba, wi, bi, lam):
    r = jax.nn.sigmoid(_bnn(xr, wa) + ba)
    i = jax.nn.sigmoid(_bnn(xr, wi) + bi)
    log_a = -LRU_C * r * _softplus(-lam)
    u = jnp.sqrt(_neg_expm1(2.0 * log_a)) * (i * xr)
    return jnp.exp(log_a), jnp.where(_valid(row0, xr.shape[0]), u, 0.0)


def _lru_gate_out(row0, hs, yw):
    return (hs * jax.nn.gelu(yw),)


def _rglru_fwd(h, p, l, o):
    hn = _prenorm(h, p["mix_pre_g"][l], "rg_prenorm")
    xw = _mm(hn, p["rg_w_x"][o], "nn", "rg_in_x")
    yw = _mm(hn, p["rg_w_y"][o], "nn", "rg_in_y")
    xr = _conv_fwd(xw, p["rg_conv_w"][o], p["rg_conv_b"][o], "rg_conv", cw=LRU_WIDTH // 2)
    gp = [p["rg_w_a"][o], p["rg_b_a"][o], p["rg_w_i"][o], p["rg_b_i"][o], p["rg_lambda"][o]]
    a, u = _rowwise("rg_gates", _lru_gates, [xr], gp, [(LRU_WIDTH, F32)] * 2, ncol=LRU_BLOCKS, tr=_tile(h.shape[0], 1056))
    hs = _lru_scan(a, u, "rg_scan")
    hg = _rowwise("rg_gate_out", _lru_gate_out, [hs, yw], [], [(LRU_WIDTH, BF16)])[0]
    m, h2 = _mm(hg, p["rg_w_out"][o], "nn", "rg_out", extra=[h], vecs=[p["mix_post_g"][l]], post=_post_residual, out_dtypes=[F32, F32])
    return h2, (h, hn, xw, yw, xr, a, hs, hg, m)


def _rglru_bwd(dh, saved, p, l, o, grads):
    h, hn, xw, yw, xr, a, hs, hg, m = saved
    dm, grads["mix_post_g"][l] = _postnorm_bwd(m, p["mix_post_g"][l], dh, "rg_postnorm_bwd")
    grads["rg_w_out"][o] = _mm(hg, dm, "tn", "rg_out_dw")
    dhg = _mm(dm, p["rg_w_out"][o], "nt", "rg_out_dx")
    (dhs, dyw), _ = _rowwise_vjp("rg_gate_out_bwd", _lru_gate_out, [hs, yw], [], [dhg], row_dtypes=[F32, BF16])
    du, da = _lru_scan_bwd(a, hs, dhs, "rg_scan_bwd")
    gp = [p["rg_w_a"][o], p["rg_b_a"][o], p["rg_w_i"][o], p["rg_b_i"][o], p["rg_lambda"][o]]
    (dxr,), gg = _rowwise_vjp("rg_gates_bwd", _lru_gates, [xr], gp, [da, du], ncol=LRU_BLOCKS, tr=_tile(h.shape[0], 1056))
    grads["rg_w_a"][o], grads["rg_b_a"][o], grads["rg_w_i"][o], grads["rg_b_i"][o], grads["rg_lambda"][o] = gg
    dxw, grads["rg_conv_w"][o], grads["rg_conv_b"][o] = _conv_bwd(xw, p["rg_conv_w"][o], dxr, "rg_conv_bwd", cw=LRU_WIDTH // 2)
    grads["rg_w_x"][o] = _mm(hn, dxw, "tn", "rg_in_x_dw")
    grads["rg_w_y"][o] = _mm(hn, dyw, "tn", "rg_in_y_dw")
    dhx = _mm(dxw, p["rg_w_x"][o], "nt", "rg_in_x_dx")
    dhy = _mm(dyw, p["rg_w_y"][o], "nt", "rg_in_y_dx")
    dh, grads["mix_pre_g"][l] = _prenorm_bwd_add(h, p["mix_pre_g"][l], [dhx, dhy], dh, "rg_prenorm_bwd")
    return dh


SSD_GW = SSD_D_INNER // SSD_GROUPS
SSD_GH = SSD_HEADS // SSD_GROUPS
XACT_B = SSD_D_INNER // SSD_STATE
XACT_C = XACT_B + SSD_GROUPS


def _hp(a, b, dims=_DIMS["nn"]):
    return lax.dot_general(a, b, dims, precision=lax.Precision.HIGHEST, preferred_element_type=F32)


def _split_dot(a, e, mode, parts):
    eb = e.astype(BF16)
    out, rest = None, a
    for _ in range(parts):
        term = rest.astype(BF16)
        rest = rest - term.astype(F32)
        if mode in ("nn", "nt"):
            prod = lax.dot_general(term, eb, _DIMS[mode], preferred_element_type=F32)
        else:
            prod = lax.dot_general(eb, term, _DIMS["nn" if mode == "left" else "tn"], preferred_element_type=F32)
        out = prod if out is None else out + prod
    return out


@jax.custom_vjp
def _select_nn(a, e):
    return _split_dot(a, e, "nn", 3)


_select_nn.defvjp(lambda a, e: (_split_dot(a, e, "nn", 3), e), lambda e, ct: (_split_dot(ct, e, "nt", 2), jnp.zeros_like(e)))


@jax.custom_vjp
def _select_left(e, a):
    return _split_dot(a, e, "left", 3)


_select_left.defvjp(lambda e, a: (_split_dot(a, e, "left", 3), e),
                    lambda e, ct: (jnp.zeros_like(e), _split_dot(ct, e, "left_t", 2)))


def _ssd_chunk(xs, bm, cm, dt, da, ht, g):
    l = CHUNK
    ri = lax.broadcasted_iota(jnp.int32, (l, l), 0)
    ci = lax.broadcasted_iota(jnp.int32, (l, l), 1)
    causal = ri >= ci
    tri = causal.astype(F32)
    hr = lax.broadcasted_iota(jnp.int32, (LANES, SSD_GW), 0)
    hc = lax.broadcasted_iota(jnp.int32, (LANES, SSD_GW), 1)
    expand = (hr == g * SSD_GH + hc // SSD_HEAD_DIM).astype(F32)
    acs = _select_left(tri, da)
    acs_t = acs.T
    acs_e = _select_nn(acs, expand)
    x = xs * _select_nn(dt, expand)
    gmat = _bnt(cm, bm)
    lane = lax.broadcasted_iota(jnp.int32, (1, LANES), 1)
    sub = lax.broadcasted_iota(jnp.int32, (LANES, 1), 0)
    colhead = lax.broadcasted_iota(jnp.int32, (1, SSD_GW), 1) // SSD_HEAD_DIM
    y = _bnn(cm, ht) * jnp.exp(acs_e)
    for k in range(SSD_GH):
        hh = g * SSD_GH + k
        col = jnp.sum(jnp.where(lane == hh, acs, 0.0), axis=1, keepdims=True)
        row = jnp.sum(jnp.where(sub == hh, acs_t, 0.0), axis=0, keepdims=True)
        decay = jnp.exp(jnp.where(causal, col - row, -1e30))
        y = y + _bnn(gmat * decay, jnp.where(colhead == k, x, 0.0))
    last = lax.broadcasted_iota(jnp.int32, (l, 1), 0) == l - 1
    a_last = jnp.sum(jnp.where(last, acs_e, 0.0), axis=0, keepdims=True)
    st = _btn(bm, x * jnp.exp(a_last - acs_e))
    return y, ht * jnp.exp(a_last) + st


def _ssd_specs(nc, rev):
    def cc(c):
        return nc - 1 - c if rev else c

    return [pl.BlockSpec((CHUNK, SSD_GW), lambda c, g: (cc(c), g)),
            pl.BlockSpec((CHUNK, SSD_STATE), lambda c, g: (cc(c), XACT_B + g)),
            pl.BlockSpec((CHUNK, SSD_STATE), lambda c, g: (cc(c), XACT_C + g)),
            pl.BlockSpec((CHUNK, LANES), lambda c, g: (cc(c), 0)),
            pl.BlockSpec((CHUNK, LANES), lambda c, g: (cc(c), 0))]


def _ssd_scan(xact, dt, da, name):
    t = xact.shape[0]
    nc = t // CHUNK

    def body(xs_ref, b_ref, c_ref, dt_ref, da_ref, y_ref, hs_ref, state):
        c, g = pl.program_id(0), pl.program_id(1)

        @pl.when(c == 0)
        def _():
            state[g] = jnp.zeros((SSD_STATE, SSD_GW), F32)

        ht = state[g]
        hs_ref[0] = ht
        y, ht2 = _ssd_chunk(xs_ref[...], b_ref[...], c_ref[...], dt_ref[...], da_ref[...], ht, g)
        y_ref[...] = y
        state[g] = ht2

    return pl.pallas_call(
        body, name=name, grid=(nc, SSD_GROUPS), in_specs=_ssd_specs(nc, False),
        out_specs=[pl.BlockSpec((CHUNK, SSD_GW), lambda c, g: (c, g)),
                   pl.BlockSpec((1, SSD_STATE, SSD_GW), lambda c, g: (c * SSD_GROUPS + g, 0, 0))],
        out_shape=[jax.ShapeDtypeStruct((t, SSD_D_INNER), F32), jax.ShapeDtypeStruct((nc * SSD_GROUPS, SSD_STATE, SSD_GW), F32)],
        scratch_shapes=[pltpu.VMEM((SSD_GROUPS, SSD_STATE, SSD_GW), F32)],
        compiler_params=_params(("arbitrary", "arbitrary")),
    )(xact, xact, xact, dt, da)


def _ssd_scan_bwd(xact, dt, da, hsave, dy, dxskip, name):
    t = xact.shape[0]
    nc = t // CHUNK

    def body(xs_ref, b_ref, c_ref, dt_ref, da_ref, hs_ref, dy_ref, sk_ref, dxs_ref, db_ref, dc_ref, ddt_ref, dda_ref, dstate):
        c, g = pl.program_id(0), pl.program_id(1)

        @pl.when(c == 0)
        def _():
            dstate[g] = jnp.zeros((SSD_STATE, SSD_GW), F32)

        _, vjp = jax.vjp(lambda *a: _ssd_chunk(*a, g), xs_ref[...], b_ref[...], c_ref[...], dt_ref[...], da_ref[...], hs_ref[0])
        dxs, dbm, dcm, ddt, dda, dht = vjp((dy_ref[...], dstate[g]))
        dxs_ref[...] = dxs + sk_ref[...]
        db_ref[...] = dbm
        dc_ref[...] = dcm
        dstate[g] = dht

        @pl.when(g == 0)
        def _():
            ddt_ref[...] = ddt
            dda_ref[...] = dda

        @pl.when(g > 0)
        def _():
            ddt_ref[...] += ddt
            dda_ref[...] += dda

    grp = pl.BlockSpec((CHUNK, SSD_GW), lambda c, g: (nc - 1 - c, g))
    st = pl.BlockSpec((CHUNK, SSD_STATE), lambda c, g: (nc - 1 - c, g))
    hd = pl.BlockSpec((CHUNK, LANES), lambda c, g: (nc - 1 - c, 0))
    return pl.pallas_call(
        body, name=name, grid=(nc, SSD_GROUPS),
        in_specs=_ssd_specs(nc, True) + [pl.BlockSpec((1, SSD_STATE, SSD_GW), lambda c, g: ((nc - 1 - c) * SSD_GROUPS + g, 0, 0)), grp, grp],
        out_specs=[grp, st, st, hd, hd],
        out_shape=[jax.ShapeDtypeStruct((t, SSD_D_INNER), F32), jax.ShapeDtypeStruct((t, SSD_GROUPS * SSD_STATE), F32),
                   jax.ShapeDtypeStruct((t, SSD_GROUPS * SSD_STATE), F32), jax.ShapeDtypeStruct((t, LANES), F32),
                   jax.ShapeDtypeStruct((t, LANES), F32)],
        scratch_shapes=[pltpu.VMEM((SSD_GROUPS, SSD_STATE, SSD_GW), F32)],
        compiler_params=_params(("arbitrary", "arbitrary")),
    )(xact, xact, xact, dt, da, hsave, dy, dxskip)


def _ssd_act(row0, xc):
    return (jnp.where(_valid(row0, xc.shape[0]), jax.nn.silu(xc), 0.0),)


def _ssd_dt(row0, dtraw, dt_bias, a_log):
    dt = jnp.where(_valid(row0, dtraw.shape[0]), _softplus(dtraw + dt_bias), 0.0)
    return dt, dt * -jnp.exp(a_log)


def _ssd_post(row0, y, xs, z, d_skip, norm_g):
    hr = lax.broadcasted_iota(jnp.int32, (LANES, SSD_D_INNER), 0)
    hc = lax.broadcasted_iota(jnp.int32, (LANES, SSD_D_INNER), 1)
    expand = (hr == hc // SSD_HEAD_DIM).astype(F32)
    d_e = jnp.sum(_hp(jnp.broadcast_to(d_skip, (SUB, LANES)), expand), axis=0, keepdims=True) * (1.0 / SUB)
    return (_rms((y + xs * d_e) * jax.nn.silu(z), norm_g),)


ROPE_LO, ROPE_MID, ROPE_HI = MLA_NOPE, MLA_NOPE + MLA_ROPE // 2, MLA_NOPE + MLA_ROPE
ATT_SCALE = (MLA_NOPE + MLA_ROPE) ** -0.5


def _slot_lane(width):
    return lax.broadcasted_iota(jnp.int32, (1, width), 1) % LANES


def _swap_halves(x):
    width = x.shape[1]
    lane = _slot_lane(width)
    sw = jnp.where(lane < ROPE_MID, pltpu.roll(x, width - MLA_ROPE // 2, 1), pltpu.roll(x, MLA_ROPE // 2, 1))
    return jnp.where((lane >= ROPE_LO) & (lane < ROPE_HI), sw, 0.0)


def _rope(x, cos, sin):
    n = x.shape[1] // LANES
    return x * jnp.tile(cos, (1, n)) + _swap_halves(x) * jnp.tile(sin, (1, n))


def _rope_t(dy, cos, sin):
    n = dy.shape[1] // LANES
    return dy * jnp.tile(cos, (1, n)) + _swap_halves(dy * jnp.tile(sin, (1, n)))


ATT_SCALE2 = ATT_SCALE * float(np.log2(np.e))
MASKED = -1e30


def _att_bias(blk):
    r = jnp.arange(blk)[:, None]
    c = jnp.arange(blk)[None, :]
    zero = jnp.zeros((blk, blk), F32)
    first = jnp.where(c >= PAD, 0.0, MASKED) + zero
    diag = jnp.where(c <= r, 0.0, MASKED).astype(F32)
    return jnp.stack([zero, first, diag, jnp.minimum(first, diag), zero + MASKED])


def _att_bias_index(j, i):
    return jnp.where(j > i, 4, jnp.where(j == 0, 1, 0) + jnp.where(j == i, 2, 0))


def _key_slots(row0, kv, kr):
    width = kv.shape[1]
    return jnp.where(_slot_lane(width) < MLA_NOPE, kv, jnp.tile(kr, (1, width // LANES))), kv


def _attn_fwd(qr, km, vb, name, carried=None):
    t = qr.shape[0]
    blk = _tile(t, 384, LANES)
    nq = t // blk

    bias = _att_bias(blk)

    def body(q_ref, k_ref, v_ref, b_ref, o_ref, s0, s1, p0, p1):
        i = pl.program_id(1)
        lane = lax.broadcasted_iota(jnp.int32, (1, LANES), 1)
        qb = q_ref[...]

        def rows(j):
            return pl.ds(pl.multiple_of(jnp.clip(j, 0, i) * blk, blk), blk)

        def scores(j):
            return lax.dot_general(qb, k_ref[rows(j), :], _DIMS["nt"], preferred_element_type=F32) + b_ref[_att_bias_index(j, i)]

        def half(j, car, s_cur, s_nxt, p_cur, p_prv):
            m, l, acc, al_prev = car
            s_nxt[...] = scores(j + 1)
            acc2 = al_prev * acc + lax.dot_general(p_prv[...], v_ref[rows(j - 1), :], _DIMS["nn"], preferred_element_type=F32)
            m2 = jnp.maximum(m, jnp.max(s_cur[...], axis=1, keepdims=True))
            al = jnp.exp2((m - m2) * ATT_SCALE2)
            pm = jnp.exp2(s_cur[...] * ATT_SCALE2 - m2 * ATT_SCALE2)
            p_cur[...] = pm.astype(BF16)
            return m2, al * l + jnp.sum(pm, axis=1, keepdims=True), acc2, al

        def step(jj, car):
            car = half(2 * jj, car, s0, s1, p0, p1)
            return half(2 * jj + 1, car, s1, s0, p1, p0)

        s0[...] = scores(0)
        p1[...] = jnp.zeros((blk, blk), BF16)
        car = (jnp.full((blk, 1), MASKED, F32), jnp.zeros((blk, 1), F32), jnp.zeros((blk, LANES), F32), jnp.ones((blk, 1), F32))
        steps = i // 2 + 1
        m, l, acc, al_last = lax.fori_loop(0, steps, step, car)
        acc = al_last * acc + lax.dot_general(p1[...], v_ref[rows(2 * steps - 1), :], _DIMS["nn"], preferred_element_type=F32)
        out = jnp.where(lane >= MLA_NOPE, acc / l, m * ATT_SCALE + jnp.log(l))
        o_ref[...] = jnp.where(_valid(i * blk, blk), out, 0.0)

    seq_h = pl.BlockSpec((t, LANES), lambda h, i: (0, h))
    (o,), carried_out = _carry_call(
        body, name, (MLA_HEADS, nq),
        [pl.BlockSpec((blk, LANES), lambda h, i: (i, h)), seq_h, seq_h, _full_spec(bias.shape, 2)],
        [pl.BlockSpec((blk, LANES), lambda h, i: (i, h))], [jax.ShapeDtypeStruct((t, MLA_HEADS * LANES), F32)],
        [pltpu.VMEM((blk, blk), F32)] * 2 + [pltpu.VMEM((blk, blk), BF16)] * 2, (qr, km, vb, bias), carried)
    return o, carried_out


def _attn_bwd(qr, km, vb, o, do, name, carried=None):
    t = qr.shape[0]
    blk = _tile(t, 384, LANES)
    nq = t // blk

    bias = _att_bias(blk)
    log2e = float(np.log2(np.e))

    def body(q_ref, o_ref, do_ref, k_ref, v_ref, b_ref, dq_ref, dkv_ref, dkr_ref, s0, s1, dp0, dp1, p0, p1, ds0, ds1, dk_s, dv_s):
        h, j = pl.program_id(0), pl.program_id(1)
        lane = lax.broadcasted_iota(jnp.int32, (1, LANES), 1)

        @pl.when(j == 0)
        def _():
            dq_ref[...] = jnp.zeros_like(dq_ref)

        @pl.when((h == 0) & (j == 0))
        def _():
            dkr_ref[...] = jnp.zeros_like(dkr_ref)

        kmat, vmat = k_ref[...], v_ref[...]

        def rows(i):
            return pl.ds(pl.multiple_of(jnp.clip(i, j, nq - 1) * blk, blk), blk)

        def first_stage(i, s_buf, dp_buf):
            ic = jnp.minimum(i, nq - 1)
            s_buf[...] = lax.dot_general(q_ref[rows(ic), :], kmat, _DIMS["nt"], preferred_element_type=F32) + b_ref[_att_bias_index(j, ic)]
            dp_buf[...] = lax.dot_general(do_ref[rows(ic), :].astype(BF16), vmat, _DIMS["nt"], preferred_element_type=F32)

        def middle_stage(i, s_buf, dp_buf, p_buf, ds_buf):
            r = rows(i)
            ob, dob = o_ref[r, :], do_ref[r, :]
            delta = jnp.sum(dob * ob, axis=1, keepdims=True)
            pm = jnp.exp2(s_buf[...] * ATT_SCALE2 - ob[:, 0:1] * log2e)
            p_buf[...] = pm.astype(BF16)
            ds_buf[...] = (pm * (dp_buf[...] - delta) * ATT_SCALE).astype(BF16)

        def last_stage(i, p_buf, ds_buf):
            r = rows(i)
            dv_s[...] += lax.dot_general(p_buf[...], do_ref[r, :].astype(BF16), _DIMS["tn"], preferred_element_type=F32)
            dk_s[...] += lax.dot_general(ds_buf[...], q_ref[r, :], _DIMS["tn"], preferred_element_type=F32)
            dq_ref[r, :] += lax.dot_general(ds_buf[...], kmat, _DIMS["nn"], preferred_element_type=F32)

        n = nq - j
        dk_s[...] = jnp.zeros((blk, LANES), F32)
        dv_s[...] = jnp.zeros((blk, LANES), F32)
        first_stage(j, s0, dp0)
        first_stage(j + 1, s1, dp1)
        middle_stage(j, s0, dp0, p0, ds0)

        def step(tt, carry):
            i = j + 2 * tt + 1
            first_stage(i + 1, s0, dp0)
            last_stage(i - 1, p0, ds0)
            middle_stage(i, s1, dp1, p1, ds1)
            first_stage(i + 2, s1, dp1)
            last_stage(i, p1, ds1)
            middle_stage(i + 1, s0, dp0, p0, ds0)
            return carry

        lax.fori_loop(0, (n - 1) // 2, step, 0)

        @pl.when(n % 2 == 0)
        def _():
            last_stage(nq - 2, p0, ds0)
            middle_stage(nq - 1, s1, dp1, p1, ds1)
            last_stage(nq - 1, p1, ds1)

        @pl.when(n % 2 == 1)
        def _():
            last_stage(nq - 1, p0, ds0)

        dk = dk_s[...]
        dkv_ref[...] = jnp.where(lane < MLA_NOPE, dk, dv_s[...]).astype(dkv_ref.dtype)
        dkr_ref[rows(j), :] += jnp.where(lane >= MLA_NOPE, dk, 0.0)

    seq_h = pl.BlockSpec((t, LANES), lambda h, j: (0, h))
    blk_h = pl.BlockSpec((blk, LANES), lambda h, j: (j, h))
    return _carry_call(
        body, name, (MLA_HEADS, nq), [seq_h, seq_h, seq_h, blk_h, blk_h, _full_spec(bias.shape, 2)],
        [seq_h, blk_h, pl.BlockSpec((t, LANES), lambda h, j: (0, 0))],
        [jax.ShapeDtypeStruct((t, MLA_HEADS * LANES), F32), jax.ShapeDtypeStruct((t, MLA_HEADS * LANES), BF16),
         jax.ShapeDtypeStruct((t, LANES), F32)],
        [pltpu.VMEM((blk, blk), F32)] * 4 + [pltpu.VMEM((blk, blk), BF16)] * 4 + [pltpu.VMEM((blk, LANES), F32)] * 2,
        (qr, o, do, km, vb, bias), carried)


def _rms_rows(row0, x, g):
    return (_rms(x, g),)


def _ssdmla_fwd(h, p, l, e, cos, sin, carried=None):
    hn = _prenorm(h, p["mix_pre_g"][l], "sm_prenorm")
    proj = _mm(hn, p["w_in"][e], "nn", "sm_in")
    xc = _conv_fwd(proj, p["ssd_conv_w"][e], p["ssd_conv_b"][e], "ssd_conv", cw=SSD_GW, c0=PROJ_XBC // SSD_GW)
    xact = _rowwise("ssd_act", _ssd_act, [xc], [], [(SSD_CONV_CH, F32)])[0]
    dt, da = _rowwise("ssd_dt", _ssd_dt, [(proj, LANES, PROJ_DT // LANES)], [p["ssd_dt_bias"][e], p["ssd_a_log"][e]],
                      [(LANES, F32)] * 2)
    y, hsave = _ssd_scan(xact, dt, da, "ssd_scan")
    y_ssd = _rowwise("ssd_post", _ssd_post, [y, (xact, SSD_D_INNER, 0), (proj, SSD_D_INNER, 0)],
                     [p["ssd_d"][e], p["ssd_norm_g"][e]], [(SSD_D_INNER, BF16)])[0]
    cqn = _prenorm((proj, MLA_Q_RANK, PROJ_CQ // MLA_Q_RANK), p["mla_q_norm_g"][e], "mla_qnorm")
    ckvn = _prenorm((proj, MLA_KV_RANK, PROJ_CKV // MLA_KV_RANK), p["mla_kv_norm_g"][e], "mla_kvnorm")
    kr = _rowwise("mla_krope", lambda r0, x, c, s: (_rope(x, c, s),), [(proj, LANES, PROJ_KR // LANES), cos, sin], [],
                  [(LANES, F32)])[0]
    qr = _mm(cqn, p["mla_w_q_up"][e], "nn", "mla_q_up", slots=[cos, sin], post=lambda v, c, s: (_rope(v, c, s),), out_dtypes=[BF16])
    km, vb = _mm(ckvn, p["mla_w_kv_up"][e], "nn", "mla_kv_up", slots=[kr], post=lambda v, k: _key_slots(0, v, k),
                 out_dtypes=[BF16, BF16])
    o, carried_out = _attn_fwd(qr, km, vb, "mla_attn", carried)
    m1 = _mm(y_ssd, p["w_out_ssd"][e], "nn", "sm_out_ssd")
    m, h2 = _mm(o, p["w_out_att"][e], "nn", "sm_out_att", extra=[m1, h], vecs=[p["mix_post_g"][l]],
                post=lambda v, m1b, hb, g: _post_residual(v + m1b, hb, g), out_dtypes=[F32, F32])
    return h2, (h, hn, proj, xc, xact, dt, da, y, hsave, y_ssd, cqn, ckvn, qr, km, vb, o, m), carried_out


def _ssdmla_bwd(dh, saved, p, l, e, cos, sin, grads, carried=None):
    h, hn, proj, xc, xact, dt, da, y, hsave, y_ssd, cqn, ckvn, qr, km, vb, o, m = saved
    dm, grads["mix_post_g"][l] = _postnorm_bwd(m, p["mix_post_g"][l], dh, "sm_postnorm_bwd")
    grads["w_out_ssd"][e] = _mm(y_ssd, dm, "tn", "sm_out_ssd_dw")
    grads["w_out_att"][e] = _mm(o, dm, "tn", "sm_out_att_dw")
    dy_ssd = _mm(dm, p["w_out_ssd"][e], "nt", "sm_out_ssd_dx")
    do = _mm(dm, p["w_out_att"][e], "nt", "sm_out_att_dx")
    (dqr, dkv, dkr), carried_out = _attn_bwd(qr, km, vb, o, do, "mla_attn_bwd", carried)
    dq = _rowwise("mla_q_rope_bwd", lambda r0, a, c, s: (_rope_t(a, c, s),), [dqr, cos, sin], [], [(MLA_HEADS * LANES, BF16)],
                  tr=_tile(h.shape[0], 264, 16))[0]
    dkr_raw = _rowwise("mla_krope_bwd", lambda r0, d, c, s: (_rope_t(d, c, s),), [dkr, cos, sin], [], [(LANES, F32)])[0]
    grads["mla_w_q_up"][e] = _mm(cqn, dq, "tn", "mla_q_up_dw")
    dcqn = _mm(dq, p["mla_w_q_up"][e], "nt", "mla_q_up_dx")
    (dcq,), (grads["mla_q_norm_g"][e],) = _rowwise_vjp(
        "mla_qnorm_bwd", _rms_rows, [(proj, MLA_Q_RANK, PROJ_CQ // MLA_Q_RANK)], [p["mla_q_norm_g"][e]], [dcqn])
    grads["mla_w_kv_up"][e] = _mm(ckvn, dkv, "tn", "mla_kv_up_dw")
    dckvn = _mm(dkv, p["mla_w_kv_up"][e], "nt", "mla_kv_up_dx")
    (dckv,), (grads["mla_kv_norm_g"][e],) = _rowwise_vjp(
        "mla_kvnorm_bwd", _rms_rows, [(proj, MLA_KV_RANK, PROJ_CKV // MLA_KV_RANK)], [p["mla_kv_norm_g"][e]], [dckvn])
    (dy, dxskip, dz), (grads["ssd_d"][e], grads["ssd_norm_g"][e]) = _rowwise_vjp(
        "ssd_post_bwd", _ssd_post, [y, (xact, SSD_D_INNER, 0), (proj, SSD_D_INNER, 0)], [p["ssd_d"][e], p["ssd_norm_g"][e]], [dy_ssd])
    dxs, db, dc, ddt, dda = _ssd_scan_bwd(xact, dt, da, hsave, dy, dxskip, "ssd_scan_bwd")
    dxact = jnp.concatenate([dxs, db, dc], axis=1)
    (dxc,), _ = _rowwise_vjp("ssd_act_bwd", _ssd_act, [xc], [], [dxact])
    dxbc, grads["ssd_conv_w"][e], grads["ssd_conv_b"][e] = _conv_bwd(
        proj, p["ssd_conv_w"][e], dxc, "ssd_conv_bwd", cw=SSD_GW, c0=PROJ_XBC // SSD_GW)
    (ddtraw,), (grads["ssd_dt_bias"][e], grads["ssd_a_log"][e]) = _rowwise_vjp(
        "ssd_dt_bwd", _ssd_dt, [(proj, LANES, PROJ_DT // LANES)], [p["ssd_dt_bias"][e], p["ssd_a_log"][e]], [ddt, dda])
    dproj = jnp.concatenate([dz, dxbc, ddtraw, dcq, dckv, dkr_raw], axis=1).astype(BF16)
    grads["w_in"][e] = _mm(hn, dproj, "tn", "sm_in_dw")
    dhn = _mm(dproj, p["w_in"][e], "nt", "sm_in_dx")
    dh, grads["mix_pre_g"][l] = _prenorm_bwd_add(h, p["mix_pre_g"][l], [dhn], dh, "sm_prenorm_bwd")
    return dh, carried_out


GAINS = ("mix_pre_g", "mix_post_g", "mlp_pre_g", "mlp_post_g", "ssd_norm_g", "mla_q_norm_g", "mla_kv_norm_g", "ssd_conv_b", "rg_conv_b")
HEAD_VECS = ("ssd_dt_bias", "ssd_a_log", "ssd_d")
LRU_VECS = ("rg_b_a", "rg_b_i", "rg_lambda")
IN_DT_END = SSD_D_INNER + SSD_CONV_CH + SSD_HEADS
IN_KR = IN_DT_END + MLA_Q_RANK + MLA_KV_RANK


def _each(a, f):
    layers = a if isinstance(a, list) else [a[i] for i in range(a.shape[0])]
    return [None if x is None else f(x) for x in layers]


def _layout_params(w):
    p = {k: _each(w[k], lambda a: a[None, :]) for k in GAINS}
    for k in HEAD_VECS:
        p[k] = _each(w[k], lambda a: jnp.pad(a, (0, LANES - SSD_HEADS))[None, :])
    for k in LRU_VECS:
        p[k] = _each(w[k], lambda a: a.reshape(LRU_BLOCKS, 1, LRU_BLOCK))
    for k in ("w_up", "w_down", "mla_w_kv_up", "rg_w_x", "rg_w_y", "rg_w_out"):
        p[k] = _each(w[k], lambda a: a if isinstance(a, Gathered) else a.astype(BF16))
    for k in ("ssd_conv_w", "rg_conv_w", "rg_w_a", "rg_w_i"):
        p[k] = _each(w[k], lambda a: a)

    def w_in(a):
        def zcols(n):
            return jnp.zeros((a.shape[0], n), a.dtype)

        return jnp.concatenate([a[:, :IN_DT_END], zcols(PROJ_CQ - IN_DT_END), a[:, IN_DT_END:IN_KR], zcols(ROPE_LO),
                                a[:, IN_KR:], zcols(LANES - ROPE_HI)], axis=1).astype(BF16)

    def q_up(a):
        a = a.reshape(MLA_Q_RANK, MLA_HEADS, MLA_NOPE + MLA_ROPE)
        return jnp.pad(a, ((0, 0), (0, 0), (0, LANES - MLA_NOPE - MLA_ROPE))).reshape(MLA_Q_RANK, MLA_HEADS * LANES).astype(BF16)

    def out_att(a):
        a = a[SSD_D_INNER:].reshape(MLA_HEADS, MLA_V, D_MODEL)
        return jnp.pad(a, ((0, 0), (LANES - MLA_V, 0), (0, 0))).reshape(MLA_HEADS * LANES, D_MODEL).astype(BF16)

    p["w_in"] = _each(w["w_in"], w_in)
    p["mla_w_q_up"] = _each(w["mla_w_q_up"], q_up)
    p["w_out_ssd"] = _each(w["w_out_ab"], lambda a: a[:SSD_D_INNER].astype(BF16))
    p["w_out_att"] = _each(w["w_out_ab"], out_att)
    return p


def _rope_tables(t):
    pos = (jnp.arange(t) - PAD).astype(F32)
    inv = ROPE_BASE ** (-jnp.arange(0, MLA_ROPE, 2, dtype=F32) / MLA_ROPE)
    ang = pos[:, None] * inv[None, :]
    c, s = jnp.cos(ang), jnp.sin(ang)
    one, zero = jnp.ones((t, MLA_NOPE), F32), jnp.zeros((t, MLA_NOPE), F32)
    tail = LANES - ROPE_HI
    return (jnp.concatenate([one, c, c, one[:, :tail]], axis=1), jnp.concatenate([zero, -s, s, zero[:, :tail]], axis=1))


GRAD_KEYS = GAINS + HEAD_VECS + LRU_VECS + ("w_up", "w_down", "mla_w_kv_up", "rg_w_x", "rg_w_y", "rg_w_out", "ssd_conv_w",
                                            "rg_conv_w", "rg_w_a", "rg_w_i", "w_in", "mla_w_q_up", "w_out_ssd", "w_out_att")


def _device_step(x, meta, target, p, hooks=None):
    t = PAD + N_META + x.shape[0]
    cos, sin = _rope_tables(t)
    h = jnp.concatenate([jnp.zeros((PAD, D_MODEL), F32), meta, x], axis=0)
    n_even, n_odd = (DEPTH + 1) // 2, DEPTH // 2
    saved = []
    for l in range(DEPTH):
        if l % 2 == 0:
            carried = hooks.forward_exchange() if hooks and l == 0 else None
            h, sm, arrived = _ssdmla_fwd(h, p, l, l // 2, cos, sin, carried)
            if carried is not None:
                p = hooks.after_forward_exchange(arrived)
        else:
            h, sm = _rglru_fwd(h, p, l, l // 2)
        h, sp = _mlp_fwd(h, p, l)
        saved.append((sm, sp))
    sq, dh = _loss_and_grad(h, target, "loss")
    per_layer = {"mix_pre_g": DEPTH, "mix_post_g": DEPTH, "mlp_pre_g": DEPTH, "mlp_post_g": DEPTH, "w_up": DEPTH, "w_down": DEPTH}
    grads = {k: [None] * per_layer.get(k, n_odd if k.startswith("rg_") else n_even) for k in GRAD_KEYS}
    for l in reversed(range(DEPTH)):
        sm, sp = saved[l]
        dh = _mlp_bwd(dh, sp, p, l, grads)
        if l % 2 == 0:
            carried = hooks.backward_exchange(grads, l) if hooks else None
            dh, arrived = _ssdmla_bwd(dh, sm, p, l, l // 2, cos, sin, grads, carried)
            if carried is not None:
                hooks.after_backward_exchange(arrived, l)
        else:
            dh = _rglru_bwd(dh, sm, p, l, l // 2, grads)
    return sq, dh, grads


MESH = pl.DeviceIdType.MESH
ANY = pl.BlockSpec(memory_space=pl.ANY)


def _mesh_pos():
    return lax.axis_index("x"), lax.axis_index("y"), lax.axis_index("c")


def _other_chips(x, y):
    return [(1 - x, y), (x, 1 - y), (1 - x, 1 - y)]


def _remote(src, dst, send_sems, recv_sems, k, to):
    return pltpu.make_async_remote_copy(src_ref=src, dst_ref=dst, send_sem=send_sems.at[k], recv_sem=recv_sems.at[k],
                                        device_id=to, device_id_type=MESH)


class Exchange:
    def __init__(self, ins, outs, aliases, n_sems, plan):
        self.ins, self.outs, self.aliases, self.n_sems, self.plan = list(ins), list(outs), dict(aliases), n_sems, plan


def _sems(n):
    return [pltpu.SemaphoreType.DMA((n,)), pltpu.SemaphoreType.DMA((n,))]


def _run_exchange(name, ex):
    ni, no = len(ex.ins), len(ex.outs)

    def body(*refs):
        sends = ex.plan(refs[:ni], refs[ni:ni + no], refs[-2], refs[-1], False)
        for cp in sends:
            cp.start()
        for cp in ex.plan(refs[:ni], refs[ni:ni + no], refs[-2], refs[-1], True):
            cp.wait_recv()
        for cp in sends:
            cp.wait_send()

    return pl.pallas_call(body, name=name, in_specs=[ANY] * ni, out_specs=[ANY] * no, out_shape=ex.outs,
                          input_output_aliases=ex.aliases, scratch_shapes=_sems(ex.n_sems))(*ex.ins)


def _carry_call(body, name, grid, in_specs, out_specs, out_shape, scratch_shapes, args, ex):
    if ex is None:
        res = pl.pallas_call(body, name=name, grid=grid, in_specs=in_specs, out_specs=out_specs, out_shape=out_shape,
                             scratch_shapes=scratch_shapes, compiler_params=_params(("arbitrary",) * len(grid)))(*args)
        return res, None
    ni, no, ns, xi, xo = len(in_specs), len(out_specs), len(scratch_shapes), len(ex.ins), len(ex.outs)

    def wrapped(*refs):
        ins, xin = refs[:ni], refs[ni:ni + xi]
        outs, xout = refs[ni + xi:ni + xi + no], refs[ni + xi + no:ni + xi + no + xo]
        scr, send_sems, recv_sems = refs[ni + xi + no + xo:-2], refs[-2], refs[-1]
        pid = [pl.program_id(d) for d in range(len(grid))]
        first = functools.reduce(jnp.logical_and, [p == 0 for p in pid])
        last = functools.reduce(jnp.logical_and, [p == g - 1 for p, g in zip(pid, grid)])

        @pl.when(first)
        def _():
            for cp in ex.plan(xin, xout, send_sems, recv_sems, False):
                cp.start()

        body(*ins, *outs, *scr)

        @pl.when(last)
        def _():
            for cp in ex.plan(xin, xout, send_sems, recv_sems, True):
                cp.wait_recv()
            for cp in ex.plan(xin, xout, send_sems, recv_sems, False):
                cp.wait_send()

    res = pl.pallas_call(
        wrapped, name=name, grid=grid, in_specs=list(in_specs) + [ANY] * xi, out_specs=list(out_specs) + [ANY] * xo,
        out_shape=list(out_shape) + ex.outs, scratch_shapes=list(scratch_shapes) + _sems(ex.n_sems),
        input_output_aliases={ni + i: no + o for i, o in ex.aliases.items()},
        compiler_params=_params(("arbitrary",) * len(grid)))(*args, *ex.ins)
    return res[:no], res[no:]


def _gather_ici(srcs, bufs, ranges):
    n = len(srcs)

    def plan(in_refs, out_refs, ss, rs, arrivals):
        x, y, c = _mesh_pos()
        cps = []
        for t, (l0, nl) in enumerate(ranges):
            if nl:
                s, o, lr = in_refs[t], out_refs[t], pl.ds(l0, nl)
                for j, (cx, cy) in enumerate(_other_chips(x, y)):
                    chip = 2 * cx + cy if arrivals else 2 * x + y
                    cps.append(_remote(s.at[lr, c], o.at[chip, lr, c], ss, rs, (N_CHIPS - 1) * t + j, (cx, cy, c)))
        return cps

    outs = [jax.ShapeDtypeStruct((N_CHIPS,) + s.shape, s.dtype) for s in srcs]
    if bufs is None:
        return Exchange(srcs, outs, {}, (N_CHIPS - 1) * n, plan)
    return Exchange(list(srcs) + list(bufs), outs, {n + t: t for t in range(n)}, (N_CHIPS - 1) * n, plan)


def _gather_d2d(srcs, bufs, ranges):
    n = len(srcs)

    def plan(in_refs, out_refs, ss, rs, arrivals):
        x, y, c = _mesh_pos()
        sib, me = (x, y, 1 - c), 2 * x + y
        cps = []
        for t, (l0, nl) in enumerate(ranges):
            if nl:
                s, o, lr = in_refs[t], out_refs[t], pl.ds(l0, nl)
                for j, (cx, cy) in enumerate(_other_chips(x, y)):
                    slot = o.at[2 * cx + cy, lr, c]
                    cps.append(_remote(slot, o.at[2 * cx + cy, lr, 1 - c] if arrivals else slot, ss, rs, N_CHIPS * t + j, sib))
                cps.append(_remote(s.at[lr], o.at[me, lr], ss, rs, N_CHIPS * t + N_CHIPS - 1, sib))
        return cps

    outs = [jax.ShapeDtypeStruct(b.shape, b.dtype) for b in bufs]
    return Exchange(list(srcs) + list(bufs), outs, {n + t: t for t in range(n)}, N_CHIPS * n, plan)


def _gather_chips(srcs, name):
    ranges = [(0, s.shape[0]) for s in srcs]
    bufs = _run_exchange(name + "_ici", _gather_ici(srcs, None, ranges))
    return _run_exchange(name + "_d2d", _gather_d2d(srcs, bufs, ranges))


def _pair_exchange(gs):
    def plan(in_refs, out_refs, ss, rs, arrivals):
        x, y, c = _mesh_pos()
        return [_remote(g.at[pl.ds(0, N_CHIPS), 1 - c], o, ss, rs, t, (x, y, 1 - c)) for t, (g, o) in enumerate(zip(in_refs, out_refs))]

    return Exchange(gs, [jax.ShapeDtypeStruct((g.shape[0],) + g.shape[2:], g.dtype) for g in gs], {}, len(gs), plan)


def _chip_exchange(ps, slots, qs, q_shapes):
    n = len(ps)
    kept = [g for g, q in enumerate(qs) if q is not None]

    def plan(in_refs, out_refs, ss, rs, arrivals):
        x, y, c = _mesh_pos()
        return [_remote(in_refs[t].at[2 * cx + cy], out_refs[g].at[j, li], ss, rs, (N_CHIPS - 1) * t + j, (cx, cy, c))
                for t, (g, li) in enumerate(slots) for j, (cx, cy) in enumerate(_other_chips(x, y))]

    return Exchange(list(ps) + [qs[g] for g in kept], q_shapes, {n + i: g for i, g in enumerate(kept)}, (N_CHIPS - 1) * n, plan)


def _pair_share(fs):
    def plan(in_refs, out_refs, ss, rs, arrivals):
        x, y, c = _mesh_pos()
        return [_remote(o.at[pl.ds(0, o.shape[0]), c], o.at[pl.ds(0, o.shape[0]), 1 - c if arrivals else c], ss, rs, t, (x, y, 1 - c))
                for t, o in enumerate(out_refs)]

    return Exchange(fs, [jax.ShapeDtypeStruct(f.shape, f.dtype) for f in fs], {t: t for t in range(len(fs))}, len(fs), plan)


SUM_BLOCK = 512 * 1024


def _sum_pair(g, ra, c, name):
    n, _, h, w = g.shape
    tr = _tile(h, max(16, SUM_BLOCK // w), 16)

    def body(c_ref, g_ref, r_ref, o_ref):
        o_ref[...] = (g_ref[0] + r_ref[...]).astype(o_ref.dtype)

    return pl.pallas_call(
        body, name=name,
        grid_spec=pltpu.PrefetchScalarGridSpec(
            num_scalar_prefetch=1, grid=(n, h // tr),
            in_specs=[pl.BlockSpec((1, 1, tr, w), lambda s, i, cr: (s, cr[0], i, 0)), pl.BlockSpec((1, tr, w), lambda s, i, cr: (s, i, 0))],
            out_specs=pl.BlockSpec((1, tr, w), lambda s, i, cr: (s, i, 0))),
        out_shape=jax.ShapeDtypeStruct((n, h, w), BF16),
        compiler_params=_params(("parallel", "parallel")),
    )(c.reshape(1).astype(jnp.int32), g, ra)


def _sum_chips(ps, q, pos, name):
    nc, nl, h, w = q.shape
    tr = _tile(h, max(16, SUM_BLOCK // (w * nl)), 16)

    def body(x_ref, y_ref, c_ref, *refs):
        q_ref, o_ref = refs[nl], refs[nl + 1]
        for l in range(nl):
            acc = refs[l][0].astype(F32)
            for j in range(nc):
                acc = acc + q_ref[j, l].astype(F32)
            o_ref[l] = acc

    return pl.pallas_call(
        body, name=name,
        grid_spec=pltpu.PrefetchScalarGridSpec(
            num_scalar_prefetch=3, grid=(h // tr,),
            in_specs=[pl.BlockSpec((1, tr, w), lambda i, x, y, c: (2 * x[0] + y[0], i, 0))] * nl
            + [pl.BlockSpec((nc, nl, tr, w), lambda i, x, y, c: (0, 0, i, 0))],
            out_specs=pl.BlockSpec((nl, None, tr, w), lambda i, x, y, c: (0, c[0], i, 0))),
        out_shape=jax.ShapeDtypeStruct((nl, 2, h, w), F32),
        compiler_params=_params(("parallel",)),
    )(*pos, *ps, q)


def _adamw(g, w, m, v, name):
    def f(r0, gg, ww, mm, vv):
        m2 = ADAM_B1 * mm + (1.0 - ADAM_B1) * gg
        v2 = ADAM_B2 * vv + (1.0 - ADAM_B2) * jnp.square(gg)
        m_hat = m2 / (1.0 - ADAM_B1 ** ADAM_STEP)
        v_hat = v2 / (1.0 - ADAM_B2 ** ADAM_STEP)
        return gg, -ADAM_LR * (m_hat / (jnp.sqrt(v_hat) + ADAM_EPS) + ADAM_WD * ww), m2, v2

    return _rowwise(name, f, [g, w, m, v], [], [(g.shape[1], F32)] * 4, tr=_tile(g.shape[0], 512))


WEIGHTS = (
    ("meta_tokens", (N_META, D_MODEL), 1), ("mix_pre_g", (DEPTH, D_MODEL), None), ("mix_post_g", (DEPTH, D_MODEL), None),
    ("mlp_pre_g", (DEPTH, D_MODEL), None), ("mlp_post_g", (DEPTH, D_MODEL), None), ("w_up", (DEPTH, D_MODEL, D_FF), 2),
    ("w_down", (DEPTH, D_FF, D_MODEL), 1), ("w_in", (2, D_MODEL, 3248), 2), ("ssd_conv_w", (2, CONV_K, SSD_CONV_CH), 2),
    ("ssd_conv_b", (2, SSD_CONV_CH), None), ("ssd_dt_bias", (2, SSD_HEADS), None), ("ssd_a_log", (2, SSD_HEADS), None),
    ("ssd_d", (2, SSD_HEADS), None), ("ssd_norm_g", (2, SSD_D_INNER), None), ("mla_q_norm_g", (2, MLA_Q_RANK), None),
    ("mla_w_q_up", (2, MLA_Q_RANK, MLA_HEADS * (MLA_NOPE + MLA_ROPE)), 2), ("mla_kv_norm_g", (2, MLA_KV_RANK), None),
    ("mla_w_kv_up", (2, MLA_KV_RANK, MLA_HEADS * (MLA_NOPE + MLA_V)), 2), ("w_out_ab", (2, SSD_D_INNER + MLA_HEADS * MLA_V, D_MODEL), 1),
    ("rg_w_x", (2, D_MODEL, LRU_WIDTH), 2), ("rg_w_y", (2, D_MODEL, LRU_WIDTH), 2), ("rg_conv_w", (2, CONV_K, LRU_WIDTH), 2),
    ("rg_conv_b", (2, LRU_WIDTH), 1), ("rg_w_a", (2, LRU_BLOCKS, LRU_BLOCK, LRU_BLOCK), None), ("rg_b_a", (2, LRU_WIDTH), 1),
    ("rg_w_i", (2, LRU_BLOCKS, LRU_BLOCK, LRU_BLOCK), None), ("rg_b_i", (2, LRU_WIDTH), 1), ("rg_lambda", (2, LRU_WIDTH), 1),
    ("rg_w_out", (2, LRU_WIDTH, D_MODEL), 1),
)
BIG = {"w_up": "col", "w_down": "row", "w_in": "col", "mla_w_q_up": "col", "mla_w_kv_up": "col", "w_out_ab": "row",
       "rg_w_x": "col", "rg_w_y": "col", "rg_w_out": "row"}
DIRECT = ("w_up", "w_down")
FLAT_QUANTUM = 2 * 16 * LANES
TABLE = {name: (shape, d) for name, shape, d in WEIGHTS}
SMALL_SHARDED = tuple(name for name, _, d in WEIGHTS if d is not None and name not in BIG)
REPLICATED = tuple(name for name, _, d in WEIGHTS if d is None)


def _chips_to_full(a, kind):
    if kind == "col":
        return jnp.moveaxis(a, 0, 2).reshape(a.shape[1], a.shape[2], -1)
    return jnp.moveaxis(a, 0, 1).reshape(a.shape[1], -1, a.shape[3])


def _full_to_chips(g, kind):
    if kind == "col":
        return jnp.moveaxis(g.reshape(g.shape[0], N_CHIPS, -1), 1, 0)
    return g.reshape(N_CHIPS, -1, g.shape[1])


def _shard_shape(shape, d):
    return shape[:d] + (shape[d] // N_CHIPS,) + shape[d + 1:]


def _shard_major(full, d):
    s = full.shape
    return jnp.moveaxis(full.reshape(s[:d] + (N_CHIPS, s[d] // N_CHIPS) + s[d + 1:]), d, 0).reshape(N_CHIPS, -1)


def _from_shard_major(a, shape, d):
    ss = _shard_shape(shape, d)
    return jnp.moveaxis(a.reshape((N_CHIPS,) + ss), 0, d).reshape(shape)


def _pad_cols(a, quantum):
    n = a.shape[-1]
    return jnp.pad(a, [(0, 0)] * (a.ndim - 1) + [(0, -n % quantum)])


def _big_pieces(g):
    def w_in(a):
        return jnp.concatenate([a[:, :IN_DT_END], a[:, PROJ_CQ:PROJ_KR], a[:, PROJ_KR + ROPE_LO:PROJ_KR + ROPE_HI]], axis=1)

    def q_up(a):
        return a.reshape(MLA_Q_RANK, MLA_HEADS, LANES)[:, :, :MLA_NOPE + MLA_ROPE].reshape(MLA_Q_RANK, -1)

    def out_ab(sa):
        s, a = sa
        return jnp.concatenate([s, a.reshape(MLA_HEADS, LANES, D_MODEL)[:, LANES - MLA_V:, :].reshape(-1, D_MODEL)], axis=0)

    ident = lambda a: a
    full = {"w_down": _each(g["w_down"], ident), "w_in": _each(g["w_in"], w_in), "mla_w_q_up": _each(g["mla_w_q_up"], q_up),
            "mla_w_kv_up": _each(g["mla_w_kv_up"], ident),
            "w_out_ab": _each([None if s is None or a is None else (s, a) for s, a in zip(g["w_out_ssd"], g["w_out_att"])], out_ab),
            "rg_w_x": _each(g["rg_w_x"], ident), "rg_w_y": _each(g["rg_w_y"], ident), "rg_w_out": _each(g["rg_w_out"], ident)}
    return {name: (list(g[name]) if name == "w_up" else _each(full[name], lambda a, k=BIG[name]: _full_to_chips(a, k))) for name in BIG}


def _small_grads(g, dh):
    out = {k: jnp.stack(g[k])[:, 0, :] for k in GAINS}
    for k in HEAD_VECS:
        out[k] = jnp.stack(g[k])[:, 0, :SSD_HEADS]
    for k in LRU_VECS:
        out[k] = jnp.stack(g[k]).reshape(-1, LRU_WIDTH)
    for k in ("ssd_conv_w", "rg_conv_w", "rg_w_a", "rg_w_i"):
        out[k] = jnp.stack(g[k])
    out["meta_tokens"] = dh[PAD:PAD + N_META]
    return out


class StepExchanges:
    def __init__(self, w):
        self.w = w
        self.c = lax.axis_index("c")
        self.riding, self.ras = {}, {}
        small = _pad_cols(jnp.concatenate([w[n].reshape(-1) for n in SMALL_SHARDED]), FLAT_QUANTUM).reshape(1, 2, -1, LANES)
        self.srcs = [self._halves(w[n].astype(BF16)) for n in BIG] + [small]
        first = {n: (0, 1 if n in ("w_in", "mla_w_q_up", "mla_w_kv_up", "w_out_ab") else 0) for n in BIG}
        self.first = [first[n] for n in BIG] + [(0, 1)]
        self.rest = [(nl, TABLE[n][0][0] - nl) for n, (_, nl) in zip(BIG, self.first)] + [(0, 0)]
        bufs = _run_exchange("gather_first_ici", _gather_ici(self.srcs, None, self.first))
        self.bufs = _run_exchange("gather_first_d2d", _gather_d2d(self.srcs, bufs, self.first))

    @staticmethod
    def _halves(a):
        return a.reshape(a.shape[0], 2, a.shape[1] // 2, a.shape[2])

    def params(self, ranges):
        w = self.w
        full = {n: w[n] for n in REPLICATED}
        for name, buf, (l0, nl) in zip(BIG, self.bufs, ranges):
            a = buf.reshape(buf.shape[:2] + (-1, buf.shape[4]))
            have = range(l0, l0 + nl)
            if name in DIRECT:
                full[name] = [Gathered(a, BIG[name], l) if l in have else None for l in range(a.shape[1])]
            else:
                full[name] = [_chips_to_full(a[:, l:l + 1], BIG[name])[0] if l in have else None for l in range(a.shape[1])]
        got, off = self.bufs[-1].reshape(N_CHIPS, -1), 0
        for name in SMALL_SHARDED:
            shape, d = TABLE[name]
            n = int(np.prod(_shard_shape(shape, d)))
            full[name] = _from_shard_major(got[:, off:off + n], shape, d)
            off += n
        self.meta = full.pop("meta_tokens")
        return _layout_params(full)

    def forward_exchange(self):
        return _gather_ici(self.srcs, self.bufs, self.rest)

    def after_forward_exchange(self, arrived):
        self.bufs = _run_exchange("gather_rest_d2d", _gather_d2d(self.srcs, arrived, self.rest))
        return self.params([(0, TABLE[n][0][0]) for n in BIG])

    def _pair_sums(self, pieces, tag):
        keys = list(pieces)
        ras = _run_exchange("grads_pair_exchange_" + tag, _pair_exchange([pieces[k] for k in keys]))
        return {k: _sum_pair(pieces[k], ra, self.c, "grads_pair_sum") for k, ra in zip(keys, ras)}

    def _q_shapes(self):
        return [jax.ShapeDtypeStruct((N_CHIPS - 1, s.shape[0]) + s.shape[2:], BF16) for s in self.srcs[:-1]]

    def backward_exchange(self, grads, layer):
        big = _big_pieces(grads)
        pieces = {(g, l): pc.reshape(N_CHIPS, 2, pc.shape[1] // 2, pc.shape[2]) for g, name in enumerate(BIG)
                  for l, pc in enumerate(big[name]) if pc is not None and (g, l) not in self.riding}
        if layer > 0:
            self.riding = pieces
            return _pair_exchange(list(pieces.values()))
        self.ps = {k: _sum_pair(self.riding[k], ra, self.c, "grads_pair_sum") for k, ra in self.ras.items()}
        self.ps.update(self._pair_sums(pieces, "early"))
        self.early = list(self.ps)
        return _chip_exchange([self.ps[k] for k in self.early], self.early, [None] * len(BIG), self._q_shapes())

    def after_backward_exchange(self, arrived, layer):
        if layer > 0:
            self.ras = dict(zip(self.riding, arrived))
        else:
            self.qs = list(arrived)

    def finish(self, grads, dh):
        big, small = _big_pieces(grads), _small_grads(grads, dh)
        pieces = {(g, l): pc.reshape(N_CHIPS, 2, pc.shape[1] // 2, pc.shape[2])
                  for g, name in enumerate(BIG) for l, pc in enumerate(big[name]) if (g, l) not in self.ps}
        sharded = jnp.concatenate([_shard_major(small[n], TABLE[n][1]) for n in SMALL_SHARDED], axis=1)
        rep = _pad_cols(jnp.concatenate([small[n].reshape(-1) for n in REPLICATED]), N_CHIPS * FLAT_QUANTUM)
        n_sh, n_rep = sharded.shape[1], rep.shape[0] // N_CHIPS
        flat = _pad_cols(jnp.concatenate([sharded, rep.reshape(N_CHIPS, n_rep)], axis=1), FLAT_QUANTUM)
        pieces[(len(BIG), 0)] = flat.reshape(N_CHIPS, 2, -1, LANES)
        late = self._pair_sums(pieces, "late")
        self.ps.update(late)
        keys = list(late)
        small_q = jax.ShapeDtypeStruct((N_CHIPS - 1, 1) + late[(len(BIG), 0)].shape[1:], BF16)
        qs = _run_exchange("grads_chip_exchange_late",
                           _chip_exchange([late[k] for k in keys], keys, self.qs + [None], self._q_shapes() + [small_q]))
        pos = [lax.axis_index(a).reshape(1).astype(jnp.int32) for a in ("x", "y", "c")]
        sums = [_sum_chips([self.ps[(g, l)] for l in range(q.shape[1])], q, pos, "grads_chip_sum") for g, q in enumerate(qs)]
        outs = _run_exchange("grads_pair_share", _pair_share(sums))
        out = {name: o.reshape(o.shape[0], -1, o.shape[3]) for name, o in zip(BIG, outs)}
        f = outs[-1].reshape(-1)
        rep_all = _gather_chips([f[n_sh:n_sh + n_rep].reshape(1, 2, -1, LANES)], "grads_gather_replicated")[0].reshape(-1)
        off = 0
        for name in SMALL_SHARDED:
            ss = _shard_shape(*TABLE[name])
            n = int(np.prod(ss))
            out[name] = f[off:off + n].reshape(ss)
            off += n
        off = 0
        for name in REPLICATED:
            shape = TABLE[name][0]
            n = int(np.prod(shape))
            out[name] = rep_all[off:off + n].reshape(shape)
            off += n
        return out


def kernel(x, meta_tokens, mix_pre_g, mix_post_g, mlp_pre_g, mlp_post_g, w_up, w_down, w_in, ssd_conv_w, ssd_conv_b, ssd_dt_bias, ssd_a_log, ssd_d, ssd_norm_g, mla_q_norm_g, mla_w_q_up, mla_kv_norm_g, mla_w_kv_up, w_out_ab, rg_w_x, rg_w_y, rg_conv_w, rg_conv_b, rg_w_a, rg_b_a, rg_w_i, rg_b_i, rg_lambda, rg_w_out, loss_target, m_meta_tokens, m_mix_pre_g, m_mix_post_g, m_mlp_pre_g, m_mlp_post_g, m_w_up, m_w_down, m_w_in, m_ssd_conv_w, m_ssd_conv_b, m_ssd_dt_bias, m_ssd_a_log, m_ssd_d, m_ssd_norm_g, m_mla_q_norm_g, m_mla_w_q_up, m_mla_kv_norm_g, m_mla_w_kv_up, m_w_out_ab, m_rg_w_x, m_rg_w_y, m_rg_conv_w, m_rg_conv_b, m_rg_w_a, m_rg_b_a, m_rg_w_i, m_rg_b_i, m_rg_lambda, m_rg_w_out, v_meta_tokens, v_mix_pre_g, v_mix_post_g, v_mlp_pre_g, v_mlp_post_g, v_w_up, v_w_down, v_w_in, v_ssd_conv_w, v_ssd_conv_b, v_ssd_dt_bias, v_ssd_a_log, v_ssd_d, v_ssd_norm_g, v_mla_q_norm_g, v_mla_w_q_up, v_mla_kv_norm_g, v_mla_w_kv_up, v_w_out_ab, v_rg_w_x, v_rg_w_y, v_rg_conv_w, v_rg_conv_b, v_rg_w_a, v_rg_b_a, v_rg_w_i, v_rg_b_i, v_rg_lambda, v_rg_w_out):
    names = [n for n, _, _ in WEIGHTS]
    w = dict(zip(names, (meta_tokens, mix_pre_g, mix_post_g, mlp_pre_g, mlp_post_g, w_up, w_down, w_in, ssd_conv_w, ssd_conv_b, ssd_dt_bias, ssd_a_log, ssd_d, ssd_norm_g, mla_q_norm_g, mla_w_q_up, mla_kv_norm_g, mla_w_kv_up, w_out_ab, rg_w_x, rg_w_y, rg_conv_w, rg_conv_b, rg_w_a, rg_b_a, rg_w_i, rg_b_i, rg_lambda, rg_w_out)))
    m = dict(zip(names, (m_meta_tokens, m_mix_pre_g, m_mix_post_g, m_mlp_pre_g, m_mlp_post_g, m_w_up, m_w_down, m_w_in, m_ssd_conv_w, m_ssd_conv_b, m_ssd_dt_bias, m_ssd_a_log, m_ssd_d, m_ssd_norm_g, m_mla_q_norm_g, m_mla_w_q_up, m_mla_kv_norm_g, m_mla_w_kv_up, m_w_out_ab, m_rg_w_x, m_rg_w_y, m_rg_conv_w, m_rg_conv_b, m_rg_w_a, m_rg_b_a, m_rg_w_i, m_rg_b_i, m_rg_lambda, m_rg_w_out)))
    v = dict(zip(names, (v_meta_tokens, v_mix_pre_g, v_mix_post_g, v_mlp_pre_g, v_mlp_post_g, v_w_up, v_w_down, v_w_in, v_ssd_conv_w, v_ssd_conv_b, v_ssd_dt_bias, v_ssd_a_log, v_ssd_d, v_ssd_norm_g, v_mla_q_norm_g, v_mla_w_q_up, v_mla_kv_norm_g, v_mla_w_kv_up, v_w_out_ab, v_rg_w_x, v_rg_w_y, v_rg_conv_w, v_rg_conv_b, v_rg_w_a, v_rg_b_a, v_rg_w_i, v_rg_b_i, v_rg_lambda, v_rg_w_out)))
    ex = StepExchanges(w)
    p = ex.params(ex.first)
    sq, dh, grads = _device_step(x[0], ex.meta, loss_target[0], p, hooks=ex)
    loss = lax.psum(0.5 * sq[0, 0] / D_MODEL, ("x", "y", "c"))
    g = ex.finish(grads, dh)
    grad, delta, new_m, new_v = {}, {}, {}, {}
    for name in names:
        shape = g[name].shape
        two_d = (int(np.prod(shape[:-1])), shape[-1])
        res = _adamw(g[name].reshape(two_d), w[name].reshape(two_d), m[name].reshape(two_d), v[name].reshape(two_d), "adamw")
        grad[name], delta[name], new_m[name], new_v[name] = (r.reshape(shape) for r in res)
    grad_x = dh[PAD + N_META:][None]
    return (loss, grad_x, *[grad[n] for n in names], *[delta[n] for n in names], *[new_m[n] for n in names], *[new_v[n] for n in names])
```

```python
import functools

import jax
import jax.numpy as jnp
import numpy as np
from jax import lax
from jax.experimental import pallas as pl
from jax.experimental.pallas import tpu as pltpu

F32 = jnp.float32
BF16 = jnp.bfloat16

D_MODEL = 1024
DEPTH = 4
N_META = 16
CHUNK = 128
PAD = CHUNK - N_META
EPS = 1e-6
SSD_HEADS = 16
SSD_HEAD_DIM = 64
SSD_D_INNER = SSD_HEADS * SSD_HEAD_DIM
SSD_GROUPS = 2
SSD_STATE = 128
SSD_CONV_CH = SSD_D_INNER + 2 * SSD_GROUPS * SSD_STATE
MLA_HEADS = 16
MLA_NOPE = 64
MLA_ROPE = 32
MLA_V = 64
MLA_Q_RANK = 384
MLA_KV_RANK = 256
ROPE_BASE = 10000.0
LRU_WIDTH = 1280
LRU_BLOCKS = 10
LRU_BLOCK = 128
LRU_C = 8.0
D_FF = 4 * D_MODEL
ADAM_LR, ADAM_B1, ADAM_B2, ADAM_EPS, ADAM_WD, ADAM_STEP = 0.001, 0.9, 0.999, 1e-08, 0.01, 10

LANES = 128
VMEM_LIMIT = 56 * 1024 * 1024
MM_VMEM_BUDGET = 40 * 1024 * 1024
PROJ_Z, PROJ_XBC, PROJ_DT, PROJ_CQ, PROJ_CKV, PROJ_KR = 0, 1024, 2560, 2688, 3072, 3328
PROJ_W = 3456


def _tile(n, cap, mult=8):
    for t in range(min(n, cap), 0, -1):
        if n % t == 0 and t % mult == 0:
            return t
    return n


def _params(sem):
    return pltpu.CompilerParams(dimension_semantics=sem, vmem_limit_bytes=VMEM_LIMIT)


def _full_spec(shape, ngrid):
    nd = len(shape)
    if ngrid == 1:
        return pl.BlockSpec(shape, lambda i: (0,) * nd)
    if ngrid == 2:
        return pl.BlockSpec(shape, lambda i, j: (0,) * nd)
    return pl.BlockSpec(shape, lambda i, j, k: (0,) * nd)


_DIMS = {"nn": (((1,), (0,)), ((), ())), "nt": (((1,), (1,)), ((), ())), "tn": (((0,), (0,)), ((), ()))}


class Gathered:
    def __init__(self, arr, kind, layer):
        self.arr, self.kind, self.layer = arr, kind, layer
        _, _, r, c = arr.shape
        self.shape = (r, N_CHIPS * c) if kind == "col" else (N_CHIPS * r, c)


N_CHIPS = 4


def _mm(a, b, mode, name, out_dtype=F32, add=None, out_chip_major=False, extra=(), vecs=(), slots=(), post=None, out_dtypes=None):
    if mode == "nn":
        (m, kc), (_, n) = a.shape, b.shape
    elif mode == "nt":
        (m, kc), (n, _) = a.shape, b.shape
    else:
        (kc, m), (_, n) = a.shape, b.shape
    n_tile = n // N_CHIPS if out_chip_major else n
    across = isinstance(b, Gathered) and (mode, b.kind) in (("nn", "row"), ("nt", "col"))
    if mode == "tn":
        tm, tk = _tile(m, 1024, LANES), kc
        fits = [c for c in (1280, 1152, 1024, 768, 640, 512) if n_tile % c == 0 and
                2 * kc * (tm * a.dtype.itemsize + c * b.dtype.itemsize) + 2 * tm * c * 4 <= MM_VMEM_BUDGET]
        tn = fits[0] if fits else _tile(n_tile, 1280, LANES)
        if not fits:
            tk = _tile(kc, 1408, LANES)
    else:
        tn = _tile(n_tile, 1280, LANES)
        tk = _tile(kc, 4096, LANES)
        tm = _tile(m, 1056 if tk <= 1024 else 528, 16)
    nk = kc // tk
    if mode == "tn":
        a_spec = pl.BlockSpec((tk, tm), lambda i, j, k: (k, i))
    else:
        a_spec = pl.BlockSpec((tm, tk), lambda i, j, k: (i, k))
    b_arrs = [b]
    if isinstance(b, Gathered):
        layer = b.layer
        sr, sc = b.arr.shape[2:]
        if across:
            assert nk == 1 and kc == N_CHIPS * (sr if b.kind == "row" else sc)
            b_arrs = [b.arr] * N_CHIPS
            if b.kind == "row":
                b_specs = [pl.BlockSpec((None, None, sr, tn), lambda i, j, k, s=s: (s, layer, 0, j)) for s in range(N_CHIPS)]
            else:
                b_specs = [pl.BlockSpec((None, None, tn, sc), lambda i, j, k, s=s: (s, layer, j, 0)) for s in range(N_CHIPS)]
        else:
            b_arrs = [b.arr]
            br, bc = (tk, tn) if mode == "nn" else (tn, tk)
            assert mode in ("nn", "nt") and sr % br == 0 and sc % bc == 0

            def b_map(i, j, k):
                r, c = (k, j) if mode == "nn" else (j, k)
                if b.kind == "col":
                    return ((c * bc) // sc, layer, r, ((c * bc) % sc) // bc)
                return ((r * br) // sr, layer, ((r * br) % sr) // br, c)

            b_specs = [pl.BlockSpec((None, None, br, bc), b_map)]
    elif mode == "nt":
        b_specs = [pl.BlockSpec((tn, tk), lambda i, j, k: (j, k))]
    else:
        b_specs = [pl.BlockSpec((tk, tn), lambda i, j, k: (k, j))]
    nb = len(b_arrs)
    dims = _DIMS[mode]
    if out_chip_major:
        ns = n // N_CHIPS
        o_spec = pl.BlockSpec((None, tm, tn), lambda i, j, k: ((j * tn) // ns, i, ((j * tn) % ns) // tn))
        o_shape = jax.ShapeDtypeStruct((N_CHIPS, m, ns), out_dtype)
    else:
        o_spec = pl.BlockSpec((tm, tn), lambda i, j, k: (i, j))
        o_shape = jax.ShapeDtypeStruct((m, n), out_dtype)
    extra = list(extra) + ([add] if add is not None else [])
    if add is not None:
        post = lambda v, x: (v + x,)
    vecs, slots = list(vecs), list(slots)
    nx = len(extra) + len(vecs) + len(slots)
    out_dtypes = out_dtypes or [out_dtype]
    no = len(out_dtypes)

    def body(a_ref, *rest):
        b_refs, rest = rest[:nb], rest[nb:]
        o_refs, acc = rest[nx:nx + no], rest[nx + no:]
        if across:
            w = kc // N_CHIPS
            p = functools.reduce(jnp.add, [
                lax.dot_general(a_ref[:, s * w:(s + 1) * w].astype(BF16), b_refs[s][...].astype(BF16), dims, preferred_element_type=F32)
                for s in range(N_CHIPS)])
        else:
            p = lax.dot_general(a_ref[...].astype(BF16), b_refs[0][...].astype(BF16), dims, preferred_element_type=F32)

        def emit(v):
            res = post(v, *[r[...] for r in rest[:nx]]) if post else (v,)
            for o_ref, r in zip(o_refs, res):
                o_ref[...] = r.astype(o_ref.dtype)

        if nk == 1:
            emit(p)
        else:
            k = pl.program_id(2)

            @pl.when(k == 0)
            def _():
                acc[0][...] = p

            @pl.when(k > 0)
            def _():
                acc[0][...] += p

            @pl.when(k == nk - 1)
            def _():
                emit(acc[0][...])

    res = pl.pallas_call(
        body, name=name, grid=(m // tm, n // tn, nk),
        in_specs=[a_spec] + b_specs + [o_spec] * len(extra) + [pl.BlockSpec((1, tn), lambda i, j, k: (0, j))] * len(vecs)
        + [pl.BlockSpec((tm, LANES), lambda i, j, k: (i, 0))] * len(slots),
        out_specs=[o_spec] * no,
        out_shape=[jax.ShapeDtypeStruct(o_shape.shape, dt) for dt in out_dtypes],
        scratch_shapes=[pltpu.VMEM((tm, tn), F32)] if nk > 1 else [],
        compiler_params=_params(("parallel", "parallel", "arbitrary")),
    )(a, *b_arrs, *extra, *vecs, *slots)
    return res[0] if no == 1 else res


def _rowarg(r):
    return r if isinstance(r, tuple) else (r, r.shape[1], 0)


def _rowspec(r, tr, ncol):
    _, w, cb = r
    if ncol > 1:
        return pl.BlockSpec((tr, w // ncol), lambda j, i: (i, j))
    return pl.BlockSpec((tr, w), lambda j, i: (i, cb))


def _rowwise(name, f, rows, params, outs, tr=None, ncol=1):
    rows = [_rowarg(r) for r in rows]
    t = rows[0][0].shape[0]
    tr = tr or _tile(t, 528)
    nr, npm = len(rows), len(params)

    def body(*refs):
        vals = [r[...] for r in refs[:nr]] + [(p[0] if ncol > 1 else p[...]) for p in refs[nr:nr + npm]]
        res = f(pl.program_id(1) * tr, *vals)
        for o_ref, v in zip(refs[nr + npm:], res):
            o_ref[...] = v.astype(o_ref.dtype)

    def pspec(p):
        if ncol > 1:
            return pl.BlockSpec((1,) + p.shape[1:], lambda j, i, n=p.ndim: (j,) + (0,) * (n - 1))
        return _full_spec(p.shape, 2)

    return pl.pallas_call(
        body, name=name, grid=(ncol, t // tr),
        in_specs=[_rowspec(r, tr, ncol) for r in rows] + [pspec(p) for p in params],
        out_specs=[pl.BlockSpec((tr, w // ncol), lambda j, i: (i, j)) for w, _ in outs],
        out_shape=[jax.ShapeDtypeStruct((t, w), dt) for w, dt in outs],
        compiler_params=_params(("parallel", "parallel")),
    )(*[r[0] for r in rows], *params)


def _rowwise_vjp(name, f, rows, params, cts, tr=None, ncol=1, row_dtypes=None):
    rows = [_rowarg(r) for r in rows]
    cts = [_rowarg(c) for c in cts]
    t = rows[0][0].shape[0]
    tr = tr or _tile(t, 528)
    nr, npm, nc = len(rows), len(params), len(cts)
    row_dtypes = row_dtypes or [F32] * nr

    def body(*refs):
        i = pl.program_id(1)
        vals = [r[...] for r in refs[:nr]] + [(p[0] if ncol > 1 else p[...]) for p in refs[nr:nr + npm]]
        ct = tuple(c[...].astype(F32) for c in refs[nr + npm:nr + npm + nc])
        _, vjp = jax.vjp(lambda *a: tuple(f(i * tr, *a)), *vals)
        g = vjp(ct)
        outs = refs[nr + npm + nc:]
        for o_ref, v in zip(outs[:nr], g[:nr]):
            o_ref[...] = v.astype(o_ref.dtype)
        pg = [(v[None] if ncol > 1 else v) for v in g[nr:]]

        @pl.when(i == 0)
        def _():
            for o_ref, v in zip(outs[nr:], pg):
                o_ref[...] = v

        @pl.when(i > 0)
        def _():
            for o_ref, v in zip(outs[nr:], pg):
                o_ref[...] += v

    def pspec(p):
        if ncol > 1:
            return pl.BlockSpec((1,) + p.shape[1:], lambda j, i, n=p.ndim: (j,) + (0,) * (n - 1))
        return _full_spec(p.shape, 2)

    res = pl.pallas_call(
        body, name=name, grid=(ncol, t // tr),
        in_specs=[_rowspec(r, tr, ncol) for r in rows] + [pspec(p) for p in params] + [_rowspec(c, tr, ncol) for c in cts],
        out_specs=[pl.BlockSpec((tr, w // ncol), lambda j, i: (i, j)) for _, w, _ in rows] + [pspec(p) for p in params],
        out_shape=[jax.ShapeDtypeStruct((t, w), dt) for (_, w, _), dt in zip(rows, row_dtypes)]
        + [jax.ShapeDtypeStruct(p.shape, F32) for p in params],
        compiler_params=_params(("parallel", "arbitrary")),
    )(*[r[0] for r in rows], *params, *[c[0] for c in cts])
    return res[:nr], res[nr:]


def _valid(row0, tr):
    return (row0 + lax.broadcasted_iota(jnp.int32, (tr, 1), 0)) >= PAD


def _rms(x, g):
    return x * lax.rsqrt(jnp.mean(x * x, axis=-1, keepdims=True) + EPS) * g


def _softplus(x):
    return jnp.where(x < -15.0, jnp.exp(x), jnp.maximum(x, 0.0) + jnp.log(1.0 + jnp.exp(-jnp.abs(x))))


def _neg_expm1(z):
    return jnp.where(z > -0.01, -z * (1.0 + z * (0.5 + z * (1.0 / 6.0))), 1.0 - jnp.exp(z))


def _prenorm(h, g, name):
    return _rowwise(name, lambda r0, x, gg: (_rms(x, gg),), [h], [g], [(_rowarg(h)[1], BF16)])[0]


def _post_residual(m, h, g):
    assert m.shape[1] == D_MODEL
    return m, h + _rms(m, g)


def _postnorm_bwd(m, g, dh, name):
    (dm,), (dg,) = _rowwise_vjp(name, lambda r0, mm, gg: (_rms(mm, gg),), [m], [g], [dh], row_dtypes=[BF16])
    return dm, dg


def _prenorm_bwd_add(h, g, dhns, dh, name):
    t, w = h.shape
    tr = _tile(t, 528)
    nd = len(dhns)

    def body(h_ref, g_ref, *refs):
        dh_ref, o_ref, dg_ref = refs[nd:]
        i = pl.program_id(0)
        _, vjp = jax.vjp(_rms, h_ref[...], g_ref[...])
        dhn = refs[0][...].astype(F32)
        for r in refs[1:nd]:
            dhn = dhn + r[...].astype(F32)
        dx, dg = vjp(dhn)
        o_ref[...] = dh_ref[...] + dx

        @pl.when(i == 0)
        def _():
            dg_ref[...] = dg

        @pl.when(i > 0)
        def _():
            dg_ref[...] += dg

    row = pl.BlockSpec((tr, w), lambda i: (i, 0))
    return pl.pallas_call(
        body, name=name, grid=(t // tr,), in_specs=[row, _full_spec(g.shape, 1)] + [row] * (nd + 1),
        out_specs=[row, _full_spec(g.shape, 1)],
        out_shape=[jax.ShapeDtypeStruct((t, w), F32), jax.ShapeDtypeStruct(g.shape, F32)],
        compiler_params=_params(("arbitrary",)),
    )(h, g, *dhns, dh)


def _loss_and_grad(h, target, name):
    t, w = h.shape
    nb = t // CHUNK

    def body(h_ref, t_ref, s_ref, dh_ref):
        i = pl.program_id(0)

        @pl.when(i == 0)
        def _():
            s_ref[...] = jnp.zeros_like(s_ref)
            dh_ref[...] = jnp.zeros_like(dh_ref)

        @pl.when(i > 0)
        def _():
            err = h_ref[...] - t_ref[...]
            s_ref[...] += jnp.sum(err * err)
            dh_ref[...] = err * (1.0 / w)

    return pl.pallas_call(
        body, name=name, grid=(nb,),
        in_specs=[pl.BlockSpec((CHUNK, w), lambda i: (i, 0)), pl.BlockSpec((CHUNK, w), lambda i: (jnp.maximum(i - 1, 0), 0))],
        out_specs=[_full_spec((1, LANES), 1), pl.BlockSpec((CHUNK, w), lambda i: (i, 0))],
        out_shape=[jax.ShapeDtypeStruct((1, LANES), F32), jax.ShapeDtypeStruct((t, w), F32)],
        compiler_params=_params(("arbitrary",)),
    )(h, target)


def _mlp_fwd(h, p, l):
    hn = _prenorm(h, p["mlp_pre_g"][l], "mlp_prenorm")
    a, u = _mm(hn, p["w_up"][l], "nn", "mlp_up", post=lambda v: (v, jnp.square(jnp.maximum(v, 0.0))), out_dtypes=[BF16, BF16])
    d, h2 = _mm(u, p["w_down"][l], "nn", "mlp_down", extra=[h], vecs=[p["mlp_post_g"][l]], post=_post_residual, out_dtypes=[F32, F32])
    return h2, (h, hn, a, u, d)


def _mlp_bwd(dh, saved, p, l, grads):
    h, hn, a, u, d = saved
    dd, grads["mlp_post_g"][l] = _postnorm_bwd(d, p["mlp_post_g"][l], dh, "mlp_postnorm_bwd")
    grads["w_down"][l] = _mm(u, dd, "tn", "mlp_down_dw")
    da = _mm(dd, p["w_down"][l], "nt", "mlp_down_dx", extra=[a], post=lambda v, x: (2.0 * jnp.maximum(x.astype(F32), 0.0) * v,),
             out_dtypes=[BF16])
    grads["w_up"][l] = _mm(hn, da, "tn", "mlp_up_dw", out_chip_major=True)
    dhn = _mm(da, p["w_up"][l], "nt", "mlp_up_dx")
    dh, grads["mlp_pre_g"][l] = _prenorm_bwd_add(h, p["mlp_pre_g"][l], [dhn], dh, "mlp_prenorm_bwd")
    return dh


def _dot(a, b, mode):
    return lax.dot_general(a.astype(BF16), b.astype(BF16), _DIMS[mode], preferred_element_type=F32)


@jax.custom_vjp
def _bnn(a, b):
    return _dot(a, b, "nn")


_bnn.defvjp(lambda a, b: (_dot(a, b, "nn"), (a, b)), lambda r, ct: (_dot(ct, r[1], "nt"), _dot(r[0], ct, "tn")))


@jax.custom_vjp
def _bnt(a, b):
    return _dot(a, b, "nt")


_bnt.defvjp(lambda a, b: (_dot(a, b, "nt"), (a, b)), lambda r, ct: (_dot(ct, r[1], "nn"), _dot(ct, r[0], "tn")))


@jax.custom_vjp
def _btn(a, b):
    return _dot(a, b, "tn")


_btn.defvjp(lambda a, b: (_dot(a, b, "tn"), (a, b)), lambda r, ct: (_dot(r[1], ct, "nt"), _dot(r[0], ct, "nn")))


CONV_K = 4
HALO = 8


def _conv_fwd(x, w, b, name, cw, c0=0):
    t, c = x.shape[0], w.shape[1]
    tr = _tile(t, 528)
    hb = tr // HALO

    def body(x_ref, halo_ref, w_ref, b_ref, o_ref, ext):
        i = pl.program_id(1)
        ext[pl.ds(0, HALO), :] = jnp.where(i > 0, halo_ref[...], 0.0)
        ext[pl.ds(HALO, tr), :] = x_ref[...]
        acc = jnp.broadcast_to(b_ref[...], (tr, cw))
        for k in range(CONV_K):
            acc = acc + w_ref[pl.ds(k, 1), :] * ext[pl.ds(HALO - (CONV_K - 1) + k, tr), :]
        o_ref[...] = acc

    return pl.pallas_call(
        body, name=name, grid=(c // cw, t // tr),
        in_specs=[pl.BlockSpec((tr, cw), lambda j, i: (i, c0 + j)),
                  pl.BlockSpec((HALO, cw), lambda j, i: (jnp.maximum(i * hb - 1, 0), c0 + j)),
                  pl.BlockSpec((CONV_K, cw), lambda j, i: (0, j)), pl.BlockSpec((1, cw), lambda j, i: (0, j))],
        out_specs=pl.BlockSpec((tr, cw), lambda j, i: (i, j)),
        out_shape=jax.ShapeDtypeStruct((t, c), F32),
        scratch_shapes=[pltpu.VMEM((tr + HALO, cw), F32)],
        compiler_params=_params(("parallel", "parallel")),
    )(x, x, w, b)


def _conv_bwd(x, w, dy, name, cw, c0=0):
    t, c = x.shape[0], w.shape[1]
    tr = _tile(t, 528)
    hb = tr // HALO
    nb = t // tr

    def body(x_ref, xh_ref, w_ref, dy_ref, dyh_ref, dx_ref, dw_ref, db_ref, xe, de):
        c = cw
        i = pl.program_id(1)
        xe[pl.ds(0, HALO), :] = jnp.where(i > 0, xh_ref[...], 0.0)
        xe[pl.ds(HALO, tr), :] = x_ref[...]
        de[pl.ds(0, tr), :] = dy_ref[...]
        de[pl.ds(tr, HALO), :] = jnp.where(i < nb - 1, dyh_ref[...], 0.0)
        dy = dy_ref[...]
        acc = jnp.zeros((tr, c), F32)
        dw = jnp.zeros((CONV_K, c), F32)
        rows = lax.broadcasted_iota(jnp.int32, (CONV_K, 1), 0)
        for k in range(CONV_K):
            acc = acc + w_ref[pl.ds(k, 1), :] * de[pl.ds(CONV_K - 1 - k, tr), :]
            dwk = jnp.sum(dy * xe[pl.ds(HALO - (CONV_K - 1) + k, tr), :], axis=0, keepdims=True)
            dw = dw + jnp.where(rows == k, dwk, 0.0)
        dx_ref[...] = jnp.where(_valid(i * tr, tr), acc, 0.0).astype(dx_ref.dtype)
        db = jnp.sum(dy, axis=0, keepdims=True)

        @pl.when(i == 0)
        def _():
            dw_ref[...] = dw
            db_ref[...] = db

        @pl.when(i > 0)
        def _():
            dw_ref[...] += dw
            db_ref[...] += db

    row = pl.BlockSpec((tr, cw), lambda j, i: (i, j))
    return pl.pallas_call(
        body, name=name, grid=(c // cw, nb),
        in_specs=[pl.BlockSpec((tr, cw), lambda j, i: (i, c0 + j)),
                  pl.BlockSpec((HALO, cw), lambda j, i: (jnp.maximum(i * hb - 1, 0), c0 + j)),
                  pl.BlockSpec((CONV_K, cw), lambda j, i: (0, j)),
                  row, pl.BlockSpec((HALO, cw), lambda j, i: (jnp.minimum((i + 1) * hb, t // HALO - 1), j))],
        out_specs=[row, pl.BlockSpec((CONV_K, cw), lambda j, i: (0, j)), pl.BlockSpec((1, cw), lambda j, i: (0, j))],
        out_shape=[jax.ShapeDtypeStruct((t, c), BF16), jax.ShapeDtypeStruct((CONV_K, c), F32), jax.ShapeDtypeStruct((1, c), F32)],
        scratch_shapes=[pltpu.VMEM((tr + HALO, cw), F32), pltpu.VMEM((tr + HALO, cw), F32)],
        compiler_params=_params(("parallel", "arbitrary")),
    )(x, x, w, dy, dy)


SUB = 8


def _lru_scan(a, u, name):
    t, c = a.shape
    tr = _tile(t, 528)

    def body(a_ref, u_ref, o_ref, carry):
        @pl.when(pl.program_id(0) == 0)
        def _():
            carry[...] = jnp.zeros_like(carry)

        rows = lax.broadcasted_iota(jnp.int32, (SUB, 1), 0)

        def step(k, cin):
            r = pl.multiple_of(k * SUB, SUB)
            av, uv = a_ref[pl.ds(r, SUB), :], u_ref[pl.ds(r, SUB), :]
            for d in (1, 2, 4):
                m = rows >= d
                uv = uv + av * jnp.where(m, pltpu.roll(uv, d, 0), 0.0)
                av = av * jnp.where(m, pltpu.roll(av, d, 0), 1.0)
            hv = uv + av * cin
            o_ref[pl.ds(r, SUB), :] = hv
            return jnp.broadcast_to(hv[SUB - 1:SUB, :], (SUB, c))

        carry[...] = lax.fori_loop(0, tr // SUB, step, carry[...])

    row = pl.BlockSpec((tr, c), lambda i: (i, 0))
    return pl.pallas_call(
        body, name=name, grid=(t // tr,), in_specs=[row, row], out_specs=row,
        out_shape=jax.ShapeDtypeStruct((t, c), F32), scratch_shapes=[pltpu.VMEM((SUB, c), F32)],
        compiler_params=_params(("arbitrary",)),
    )(a, u)


def _lru_scan_bwd(a, hs, dy, name):
    t, c = a.shape
    tr = _tile(t, 528)
    nb, nt = t // tr, tr // SUB

    def body(a_ref, h_ref, hh_ref, dy_ref, du_ref, da_ref, gcar, acar):
        i = pl.program_id(0)

        @pl.when(i == 0)
        def _():
            gcar[...] = jnp.zeros_like(gcar)
            acar[...] = jnp.zeros_like(acar)

        rows = lax.broadcasted_iota(jnp.int32, (SUB, 1), 0)
        hhalo = jnp.where(i < nb - 1, hh_ref[...], 0.0)

        def step(kk, car):
            gin, a_next_first = car
            k = nt - 1 - kk
            r = pl.multiple_of(k * SUB, SUB)
            av, hv, dv = a_ref[pl.ds(r, SUB), :], h_ref[pl.ds(r, SUB), :], dy_ref[pl.ds(r, SUB), :]
            rp = pl.multiple_of(jnp.maximum(k - 1, 0) * SUB, SUB)
            hp = jnp.where(k > 0, h_ref[pl.ds(rp, SUB), :], hhalo)
            cv = jnp.where(rows < SUB - 1, pltpu.roll(av, SUB - 1, 0), a_next_first)
            gv = dv
            for d in (1, 2, 4):
                m = rows < SUB - d
                gv = gv + cv * jnp.where(m, pltpu.roll(gv, SUB - d, 0), 0.0)
                cv = cv * jnp.where(m, pltpu.roll(cv, SUB - d, 0), 1.0)
            gv = gv + cv * gin
            hprev = jnp.where(rows >= 1, pltpu.roll(hv, 1, 0), jnp.broadcast_to(hp[SUB - 1:SUB, :], (SUB, c)))
            du_ref[pl.ds(r, SUB), :] = gv
            da_ref[pl.ds(r, SUB), :] = gv * hprev
            return jnp.broadcast_to(gv[0:1, :], (SUB, c)), jnp.broadcast_to(av[0:1, :], (SUB, c))

        g, af = lax.fori_loop(0, nt, step, (gcar[...], acar[...]))
        gcar[...] = g
        acar[...] = af

    hb = tr // SUB
    row = pl.BlockSpec((tr, c), lambda i: (nb - 1 - i, 0))
    halo = pl.BlockSpec((SUB, c), lambda i: (jnp.maximum((nb - 1 - i) * hb - 1, 0), 0))
    return pl.pallas_call(
        body, name=name, grid=(nb,), in_specs=[row, row, halo, row], out_specs=[row, row],
        out_shape=[jax.ShapeDtypeStruct((t, c), F32)] * 2,
        scratch_shapes=[pltpu.VMEM((SUB, c), F32), pltpu.VMEM((SUB, c), F32)],
        compiler_params=_params(("arbitrary",)),
    )(a, hs, hs, dy)


def _lru_gates(row0, xr, wa, ba, wi, bi, lam):
    r = jax.nn.sigmoid(_bnn(xr, wa) + ba)
    i = jax.nn.sigmoid(_bnn(xr, wi) + bi)
    log_a = -LRU_C * r * _softplus(-lam)
    u = jnp.sqrt(_neg_expm1(2.0 * log_a)) * (i * xr)
    return jnp.exp(log_a), jnp.where(_valid(row0, xr.shape[0]), u, 0.0)


def _lru_gate_out(row0, hs, yw):
    return (hs * jax.nn.gelu(yw),)


def _rglru_fwd(h, p, l, o):
    hn = _prenorm(h, p["mix_pre_g"][l], "rg_prenorm")
    xw = _mm(hn, p["rg_w_x"][o], "nn", "rg_in_x")
    yw = _mm(hn, p["rg_w_y"][o], "nn", "rg_in_y")
    xr = _conv_fwd(xw, p["rg_conv_w"][o], p["rg_conv_b"][o], "rg_conv", cw=LRU_WIDTH // 2)
    gp = [p["rg_w_a"][o], p["rg_b_a"][o], p["rg_w_i"][o], p["rg_b_i"][o], p["rg_lambda"][o]]
    a, u = _rowwise("rg_gates", _lru_gates, [xr], gp, [(LRU_WIDTH, F32)] * 2, ncol=LRU_BLOCKS, tr=_tile(h.shape[0], 1056))
    hs = _lru_scan(a, u, "rg_scan")
    hg = _rowwise("rg_gate_out", _lru_gate_out, [hs, yw], [], [(LRU_WIDTH, BF16)])[0]
    m, h2 = _mm(hg, p["rg_w_out"][o], "nn", "rg_out", extra=[h], vecs=[p["mix_post_g"][l]], post=_post_residual, out_dtypes=[F32, F32])
    return h2, (h, hn, xw, yw, xr, a, hs, hg, m)


def _rglru_bwd(dh, saved, p, l, o, grads):
    h, hn, xw, yw, xr, a, hs, hg, m = saved
    dm, grads["mix_post_g"][l] = _postnorm_bwd(m, p["mix_post_g"][l], dh, "rg_postnorm_bwd")
    grads["rg_w_out"][o] = _mm(hg, dm, "tn", "rg_out_dw")
    dhg = _mm(dm, p["rg_w_out"][o], "nt", "rg_out_dx")
    (dhs, dyw), _ = _rowwise_vjp("rg_gate_out_bwd", _lru_gate_out, [hs, yw], [], [dhg], row_dtypes=[F32, BF16])
    du, da = _lru_scan_bwd(a, hs, dhs, "rg_scan_bwd")
    gp = [p["rg_w_a"][o], p["rg_b_a"][o], p["rg_w_i"][o], p["rg_b_i"][o], p["rg_lambda"][o]]
    (dxr,), gg = _rowwise_vjp("rg_gates_bwd", _lru_gates, [xr], gp, [da, du], ncol=LRU_BLOCKS, tr=_tile(h.shape[0], 1056))
    grads["rg_w_a"][o], grads["rg_b_a"][o], grads["rg_w_i"][o], grads["rg_b_i"][o], grads["rg_lambda"][o] = gg
    dxw, grads["rg_conv_w"][o], grads["rg_conv_b"][o] = _conv_bwd(xw, p["rg_conv_w"][o], dxr, "rg_conv_bwd", cw=LRU_WIDTH // 2)
    grads["rg_w_x"][o] = _mm(hn, dxw, "tn", "rg_in_x_dw")
    grads["rg_w_y"][o] = _mm(hn, dyw, "tn", "rg_in_y_dw")
    dhx = _mm(dxw, p["rg_w_x"][o], "nt", "rg_in_x_dx")
    dhy = _mm(dyw, p["rg_w_y"][o], "nt", "rg_in_y_dx")
    dh, grads["mix_pre_g"][l] = _prenorm_bwd_add(h, p["mix_pre_g"][l], [dhx, dhy], dh, "rg_prenorm_bwd")
    return dh


SSD_GW = SSD_D_INNER // SSD_GROUPS
SSD_GH = SSD_HEADS // SSD_GROUPS
XACT_B = SSD_D_INNER // SSD_STATE
XACT_C = XACT_B + SSD_GROUPS


def _hp(a, b, dims=_DIMS["nn"]):
    return lax.dot_general(a, b, dims, precision=lax.Precision.HIGHEST, preferred_element_type=F32)


def _split_dot(a, e, mode, parts):
    eb = e.astype(BF16)
    out, rest = None, a
    for _ in range(parts):
        term = rest.astype(BF16)
        rest = rest - term.astype(F32)
        if mode in ("nn", "nt"):
            prod = lax.dot_general(term, eb, _DIMS[mode], preferred_element_type=F32)
        else:
            prod = lax.dot_general(eb, term, _DIMS["nn" if mode == "left" else "tn"], preferred_element_type=F32)
        out = prod if out is None else out + prod
    return out


@jax.custom_vjp
def _select_nn(a, e):
    return _split_dot(a, e, "nn", 3)


_select_nn.defvjp(lambda a, e: (_split_dot(a, e, "nn", 3), e), lambda e, ct: (_split_dot(ct, e, "nt", 2), jnp.zeros_like(e)))


@jax.custom_vjp
def _select_left(e, a):
    return _split_dot(a, e, "left", 3)


_select_left.defvjp(lambda e, a: (_split_dot(a, e, "left", 3), e),
                    lambda e, ct: (jnp.zeros_like(e), _split_dot(ct, e, "left_t", 2)))


def _ssd_chunk(xs, bm, cm, dt, da, ht, g):
    l = CHUNK
    ri = lax.broadcasted_iota(jnp.int32, (l, l), 0)
    ci = lax.broadcasted_iota(jnp.int32, (l, l), 1)
    causal = ri >= ci
    tri = causal.astype(F32)
    hr = lax.broadcasted_iota(jnp.int32, (LANES, SSD_GW), 0)
    hc = lax.broadcasted_iota(jnp.int32, (LANES, SSD_GW), 1)
    expand = (hr == g * SSD_GH + hc // SSD_HEAD_DIM).astype(F32)
    acs = _select_left(tri, da)
    acs_t = acs.T
    acs_e = _select_nn(acs, expand)
    x = xs * _select_nn(dt, expand)
    gmat = _bnt(cm, bm)
    lane = lax.broadcasted_iota(jnp.int32, (1, LANES), 1)
    sub = lax.broadcasted_iota(jnp.int32, (LANES, 1), 0)
    colhead = lax.broadcasted_iota(jnp.int32, (1, SSD_GW), 1) // SSD_HEAD_DIM
    y = _bnn(cm, ht) * jnp.exp(acs_e)
    for k in range(SSD_GH):
        hh = g * SSD_GH + k
        col = jnp.sum(jnp.where(lane == hh, acs, 0.0), axis=1, keepdims=True)
        row = jnp.sum(jnp.where(sub == hh, acs_t, 0.0), axis=0, keepdims=True)
        decay = jnp.exp(jnp.where(causal, col - row, -1e30))
        y = y + _bnn(gmat * decay, jnp.where(colhead == k, x, 0.0))
    last = lax.broadcasted_iota(jnp.int32, (l, 1), 0) == l - 1
    a_last = jnp.sum(jnp.where(last, acs_e, 0.0), axis=0, keepdims=True)
    st = _btn(bm, x * jnp.exp(a_last - acs_e))
    return y, ht * jnp.exp(a_last) + st


def _ssd_specs(nc, rev):
    def cc(c):
        return nc - 1 - c if rev else c

    return [pl.BlockSpec((CHUNK, SSD_GW), lambda c, g: (cc(c), g)),
            pl.BlockSpec((CHUNK, SSD_STATE), lambda c, g: (cc(c), XACT_B + g)),
            pl.BlockSpec((CHUNK, SSD_STATE), lambda c, g: (cc(c), XACT_C + g)),
            pl.BlockSpec((CHUNK, LANES), lambda c, g: (cc(c), 0)),
            pl.BlockSpec((CHUNK, LANES), lambda c, g: (cc(c), 0))]


def _ssd_scan(xact, dt, da, name):
    t = xact.shape[0]
    nc = t // CHUNK

    def body(xs_ref, b_ref, c_ref, dt_ref, da_ref, y_ref, hs_ref, state):
        c, g = pl.program_id(0), pl.program_id(1)

        @pl.when(c == 0)
        def _():
            state[g] = jnp.zeros((SSD_STATE, SSD_GW), F32)

        ht = state[g]
        hs_ref[0] = ht
        y, ht2 = _ssd_chunk(xs_ref[...], b_ref[...], c_ref[...], dt_ref[...], da_ref[...], ht, g)
        y_ref[...] = y
        state[g] = ht2

    return pl.pallas_call(
        body, name=name, grid=(nc, SSD_GROUPS), in_specs=_ssd_specs(nc, False),
        out_specs=[pl.BlockSpec((CHUNK, SSD_GW), lambda c, g: (c, g)),
                   pl.BlockSpec((1, SSD_STATE, SSD_GW), lambda c, g: (c * SSD_GROUPS + g, 0, 0))],
        out_shape=[jax.ShapeDtypeStruct((t, SSD_D_INNER), F32), jax.ShapeDtypeStruct((nc * SSD_GROUPS, SSD_STATE, SSD_GW), F32)],
        scratch_shapes=[pltpu.VMEM((SSD_GROUPS, SSD_STATE, SSD_GW), F32)],
        compiler_params=_params(("arbitrary", "arbitrary")),
    )(xact, xact, xact, dt, da)


def _ssd_scan_bwd(xact, dt, da, hsave, dy, dxskip, name):
    t = xact.shape[0]
    nc = t // CHUNK

    def body(xs_ref, b_ref, c_ref, dt_ref, da_ref, hs_ref, dy_ref, sk_ref, dxs_ref, db_ref, dc_ref, ddt_ref, dda_ref, dstate):
        c, g = pl.program_id(0), pl.program_id(1)

        @pl.when(c == 0)
        def _():
            dstate[g] = jnp.zeros((SSD_STATE, SSD_GW), F32)

        _, vjp = jax.vjp(lambda *a: _ssd_chunk(*a, g), xs_ref[...], b_ref[...], c_ref[...], dt_ref[...], da_ref[...], hs_ref[0])
        dxs, dbm, dcm, ddt, dda, dht = vjp((dy_ref[...], dstate[g]))
        dxs_ref[...] = dxs + sk_ref[...]
        db_ref[...] = dbm
        dc_ref[...] = dcm
        dstate[g] = dht

        @pl.when(g == 0)
        def _():
            ddt_ref[...] = ddt
            dda_ref[...] = dda

        @pl.when(g > 0)
        def _():
            ddt_ref[...] += ddt
            dda_ref[...] += dda

    grp = pl.BlockSpec((CHUNK, SSD_GW), lambda c, g: (nc - 1 - c, g))
    st = pl.BlockSpec((CHUNK, SSD_STATE), lambda c, g: (nc - 1 - c, g))
    hd = pl.BlockSpec((CHUNK, LANES), lambda c, g: (nc - 1 - c, 0))
    return pl.pallas_call(
        body, name=name, grid=(nc, SSD_GROUPS),
        in_specs=_ssd_specs(nc, True) + [pl.BlockSpec((1, SSD_STATE, SSD_GW), lambda c, g: ((nc - 1 - c) * SSD_GROUPS + g, 0, 0)), grp, grp],
        out_specs=[grp, st, st, hd, hd],
        out_shape=[jax.ShapeDtypeStruct((t, SSD_D_INNER), F32), jax.ShapeDtypeStruct((t, SSD_GROUPS * SSD_STATE), F32),
                   jax.ShapeDtypeStruct((t, SSD_GROUPS * SSD_STATE), F32), jax.ShapeDtypeStruct((t, LANES), F32),
                   jax.ShapeDtypeStruct((t, LANES), F32)],
        scratch_shapes=[pltpu.VMEM((SSD_GROUPS, SSD_STATE, SSD_GW), F32)],
        compiler_params=_params(("arbitrary", "arbitrary")),
    )(xact, xact, xact, dt, da, hsave, dy, dxskip)


def _ssd_act(row0, xc):
    return (jnp.where(_valid(row0, xc.shape[0]), jax.nn.silu(xc), 0.0),)


def _ssd_dt(row0, dtraw, dt_bias, a_log):
    dt = jnp.where(_valid(row0, dtraw.shape[0]), _softplus(dtraw + dt_bias), 0.0)
    return dt, dt * -jnp.exp(a_log)


def _ssd_post(row0, y, xs, z, d_skip, norm_g):
    hr = lax.broadcasted_iota(jnp.int32, (LANES, SSD_D_INNER), 0)
    hc = lax.broadcasted_iota(jnp.int32, (LANES, SSD_D_INNER), 1)
    expand = (hr == hc // SSD_HEAD_DIM).astype(F32)
    d_e = jnp.sum(_hp(jnp.broadcast_to(d_skip, (SUB, LANES)), expand), axis=0, keepdims=True) * (1.0 / SUB)
    return (_rms((y + xs * d_e) * jax.nn.silu(z), norm_g),)


ROPE_LO, ROPE_MID, ROPE_HI = MLA_NOPE, MLA_NOPE + MLA_ROPE // 2, MLA_NOPE + MLA_ROPE
ATT_SCALE = (MLA_NOPE + MLA_ROPE) ** -0.5


def _slot_lane(width):
    return lax.broadcasted_iota(jnp.int32, (1, width), 1) % LANES


def _swap_halves(x):
    width = x.shape[1]
    lane = _slot_lane(width)
    sw = jnp.where(lane < ROPE_MID, pltpu.roll(x, width - MLA_ROPE // 2, 1), pltpu.roll(x, MLA_ROPE // 2, 1))
    return jnp.where((lane >= ROPE_LO) & (lane < ROPE_HI), sw, 0.0)


def _rope(x, cos, sin):
    n = x.shape[1] // LANES
    return x * jnp.tile(cos, (1, n)) + _swap_halves(x) * jnp.tile(sin, (1, n))


def _rope_t(dy, cos, sin):
    n = dy.shape[1] // LANES
    return dy * jnp.tile(cos, (1, n)) + _swap_halves(dy * jnp.tile(sin, (1, n)))


ATT_SCALE2 = ATT_SCALE * float(np.log2(np.e))
MASKED = -1e30


def _att_bias(blk):
    r = jnp.arange(blk)[:, None]
    c = jnp.arange(blk)[None, :]
    zero = jnp.zeros((blk, blk), F32)
    first = jnp.where(c >= PAD, 0.0, MASKED) + zero
    diag = jnp.where(c <= r, 0.0, MASKED).astype(F32)
    return jnp.stack([zero, first, diag, jnp.minimum(first, diag), zero + MASKED])


def _att_bias_index(j, i):
    return jnp.where(j > i, 4, jnp.where(j == 0, 1, 0) + jnp.where(j == i, 2, 0))


def _key_slots(row0, kv, kr):
    width = kv.shape[1]
    return jnp.where(_slot_lane(width) < MLA_NOPE, kv, jnp.tile(kr, (1, width // LANES))), kv


def _attn_fwd(qr, km, vb, name, carried=None):
    t = qr.shape[0]
    blk = _tile(t, 384, LANES)
    nq = t // blk

    bias = _att_bias(blk)

    def body(q_ref, k_ref, v_ref, b_ref, o_ref, s0, s1, p0, p1):
        i = pl.program_id(1)
        lane = lax.broadcasted_iota(jnp.int32, (1, LANES), 1)
        qb = q_ref[...]

        def rows(j):
            return pl.ds(pl.multiple_of(jnp.clip(j, 0, i) * blk, blk), blk)

        def scores(j):
            return lax.dot_general(qb, k_ref[rows(j), :], _DIMS["nt"], preferred_element_type=F32) + b_ref[_att_bias_index(j, i)]

        def half(j, car, s_cur, s_nxt, p_cur, p_prv):
            m, l, acc, al_prev = car
            s_nxt[...] = scores(j + 1)
            acc2 = al_prev * acc + lax.dot_general(p_prv[...], v_ref[rows(j - 1), :], _DIMS["nn"], preferred_element_type=F32)
            m2 = jnp.maximum(m, jnp.max(s_cur[...], axis=1, keepdims=True))
            al = jnp.exp2((m - m2) * ATT_SCALE2)
            pm = jnp.exp2(s_cur[...] * ATT_SCALE2 - m2 * ATT_SCALE2)
            p_cur[...] = pm.astype(BF16)
            return m2, al * l + jnp.sum(pm, axis=1, keepdims=True), acc2, al

        def step(jj, car):
            car = half(2 * jj, car, s0, s1, p0, p1)
            return half(2 * jj + 1, car, s1, s0, p1, p0)

        s0[...] = scores(0)
        p1[...] = jnp.zeros((blk, blk), BF16)
        car = (jnp.full((blk, 1), MASKED, F32), jnp.zeros((blk, 1), F32), jnp.zeros((blk, LANES), F32), jnp.ones((blk, 1), F32))
        steps = i // 2 + 1
        m, l, acc, al_last = lax.fori_loop(0, steps, step, car)
        acc = al_last * acc + lax.dot_general(p1[...], v_ref[rows(2 * steps - 1), :], _DIMS["nn"], preferred_element_type=F32)
        out = jnp.where(lane >= MLA_NOPE, acc / l, m * ATT_SCALE + jnp.log(l))
        o_ref[...] = jnp.where(_valid(i * blk, blk), out, 0.0)

    seq_h = pl.BlockSpec((t, LANES), lambda h, i: (0, h))
    (o,), carried_out = _carry_call(
        body, name, (MLA_HEADS, nq),
        [pl.BlockSpec((blk, LANES), lambda h, i: (i, h)), seq_h, seq_h, _full_spec(bias.shape, 2)],
        [pl.BlockSpec((blk, LANES), lambda h, i: (i, h))], [jax.ShapeDtypeStruct((t, MLA_HEADS * LANES), F32)],
        [pltpu.VMEM((blk, blk), F32)] * 2 + [pltpu.VMEM((blk, blk), BF16)] * 2, (qr, km, vb, bias), carried)
    return o, carried_out


def _attn_bwd(qr, km, vb, o, do, cos, sin, name, carried=None):
    t = qr.shape[0]
    blk = _tile(t, 384, LANES)
    nq = t // blk

    bias = _att_bias(blk)
    log2e = float(np.log2(np.e))

    def body(q_ref, o_ref, do_ref, k_ref, v_ref, b_ref, cos_ref, sin_ref, dq_out, dkv_ref, dkr_ref,
             s0, s1, dp0, dp1, p0, p1, ds0, ds1, dk_s, dv_s, dq_ref):
        h, j = pl.program_id(0), pl.program_id(1)
        lane = lax.broadcasted_iota(jnp.int32, (1, LANES), 1)

        @pl.when(j == 0)
        def _():
            dq_ref[...] = jnp.zeros_like(dq_ref)

        @pl.when((h == 0) & (j == 0))
        def _():
            dkr_ref[...] = jnp.zeros_like(dkr_ref)

        kmat, vmat = k_ref[...], v_ref[...]

        def rows(i):
            return pl.ds(pl.multiple_of(jnp.clip(i, j, nq - 1) * blk, blk), blk)

        def first_stage(i, s_buf, dp_buf):
            ic = jnp.minimum(i, nq - 1)
            s_buf[...] = lax.dot_general(q_ref[rows(ic), :], kmat, _DIMS["nt"], preferred_element_type=F32) + b_ref[_att_bias_index(j, ic)]
            dp_buf[...] = lax.dot_general(do_ref[rows(ic), :].astype(BF16), vmat, _DIMS["nt"], preferred_element_type=F32)

        def middle_stage(i, s_buf, dp_buf, p_buf, ds_buf):
            r = rows(i)
            ob, dob = o_ref[r, :], do_ref[r, :]
            delta = jnp.sum(dob * ob, axis=1, keepdims=True)
            pm = jnp.exp2(s_buf[...] * ATT_SCALE2 - ob[:, 0:1] * log2e)
            p_buf[...] = pm.astype(BF16)
            ds_buf[...] = (pm * (dp_buf[...] - delta) * ATT_SCALE).astype(BF16)

        def last_stage(i, p_buf, ds_buf):
            r = rows(i)
            dv_s[...] += lax.dot_general(p_buf[...], do_ref[r, :].astype(BF16), _DIMS["tn"], preferred_element_type=F32)
            dk_s[...] += lax.dot_general(ds_buf[...], q_ref[r, :], _DIMS["tn"], preferred_element_type=F32)
            dq_ref[r, :] += lax.dot_general(ds_buf[...], kmat, _DIMS["nn"], preferred_element_type=F32)

        n = nq - j
        dk_s[...] = jnp.zeros((blk, LANES), F32)
        dv_s[...] = jnp.zeros((blk, LANES), F32)
        first_stage(j, s0, dp0)
        first_stage(j + 1, s1, dp1)
        middle_stage(j, s0, dp0, p0, ds0)

        def step(tt, carry):
            i = j + 2 * tt + 1
            first_stage(i + 1, s0, dp0)
            last_stage(i - 1, p0, ds0)
            middle_stage(i, s1, dp1, p1, ds1)
            first_stage(i + 2, s1, dp1)
            last_stage(i, p1, ds1)
            middle_stage(i + 1, s0, dp0, p0, ds0)
            return carry

        lax.fori_loop(0, (n - 1) // 2, step, 0)

        @pl.when(n % 2 == 0)
        def _():
            last_stage(nq - 2, p0, ds0)
            middle_stage(nq - 1, s1, dp1, p1, ds1)
            last_stage(nq - 1, p1, ds1)

        @pl.when(n % 2 == 1)
        def _():
            last_stage(nq - 1, p0, ds0)

        dk = dk_s[...]
        dkv_ref[...] = jnp.where(lane < MLA_NOPE, dk, dv_s[...]).astype(dkv_ref.dtype)
        dkr_ref[rows(j), :] += jnp.where(lane >= MLA_NOPE, dk, 0.0)

        @pl.when(j == nq - 1)
        def _():
            dq_out[...] = _rope_t(dq_ref[...], cos_ref[...], sin_ref[...]).astype(dq_out.dtype)

    seq_h = pl.BlockSpec((t, LANES), lambda h, j: (0, h))
    seq = pl.BlockSpec((t, LANES), lambda h, j: (0, 0))
    blk_h = pl.BlockSpec((blk, LANES), lambda h, j: (j, h))
    return _carry_call(
        body, name, (MLA_HEADS, nq), [seq_h, seq_h, seq_h, blk_h, blk_h, _full_spec(bias.shape, 2), seq, seq],
        [seq_h, blk_h, seq],
        [jax.ShapeDtypeStruct((t, MLA_HEADS * LANES), BF16), jax.ShapeDtypeStruct((t, MLA_HEADS * LANES), BF16),
         jax.ShapeDtypeStruct((t, LANES), F32)],
        [pltpu.VMEM((blk, blk), F32)] * 4 + [pltpu.VMEM((blk, blk), BF16)] * 4 + [pltpu.VMEM((blk, LANES), F32)] * 2
        + [pltpu.VMEM((t, LANES), F32)], (qr, o, do, km, vb, bias, cos, sin), carried)


def _rms_rows(row0, x, g):
    return (_rms(x, g),)


def _ssdmla_fwd(h, p, l, e, cos, sin, carried=None):
    hn = _prenorm(h, p["mix_pre_g"][l], "sm_prenorm")
    proj = _mm(hn, p["w_in"][e], "nn", "sm_in")
    xc = _conv_fwd(proj, p["ssd_conv_w"][e], p["ssd_conv_b"][e], "ssd_conv", cw=SSD_GW, c0=PROJ_XBC // SSD_GW)
    xact = _rowwise("ssd_act", _ssd_act, [xc], [], [(SSD_CONV_CH, F32)])[0]
    dt, da = _rowwise("ssd_dt", _ssd_dt, [(proj, LANES, PROJ_DT // LANES)], [p["ssd_dt_bias"][e], p["ssd_a_log"][e]],
                      [(LANES, F32)] * 2)
    y, hsave = _ssd_scan(xact, dt, da, "ssd_scan")
    y_ssd = _rowwise("ssd_post", _ssd_post, [y, (xact, SSD_D_INNER, 0), (proj, SSD_D_INNER, 0)],
                     [p["ssd_d"][e], p["ssd_norm_g"][e]], [(SSD_D_INNER, BF16)])[0]
    cqn = _prenorm((proj, MLA_Q_RANK, PROJ_CQ // MLA_Q_RANK), p["mla_q_norm_g"][e], "mla_qnorm")
    ckvn = _prenorm((proj, MLA_KV_RANK, PROJ_CKV // MLA_KV_RANK), p["mla_kv_norm_g"][e], "mla_kvnorm")
    kr = _rowwise("mla_krope", lambda r0, x, c, s: (_rope(x, c, s),), [(proj, LANES, PROJ_KR // LANES), cos, sin], [],
                  [(LANES, F32)])[0]
    qr = _mm(cqn, p["mla_w_q_up"][e], "nn", "mla_q_up", slots=[cos, sin], post=lambda v, c, s: (_rope(v, c, s),), out_dtypes=[BF16])
    km, vb = _mm(ckvn, p["mla_w_kv_up"][e], "nn", "mla_kv_up", slots=[kr], post=lambda v, k: _key_slots(0, v, k),
                 out_dtypes=[BF16, BF16])
    o, carried_out = _attn_fwd(qr, km, vb, "mla_attn", carried)
    m1 = _mm(y_ssd, p["w_out_ssd"][e], "nn", "sm_out_ssd")
    m, h2 = _mm(o, p["w_out_att"][e], "nn", "sm_out_att", extra=[m1, h], vecs=[p["mix_post_g"][l]],
                post=lambda v, m1b, hb, g: _post_residual(v + m1b, hb, g), out_dtypes=[F32, F32])
    return h2, (h, hn, proj, xc, xact, dt, da, y, hsave, y_ssd, cqn, ckvn, qr, km, vb, o, m), carried_out


def _ssdmla_bwd(dh, saved, p, l, e, cos, sin, grads, carried=None):
    h, hn, proj, xc, xact, dt, da, y, hsave, y_ssd, cqn, ckvn, qr, km, vb, o, m = saved
    dm, grads["mix_post_g"][l] = _postnorm_bwd(m, p["mix_post_g"][l], dh, "sm_postnorm_bwd")
    grads["w_out_ssd"][e] = _mm(y_ssd, dm, "tn", "sm_out_ssd_dw")
    grads["w_out_att"][e] = _mm(o, dm, "tn", "sm_out_att_dw")
    dy_ssd = _mm(dm, p["w_out_ssd"][e], "nt", "sm_out_ssd_dx")
    do = _mm(dm, p["w_out_att"][e], "nt", "sm_out_att_dx")
    (dq, dkv, dkr), carried_out = _attn_bwd(qr, km, vb, o, do, cos, sin, "mla_attn_bwd", carried)
    dkr_raw = _rowwise("mla_krope_bwd", lambda r0, d, c, s: (_rope_t(d, c, s),), [dkr, cos, sin], [], [(LANES, F32)])[0]
    grads["mla_w_q_up"][e] = _mm(cqn, dq, "tn", "mla_q_up_dw")
    dcqn = _mm(dq, p["mla_w_q_up"][e], "nt", "mla_q_up_dx")
    (dcq,), (grads["mla_q_norm_g"][e],) = _rowwise_vjp(
        "mla_qnorm_bwd", _rms_rows, [(proj, MLA_Q_RANK, PROJ_CQ // MLA_Q_RANK)], [p["mla_q_norm_g"][e]], [dcqn])
    grads["mla_w_kv_up"][e] = _mm(ckvn, dkv, "tn", "mla_kv_up_dw")
    dckvn = _mm(dkv, p["mla_w_kv_up"][e], "nt", "mla_kv_up_dx")
    (dckv,), (grads["mla_kv_norm_g"][e],) = _rowwise_vjp(
        "mla_kvnorm_bwd", _rms_rows, [(proj, MLA_KV_RANK, PROJ_CKV // MLA_KV_RANK)], [p["mla_kv_norm_g"][e]], [dckvn])
    (dy, dxskip, dz), (grads["ssd_d"][e], grads["ssd_norm_g"][e]) = _rowwise_vjp(
        "ssd_post_bwd", _ssd_post, [y, (xact, SSD_D_INNER, 0), (proj, SSD_D_INNER, 0)], [p["ssd_d"][e], p["ssd_norm_g"][e]], [dy_ssd])
    dxs, db, dc, ddt, dda = _ssd_scan_bwd(xact, dt, da, hsave, dy, dxskip, "ssd_scan_bwd")
    dxact = jnp.concatenate([dxs, db, dc], axis=1)
    (dxc,), _ = _rowwise_vjp("ssd_act_bwd", _ssd_act, [xc], [], [dxact])
    dxbc, grads["ssd_conv_w"][e], grads["ssd_conv_b"][e] = _conv_bwd(
        proj, p["ssd_conv_w"][e], dxc, "ssd_conv_bwd", cw=SSD_GW, c0=PROJ_XBC // SSD_GW)
    (ddtraw,), (grads["ssd_dt_bias"][e], grads["ssd_a_log"][e]) = _rowwise_vjp(
        "ssd_dt_bwd", _ssd_dt, [(proj, LANES, PROJ_DT // LANES)], [p["ssd_dt_bias"][e], p["ssd_a_log"][e]], [ddt, dda])
    dproj = jnp.concatenate([dz, dxbc, ddtraw, dcq, dckv, dkr_raw], axis=1).astype(BF16)
    grads["w_in"][e] = _mm(hn, dproj, "tn", "sm_in_dw")
    dhn = _mm(dproj, p["w_in"][e], "nt", "sm_in_dx")
    dh, grads["mix_pre_g"][l] = _prenorm_bwd_add(h, p["mix_pre_g"][l], [dhn], dh, "sm_prenorm_bwd")
    return dh, carried_out


GAINS = ("mix_pre_g", "mix_post_g", "mlp_pre_g", "mlp_post_g", "ssd_norm_g", "mla_q_norm_g", "mla_kv_norm_g", "ssd_conv_b", "rg_conv_b")
HEAD_VECS = ("ssd_dt_bias", "ssd_a_log", "ssd_d")
LRU_VECS = ("rg_b_a", "rg_b_i", "rg_lambda")
IN_DT_END = SSD_D_INNER + SSD_CONV_CH + SSD_HEADS
IN_KR = IN_DT_END + MLA_Q_RANK + MLA_KV_RANK


def _each(a, f):
    layers = a if isinstance(a, list) else [a[i] for i in range(a.shape[0])]
    return [None if x is None else f(x) for x in layers]


def _layout_params(w):
    p = {k: _each(w[k], lambda a: a[None, :]) for k in GAINS}
    for k in HEAD_VECS:
        p[k] = _each(w[k], lambda a: jnp.pad(a, (0, LANES - SSD_HEADS))[None, :])
    for k in LRU_VECS:
        p[k] = _each(w[k], lambda a: a.reshape(LRU_BLOCKS, 1, LRU_BLOCK))
    for k in ("w_up", "w_down", "mla_w_kv_up", "rg_w_x", "rg_w_y", "rg_w_out"):
        p[k] = _each(w[k], lambda a: a if isinstance(a, Gathered) else a.astype(BF16))
    for k in ("ssd_conv_w", "rg_conv_w", "rg_w_a", "rg_w_i"):
        p[k] = _each(w[k], lambda a: a)

    def w_in(a):
        def zcols(n):
            return jnp.zeros((a.shape[0], n), a.dtype)

        return jnp.concatenate([a[:, :IN_DT_END], zcols(PROJ_CQ - IN_DT_END), a[:, IN_DT_END:IN_KR], zcols(ROPE_LO),
                                a[:, IN_KR:], zcols(LANES - ROPE_HI)], axis=1).astype(BF16)

    def q_up(a):
        a = a.reshape(MLA_Q_RANK, MLA_HEADS, MLA_NOPE + MLA_ROPE)
        return jnp.pad(a, ((0, 0), (0, 0), (0, LANES - MLA_NOPE - MLA_ROPE))).reshape(MLA_Q_RANK, MLA_HEADS * LANES).astype(BF16)

    def out_att(a):
        a = a[SSD_D_INNER:].reshape(MLA_HEADS, MLA_V, D_MODEL)
        return jnp.pad(a, ((0, 0), (LANES - MLA_V, 0), (0, 0))).reshape(MLA_HEADS * LANES, D_MODEL).astype(BF16)

    p["w_in"] = _each(w["w_in"], w_in)
    p["mla_w_q_up"] = _each(w["mla_w_q_up"], q_up)
    p["w_out_ssd"] = _each(w["w_out_ab"], lambda a: a[:SSD_D_INNER].astype(BF16))
    p["w_out_att"] = _each(w["w_out_ab"], out_att)
    return p


def _rope_tables(t):
    pos = (jnp.arange(t) - PAD).astype(F32)
    inv = ROPE_BASE ** (-jnp.arange(0, MLA_ROPE, 2, dtype=F32) / MLA_ROPE)
    ang = pos[:, None] * inv[None, :]
    c, s = jnp.cos(ang), jnp.sin(ang)
    one, zero = jnp.ones((t, MLA_NOPE), F32), jnp.zeros((t, MLA_NOPE), F32)
    tail = LANES - ROPE_HI
    return (jnp.concatenate([one, c, c, one[:, :tail]], axis=1), jnp.concatenate([zero, -s, s, zero[:, :tail]], axis=1))


GRAD_KEYS = GAINS + HEAD_VECS + LRU_VECS + ("w_up", "w_down", "mla_w_kv_up", "rg_w_x", "rg_w_y", "rg_w_out", "ssd_conv_w",
                                            "rg_conv_w", "rg_w_a", "rg_w_i", "w_in", "mla_w_q_up", "w_out_ssd", "w_out_att")


def _device_step(x, meta, target, p, hooks=None):
    t = PAD + N_META + x.shape[0]
    cos, sin = _rope_tables(t)
    h = jnp.concatenate([jnp.zeros((PAD, D_MODEL), F32), meta, x], axis=0)
    n_even, n_odd = (DEPTH + 1) // 2, DEPTH // 2
    saved = []
    for l in range(DEPTH):
        if l % 2 == 0:
            carried = hooks.forward_exchange() if hooks and l == 0 else None
            h, sm, arrived = _ssdmla_fwd(h, p, l, l // 2, cos, sin, carried)
            if carried is not None:
                p = hooks.after_forward_exchange(arrived)
        else:
            h, sm = _rglru_fwd(h, p, l, l // 2)
        h, sp = _mlp_fwd(h, p, l)
        saved.append((sm, sp))
    sq, dh = _loss_and_grad(h, target, "loss")
    per_layer = {"mix_pre_g": DEPTH, "mix_post_g": DEPTH, "mlp_pre_g": DEPTH, "mlp_post_g": DEPTH, "w_up": DEPTH, "w_down": DEPTH}
    grads = {k: [None] * per_layer.get(k, n_odd if k.startswith("rg_") else n_even) for k in GRAD_KEYS}
    for l in reversed(range(DEPTH)):
        sm, sp = saved[l]
        dh = _mlp_bwd(dh, sp, p, l, grads)
        if l % 2 == 0:
            carried = hooks.backward_exchange(grads, l) if hooks else None
            dh, arrived = _ssdmla_bwd(dh, sm, p, l, l // 2, cos, sin, grads, carried)
            if carried is not None:
                hooks.after_backward_exchange(arrived, l)
        else:
            dh = _rglru_bwd(dh, sm, p, l, l // 2, grads)
    return sq, dh, grads


MESH = pl.DeviceIdType.MESH
ANY = pl.BlockSpec(memory_space=pl.ANY)


def _mesh_pos():
    return lax.axis_index("x"), lax.axis_index("y"), lax.axis_index("c")


def _other_chips(x, y):
    return [(1 - x, y), (x, 1 - y), (1 - x, 1 - y)]


def _remote(src, dst, send_sems, recv_sems, k, to):
    return pltpu.make_async_remote_copy(src_ref=src, dst_ref=dst, send_sem=send_sems.at[k], recv_sem=recv_sems.at[k],
                                        device_id=to, device_id_type=MESH)


class Exchange:
    def __init__(self, ins, outs, aliases, n_sems, plan):
        self.ins, self.outs, self.aliases, self.n_sems, self.plan = list(ins), list(outs), dict(aliases), n_sems, plan


def _sems(n):
    return [pltpu.SemaphoreType.DMA((n,)), pltpu.SemaphoreType.DMA((n,))]


def _run_exchange(name, ex):
    ni, no = len(ex.ins), len(ex.outs)

    def body(*refs):
        sends = ex.plan(refs[:ni], refs[ni:ni + no], refs[-2], refs[-1], False)
        for cp in sends:
            cp.start()
        for cp in ex.plan(refs[:ni], refs[ni:ni + no], refs[-2], refs[-1], True):
            cp.wait_recv()
        for cp in sends:
            cp.wait_send()

    return pl.pallas_call(body, name=name, in_specs=[ANY] * ni, out_specs=[ANY] * no, out_shape=ex.outs,
                          input_output_aliases=ex.aliases, scratch_shapes=_sems(ex.n_sems))(*ex.ins)


def _carry_call(body, name, grid, in_specs, out_specs, out_shape, scratch_shapes, args, ex):
    if ex is None:
        res = pl.pallas_call(body, name=name, grid=grid, in_specs=in_specs, out_specs=out_specs, out_shape=out_shape,
                             scratch_shapes=scratch_shapes, compiler_params=_params(("arbitrary",) * len(grid)))(*args)
        return res, None
    ni, no, ns, xi, xo = len(in_specs), len(out_specs), len(scratch_shapes), len(ex.ins), len(ex.outs)

    def wrapped(*refs):
        ins, xin = refs[:ni], refs[ni:ni + xi]
        outs, xout = refs[ni + xi:ni + xi + no], refs[ni + xi + no:ni + xi + no + xo]
        scr, send_sems, recv_sems = refs[ni + xi + no + xo:-2], refs[-2], refs[-1]
        pid = [pl.program_id(d) for d in range(len(grid))]
        first = functools.reduce(jnp.logical_and, [p == 0 for p in pid])
        last = functools.reduce(jnp.logical_and, [p == g - 1 for p, g in zip(pid, grid)])

        @pl.when(first)
        def _():
            for cp in ex.plan(xin, xout, send_sems, recv_sems, False):
                cp.start()

        body(*ins, *outs, *scr)

        @pl.when(last)
        def _():
            for cp in ex.plan(xin, xout, send_sems, recv_sems, True):
                cp.wait_recv()
            for cp in ex.plan(xin, xout, send_sems, recv_sems, False):
                cp.wait_send()

    res = pl.pallas_call(
        wrapped, name=name, grid=grid, in_specs=list(in_specs) + [ANY] * xi, out_specs=list(out_specs) + [ANY] * xo,
        out_shape=list(out_shape) + ex.outs, scratch_shapes=list(scratch_shapes) + _sems(ex.n_sems),
        input_output_aliases={ni + i: no + o for i, o in ex.aliases.items()},
        compiler_params=_params(("arbitrary",) * len(grid)))(*args, *ex.ins)
    return res[:no], res[no:]


def _gather_ici(srcs, bufs, ranges):
    n = len(srcs)

    def plan(in_refs, out_refs, ss, rs, arrivals):
        x, y, c = _mesh_pos()
        cps = []
        for t, (l0, nl) in enumerate(ranges):
            if nl:
                s, o, lr = in_refs[t], out_refs[t], pl.ds(l0, nl)
                for j, (cx, cy) in enumerate(_other_chips(x, y)):
                    chip = 2 * cx + cy if arrivals else 2 * x + y
                    cps.append(_remote(s.at[lr, c], o.at[chip, lr, c], ss, rs, (N_CHIPS - 1) * t + j, (cx, cy, c)))
        return cps

    outs = [jax.ShapeDtypeStruct((N_CHIPS,) + s.shape, s.dtype) for s in srcs]
    if bufs is None:
        return Exchange(srcs, outs, {}, (N_CHIPS - 1) * n, plan)
    return Exchange(list(srcs) + list(bufs), outs, {n + t: t for t in range(n)}, (N_CHIPS - 1) * n, plan)


def _gather_d2d(srcs, bufs, ranges):
    n = len(srcs)

    def plan(in_refs, out_refs, ss, rs, arrivals):
        x, y, c = _mesh_pos()
        sib, me = (x, y, 1 - c), 2 * x + y
        cps = []
        for t, (l0, nl) in enumerate(ranges):
            if nl:
                s, o, lr = in_refs[t], out_refs[t], pl.ds(l0, nl)
                for j, (cx, cy) in enumerate(_other_chips(x, y)):
                    slot = o.at[2 * cx + cy, lr, c]
                    cps.append(_remote(slot, o.at[2 * cx + cy, lr, 1 - c] if arrivals else slot, ss, rs, N_CHIPS * t + j, sib))
                cps.append(_remote(s.at[lr], o.at[me, lr], ss, rs, N_CHIPS * t + N_CHIPS - 1, sib))
        return cps

    outs = [jax.ShapeDtypeStruct(b.shape, b.dtype) for b in bufs]
    return Exchange(list(srcs) + list(bufs), outs, {n + t: t for t in range(n)}, N_CHIPS * n, plan)


def _gather_chips(srcs, name):
    ranges = [(0, s.shape[0]) for s in srcs]
    bufs = _run_exchange(name + "_ici", _gather_ici(srcs, None, ranges))
    return _run_exchange(name + "_d2d", _gather_d2d(srcs, bufs, ranges))


def _pair_exchange(gs):
    def plan(in_refs, out_refs, ss, rs, arrivals):
        x, y, c = _mesh_pos()
        return [_remote(g.at[pl.ds(0, N_CHIPS), 1 - c], o, ss, rs, t, (x, y, 1 - c)) for t, (g, o) in enumerate(zip(in_refs, out_refs))]

    return Exchange(gs, [jax.ShapeDtypeStruct((g.shape[0],) + g.shape[2:], g.dtype) for g in gs], {}, len(gs), plan)


def _chip_exchange(ps, slots, qs, q_shapes):
    n = len(ps)
    kept = [g for g, q in enumerate(qs) if q is not None]

    def plan(in_refs, out_refs, ss, rs, arrivals):
        x, y, c = _mesh_pos()
        return [_remote(in_refs[t].at[2 * cx + cy], out_refs[g].at[j, li], ss, rs, (N_CHIPS - 1) * t + j, (cx, cy, c))
                for t, (g, li) in enumerate(slots) for j, (cx, cy) in enumerate(_other_chips(x, y))]

    return Exchange(list(ps) + [qs[g] for g in kept], q_shapes, {n + i: g for i, g in enumerate(kept)}, (N_CHIPS - 1) * n, plan)


def _pair_share(fs):
    def plan(in_refs, out_refs, ss, rs, arrivals):
        x, y, c = _mesh_pos()
        return [_remote(o.at[pl.ds(0, o.shape[0]), c], o.at[pl.ds(0, o.shape[0]), 1 - c if arrivals else c], ss, rs, t, (x, y, 1 - c))
                for t, o in enumerate(out_refs)]

    return Exchange(fs, [jax.ShapeDtypeStruct(f.shape, f.dtype) for f in fs], {t: t for t in range(len(fs))}, len(fs), plan)


SUM_BLOCK = 512 * 1024


def _sum_pair(g, ra, c, name):
    n, _, h, w = g.shape
    tr = _tile(h, max(16, SUM_BLOCK // w), 16)

    def body(c_ref, g_ref, r_ref, o_ref):
        o_ref[...] = (g_ref[0] + r_ref[...]).astype(o_ref.dtype)

    return pl.pallas_call(
        body, name=name,
        grid_spec=pltpu.PrefetchScalarGridSpec(
            num_scalar_prefetch=1, grid=(n, h // tr),
            in_specs=[pl.BlockSpec((1, 1, tr, w), lambda s, i, cr: (s, cr[0], i, 0)), pl.BlockSpec((1, tr, w), lambda s, i, cr: (s, i, 0))],
            out_specs=pl.BlockSpec((1, tr, w), lambda s, i, cr: (s, i, 0))),
        out_shape=jax.ShapeDtypeStruct((n, h, w), BF16),
        compiler_params=_params(("parallel", "parallel")),
    )(c.reshape(1).astype(jnp.int32), g, ra)


def _sum_chips(ps, q, pos, name):
    nc, nl, h, w = q.shape
    tr = _tile(h, max(16, SUM_BLOCK // (w * nl)), 16)

    def body(x_ref, y_ref, c_ref, *refs):
        q_ref, o_ref = refs[nl], refs[nl + 1]
        for l in range(nl):
            acc = refs[l][0].astype(F32)
            for j in range(nc):
                acc = acc + q_ref[j, l].astype(F32)
            o_ref[l] = acc

    return pl.pallas_call(
        body, name=name,
        grid_spec=pltpu.PrefetchScalarGridSpec(
            num_scalar_prefetch=3, grid=(h // tr,),
            in_specs=[pl.BlockSpec((1, tr, w), lambda i, x, y, c: (2 * x[0] + y[0], i, 0))] * nl
            + [pl.BlockSpec((nc, nl, tr, w), lambda i, x, y, c: (0, 0, i, 0))],
            out_specs=pl.BlockSpec((nl, None, tr, w), lambda i, x, y, c: (0, c[0], i, 0))),
        out_shape=jax.ShapeDtypeStruct((nl, 2, h, w), F32),
        compiler_params=_params(("parallel",)),
    )(*pos, *ps, q)


def _adamw(g, w, m, v, name):
    def f(r0, gg, ww, mm, vv):
        m2 = ADAM_B1 * mm + (1.0 - ADAM_B1) * gg
        v2 = ADAM_B2 * vv + (1.0 - ADAM_B2) * jnp.square(gg)
        m_hat = m2 / (1.0 - ADAM_B1 ** ADAM_STEP)
        v_hat = v2 / (1.0 - ADAM_B2 ** ADAM_STEP)
        return gg, -ADAM_LR * (m_hat / (jnp.sqrt(v_hat) + ADAM_EPS) + ADAM_WD * ww), m2, v2

    return _rowwise(name, f, [g, w, m, v], [], [(g.shape[1], F32)] * 4, tr=_tile(g.shape[0], 512))


WEIGHTS = (
    ("meta_tokens", (N_META, D_MODEL), 1), ("mix_pre_g", (DEPTH, D_MODEL), None), ("mix_post_g", (DEPTH, D_MODEL), None),
    ("mlp_pre_g", (DEPTH, D_MODEL), None), ("mlp_post_g", (DEPTH, D_MODEL), None), ("w_up", (DEPTH, D_MODEL, D_FF), 2),
    ("w_down", (DEPTH, D_FF, D_MODEL), 1), ("w_in", (2, D_MODEL, 3248), 2), ("ssd_conv_w", (2, CONV_K, SSD_CONV_CH), 2),
    ("ssd_conv_b", (2, SSD_CONV_CH), None), ("ssd_dt_bias", (2, SSD_HEADS), None), ("ssd_a_log", (2, SSD_HEADS), None),
    ("ssd_d", (2, SSD_HEADS), None), ("ssd_norm_g", (2, SSD_D_INNER), None), ("mla_q_norm_g", (2, MLA_Q_RANK), None),
    ("mla_w_q_up", (2, MLA_Q_RANK, MLA_HEADS * (MLA_NOPE + MLA_ROPE)), 2), ("mla_kv_norm_g", (2, MLA_KV_RANK), None),
    ("mla_w_kv_up", (2, MLA_KV_RANK, MLA_HEADS * (MLA_NOPE + MLA_V)), 2), ("w_out_ab", (2, SSD_D_INNER + MLA_HEADS * MLA_V, D_MODEL), 1),
    ("rg_w_x", (2, D_MODEL, LRU_WIDTH), 2), ("rg_w_y", (2, D_MODEL, LRU_WIDTH), 2), ("rg_conv_w", (2, CONV_K, LRU_WIDTH), 2),
    ("rg_conv_b", (2, LRU_WIDTH), 1), ("rg_w_a", (2, LRU_BLOCKS, LRU_BLOCK, LRU_BLOCK), None), ("rg_b_a", (2, LRU_WIDTH), 1),
    ("rg_w_i", (2, LRU_BLOCKS, LRU_BLOCK, LRU_BLOCK), None), ("rg_b_i", (2, LRU_WIDTH), 1), ("rg_lambda", (2, LRU_WIDTH), 1),
    ("rg_w_out", (2, LRU_WIDTH, D_MODEL), 1),
)
BIG = {"w_up": "col", "w_down": "row", "w_in": "col", "mla_w_q_up": "col", "mla_w_kv_up": "col", "w_out_ab": "row",
       "rg_w_x": "col", "rg_w_y": "col", "rg_w_out": "row"}
DIRECT = ("w_up", "w_down")
FLAT_QUANTUM = 2 * 16 * LANES
TABLE = {name: (shape, d) for name, shape, d in WEIGHTS}
SMALL_SHARDED = tuple(name for name, _, d in WEIGHTS if d is not None and name not in BIG)
REPLICATED = tuple(name for name, _, d in WEIGHTS if d is None)


def _chips_to_full(a, kind):
    if kind == "col":
        return jnp.moveaxis(a, 0, 2).reshape(a.shape[1], a.shape[2], -1)
    return jnp.moveaxis(a, 0, 1).reshape(a.shape[1], -1, a.shape[3])


def _full_to_chips(g, kind):
    if kind == "col":
        return jnp.moveaxis(g.reshape(g.shape[0], N_CHIPS, -1), 1, 0)
    return g.reshape(N_CHIPS, -1, g.shape[1])


def _shard_shape(shape, d):
    return shape[:d] + (shape[d] // N_CHIPS,) + shape[d + 1:]


def _shard_major(full, d):
    s = full.shape
    return jnp.moveaxis(full.reshape(s[:d] + (N_CHIPS, s[d] // N_CHIPS) + s[d + 1:]), d, 0).reshape(N_CHIPS, -1)


def _from_shard_major(a, shape, d):
    ss = _shard_shape(shape, d)
    return jnp.moveaxis(a.reshape((N_CHIPS,) + ss), 0, d).reshape(shape)


def _pad_cols(a, quantum):
    n = a.shape[-1]
    return jnp.pad(a, [(0, 0)] * (a.ndim - 1) + [(0, -n % quantum)])


def _big_pieces(g):
    def w_in(a):
        return jnp.concatenate([a[:, :IN_DT_END], a[:, PROJ_CQ:PROJ_KR], a[:, PROJ_KR + ROPE_LO:PROJ_KR + ROPE_HI]], axis=1)

    def q_up(a):
        return a.reshape(MLA_Q_RANK, MLA_HEADS, LANES)[:, :, :MLA_NOPE + MLA_ROPE].reshape(MLA_Q_RANK, -1)

    def out_ab(sa):
        s, a = sa
        return jnp.concatenate([s, a.reshape(MLA_HEADS, LANES, D_MODEL)[:, LANES - MLA_V:, :].reshape(-1, D_MODEL)], axis=0)

    ident = lambda a: a
    full = {"w_down": _each(g["w_down"], ident), "w_in": _each(g["w_in"], w_in), "mla_w_q_up": _each(g["mla_w_q_up"], q_up),
            "mla_w_kv_up": _each(g["mla_w_kv_up"], ident),
            "w_out_ab": _each([None if s is None or a is None else (s, a) for s, a in zip(g["w_out_ssd"], g["w_out_att"])], out_ab),
            "rg_w_x": _each(g["rg_w_x"], ident), "rg_w_y": _each(g["rg_w_y"], ident), "rg_w_out": _each(g["rg_w_out"], ident)}
    return {name: (list(g[name]) if name == "w_up" else _each(full[name], lambda a, k=BIG[name]: _full_to_chips(a, k))) for name in BIG}


def _small_grads(g, dh):
    out = {k: jnp.stack(g[k])[:, 0, :] for k in GAINS}
    for k in HEAD_VECS:
        out[k] = jnp.stack(g[k])[:, 0, :SSD_HEADS]
    for k in LRU_VECS:
        out[k] = jnp.stack(g[k]).reshape(-1, LRU_WIDTH)
    for k in ("ssd_conv_w", "rg_conv_w", "rg_w_a", "rg_w_i"):
        out[k] = jnp.stack(g[k])
    out["meta_tokens"] = dh[PAD:PAD + N_META]
    return out


class StepExchanges:
    def __init__(self, w):
        self.w = w
        self.c = lax.axis_index("c")
        self.riding, self.ras = {}, {}
        small = _pad_cols(jnp.concatenate([w[n].reshape(-1) for n in SMALL_SHARDED]), FLAT_QUANTUM).reshape(1, 2, -1, LANES)
        self.srcs = [self._halves(w[n].astype(BF16)) for n in BIG] + [small]
        first = {n: (0, 1 if n in ("w_in", "mla_w_q_up", "mla_w_kv_up", "w_out_ab") else 0) for n in BIG}
        self.first = [first[n] for n in BIG] + [(0, 1)]
        self.rest = [(nl, TABLE[n][0][0] - nl) for n, (_, nl) in zip(BIG, self.first)] + [(0, 0)]
        bufs = _run_exchange("gather_first_ici", _gather_ici(self.srcs, None, self.first))
        self.bufs = _run_exchange("gather_first_d2d", _gather_d2d(self.srcs, bufs, self.first))

    @staticmethod
    def _halves(a):
        return a.reshape(a.shape[0], 2, a.shape[1] // 2, a.shape[2])

    def params(self, ranges):
        w = self.w
        full = {n: w[n] for n in REPLICATED}
        for name, buf, (l0, nl) in zip(BIG, self.bufs, ranges):
            a = buf.reshape(buf.shape[:2] + (-1, buf.shape[4]))
            have = range(l0, l0 + nl)
            if name in DIRECT:
                full[name] = [Gathered(a, BIG[name], l) if l in have else None for l in range(a.shape[1])]
            else:
                full[name] = [_chips_to_full(a[:, l:l + 1], BIG[name])[0] if l in have else None for l in range(a.shape[1])]
        got, off = self.bufs[-1].reshape(N_CHIPS, -1), 0
        for name in SMALL_SHARDED:
            shape, d = TABLE[name]
            n = int(np.prod(_shard_shape(shape, d)))
            full[name] = _from_shard_major(got[:, off:off + n], shape, d)
            off += n
        self.meta = full.pop("meta_tokens")
        return _layout_params(full)

    def forward_exchange(self):
        return _gather_ici(self.srcs, self.bufs, self.rest)

    def after_forward_exchange(self, arrived):
        self.bufs = _run_exchange("gather_rest_d2d", _gather_d2d(self.srcs, arrived, self.rest))
        return self.params([(0, TABLE[n][0][0]) for n in BIG])

    def _pair_sums(self, pieces, tag):
        keys = list(pieces)
        ras = _run_exchange("grads_pair_exchange_" + tag, _pair_exchange([pieces[k] for k in keys]))
        return {k: _sum_pair(pieces[k], ra, self.c, "grads_pair_sum") for k, ra in zip(keys, ras)}

    def _q_shapes(self):
        return [jax.ShapeDtypeStruct((N_CHIPS - 1, s.shape[0]) + s.shape[2:], BF16) for s in self.srcs[:-1]]

    def backward_exchange(self, grads, layer):
        big = _big_pieces(grads)
        pieces = {(g, l): pc.reshape(N_CHIPS, 2, pc.shape[1] // 2, pc.shape[2]) for g, name in enumerate(BIG)
                  for l, pc in enumerate(big[name]) if pc is not None and (g, l) not in self.riding}
        if layer > 0:
            self.riding = pieces
            return _pair_exchange(list(pieces.values()))
        self.ps = {k: _sum_pair(self.riding[k], ra, self.c, "grads_pair_sum") for k, ra in self.ras.items()}
        self.ps.update(self._pair_sums(pieces, "early"))
        self.early = list(self.ps)
        return _chip_exchange([self.ps[k] for k in self.early], self.early, [None] * len(BIG), self._q_shapes())

    def after_backward_exchange(self, arrived, layer):
        if layer > 0:
            self.ras = dict(zip(self.riding, arrived))
        else:
            self.qs = list(arrived)

    def finish(self, grads, dh):
        big, small = _big_pieces(grads), _small_grads(grads, dh)
        pieces = {(g, l): pc.reshape(N_CHIPS, 2, pc.shape[1] // 2, pc.shape[2])
                  for g, name in enumerate(BIG) for l, pc in enumerate(big[name]) if (g, l) not in self.ps}
        sharded = jnp.concatenate([_shard_major(small[n], TABLE[n][1]) for n in SMALL_SHARDED], axis=1)
        rep = _pad_cols(jnp.concatenate([small[n].reshape(-1) for n in REPLICATED]), N_CHIPS * FLAT_QUANTUM)
        n_sh, n_rep = sharded.shape[1], rep.shape[0] // N_CHIPS
        flat = _pad_cols(jnp.concatenate([sharded, rep.reshape(N_CHIPS, n_rep)], axis=1), FLAT_QUANTUM)
        pieces[(len(BIG), 0)] = flat.reshape(N_CHIPS, 2, -1, LANES)
        late = self._pair_sums(pieces, "late")
        self.ps.update(late)
        keys = list(late)
        small_q = jax.ShapeDtypeStruct((N_CHIPS - 1, 1) + late[(len(BIG), 0)].shape[1:], BF16)
        qs = _run_exchange("grads_chip_exchange_late",
                           _chip_exchange([late[k] for k in keys], keys, self.qs + [None], self._q_shapes() + [small_q]))
        pos = [lax.axis_index(a).reshape(1).astype(jnp.int32) for a in ("x", "y", "c")]
        sums = [_sum_chips([self.ps[(g, l)] for l in range(q.shape[1])], q, pos, "grads_chip_sum") for g, q in enumerate(qs)]
        outs = _run_exchange("grads_pair_share", _pair_share(sums))
        out = {name: o.reshape(o.shape[0], -1, o.shape[3]) for name, o in zip(BIG, outs)}
        f = outs[-1].reshape(-1)
        rep_all = _gather_chips([f[n_sh:n_sh + n_rep].reshape(1, 2, -1, LANES)], "grads_gather_replicated")[0].reshape(-1)
        off = 0
        for name in SMALL_SHARDED:
            ss = _shard_shape(*TABLE[name])
            n = int(np.prod(ss))
            out[name] = f[off:off + n].reshape(ss)
            off += n
        off = 0
        for name in REPLICATED:
            shape = TABLE[name][0]
            n = int(np.prod(shape))
            out[name] = rep_all[off:off + n].reshape(shape)
            off += n
        return out


def kernel(x, meta_tokens, mix_pre_g, mix_post_g, mlp_pre_g, mlp_post_g, w_up, w_down, w_in, ssd_conv_w, ssd_conv_b, ssd_dt_bias, ssd_a_log, ssd_d, ssd_norm_g, mla_q_norm_g, mla_w_q_up, mla_kv_norm_g, mla_w_kv_up, w_out_ab, rg_w_x, rg_w_y, rg_conv_w, rg_conv_b, rg_w_a, rg_b_a, rg_w_i, rg_b_i, rg_lambda, rg_w_out, loss_target, m_meta_tokens, m_mix_pre_g, m_mix_post_g, m_mlp_pre_g, m_mlp_post_g, m_w_up, m_w_down, m_w_in, m_ssd_conv_w, m_ssd_conv_b, m_ssd_dt_bias, m_ssd_a_log, m_ssd_d, m_ssd_norm_g, m_mla_q_norm_g, m_mla_w_q_up, m_mla_kv_norm_g, m_mla_w_kv_up, m_w_out_ab, m_rg_w_x, m_rg_w_y, m_rg_conv_w, m_rg_conv_b, m_rg_w_a, m_rg_b_a, m_rg_w_i, m_rg_b_i, m_rg_lambda, m_rg_w_out, v_meta_tokens, v_mix_pre_g, v_mix_post_g, v_mlp_pre_g, v_mlp_post_g, v_w_up, v_w_down, v_w_in, v_ssd_conv_w, v_ssd_conv_b, v_ssd_dt_bias, v_ssd_a_log, v_ssd_d, v_ssd_norm_g, v_mla_q_norm_g, v_mla_w_q_up, v_mla_kv_norm_g, v_mla_w_kv_up, v_w_out_ab, v_rg_w_x, v_rg_w_y, v_rg_conv_w, v_rg_conv_b, v_rg_w_a, v_rg_b_a, v_rg_w_i, v_rg_b_i, v_rg_lambda, v_rg_w_out):
    names = [n for n, _, _ in WEIGHTS]
    w = dict(zip(names, (meta_tokens, mix_pre_g, mix_post_g, mlp_pre_g, mlp_post_g, w_up, w_down, w_in, ssd_conv_w, ssd_conv_b, ssd_dt_bias, ssd_a_log, ssd_d, ssd_norm_g, mla_q_norm_g, mla_w_q_up, mla_kv_norm_g, mla_w_kv_up, w_out_ab, rg_w_x, rg_w_y, rg_conv_w, rg_conv_b, rg_w_a, rg_b_a, rg_w_i, rg_b_i, rg_lambda, rg_w_out)))
    m = dict(zip(names, (m_meta_tokens, m_mix_pre_g, m_mix_post_g, m_mlp_pre_g, m_mlp_post_g, m_w_up, m_w_down, m_w_in, m_ssd_conv_w, m_ssd_conv_b, m_ssd_dt_bias, m_ssd_a_log, m_ssd_d, m_ssd_norm_g, m_mla_q_norm_g, m_mla_w_q_up, m_mla_kv_norm_g, m_mla_w_kv_up, m_w_out_ab, m_rg_w_x, m_rg_w_y, m_rg_conv_w, m_rg_conv_b, m_rg_w_a, m_rg_b_a, m_rg_w_i, m_rg_b_i, m_rg_lambda, m_rg_w_out)))
    v = dict(zip(names, (v_meta_tokens, v_mix_pre_g, v_mix_post_g, v_mlp_pre_g, v_mlp_post_g, v_w_up, v_w_down, v_w_in, v_ssd_conv_w, v_ssd_conv_b, v_ssd_dt_bias, v_ssd_a_log, v_ssd_d, v_ssd_norm_g, v_mla_q_norm_g, v_mla_w_q_up, v_mla_kv_norm_g, v_mla_w_kv_up, v_w_out_ab, v_rg_w_x, v_rg_w_y, v_rg_conv_w, v_rg_conv_b, v_rg_w_a, v_rg_b_a, v_rg_w_i, v_rg_b_i, v_rg_lambda, v_rg_w_out)))
    ex = StepExchanges(w)
    p = ex.params(ex.first)
    sq, dh, grads = _device_step(x[0], ex.meta, loss_target[0], p, hooks=ex)
    loss = lax.psum(0.5 * sq[0, 0] / D_MODEL, ("x", "y", "c"))
    g = ex.finish(grads, dh)
    grad, delta, new_m, new_v = {}, {}, {}, {}
    for name in names:
        shape = g[name].shape
        two_d = (int(np.prod(shape[:-1])), shape[-1])
        res = _adamw(g[name].reshape(two_d), w[name].reshape(two_d), m[name].reshape(two_d), v[name].reshape(two_d), "adamw")
        grad[name], delta[name], new_m[name], new_v[name] = (r.reshape(shape) for r in res)
    grad_x = dh[PAD + N_META:][None]
    return (loss, grad_x, *[grad[n] for n in names], *[delta[n] for n in names], *[new_m[n] for n in names], *[new_v[n] for n in names])
```

```python
import functools

import jax
import jax.numpy as jnp
import numpy as np
from jax import lax
from jax.experimental import pallas as pl
from jax.experimental.pallas import tpu as pltpu

F32 = jnp.float32
BF16 = jnp.bfloat16

D_MODEL = 1024
DEPTH = 4
N_META = 16
CHUNK = 128
PAD = CHUNK - N_META
EPS = 1e-6
SSD_HEADS = 16
SSD_HEAD_DIM = 64
SSD_D_INNER = SSD_HEADS * SSD_HEAD_DIM
SSD_GROUPS = 2
SSD_STATE = 128
SSD_CONV_CH = SSD_D_INNER + 2 * SSD_GROUPS * SSD_STATE
MLA_HEADS = 16
MLA_NOPE = 64
MLA_ROPE = 32
MLA_V = 64
MLA_Q_RANK = 384
MLA_KV_RANK = 256
ROPE_BASE = 10000.0
LRU_WIDTH = 1280
LRU_BLOCKS = 10
LRU_BLOCK = 128
LRU_C = 8.0
D_FF = 4 * D_MODEL
ADAM_LR, ADAM_B1, ADAM_B2, ADAM_EPS, ADAM_WD, ADAM_STEP = 0.001, 0.9, 0.999, 1e-08, 0.01, 10

LANES = 128
VMEM_LIMIT = 56 * 1024 * 1024
MM_VMEM_BUDGET = 40 * 1024 * 1024
PROJ_Z, PROJ_XBC, PROJ_DT, PROJ_CQ, PROJ_CKV, PROJ_KR = 0, 1024, 2560, 2688, 3072, 3328
PROJ_W = 3456


def _tile(n, cap, mult=8):
    for t in range(min(n, cap), 0, -1):
        if n % t == 0 and t % mult == 0:
            return t
    return n


def _params(sem):
    return pltpu.CompilerParams(dimension_semantics=sem, vmem_limit_bytes=VMEM_LIMIT)


def _full_spec(shape, ngrid):
    nd = len(shape)
    if ngrid == 1:
        return pl.BlockSpec(shape, lambda i: (0,) * nd)
    if ngrid == 2:
        return pl.BlockSpec(shape, lambda i, j: (0,) * nd)
    return pl.BlockSpec(shape, lambda i, j, k: (0,) * nd)


_DIMS = {"nn": (((1,), (0,)), ((), ())), "nt": (((1,), (1,)), ((), ())), "tn": (((0,), (0,)), ((), ()))}


class Gathered:
    def __init__(self, arr, kind, layer):
        self.arr, self.kind, self.layer = arr, kind, layer
        _, _, r, c = arr.shape
        self.shape = (r, N_CHIPS * c) if kind == "col" else (N_CHIPS * r, c)


N_CHIPS = 4


def _mm(a, b, mode, name, out_dtype=F32, add=None, out_chip_major=False, extra=(), vecs=(), slots=(), post=None, out_dtypes=None):
    if mode == "nn":
        (m, kc), (_, n) = a.shape, b.shape
    elif mode == "nt":
        (m, kc), (n, _) = a.shape, b.shape
    else:
        (kc, m), (_, n) = a.shape, b.shape
    n_tile = n // N_CHIPS if out_chip_major else n
    across = isinstance(b, Gathered) and (mode, b.kind) in (("nn", "row"), ("nt", "col"))
    if mode == "tn":
        tm, tk = _tile(m, 1024, LANES), kc
        fits = [c for c in (1280, 1152, 1024, 768, 640, 512) if n_tile % c == 0 and
                2 * kc * (tm * a.dtype.itemsize + c * b.dtype.itemsize) + 2 * tm * c * 4 <= MM_VMEM_BUDGET]
        tn = fits[0] if fits else _tile(n_tile, 1280, LANES)
        if not fits:
            tk = _tile(kc, 1408, LANES)
    else:
        tn = _tile(n_tile, 1280, LANES)
        tk = _tile(kc, 4096, LANES)
        tm = _tile(m, 1056 if tk <= 1024 else 528, 16)
    nk = kc // tk
    if mode == "tn":
        a_spec = pl.BlockSpec((tk, tm), lambda i, j, k: (k, i))
    else:
        a_spec = pl.BlockSpec((tm, tk), lambda i, j, k: (i, k))
    b_arrs = [b]
    if isinstance(b, Gathered):
        layer = b.layer
        sr, sc = b.arr.shape[2:]
        if across:
            assert nk == 1 and kc == N_CHIPS * (sr if b.kind == "row" else sc)
            b_arrs = [b.arr] * N_CHIPS
            if b.kind == "row":
                b_specs = [pl.BlockSpec((None, None, sr, tn), lambda i, j, k, s=s: (s, layer, 0, j)) for s in range(N_CHIPS)]
            else:
                b_specs = [pl.BlockSpec((None, None, tn, sc), lambda i, j, k, s=s: (s, layer, j, 0)) for s in range(N_CHIPS)]
        else:
            b_arrs = [b.arr]
            br, bc = (tk, tn) if mode == "nn" else (tn, tk)
            assert mode in ("nn", "nt") and sr % br == 0 and sc % bc == 0

            def b_map(i, j, k):
                r, c = (k, j) if mode == "nn" else (j, k)
                if b.kind == "col":
                    return ((c * bc) // sc, layer, r, ((c * bc) % sc) // bc)
                return ((r * br) // sr, layer, ((r * br) % sr) // br, c)

            b_specs = [pl.BlockSpec((None, None, br, bc), b_map)]
    elif mode == "nt":
        b_specs = [pl.BlockSpec((tn, tk), lambda i, j, k: (j, k))]
    else:
        b_specs = [pl.BlockSpec((tk, tn), lambda i, j, k: (k, j))]
    nb = len(b_arrs)
    dims = _DIMS[mode]
    if out_chip_major:
        ns = n // N_CHIPS
        o_spec = pl.BlockSpec((None, tm, tn), lambda i, j, k: ((j * tn) // ns, i, ((j * tn) % ns) // tn))
        o_shape = jax.ShapeDtypeStruct((N_CHIPS, m, ns), out_dtype)
    else:
        o_spec = pl.BlockSpec((tm, tn), lambda i, j, k: (i, j))
        o_shape = jax.ShapeDtypeStruct((m, n), out_dtype)
    extra = list(extra) + ([add] if add is not None else [])
    if add is not None:
        post = lambda v, x: (v + x,)
    vecs, slots = list(vecs), list(slots)
    nx = len(extra) + len(vecs) + len(slots)
    out_dtypes = out_dtypes or [out_dtype]
    no = len(out_dtypes)

    def body(a_ref, *rest):
        b_refs, rest = rest[:nb], rest[nb:]
        o_refs, acc = rest[nx:nx + no], rest[nx + no:]
        if across:
            w = kc // N_CHIPS
            p = functools.reduce(jnp.add, [
                lax.dot_general(a_ref[:, s * w:(s + 1) * w].astype(BF16), b_refs[s][...].astype(BF16), dims, preferred_element_type=F32)
                for s in range(N_CHIPS)])
        else:
            p = lax.dot_general(a_ref[...].astype(BF16), b_refs[0][...].astype(BF16), dims, preferred_element_type=F32)

        def emit(v):
            res = post(v, *[r[...] for r in rest[:nx]]) if post else (v,)
            for o_ref, r in zip(o_refs, res):
                o_ref[...] = r.astype(o_ref.dtype)

        if nk == 1:
            emit(p)
        else:
            k = pl.program_id(2)

            @pl.when(k == 0)
            def _():
                acc[0][...] = p

            @pl.when(k > 0)
            def _():
                acc[0][...] += p

            @pl.when(k == nk - 1)
            def _():
                emit(acc[0][...])

    res = pl.pallas_call(
        body, name=name, grid=(m // tm, n // tn, nk),
        in_specs=[a_spec] + b_specs + [o_spec] * len(extra) + [pl.BlockSpec((1, tn), lambda i, j, k: (0, j))] * len(vecs)
        + [pl.BlockSpec((tm, LANES), lambda i, j, k: (i, 0))] * len(slots),
        out_specs=[o_spec] * no,
        out_shape=[jax.ShapeDtypeStruct(o_shape.shape, dt) for dt in out_dtypes],
        scratch_shapes=[pltpu.VMEM((tm, tn), F32)] if nk > 1 else [],
        compiler_params=_params(("parallel", "parallel", "arbitrary")),
    )(a, *b_arrs, *extra, *vecs, *slots)
    return res[0] if no == 1 else res


def _rowarg(r):
    return r if isinstance(r, tuple) else (r, r.shape[1], 0)


def _rowspec(r, tr, ncol):
    _, w, cb = r
    if ncol > 1:
        return pl.BlockSpec((tr, w // ncol), lambda j, i: (i, j))
    return pl.BlockSpec((tr, w), lambda j, i: (i, cb))


def _rowwise(name, f, rows, params, outs, tr=None, ncol=1):
    rows = [_rowarg(r) for r in rows]
    t = rows[0][0].shape[0]
    tr = tr or _tile(t, 528)
    nr, npm = len(rows), len(params)

    def body(*refs):
        vals = [r[...] for r in refs[:nr]] + [(p[0] if ncol > 1 else p[...]) for p in refs[nr:nr + npm]]
        res = f(pl.program_id(1) * tr, *vals)
        for o_ref, v in zip(refs[nr + npm:], res):
            o_ref[...] = v.astype(o_ref.dtype)

    def pspec(p):
        if ncol > 1:
            return pl.BlockSpec((1,) + p.shape[1:], lambda j, i, n=p.ndim: (j,) + (0,) * (n - 1))
        return _full_spec(p.shape, 2)

    return pl.pallas_call(
        body, name=name, grid=(ncol, t // tr),
        in_specs=[_rowspec(r, tr, ncol) for r in rows] + [pspec(p) for p in params],
        out_specs=[pl.BlockSpec((tr, w // ncol), lambda j, i: (i, j)) for w, _ in outs],
        out_shape=[jax.ShapeDtypeStruct((t, w), dt) for w, dt in outs],
        compiler_params=_params(("parallel", "parallel")),
    )(*[r[0] for r in rows], *params)


def _rowwise_vjp(name, f, rows, params, cts, tr=None, ncol=1, row_dtypes=None):
    rows = [_rowarg(r) for r in rows]
    cts = [_rowarg(c) for c in cts]
    t = rows[0][0].shape[0]
    tr = tr or _tile(t, 528)
    nr, npm, nc = len(rows), len(params), len(cts)
    row_dtypes = row_dtypes or [F32] * nr

    def body(*refs):
        i = pl.program_id(1)
        vals = [r[...] for r in refs[:nr]] + [(p[0] if ncol > 1 else p[...]) for p in refs[nr:nr + npm]]
        ct = tuple(c[...].astype(F32) for c in refs[nr + npm:nr + npm + nc])
        _, vjp = jax.vjp(lambda *a: tuple(f(i * tr, *a)), *vals)
        g = vjp(ct)
        outs = refs[nr + npm + nc:]
        for o_ref, v in zip(outs[:nr], g[:nr]):
            o_ref[...] = v.astype(o_ref.dtype)
        pg = [(v[None] if ncol > 1 else v) for v in g[nr:]]

        @pl.when(i == 0)
        def _():
            for o_ref, v in zip(outs[nr:], pg):
                o_ref[...] = v

        @pl.when(i > 0)
        def _():
            for o_ref, v in zip(outs[nr:], pg):
                o_ref[...] += v

    def pspec(p):
        if ncol > 1:
            return pl.BlockSpec((1,) + p.shape[1:], lambda j, i, n=p.ndim: (j,) + (0,) * (n - 1))
        return _full_spec(p.shape, 2)

    res = pl.pallas_call(
        body, name=name, grid=(ncol, t // tr),
        in_specs=[_rowspec(r, tr, ncol) for r in rows] + [pspec(p) for p in params] + [_rowspec(c, tr, ncol) for c in cts],
        out_specs=[pl.BlockSpec((tr, w // ncol), lambda j, i: (i, j)) for _, w, _ in rows] + [pspec(p) for p in params],
        out_shape=[jax.ShapeDtypeStruct((t, w), dt) for (_, w, _), dt in zip(rows, row_dtypes)]
        + [jax.ShapeDtypeStruct(p.shape, F32) for p in params],
        compiler_params=_params(("parallel", "arbitrary")),
    )(*[r[0] for r in rows], *params, *[c[0] for c in cts])
    return res[:nr], res[nr:]


def _valid(row0, tr):
    return (row0 + lax.broadcasted_iota(jnp.int32, (tr, 1), 0)) >= PAD


def _rms(x, g):
    return x * lax.rsqrt(jnp.mean(x * x, axis=-1, keepdims=True) + EPS) * g


def _softplus(x):
    return jnp.where(x < -15.0, jnp.exp(x), jnp.maximum(x, 0.0) + jnp.log(1.0 + jnp.exp(-jnp.abs(x))))


def _neg_expm1(z):
    return jnp.where(z > -0.01, -z * (1.0 + z * (0.5 + z * (1.0 / 6.0))), 1.0 - jnp.exp(z))


def _prenorm(h, g, name):
    return _rowwise(name, lambda r0, x, gg: (_rms(x, gg),), [h], [g], [(_rowarg(h)[1], BF16)])[0]


def _post_residual(m, h, g):
    assert m.shape[1] == D_MODEL
    return m, h + _rms(m, g)


def _postnorm_bwd(m, g, dh, name):
    (dm,), (dg,) = _rowwise_vjp(name, lambda r0, mm, gg: (_rms(mm, gg),), [m], [g], [dh], row_dtypes=[BF16])
    return dm, dg


def _prenorm_bwd_add(h, g, dhns, dh, name):
    t, w = h.shape
    tr = _tile(t, 528)
    nd = len(dhns)

    def body(h_ref, g_ref, *refs):
        dh_ref, o_ref, dg_ref = refs[nd:]
        i = pl.program_id(0)
        _, vjp = jax.vjp(_rms, h_ref[...], g_ref[...])
        dhn = refs[0][...].astype(F32)
        for r in refs[1:nd]:
            dhn = dhn + r[...].astype(F32)
        dx, dg = vjp(dhn)
        o_ref[...] = dh_ref[...] + dx

        @pl.when(i == 0)
        def _():
            dg_ref[...] = dg

        @pl.when(i > 0)
        def _():
            dg_ref[...] += dg

    row = pl.BlockSpec((tr, w), lambda i: (i, 0))
    return pl.pallas_call(
        body, name=name, grid=(t // tr,), in_specs=[row, _full_spec(g.shape, 1)] + [row] * (nd + 1),
        out_specs=[row, _full_spec(g.shape, 1)],
        out_shape=[jax.ShapeDtypeStruct((t, w), F32), jax.ShapeDtypeStruct(g.shape, F32)],
        compiler_params=_params(("arbitrary",)),
    )(h, g, *dhns, dh)


def _loss_and_grad(h, target, name):
    t, w = h.shape
    nb = t // CHUNK

    def body(h_ref, t_ref, s_ref, dh_ref):
        i = pl.program_id(0)

        @pl.when(i == 0)
        def _():
            s_ref[...] = jnp.zeros_like(s_ref)
            dh_ref[...] = jnp.zeros_like(dh_ref)

        @pl.when(i > 0)
        def _():
            err = h_ref[...] - t_ref[...]
            s_ref[...] += jnp.sum(err * err)
            dh_ref[...] = err * (1.0 / w)

    return pl.pallas_call(
        body, name=name, grid=(nb,),
        in_specs=[pl.BlockSpec((CHUNK, w), lambda i: (i, 0)), pl.BlockSpec((CHUNK, w), lambda i: (jnp.maximum(i - 1, 0), 0))],
        out_specs=[_full_spec((1, LANES), 1), pl.BlockSpec((CHUNK, w), lambda i: (i, 0))],
        out_shape=[jax.ShapeDtypeStruct((1, LANES), F32), jax.ShapeDtypeStruct((t, w), F32)],
        compiler_params=_params(("arbitrary",)),
    )(h, target)


def _mlp_fwd(h, p, l):
    hn = _prenorm(h, p["mlp_pre_g"][l], "mlp_prenorm")
    a, u = _mm(hn, p["w_up"][l], "nn", "mlp_up", post=lambda v: (v, jnp.square(jnp.maximum(v, 0.0))), out_dtypes=[BF16, BF16])
    d, h2 = _mm(u, p["w_down"][l], "nn", "mlp_down", extra=[h], vecs=[p["mlp_post_g"][l]], post=_post_residual, out_dtypes=[F32, F32])
    return h2, (h, hn, a, u, d)


def _mlp_bwd(dh, saved, p, l, grads):
    h, hn, a, u, d = saved
    dd, grads["mlp_post_g"][l] = _postnorm_bwd(d, p["mlp_post_g"][l], dh, "mlp_postnorm_bwd")
    grads["w_down"][l] = _mm(u, dd, "tn", "mlp_down_dw")
    da = _mm(dd, p["w_down"][l], "nt", "mlp_down_dx", extra=[a], post=lambda v, x: (2.0 * jnp.maximum(x.astype(F32), 0.0) * v,),
             out_dtypes=[BF16])
    grads["w_up"][l] = _mm(hn, da, "tn", "mlp_up_dw", out_chip_major=True)
    dhn = _mm(da, p["w_up"][l], "nt", "mlp_up_dx")
    dh, grads["mlp_pre_g"][l] = _prenorm_bwd_add(h, p["mlp_pre_g"][l], [dhn], dh, "mlp_prenorm_bwd")
    return dh


def _dot(a, b, mode):
    return lax.dot_general(a.astype(BF16), b.astype(BF16), _DIMS[mode], preferred_element_type=F32)


@jax.custom_vjp
def _bnn(a, b):
    return _dot(a, b, "nn")


_bnn.defvjp(lambda a, b: (_dot(a, b, "nn"), (a, b)), lambda r, ct: (_dot(ct, r[1], "nt"), _dot(r[0], ct, "tn")))


@jax.custom_vjp
def _bnt(a, b):
    return _dot(a, b, "nt")


_bnt.defvjp(lambda a, b: (_dot(a, b, "nt"), (a, b)), lambda r, ct: (_dot(ct, r[1], "nn"), _dot(ct, r[0], "tn")))


@jax.custom_vjp
def _btn(a, b):
    return _dot(a, b, "tn")


_btn.defvjp(lambda a, b: (_dot(a, b, "tn"), (a, b)), lambda r, ct: (_dot(r[1], ct, "nt"), _dot(r[0], ct, "nn")))


CONV_K = 4
HALO = 8


def _conv_fwd(x, w, b, name, cw, c0=0):
    t, c = x.shape[0], w.shape[1]
    tr = _tile(t, 528)
    hb = tr // HALO

    def body(x_ref, halo_ref, w_ref, b_ref, o_ref, ext):
        i = pl.program_id(1)
        ext[pl.ds(0, HALO), :] = jnp.where(i > 0, halo_ref[...], 0.0)
        ext[pl.ds(HALO, tr), :] = x_ref[...]
        acc = jnp.broadcast_to(b_ref[...], (tr, cw))
        for k in range(CONV_K):
            acc = acc + w_ref[pl.ds(k, 1), :] * ext[pl.ds(HALO - (CONV_K - 1) + k, tr), :]
        o_ref[...] = acc

    return pl.pallas_call(
        body, name=name, grid=(c // cw, t // tr),
        in_specs=[pl.BlockSpec((tr, cw), lambda j, i: (i, c0 + j)),
                  pl.BlockSpec((HALO, cw), lambda j, i: (jnp.maximum(i * hb - 1, 0), c0 + j)),
                  pl.BlockSpec((CONV_K, cw), lambda j, i: (0, j)), pl.BlockSpec((1, cw), lambda j, i: (0, j))],
        out_specs=pl.BlockSpec((tr, cw), lambda j, i: (i, j)),
        out_shape=jax.ShapeDtypeStruct((t, c), F32),
        scratch_shapes=[pltpu.VMEM((tr + HALO, cw), F32)],
        compiler_params=_params(("parallel", "parallel")),
    )(x, x, w, b)


def _conv_bwd(x, w, dy, name, cw, c0=0):
    t, c = x.shape[0], w.shape[1]
    tr = _tile(t, 528)
    hb = tr // HALO
    nb = t // tr

    def body(x_ref, xh_ref, w_ref, dy_ref, dyh_ref, dx_ref, dw_ref, db_ref, xe, de):
        c = cw
        i = pl.program_id(1)
        xe[pl.ds(0, HALO), :] = jnp.where(i > 0, xh_ref[...], 0.0)
        xe[pl.ds(HALO, tr), :] = x_ref[...]
        de[pl.ds(0, tr), :] = dy_ref[...]
        de[pl.ds(tr, HALO), :] = jnp.where(i < nb - 1, dyh_ref[...], 0.0)
        dy = dy_ref[...]
        acc = jnp.zeros((tr, c), F32)
        dw = jnp.zeros((CONV_K, c), F32)
        rows = lax.broadcasted_iota(jnp.int32, (CONV_K, 1), 0)
        for k in range(CONV_K):
            acc = acc + w_ref[pl.ds(k, 1), :] * de[pl.ds(CONV_K - 1 - k, tr), :]
            dwk = jnp.sum(dy * xe[pl.ds(HALO - (CONV_K - 1) + k, tr), :], axis=0, keepdims=True)
            dw = dw + jnp.where(rows == k, dwk, 0.0)
        dx_ref[...] = jnp.where(_valid(i * tr, tr), acc, 0.0).astype(dx_ref.dtype)
        db = jnp.sum(dy, axis=0, keepdims=True)

        @pl.when(i == 0)
        def _():
            dw_ref[...] = dw
            db_ref[...] = db

        @pl.when(i > 0)
        def _():
            dw_ref[...] += dw
            db_ref[...] += db

    row = pl.BlockSpec((tr, cw), lambda j, i: (i, j))
    return pl.pallas_call(
        body, name=name, grid=(c // cw, nb),
        in_specs=[pl.BlockSpec((tr, cw), lambda j, i: (i, c0 + j)),
                  pl.BlockSpec((HALO, cw), lambda j, i: (jnp.maximum(i * hb - 1, 0), c0 + j)),
                  pl.BlockSpec((CONV_K, cw), lambda j, i: (0, j)),
                  row, pl.BlockSpec((HALO, cw), lambda j, i: (jnp.minimum((i + 1) * hb, t // HALO - 1), j))],
        out_specs=[row, pl.BlockSpec((CONV_K, cw), lambda j, i: (0, j)), pl.BlockSpec((1, cw), lambda j, i: (0, j))],
        out_shape=[jax.ShapeDtypeStruct((t, c), BF16), jax.ShapeDtypeStruct((CONV_K, c), F32), jax.ShapeDtypeStruct((1, c), F32)],
        scratch_shapes=[pltpu.VMEM((tr + HALO, cw), F32), pltpu.VMEM((tr + HALO, cw), F32)],
        compiler_params=_params(("parallel", "arbitrary")),
    )(x, x, w, dy, dy)


SUB = 8


def _lru_scan(a, u, name):
    t, c = a.shape
    tr = _tile(t, 528)

    def body(a_ref, u_ref, o_ref, carry):
        @pl.when(pl.program_id(0) == 0)
        def _():
            carry[...] = jnp.zeros_like(carry)

        rows = lax.broadcasted_iota(jnp.int32, (SUB, 1), 0)

        def step(k, cin):
            r = pl.multiple_of(k * SUB, SUB)
            av, uv = a_ref[pl.ds(r, SUB), :], u_ref[pl.ds(r, SUB), :]
            for d in (1, 2, 4):
                m = rows >= d
                uv = uv + av * jnp.where(m, pltpu.roll(uv, d, 0), 0.0)
                av = av * jnp.where(m, pltpu.roll(av, d, 0), 1.0)
            hv = uv + av * cin
            o_ref[pl.ds(r, SUB), :] = hv
            return jnp.broadcast_to(hv[SUB - 1:SUB, :], (SUB, c))

        carry[...] = lax.fori_loop(0, tr // SUB, step, carry[...])

    row = pl.BlockSpec((tr, c), lambda i: (i, 0))
    return pl.pallas_call(
        body, name=name, grid=(t // tr,), in_specs=[row, row], out_specs=row,
        out_shape=jax.ShapeDtypeStruct((t, c), F32), scratch_shapes=[pltpu.VMEM((SUB, c), F32)],
        compiler_params=_params(("arbitrary",)),
    )(a, u)


def _lru_scan_bwd(a, hs, dy, name):
    t, c = a.shape
    tr = _tile(t, 528)
    nb, nt = t // tr, tr // SUB

    def body(a_ref, h_ref, hh_ref, dy_ref, du_ref, da_ref, gcar, acar):
        i = pl.program_id(0)

        @pl.when(i == 0)
        def _():
            gcar[...] = jnp.zeros_like(gcar)
            acar[...] = jnp.zeros_like(acar)

        rows = lax.broadcasted_iota(jnp.int32, (SUB, 1), 0)
        hhalo = jnp.where(i < nb - 1, hh_ref[...], 0.0)

        def step(kk, car):
            gin, a_next_first = car
            k = nt - 1 - kk
            r = pl.multiple_of(k * SUB, SUB)
            av, hv, dv = a_ref[pl.ds(r, SUB), :], h_ref[pl.ds(r, SUB), :], dy_ref[pl.ds(r, SUB), :]
            rp = pl.multiple_of(jnp.maximum(k - 1, 0) * SUB, SUB)
            hp = jnp.where(k > 0, h_ref[pl.ds(rp, SUB), :], hhalo)
            cv = jnp.where(rows < SUB - 1, pltpu.roll(av, SUB - 1, 0), a_next_first)
            gv = dv
            for d in (1, 2, 4):
                m = rows < SUB - d
                gv = gv + cv * jnp.where(m, pltpu.roll(gv, SUB - d, 0), 0.0)
                cv = cv * jnp.where(m, pltpu.roll(cv, SUB - d, 0), 1.0)
            gv = gv + cv * gin
            hprev = jnp.where(rows >= 1, pltpu.roll(hv, 1, 0), jnp.broadcast_to(hp[SUB - 1:SUB, :], (SUB, c)))
            du_ref[pl.ds(r, SUB), :] = gv
            da_ref[pl.ds(r, SUB), :] = gv * hprev
            return jnp.broadcast_to(gv[0:1, :], (SUB, c)), jnp.broadcast_to(av[0:1, :], (SUB, c))

        g, af = lax.fori_loop(0, nt, step, (gcar[...], acar[...]))
        gcar[...] = g
        acar[...] = af

    hb = tr // SUB
    row = pl.BlockSpec((tr, c), lambda i: (nb - 1 - i, 0))
    halo = pl.BlockSpec((SUB, c), lambda i: (jnp.maximum((nb - 1 - i) * hb - 1, 0), 0))
    return pl.pallas_call(
        body, name=name, grid=(nb,), in_specs=[row, row, halo, row], out_specs=[row, row],
        out_shape=[jax.ShapeDtypeStruct((t, c), F32)] * 2,
        scratch_shapes=[pltpu.VMEM((SUB, c), F32), pltpu.VMEM((SUB, c), F32)],
        compiler_params=_params(("arbitrary",)),
    )(a, hs, hs, dy)


def _lru_gates(row0, xr, wa, ba, wi, bi, lam):
    r = jax.nn.sigmoid(_bnn(xr, wa) + ba)
    i = jax.nn.sigmoid(_bnn(xr, wi) + bi)
    log_a = -LRU_C * r * _softplus(-lam)
    u = jnp.sqrt(_neg_expm1(2.0 * log_a)) * (i * xr)
    return jnp.exp(log_a), jnp.where(_valid(row0, xr.shape[0]), u, 0.0)


def _lru_gate_out(row0, hs, yw):
    return (hs * jax.nn.gelu(yw),)


def _rglru_fwd(h, p, l, o):
    hn = _prenorm(h, p["mix_pre_g"][l], "rg_prenorm")
    xw = _mm(hn, p["rg_w_x"][o], "nn", "rg_in_x")
    yw = _mm(hn, p["rg_w_y"][o], "nn", "rg_in_y")
    xr = _conv_fwd(xw, p["rg_conv_w"][o], p["rg_conv_b"][o], "rg_conv", cw=LRU_WIDTH // 2)
    gp = [p["rg_w_a"][o], p["rg_b_a"][o], p["rg_w_i"][o], p["rg_b_i"][o], p["rg_lambda"][o]]
    a, u = _rowwise("rg_gates", _lru_gates, [xr], gp, [(LRU_WIDTH, F32)] * 2, ncol=LRU_BLOCKS, tr=_tile(h.shape[0], 1056))
    hs = _lru_scan(a, u, "rg_scan")
    hg = _rowwise("rg_gate_out", _lru_gate_out, [hs, yw], [], [(LRU_WIDTH, BF16)])[0]
    m, h2 = _mm(hg, p["rg_w_out"][o], "nn", "rg_out", extra=[h], vecs=[p["mix_post_g"][l]], post=_post_residual, out_dtypes=[F32, F32])
    return h2, (h, hn, xw, yw, xr, a, hs, hg, m)


def _rglru_bwd(dh, saved, p, l, o, grads):
    h, hn, xw, yw, xr, a, hs, hg, m = saved
    dm, grads["mix_post_g"][l] = _postnorm_bwd(m, p["mix_post_g"][l], dh, "rg_postnorm_bwd")
    grads["rg_w_out"][o] = _mm(hg, dm, "tn", "rg_out_dw")
    dhg = _mm(dm, p["rg_w_out"][o], "nt", "rg_out_dx")
    (dhs, dyw), _ = _rowwise_vjp("rg_gate_out_bwd", _lru_gate_out, [hs, yw], [], [dhg], row_dtypes=[F32, BF16])
    du, da = _lru_scan_bwd(a, hs, dhs, "rg_scan_bwd")
    gp = [p["rg_w_a"][o], p["rg_b_a"][o], p["rg_w_i"][o], p["rg_b_i"][o], p["rg_lambda"][o]]
    (dxr,), gg = _rowwise_vjp("rg_gates_bwd", _lru_gates, [xr], gp, [da, du], ncol=LRU_BLOCKS, tr=_tile(h.shape[0], 1056))
    grads["rg_w_a"][o], grads["rg_b_a"][o], grads["rg_w_i"][o], grads["rg_b_i"][o], grads["rg_lambda"][o] = gg
    dxw, grads["rg_conv_w"][o], grads["rg_conv_b"][o] = _conv_bwd(xw, p["rg_conv_w"][o], dxr, "rg_conv_bwd", cw=LRU_WIDTH // 2)
    grads["rg_w_x"][o] = _mm(hn, dxw, "tn", "rg_in_x_dw")
    grads["rg_w_y"][o] = _mm(hn, dyw, "tn", "rg_in_y_dw")
    dhx = _mm(dxw, p["rg_w_x"][o], "nt", "rg_in_x_dx")
    dhy = _mm(dyw, p["rg_w_y"][o], "nt", "rg_in_y_dx")
    dh, grads["mix_pre_g"][l] = _prenorm_bwd_add(h, p["mix_pre_g"][l], [dhx, dhy], dh, "rg_prenorm_bwd")
    return dh


SSD_GW = SSD_D_INNER // SSD_GROUPS
SSD_GH = SSD_HEADS // SSD_GROUPS
XACT_B = SSD_D_INNER // SSD_STATE
XACT_C = XACT_B + SSD_GROUPS


def _hp(a, b, dims=_DIMS["nn"]):
    return lax.dot_general(a, b, dims, precision=lax.Precision.HIGHEST, preferred_element_type=F32)


def _split_dot(a, e, mode, parts):
    eb = e.astype(BF16)
    out, rest = None, a
    for _ in range(parts):
        term = rest.astype(BF16)
        rest = rest - term.astype(F32)
        if mode in ("nn", "nt"):
            prod = lax.dot_general(term, eb, _DIMS[mode], preferred_element_type=F32)
        else:
            prod = lax.dot_general(eb, term, _DIMS["nn" if mode == "left" else "tn"], preferred_element_type=F32)
        out = prod if out is None else out + prod
    return out


@jax.custom_vjp
def _select_nn(a, e):
    return _split_dot(a, e, "nn", 3)


_select_nn.defvjp(lambda a, e: (_split_dot(a, e, "nn", 3), e), lambda e, ct: (_split_dot(ct, e, "nt", 2), jnp.zeros_like(e)))


@jax.custom_vjp
def _select_left(e, a):
    return _split_dot(a, e, "left", 3)


_select_left.defvjp(lambda e, a: (_split_dot(a, e, "left", 3), e),
                    lambda e, ct: (jnp.zeros_like(e), _split_dot(ct, e, "left_t", 2)))


def _ssd_chunk(xs, bm, cm, dt, da, ht, g):
    l = CHUNK
    ri = lax.broadcasted_iota(jnp.int32, (l, l), 0)
    ci = lax.broadcasted_iota(jnp.int32, (l, l), 1)
    causal = ri >= ci
    tri = causal.astype(F32)
    hr = lax.broadcasted_iota(jnp.int32, (LANES, SSD_GW), 0)
    hc = lax.broadcasted_iota(jnp.int32, (LANES, SSD_GW), 1)
    expand = (hr == g * SSD_GH + hc // SSD_HEAD_DIM).astype(F32)
    acs = _select_left(tri, da)
    acs_t = acs.T
    acs_e = _select_nn(acs, expand)
    x = xs * _select_nn(dt, expand)
    gmat = _bnt(cm, bm)
    lane = lax.broadcasted_iota(jnp.int32, (1, LANES), 1)
    sub = lax.broadcasted_iota(jnp.int32, (LANES, 1), 0)
    colhead = lax.broadcasted_iota(jnp.int32, (1, SSD_GW), 1) // SSD_HEAD_DIM
    y = _bnn(cm, ht) * jnp.exp(acs_e)
    for k in range(SSD_GH):
        hh = g * SSD_GH + k
        col = jnp.sum(jnp.where(lane == hh, acs, 0.0), axis=1, keepdims=True)
        row = jnp.sum(jnp.where(sub == hh, acs_t, 0.0), axis=0, keepdims=True)
        decay = jnp.exp(jnp.where(causal, col - row, -1e30))
        y = y + _bnn(gmat * decay, jnp.where(colhead == k, x, 0.0))
    last = lax.broadcasted_iota(jnp.int32, (l, 1), 0) == l - 1
    a_last = jnp.sum(jnp.where(last, acs_e, 0.0), axis=0, keepdims=True)
    st = _btn(bm, x * jnp.exp(a_last - acs_e))
    return y, ht * jnp.exp(a_last) + st


def _ssd_specs(nc, rev):
    def cc(c):
        return nc - 1 - c if rev else c

    return [pl.BlockSpec((CHUNK, SSD_GW), lambda c, g: (cc(c), g)),
            pl.BlockSpec((CHUNK, SSD_STATE), lambda c, g: (cc(c), XACT_B + g)),
            pl.BlockSpec((CHUNK, SSD_STATE), lambda c, g: (cc(c), XACT_C + g)),
            pl.BlockSpec((CHUNK, LANES), lambda c, g: (cc(c), 0)),
            pl.BlockSpec((CHUNK, LANES), lambda c, g: (cc(c), 0))]


def _ssd_scan(xact, dt, da, name):
    t = xact.shape[0]
    nc = t // CHUNK

    def body(xs_ref, b_ref, c_ref, dt_ref, da_ref, y_ref, hs_ref, state):
        c, g = pl.program_id(0), pl.program_id(1)

        @pl.when(c == 0)
        def _():
            state[g] = jnp.zeros((SSD_STATE, SSD_GW), F32)

        ht = state[g]
        hs_ref[0] = ht
        y, ht2 = _ssd_chunk(xs_ref[...], b_ref[...], c_ref[...], dt_ref[...], da_ref[...], ht, g)
        y_ref[...] = y
        state[g] = ht2

    return pl.pallas_call(
        body, name=name, grid=(nc, SSD_GROUPS), in_specs=_ssd_specs(nc, False),
        out_specs=[pl.BlockSpec((CHUNK, SSD_GW), lambda c, g: (c, g)),
                   pl.BlockSpec((1, SSD_STATE, SSD_GW), lambda c, g: (c * SSD_GROUPS + g, 0, 0))],
        out_shape=[jax.ShapeDtypeStruct((t, SSD_D_INNER), F32), jax.ShapeDtypeStruct((nc * SSD_GROUPS, SSD_STATE, SSD_GW), F32)],
        scratch_shapes=[pltpu.VMEM((SSD_GROUPS, SSD_STATE, SSD_GW), F32)],
        compiler_params=_params(("arbitrary", "arbitrary")),
    )(xact, xact, xact, dt, da)


def _ssd_scan_bwd(xact, dt, da, hsave, dy, dxskip, name):
    t = xact.shape[0]
    nc = t // CHUNK

    def body(xs_ref, b_ref, c_ref, dt_ref, da_ref, hs_ref, dy_ref, sk_ref, dxs_ref, db_ref, dc_ref, ddt_ref, dda_ref, dstate):
        c, g = pl.program_id(0), pl.program_id(1)

        @pl.when(c == 0)
        def _():
            dstate[g] = jnp.zeros((SSD_STATE, SSD_GW), F32)

        _, vjp = jax.vjp(lambda *a: _ssd_chunk(*a, g), xs_ref[...], b_ref[...], c_ref[...], dt_ref[...], da_ref[...], hs_ref[0])
        dxs, dbm, dcm, ddt, dda, dht = vjp((dy_ref[...], dstate[g]))
        dxs_ref[...] = dxs + sk_ref[...]
        db_ref[...] = dbm
        dc_ref[...] = dcm
        dstate[g] = dht

        @pl.when(g == 0)
        def _():
            ddt_ref[...] = ddt
            dda_ref[...] = dda

        @pl.when(g > 0)
        def _():
            ddt_ref[...] += ddt
            dda_ref[...] += dda

    grp = pl.BlockSpec((CHUNK, SSD_GW), lambda c, g: (nc - 1 - c, g))
    st = pl.BlockSpec((CHUNK, SSD_STATE), lambda c, g: (nc - 1 - c, g))
    hd = pl.BlockSpec((CHUNK, LANES), lambda c, g: (nc - 1 - c, 0))
    return pl.pallas_call(
        body, name=name, grid=(nc, SSD_GROUPS),
        in_specs=_ssd_specs(nc, True) + [pl.BlockSpec((1, SSD_STATE, SSD_GW), lambda c, g: ((nc - 1 - c) * SSD_GROUPS + g, 0, 0)), grp, grp],
        out_specs=[grp, st, st, hd, hd],
        out_shape=[jax.ShapeDtypeStruct((t, SSD_D_INNER), F32), jax.ShapeDtypeStruct((t, SSD_GROUPS * SSD_STATE), F32),
                   jax.ShapeDtypeStruct((t, SSD_GROUPS * SSD_STATE), F32), jax.ShapeDtypeStruct((t, LANES), F32),
                   jax.ShapeDtypeStruct((t, LANES), F32)],
        scratch_shapes=[pltpu.VMEM((SSD_GROUPS, SSD_STATE, SSD_GW), F32)],
        compiler_params=_params(("arbitrary", "arbitrary")),
    )(xact, xact, xact, dt, da, hsave, dy, dxskip)


def _ssd_act(row0, xc):
    return (jnp.where(_valid(row0, xc.shape[0]), jax.nn.silu(xc), 0.0),)


def _ssd_dt(row0, dtraw, dt_bias, a_log):
    dt = jnp.where(_valid(row0, dtraw.shape[0]), _softplus(dtraw + dt_bias), 0.0)
    return dt, dt * -jnp.exp(a_log)


def _ssd_post(row0, y, xs, z, d_skip, norm_g):
    hr = lax.broadcasted_iota(jnp.int32, (LANES, SSD_D_INNER), 0)
    hc = lax.broadcasted_iota(jnp.int32, (LANES, SSD_D_INNER), 1)
    expand = (hr == hc // SSD_HEAD_DIM).astype(F32)
    d_e = jnp.sum(_hp(jnp.broadcast_to(d_skip, (SUB, LANES)), expand), axis=0, keepdims=True) * (1.0 / SUB)
    return (_rms((y + xs * d_e) * jax.nn.silu(z), norm_g),)


ROPE_LO, ROPE_MID, ROPE_HI = MLA_NOPE, MLA_NOPE + MLA_ROPE // 2, MLA_NOPE + MLA_ROPE
ATT_SCALE = (MLA_NOPE + MLA_ROPE) ** -0.5


def _slot_lane(width):
    return lax.broadcasted_iota(jnp.int32, (1, width), 1) % LANES


def _swap_halves(x):
    width = x.shape[1]
    lane = _slot_lane(width)
    sw = jnp.where(lane < ROPE_MID, pltpu.roll(x, width - MLA_ROPE // 2, 1), pltpu.roll(x, MLA_ROPE // 2, 1))
    return jnp.where((lane >= ROPE_LO) & (lane < ROPE_HI), sw, 0.0)


def _rope(x, cos, sin):
    n = x.shape[1] // LANES
    return x * jnp.tile(cos, (1, n)) + _swap_halves(x) * jnp.tile(sin, (1, n))


def _rope_t(dy, cos, sin):
    n = dy.shape[1] // LANES
    return dy * jnp.tile(cos, (1, n)) + _swap_halves(dy * jnp.tile(sin, (1, n)))


ATT_SCALE2 = ATT_SCALE * float(np.log2(np.e))
MASKED = -1e30


def _att_bias(blk):
    r = jnp.arange(blk)[:, None]
    c = jnp.arange(blk)[None, :]
    zero = jnp.zeros((blk, blk), F32)
    first = jnp.where(c >= PAD, 0.0, MASKED) + zero
    diag = jnp.where(c <= r, 0.0, MASKED).astype(F32)
    return jnp.stack([zero, first, diag, jnp.minimum(first, diag), zero + MASKED])


def _att_bias_index(j, i):
    return jnp.where(j > i, 4, jnp.where(j == 0, 1, 0) + jnp.where(j == i, 2, 0))


def _key_slots(row0, kv, kr):
    width = kv.shape[1]
    return jnp.where(_slot_lane(width) < MLA_NOPE, kv, jnp.tile(kr, (1, width // LANES))), kv


def _attn_fwd(qr, km, vb, name, carried=None):
    t = qr.shape[0]
    blk = _tile(t, 384, LANES)
    nq = t // blk

    bias = _att_bias(blk)

    def body(q_ref, k_ref, v_ref, b_ref, o_ref, s0, s1, p0, p1):
        i = pl.program_id(1)
        lane = lax.broadcasted_iota(jnp.int32, (1, LANES), 1)
        qb = q_ref[...]

        def rows(j):
            return pl.ds(pl.multiple_of(jnp.clip(j, 0, i) * blk, blk), blk)

        def scores(j):
            return lax.dot_general(qb, k_ref[rows(j), :], _DIMS["nt"], preferred_element_type=F32) + b_ref[_att_bias_index(j, i)]

        def half(j, car, s_cur, s_nxt, p_cur, p_prv):
            m, l, acc, al_prev = car
            s_nxt[...] = scores(j + 1)
            acc2 = al_prev * acc + lax.dot_general(p_prv[...], v_ref[rows(j - 1), :], _DIMS["nn"], preferred_element_type=F32)
            m2 = jnp.maximum(m, jnp.max(s_cur[...], axis=1, keepdims=True))
            al = jnp.exp2((m - m2) * ATT_SCALE2)
            pm = jnp.exp2(s_cur[...] * ATT_SCALE2 - m2 * ATT_SCALE2)
            p_cur[...] = pm.astype(BF16)
            return m2, al * l + jnp.sum(pm, axis=1, keepdims=True), acc2, al

        def step(jj, car):
            car = half(2 * jj, car, s0, s1, p0, p1)
            return half(2 * jj + 1, car, s1, s0, p1, p0)

        s0[...] = scores(0)
        p1[...] = jnp.zeros((blk, blk), BF16)
        car = (jnp.full((blk, 1), MASKED, F32), jnp.zeros((blk, 1), F32), jnp.zeros((blk, LANES), F32), jnp.ones((blk, 1), F32))
        steps = i // 2 + 1
        m, l, acc, al_last = lax.fori_loop(0, steps, step, car)
        acc = al_last * acc + lax.dot_general(p1[...], v_ref[rows(2 * steps - 1), :], _DIMS["nn"], preferred_element_type=F32)
        out = jnp.where(lane >= MLA_NOPE, acc / l, m * ATT_SCALE + jnp.log(l))
        o_ref[...] = jnp.where(_valid(i * blk, blk), out, 0.0)

    seq_h = pl.BlockSpec((t, LANES), lambda h, i: (0, h))
    (o,), carried_out = _carry_call(
        body, name, (MLA_HEADS, nq),
        [pl.BlockSpec((blk, LANES), lambda h, i: (i, h)), seq_h, seq_h, _full_spec(bias.shape, 2)],
        [pl.BlockSpec((blk, LANES), lambda h, i: (i, h))], [jax.ShapeDtypeStruct((t, MLA_HEADS * LANES), F32)],
        [pltpu.VMEM((blk, blk), F32)] * 2 + [pltpu.VMEM((blk, blk), BF16)] * 2, (qr, km, vb, bias), carried)
    return o, carried_out


def _attn_bwd(qr, km, vb, o, do, cos, sin, name, carried=None):
    t = qr.shape[0]
    blk = _tile(t, 384, LANES)
    nq = t // blk

    bias = _att_bias(blk)
    log2e = float(np.log2(np.e))

    def body(q_ref, o_ref, do_ref, k_ref, v_ref, b_ref, cos_ref, sin_ref, dq_out, dkv_ref, dkr_ref,
             s0, s1, dp0, dp1, p0, p1, ds0, ds1, dk_s, dv_s, dq_ref):
        h, j = pl.program_id(0), pl.program_id(1)
        lane = lax.broadcasted_iota(jnp.int32, (1, LANES), 1)

        @pl.when(j == 0)
        def _():
            dq_ref[...] = jnp.zeros_like(dq_ref)

        @pl.when((h == 0) & (j == 0))
        def _():
            dkr_ref[...] = jnp.zeros_like(dkr_ref)

        kmat, vmat = k_ref[...], v_ref[...]

        def rows(i):
            return pl.ds(pl.multiple_of(jnp.clip(i, j, nq - 1) * blk, blk), blk)

        def first_stage(i, s_buf, dp_buf):
            ic = jnp.minimum(i, nq - 1)
            s_buf[...] = lax.dot_general(q_ref[rows(ic), :], kmat, _DIMS["nt"], preferred_element_type=F32) + b_ref[_att_bias_index(j, ic)]
            dp_buf[...] = lax.dot_general(do_ref[rows(ic), :].astype(BF16), vmat, _DIMS["nt"], preferred_element_type=F32)

        def middle_stage(i, s_buf, dp_buf, p_buf, ds_buf):
            r = rows(i)
            ob, dob = o_ref[r, :], do_ref[r, :]
            delta = jnp.sum(dob * ob, axis=1, keepdims=True)
            pm = jnp.exp2(s_buf[...] * ATT_SCALE2 - ob[:, 0:1] * log2e)
            p_buf[...] = pm.astype(BF16)
            ds_buf[...] = (pm * (dp_buf[...] - delta) * ATT_SCALE).astype(BF16)

        def last_stage(i, p_buf, ds_buf):
            r = rows(i)
            dv_s[...] += lax.dot_general(p_buf[...], do_ref[r, :].astype(BF16), _DIMS["tn"], preferred_element_type=F32)
            dk_s[...] += lax.dot_general(ds_buf[...], q_ref[r, :], _DIMS["tn"], preferred_element_type=F32)
            dq_ref[r, :] += lax.dot_general(ds_buf[...], kmat, _DIMS["nn"], preferred_element_type=F32)

        n = nq - j
        dk_s[...] = jnp.zeros((blk, LANES), F32)
        dv_s[...] = jnp.zeros((blk, LANES), F32)
        first_stage(j, s0, dp0)
        first_stage(j + 1, s1, dp1)
        middle_stage(j, s0, dp0, p0, ds0)

        def step(tt, carry):
            i = j + 2 * tt + 1
            first_stage(i + 1, s0, dp0)
            last_stage(i - 1, p0, ds0)
            middle_stage(i, s1, dp1, p1, ds1)
            first_stage(i + 2, s1, dp1)
            last_stage(i, p1, ds1)
            middle_stage(i + 1, s0, dp0, p0, ds0)
            return carry

        lax.fori_loop(0, (n - 1) // 2, step, 0)

        @pl.when(n % 2 == 0)
        def _():
            last_stage(nq - 2, p0, ds0)
            middle_stage(nq - 1, s1, dp1, p1, ds1)
            last_stage(nq - 1, p1, ds1)

        @pl.when(n % 2 == 1)
        def _():
            last_stage(nq - 1, p0, ds0)

        dk = dk_s[...]
        dkv_ref[...] = jnp.where(lane < MLA_NOPE, dk, dv_s[...]).astype(dkv_ref.dtype)
        dkr_ref[rows(j), :] += jnp.where(lane >= MLA_NOPE, dk, 0.0)

        @pl.when(j == nq - 1)
        def _():
            dq_out[...] = _rope_t(dq_ref[...], cos_ref[...], sin_ref[...]).astype(dq_out.dtype)

    seq_h = pl.BlockSpec((t, LANES), lambda h, j: (0, h))
    seq = pl.BlockSpec((t, LANES), lambda h, j: (0, 0))
    blk_h = pl.BlockSpec((blk, LANES), lambda h, j: (j, h))
    return _carry_call(
        body, name, (MLA_HEADS, nq), [seq_h, seq_h, seq_h, blk_h, blk_h, _full_spec(bias.shape, 2), seq, seq],
        [seq_h, blk_h, seq],
        [jax.ShapeDtypeStruct((t, MLA_HEADS * LANES), BF16), jax.ShapeDtypeStruct((t, MLA_HEADS * LANES), BF16),
         jax.ShapeDtypeStruct((t, LANES), F32)],
        [pltpu.VMEM((blk, blk), F32)] * 4 + [pltpu.VMEM((blk, blk), BF16)] * 4 + [pltpu.VMEM((blk, LANES), F32)] * 2
        + [pltpu.VMEM((t, LANES), F32)], (qr, o, do, km, vb, bias, cos, sin), carried)


def _rms_rows(row0, x, g):
    return (_rms(x, g),)


def _ssdmla_fwd(h, p, l, e, cos, sin, carried=None):
    hn = _prenorm(h, p["mix_pre_g"][l], "sm_prenorm")
    proj = _mm(hn, p["w_in"][e], "nn", "sm_in")
    xc = _conv_fwd(proj, p["ssd_conv_w"][e], p["ssd_conv_b"][e], "ssd_conv", cw=SSD_GW, c0=PROJ_XBC // SSD_GW)
    xact = _rowwise("ssd_act", _ssd_act, [xc], [], [(SSD_CONV_CH, F32)])[0]
    dt, da = _rowwise("ssd_dt", _ssd_dt, [(proj, LANES, PROJ_DT // LANES)], [p["ssd_dt_bias"][e], p["ssd_a_log"][e]],
                      [(LANES, F32)] * 2)
    y, hsave = _ssd_scan(xact, dt, da, "ssd_scan")
    y_ssd = _rowwise("ssd_post", _ssd_post, [y, (xact, SSD_D_INNER, 0), (proj, SSD_D_INNER, 0)],
                     [p["ssd_d"][e], p["ssd_norm_g"][e]], [(SSD_D_INNER, BF16)])[0]
    cqn = _prenorm((proj, MLA_Q_RANK, PROJ_CQ // MLA_Q_RANK), p["mla_q_norm_g"][e], "mla_qnorm")
    ckvn = _prenorm((proj, MLA_KV_RANK, PROJ_CKV // MLA_KV_RANK), p["mla_kv_norm_g"][e], "mla_kvnorm")
    kr = _rowwise("mla_krope", lambda r0, x, c, s: (_rope(x, c, s),), [(proj, LANES, PROJ_KR // LANES), cos, sin], [],
                  [(LANES, F32)])[0]
    qr = _mm(cqn, p["mla_w_q_up"][e], "nn", "mla_q_up", slots=[cos, sin], post=lambda v, c, s: (_rope(v, c, s),), out_dtypes=[BF16])
    km, vb = _mm(ckvn, p["mla_w_kv_up"][e], "nn", "mla_kv_up", slots=[kr], post=lambda v, k: _key_slots(0, v, k),
                 out_dtypes=[BF16, BF16])
    o, carried_out = _attn_fwd(qr, km, vb, "mla_attn", carried)
    m1 = _mm(y_ssd, p["w_out_ssd"][e], "nn", "sm_out_ssd")
    m, h2 = _mm(o, p["w_out_att"][e], "nn", "sm_out_att", extra=[m1, h], vecs=[p["mix_post_g"][l]],
                post=lambda v, m1b, hb, g: _post_residual(v + m1b, hb, g), out_dtypes=[F32, F32])
    return h2, (h, hn, proj, xc, xact, dt, da, y, hsave, y_ssd, cqn, ckvn, qr, km, vb, o, m), carried_out


def _ssdmla_bwd(dh, saved, p, l, e, cos, sin, grads, carry=None):
    h, hn, proj, xc, xact, dt, da, y, hsave, y_ssd, cqn, ckvn, qr, km, vb, o, m = saved
    dm, grads["mix_post_g"][l] = _postnorm_bwd(m, p["mix_post_g"][l], dh, "sm_postnorm_bwd")
    grads["w_out_ssd"][e] = _mm(y_ssd, dm, "tn", "sm_out_ssd_dw")
    grads["w_out_att"][e] = _mm(o, dm, "tn", "sm_out_att_dw")
    dy_ssd = _mm(dm, p["w_out_ssd"][e], "nt", "sm_out_ssd_dx")
    do = _mm(dm, p["w_out_att"][e], "nt", "sm_out_att_dx")
    (dq, dkv, dkr), carried_out = _attn_bwd(qr, km, vb, o, do, cos, sin, "mla_attn_bwd", carry() if carry else None)
    dkr_raw = _rowwise("mla_krope_bwd", lambda r0, d, c, s: (_rope_t(d, c, s),), [dkr, cos, sin], [], [(LANES, F32)])[0]
    grads["mla_w_q_up"][e] = _mm(cqn, dq, "tn", "mla_q_up_dw")
    dcqn = _mm(dq, p["mla_w_q_up"][e], "nt", "mla_q_up_dx")
    (dcq,), (grads["mla_q_norm_g"][e],) = _rowwise_vjp(
        "mla_qnorm_bwd", _rms_rows, [(proj, MLA_Q_RANK, PROJ_CQ // MLA_Q_RANK)], [p["mla_q_norm_g"][e]], [dcqn])
    grads["mla_w_kv_up"][e] = _mm(ckvn, dkv, "tn", "mla_kv_up_dw")
    dckvn = _mm(dkv, p["mla_w_kv_up"][e], "nt", "mla_kv_up_dx")
    (dckv,), (grads["mla_kv_norm_g"][e],) = _rowwise_vjp(
        "mla_kvnorm_bwd", _rms_rows, [(proj, MLA_KV_RANK, PROJ_CKV // MLA_KV_RANK)], [p["mla_kv_norm_g"][e]], [dckvn])
    (dy, dxskip, dz), (grads["ssd_d"][e], grads["ssd_norm_g"][e]) = _rowwise_vjp(
        "ssd_post_bwd", _ssd_post, [y, (xact, SSD_D_INNER, 0), (proj, SSD_D_INNER, 0)], [p["ssd_d"][e], p["ssd_norm_g"][e]], [dy_ssd])
    dxs, db, dc, ddt, dda = _ssd_scan_bwd(xact, dt, da, hsave, dy, dxskip, "ssd_scan_bwd")
    dxact = jnp.concatenate([dxs, db, dc], axis=1)
    (dxc,), _ = _rowwise_vjp("ssd_act_bwd", _ssd_act, [xc], [], [dxact])
    dxbc, grads["ssd_conv_w"][e], grads["ssd_conv_b"][e] = _conv_bwd(
        proj, p["ssd_conv_w"][e], dxc, "ssd_conv_bwd", cw=SSD_GW, c0=PROJ_XBC // SSD_GW)
    (ddtraw,), (grads["ssd_dt_bias"][e], grads["ssd_a_log"][e]) = _rowwise_vjp(
        "ssd_dt_bwd", _ssd_dt, [(proj, LANES, PROJ_DT // LANES)], [p["ssd_dt_bias"][e], p["ssd_a_log"][e]], [ddt, dda])
    dproj = jnp.concatenate([dz, dxbc, ddtraw, dcq, dckv, dkr_raw], axis=1).astype(BF16)
    grads["w_in"][e] = _mm(hn, dproj, "tn", "sm_in_dw")
    dhn = _mm(dproj, p["w_in"][e], "nt", "sm_in_dx")
    dh, grads["mix_pre_g"][l] = _prenorm_bwd_add(h, p["mix_pre_g"][l], [dhn], dh, "sm_prenorm_bwd")
    return dh, carried_out


GAINS = ("mix_pre_g", "mix_post_g", "mlp_pre_g", "mlp_post_g", "ssd_norm_g", "mla_q_norm_g", "mla_kv_norm_g", "ssd_conv_b", "rg_conv_b")
HEAD_VECS = ("ssd_dt_bias", "ssd_a_log", "ssd_d")
LRU_VECS = ("rg_b_a", "rg_b_i", "rg_lambda")
IN_DT_END = SSD_D_INNER + SSD_CONV_CH + SSD_HEADS
IN_KR = IN_DT_END + MLA_Q_RANK + MLA_KV_RANK


def _each(a, f):
    layers = a if isinstance(a, list) else [a[i] for i in range(a.shape[0])]
    return [None if x is None else f(x) for x in layers]


def _layout_params(w):
    p = {k: _each(w[k], lambda a: a[None, :]) for k in GAINS}
    for k in HEAD_VECS:
        p[k] = _each(w[k], lambda a: jnp.pad(a, (0, LANES - SSD_HEADS))[None, :])
    for k in LRU_VECS:
        p[k] = _each(w[k], lambda a: a.reshape(LRU_BLOCKS, 1, LRU_BLOCK))
    for k in ("w_up", "w_down", "mla_w_kv_up", "rg_w_x", "rg_w_y", "rg_w_out"):
        p[k] = _each(w[k], lambda a: a if isinstance(a, Gathered) else a.astype(BF16))
    for k in ("ssd_conv_w", "rg_conv_w", "rg_w_a", "rg_w_i"):
        p[k] = _each(w[k], lambda a: a)

    def w_in(a):
        def zcols(n):
            return jnp.zeros((a.shape[0], n), a.dtype)

        return jnp.concatenate([a[:, :IN_DT_END], zcols(PROJ_CQ - IN_DT_END), a[:, IN_DT_END:IN_KR], zcols(ROPE_LO),
                                a[:, IN_KR:], zcols(LANES - ROPE_HI)], axis=1).astype(BF16)

    def q_up(a):
        a = a.reshape(MLA_Q_RANK, MLA_HEADS, MLA_NOPE + MLA_ROPE)
        return jnp.pad(a, ((0, 0), (0, 0), (0, LANES - MLA_NOPE - MLA_ROPE))).reshape(MLA_Q_RANK, MLA_HEADS * LANES).astype(BF16)

    def out_att(a):
        a = a[SSD_D_INNER:].reshape(MLA_HEADS, MLA_V, D_MODEL)
        return jnp.pad(a, ((0, 0), (LANES - MLA_V, 0), (0, 0))).reshape(MLA_HEADS * LANES, D_MODEL).astype(BF16)

    p["w_in"] = _each(w["w_in"], w_in)
    p["mla_w_q_up"] = _each(w["mla_w_q_up"], q_up)
    p["w_out_ssd"] = _each(w["w_out_ab"], lambda a: a[:SSD_D_INNER].astype(BF16))
    p["w_out_att"] = _each(w["w_out_ab"], out_att)
    return p


def _rope_tables(t):
    pos = (jnp.arange(t) - PAD).astype(F32)
    inv = ROPE_BASE ** (-jnp.arange(0, MLA_ROPE, 2, dtype=F32) / MLA_ROPE)
    ang = pos[:, None] * inv[None, :]
    c, s = jnp.cos(ang), jnp.sin(ang)
    one, zero = jnp.ones((t, MLA_NOPE), F32), jnp.zeros((t, MLA_NOPE), F32)
    tail = LANES - ROPE_HI
    return (jnp.concatenate([one, c, c, one[:, :tail]], axis=1), jnp.concatenate([zero, -s, s, zero[:, :tail]], axis=1))


GRAD_KEYS = GAINS + HEAD_VECS + LRU_VECS + ("w_up", "w_down", "mla_w_kv_up", "rg_w_x", "rg_w_y", "rg_w_out", "ssd_conv_w",
                                            "rg_conv_w", "rg_w_a", "rg_w_i", "w_in", "mla_w_q_up", "w_out_ssd", "w_out_att")


def _device_step(x, meta, target, p, hooks=None):
    t = PAD + N_META + x.shape[0]
    cos, sin = _rope_tables(t)
    h = jnp.concatenate([jnp.zeros((PAD, D_MODEL), F32), meta, x], axis=0)
    n_even, n_odd = (DEPTH + 1) // 2, DEPTH // 2
    saved = []
    for l in range(DEPTH):
        if l % 2 == 0:
            carried = hooks.forward_exchange() if hooks and l == 0 else None
            h, sm, arrived = _ssdmla_fwd(h, p, l, l // 2, cos, sin, carried)
            if carried is not None:
                p = hooks.after_forward_exchange(arrived)
        else:
            h, sm = _rglru_fwd(h, p, l, l // 2)
        h, sp = _mlp_fwd(h, p, l)
        saved.append((sm, sp))
    sq, dh = _loss_and_grad(h, target, "loss")
    per_layer = {"mix_pre_g": DEPTH, "mix_post_g": DEPTH, "mlp_pre_g": DEPTH, "mlp_post_g": DEPTH, "w_up": DEPTH, "w_down": DEPTH}
    grads = {k: [None] * per_layer.get(k, n_odd if k.startswith("rg_") else n_even) for k in GRAD_KEYS}
    for l in reversed(range(DEPTH)):
        sm, sp = saved[l]
        dh = _mlp_bwd(dh, sp, p, l, grads)
        if l % 2 == 0:
            carry = functools.partial(hooks.backward_exchange, grads, l) if hooks else None
            dh, arrived = _ssdmla_bwd(dh, sm, p, l, l // 2, cos, sin, grads, carry)
            if hooks:
                hooks.after_backward_exchange(arrived, l)
        else:
            dh = _rglru_bwd(dh, sm, p, l, l // 2, grads)
    return sq, dh, grads


MESH = pl.DeviceIdType.MESH
ANY = pl.BlockSpec(memory_space=pl.ANY)


def _mesh_pos():
    return lax.axis_index("x"), lax.axis_index("y"), lax.axis_index("c")


def _other_chips(x, y):
    return [(1 - x, y), (x, 1 - y), (1 - x, 1 - y)]


def _remote(src, dst, send_sems, recv_sems, k, to):
    return pltpu.make_async_remote_copy(src_ref=src, dst_ref=dst, send_sem=send_sems.at[k], recv_sem=recv_sems.at[k],
                                        device_id=to, device_id_type=MESH)


class Exchange:
    def __init__(self, ins, outs, aliases, n_sems, plan):
        self.ins, self.outs, self.aliases, self.n_sems, self.plan = list(ins), list(outs), dict(aliases), n_sems, plan


def _sems(n):
    return [pltpu.SemaphoreType.DMA((n,)), pltpu.SemaphoreType.DMA((n,))]


def _run_exchange(name, ex):
    ni, no = len(ex.ins), len(ex.outs)

    def body(*refs):
        sends = ex.plan(refs[:ni], refs[ni:ni + no], refs[-2], refs[-1], False)
        for cp in sends:
            cp.start()
        for cp in ex.plan(refs[:ni], refs[ni:ni + no], refs[-2], refs[-1], True):
            cp.wait_recv()
        for cp in sends:
            cp.wait_send()

    return pl.pallas_call(body, name=name, in_specs=[ANY] * ni, out_specs=[ANY] * no, out_shape=ex.outs,
                          input_output_aliases=ex.aliases, scratch_shapes=_sems(ex.n_sems))(*ex.ins)


def _carry_call(body, name, grid, in_specs, out_specs, out_shape, scratch_shapes, args, ex):
    if ex is None:
        res = pl.pallas_call(body, name=name, grid=grid, in_specs=in_specs, out_specs=out_specs, out_shape=out_shape,
                             scratch_shapes=scratch_shapes, compiler_params=_params(("arbitrary",) * len(grid)))(*args)
        return res, None
    ni, no, ns, xi, xo = len(in_specs), len(out_specs), len(scratch_shapes), len(ex.ins), len(ex.outs)

    def wrapped(*refs):
        ins, xin = refs[:ni], refs[ni:ni + xi]
        outs, xout = refs[ni + xi:ni + xi + no], refs[ni + xi + no:ni + xi + no + xo]
        scr, send_sems, recv_sems = refs[ni + xi + no + xo:-2], refs[-2], refs[-1]
        pid = [pl.program_id(d) for d in range(len(grid))]
        first = functools.reduce(jnp.logical_and, [p == 0 for p in pid])
        last = functools.reduce(jnp.logical_and, [p == g - 1 for p, g in zip(pid, grid)])

        @pl.when(first)
        def _():
            for cp in ex.plan(xin, xout, send_sems, recv_sems, False):
                cp.start()

        body(*ins, *outs, *scr)

        @pl.when(last)
        def _():
            for cp in ex.plan(xin, xout, send_sems, recv_sems, True):
                cp.wait_recv()
            for cp in ex.plan(xin, xout, send_sems, recv_sems, False):
                cp.wait_send()

    res = pl.pallas_call(
        wrapped, name=name, grid=grid, in_specs=list(in_specs) + [ANY] * xi, out_specs=list(out_specs) + [ANY] * xo,
        out_shape=list(out_shape) + ex.outs, scratch_shapes=list(scratch_shapes) + _sems(ex.n_sems),
        input_output_aliases={ni + i: no + o for i, o in ex.aliases.items()},
        compiler_params=_params(("arbitrary",) * len(grid)))(*args, *ex.ins)
    return res[:no], res[no:]


def _gather_ici(srcs, bufs, ranges):
    n = len(srcs)

    def plan(in_refs, out_refs, ss, rs, arrivals):
        x, y, c = _mesh_pos()
        cps = []
        for t, (l0, nl) in enumerate(ranges):
            if nl:
                s, o, lr = in_refs[t], out_refs[t], pl.ds(l0, nl)
                for j, (cx, cy) in enumerate(_other_chips(x, y)):
                    chip = 2 * cx + cy if arrivals else 2 * x + y
                    cps.append(_remote(s.at[lr, c], o.at[chip, lr, c], ss, rs, (N_CHIPS - 1) * t + j, (cx, cy, c)))
        return cps

    outs = [jax.ShapeDtypeStruct((N_CHIPS,) + s.shape, s.dtype) for s in srcs]
    if bufs is None:
        return Exchange(srcs, outs, {}, (N_CHIPS - 1) * n, plan)
    return Exchange(list(srcs) + list(bufs), outs, {n + t: t for t in range(n)}, (N_CHIPS - 1) * n, plan)


def _gather_d2d(srcs, bufs, ranges):
    n = len(srcs)

    def plan(in_refs, out_refs, ss, rs, arrivals):
        x, y, c = _mesh_pos()
        sib, me = (x, y, 1 - c), 2 * x + y
        cps = []
        for t, (l0, nl) in enumerate(ranges):
            if nl:
                s, o, lr = in_refs[t], out_refs[t], pl.ds(l0, nl)
                for j, (cx, cy) in enumerate(_other_chips(x, y)):
                    slot = o.at[2 * cx + cy, lr, c]
                    cps.append(_remote(slot, o.at[2 * cx + cy, lr, 1 - c] if arrivals else slot, ss, rs, N_CHIPS * t + j, sib))
                cps.append(_remote(s.at[lr], o.at[me, lr], ss, rs, N_CHIPS * t + N_CHIPS - 1, sib))
        return cps

    outs = [jax.ShapeDtypeStruct(b.shape, b.dtype) for b in bufs]
    return Exchange(list(srcs) + list(bufs), outs, {n + t: t for t in range(n)}, N_CHIPS * n, plan)


def _gather_chips(srcs, name):
    ranges = [(0, s.shape[0]) for s in srcs]
    bufs = _run_exchange(name + "_ici", _gather_ici(srcs, None, ranges))
    return _run_exchange(name + "_d2d", _gather_d2d(srcs, bufs, ranges))


def _pair_exchange(gs):
    def plan(in_refs, out_refs, ss, rs, arrivals):
        x, y, c = _mesh_pos()
        return [_remote(g.at[pl.ds(0, N_CHIPS), 1 - c], o, ss, rs, t, (x, y, 1 - c)) for t, (g, o) in enumerate(zip(in_refs, out_refs))]

    return Exchange(gs, [jax.ShapeDtypeStruct((g.shape[0],) + g.shape[2:], g.dtype) for g in gs], {}, len(gs), plan)


def _chip_exchange(ps, slots, qs, q_shapes):
    n = len(ps)
    kept = [g for g, q in enumerate(qs) if q is not None]

    def plan(in_refs, out_refs, ss, rs, arrivals):
        x, y, c = _mesh_pos()
        return [_remote(in_refs[t].at[2 * cx + cy], out_refs[g].at[j, li], ss, rs, (N_CHIPS - 1) * t + j, (cx, cy, c))
                for t, (g, li) in enumerate(slots) for j, (cx, cy) in enumerate(_other_chips(x, y))]

    return Exchange(list(ps) + [qs[g] for g in kept], q_shapes, {n + i: g for i, g in enumerate(kept)}, (N_CHIPS - 1) * n, plan)


def _pair_share(fs):
    def plan(in_refs, out_refs, ss, rs, arrivals):
        x, y, c = _mesh_pos()
        return [_remote(o.at[pl.ds(0, o.shape[0]), c], o.at[pl.ds(0, o.shape[0]), 1 - c if arrivals else c], ss, rs, t, (x, y, 1 - c))
                for t, o in enumerate(out_refs)]

    return Exchange(fs, [jax.ShapeDtypeStruct(f.shape, f.dtype) for f in fs], {t: t for t in range(len(fs))}, len(fs), plan)


SUM_BLOCK = 512 * 1024


def _sum_pair(g, ra, c, name):
    n, _, h, w = g.shape
    tr = _tile(h, max(16, SUM_BLOCK // w), 16)

    def body(c_ref, g_ref, r_ref, o_ref):
        o_ref[...] = (g_ref[0] + r_ref[...]).astype(o_ref.dtype)

    return pl.pallas_call(
        body, name=name,
        grid_spec=pltpu.PrefetchScalarGridSpec(
            num_scalar_prefetch=1, grid=(n, h // tr),
            in_specs=[pl.BlockSpec((1, 1, tr, w), lambda s, i, cr: (s, cr[0], i, 0)), pl.BlockSpec((1, tr, w), lambda s, i, cr: (s, i, 0))],
            out_specs=pl.BlockSpec((1, tr, w), lambda s, i, cr: (s, i, 0))),
        out_shape=jax.ShapeDtypeStruct((n, h, w), BF16),
        compiler_params=_params(("parallel", "parallel")),
    )(c.reshape(1).astype(jnp.int32), g, ra)


def _sum_chips(ps, q, pos, name):
    nc, nl, h, w = q.shape
    tr = _tile(h, max(16, SUM_BLOCK // (w * nl)), 16)

    def body(x_ref, y_ref, c_ref, *refs):
        q_ref, o_ref = refs[nl], refs[nl + 1]
        for l in range(nl):
            acc = refs[l][0].astype(F32)
            for j in range(nc):
                acc = acc + q_ref[j, l].astype(F32)
            o_ref[l] = acc

    return pl.pallas_call(
        body, name=name,
        grid_spec=pltpu.PrefetchScalarGridSpec(
            num_scalar_prefetch=3, grid=(h // tr,),
            in_specs=[pl.BlockSpec((1, tr, w), lambda i, x, y, c: (2 * x[0] + y[0], i, 0))] * nl
            + [pl.BlockSpec((nc, nl, tr, w), lambda i, x, y, c: (0, 0, i, 0))],
            out_specs=pl.BlockSpec((nl, None, tr, w), lambda i, x, y, c: (0, c[0], i, 0))),
        out_shape=jax.ShapeDtypeStruct((nl, 2, h, w), F32),
        compiler_params=_params(("parallel",)),
    )(*pos, *ps, q)


def _adamw(g, w, m, v, name):
    def f(r0, gg, ww, mm, vv):
        m2 = ADAM_B1 * mm + (1.0 - ADAM_B1) * gg
        v2 = ADAM_B2 * vv + (1.0 - ADAM_B2) * jnp.square(gg)
        m_hat = m2 / (1.0 - ADAM_B1 ** ADAM_STEP)
        v_hat = v2 / (1.0 - ADAM_B2 ** ADAM_STEP)
        return gg, -ADAM_LR * (m_hat / (jnp.sqrt(v_hat) + ADAM_EPS) + ADAM_WD * ww), m2, v2

    return _rowwise(name, f, [g, w, m, v], [], [(g.shape[1], F32)] * 4, tr=_tile(g.shape[0], 512))


WEIGHTS = (
    ("meta_tokens", (N_META, D_MODEL), 1), ("mix_pre_g", (DEPTH, D_MODEL), None), ("mix_post_g", (DEPTH, D_MODEL), None),
    ("mlp_pre_g", (DEPTH, D_MODEL), None), ("mlp_post_g", (DEPTH, D_MODEL), None), ("w_up", (DEPTH, D_MODEL, D_FF), 2),
    ("w_down", (DEPTH, D_FF, D_MODEL), 1), ("w_in", (2, D_MODEL, 3248), 2), ("ssd_conv_w", (2, CONV_K, SSD_CONV_CH), 2),
    ("ssd_conv_b", (2, SSD_CONV_CH), None), ("ssd_dt_bias", (2, SSD_HEADS), None), ("ssd_a_log", (2, SSD_HEADS), None),
    ("ssd_d", (2, SSD_HEADS), None), ("ssd_norm_g", (2, SSD_D_INNER), None), ("mla_q_norm_g", (2, MLA_Q_RANK), None),
    ("mla_w_q_up", (2, MLA_Q_RANK, MLA_HEADS * (MLA_NOPE + MLA_ROPE)), 2), ("mla_kv_norm_g", (2, MLA_KV_RANK), None),
    ("mla_w_kv_up", (2, MLA_KV_RANK, MLA_HEADS * (MLA_NOPE + MLA_V)), 2), ("w_out_ab", (2, SSD_D_INNER + MLA_HEADS * MLA_V, D_MODEL), 1),
    ("rg_w_x", (2, D_MODEL, LRU_WIDTH), 2), ("rg_w_y", (2, D_MODEL, LRU_WIDTH), 2), ("rg_conv_w", (2, CONV_K, LRU_WIDTH), 2),
    ("rg_conv_b", (2, LRU_WIDTH), 1), ("rg_w_a", (2, LRU_BLOCKS, LRU_BLOCK, LRU_BLOCK), None), ("rg_b_a", (2, LRU_WIDTH), 1),
    ("rg_w_i", (2, LRU_BLOCKS, LRU_BLOCK, LRU_BLOCK), None), ("rg_b_i", (2, LRU_WIDTH), 1), ("rg_lambda", (2, LRU_WIDTH), 1),
    ("rg_w_out", (2, LRU_WIDTH, D_MODEL), 1),
)
BIG = {"w_up": "col", "w_down": "row", "w_in": "col", "mla_w_q_up": "col", "mla_w_kv_up": "col", "w_out_ab": "row",
       "rg_w_x": "col", "rg_w_y": "col", "rg_w_out": "row"}
DIRECT = ("w_up", "w_down")
FLAT_QUANTUM = 2 * 16 * LANES
TABLE = {name: (shape, d) for name, shape, d in WEIGHTS}
SMALL_SHARDED = tuple(name for name, _, d in WEIGHTS if d is not None and name not in BIG)
REPLICATED = tuple(name for name, _, d in WEIGHTS if d is None)


def _chips_to_full(a, kind):
    if kind == "col":
        return jnp.moveaxis(a, 0, 2).reshape(a.shape[1], a.shape[2], -1)
    return jnp.moveaxis(a, 0, 1).reshape(a.shape[1], -1, a.shape[3])


def _full_to_chips(g, kind):
    if kind == "col":
        return jnp.moveaxis(g.reshape(g.shape[0], N_CHIPS, -1), 1, 0)
    return g.reshape(N_CHIPS, -1, g.shape[1])


def _shard_shape(shape, d):
    return shape[:d] + (shape[d] // N_CHIPS,) + shape[d + 1:]


def _shard_major(full, d):
    s = full.shape
    return jnp.moveaxis(full.reshape(s[:d] + (N_CHIPS, s[d] // N_CHIPS) + s[d + 1:]), d, 0).reshape(N_CHIPS, -1)


def _from_shard_major(a, shape, d):
    ss = _shard_shape(shape, d)
    return jnp.moveaxis(a.reshape((N_CHIPS,) + ss), 0, d).reshape(shape)


def _pad_cols(a, quantum):
    n = a.shape[-1]
    return jnp.pad(a, [(0, 0)] * (a.ndim - 1) + [(0, -n % quantum)])


def _big_pieces(g):
    def w_in(a):
        return jnp.concatenate([a[:, :IN_DT_END], a[:, PROJ_CQ:PROJ_KR], a[:, PROJ_KR + ROPE_LO:PROJ_KR + ROPE_HI]], axis=1)

    def q_up(a):
        return a.reshape(MLA_Q_RANK, MLA_HEADS, LANES)[:, :, :MLA_NOPE + MLA_ROPE].reshape(MLA_Q_RANK, -1)

    def out_ab(sa):
        s, a = sa
        return jnp.concatenate([s, a.reshape(MLA_HEADS, LANES, D_MODEL)[:, LANES - MLA_V:, :].reshape(-1, D_MODEL)], axis=0)

    ident = lambda a: a
    full = {"w_down": _each(g["w_down"], ident), "w_in": _each(g["w_in"], w_in), "mla_w_q_up": _each(g["mla_w_q_up"], q_up),
            "mla_w_kv_up": _each(g["mla_w_kv_up"], ident),
            "w_out_ab": _each([None if s is None or a is None else (s, a) for s, a in zip(g["w_out_ssd"], g["w_out_att"])], out_ab),
            "rg_w_x": _each(g["rg_w_x"], ident), "rg_w_y": _each(g["rg_w_y"], ident), "rg_w_out": _each(g["rg_w_out"], ident)}
    return {name: (list(g[name]) if name == "w_up" else _each(full[name], lambda a, k=BIG[name]: _full_to_chips(a, k))) for name in BIG}


def _small_grads(g, dh):
    out = {k: jnp.stack(g[k])[:, 0, :] for k in GAINS}
    for k in HEAD_VECS:
        out[k] = jnp.stack(g[k])[:, 0, :SSD_HEADS]
    for k in LRU_VECS:
        out[k] = jnp.stack(g[k]).reshape(-1, LRU_WIDTH)
    for k in ("ssd_conv_w", "rg_conv_w", "rg_w_a", "rg_w_i"):
        out[k] = jnp.stack(g[k])
    out["meta_tokens"] = dh[PAD:PAD + N_META]
    return out


class StepExchanges:
    def __init__(self, w):
        self.w = w
        self.c = lax.axis_index("c")
        self.riding, self.ras = {}, {}
        small = _pad_cols(jnp.concatenate([w[n].reshape(-1) for n in SMALL_SHARDED]), FLAT_QUANTUM).reshape(1, 2, -1, LANES)
        self.srcs = [self._halves(w[n].astype(BF16)) for n in BIG] + [small]
        first = {n: (0, 1 if n in ("w_in", "mla_w_q_up", "mla_w_kv_up", "w_out_ab") else 0) for n in BIG}
        self.first = [first[n] for n in BIG] + [(0, 1)]
        self.rest = [(nl, TABLE[n][0][0] - nl) for n, (_, nl) in zip(BIG, self.first)] + [(0, 0)]
        bufs = _run_exchange("gather_first_ici", _gather_ici(self.srcs, None, self.first))
        self.bufs = _run_exchange("gather_first_d2d", _gather_d2d(self.srcs, bufs, self.first))

    @staticmethod
    def _halves(a):
        return a.reshape(a.shape[0], 2, a.shape[1] // 2, a.shape[2])

    def params(self, ranges):
        w = self.w
        full = {n: w[n] for n in REPLICATED}
        for name, buf, (l0, nl) in zip(BIG, self.bufs, ranges):
            a = buf.reshape(buf.shape[:2] + (-1, buf.shape[4]))
            have = range(l0, l0 + nl)
            if name in DIRECT:
                full[name] = [Gathered(a, BIG[name], l) if l in have else None for l in range(a.shape[1])]
            else:
                full[name] = [_chips_to_full(a[:, l:l + 1], BIG[name])[0] if l in have else None for l in range(a.shape[1])]
        got, off = self.bufs[-1].reshape(N_CHIPS, -1), 0
        for name in SMALL_SHARDED:
            shape, d = TABLE[name]
            n = int(np.prod(_shard_shape(shape, d)))
            full[name] = _from_shard_major(got[:, off:off + n], shape, d)
            off += n
        self.meta = full.pop("meta_tokens")
        return _layout_params(full)

    def forward_exchange(self):
        return _gather_ici(self.srcs, self.bufs, self.rest)

    def after_forward_exchange(self, arrived):
        self.bufs = _run_exchange("gather_rest_d2d", _gather_d2d(self.srcs, arrived, self.rest))
        return self.params([(0, TABLE[n][0][0]) for n in BIG])

    def _pair_sums(self, pieces, tag):
        keys = list(pieces)
        ras = _run_exchange("grads_pair_exchange_" + tag, _pair_exchange([pieces[k] for k in keys]))
        return {k: _sum_pair(pieces[k], ra, self.c, "grads_pair_sum") for k, ra in zip(keys, ras)}

    def _q_shapes(self):
        return [jax.ShapeDtypeStruct((N_CHIPS - 1, s.shape[0]) + s.shape[2:], BF16) for s in self.srcs[:-1]]

    def backward_exchange(self, grads, layer):
        big = _big_pieces(grads)
        pieces = {(g, l): pc.reshape(N_CHIPS, 2, pc.shape[1] // 2, pc.shape[2]) for g, name in enumerate(BIG)
                  for l, pc in enumerate(big[name]) if pc is not None and (g, l) not in self.riding}
        if layer > 0:
            self.riding = pieces
            return _pair_exchange(list(pieces.values()))
        self.ps = {k: _sum_pair(self.riding[k], ra, self.c, "grads_pair_sum") for k, ra in self.ras.items()}
        self.ps.update(self._pair_sums(pieces, "early"))
        self.early = list(self.ps)
        return _chip_exchange([self.ps[k] for k in self.early], self.early, [None] * len(BIG), self._q_shapes())

    def after_backward_exchange(self, arrived, layer):
        if layer > 0:
            self.ras = dict(zip(self.riding, arrived))
        else:
            self.qs = list(arrived)

    def finish(self, grads, dh):
        big, small = _big_pieces(grads), _small_grads(grads, dh)
        pieces = {(g, l): pc.reshape(N_CHIPS, 2, pc.shape[1] // 2, pc.shape[2])
                  for g, name in enumerate(BIG) for l, pc in enumerate(big[name]) if (g, l) not in self.ps}
        sharded = jnp.concatenate([_shard_major(small[n], TABLE[n][1]) for n in SMALL_SHARDED], axis=1)
        rep = _pad_cols(jnp.concatenate([small[n].reshape(-1) for n in REPLICATED]), N_CHIPS * FLAT_QUANTUM)
        n_sh, n_rep = sharded.shape[1], rep.shape[0] // N_CHIPS
        flat = _pad_cols(jnp.concatenate([sharded, rep.reshape(N_CHIPS, n_rep)], axis=1), FLAT_QUANTUM)
        pieces[(len(BIG), 0)] = flat.reshape(N_CHIPS, 2, -1, LANES)
        late = self._pair_sums(pieces, "late")
        self.ps.update(late)
        keys = list(late)
        small_q = jax.ShapeDtypeStruct((N_CHIPS - 1, 1) + late[(len(BIG), 0)].shape[1:], BF16)
        qs = _run_exchange("grads_chip_exchange_late",
                           _chip_exchange([late[k] for k in keys], keys, self.qs + [None], self._q_shapes() + [small_q]))
        pos = [lax.axis_index(a).reshape(1).astype(jnp.int32) for a in ("x", "y", "c")]
        sums = [_sum_chips([self.ps[(g, l)] for l in range(q.shape[1])], q, pos, "grads_chip_sum") for g, q in enumerate(qs)]
        outs = _run_exchange("grads_pair_share", _pair_share(sums))
        out = {name: o.reshape(o.shape[0], -1, o.shape[3]) for name, o in zip(BIG, outs)}
        f = outs[-1].reshape(-1)
        rep_all = _gather_chips([f[n_sh:n_sh + n_rep].reshape(1, 2, -1, LANES)], "grads_gather_replicated")[0].reshape(-1)
        off = 0
        for name in SMALL_SHARDED:
            ss = _shard_shape(*TABLE[name])
            n = int(np.prod(ss))
            out[name] = f[off:off + n].reshape(ss)
            off += n
        off = 0
        for name in REPLICATED:
            shape = TABLE[name][0]
            n = int(np.prod(shape))
            out[name] = rep_all[off:off + n].reshape(shape)
            off += n
        return out


def kernel(x, meta_tokens, mix_pre_g, mix_post_g, mlp_pre_g, mlp_post_g, w_up, w_down, w_in, ssd_conv_w, ssd_conv_b, ssd_dt_bias, ssd_a_log, ssd_d, ssd_norm_g, mla_q_norm_g, mla_w_q_up, mla_kv_norm_g, mla_w_kv_up, w_out_ab, rg_w_x, rg_w_y, rg_conv_w, rg_conv_b, rg_w_a, rg_b_a, rg_w_i, rg_b_i, rg_lambda, rg_w_out, loss_target, m_meta_tokens, m_mix_pre_g, m_mix_post_g, m_mlp_pre_g, m_mlp_post_g, m_w_up, m_w_down, m_w_in, m_ssd_conv_w, m_ssd_conv_b, m_ssd_dt_bias, m_ssd_a_log, m_ssd_d, m_ssd_norm_g, m_mla_q_norm_g, m_mla_w_q_up, m_mla_kv_norm_g, m_mla_w_kv_up, m_w_out_ab, m_rg_w_x, m_rg_w_y, m_rg_conv_w, m_rg_conv_b, m_rg_w_a, m_rg_b_a, m_rg_w_i, m_rg_b_i, m_rg_lambda, m_rg_w_out, v_meta_tokens, v_mix_pre_g, v_mix_post_g, v_mlp_pre_g, v_mlp_post_g, v_w_up, v_w_down, v_w_in, v_ssd_conv_w, v_ssd_conv_b, v_ssd_dt_bias, v_ssd_a_log, v_ssd_d, v_ssd_norm_g, v_mla_q_norm_g, v_mla_w_q_up, v_mla_kv_norm_g, v_mla_w_kv_up, v_w_out_ab, v_rg_w_x, v_rg_w_y, v_rg_conv_w, v_rg_conv_b, v_rg_w_a, v_rg_b_a, v_rg_w_i, v_rg_b_i, v_rg_lambda, v_rg_w_out):
    names = [n for n, _, _ in WEIGHTS]
    w = dict(zip(names, (meta_tokens, mix_pre_g, mix_post_g, mlp_pre_g, mlp_post_g, w_up, w_down, w_in, ssd_conv_w, ssd_conv_b, ssd_dt_bias, ssd_a_log, ssd_d, ssd_norm_g, mla_q_norm_g, mla_w_q_up, mla_kv_norm_g, mla_w_kv_up, w_out_ab, rg_w_x, rg_w_y, rg_conv_w, rg_conv_b, rg_w_a, rg_b_a, rg_w_i, rg_b_i, rg_lambda, rg_w_out)))
    m = dict(zip(names, (m_meta_tokens, m_mix_pre_g, m_mix_post_g, m_mlp_pre_g, m_mlp_post_g, m_w_up, m_w_down, m_w_in, m_ssd_conv_w, m_ssd_conv_b, m_ssd_dt_bias, m_ssd_a_log, m_ssd_d, m_ssd_norm_g, m_mla_q_norm_g, m_mla_w_q_up, m_mla_kv_norm_g, m_mla_w_kv_up, m_w_out_ab, m_rg_w_x, m_rg_w_y, m_rg_conv_w, m_rg_conv_b, m_rg_w_a, m_rg_b_a, m_rg_w_i, m_rg_b_i, m_rg_lambda, m_rg_w_out)))
    v = dict(zip(names, (v_meta_tokens, v_mix_pre_g, v_mix_post_g, v_mlp_pre_g, v_mlp_post_g, v_w_up, v_w_down, v_w_in, v_ssd_conv_w, v_ssd_conv_b, v_ssd_dt_bias, v_ssd_a_log, v_ssd_d, v_ssd_norm_g, v_mla_q_norm_g, v_mla_w_q_up, v_mla_kv_norm_g, v_mla_w_kv_up, v_w_out_ab, v_rg_w_x, v_rg_w_y, v_rg_conv_w, v_rg_conv_b, v_rg_w_a, v_rg_b_a, v_rg_w_i, v_rg_b_i, v_rg_lambda, v_rg_w_out)))
    ex = StepExchanges(w)
    p = ex.params(ex.first)
    sq, dh, grads = _device_step(x[0], ex.meta, loss_target[0], p, hooks=ex)
    loss = lax.psum(0.5 * sq[0, 0] / D_MODEL, ("x", "y", "c"))
    g = ex.finish(grads, dh)
    grad, delta, new_m, new_v = {}, {}, {}, {}
    for name in names:
        shape = g[name].shape
        two_d = (int(np.prod(shape[:-1])), shape[-1])
        res = _adamw(g[name].reshape(two_d), w[name].reshape(two_d), m[name].reshape(two_d), v[name].reshape(two_d), "adamw")
        grad[name], delta[name], new_m[name], new_v[name] = (r.reshape(shape) for r in res)
    grad_x = dh[PAD + N_META:][None]
    return (loss, grad_x, *[grad[n] for n in names], *[delta[n] for n in names], *[new_m[n] for n in names], *[new_v[n] for n in names])
```

```python
import functools

import jax
import jax.numpy as jnp
import numpy as np
from jax import lax
from jax.experimental import pallas as pl
from jax.experimental.pallas import tpu as pltpu

F32 = jnp.float32
BF16 = jnp.bfloat16

D_MODEL = 1024
DEPTH = 4
N_META = 16
CHUNK = 128
PAD = CHUNK - N_META
EPS = 1e-6
SSD_HEADS = 16
SSD_HEAD_DIM = 64
SSD_D_INNER = SSD_HEADS * SSD_HEAD_DIM
SSD_GROUPS = 2
SSD_STATE = 128
SSD_CONV_CH = SSD_D_INNER + 2 * SSD_GROUPS * SSD_STATE
MLA_HEADS = 16
MLA_NOPE = 64
MLA_ROPE = 32
MLA_V = 64
MLA_Q_RANK = 384
MLA_KV_RANK = 256
ROPE_BASE = 10000.0
LRU_WIDTH = 1280
LRU_BLOCKS = 10
LRU_BLOCK = 128
LRU_C = 8.0
D_FF = 4 * D_MODEL
ADAM_LR, ADAM_B1, ADAM_B2, ADAM_EPS, ADAM_WD, ADAM_STEP = 0.001, 0.9, 0.999, 1e-08, 0.01, 10

LANES = 128
VMEM_LIMIT = 56 * 1024 * 1024
MM_VMEM_BUDGET = 40 * 1024 * 1024
PROJ_Z, PROJ_XBC, PROJ_DT, PROJ_CQ, PROJ_CKV, PROJ_KR = 0, 1024, 2560, 2688, 3072, 3328
PROJ_W = 3456


def _tile(n, cap, mult=8):
    for t in range(min(n, cap), 0, -1):
        if n % t == 0 and t % mult == 0:
            return t
    return n


def _params(sem):
    return pltpu.CompilerParams(dimension_semantics=sem, vmem_limit_bytes=VMEM_LIMIT)


def _full_spec(shape, ngrid):
    nd = len(shape)
    if ngrid == 1:
        return pl.BlockSpec(shape, lambda i: (0,) * nd)
    if ngrid == 2:
        return pl.BlockSpec(shape, lambda i, j: (0,) * nd)
    return pl.BlockSpec(shape, lambda i, j, k: (0,) * nd)


_DIMS = {"nn": (((1,), (0,)), ((), ())), "nt": (((1,), (1,)), ((), ())), "tn": (((0,), (0,)), ((), ()))}


class Gathered:
    def __init__(self, arr, kind, layer):
        self.arr, self.kind, self.layer = arr, kind, layer
        _, _, r, c = arr.shape
        self.shape = (r, N_CHIPS * c) if kind == "col" else (N_CHIPS * r, c)


N_CHIPS = 4


def _mm(a, b, mode, name, out_dtype=F32, add=None, out_chip_major=False, extra=(), vecs=(), slots=(), post=None, out_dtypes=None):
    if mode == "nn":
        (m, kc), (_, n) = a.shape, b.shape
    elif mode == "nt":
        (m, kc), (n, _) = a.shape, b.shape
    else:
        (kc, m), (_, n) = a.shape, b.shape
    n_tile = n // N_CHIPS if out_chip_major else n
    across = isinstance(b, Gathered) and (mode, b.kind) in (("nn", "row"), ("nt", "col"))
    if mode == "tn":
        tm, tk = _tile(m, 1024, LANES), kc
        fits = [c for c in (1280, 1152, 1024, 768, 640, 512) if n_tile % c == 0 and
                2 * kc * (tm * a.dtype.itemsize + c * b.dtype.itemsize) + 2 * tm * c * 4 <= MM_VMEM_BUDGET]
        tn = fits[0] if fits else _tile(n_tile, 1280, LANES)
        if not fits:
            tk = _tile(kc, 1408, LANES)
    else:
        tn = _tile(n_tile, 1280, LANES)
        tk = _tile(kc, 4096, LANES)
        tm = _tile(m, 1056 if tk <= 1024 else 528, 16)
    nk = kc // tk
    if mode == "tn":
        a_spec = pl.BlockSpec((tk, tm), lambda i, j, k: (k, i))
    else:
        a_spec = pl.BlockSpec((tm, tk), lambda i, j, k: (i, k))
    b_arrs = [b]
    if isinstance(b, Gathered):
        layer = b.layer
        sr, sc = b.arr.shape[2:]
        if across:
            assert nk == 1 and kc == N_CHIPS * (sr if b.kind == "row" else sc)
            b_arrs = [b.arr] * N_CHIPS
            if b.kind == "row":
                b_specs = [pl.BlockSpec((None, None, sr, tn), lambda i, j, k, s=s: (s, layer, 0, j)) for s in range(N_CHIPS)]
            else:
                b_specs = [pl.BlockSpec((None, None, tn, sc), lambda i, j, k, s=s: (s, layer, j, 0)) for s in range(N_CHIPS)]
        else:
            b_arrs = [b.arr]
            br, bc = (tk, tn) if mode == "nn" else (tn, tk)
            assert mode in ("nn", "nt") and sr % br == 0 and sc % bc == 0

            def b_map(i, j, k):
                r, c = (k, j) if mode == "nn" else (j, k)
                if b.kind == "col":
                    return ((c * bc) // sc, layer, r, ((c * bc) % sc) // bc)
                return ((r * br) // sr, layer, ((r * br) % sr) // br, c)

            b_specs = [pl.BlockSpec((None, None, br, bc), b_map)]
    elif mode == "nt":
        b_specs = [pl.BlockSpec((tn, tk), lambda i, j, k: (j, k))]
    else:
        b_specs = [pl.BlockSpec((tk, tn), lambda i, j, k: (k, j))]
    nb = len(b_arrs)
    dims = _DIMS[mode]
    if out_chip_major:
        ns = n // N_CHIPS
        o_spec = pl.BlockSpec((None, tm, tn), lambda i, j, k: ((j * tn) // ns, i, ((j * tn) % ns) // tn))
        o_shape = jax.ShapeDtypeStruct((N_CHIPS, m, ns), out_dtype)
    else:
        o_spec = pl.BlockSpec((tm, tn), lambda i, j, k: (i, j))
        o_shape = jax.ShapeDtypeStruct((m, n), out_dtype)
    extra = list(extra) + ([add] if add is not None else [])
    if add is not None:
        post = lambda v, x: (v + x,)
    vecs, slots = list(vecs), list(slots)
    nx = len(extra) + len(vecs) + len(slots)
    out_dtypes = out_dtypes or [out_dtype]
    no = len(out_dtypes)

    def body(a_ref, *rest):
        b_refs, rest = rest[:nb], rest[nb:]
        o_refs, acc = rest[nx:nx + no], rest[nx + no:]
        if across:
            w = kc // N_CHIPS
            p = functools.reduce(jnp.add, [
                lax.dot_general(a_ref[:, s * w:(s + 1) * w].astype(BF16), b_refs[s][...].astype(BF16), dims, preferred_element_type=F32)
                for s in range(N_CHIPS)])
        else:
            p = lax.dot_general(a_ref[...].astype(BF16), b_refs[0][...].astype(BF16), dims, preferred_element_type=F32)

        def emit(v):
            res = post(v, *[r[...] for r in rest[:nx]]) if post else (v,)
            for o_ref, r in zip(o_refs, res):
                o_ref[...] = r.astype(o_ref.dtype)

        if nk == 1:
            emit(p)
        else:
            k = pl.program_id(2)

            @pl.when(k == 0)
            def _():
                acc[0][...] = p

            @pl.when(k > 0)
            def _():
                acc[0][...] += p

            @pl.when(k == nk - 1)
            def _():
                emit(acc[0][...])

    res = pl.pallas_call(
        body, name=name, grid=(m // tm, n // tn, nk),
        in_specs=[a_spec] + b_specs + [o_spec] * len(extra) + [pl.BlockSpec((1, tn), lambda i, j, k: (0, j))] * len(vecs)
        + [pl.BlockSpec((tm, LANES), lambda i, j, k: (i, 0))] * len(slots),
        out_specs=[o_spec] * no,
        out_shape=[jax.ShapeDtypeStruct(o_shape.shape, dt) for dt in out_dtypes],
        scratch_shapes=[pltpu.VMEM((tm, tn), F32)] if nk > 1 else [],
        compiler_params=_params(("parallel", "parallel", "arbitrary")),
    )(a, *b_arrs, *extra, *vecs, *slots)
    return res[0] if no == 1 else res


def _rowarg(r):
    return r if isinstance(r, tuple) else (r, r.shape[1], 0)


def _rowspec(r, tr, ncol):
    _, w, cb = r
    if ncol > 1:
        return pl.BlockSpec((tr, w // ncol), lambda j, i: (i, j))
    return pl.BlockSpec((tr, w), lambda j, i: (i, cb))


def _rowwise(name, f, rows, params, outs, tr=None, ncol=1):
    rows = [_rowarg(r) for r in rows]
    t = rows[0][0].shape[0]
    tr = tr or _tile(t, 528)
    nr, npm = len(rows), len(params)

    def body(*refs):
        vals = [r[...] for r in refs[:nr]] + [(p[0] if ncol > 1 else p[...]) for p in refs[nr:nr + npm]]
        res = f(pl.program_id(1) * tr, *vals)
        for o_ref, v in zip(refs[nr + npm:], res):
            o_ref[...] = v.astype(o_ref.dtype)

    def pspec(p):
        if ncol > 1:
            return pl.BlockSpec((1,) + p.shape[1:], lambda j, i, n=p.ndim: (j,) + (0,) * (n - 1))
        return _full_spec(p.shape, 2)

    return pl.pallas_call(
        body, name=name, grid=(ncol, t // tr),
        in_specs=[_rowspec(r, tr, ncol) for r in rows] + [pspec(p) for p in params],
        out_specs=[pl.BlockSpec((tr, w // ncol), lambda j, i: (i, j)) for w, _ in outs],
        out_shape=[jax.ShapeDtypeStruct((t, w), dt) for w, dt in outs],
        compiler_params=_params(("parallel", "parallel")),
    )(*[r[0] for r in rows], *params)


def _rowwise_vjp(name, f, rows, params, cts, tr=None, ncol=1, row_dtypes=None):
    rows = [_rowarg(r) for r in rows]
    cts = [_rowarg(c) for c in cts]
    t = rows[0][0].shape[0]
    tr = tr or _tile(t, 528)
    nr, npm, nc = len(rows), len(params), len(cts)
    row_dtypes = row_dtypes or [F32] * nr

    def body(*refs):
        i = pl.program_id(1)
        vals = [r[...] for r in refs[:nr]] + [(p[0] if ncol > 1 else p[...]) for p in refs[nr:nr + npm]]
        ct = tuple(c[...].astype(F32) for c in refs[nr + npm:nr + npm + nc])
        _, vjp = jax.vjp(lambda *a: tuple(f(i * tr, *a)), *vals)
        g = vjp(ct)
        outs = refs[nr + npm + nc:]
        for o_ref, v in zip(outs[:nr], g[:nr]):
            o_ref[...] = v.astype(o_ref.dtype)
        pg = [(v[None] if ncol > 1 else v) for v in g[nr:]]

        @pl.when(i == 0)
        def _():
            for o_ref, v in zip(outs[nr:], pg):
                o_ref[...] = v

        @pl.when(i > 0)
        def _():
            for o_ref, v in zip(outs[nr:], pg):
                o_ref[...] += v

    def pspec(p):
        if ncol > 1:
            return pl.BlockSpec((1,) + p.shape[1:], lambda j, i, n=p.ndim: (j,) + (0,) * (n - 1))
        return _full_spec(p.shape, 2)

    res = pl.pallas_call(
        body, name=name, grid=(ncol, t // tr),
        in_specs=[_rowspec(r, tr, ncol) for r in rows] + [pspec(p) for p in params] + [_rowspec(c, tr, ncol) for c in cts],
        out_specs=[pl.BlockSpec((tr, w // ncol), lambda j, i: (i, j)) for _, w, _ in rows] + [pspec(p) for p in params],
        out_shape=[jax.ShapeDtypeStruct((t, w), dt) for (_, w, _), dt in zip(rows, row_dtypes)]
        + [jax.ShapeDtypeStruct(p.shape, F32) for p in params],
        compiler_params=_params(("parallel", "arbitrary")),
    )(*[r[0] for r in rows], *params, *[c[0] for c in cts])
    return res[:nr], res[nr:]


def _valid(row0, tr):
    return (row0 + lax.broadcasted_iota(jnp.int32, (tr, 1), 0)) >= PAD


def _rms(x, g):
    return x * lax.rsqrt(jnp.mean(x * x, axis=-1, keepdims=True) + EPS) * g


def _softplus(x):
    return jnp.where(x < -15.0, jnp.exp(x), jnp.maximum(x, 0.0) + jnp.log(1.0 + jnp.exp(-jnp.abs(x))))


def _neg_expm1(z):
    return jnp.where(z > -0.01, -z * (1.0 + z * (0.5 + z * (1.0 / 6.0))), 1.0 - jnp.exp(z))


def _prenorm(h, g, name):
    return _rowwise(name, lambda r0, x, gg: (_rms(x, gg),), [h], [g], [(_rowarg(h)[1], BF16)])[0]


def _post_residual(m, h, g):
    assert m.shape[1] == D_MODEL
    return m, h + _rms(m, g)


def _postnorm_bwd(m, g, dh, name):
    (dm,), (dg,) = _rowwise_vjp(name, lambda r0, mm, gg: (_rms(mm, gg),), [m], [g], [dh], row_dtypes=[BF16])
    return dm, dg


def _prenorm_bwd_add(h, g, dhns, dh, name):
    t, w = h.shape
    tr = _tile(t, 528)
    nd = len(dhns)

    def body(h_ref, g_ref, *refs):
        dh_ref, o_ref, dg_ref = refs[nd:]
        i = pl.program_id(0)
        _, vjp = jax.vjp(_rms, h_ref[...], g_ref[...])
        dhn = refs[0][...].astype(F32)
        for r in refs[1:nd]:
            dhn = dhn + r[...].astype(F32)
        dx, dg = vjp(dhn)
        o_ref[...] = dh_ref[...] + dx

        @pl.when(i == 0)
        def _():
            dg_ref[...] = dg

        @pl.when(i > 0)
        def _():
            dg_ref[...] += dg

    row = pl.BlockSpec((tr, w), lambda i: (i, 0))
    return pl.pallas_call(
        body, name=name, grid=(t // tr,), in_specs=[row, _full_spec(g.shape, 1)] + [row] * (nd + 1),
        out_specs=[row, _full_spec(g.shape, 1)],
        out_shape=[jax.ShapeDtypeStruct((t, w), F32), jax.ShapeDtypeStruct(g.shape, F32)],
        compiler_params=_params(("arbitrary",)),
    )(h, g, *dhns, dh)


def _loss_and_grad(h, target, name):
    t, w = h.shape
    nb = t // CHUNK

    def body(h_ref, t_ref, s_ref, dh_ref):
        i = pl.program_id(0)

        @pl.when(i == 0)
        def _():
            s_ref[...] = jnp.zeros_like(s_ref)
            dh_ref[...] = jnp.zeros_like(dh_ref)

        @pl.when(i > 0)
        def _():
            err = h_ref[...] - t_ref[...]
            s_ref[...] += jnp.sum(err * err)
            dh_ref[...] = err * (1.0 / w)

    return pl.pallas_call(
        body, name=name, grid=(nb,),
        in_specs=[pl.BlockSpec((CHUNK, w), lambda i: (i, 0)), pl.BlockSpec((CHUNK, w), lambda i: (jnp.maximum(i - 1, 0), 0))],
        out_specs=[_full_spec((1, LANES), 1), pl.BlockSpec((CHUNK, w), lambda i: (i, 0))],
        out_shape=[jax.ShapeDtypeStruct((1, LANES), F32), jax.ShapeDtypeStruct((t, w), F32)],
        compiler_params=_params(("arbitrary",)),
    )(h, target)


def _mlp_fwd(h, p, l):
    hn = _prenorm(h, p["mlp_pre_g"][l], "mlp_prenorm")
    a, u = _mm(hn, p["w_up"][l], "nn", "mlp_up", post=lambda v: (v, jnp.square(jnp.maximum(v, 0.0))), out_dtypes=[BF16, BF16])
    d, h2 = _mm(u, p["w_down"][l], "nn", "mlp_down", extra=[h], vecs=[p["mlp_post_g"][l]], post=_post_residual, out_dtypes=[F32, F32])
    return h2, (h, hn, a, u, d)


def _mlp_bwd(dh, saved, p, l, grads):
    h, hn, a, u, d = saved
    dd, grads["mlp_post_g"][l] = _postnorm_bwd(d, p["mlp_post_g"][l], dh, "mlp_postnorm_bwd")
    grads["w_down"][l] = _mm(u, dd, "tn", "mlp_down_dw")
    da = _mm(dd, p["w_down"][l], "nt", "mlp_down_dx", extra=[a], post=lambda v, x: (2.0 * jnp.maximum(x.astype(F32), 0.0) * v,),
             out_dtypes=[BF16])
    grads["w_up"][l] = _mm(hn, da, "tn", "mlp_up_dw", out_chip_major=True)
    dhn = _mm(da, p["w_up"][l], "nt", "mlp_up_dx")
    dh, grads["mlp_pre_g"][l] = _prenorm_bwd_add(h, p["mlp_pre_g"][l], [dhn], dh, "mlp_prenorm_bwd")
    return dh


def _dot(a, b, mode):
    return lax.dot_general(a.astype(BF16), b.astype(BF16), _DIMS[mode], preferred_element_type=F32)


@jax.custom_vjp
def _bnn(a, b):
    return _dot(a, b, "nn")


_bnn.defvjp(lambda a, b: (_dot(a, b, "nn"), (a, b)), lambda r, ct: (_dot(ct, r[1], "nt"), _dot(r[0], ct, "tn")))


@jax.custom_vjp
def _bnt(a, b):
    return _dot(a, b, "nt")


_bnt.defvjp(lambda a, b: (_dot(a, b, "nt"), (a, b)), lambda r, ct: (_dot(ct, r[1], "nn"), _dot(ct, r[0], "tn")))


@jax.custom_vjp
def _btn(a, b):
    return _dot(a, b, "tn")


_btn.defvjp(lambda a, b: (_dot(a, b, "tn"), (a, b)), lambda r, ct: (_dot(r[1], ct, "nt"), _dot(r[0], ct, "nn")))


CONV_K = 4
HALO = 8


def _conv_fwd(x, w, b, name, cw, c0=0):
    t, c = x.shape[0], w.shape[1]
    tr = _tile(t, 528)
    hb = tr // HALO

    def body(x_ref, halo_ref, w_ref, b_ref, o_ref, ext):
        i = pl.program_id(1)
        ext[pl.ds(0, HALO), :] = jnp.where(i > 0, halo_ref[...], 0.0)
        ext[pl.ds(HALO, tr), :] = x_ref[...]
        acc = jnp.broadcast_to(b_ref[...], (tr, cw))
        for k in range(CONV_K):
            acc = acc + w_ref[pl.ds(k, 1), :] * ext[pl.ds(HALO - (CONV_K - 1) + k, tr), :]
        o_ref[...] = acc

    return pl.pallas_call(
        body, name=name, grid=(c // cw, t // tr),
        in_specs=[pl.BlockSpec((tr, cw), lambda j, i: (i, c0 + j)),
                  pl.BlockSpec((HALO, cw), lambda j, i: (jnp.maximum(i * hb - 1, 0), c0 + j)),
                  pl.BlockSpec((CONV_K, cw), lambda j, i: (0, j)), pl.BlockSpec((1, cw), lambda j, i: (0, j))],
        out_specs=pl.BlockSpec((tr, cw), lambda j, i: (i, j)),
        out_shape=jax.ShapeDtypeStruct((t, c), F32),
        scratch_shapes=[pltpu.VMEM((tr + HALO, cw), F32)],
        compiler_params=_params(("parallel", "parallel")),
    )(x, x, w, b)


def _conv_bwd(x, w, dy, name, cw, c0=0):
    t, c = x.shape[0], w.shape[1]
    tr = _tile(t, 528)
    hb = tr // HALO
    nb = t // tr

    def body(x_ref, xh_ref, w_ref, dy_ref, dyh_ref, dx_ref, dw_ref, db_ref, xe, de):
        c = cw
        i = pl.program_id(1)
        xe[pl.ds(0, HALO), :] = jnp.where(i > 0, xh_ref[...], 0.0)
        xe[pl.ds(HALO, tr), :] = x_ref[...]
        de[pl.ds(0, tr), :] = dy_ref[...]
        de[pl.ds(tr, HALO), :] = jnp.where(i < nb - 1, dyh_ref[...], 0.0)
        dy = dy_ref[...]
        acc = jnp.zeros((tr, c), F32)
        dw = jnp.zeros((CONV_K, c), F32)
        rows = lax.broadcasted_iota(jnp.int32, (CONV_K, 1), 0)
        for k in range(CONV_K):
            acc = acc + w_ref[pl.ds(k, 1), :] * de[pl.ds(CONV_K - 1 - k, tr), :]
            dwk = jnp.sum(dy * xe[pl.ds(HALO - (CONV_K - 1) + k, tr), :], axis=0, keepdims=True)
            dw = dw + jnp.where(rows == k, dwk, 0.0)
        dx_ref[...] = jnp.where(_valid(i * tr, tr), acc, 0.0).astype(dx_ref.dtype)
        db = jnp.sum(dy, axis=0, keepdims=True)

        @pl.when(i == 0)
        def _():
            dw_ref[...] = dw
            db_ref[...] = db

        @pl.when(i > 0)
        def _():
            dw_ref[...] += dw
            db_ref[...] += db

    row = pl.BlockSpec((tr, cw), lambda j, i: (i, j))
    return pl.pallas_call(
        body, name=name, grid=(c // cw, nb),
        in_specs=[pl.BlockSpec((tr, cw), lambda j, i: (i, c0 + j)),
                  pl.BlockSpec((HALO, cw), lambda j, i: (jnp.maximum(i * hb - 1, 0), c0 + j)),
                  pl.BlockSpec((CONV_K, cw), lambda j, i: (0, j)),
                  row, pl.BlockSpec((HALO, cw), lambda j, i: (jnp.minimum((i + 1) * hb, t // HALO - 1), j))],
        out_specs=[row, pl.BlockSpec((CONV_K, cw), lambda j, i: (0, j)), pl.BlockSpec((1, cw), lambda j, i: (0, j))],
        out_shape=[jax.ShapeDtypeStruct((t, c), BF16), jax.ShapeDtypeStruct((CONV_K, c), F32), jax.ShapeDtypeStruct((1, c), F32)],
        scratch_shapes=[pltpu.VMEM((tr + HALO, cw), F32), pltpu.VMEM((tr + HALO, cw), F32)],
        compiler_params=_params(("parallel", "arbitrary")),
    )(x, x, w, dy, dy)


SUB = 8


def _lru_scan(a, u, name):
    t, c = a.shape
    tr = _tile(t, 528)

    def body(a_ref, u_ref, o_ref, carry):
        @pl.when(pl.program_id(0) == 0)
        def _():
            carry[...] = jnp.zeros_like(carry)

        rows = lax.broadcasted_iota(jnp.int32, (SUB, 1), 0)

        def step(k, cin):
            r = pl.multiple_of(k * SUB, SUB)
            av, uv = a_ref[pl.ds(r, SUB), :], u_ref[pl.ds(r, SUB), :]
            for d in (1, 2, 4):
                m = rows >= d
                uv = uv + av * jnp.where(m, pltpu.roll(uv, d, 0), 0.0)
                av = av * jnp.where(m, pltpu.roll(av, d, 0), 1.0)
            hv = uv + av * cin
            o_ref[pl.ds(r, SUB), :] = hv
            return jnp.broadcast_to(hv[SUB - 1:SUB, :], (SUB, c))

        carry[...] = lax.fori_loop(0, tr // SUB, step, carry[...])

    row = pl.BlockSpec((tr, c), lambda i: (i, 0))
    return pl.pallas_call(
        body, name=name, grid=(t // tr,), in_specs=[row, row], out_specs=row,
        out_shape=jax.ShapeDtypeStruct((t, c), F32), scratch_shapes=[pltpu.VMEM((SUB, c), F32)],
        compiler_params=_params(("arbitrary",)),
    )(a, u)


def _lru_scan_bwd(a, hs, dy, name):
    t, c = a.shape
    tr = _tile(t, 528)
    nb, nt = t // tr, tr // SUB

    def body(a_ref, h_ref, hh_ref, dy_ref, du_ref, da_ref, gcar, acar):
        i = pl.program_id(0)

        @pl.when(i == 0)
        def _():
            gcar[...] = jnp.zeros_like(gcar)
            acar[...] = jnp.zeros_like(acar)

        rows = lax.broadcasted_iota(jnp.int32, (SUB, 1), 0)
        hhalo = jnp.where(i < nb - 1, hh_ref[...], 0.0)

        def step(kk, car):
            gin, a_next_first = car
            k = nt - 1 - kk
            r = pl.multiple_of(k * SUB, SUB)
            av, hv, dv = a_ref[pl.ds(r, SUB), :], h_ref[pl.ds(r, SUB), :], dy_ref[pl.ds(r, SUB), :]
            rp = pl.multiple_of(jnp.maximum(k - 1, 0) * SUB, SUB)
            hp = jnp.where(k > 0, h_ref[pl.ds(rp, SUB), :], hhalo)
            cv = jnp.where(rows < SUB - 1, pltpu.roll(av, SUB - 1, 0), a_next_first)
            gv = dv
            for d in (1, 2, 4):
                m = rows < SUB - d
                gv = gv + cv * jnp.where(m, pltpu.roll(gv, SUB - d, 0), 0.0)
                cv = cv * jnp.where(m, pltpu.roll(cv, SUB - d, 0), 1.0)
            gv = gv + cv * gin
            hprev = jnp.where(rows >= 1, pltpu.roll(hv, 1, 0), jnp.broadcast_to(hp[SUB - 1:SUB, :], (SUB, c)))
            du_ref[pl.ds(r, SUB), :] = gv
            da_ref[pl.ds(r, SUB), :] = gv * hprev
            return jnp.broadcast_to(gv[0:1, :], (SUB, c)), jnp.broadcast_to(av[0:1, :], (SUB, c))

        g, af = lax.fori_loop(0, nt, step, (gcar[...], acar[...]))
        gcar[...] = g
        acar[...] = af

    hb = tr // SUB
    row = pl.BlockSpec((tr, c), lambda i: (nb - 1 - i, 0))
    halo = pl.BlockSpec((SUB, c), lambda i: (jnp.maximum((nb - 1 - i) * hb - 1, 0), 0))
    return pl.pallas_call(
        body, name=name, grid=(nb,), in_specs=[row, row, halo, row], out_specs=[row, row],
        out_shape=[jax.ShapeDtypeStruct((t, c), F32)] * 2,
        scratch_shapes=[pltpu.VMEM((SUB, c), F32), pltpu.VMEM((SUB, c), F32)],
        compiler_params=_params(("arbitrary",)),
    )(a, hs, hs, dy)


def _lru_gates(row0, xr, wa, ba, wi, bi, lam):
    r = jax.nn.sigmoid(_bnn(xr, wa) + ba)
    i = jax.nn.sigmoid(_bnn(xr, wi) + bi)
    log_a = -LRU_C * r * _softplus(-lam)
    u = jnp.sqrt(_neg_expm1(2.0 * log_a)) * (i * xr)
    return jnp.exp(log_a), jnp.where(_valid(row0, xr.shape[0]), u, 0.0)


def _lru_gate_out(row0, hs, yw):
    return (hs * jax.nn.gelu(yw),)


def _rglru_fwd(h, p, l, o):
    hn = _prenorm(h, p["mix_pre_g"][l], "rg_prenorm")
    xw = _mm(hn, p["rg_w_x"][o], "nn", "rg_in_x")
    yw = _mm(hn, p["rg_w_y"][o], "nn", "rg_in_y")
    xr = _conv_fwd(xw, p["rg_conv_w"][o], p["rg_conv_b"][o], "rg_conv", cw=LRU_WIDTH // 2)
    gp = [p["rg_w_a"][o], p["rg_b_a"][o], p["rg_w_i"][o], p["rg_b_i"][o], p["rg_lambda"][o]]
    a, u = _rowwise("rg_gates", _lru_gates, [xr], gp, [(LRU_WIDTH, F32)] * 2, ncol=LRU_BLOCKS, tr=_tile(h.shape[0], 1056))
    hs = _lru_scan(a, u, "rg_scan")
    hg = _rowwise("rg_gate_out", _lru_gate_out, [hs, yw], [], [(LRU_WIDTH, BF16)])[0]
    m, h2 = _mm(hg, p["rg_w_out"][o], "nn", "rg_out", extra=[h], vecs=[p["mix_post_g"][l]], post=_post_residual, out_dtypes=[F32, F32])
    return h2, (h, hn, xw, yw, xr, a, hs, hg, m)


def _rglru_bwd(dh, saved, p, l, o, grads):
    h, hn, xw, yw, xr, a, hs, hg, m = saved
    dm, grads["mix_post_g"][l] = _postnorm_bwd(m, p["mix_post_g"][l], dh, "rg_postnorm_bwd")
    grads["rg_w_out"][o] = _mm(hg, dm, "tn", "rg_out_dw")
    dhg = _mm(dm, p["rg_w_out"][o], "nt", "rg_out_dx")
    (dhs, dyw), _ = _rowwise_vjp("rg_gate_out_bwd", _lru_gate_out, [hs, yw], [], [dhg], row_dtypes=[F32, BF16])
    du, da = _lru_scan_bwd(a, hs, dhs, "rg_scan_bwd")
    gp = [p["rg_w_a"][o], p["rg_b_a"][o], p["rg_w_i"][o], p["rg_b_i"][o], p["rg_lambda"][o]]
    (dxr,), gg = _rowwise_vjp("rg_gates_bwd", _lru_gates, [xr], gp, [da, du], ncol=LRU_BLOCKS, tr=_tile(h.shape[0], 1056))
    grads["rg_w_a"][o], grads["rg_b_a"][o], grads["rg_w_i"][o], grads["rg_b_i"][o], grads["rg_lambda"][o] = gg
    dxw, grads["rg_conv_w"][o], grads["rg_conv_b"][o] = _conv_bwd(xw, p["rg_conv_w"][o], dxr, "rg_conv_bwd", cw=LRU_WIDTH // 2)
    grads["rg_w_x"][o] = _mm(hn, dxw, "tn", "rg_in_x_dw")
    grads["rg_w_y"][o] = _mm(hn, dyw, "tn", "rg_in_y_dw")
    dhx = _mm(dxw, p["rg_w_x"][o], "nt", "rg_in_x_dx")
    dhy = _mm(dyw, p["rg_w_y"][o], "nt", "rg_in_y_dx")
    dh, grads["mix_pre_g"][l] = _prenorm_bwd_add(h, p["mix_pre_g"][l], [dhx, dhy], dh, "rg_prenorm_bwd")
    return dh


SSD_GW = SSD_D_INNER // SSD_GROUPS
SSD_GH = SSD_HEADS // SSD_GROUPS
XACT_B = SSD_D_INNER // SSD_STATE
XACT_C = XACT_B + SSD_GROUPS


def _hp(a, b, dims=_DIMS["nn"]):
    return lax.dot_general(a, b, dims, precision=lax.Precision.HIGHEST, preferred_element_type=F32)


def _split_dot(a, e, mode, parts):
    eb = e.astype(BF16)
    out, rest = None, a
    for _ in range(parts):
        term = rest.astype(BF16)
        rest = rest - term.astype(F32)
        if mode in ("nn", "nt"):
            prod = lax.dot_general(term, eb, _DIMS[mode], preferred_element_type=F32)
        else:
            prod = lax.dot_general(eb, term, _DIMS["nn" if mode == "left" else "tn"], preferred_element_type=F32)
        out = prod if out is None else out + prod
    return out


@jax.custom_vjp
def _select_nn(a, e):
    return _split_dot(a, e, "nn", 3)


_select_nn.defvjp(lambda a, e: (_split_dot(a, e, "nn", 3), e), lambda e, ct: (_split_dot(ct, e, "nt", 2), jnp.zeros_like(e)))


@jax.custom_vjp
def _select_left(e, a):
    return _split_dot(a, e, "left", 3)


_select_left.defvjp(lambda e, a: (_split_dot(a, e, "left", 3), e),
                    lambda e, ct: (jnp.zeros_like(e), _split_dot(ct, e, "left_t", 2)))


def _ssd_chunk(xs, bm, cm, dt, da, ht, g):
    l = CHUNK
    ri = lax.broadcasted_iota(jnp.int32, (l, l), 0)
    ci = lax.broadcasted_iota(jnp.int32, (l, l), 1)
    causal = ri >= ci
    tri = causal.astype(F32)
    hr = lax.broadcasted_iota(jnp.int32, (LANES, SSD_GW), 0)
    hc = lax.broadcasted_iota(jnp.int32, (LANES, SSD_GW), 1)
    expand = (hr == g * SSD_GH + hc // SSD_HEAD_DIM).astype(F32)
    acs = _select_left(tri, da)
    acs_t = acs.T
    acs_e = _select_nn(acs, expand)
    x = xs * _select_nn(dt, expand)
    gmat = _bnt(cm, bm)
    lane = lax.broadcasted_iota(jnp.int32, (1, LANES), 1)
    sub = lax.broadcasted_iota(jnp.int32, (LANES, 1), 0)
    colhead = lax.broadcasted_iota(jnp.int32, (1, SSD_GW), 1) // SSD_HEAD_DIM
    y = _bnn(cm, ht) * jnp.exp(acs_e)
    for k in range(SSD_GH):
        hh = g * SSD_GH + k
        col = jnp.sum(jnp.where(lane == hh, acs, 0.0), axis=1, keepdims=True)
        row = jnp.sum(jnp.where(sub == hh, acs_t, 0.0), axis=0, keepdims=True)
        decay = jnp.exp(jnp.where(causal, col - row, -1e30))
        y = y + _bnn(gmat * decay, jnp.where(colhead == k, x, 0.0))
    last = lax.broadcasted_iota(jnp.int32, (l, 1), 0) == l - 1
    a_last = jnp.sum(jnp.where(last, acs_e, 0.0), axis=0, keepdims=True)
    st = _btn(bm, x * jnp.exp(a_last - acs_e))
    return y, ht * jnp.exp(a_last) + st


def _ssd_specs(nc, rev):
    def cc(c):
        return nc - 1 - c if rev else c

    return [pl.BlockSpec((CHUNK, SSD_GW), lambda c, g: (cc(c), g)),
            pl.BlockSpec((CHUNK, SSD_STATE), lambda c, g: (cc(c), XACT_B + g)),
            pl.BlockSpec((CHUNK, SSD_STATE), lambda c, g: (cc(c), XACT_C + g)),
            pl.BlockSpec((CHUNK, LANES), lambda c, g: (cc(c), 0)),
            pl.BlockSpec((CHUNK, LANES), lambda c, g: (cc(c), 0))]


def _ssd_scan(xact, dt, da, name):
    t = xact.shape[0]
    nc = t // CHUNK

    def body(xs_ref, b_ref, c_ref, dt_ref, da_ref, y_ref, hs_ref, state):
        c, g = pl.program_id(0), pl.program_id(1)

        @pl.when(c == 0)
        def _():
            state[g] = jnp.zeros((SSD_STATE, SSD_GW), F32)

        ht = state[g]
        hs_ref[0] = ht
        y, ht2 = _ssd_chunk(xs_ref[...], b_ref[...], c_ref[...], dt_ref[...], da_ref[...], ht, g)
        y_ref[...] = y
        state[g] = ht2

    return pl.pallas_call(
        body, name=name, grid=(nc, SSD_GROUPS), in_specs=_ssd_specs(nc, False),
        out_specs=[pl.BlockSpec((CHUNK, SSD_GW), lambda c, g: (c, g)),
                   pl.BlockSpec((1, SSD_STATE, SSD_GW), lambda c, g: (c * SSD_GROUPS + g, 0, 0))],
        out_shape=[jax.ShapeDtypeStruct((t, SSD_D_INNER), F32), jax.ShapeDtypeStruct((nc * SSD_GROUPS, SSD_STATE, SSD_GW), F32)],
        scratch_shapes=[pltpu.VMEM((SSD_GROUPS, SSD_STATE, SSD_GW), F32)],
        compiler_params=_params(("arbitrary", "arbitrary")),
    )(xact, xact, xact, dt, da)


def _ssd_scan_bwd(xact, dt, da, hsave, dy, dxskip, name):
    t = xact.shape[0]
    nc = t // CHUNK

    def body(xs_ref, b_ref, c_ref, dt_ref, da_ref, hs_ref, dy_ref, sk_ref, dxs_ref, db_ref, dc_ref, ddt_ref, dda_ref, dstate):
        c, g = pl.program_id(0), pl.program_id(1)

        @pl.when(c == 0)
        def _():
            dstate[g] = jnp.zeros((SSD_STATE, SSD_GW), F32)

        _, vjp = jax.vjp(lambda *a: _ssd_chunk(*a, g), xs_ref[...], b_ref[...], c_ref[...], dt_ref[...], da_ref[...], hs_ref[0])
        dxs, dbm, dcm, ddt, dda, dht = vjp((dy_ref[...], dstate[g]))
        dxs_ref[...] = dxs + sk_ref[...]
        db_ref[...] = dbm
        dc_ref[...] = dcm
        dstate[g] = dht

        @pl.when(g == 0)
        def _():
            ddt_ref[...] = ddt
            dda_ref[...] = dda

        @pl.when(g > 0)
        def _():
            ddt_ref[...] += ddt
            dda_ref[...] += dda

    grp = pl.BlockSpec((CHUNK, SSD_GW), lambda c, g: (nc - 1 - c, g))
    st = pl.BlockSpec((CHUNK, SSD_STATE), lambda c, g: (nc - 1 - c, g))
    hd = pl.BlockSpec((CHUNK, LANES), lambda c, g: (nc - 1 - c, 0))
    return pl.pallas_call(
        body, name=name, grid=(nc, SSD_GROUPS),
        in_specs=_ssd_specs(nc, True) + [pl.BlockSpec((1, SSD_STATE, SSD_GW), lambda c, g: ((nc - 1 - c) * SSD_GROUPS + g, 0, 0)), grp, grp],
        out_specs=[grp, st, st, hd, hd],
        out_shape=[jax.ShapeDtypeStruct((t, SSD_D_INNER), F32), jax.ShapeDtypeStruct((t, SSD_GROUPS * SSD_STATE), F32),
                   jax.ShapeDtypeStruct((t, SSD_GROUPS * SSD_STATE), F32), jax.ShapeDtypeStruct((t, LANES), F32),
                   jax.ShapeDtypeStruct((t, LANES), F32)],
        scratch_shapes=[pltpu.VMEM((SSD_GROUPS, SSD_STATE, SSD_GW), F32)],
        compiler_params=_params(("arbitrary", "arbitrary")),
    )(xact, xact, xact, dt, da, hsave, dy, dxskip)


def _ssd_act(row0, xc):
    return (jnp.where(_valid(row0, xc.shape[0]), jax.nn.silu(xc), 0.0),)


def _ssd_dt(row0, dtraw, dt_bias, a_log):
    dt = jnp.where(_valid(row0, dtraw.shape[0]), _softplus(dtraw + dt_bias), 0.0)
    return dt, dt * -jnp.exp(a_log)


def _ssd_post(row0, y, xs, z, d_skip, norm_g):
    hr = lax.broadcasted_iota(jnp.int32, (LANES, SSD_D_INNER), 0)
    hc = lax.broadcasted_iota(jnp.int32, (LANES, SSD_D_INNER), 1)
    expand = (hr == hc // SSD_HEAD_DIM).astype(F32)
    d_e = jnp.sum(_hp(jnp.broadcast_to(d_skip, (SUB, LANES)), expand), axis=0, keepdims=True) * (1.0 / SUB)
    return (_rms((y + xs * d_e) * jax.nn.silu(z), norm_g),)


ROPE_LO, ROPE_MID, ROPE_HI = MLA_NOPE, MLA_NOPE + MLA_ROPE // 2, MLA_NOPE + MLA_ROPE
ATT_SCALE = (MLA_NOPE + MLA_ROPE) ** -0.5


def _slot_lane(width):
    return lax.broadcasted_iota(jnp.int32, (1, width), 1) % LANES


def _swap_halves(x):
    width = x.shape[1]
    lane = _slot_lane(width)
    sw = jnp.where(lane < ROPE_MID, pltpu.roll(x, width - MLA_ROPE // 2, 1), pltpu.roll(x, MLA_ROPE // 2, 1))
    return jnp.where((lane >= ROPE_LO) & (lane < ROPE_HI), sw, 0.0)


def _rope(x, cos, sin):
    n = x.shape[1] // LANES
    return x * jnp.tile(cos, (1, n)) + _swap_halves(x) * jnp.tile(sin, (1, n))


def _rope_t(dy, cos, sin):
    n = dy.shape[1] // LANES
    return dy * jnp.tile(cos, (1, n)) + _swap_halves(dy * jnp.tile(sin, (1, n)))


ATT_SCALE2 = ATT_SCALE * float(np.log2(np.e))
MASKED = -1e30


def _att_bias(blk):
    r = jnp.arange(blk)[:, None]
    c = jnp.arange(blk)[None, :]
    zero = jnp.zeros((blk, blk), F32)
    first = jnp.where(c >= PAD, 0.0, MASKED) + zero
    diag = jnp.where(c <= r, 0.0, MASKED).astype(F32)
    return jnp.stack([zero, first, diag, jnp.minimum(first, diag), zero + MASKED])


def _att_bias_index(j, i):
    return jnp.where(j > i, 4, jnp.where(j == 0, 1, 0) + jnp.where(j == i, 2, 0))


def _key_slots(row0, kv, kr):
    width = kv.shape[1]
    return jnp.where(_slot_lane(width) < MLA_NOPE, kv, jnp.tile(kr, (1, width // LANES))), kv


def _attn_fwd(qr, km, vb, name, carried=None):
    t = qr.shape[0]
    blk = _tile(t, 384, LANES)
    nq = t // blk

    bias = _att_bias(blk)

    def body(q_ref, k_ref, v_ref, b_ref, o_ref, s0, s1, p0, p1):
        i = pl.program_id(1)
        lane = lax.broadcasted_iota(jnp.int32, (1, LANES), 1)
        qb = q_ref[...]

        def rows(j):
            return pl.ds(pl.multiple_of(jnp.clip(j, 0, i) * blk, blk), blk)

        def scores(j):
            return lax.dot_general(qb, k_ref[rows(j), :], _DIMS["nt"], preferred_element_type=F32) + b_ref[_att_bias_index(j, i)]

        def half(j, car, s_cur, s_nxt, p_cur, p_prv):
            m, l, acc, al_prev = car
            s_nxt[...] = scores(j + 1)
            acc2 = al_prev * acc + lax.dot_general(p_prv[...], v_ref[rows(j - 1), :], _DIMS["nn"], preferred_element_type=F32)
            m2 = jnp.maximum(m, jnp.max(s_cur[...], axis=1, keepdims=True))
            al = jnp.exp2((m - m2) * ATT_SCALE2)
            pm = jnp.exp2(s_cur[...] * ATT_SCALE2 - m2 * ATT_SCALE2)
            p_cur[...] = pm.astype(BF16)
            return m2, al * l + jnp.sum(pm, axis=1, keepdims=True), acc2, al

        def step(jj, car):
            car = half(2 * jj, car, s0, s1, p0, p1)
            return half(2 * jj + 1, car, s1, s0, p1, p0)

        s0[...] = scores(0)
        p1[...] = jnp.zeros((blk, blk), BF16)
        car = (jnp.full((blk, 1), MASKED, F32), jnp.zeros((blk, 1), F32), jnp.zeros((blk, LANES), F32), jnp.ones((blk, 1), F32))
        steps = i // 2 + 1
        m, l, acc, al_last = lax.fori_loop(0, steps, step, car)
        acc = al_last * acc + lax.dot_general(p1[...], v_ref[rows(2 * steps - 1), :], _DIMS["nn"], preferred_element_type=F32)
        out = jnp.where(lane >= MLA_NOPE, acc / l, m * ATT_SCALE + jnp.log(l))
        o_ref[...] = jnp.where(_valid(i * blk, blk), out, 0.0)

    seq_h = pl.BlockSpec((t, LANES), lambda h, i: (0, h))
    (o,), carried_out = _carry_call(
        body, name, (MLA_HEADS, nq),
        [pl.BlockSpec((blk, LANES), lambda h, i: (i, h)), seq_h, seq_h, _full_spec(bias.shape, 2)],
        [pl.BlockSpec((blk, LANES), lambda h, i: (i, h))], [jax.ShapeDtypeStruct((t, MLA_HEADS * LANES), F32)],
        [pltpu.VMEM((blk, blk), F32)] * 2 + [pltpu.VMEM((blk, blk), BF16)] * 2, (qr, km, vb, bias), carried)
    return o, carried_out


def _attn_bwd(qr, km, vb, o, do, cos, sin, name, carried=None):
    t = qr.shape[0]
    blk = _tile(t, 384, LANES)
    nq = t // blk

    bias = _att_bias(blk)
    log2e = float(np.log2(np.e))

    def body(q_ref, o_ref, do_ref, k_ref, v_ref, b_ref, cos_ref, sin_ref, dq_out, dkv_ref, dkr_ref,
             s0, s1, dp0, dp1, p0, p1, ds0, ds1, dk_s, dv_s, dq_ref):
        h, j = pl.program_id(0), pl.program_id(1)
        lane = lax.broadcasted_iota(jnp.int32, (1, LANES), 1)

        @pl.when(j == 0)
        def _():
            dq_ref[...] = jnp.zeros_like(dq_ref)

        @pl.when((h == 0) & (j == 0))
        def _():
            dkr_ref[...] = jnp.zeros_like(dkr_ref)

        kmat, vmat = k_ref[...], v_ref[...]

        def rows(i):
            return pl.ds(pl.multiple_of(jnp.clip(i, j, nq - 1) * blk, blk), blk)

        def first_stage(i, s_buf, dp_buf):
            ic = jnp.minimum(i, nq - 1)
            s_buf[...] = lax.dot_general(q_ref[rows(ic), :], kmat, _DIMS["nt"], preferred_element_type=F32) + b_ref[_att_bias_index(j, ic)]
            dp_buf[...] = lax.dot_general(do_ref[rows(ic), :].astype(BF16), vmat, _DIMS["nt"], preferred_element_type=F32)

        def middle_stage(i, s_buf, dp_buf, p_buf, ds_buf):
            r = rows(i)
            ob, dob = o_ref[r, :], do_ref[r, :].astype(F32)
            delta = jnp.sum(dob * ob, axis=1, keepdims=True)
            pm = jnp.exp2(s_buf[...] * ATT_SCALE2 - ob[:, 0:1] * log2e)
            p_buf[...] = pm.astype(BF16)
            ds_buf[...] = (pm * (dp_buf[...] - delta) * ATT_SCALE).astype(BF16)

        def last_stage(i, p_buf, ds_buf):
            r = rows(i)
            dv_s[...] += lax.dot_general(p_buf[...], do_ref[r, :].astype(BF16), _DIMS["tn"], preferred_element_type=F32)
            dk_s[...] += lax.dot_general(ds_buf[...], q_ref[r, :], _DIMS["tn"], preferred_element_type=F32)
            dq_ref[r, :] += lax.dot_general(ds_buf[...], kmat, _DIMS["nn"], preferred_element_type=F32)

        n = nq - j
        dk_s[...] = jnp.zeros((blk, LANES), F32)
        dv_s[...] = jnp.zeros((blk, LANES), F32)
        first_stage(j, s0, dp0)
        first_stage(j + 1, s1, dp1)
        middle_stage(j, s0, dp0, p0, ds0)

        def step(tt, carry):
            i = j + 2 * tt + 1
            first_stage(i + 1, s0, dp0)
            last_stage(i - 1, p0, ds0)
            middle_stage(i, s1, dp1, p1, ds1)
            first_stage(i + 2, s1, dp1)
            last_stage(i, p1, ds1)
            middle_stage(i + 1, s0, dp0, p0, ds0)
            return carry

        lax.fori_loop(0, (n - 1) // 2, step, 0)

        @pl.when(n % 2 == 0)
        def _():
            last_stage(nq - 2, p0, ds0)
            middle_stage(nq - 1, s1, dp1, p1, ds1)
            last_stage(nq - 1, p1, ds1)

        @pl.when(n % 2 == 1)
        def _():
            last_stage(nq - 1, p0, ds0)

        dk = dk_s[...]
        dkv_ref[...] = jnp.where(lane < MLA_NOPE, dk, dv_s[...]).astype(dkv_ref.dtype)
        dkr_ref[rows(j), :] += jnp.where(lane >= MLA_NOPE, dk, 0.0)

        @pl.when(j == nq - 1)
        def _():
            dq_out[...] = _rope_t(dq_ref[...], cos_ref[...], sin_ref[...]).astype(dq_out.dtype)

    seq_h = pl.BlockSpec((t, LANES), lambda h, j: (0, h))
    seq = pl.BlockSpec((t, LANES), lambda h, j: (0, 0))
    blk_h = pl.BlockSpec((blk, LANES), lambda h, j: (j, h))
    return _carry_call(
        body, name, (MLA_HEADS, nq), [seq_h, seq_h, seq_h, blk_h, blk_h, _full_spec(bias.shape, 2), seq, seq],
        [seq_h, blk_h, seq],
        [jax.ShapeDtypeStruct((t, MLA_HEADS * LANES), BF16), jax.ShapeDtypeStruct((t, MLA_HEADS * LANES), BF16),
         jax.ShapeDtypeStruct((t, LANES), F32)],
        [pltpu.VMEM((blk, blk), F32)] * 4 + [pltpu.VMEM((blk, blk), BF16)] * 4 + [pltpu.VMEM((blk, LANES), F32)] * 2
        + [pltpu.VMEM((t, LANES), F32)], (qr, o, do, km, vb, bias, cos, sin), carried)


def _rms_rows(row0, x, g):
    return (_rms(x, g),)


def _ssdmla_fwd(h, p, l, e, cos, sin, carried=None):
    hn = _prenorm(h, p["mix_pre_g"][l], "sm_prenorm")
    proj = _mm(hn, p["w_in"][e], "nn", "sm_in")
    xc = _conv_fwd(proj, p["ssd_conv_w"][e], p["ssd_conv_b"][e], "ssd_conv", cw=SSD_GW, c0=PROJ_XBC // SSD_GW)
    xact = _rowwise("ssd_act", _ssd_act, [xc], [], [(SSD_CONV_CH, F32)])[0]
    dt, da = _rowwise("ssd_dt", _ssd_dt, [(proj, LANES, PROJ_DT // LANES)], [p["ssd_dt_bias"][e], p["ssd_a_log"][e]],
                      [(LANES, F32)] * 2)
    y, hsave = _ssd_scan(xact, dt, da, "ssd_scan")
    y_ssd = _rowwise("ssd_post", _ssd_post, [y, (xact, SSD_D_INNER, 0), (proj, SSD_D_INNER, 0)],
                     [p["ssd_d"][e], p["ssd_norm_g"][e]], [(SSD_D_INNER, BF16)])[0]
    cqn = _prenorm((proj, MLA_Q_RANK, PROJ_CQ // MLA_Q_RANK), p["mla_q_norm_g"][e], "mla_qnorm")
    ckvn = _prenorm((proj, MLA_KV_RANK, PROJ_CKV // MLA_KV_RANK), p["mla_kv_norm_g"][e], "mla_kvnorm")
    kr = _rowwise("mla_krope", lambda r0, x, c, s: (_rope(x, c, s),), [(proj, LANES, PROJ_KR // LANES), cos, sin], [],
                  [(LANES, F32)])[0]
    qr = _mm(cqn, p["mla_w_q_up"][e], "nn", "mla_q_up", slots=[cos, sin], post=lambda v, c, s: (_rope(v, c, s),), out_dtypes=[BF16])
    km, vb = _mm(ckvn, p["mla_w_kv_up"][e], "nn", "mla_kv_up", slots=[kr], post=lambda v, k: _key_slots(0, v, k),
                 out_dtypes=[BF16, BF16])
    o, carried_out = _attn_fwd(qr, km, vb, "mla_attn", carried)
    m1 = _mm(y_ssd, p["w_out_ssd"][e], "nn", "sm_out_ssd")
    m, h2 = _mm(o, p["w_out_att"][e], "nn", "sm_out_att", extra=[m1, h], vecs=[p["mix_post_g"][l]],
                post=lambda v, m1b, hb, g: _post_residual(v + m1b, hb, g), out_dtypes=[F32, F32])
    return h2, (h, hn, proj, xc, xact, dt, da, y, hsave, y_ssd, cqn, ckvn, qr, km, vb, o, m), carried_out


def _ssdmla_bwd(dh, saved, p, l, e, cos, sin, grads, carry=None):
    h, hn, proj, xc, xact, dt, da, y, hsave, y_ssd, cqn, ckvn, qr, km, vb, o, m = saved
    dm, grads["mix_post_g"][l] = _postnorm_bwd(m, p["mix_post_g"][l], dh, "sm_postnorm_bwd")
    grads["w_out_ssd"][e] = _mm(y_ssd, dm, "tn", "sm_out_ssd_dw")
    grads["w_out_att"][e] = _mm(o, dm, "tn", "sm_out_att_dw")
    dy_ssd = _mm(dm, p["w_out_ssd"][e], "nt", "sm_out_ssd_dx")
    do = _mm(dm, p["w_out_att"][e], "nt", "sm_out_att_dx", out_dtype=BF16)
    (dq, dkv, dkr), carried_out = _attn_bwd(qr, km, vb, o, do, cos, sin, "mla_attn_bwd", carry() if carry else None)
    dkr_raw = _rowwise("mla_krope_bwd", lambda r0, d, c, s: (_rope_t(d, c, s),), [dkr, cos, sin], [], [(LANES, F32)])[0]
    grads["mla_w_q_up"][e] = _mm(cqn, dq, "tn", "mla_q_up_dw")
    dcqn = _mm(dq, p["mla_w_q_up"][e], "nt", "mla_q_up_dx")
    (dcq,), (grads["mla_q_norm_g"][e],) = _rowwise_vjp(
        "mla_qnorm_bwd", _rms_rows, [(proj, MLA_Q_RANK, PROJ_CQ // MLA_Q_RANK)], [p["mla_q_norm_g"][e]], [dcqn])
    grads["mla_w_kv_up"][e] = _mm(ckvn, dkv, "tn", "mla_kv_up_dw")
    dckvn = _mm(dkv, p["mla_w_kv_up"][e], "nt", "mla_kv_up_dx")
    (dckv,), (grads["mla_kv_norm_g"][e],) = _rowwise_vjp(
        "mla_kvnorm_bwd", _rms_rows, [(proj, MLA_KV_RANK, PROJ_CKV // MLA_KV_RANK)], [p["mla_kv_norm_g"][e]], [dckvn])
    (dy, dxskip, dz), (grads["ssd_d"][e], grads["ssd_norm_g"][e]) = _rowwise_vjp(
        "ssd_post_bwd", _ssd_post, [y, (xact, SSD_D_INNER, 0), (proj, SSD_D_INNER, 0)], [p["ssd_d"][e], p["ssd_norm_g"][e]], [dy_ssd])
    dxs, db, dc, ddt, dda = _ssd_scan_bwd(xact, dt, da, hsave, dy, dxskip, "ssd_scan_bwd")
    dxact = jnp.concatenate([dxs, db, dc], axis=1)
    (dxc,), _ = _rowwise_vjp("ssd_act_bwd", _ssd_act, [xc], [], [dxact])
    dxbc, grads["ssd_conv_w"][e], grads["ssd_conv_b"][e] = _conv_bwd(
        proj, p["ssd_conv_w"][e], dxc, "ssd_conv_bwd", cw=SSD_GW, c0=PROJ_XBC // SSD_GW)
    (ddtraw,), (grads["ssd_dt_bias"][e], grads["ssd_a_log"][e]) = _rowwise_vjp(
        "ssd_dt_bwd", _ssd_dt, [(proj, LANES, PROJ_DT // LANES)], [p["ssd_dt_bias"][e], p["ssd_a_log"][e]], [ddt, dda])
    dproj = jnp.concatenate([dz, dxbc, ddtraw, dcq, dckv, dkr_raw], axis=1).astype(BF16)
    grads["w_in"][e] = _mm(hn, dproj, "tn", "sm_in_dw")
    dhn = _mm(dproj, p["w_in"][e], "nt", "sm_in_dx")
    dh, grads["mix_pre_g"][l] = _prenorm_bwd_add(h, p["mix_pre_g"][l], [dhn], dh, "sm_prenorm_bwd")
    return dh, carried_out


GAINS = ("mix_pre_g", "mix_post_g", "mlp_pre_g", "mlp_post_g", "ssd_norm_g", "mla_q_norm_g", "mla_kv_norm_g", "ssd_conv_b", "rg_conv_b")
HEAD_VECS = ("ssd_dt_bias", "ssd_a_log", "ssd_d")
LRU_VECS = ("rg_b_a", "rg_b_i", "rg_lambda")
IN_DT_END = SSD_D_INNER + SSD_CONV_CH + SSD_HEADS
IN_KR = IN_DT_END + MLA_Q_RANK + MLA_KV_RANK


def _each(a, f):
    layers = a if isinstance(a, list) else [a[i] for i in range(a.shape[0])]
    return [None if x is None else f(x) for x in layers]


def _layout_params(w):
    p = {k: _each(w[k], lambda a: a[None, :]) for k in GAINS}
    for k in HEAD_VECS:
        p[k] = _each(w[k], lambda a: jnp.pad(a, (0, LANES - SSD_HEADS))[None, :])
    for k in LRU_VECS:
        p[k] = _each(w[k], lambda a: a.reshape(LRU_BLOCKS, 1, LRU_BLOCK))
    for k in ("w_up", "w_down", "mla_w_kv_up", "rg_w_x", "rg_w_y", "rg_w_out"):
        p[k] = _each(w[k], lambda a: a if isinstance(a, Gathered) else a.astype(BF16))
    for k in ("ssd_conv_w", "rg_conv_w", "rg_w_a", "rg_w_i"):
        p[k] = _each(w[k], lambda a: a)

    def w_in(a):
        def zcols(n):
            return jnp.zeros((a.shape[0], n), a.dtype)

        return jnp.concatenate([a[:, :IN_DT_END], zcols(PROJ_CQ - IN_DT_END), a[:, IN_DT_END:IN_KR], zcols(ROPE_LO),
                                a[:, IN_KR:], zcols(LANES - ROPE_HI)], axis=1).astype(BF16)

    def q_up(a):
        a = a.reshape(MLA_Q_RANK, MLA_HEADS, MLA_NOPE + MLA_ROPE)
        return jnp.pad(a, ((0, 0), (0, 0), (0, LANES - MLA_NOPE - MLA_ROPE))).reshape(MLA_Q_RANK, MLA_HEADS * LANES).astype(BF16)

    def out_att(a):
        a = a[SSD_D_INNER:].reshape(MLA_HEADS, MLA_V, D_MODEL)
        return jnp.pad(a, ((0, 0), (LANES - MLA_V, 0), (0, 0))).reshape(MLA_HEADS * LANES, D_MODEL).astype(BF16)

    p["w_in"] = _each(w["w_in"], w_in)
    p["mla_w_q_up"] = _each(w["mla_w_q_up"], q_up)
    p["w_out_ssd"] = _each(w["w_out_ab"], lambda a: a[:SSD_D_INNER].astype(BF16))
    p["w_out_att"] = _each(w["w_out_ab"], out_att)
    return p


def _rope_tables(t):
    pos = (jnp.arange(t) - PAD).astype(F32)
    inv = ROPE_BASE ** (-jnp.arange(0, MLA_ROPE, 2, dtype=F32) / MLA_ROPE)
    ang = pos[:, None] * inv[None, :]
    c, s = jnp.cos(ang), jnp.sin(ang)
    one, zero = jnp.ones((t, MLA_NOPE), F32), jnp.zeros((t, MLA_NOPE), F32)
    tail = LANES - ROPE_HI
    return (jnp.concatenate([one, c, c, one[:, :tail]], axis=1), jnp.concatenate([zero, -s, s, zero[:, :tail]], axis=1))


GRAD_KEYS = GAINS + HEAD_VECS + LRU_VECS + ("w_up", "w_down", "mla_w_kv_up", "rg_w_x", "rg_w_y", "rg_w_out", "ssd_conv_w",
                                            "rg_conv_w", "rg_w_a", "rg_w_i", "w_in", "mla_w_q_up", "w_out_ssd", "w_out_att")


def _device_step(x, meta, target, p, hooks=None):
    t = PAD + N_META + x.shape[0]
    cos, sin = _rope_tables(t)
    h = jnp.concatenate([jnp.zeros((PAD, D_MODEL), F32), meta, x], axis=0)
    n_even, n_odd = (DEPTH + 1) // 2, DEPTH // 2
    saved = []
    for l in range(DEPTH):
        if l % 2 == 0:
            carried = hooks.forward_exchange() if hooks and l == 0 else None
            h, sm, arrived = _ssdmla_fwd(h, p, l, l // 2, cos, sin, carried)
            if carried is not None:
                p = hooks.after_forward_exchange(arrived)
        else:
            h, sm = _rglru_fwd(h, p, l, l // 2)
        h, sp = _mlp_fwd(h, p, l)
        saved.append((sm, sp))
    sq, dh = _loss_and_grad(h, target, "loss")
    per_layer = {"mix_pre_g": DEPTH, "mix_post_g": DEPTH, "mlp_pre_g": DEPTH, "mlp_post_g": DEPTH, "w_up": DEPTH, "w_down": DEPTH}
    grads = {k: [None] * per_layer.get(k, n_odd if k.startswith("rg_") else n_even) for k in GRAD_KEYS}
    for l in reversed(range(DEPTH)):
        sm, sp = saved[l]
        dh = _mlp_bwd(dh, sp, p, l, grads)
        if l % 2 == 0:
            carry = functools.partial(hooks.backward_exchange, grads, l) if hooks else None
            dh, arrived = _ssdmla_bwd(dh, sm, p, l, l // 2, cos, sin, grads, carry)
            if hooks:
                hooks.after_backward_exchange(arrived, l)
        else:
            dh = _rglru_bwd(dh, sm, p, l, l // 2, grads)
    return sq, dh, grads


MESH = pl.DeviceIdType.MESH
ANY = pl.BlockSpec(memory_space=pl.ANY)


def _mesh_pos():
    return lax.axis_index("x"), lax.axis_index("y"), lax.axis_index("c")


def _other_chips(x, y):
    return [(1 - x, y), (x, 1 - y), (1 - x, 1 - y)]


def _remote(src, dst, send_sems, recv_sems, k, to):
    return pltpu.make_async_remote_copy(src_ref=src, dst_ref=dst, send_sem=send_sems.at[k], recv_sem=recv_sems.at[k],
                                        device_id=to, device_id_type=MESH)


class Exchange:
    def __init__(self, ins, outs, aliases, n_sems, plan):
        self.ins, self.outs, self.aliases, self.n_sems, self.plan = list(ins), list(outs), dict(aliases), n_sems, plan


def _sems(n):
    return [pltpu.SemaphoreType.DMA((n,)), pltpu.SemaphoreType.DMA((n,))]


def _run_exchange(name, ex):
    ni, no = len(ex.ins), len(ex.outs)

    def body(*refs):
        sends = ex.plan(refs[:ni], refs[ni:ni + no], refs[-2], refs[-1], False)
        for cp in sends:
            cp.start()
        for cp in ex.plan(refs[:ni], refs[ni:ni + no], refs[-2], refs[-1], True):
            cp.wait_recv()
        for cp in sends:
            cp.wait_send()

    return pl.pallas_call(body, name=name, in_specs=[ANY] * ni, out_specs=[ANY] * no, out_shape=ex.outs,
                          input_output_aliases=ex.aliases, scratch_shapes=_sems(ex.n_sems))(*ex.ins)


def _carry_call(body, name, grid, in_specs, out_specs, out_shape, scratch_shapes, args, ex):
    if ex is None:
        res = pl.pallas_call(body, name=name, grid=grid, in_specs=in_specs, out_specs=out_specs, out_shape=out_shape,
                             scratch_shapes=scratch_shapes, compiler_params=_params(("arbitrary",) * len(grid)))(*args)
        return res, None
    ni, no, ns, xi, xo = len(in_specs), len(out_specs), len(scratch_shapes), len(ex.ins), len(ex.outs)

    def wrapped(*refs):
        ins, xin = refs[:ni], refs[ni:ni + xi]
        outs, xout = refs[ni + xi:ni + xi + no], refs[ni + xi + no:ni + xi + no + xo]
        scr, send_sems, recv_sems = refs[ni + xi + no + xo:-2], refs[-2], refs[-1]
        pid = [pl.program_id(d) for d in range(len(grid))]
        first = functools.reduce(jnp.logical_and, [p == 0 for p in pid])
        last = functools.reduce(jnp.logical_and, [p == g - 1 for p, g in zip(pid, grid)])

        @pl.when(first)
        def _():
            for cp in ex.plan(xin, xout, send_sems, recv_sems, False):
                cp.start()

        body(*ins, *outs, *scr)

        @pl.when(last)
        def _():
            for cp in ex.plan(xin, xout, send_sems, recv_sems, True):
                cp.wait_recv()
            for cp in ex.plan(xin, xout, send_sems, recv_sems, False):
                cp.wait_send()

    res = pl.pallas_call(
        wrapped, name=name, grid=grid, in_specs=list(in_specs) + [ANY] * xi, out_specs=list(out_specs) + [ANY] * xo,
        out_shape=list(out_shape) + ex.outs, scratch_shapes=list(scratch_shapes) + _sems(ex.n_sems),
        input_output_aliases={ni + i: no + o for i, o in ex.aliases.items()},
        compiler_params=_params(("arbitrary",) * len(grid)))(*args, *ex.ins)
    return res[:no], res[no:]


def _gather_ici(srcs, bufs, ranges):
    n = len(srcs)

    def plan(in_refs, out_refs, ss, rs, arrivals):
        x, y, c = _mesh_pos()
        cps = []
        for t, (l0, nl) in enumerate(ranges):
            if nl:
                s, o, lr = in_refs[t], out_refs[t], pl.ds(l0, nl)
                for j, (cx, cy) in enumerate(_other_chips(x, y)):
                    chip = 2 * cx + cy if arrivals else 2 * x + y
                    cps.append(_remote(s.at[lr, c], o.at[chip, lr, c], ss, rs, (N_CHIPS - 1) * t + j, (cx, cy, c)))
        return cps

    outs = [jax.ShapeDtypeStruct((N_CHIPS,) + s.shape, s.dtype) for s in srcs]
    if bufs is None:
        return Exchange(srcs, outs, {}, (N_CHIPS - 1) * n, plan)
    return Exchange(list(srcs) + list(bufs), outs, {n + t: t for t in range(n)}, (N_CHIPS - 1) * n, plan)


def _gather_d2d(srcs, bufs, ranges):
    n = len(srcs)

    def plan(in_refs, out_refs, ss, rs, arrivals):
        x, y, c = _mesh_pos()
        sib, me = (x, y, 1 - c), 2 * x + y
        cps = []
        for t, (l0, nl) in enumerate(ranges):
            if nl:
                s, o, lr = in_refs[t], out_refs[t], pl.ds(l0, nl)
                for j, (cx, cy) in enumerate(_other_chips(x, y)):
                    slot = o.at[2 * cx + cy, lr, c]
                    cps.append(_remote(slot, o.at[2 * cx + cy, lr, 1 - c] if arrivals else slot, ss, rs, N_CHIPS * t + j, sib))
                cps.append(_remote(s.at[lr], o.at[me, lr], ss, rs, N_CHIPS * t + N_CHIPS - 1, sib))
        return cps

    outs = [jax.ShapeDtypeStruct(b.shape, b.dtype) for b in bufs]
    return Exchange(list(srcs) + list(bufs), outs, {n + t: t for t in range(n)}, N_CHIPS * n, plan)


def _gather_chips(srcs, name):
    ranges = [(0, s.shape[0]) for s in srcs]
    bufs = _run_exchange(name + "_ici", _gather_ici(srcs, None, ranges))
    return _run_exchange(name + "_d2d", _gather_d2d(srcs, bufs, ranges))


def _pair_exchange(gs):
    def plan(in_refs, out_refs, ss, rs, arrivals):
        x, y, c = _mesh_pos()
        return [_remote(g.at[pl.ds(0, N_CHIPS), 1 - c], o, ss, rs, t, (x, y, 1 - c)) for t, (g, o) in enumerate(zip(in_refs, out_refs))]

    return Exchange(gs, [jax.ShapeDtypeStruct((g.shape[0],) + g.shape[2:], g.dtype) for g in gs], {}, len(gs), plan)


def _chip_exchange(ps, slots, qs, q_shapes):
    n = len(ps)
    kept = [g for g, q in enumerate(qs) if q is not None]

    def plan(in_refs, out_refs, ss, rs, arrivals):
        x, y, c = _mesh_pos()
        return [_remote(in_refs[t].at[2 * cx + cy], out_refs[g].at[j, li], ss, rs, (N_CHIPS - 1) * t + j, (cx, cy, c))
                for t, (g, li) in enumerate(slots) for j, (cx, cy) in enumerate(_other_chips(x, y))]

    return Exchange(list(ps) + [qs[g] for g in kept], q_shapes, {n + i: g for i, g in enumerate(kept)}, (N_CHIPS - 1) * n, plan)


def _pair_share(fs):
    def plan(in_refs, out_refs, ss, rs, arrivals):
        x, y, c = _mesh_pos()
        return [_remote(o.at[pl.ds(0, o.shape[0]), c], o.at[pl.ds(0, o.shape[0]), 1 - c if arrivals else c], ss, rs, t, (x, y, 1 - c))
                for t, o in enumerate(out_refs)]

    return Exchange(fs, [jax.ShapeDtypeStruct(f.shape, f.dtype) for f in fs], {t: t for t in range(len(fs))}, len(fs), plan)


SUM_BLOCK = 512 * 1024


def _sum_pair(g, ra, c, name):
    n, _, h, w = g.shape
    tr = _tile(h, max(16, SUM_BLOCK // w), 16)

    def body(c_ref, g_ref, r_ref, o_ref):
        o_ref[...] = (g_ref[0] + r_ref[...]).astype(o_ref.dtype)

    return pl.pallas_call(
        body, name=name,
        grid_spec=pltpu.PrefetchScalarGridSpec(
            num_scalar_prefetch=1, grid=(n, h // tr),
            in_specs=[pl.BlockSpec((1, 1, tr, w), lambda s, i, cr: (s, cr[0], i, 0)), pl.BlockSpec((1, tr, w), lambda s, i, cr: (s, i, 0))],
            out_specs=pl.BlockSpec((1, tr, w), lambda s, i, cr: (s, i, 0))),
        out_shape=jax.ShapeDtypeStruct((n, h, w), BF16),
        compiler_params=_params(("parallel", "parallel")),
    )(c.reshape(1).astype(jnp.int32), g, ra)


def _sum_chips(ps, q, pos, name):
    nc, nl, h, w = q.shape
    tr = _tile(h, max(16, SUM_BLOCK // (w * nl)), 16)

    def body(x_ref, y_ref, c_ref, *refs):
        q_ref, o_ref = refs[nl], refs[nl + 1]
        for l in range(nl):
            acc = refs[l][0].astype(F32)
            for j in range(nc):
                acc = acc + q_ref[j, l].astype(F32)
            o_ref[l] = acc

    return pl.pallas_call(
        body, name=name,
        grid_spec=pltpu.PrefetchScalarGridSpec(
            num_scalar_prefetch=3, grid=(h // tr,),
            in_specs=[pl.BlockSpec((1, tr, w), lambda i, x, y, c: (2 * x[0] + y[0], i, 0))] * nl
            + [pl.BlockSpec((nc, nl, tr, w), lambda i, x, y, c: (0, 0, i, 0))],
            out_specs=pl.BlockSpec((nl, None, tr, w), lambda i, x, y, c: (0, c[0], i, 0))),
        out_shape=jax.ShapeDtypeStruct((nl, 2, h, w), F32),
        compiler_params=_params(("parallel",)),
    )(*pos, *ps, q)


def _adamw(g, w, m, v, name):
    def f(r0, gg, ww, mm, vv):
        m2 = ADAM_B1 * mm + (1.0 - ADAM_B1) * gg
        v2 = ADAM_B2 * vv + (1.0 - ADAM_B2) * jnp.square(gg)
        m_hat = m2 / (1.0 - ADAM_B1 ** ADAM_STEP)
        v_hat = v2 / (1.0 - ADAM_B2 ** ADAM_STEP)
        return gg, -ADAM_LR * (m_hat / (jnp.sqrt(v_hat) + ADAM_EPS) + ADAM_WD * ww), m2, v2

    return _rowwise(name, f, [g, w, m, v], [], [(g.shape[1], F32)] * 4, tr=_tile(g.shape[0], 512))


WEIGHTS = (
    ("meta_tokens", (N_META, D_MODEL), 1), ("mix_pre_g", (DEPTH, D_MODEL), None), ("mix_post_g", (DEPTH, D_MODEL), None),
    ("mlp_pre_g", (DEPTH, D_MODEL), None), ("mlp_post_g", (DEPTH, D_MODEL), None), ("w_up", (DEPTH, D_MODEL, D_FF), 2),
    ("w_down", (DEPTH, D_FF, D_MODEL), 1), ("w_in", (2, D_MODEL, 3248), 2), ("ssd_conv_w", (2, CONV_K, SSD_CONV_CH), 2),
    ("ssd_conv_b", (2, SSD_CONV_CH), None), ("ssd_dt_bias", (2, SSD_HEADS), None), ("ssd_a_log", (2, SSD_HEADS), None),
    ("ssd_d", (2, SSD_HEADS), None), ("ssd_norm_g", (2, SSD_D_INNER), None), ("mla_q_norm_g", (2, MLA_Q_RANK), None),
    ("mla_w_q_up", (2, MLA_Q_RANK, MLA_HEADS * (MLA_NOPE + MLA_ROPE)), 2), ("mla_kv_norm_g", (2, MLA_KV_RANK), None),
    ("mla_w_kv_up", (2, MLA_KV_RANK, MLA_HEADS * (MLA_NOPE + MLA_V)), 2), ("w_out_ab", (2, SSD_D_INNER + MLA_HEADS * MLA_V, D_MODEL), 1),
    ("rg_w_x", (2, D_MODEL, LRU_WIDTH), 2), ("rg_w_y", (2, D_MODEL, LRU_WIDTH), 2), ("rg_conv_w", (2, CONV_K, LRU_WIDTH), 2),
    ("rg_conv_b", (2, LRU_WIDTH), 1), ("rg_w_a", (2, LRU_BLOCKS, LRU_BLOCK, LRU_BLOCK), None), ("rg_b_a", (2, LRU_WIDTH), 1),
    ("rg_w_i", (2, LRU_BLOCKS, LRU_BLOCK, LRU_BLOCK), None), ("rg_b_i", (2, LRU_WIDTH), 1), ("rg_lambda", (2, LRU_WIDTH), 1),
    ("rg_w_out", (2, LRU_WIDTH, D_MODEL), 1),
)
BIG = {"w_up": "col", "w_down": "row", "w_in": "col", "mla_w_q_up": "col", "mla_w_kv_up": "col", "w_out_ab": "row",
       "rg_w_x": "col", "rg_w_y": "col", "rg_w_out": "row"}
DIRECT = ("w_up", "w_down")
FLAT_QUANTUM = 2 * 16 * LANES
TABLE = {name: (shape, d) for name, shape, d in WEIGHTS}
SMALL_SHARDED = tuple(name for name, _, d in WEIGHTS if d is not None and name not in BIG)
REPLICATED = tuple(name for name, _, d in WEIGHTS if d is None)


def _chips_to_full(a, kind):
    if kind == "col":
        return jnp.moveaxis(a, 0, 2).reshape(a.shape[1], a.shape[2], -1)
    return jnp.moveaxis(a, 0, 1).reshape(a.shape[1], -1, a.shape[3])


def _full_to_chips(g, kind):
    if kind == "col":
        return jnp.moveaxis(g.reshape(g.shape[0], N_CHIPS, -1), 1, 0)
    return g.reshape(N_CHIPS, -1, g.shape[1])


def _shard_shape(shape, d):
    return shape[:d] + (shape[d] // N_CHIPS,) + shape[d + 1:]


def _shard_major(full, d):
    s = full.shape
    return jnp.moveaxis(full.reshape(s[:d] + (N_CHIPS, s[d] // N_CHIPS) + s[d + 1:]), d, 0).reshape(N_CHIPS, -1)


def _from_shard_major(a, shape, d):
    ss = _shard_shape(shape, d)
    return jnp.moveaxis(a.reshape((N_CHIPS,) + ss), 0, d).reshape(shape)


def _pad_cols(a, quantum):
    n = a.shape[-1]
    return jnp.pad(a, [(0, 0)] * (a.ndim - 1) + [(0, -n % quantum)])


def _big_pieces(g):
    def w_in(a):
        return jnp.concatenate([a[:, :IN_DT_END], a[:, PROJ_CQ:PROJ_KR], a[:, PROJ_KR + ROPE_LO:PROJ_KR + ROPE_HI]], axis=1)

    def q_up(a):
        return a.reshape(MLA_Q_RANK, MLA_HEADS, LANES)[:, :, :MLA_NOPE + MLA_ROPE].reshape(MLA_Q_RANK, -1)

    def out_ab(sa):
        s, a = sa
        return jnp.concatenate([s, a.reshape(MLA_HEADS, LANES, D_MODEL)[:, LANES - MLA_V:, :].reshape(-1, D_MODEL)], axis=0)

    ident = lambda a: a
    full = {"w_down": _each(g["w_down"], ident), "w_in": _each(g["w_in"], w_in), "mla_w_q_up": _each(g["mla_w_q_up"], q_up),
            "mla_w_kv_up": _each(g["mla_w_kv_up"], ident),
            "w_out_ab": _each([None if s is None or a is None else (s, a) for s, a in zip(g["w_out_ssd"], g["w_out_att"])], out_ab),
            "rg_w_x": _each(g["rg_w_x"], ident), "rg_w_y": _each(g["rg_w_y"], ident), "rg_w_out": _each(g["rg_w_out"], ident)}
    return {name: (list(g[name]) if name == "w_up" else _each(full[name], lambda a, k=BIG[name]: _full_to_chips(a, k))) for name in BIG}


def _small_grads(g, dh):
    out = {k: jnp.stack(g[k])[:, 0, :] for k in GAINS}
    for k in HEAD_VECS:
        out[k] = jnp.stack(g[k])[:, 0, :SSD_HEADS]
    for k in LRU_VECS:
        out[k] = jnp.stack(g[k]).reshape(-1, LRU_WIDTH)
    for k in ("ssd_conv_w", "rg_conv_w", "rg_w_a", "rg_w_i"):
        out[k] = jnp.stack(g[k])
    out["meta_tokens"] = dh[PAD:PAD + N_META]
    return out


class StepExchanges:
    def __init__(self, w):
        self.w = w
        self.c = lax.axis_index("c")
        self.riding, self.ras = {}, {}
        small = _pad_cols(jnp.concatenate([w[n].reshape(-1) for n in SMALL_SHARDED]), FLAT_QUANTUM).reshape(1, 2, -1, LANES)
        self.srcs = [self._halves(w[n].astype(BF16)) for n in BIG] + [small]
        first = {n: (0, 1 if n in ("w_in", "mla_w_q_up", "mla_w_kv_up", "w_out_ab") else 0) for n in BIG}
        self.first = [first[n] for n in BIG] + [(0, 1)]
        self.rest = [(nl, TABLE[n][0][0] - nl) for n, (_, nl) in zip(BIG, self.first)] + [(0, 0)]
        bufs = _run_exchange("gather_first_ici", _gather_ici(self.srcs, None, self.first))
        self.bufs = _run_exchange("gather_first_d2d", _gather_d2d(self.srcs, bufs, self.first))

    @staticmethod
    def _halves(a):
        return a.reshape(a.shape[0], 2, a.shape[1] // 2, a.shape[2])

    def params(self, ranges):
        w = self.w
        full = {n: w[n] for n in REPLICATED}
        for name, buf, (l0, nl) in zip(BIG, self.bufs, ranges):
            a = buf.reshape(buf.shape[:2] + (-1, buf.shape[4]))
            have = range(l0, l0 + nl)
            if name in DIRECT:
                full[name] = [Gathered(a, BIG[name], l) if l in have else None for l in range(a.shape[1])]
            else:
                full[name] = [_chips_to_full(a[:, l:l + 1], BIG[name])[0] if l in have else None for l in range(a.shape[1])]
        got, off = self.bufs[-1].reshape(N_CHIPS, -1), 0
        for name in SMALL_SHARDED:
            shape, d = TABLE[name]
            n = int(np.prod(_shard_shape(shape, d)))
            full[name] = _from_shard_major(got[:, off:off + n], shape, d)
            off += n
        self.meta = full.pop("meta_tokens")
        return _layout_params(full)

    def forward_exchange(self):
        return _gather_ici(self.srcs, self.bufs, self.rest)

    def after_forward_exchange(self, arrived):
        self.bufs = _run_exchange("gather_rest_d2d", _gather_d2d(self.srcs, arrived, self.rest))
        return self.params([(0, TABLE[n][0][0]) for n in BIG])

    def _pair_sums(self, pieces, tag):
        keys = list(pieces)
        ras = _run_exchange("grads_pair_exchange_" + tag, _pair_exchange([pieces[k] for k in keys]))
        return {k: _sum_pair(pieces[k], ra, self.c, "grads_pair_sum") for k, ra in zip(keys, ras)}

    def _q_shapes(self):
        return [jax.ShapeDtypeStruct((N_CHIPS - 1, s.shape[0]) + s.shape[2:], BF16) for s in self.srcs[:-1]]

    def backward_exchange(self, grads, layer):
        big = _big_pieces(grads)
        pieces = {(g, l): pc.reshape(N_CHIPS, 2, pc.shape[1] // 2, pc.shape[2]) for g, name in enumerate(BIG)
                  for l, pc in enumerate(big[name]) if pc is not None and (g, l) not in self.riding}
        if layer > 0:
            self.riding = pieces
            return _pair_exchange(list(pieces.values()))
        self.ps = {k: _sum_pair(self.riding[k], ra, self.c, "grads_pair_sum") for k, ra in self.ras.items()}
        self.ps.update(self._pair_sums(pieces, "early"))
        self.early = list(self.ps)
        return _chip_exchange([self.ps[k] for k in self.early], self.early, [None] * len(BIG), self._q_shapes())

    def after_backward_exchange(self, arrived, layer):
        if layer > 0:
            self.ras = dict(zip(self.riding, arrived))
        else:
            self.qs = list(arrived)

    def finish(self, grads, dh):
        big, small = _big_pieces(grads), _small_grads(grads, dh)
        pieces = {(g, l): pc.reshape(N_CHIPS, 2, pc.shape[1] // 2, pc.shape[2])
                  for g, name in enumerate(BIG) for l, pc in enumerate(big[name]) if (g, l) not in self.ps}
        sharded = jnp.concatenate([_shard_major(small[n], TABLE[n][1]) for n in SMALL_SHARDED], axis=1)
        rep = _pad_cols(jnp.concatenate([small[n].reshape(-1) for n in REPLICATED]), N_CHIPS * FLAT_QUANTUM)
        n_sh, n_rep = sharded.shape[1], rep.shape[0] // N_CHIPS
        flat = _pad_cols(jnp.concatenate([sharded, rep.reshape(N_CHIPS, n_rep)], axis=1), FLAT_QUANTUM)
        pieces[(len(BIG), 0)] = flat.reshape(N_CHIPS, 2, -1, LANES)
        late = self._pair_sums(pieces, "late")
        self.ps.update(late)
        keys = list(late)
        small_q = jax.ShapeDtypeStruct((N_CHIPS - 1, 1) + late[(len(BIG), 0)].shape[1:], BF16)
        qs = _run_exchange("grads_chip_exchange_late",
                           _chip_exchange([late[k] for k in keys], keys, self.qs + [None], self._q_shapes() + [small_q]))
        pos = [lax.axis_index(a).reshape(1).astype(jnp.int32) for a in ("x", "y", "c")]
        sums = [_sum_chips([self.ps[(g, l)] for l in range(q.shape[1])], q, pos, "grads_chip_sum") for g, q in enumerate(qs)]
        outs = _run_exchange("grads_pair_share", _pair_share(sums))
        out = {name: o.reshape(o.shape[0], -1, o.shape[3]) for name, o in zip(BIG, outs)}
        f = outs[-1].reshape(-1)
        rep_all = _gather_chips([f[n_sh:n_sh + n_rep].reshape(1, 2, -1, LANES)], "grads_gather_replicated")[0].reshape(-1)
        off = 0
        for name in SMALL_SHARDED:
            ss = _shard_shape(*TABLE[name])
            n = int(np.prod(ss))
            out[name] = f[off:off + n].reshape(ss)
            off += n
        off = 0
        for name in REPLICATED:
            shape = TABLE[name][0]
            n = int(np.prod(shape))
            out[name] = rep_all[off:off + n].reshape(shape)
            off += n
        return out


def kernel(x, meta_tokens, mix_pre_g, mix_post_g, mlp_pre_g, mlp_post_g, w_up, w_down, w_in, ssd_conv_w, ssd_conv_b, ssd_dt_bias, ssd_a_log, ssd_d, ssd_norm_g, mla_q_norm_g, mla_w_q_up, mla_kv_norm_g, mla_w_kv_up, w_out_ab, rg_w_x, rg_w_y, rg_conv_w, rg_conv_b, rg_w_a, rg_b_a, rg_w_i, rg_b_i, rg_lambda, rg_w_out, loss_target, m_meta_tokens, m_mix_pre_g, m_mix_post_g, m_mlp_pre_g, m_mlp_post_g, m_w_up, m_w_down, m_w_in, m_ssd_conv_w, m_ssd_conv_b, m_ssd_dt_bias, m_ssd_a_log, m_ssd_d, m_ssd_norm_g, m_mla_q_norm_g, m_mla_w_q_up, m_mla_kv_norm_g, m_mla_w_kv_up, m_w_out_ab, m_rg_w_x, m_rg_w_y, m_rg_conv_w, m_rg_conv_b, m_rg_w_a, m_rg_b_a, m_rg_w_i, m_rg_b_i, m_rg_lambda, m_rg_w_out, v_meta_tokens, v_mix_pre_g, v_mix_post_g, v_mlp_pre_g, v_mlp_post_g, v_w_up, v_w_down, v_w_in, v_ssd_conv_w, v_ssd_conv_b, v_ssd_dt_bias, v_ssd_a_log, v_ssd_d, v_ssd_norm_g, v_mla_q_norm_g, v_mla_w_q_up, v_mla_kv_norm_g, v_mla_w_kv_up, v_w_out_ab, v_rg_w_x, v_rg_w_y, v_rg_conv_w, v_rg_conv_b, v_rg_w_a, v_rg_b_a, v_rg_w_i, v_rg_b_i, v_rg_lambda, v_rg_w_out):
    names = [n for n, _, _ in WEIGHTS]
    w = dict(zip(names, (meta_tokens, mix_pre_g, mix_post_g, mlp_pre_g, mlp_post_g, w_up, w_down, w_in, ssd_conv_w, ssd_conv_b, ssd_dt_bias, ssd_a_log, ssd_d, ssd_norm_g, mla_q_norm_g, mla_w_q_up, mla_kv_norm_g, mla_w_kv_up, w_out_ab, rg_w_x, rg_w_y, rg_conv_w, rg_conv_b, rg_w_a, rg_b_a, rg_w_i, rg_b_i, rg_lambda, rg_w_out)))
    m = dict(zip(names, (m_meta_tokens, m_mix_pre_g, m_mix_post_g, m_mlp_pre_g, m_mlp_post_g, m_w_up, m_w_down, m_w_in, m_ssd_conv_w, m_ssd_conv_b, m_ssd_dt_bias, m_ssd_a_log, m_ssd_d, m_ssd_norm_g, m_mla_q_norm_g, m_mla_w_q_up, m_mla_kv_norm_g, m_mla_w_kv_up, m_w_out_ab, m_rg_w_x, m_rg_w_y, m_rg_conv_w, m_rg_conv_b, m_rg_w_a, m_rg_b_a, m_rg_w_i, m_rg_b_i, m_rg_lambda, m_rg_w_out)))
    v = dict(zip(names, (v_meta_tokens, v_mix_pre_g, v_mix_post_g, v_mlp_pre_g, v_mlp_post_g, v_w_up, v_w_down, v_w_in, v_ssd_conv_w, v_ssd_conv_b, v_ssd_dt_bias, v_ssd_a_log, v_ssd_d, v_ssd_norm_g, v_mla_q_norm_g, v_mla_w_q_up, v_mla_kv_norm_g, v_mla_w_kv_up, v_w_out_ab, v_rg_w_x, v_rg_w_y, v_rg_conv_w, v_rg_conv_b, v_rg_w_a, v_rg_b_a, v_rg_w_i, v_rg_b_i, v_rg_lambda, v_rg_w_out)))
    ex = StepExchanges(w)
    p = ex.params(ex.first)
    sq, dh, grads = _device_step(x[0], ex.meta, loss_target[0], p, hooks=ex)
    loss = lax.psum(0.5 * sq[0, 0] / D_MODEL, ("x", "y", "c"))
    g = ex.finish(grads, dh)
    grad, delta, new_m, new_v = {}, {}, {}, {}
    for name in names:
        shape = g[name].shape
        two_d = (int(np.prod(shape[:-1])), shape[-1])
        res = _adamw(g[name].reshape(two_d), w[name].reshape(two_d), m[name].reshape(two_d), v[name].reshape(two_d), "adamw")
        grad[name], delta[name], new_m[name], new_v[name] = (r.reshape(shape) for r in res)
    grad_x = dh[PAD + N_META:][None]
    return (loss, grad_x, *[grad[n] for n in names], *[delta[n] for n in names], *[new_m[n] for n in names], *[new_v[n] for n in names])
```

```python
import functools

import jax
import jax.numpy as jnp
import numpy as np
from jax import lax
from jax.experimental import pallas as pl
from jax.experimental.pallas import tpu as pltpu

F32 = jnp.float32
BF16 = jnp.bfloat16

D_MODEL = 1024
DEPTH = 4
N_META = 16
CHUNK = 128
PAD = CHUNK - N_META
EPS = 1e-6
SSD_HEADS = 16
SSD_HEAD_DIM = 64
SSD_D_INNER = SSD_HEADS * SSD_HEAD_DIM
SSD_GROUPS = 2
SSD_STATE = 128
SSD_CONV_CH = SSD_D_INNER + 2 * SSD_GROUPS * SSD_STATE
MLA_HEADS = 16
MLA_NOPE = 64
MLA_ROPE = 32
MLA_V = 64
MLA_Q_RANK = 384
MLA_KV_RANK = 256
ROPE_BASE = 10000.0
LRU_WIDTH = 1280
LRU_BLOCKS = 10
LRU_BLOCK = 128
LRU_C = 8.0
D_FF = 4 * D_MODEL
ADAM_LR, ADAM_B1, ADAM_B2, ADAM_EPS, ADAM_WD, ADAM_STEP = 0.001, 0.9, 0.999, 1e-08, 0.01, 10

LANES = 128
VMEM_LIMIT = 56 * 1024 * 1024
MM_VMEM_BUDGET = 48 * 1024 * 1024
PROJ_Z, PROJ_XBC, PROJ_DT, PROJ_CQ, PROJ_CKV, PROJ_KR = 0, 1024, 2560, 2688, 3072, 3328
PROJ_W = 3456


def _tile(n, cap, mult=8):
    for t in range(min(n, cap), 0, -1):
        if n % t == 0 and t % mult == 0:
            return t
    return n


def _params(sem):
    return pltpu.CompilerParams(dimension_semantics=sem, vmem_limit_bytes=VMEM_LIMIT)


def _full_spec(shape, ngrid):
    nd = len(shape)
    if ngrid == 1:
        return pl.BlockSpec(shape, lambda i: (0,) * nd)
    if ngrid == 2:
        return pl.BlockSpec(shape, lambda i, j: (0,) * nd)
    return pl.BlockSpec(shape, lambda i, j, k: (0,) * nd)


_DIMS = {"nn": (((1,), (0,)), ((), ())), "nt": (((1,), (1,)), ((), ())), "tn": (((0,), (0,)), ((), ()))}


class Gathered:
    def __init__(self, arr, kind, layer):
        self.arr, self.kind, self.layer = arr, kind, layer
        _, _, r, c = arr.shape
        self.shape = (r, N_CHIPS * c) if kind == "col" else (N_CHIPS * r, c)


N_CHIPS = 4


def _mm(a, b, mode, name, out_dtype=F32, add=None, out_chip_major=False, extra=(), vecs=(), slots=(), post=None, out_dtypes=None):
    if mode == "nn":
        (m, kc), (_, n) = a.shape, b.shape
    elif mode == "nt":
        (m, kc), (n, _) = a.shape, b.shape
    else:
        (kc, m), (_, n) = a.shape, b.shape
    n_tile = n // N_CHIPS if out_chip_major else n
    across = isinstance(b, Gathered) and (mode, b.kind) in (("nn", "row"), ("nt", "col"))
    if mode == "tn":
        tm, tk = _tile(m, 1024, LANES), kc
        fits = [c for c in (1280, 1152, 1024, 768, 640, 512) if n_tile % c == 0 and
                2 * kc * (tm * a.dtype.itemsize + c * b.dtype.itemsize) + 2 * tm * c * 4 <= MM_VMEM_BUDGET]
        tn = fits[0] if fits else _tile(n_tile, 1280, LANES)
        if not fits:
            tk = _tile(kc, 1408, LANES)
    else:
        tn = _tile(n_tile, 1280, LANES)
        tk = _tile(kc, 4096, LANES)
        tm = _tile(m, 1056 if tk <= 1024 else 528, 16)
    nk = kc // tk
    if mode == "tn":
        a_spec = pl.BlockSpec((tk, tm), lambda i, j, k: (k, i))
    else:
        a_spec = pl.BlockSpec((tm, tk), lambda i, j, k: (i, k))
    b_arrs = [b]
    if isinstance(b, Gathered):
        layer = b.layer
        sr, sc = b.arr.shape[2:]
        if across:
            assert nk == 1 and kc == N_CHIPS * (sr if b.kind == "row" else sc)
            b_arrs = [b.arr] * N_CHIPS
            if b.kind == "row":
                b_specs = [pl.BlockSpec((None, None, sr, tn), lambda i, j, k, s=s: (s, layer, 0, j)) for s in range(N_CHIPS)]
            else:
                b_specs = [pl.BlockSpec((None, None, tn, sc), lambda i, j, k, s=s: (s, layer, j, 0)) for s in range(N_CHIPS)]
        else:
            b_arrs = [b.arr]
            br, bc = (tk, tn) if mode == "nn" else (tn, tk)
            assert mode in ("nn", "nt") and sr % br == 0 and sc % bc == 0

            def b_map(i, j, k):
                r, c = (k, j) if mode == "nn" else (j, k)
                if b.kind == "col":
                    return ((c * bc) // sc, layer, r, ((c * bc) % sc) // bc)
                return ((r * br) // sr, layer, ((r * br) % sr) // br, c)

            b_specs = [pl.BlockSpec((None, None, br, bc), b_map)]
    elif mode == "nt":
        b_specs = [pl.BlockSpec((tn, tk), lambda i, j, k: (j, k))]
    else:
        b_specs = [pl.BlockSpec((tk, tn), lambda i, j, k: (k, j))]
    nb = len(b_arrs)
    dims = _DIMS[mode]
    if out_chip_major:
        ns = n // N_CHIPS
        o_spec = pl.BlockSpec((None, tm, tn), lambda i, j, k: ((j * tn) // ns, i, ((j * tn) % ns) // tn))
        o_shape = jax.ShapeDtypeStruct((N_CHIPS, m, ns), out_dtype)
    else:
        o_spec = pl.BlockSpec((tm, tn), lambda i, j, k: (i, j))
        o_shape = jax.ShapeDtypeStruct((m, n), out_dtype)
    extra = list(extra) + ([add] if add is not None else [])
    if add is not None:
        post = lambda v, x: (v + x,)
    vecs, slots = list(vecs), list(slots)
    nx = len(extra) + len(vecs) + len(slots)
    out_dtypes = out_dtypes or [out_dtype]
    no = len(out_dtypes)

    def body(a_ref, *rest):
        b_refs, rest = rest[:nb], rest[nb:]
        o_refs, acc = rest[nx:nx + no], rest[nx + no:]
        if across:
            w = kc // N_CHIPS
            p = functools.reduce(jnp.add, [
                lax.dot_general(a_ref[:, s * w:(s + 1) * w].astype(BF16), b_refs[s][...].astype(BF16), dims, preferred_element_type=F32)
                for s in range(N_CHIPS)])
        else:
            p = lax.dot_general(a_ref[...].astype(BF16), b_refs[0][...].astype(BF16), dims, preferred_element_type=F32)

        def emit(v):
            res = post(v, *[r[...] for r in rest[:nx]]) if post else (v,)
            for o_ref, r in zip(o_refs, res):
                o_ref[...] = r.astype(o_ref.dtype)

        if nk == 1:
            emit(p)
        else:
            k = pl.program_id(2)

            @pl.when(k == 0)
            def _():
                acc[0][...] = p

            @pl.when(k > 0)
            def _():
                acc[0][...] += p

            @pl.when(k == nk - 1)
            def _():
                emit(acc[0][...])

    res = pl.pallas_call(
        body, name=name, grid=(m // tm, n // tn, nk),
        in_specs=[a_spec] + b_specs + [o_spec] * len(extra) + [pl.BlockSpec((1, tn), lambda i, j, k: (0, j))] * len(vecs)
        + [pl.BlockSpec((tm, LANES), lambda i, j, k: (i, 0))] * len(slots),
        out_specs=[o_spec] * no,
        out_shape=[jax.ShapeDtypeStruct(o_shape.shape, dt) for dt in out_dtypes],
        scratch_shapes=[pltpu.VMEM((tm, tn), F32)] if nk > 1 else [],
        compiler_params=_params(("parallel", "parallel", "arbitrary")),
    )(a, *b_arrs, *extra, *vecs, *slots)
    return res[0] if no == 1 else res


def _rowarg(r):
    return r if isinstance(r, tuple) else (r, r.shape[1], 0)


def _rowspec(r, tr, ncol):
    _, w, cb = r
    if ncol > 1:
        return pl.BlockSpec((tr, w // ncol), lambda j, i: (i, j))
    return pl.BlockSpec((tr, w), lambda j, i: (i, cb))


def _rowwise(name, f, rows, params, outs, tr=None, ncol=1):
    rows = [_rowarg(r) for r in rows]
    t = rows[0][0].shape[0]
    tr = tr or _tile(t, 528)
    nr, npm = len(rows), len(params)

    def body(*refs):
        vals = [r[...] for r in refs[:nr]] + [(p[0] if ncol > 1 else p[...]) for p in refs[nr:nr + npm]]
        res = f(pl.program_id(1) * tr, *vals)
        for o_ref, v in zip(refs[nr + npm:], res):
            o_ref[...] = v.astype(o_ref.dtype)

    def pspec(p):
        if ncol > 1:
            return pl.BlockSpec((1,) + p.shape[1:], lambda j, i, n=p.ndim: (j,) + (0,) * (n - 1))
        return _full_spec(p.shape, 2)

    return pl.pallas_call(
        body, name=name, grid=(ncol, t // tr),
        in_specs=[_rowspec(r, tr, ncol) for r in rows] + [pspec(p) for p in params],
        out_specs=[pl.BlockSpec((tr, w // ncol), lambda j, i: (i, j)) for w, _ in outs],
        out_shape=[jax.ShapeDtypeStruct((t, w), dt) for w, dt in outs],
        compiler_params=_params(("parallel", "parallel")),
    )(*[r[0] for r in rows], *params)


def _rowwise_vjp(name, f, rows, params, cts, tr=None, ncol=1, row_dtypes=None):
    rows = [_rowarg(r) for r in rows]
    cts = [_rowarg(c) for c in cts]
    t = rows[0][0].shape[0]
    tr = tr or _tile(t, 528)
    nr, npm, nc = len(rows), len(params), len(cts)
    row_dtypes = row_dtypes or [F32] * nr

    def body(*refs):
        i = pl.program_id(1)
        vals = [r[...] for r in refs[:nr]] + [(p[0] if ncol > 1 else p[...]) for p in refs[nr:nr + npm]]
        ct = tuple(c[...].astype(F32) for c in refs[nr + npm:nr + npm + nc])
        _, vjp = jax.vjp(lambda *a: tuple(f(i * tr, *a)), *vals)
        g = vjp(ct)
        outs = refs[nr + npm + nc:]
        for o_ref, v in zip(outs[:nr], g[:nr]):
            o_ref[...] = v.astype(o_ref.dtype)
        pg = [(v[None] if ncol > 1 else v) for v in g[nr:]]

        @pl.when(i == 0)
        def _():
            for o_ref, v in zip(outs[nr:], pg):
                o_ref[...] = v

        @pl.when(i > 0)
        def _():
            for o_ref, v in zip(outs[nr:], pg):
                o_ref[...] += v

    def pspec(p):
        if ncol > 1:
            return pl.BlockSpec((1,) + p.shape[1:], lambda j, i, n=p.ndim: (j,) + (0,) * (n - 1))
        return _full_spec(p.shape, 2)

    res = pl.pallas_call(
        body, name=name, grid=(ncol, t // tr),
        in_specs=[_rowspec(r, tr, ncol) for r in rows] + [pspec(p) for p in params] + [_rowspec(c, tr, ncol) for c in cts],
        out_specs=[pl.BlockSpec((tr, w // ncol), lambda j, i: (i, j)) for _, w, _ in rows] + [pspec(p) for p in params],
        out_shape=[jax.ShapeDtypeStruct((t, w), dt) for (_, w, _), dt in zip(rows, row_dtypes)]
        + [jax.ShapeDtypeStruct(p.shape, F32) for p in params],
        compiler_params=_params(("parallel", "arbitrary")),
    )(*[r[0] for r in rows], *params, *[c[0] for c in cts])
    return res[:nr], res[nr:]


def _valid(row0, tr):
    return (row0 + lax.broadcasted_iota(jnp.int32, (tr, 1), 0)) >= PAD


def _rms(x, g):
    return x * lax.rsqrt(jnp.mean(x * x, axis=-1, keepdims=True) + EPS) * g


def _softplus(x):
    return jnp.where(x < -15.0, jnp.exp(x), jnp.maximum(x, 0.0) + jnp.log(1.0 + jnp.exp(-jnp.abs(x))))


def _neg_expm1(z):
    return jnp.where(z > -0.01, -z * (1.0 + z * (0.5 + z * (1.0 / 6.0))), 1.0 - jnp.exp(z))


def _prenorm(h, g, name):
    return _rowwise(name, lambda r0, x, gg: (_rms(x, gg),), [h], [g], [(_rowarg(h)[1], BF16)])[0]


def _post_residual(m, h, g):
    assert m.shape[1] == D_MODEL
    return m, h + _rms(m, g)


def _postnorm_bwd(m, g, dh, name):
    (dm,), (dg,) = _rowwise_vjp(name, lambda r0, mm, gg: (_rms(mm, gg),), [m], [g], [dh], row_dtypes=[BF16])
    return dm, dg


def _prenorm_bwd_add(h, g, dhns, dh, name):
    t, w = h.shape
    tr = _tile(t, 528)
    nd = len(dhns)

    def body(h_ref, g_ref, *refs):
        dh_ref, o_ref, dg_ref = refs[nd:]
        i = pl.program_id(0)
        _, vjp = jax.vjp(_rms, h_ref[...], g_ref[...])
        dhn = refs[0][...].astype(F32)
        for r in refs[1:nd]:
            dhn = dhn + r[...].astype(F32)
        dx, dg = vjp(dhn)
        o_ref[...] = dh_ref[...] + dx

        @pl.when(i == 0)
        def _():
            dg_ref[...] = dg

        @pl.when(i > 0)
        def _():
            dg_ref[...] += dg

    row = pl.BlockSpec((tr, w), lambda i: (i, 0))
    return pl.pallas_call(
        body, name=name, grid=(t // tr,), in_specs=[row, _full_spec(g.shape, 1)] + [row] * (nd + 1),
        out_specs=[row, _full_spec(g.shape, 1)],
        out_shape=[jax.ShapeDtypeStruct((t, w), F32), jax.ShapeDtypeStruct(g.shape, F32)],
        compiler_params=_params(("arbitrary",)),
    )(h, g, *dhns, dh)


def _loss_and_grad(h, target, name):
    t, w = h.shape
    nb = t // CHUNK

    def body(h_ref, t_ref, s_ref, dh_ref):
        i = pl.program_id(0)

        @pl.when(i == 0)
        def _():
            s_ref[...] = jnp.zeros_like(s_ref)
            dh_ref[...] = jnp.zeros_like(dh_ref)

        @pl.when(i > 0)
        def _():
            err = h_ref[...] - t_ref[...]
            s_ref[...] += jnp.sum(err * err)
            dh_ref[...] = err * (1.0 / w)

    return pl.pallas_call(
        body, name=name, grid=(nb,),
        in_specs=[pl.BlockSpec((CHUNK, w), lambda i: (i, 0)), pl.BlockSpec((CHUNK, w), lambda i: (jnp.maximum(i - 1, 0), 0))],
        out_specs=[_full_spec((1, LANES), 1), pl.BlockSpec((CHUNK, w), lambda i: (i, 0))],
        out_shape=[jax.ShapeDtypeStruct((1, LANES), F32), jax.ShapeDtypeStruct((t, w), F32)],
        compiler_params=_params(("arbitrary",)),
    )(h, target)


def _mlp_fwd(h, p, l):
    hn = _prenorm(h, p["mlp_pre_g"][l], "mlp_prenorm")
    a, u = _mm(hn, p["w_up"][l], "nn", "mlp_up", post=lambda v: (v, jnp.square(jnp.maximum(v, 0.0))), out_dtypes=[BF16, BF16])
    d, h2 = _mm(u, p["w_down"][l], "nn", "mlp_down", extra=[h], vecs=[p["mlp_post_g"][l]], post=_post_residual, out_dtypes=[F32, F32])
    return h2, (h, hn, a, u, d)


def _mlp_bwd(dh, saved, p, l, grads):
    h, hn, a, u, d = saved
    dd, grads["mlp_post_g"][l] = _postnorm_bwd(d, p["mlp_post_g"][l], dh, "mlp_postnorm_bwd")
    grads["w_down"][l] = _mm(u, dd, "tn", "mlp_down_dw")
    da = _mm(dd, p["w_down"][l], "nt", "mlp_down_dx", extra=[a], post=lambda v, x: (2.0 * jnp.maximum(x.astype(F32), 0.0) * v,),
             out_dtypes=[BF16])
    grads["w_up"][l] = _mm(hn, da, "tn", "mlp_up_dw", out_chip_major=True)
    dhn = _mm(da, p["w_up"][l], "nt", "mlp_up_dx")
    dh, grads["mlp_pre_g"][l] = _prenorm_bwd_add(h, p["mlp_pre_g"][l], [dhn], dh, "mlp_prenorm_bwd")
    return dh


def _dot(a, b, mode):
    return lax.dot_general(a.astype(BF16), b.astype(BF16), _DIMS[mode], preferred_element_type=F32)


@jax.custom_vjp
def _bnn(a, b):
    return _dot(a, b, "nn")


_bnn.defvjp(lambda a, b: (_dot(a, b, "nn"), (a, b)), lambda r, ct: (_dot(ct, r[1], "nt"), _dot(r[0], ct, "tn")))


@jax.custom_vjp
def _bnt(a, b):
    return _dot(a, b, "nt")


_bnt.defvjp(lambda a, b: (_dot(a, b, "nt"), (a, b)), lambda r, ct: (_dot(ct, r[1], "nn"), _dot(ct, r[0], "tn")))


@jax.custom_vjp
def _btn(a, b):
    return _dot(a, b, "tn")


_btn.defvjp(lambda a, b: (_dot(a, b, "tn"), (a, b)), lambda r, ct: (_dot(r[1], ct, "nt"), _dot(r[0], ct, "nn")))


CONV_K = 4
HALO = 8


def _conv_fwd(x, w, b, name, cw, c0=0):
    t, c = x.shape[0], w.shape[1]
    tr = _tile(t, 528)
    hb = tr // HALO

    def body(x_ref, halo_ref, w_ref, b_ref, o_ref, ext):
        i = pl.program_id(1)
        ext[pl.ds(0, HALO), :] = jnp.where(i > 0, halo_ref[...], 0.0)
        ext[pl.ds(HALO, tr), :] = x_ref[...]
        acc = jnp.broadcast_to(b_ref[...], (tr, cw))
        for k in range(CONV_K):
            acc = acc + w_ref[pl.ds(k, 1), :] * ext[pl.ds(HALO - (CONV_K - 1) + k, tr), :]
        o_ref[...] = acc

    return pl.pallas_call(
        body, name=name, grid=(c // cw, t // tr),
        in_specs=[pl.BlockSpec((tr, cw), lambda j, i: (i, c0 + j)),
                  pl.BlockSpec((HALO, cw), lambda j, i: (jnp.maximum(i * hb - 1, 0), c0 + j)),
                  pl.BlockSpec((CONV_K, cw), lambda j, i: (0, j)), pl.BlockSpec((1, cw), lambda j, i: (0, j))],
        out_specs=pl.BlockSpec((tr, cw), lambda j, i: (i, j)),
        out_shape=jax.ShapeDtypeStruct((t, c), F32),
        scratch_shapes=[pltpu.VMEM((tr + HALO, cw), F32)],
        compiler_params=_params(("parallel", "parallel")),
    )(x, x, w, b)


def _conv_bwd(x, w, dy, name, cw, c0=0):
    t, c = x.shape[0], w.shape[1]
    tr = _tile(t, 528)
    hb = tr // HALO
    nb = t // tr

    def body(x_ref, xh_ref, w_ref, dy_ref, dyh_ref, dx_ref, dw_ref, db_ref, xe, de):
        c = cw
        i = pl.program_id(1)
        xe[pl.ds(0, HALO), :] = jnp.where(i > 0, xh_ref[...], 0.0)
        xe[pl.ds(HALO, tr), :] = x_ref[...]
        de[pl.ds(0, tr), :] = dy_ref[...]
        de[pl.ds(tr, HALO), :] = jnp.where(i < nb - 1, dyh_ref[...], 0.0)
        dy = dy_ref[...]
        acc = jnp.zeros((tr, c), F32)
        dw = jnp.zeros((CONV_K, c), F32)
        rows = lax.broadcasted_iota(jnp.int32, (CONV_K, 1), 0)
        for k in range(CONV_K):
            acc = acc + w_ref[pl.ds(k, 1), :] * de[pl.ds(CONV_K - 1 - k, tr), :]
            dwk = jnp.sum(dy * xe[pl.ds(HALO - (CONV_K - 1) + k, tr), :], axis=0, keepdims=True)
            dw = dw + jnp.where(rows == k, dwk, 0.0)
        dx_ref[...] = jnp.where(_valid(i * tr, tr), acc, 0.0).astype(dx_ref.dtype)
        db = jnp.sum(dy, axis=0, keepdims=True)

        @pl.when(i == 0)
        def _():
            dw_ref[...] = dw
            db_ref[...] = db

        @pl.when(i > 0)
        def _():
            dw_ref[...] += dw
            db_ref[...] += db

    row = pl.BlockSpec((tr, cw), lambda j, i: (i, j))
    return pl.pallas_call(
        body, name=name, grid=(c // cw, nb),
        in_specs=[pl.BlockSpec((tr, cw), lambda j, i: (i, c0 + j)),
                  pl.BlockSpec((HALO, cw), lambda j, i: (jnp.maximum(i * hb - 1, 0), c0 + j)),
                  pl.BlockSpec((CONV_K, cw), lambda j, i: (0, j)),
                  row, pl.BlockSpec((HALO, cw), lambda j, i: (jnp.minimum((i + 1) * hb, t // HALO - 1), j))],
        out_specs=[row, pl.BlockSpec((CONV_K, cw), lambda j, i: (0, j)), pl.BlockSpec((1, cw), lambda j, i: (0, j))],
        out_shape=[jax.ShapeDtypeStruct((t, c), BF16), jax.ShapeDtypeStruct((CONV_K, c), F32), jax.ShapeDtypeStruct((1, c), F32)],
        scratch_shapes=[pltpu.VMEM((tr + HALO, cw), F32), pltpu.VMEM((tr + HALO, cw), F32)],
        compiler_params=_params(("parallel", "arbitrary")),
    )(x, x, w, dy, dy)


SUB = 8


def _lru_scan(a, u, name):
    t, c = a.shape
    tr = _tile(t, 528)

    def body(a_ref, u_ref, o_ref, carry):
        @pl.when(pl.program_id(0) == 0)
        def _():
            carry[...] = jnp.zeros_like(carry)

        rows = lax.broadcasted_iota(jnp.int32, (SUB, 1), 0)

        def step(k, cin):
            r = pl.multiple_of(k * SUB, SUB)
            av, uv = a_ref[pl.ds(r, SUB), :], u_ref[pl.ds(r, SUB), :]
            for d in (1, 2, 4):
                m = rows >= d
                uv = uv + av * jnp.where(m, pltpu.roll(uv, d, 0), 0.0)
                av = av * jnp.where(m, pltpu.roll(av, d, 0), 1.0)
            hv = uv + av * cin
            o_ref[pl.ds(r, SUB), :] = hv
            return jnp.broadcast_to(hv[SUB - 1:SUB, :], (SUB, c))

        carry[...] = lax.fori_loop(0, tr // SUB, step, carry[...])

    row = pl.BlockSpec((tr, c), lambda i: (i, 0))
    return pl.pallas_call(
        body, name=name, grid=(t // tr,), in_specs=[row, row], out_specs=row,
        out_shape=jax.ShapeDtypeStruct((t, c), F32), scratch_shapes=[pltpu.VMEM((SUB, c), F32)],
        compiler_params=_params(("arbitrary",)),
    )(a, u)


def _lru_scan_bwd(a, hs, dy, name):
    t, c = a.shape
    tr = _tile(t, 528)
    nb, nt = t // tr, tr // SUB

    def body(a_ref, h_ref, hh_ref, dy_ref, du_ref, da_ref, gcar, acar):
        i = pl.program_id(0)

        @pl.when(i == 0)
        def _():
            gcar[...] = jnp.zeros_like(gcar)
            acar[...] = jnp.zeros_like(acar)

        rows = lax.broadcasted_iota(jnp.int32, (SUB, 1), 0)
        hhalo = jnp.where(i < nb - 1, hh_ref[...], 0.0)

        def step(kk, car):
            gin, a_next_first = car
            k = nt - 1 - kk
            r = pl.multiple_of(k * SUB, SUB)
            av, hv, dv = a_ref[pl.ds(r, SUB), :], h_ref[pl.ds(r, SUB), :], dy_ref[pl.ds(r, SUB), :]
            rp = pl.multiple_of(jnp.maximum(k - 1, 0) * SUB, SUB)
            hp = jnp.where(k > 0, h_ref[pl.ds(rp, SUB), :], hhalo)
            cv = jnp.where(rows < SUB - 1, pltpu.roll(av, SUB - 1, 0), a_next_first)
            gv = dv
            for d in (1, 2, 4):
                m = rows < SUB - d
                gv = gv + cv * jnp.where(m, pltpu.roll(gv, SUB - d, 0), 0.0)
                cv = cv * jnp.where(m, pltpu.roll(cv, SUB - d, 0), 1.0)
            gv = gv + cv * gin
            hprev = jnp.where(rows >= 1, pltpu.roll(hv, 1, 0), jnp.broadcast_to(hp[SUB - 1:SUB, :], (SUB, c)))
            du_ref[pl.ds(r, SUB), :] = gv
            da_ref[pl.ds(r, SUB), :] = gv * hprev
            return jnp.broadcast_to(gv[0:1, :], (SUB, c)), jnp.broadcast_to(av[0:1, :], (SUB, c))

        g, af = lax.fori_loop(0, nt, step, (gcar[...], acar[...]))
        gcar[...] = g
        acar[...] = af

    hb = tr // SUB
    row = pl.BlockSpec((tr, c), lambda i: (nb - 1 - i, 0))
    halo = pl.BlockSpec((SUB, c), lambda i: (jnp.maximum((nb - 1 - i) * hb - 1, 0), 0))
    return pl.pallas_call(
        body, name=name, grid=(nb,), in_specs=[row, row, halo, row], out_specs=[row, row],
        out_shape=[jax.ShapeDtypeStruct((t, c), F32)] * 2,
        scratch_shapes=[pltpu.VMEM((SUB, c), F32), pltpu.VMEM((SUB, c), F32)],
        compiler_params=_params(("arbitrary",)),
    )(a, hs, hs, dy)


def _lru_gates(row0, xr, wa, ba, wi, bi, lam):
    r = jax.nn.sigmoid(_bnn(xr, wa) + ba)
    i = jax.nn.sigmoid(_bnn(xr, wi) + bi)
    log_a = -LRU_C * r * _softplus(-lam)
    u = jnp.sqrt(_neg_expm1(2.0 * log_a)) * (i * xr)
    return jnp.exp(log_a), jnp.where(_valid(row0, xr.shape[0]), u, 0.0)


def _lru_gate_out(row0, hs, yw):
    return (hs * jax.nn.gelu(yw),)


def _rglru_fwd(h, p, l, o):
    hn = _prenorm(h, p["mix_pre_g"][l], "rg_prenorm")
    xw = _mm(hn, p["rg_w_x"][o], "nn", "rg_in_x")
    yw = _mm(hn, p["rg_w_y"][o], "nn", "rg_in_y")
    xr = _conv_fwd(xw, p["rg_conv_w"][o], p["rg_conv_b"][o], "rg_conv", cw=LRU_WIDTH // 2)
    gp = [p["rg_w_a"][o], p["rg_b_a"][o], p["rg_w_i"][o], p["rg_b_i"][o], p["rg_lambda"][o]]
    a, u = _rowwise("rg_gates", _lru_gates, [xr], gp, [(LRU_WIDTH, F32)] * 2, ncol=LRU_BLOCKS, tr=_tile(h.shape[0], 1056))
    hs = _lru_scan(a, u, "rg_scan")
    hg = _rowwise("rg_gate_out", _lru_gate_out, [hs, yw], [], [(LRU_WIDTH, BF16)])[0]
    m, h2 = _mm(hg, p["rg_w_out"][o], "nn", "rg_out", extra=[h], vecs=[p["mix_post_g"][l]], post=_post_residual, out_dtypes=[F32, F32])
    return h2, (h, hn, xw, yw, xr, a, hs, hg, m)


def _rglru_bwd(dh, saved, p, l, o, grads):
    h, hn, xw, yw, xr, a, hs, hg, m = saved
    dm, grads["mix_post_g"][l] = _postnorm_bwd(m, p["mix_post_g"][l], dh, "rg_postnorm_bwd")
    grads["rg_w_out"][o] = _mm(hg, dm, "tn", "rg_out_dw")
    dhg = _mm(dm, p["rg_w_out"][o], "nt", "rg_out_dx")
    (dhs, dyw), _ = _rowwise_vjp("rg_gate_out_bwd", _lru_gate_out, [hs, yw], [], [dhg], row_dtypes=[F32, BF16])
    du, da = _lru_scan_bwd(a, hs, dhs, "rg_scan_bwd")
    gp = [p["rg_w_a"][o], p["rg_b_a"][o], p["rg_w_i"][o], p["rg_b_i"][o], p["rg_lambda"][o]]
    (dxr,), gg = _rowwise_vjp("rg_gates_bwd", _lru_gates, [xr], gp, [da, du], ncol=LRU_BLOCKS, tr=_tile(h.shape[0], 1056))
    grads["rg_w_a"][o], grads["rg_b_a"][o], grads["rg_w_i"][o], grads["rg_b_i"][o], grads["rg_lambda"][o] = gg
    dxw, grads["rg_conv_w"][o], grads["rg_conv_b"][o] = _conv_bwd(xw, p["rg_conv_w"][o], dxr, "rg_conv_bwd", cw=LRU_WIDTH // 2)
    grads["rg_w_x"][o] = _mm(hn, dxw, "tn", "rg_in_x_dw")
    grads["rg_w_y"][o] = _mm(hn, dyw, "tn", "rg_in_y_dw")
    dhx = _mm(dxw, p["rg_w_x"][o], "nt", "rg_in_x_dx")
    dhy = _mm(dyw, p["rg_w_y"][o], "nt", "rg_in_y_dx")
    dh, grads["mix_pre_g"][l] = _prenorm_bwd_add(h, p["mix_pre_g"][l], [dhx, dhy], dh, "rg_prenorm_bwd")
    return dh


SSD_GW = SSD_D_INNER // SSD_GROUPS
SSD_GH = SSD_HEADS // SSD_GROUPS
XACT_B = SSD_D_INNER // SSD_STATE
XACT_C = XACT_B + SSD_GROUPS


def _hp(a, b, dims=_DIMS["nn"]):
    return lax.dot_general(a, b, dims, precision=lax.Precision.HIGHEST, preferred_element_type=F32)


def _split_dot(a, e, mode, parts):
    eb = e.astype(BF16)
    out, rest = None, a
    for _ in range(parts):
        term = rest.astype(BF16)
        rest = rest - term.astype(F32)
        if mode in ("nn", "nt"):
            prod = lax.dot_general(term, eb, _DIMS[mode], preferred_element_type=F32)
        else:
            prod = lax.dot_general(eb, term, _DIMS["nn" if mode == "left" else "tn"], preferred_element_type=F32)
        out = prod if out is None else out + prod
    return out


@jax.custom_vjp
def _select_nn(a, e):
    return _split_dot(a, e, "nn", 3)


_select_nn.defvjp(lambda a, e: (_split_dot(a, e, "nn", 3), e), lambda e, ct: (_split_dot(ct, e, "nt", 2), jnp.zeros_like(e)))


@jax.custom_vjp
def _select_left(e, a):
    return _split_dot(a, e, "left", 3)


_select_left.defvjp(lambda e, a: (_split_dot(a, e, "left", 3), e),
                    lambda e, ct: (jnp.zeros_like(e), _split_dot(ct, e, "left_t", 2)))


def _ssd_chunk(xs, bm, cm, dt, da, ht, g):
    l = CHUNK
    ri = lax.broadcasted_iota(jnp.int32, (l, l), 0)
    ci = lax.broadcasted_iota(jnp.int32, (l, l), 1)
    causal = ri >= ci
    tri = causal.astype(F32)
    hr = lax.broadcasted_iota(jnp.int32, (LANES, SSD_GW), 0)
    hc = lax.broadcasted_iota(jnp.int32, (LANES, SSD_GW), 1)
    expand = (hr == g * SSD_GH + hc // SSD_HEAD_DIM).astype(F32)
    acs = _select_left(tri, da)
    acs_t = acs.T
    acs_e = _select_nn(acs, expand)
    x = xs * _select_nn(dt, expand)
    gmat = _bnt(cm, bm)
    lane = lax.broadcasted_iota(jnp.int32, (1, LANES), 1)
    sub = lax.broadcasted_iota(jnp.int32, (LANES, 1), 0)
    colhead = lax.broadcasted_iota(jnp.int32, (1, SSD_GW), 1) // SSD_HEAD_DIM
    y = _bnn(cm, ht) * jnp.exp(acs_e)
    for k in range(SSD_GH):
        hh = g * SSD_GH + k
        col = jnp.sum(jnp.where(lane == hh, acs, 0.0), axis=1, keepdims=True)
        row = jnp.sum(jnp.where(sub == hh, acs_t, 0.0), axis=0, keepdims=True)
        decay = jnp.exp(jnp.where(causal, col - row, -1e30))
        y = y + _bnn(gmat * decay, jnp.where(colhead == k, x, 0.0))
    last = lax.broadcasted_iota(jnp.int32, (l, 1), 0) == l - 1
    a_last = jnp.sum(jnp.where(last, acs_e, 0.0), axis=0, keepdims=True)
    st = _btn(bm, x * jnp.exp(a_last - acs_e))
    return y, ht * jnp.exp(a_last) + st


def _ssd_specs(nc, rev):
    def cc(c):
        return nc - 1 - c if rev else c

    return [pl.BlockSpec((CHUNK, SSD_GW), lambda c, g: (cc(c), g)),
            pl.BlockSpec((CHUNK, SSD_STATE), lambda c, g: (cc(c), XACT_B + g)),
            pl.BlockSpec((CHUNK, SSD_STATE), lambda c, g: (cc(c), XACT_C + g)),
            pl.BlockSpec((CHUNK, LANES), lambda c, g: (cc(c), 0)),
            pl.BlockSpec((CHUNK, LANES), lambda c, g: (cc(c), 0))]


def _ssd_scan(xact, dt, da, name):
    t = xact.shape[0]
    nc = t // CHUNK

    def body(xs_ref, b_ref, c_ref, dt_ref, da_ref, y_ref, hs_ref, state):
        c, g = pl.program_id(0), pl.program_id(1)

        @pl.when(c == 0)
        def _():
            state[g] = jnp.zeros((SSD_STATE, SSD_GW), F32)

        ht = state[g]
        hs_ref[0] = ht
        y, ht2 = _ssd_chunk(xs_ref[...], b_ref[...], c_ref[...], dt_ref[...], da_ref[...], ht, g)
        y_ref[...] = y
        state[g] = ht2

    return pl.pallas_call(
        body, name=name, grid=(nc, SSD_GROUPS), in_specs=_ssd_specs(nc, False),
        out_specs=[pl.BlockSpec((CHUNK, SSD_GW), lambda c, g: (c, g)),
                   pl.BlockSpec((1, SSD_STATE, SSD_GW), lambda c, g: (c * SSD_GROUPS + g, 0, 0))],
        out_shape=[jax.ShapeDtypeStruct((t, SSD_D_INNER), F32), jax.ShapeDtypeStruct((nc * SSD_GROUPS, SSD_STATE, SSD_GW), F32)],
        scratch_shapes=[pltpu.VMEM((SSD_GROUPS, SSD_STATE, SSD_GW), F32)],
        compiler_params=_params(("arbitrary", "arbitrary")),
    )(xact, xact, xact, dt, da)


def _ssd_scan_bwd(xact, dt, da, hsave, dy, dxskip, name):
    t = xact.shape[0]
    nc = t // CHUNK

    def body(xs_ref, b_ref, c_ref, dt_ref, da_ref, hs_ref, dy_ref, sk_ref, dxs_ref, db_ref, dc_ref, ddt_ref, dda_ref, dstate):
        c, g = pl.program_id(0), pl.program_id(1)

        @pl.when(c == 0)
        def _():
            dstate[g] = jnp.zeros((SSD_STATE, SSD_GW), F32)

        _, vjp = jax.vjp(lambda *a: _ssd_chunk(*a, g), xs_ref[...], b_ref[...], c_ref[...], dt_ref[...], da_ref[...], hs_ref[0])
        dxs, dbm, dcm, ddt, dda, dht = vjp((dy_ref[...], dstate[g]))
        dxs_ref[...] = dxs + sk_ref[...]
        db_ref[...] = dbm
        dc_ref[...] = dcm
        dstate[g] = dht

        @pl.when(g == 0)
        def _():
            ddt_ref[...] = ddt
            dda_ref[...] = dda

        @pl.when(g > 0)
        def _():
            ddt_ref[...] += ddt
            dda_ref[...] += dda

    grp = pl.BlockSpec((CHUNK, SSD_GW), lambda c, g: (nc - 1 - c, g))
    st = pl.BlockSpec((CHUNK, SSD_STATE), lambda c, g: (nc - 1 - c, g))
    hd = pl.BlockSpec((CHUNK, LANES), lambda c, g: (nc - 1 - c, 0))
    return pl.pallas_call(
        body, name=name, grid=(nc, SSD_GROUPS),
        in_specs=_ssd_specs(nc, True) + [pl.BlockSpec((1, SSD_STATE, SSD_GW), lambda c, g: ((nc - 1 - c) * SSD_GROUPS + g, 0, 0)), grp, grp],
        out_specs=[grp, st, st, hd, hd],
        out_shape=[jax.ShapeDtypeStruct((t, SSD_D_INNER), F32), jax.ShapeDtypeStruct((t, SSD_GROUPS * SSD_STATE), F32),
                   jax.ShapeDtypeStruct((t, SSD_GROUPS * SSD_STATE), F32), jax.ShapeDtypeStruct((t, LANES), F32),
                   jax.ShapeDtypeStruct((t, LANES), F32)],
        scratch_shapes=[pltpu.VMEM((SSD_GROUPS, SSD_STATE, SSD_GW), F32)],
        compiler_params=_params(("arbitrary", "arbitrary")),
    )(xact, xact, xact, dt, da, hsave, dy, dxskip)


def _ssd_act(row0, xc):
    return (jnp.where(_valid(row0, xc.shape[0]), jax.nn.silu(xc), 0.0),)


def _ssd_dt(row0, dtraw, dt_bias, a_log):
    dt = jnp.where(_valid(row0, dtraw.shape[0]), _softplus(dtraw + dt_bias), 0.0)
    return dt, dt * -jnp.exp(a_log)


def _ssd_post(row0, y, xs, z, d_skip, norm_g):
    hr = lax.broadcasted_iota(jnp.int32, (LANES, SSD_D_INNER), 0)
    hc = lax.broadcasted_iota(jnp.int32, (LANES, SSD_D_INNER), 1)
    expand = (hr == hc // SSD_HEAD_DIM).astype(F32)
    d_e = jnp.sum(_hp(jnp.broadcast_to(d_skip, (SUB, LANES)), expand), axis=0, keepdims=True) * (1.0 / SUB)
    return (_rms((y + xs * d_e) * jax.nn.silu(z), norm_g),)


ROPE_LO, ROPE_MID, ROPE_HI = MLA_NOPE, MLA_NOPE + MLA_ROPE // 2, MLA_NOPE + MLA_ROPE
ATT_SCALE = (MLA_NOPE + MLA_ROPE) ** -0.5


def _slot_lane(width):
    return lax.broadcasted_iota(jnp.int32, (1, width), 1) % LANES


def _swap_halves(x):
    width = x.shape[1]
    lane = _slot_lane(width)
    sw = jnp.where(lane < ROPE_MID, pltpu.roll(x, width - MLA_ROPE // 2, 1), pltpu.roll(x, MLA_ROPE // 2, 1))
    return jnp.where((lane >= ROPE_LO) & (lane < ROPE_HI), sw, 0.0)


def _rope(x, cos, sin):
    n = x.shape[1] // LANES
    return x * jnp.tile(cos, (1, n)) + _swap_halves(x) * jnp.tile(sin, (1, n))


def _rope_t(dy, cos, sin):
    n = dy.shape[1] // LANES
    return dy * jnp.tile(cos, (1, n)) + _swap_halves(dy * jnp.tile(sin, (1, n)))


ATT_SCALE2 = ATT_SCALE * float(np.log2(np.e))
MASKED = -1e30


def _att_bias(blk):
    r = jnp.arange(blk)[:, None]
    c = jnp.arange(blk)[None, :]
    zero = jnp.zeros((blk, blk), F32)
    first = jnp.where(c >= PAD, 0.0, MASKED) + zero
    diag = jnp.where(c <= r, 0.0, MASKED).astype(F32)
    return jnp.stack([zero, first, diag, jnp.minimum(first, diag), zero + MASKED])


def _att_bias_index(j, i):
    return jnp.where(j > i, 4, jnp.where(j == 0, 1, 0) + jnp.where(j == i, 2, 0))


def _key_slots(row0, kv, kr):
    width = kv.shape[1]
    return jnp.where(_slot_lane(width) < MLA_NOPE, kv, jnp.tile(kr, (1, width // LANES))), kv


def _attn_fwd(qr, km, vb, name, carried=None):
    t = qr.shape[0]
    blk = _tile(t, 384, LANES)
    nq = t // blk

    bias = _att_bias(blk)

    def body(q_ref, k_ref, v_ref, b_ref, o_ref, s0, s1, p0, p1):
        i = pl.program_id(1)
        lane = lax.broadcasted_iota(jnp.int32, (1, LANES), 1)
        qb = q_ref[...]

        def rows(j):
            return pl.ds(pl.multiple_of(jnp.clip(j, 0, i) * blk, blk), blk)

        def scores(j):
            return lax.dot_general(qb, k_ref[rows(j), :], _DIMS["nt"], preferred_element_type=F32) + b_ref[_att_bias_index(j, i)]

        def half(j, car, s_cur, s_nxt, p_cur, p_prv):
            m, l, acc, al_prev = car
            s_nxt[...] = scores(j + 1)
            acc2 = al_prev * acc + lax.dot_general(p_prv[...], v_ref[rows(j - 1), :], _DIMS["nn"], preferred_element_type=F32)
            m2 = jnp.maximum(m, jnp.max(s_cur[...], axis=1, keepdims=True))
            al = jnp.exp2((m - m2) * ATT_SCALE2)
            pm = jnp.exp2(s_cur[...] * ATT_SCALE2 - m2 * ATT_SCALE2)
            p_cur[...] = pm.astype(BF16)
            return m2, al * l + jnp.sum(pm, axis=1, keepdims=True), acc2, al

        def step(jj, car):
            car = half(2 * jj, car, s0, s1, p0, p1)
            return half(2 * jj + 1, car, s1, s0, p1, p0)

        s0[...] = scores(0)
        p1[...] = jnp.zeros((blk, blk), BF16)
        car = (jnp.full((blk, 1), MASKED, F32), jnp.zeros((blk, 1), F32), jnp.zeros((blk, LANES), F32), jnp.ones((blk, 1), F32))
        steps = i // 2 + 1
        m, l, acc, al_last = lax.fori_loop(0, steps, step, car)
        acc = al_last * acc + lax.dot_general(p1[...], v_ref[rows(2 * steps - 1), :], _DIMS["nn"], preferred_element_type=F32)
        out = jnp.where(lane >= MLA_NOPE, acc / l, m * ATT_SCALE + jnp.log(l))
        o_ref[...] = jnp.where(_valid(i * blk, blk), out, 0.0)

    seq_h = pl.BlockSpec((t, LANES), lambda h, i: (0, h))
    (o,), carried_out = _carry_call(
        body, name, (MLA_HEADS, nq),
        [pl.BlockSpec((blk, LANES), lambda h, i: (i, h)), seq_h, seq_h, _full_spec(bias.shape, 2)],
        [pl.BlockSpec((blk, LANES), lambda h, i: (i, h))], [jax.ShapeDtypeStruct((t, MLA_HEADS * LANES), F32)],
        [pltpu.VMEM((blk, blk), F32)] * 2 + [pltpu.VMEM((blk, blk), BF16)] * 2, (qr, km, vb, bias), carried)
    return o, carried_out


def _attn_bwd(qr, km, vb, o, do, cos, sin, name, carried=None):
    t = qr.shape[0]
    blk = _tile(t, 384, LANES)
    nq = t // blk

    bias = _att_bias(blk)
    log2e = float(np.log2(np.e))

    def body(q_ref, o_ref, do_ref, k_ref, v_ref, b_ref, cos_ref, sin_ref, dq_out, dkv_ref, dkr_ref,
             s0, s1, dp0, dp1, p0, p1, ds0, ds1, dk_s, dv_s, dq_ref):
        h, j = pl.program_id(0), pl.program_id(1)
        lane = lax.broadcasted_iota(jnp.int32, (1, LANES), 1)

        @pl.when(j == 0)
        def _():
            dq_ref[...] = jnp.zeros_like(dq_ref)

        @pl.when((h == 0) & (j == 0))
        def _():
            dkr_ref[...] = jnp.zeros_like(dkr_ref)

        kmat, vmat = k_ref[...], v_ref[...]

        def rows(i):
            return pl.ds(pl.multiple_of(jnp.clip(i, j, nq - 1) * blk, blk), blk)

        def first_stage(i, s_buf, dp_buf):
            ic = jnp.minimum(i, nq - 1)
            s_buf[...] = lax.dot_general(q_ref[rows(ic), :], kmat, _DIMS["nt"], preferred_element_type=F32) + b_ref[_att_bias_index(j, ic)]
            dp_buf[...] = lax.dot_general(do_ref[rows(ic), :].astype(BF16), vmat, _DIMS["nt"], preferred_element_type=F32)

        def middle_stage(i, s_buf, dp_buf, p_buf, ds_buf):
            r = rows(i)
            ob, dob = o_ref[r, :], do_ref[r, :].astype(F32)
            delta = jnp.sum(dob * ob, axis=1, keepdims=True)
            pm = jnp.exp2(s_buf[...] * ATT_SCALE2 - ob[:, 0:1] * log2e)
            p_buf[...] = pm.astype(BF16)
            ds_buf[...] = (pm * (dp_buf[...] - delta) * ATT_SCALE).astype(BF16)

        def last_stage(i, p_buf, ds_buf):
            r = rows(i)
            dv_s[...] += lax.dot_general(p_buf[...], do_ref[r, :].astype(BF16), _DIMS["tn"], preferred_element_type=F32)
            dk_s[...] += lax.dot_general(ds_buf[...], q_ref[r, :], _DIMS["tn"], preferred_element_type=F32)
            dq_ref[r, :] += lax.dot_general(ds_buf[...], kmat, _DIMS["nn"], preferred_element_type=F32)

        n = nq - j
        dk_s[...] = jnp.zeros((blk, LANES), F32)
        dv_s[...] = jnp.zeros((blk, LANES), F32)
        first_stage(j, s0, dp0)
        first_stage(j + 1, s1, dp1)
        middle_stage(j, s0, dp0, p0, ds0)

        def step(tt, carry):
            i = j + 2 * tt + 1
            first_stage(i + 1, s0, dp0)
            last_stage(i - 1, p0, ds0)
            middle_stage(i, s1, dp1, p1, ds1)
            first_stage(i + 2, s1, dp1)
            last_stage(i, p1, ds1)
            middle_stage(i + 1, s0, dp0, p0, ds0)
            return carry

        lax.fori_loop(0, (n - 1) // 2, step, 0)

        @pl.when(n % 2 == 0)
        def _():
            last_stage(nq - 2, p0, ds0)
            middle_stage(nq - 1, s1, dp1, p1, ds1)
            last_stage(nq - 1, p1, ds1)

        @pl.when(n % 2 == 1)
        def _():
            last_stage(nq - 1, p0, ds0)

        dk = dk_s[...]
        dkv_ref[...] = jnp.where(lane < MLA_NOPE, dk, dv_s[...]).astype(dkv_ref.dtype)
        dkr_ref[rows(j), :] += jnp.where(lane >= MLA_NOPE, dk, 0.0)

        @pl.when(j == nq - 1)
        def _():
            dq_out[...] = _rope_t(dq_ref[...], cos_ref[...], sin_ref[...]).astype(dq_out.dtype)

    seq_h = pl.BlockSpec((t, LANES), lambda h, j: (0, h))
    seq = pl.BlockSpec((t, LANES), lambda h, j: (0, 0))
    blk_h = pl.BlockSpec((blk, LANES), lambda h, j: (j, h))
    return _carry_call(
        body, name, (MLA_HEADS, nq), [seq_h, seq_h, seq_h, blk_h, blk_h, _full_spec(bias.shape, 2), seq, seq],
        [seq_h, blk_h, seq],
        [jax.ShapeDtypeStruct((t, MLA_HEADS * LANES), BF16), jax.ShapeDtypeStruct((t, MLA_HEADS * LANES), BF16),
         jax.ShapeDtypeStruct((t, LANES), F32)],
        [pltpu.VMEM((blk, blk), F32)] * 4 + [pltpu.VMEM((blk, blk), BF16)] * 4 + [pltpu.VMEM((blk, LANES), F32)] * 2
        + [pltpu.VMEM((t, LANES), F32)], (qr, o, do, km, vb, bias, cos, sin), carried)


def _rms_rows(row0, x, g):
    return (_rms(x, g),)


def _ssdmla_fwd(h, p, l, e, cos, sin, carried=None):
    hn = _prenorm(h, p["mix_pre_g"][l], "sm_prenorm")
    proj = _mm(hn, p["w_in"][e], "nn", "sm_in")
    xc = _conv_fwd(proj, p["ssd_conv_w"][e], p["ssd_conv_b"][e], "ssd_conv", cw=SSD_GW, c0=PROJ_XBC // SSD_GW)
    xact = _rowwise("ssd_act", _ssd_act, [xc], [], [(SSD_CONV_CH, F32)])[0]
    dt, da = _rowwise("ssd_dt", _ssd_dt, [(proj, LANES, PROJ_DT // LANES)], [p["ssd_dt_bias"][e], p["ssd_a_log"][e]],
                      [(LANES, F32)] * 2)
    y, hsave = _ssd_scan(xact, dt, da, "ssd_scan")
    y_ssd = _rowwise("ssd_post", _ssd_post, [y, (xact, SSD_D_INNER, 0), (proj, SSD_D_INNER, 0)],
                     [p["ssd_d"][e], p["ssd_norm_g"][e]], [(SSD_D_INNER, BF16)])[0]
    cqn = _prenorm((proj, MLA_Q_RANK, PROJ_CQ // MLA_Q_RANK), p["mla_q_norm_g"][e], "mla_qnorm")
    ckvn = _prenorm((proj, MLA_KV_RANK, PROJ_CKV // MLA_KV_RANK), p["mla_kv_norm_g"][e], "mla_kvnorm")
    kr = _rowwise("mla_krope", lambda r0, x, c, s: (_rope(x, c, s),), [(proj, LANES, PROJ_KR // LANES), cos, sin], [],
                  [(LANES, F32)])[0]
    qr = _mm(cqn, p["mla_w_q_up"][e], "nn", "mla_q_up", slots=[cos, sin], post=lambda v, c, s: (_rope(v, c, s),), out_dtypes=[BF16])
    km, vb = _mm(ckvn, p["mla_w_kv_up"][e], "nn", "mla_kv_up", slots=[kr], post=lambda v, k: _key_slots(0, v, k),
                 out_dtypes=[BF16, BF16])
    o, carried_out = _attn_fwd(qr, km, vb, "mla_attn", carried)
    m1 = _mm(y_ssd, p["w_out_ssd"][e], "nn", "sm_out_ssd")
    m, h2 = _mm(o, p["w_out_att"][e], "nn", "sm_out_att", extra=[m1, h], vecs=[p["mix_post_g"][l]],
                post=lambda v, m1b, hb, g: _post_residual(v + m1b, hb, g), out_dtypes=[F32, F32])
    return h2, (h, hn, proj, xc, xact, dt, da, y, hsave, y_ssd, cqn, ckvn, qr, km, vb, o, m), carried_out


def _ssdmla_bwd(dh, saved, p, l, e, cos, sin, grads, carry=None):
    h, hn, proj, xc, xact, dt, da, y, hsave, y_ssd, cqn, ckvn, qr, km, vb, o, m = saved
    dm, grads["mix_post_g"][l] = _postnorm_bwd(m, p["mix_post_g"][l], dh, "sm_postnorm_bwd")
    grads["w_out_ssd"][e] = _mm(y_ssd, dm, "tn", "sm_out_ssd_dw")
    grads["w_out_att"][e] = _mm(o, dm, "tn", "sm_out_att_dw")
    dy_ssd = _mm(dm, p["w_out_ssd"][e], "nt", "sm_out_ssd_dx")
    do = _mm(dm, p["w_out_att"][e], "nt", "sm_out_att_dx", out_dtype=BF16)
    (dq, dkv, dkr), carried_out = _attn_bwd(qr, km, vb, o, do, cos, sin, "mla_attn_bwd", carry() if carry else None)
    dkr_raw = _rowwise("mla_krope_bwd", lambda r0, d, c, s: (_rope_t(d, c, s),), [dkr, cos, sin], [], [(LANES, F32)])[0]
    grads["mla_w_q_up"][e] = _mm(cqn, dq, "tn", "mla_q_up_dw")
    dcqn = _mm(dq, p["mla_w_q_up"][e], "nt", "mla_q_up_dx")
    (dcq,), (grads["mla_q_norm_g"][e],) = _rowwise_vjp(
        "mla_qnorm_bwd", _rms_rows, [(proj, MLA_Q_RANK, PROJ_CQ // MLA_Q_RANK)], [p["mla_q_norm_g"][e]], [dcqn])
    grads["mla_w_kv_up"][e] = _mm(ckvn, dkv, "tn", "mla_kv_up_dw")
    dckvn = _mm(dkv, p["mla_w_kv_up"][e], "nt", "mla_kv_up_dx")
    (dckv,), (grads["mla_kv_norm_g"][e],) = _rowwise_vjp(
        "mla_kvnorm_bwd", _rms_rows, [(proj, MLA_KV_RANK, PROJ_CKV // MLA_KV_RANK)], [p["mla_kv_norm_g"][e]], [dckvn])
    (dy, dxskip, dz), (grads["ssd_d"][e], grads["ssd_norm_g"][e]) = _rowwise_vjp(
        "ssd_post_bwd", _ssd_post, [y, (xact, SSD_D_INNER, 0), (proj, SSD_D_INNER, 0)], [p["ssd_d"][e], p["ssd_norm_g"][e]], [dy_ssd])
    dxs, db, dc, ddt, dda = _ssd_scan_bwd(xact, dt, da, hsave, dy, dxskip, "ssd_scan_bwd")
    dxact = jnp.concatenate([dxs, db, dc], axis=1)
    (dxc,), _ = _rowwise_vjp("ssd_act_bwd", _ssd_act, [xc], [], [dxact])
    dxbc, grads["ssd_conv_w"][e], grads["ssd_conv_b"][e] = _conv_bwd(
        proj, p["ssd_conv_w"][e], dxc, "ssd_conv_bwd", cw=SSD_GW, c0=PROJ_XBC // SSD_GW)
    (ddtraw,), (grads["ssd_dt_bias"][e], grads["ssd_a_log"][e]) = _rowwise_vjp(
        "ssd_dt_bwd", _ssd_dt, [(proj, LANES, PROJ_DT // LANES)], [p["ssd_dt_bias"][e], p["ssd_a_log"][e]], [ddt, dda])
    dproj = jnp.concatenate([dz, dxbc, ddtraw, dcq, dckv, dkr_raw], axis=1).astype(BF16)
    grads["w_in"][e] = _mm(hn, dproj, "tn", "sm_in_dw")
    dhn = _mm(dproj, p["w_in"][e], "nt", "sm_in_dx")
    dh, grads["mix_pre_g"][l] = _prenorm_bwd_add(h, p["mix_pre_g"][l], [dhn], dh, "sm_prenorm_bwd")
    return dh, carried_out


GAINS = ("mix_pre_g", "mix_post_g", "mlp_pre_g", "mlp_post_g", "ssd_norm_g", "mla_q_norm_g", "mla_kv_norm_g", "ssd_conv_b", "rg_conv_b")
HEAD_VECS = ("ssd_dt_bias", "ssd_a_log", "ssd_d")
LRU_VECS = ("rg_b_a", "rg_b_i", "rg_lambda")
IN_DT_END = SSD_D_INNER + SSD_CONV_CH + SSD_HEADS
IN_KR = IN_DT_END + MLA_Q_RANK + MLA_KV_RANK


def _each(a, f):
    layers = a if isinstance(a, list) else [a[i] for i in range(a.shape[0])]
    return [None if x is None else f(x) for x in layers]


def _layout_params(w):
    p = {k: _each(w[k], lambda a: a[None, :]) for k in GAINS}
    for k in HEAD_VECS:
        p[k] = _each(w[k], lambda a: jnp.pad(a, (0, LANES - SSD_HEADS))[None, :])
    for k in LRU_VECS:
        p[k] = _each(w[k], lambda a: a.reshape(LRU_BLOCKS, 1, LRU_BLOCK))
    for k in ("w_up", "w_down", "mla_w_kv_up", "rg_w_x", "rg_w_y", "rg_w_out"):
        p[k] = _each(w[k], lambda a: a if isinstance(a, Gathered) else a.astype(BF16))
    for k in ("ssd_conv_w", "rg_conv_w", "rg_w_a", "rg_w_i"):
        p[k] = _each(w[k], lambda a: a)

    def w_in(a):
        def zcols(n):
            return jnp.zeros((a.shape[0], n), a.dtype)

        return jnp.concatenate([a[:, :IN_DT_END], zcols(PROJ_CQ - IN_DT_END), a[:, IN_DT_END:IN_KR], zcols(ROPE_LO),
                                a[:, IN_KR:], zcols(LANES - ROPE_HI)], axis=1).astype(BF16)

    def q_up(a):
        a = a.reshape(MLA_Q_RANK, MLA_HEADS, MLA_NOPE + MLA_ROPE)
        return jnp.pad(a, ((0, 0), (0, 0), (0, LANES - MLA_NOPE - MLA_ROPE))).reshape(MLA_Q_RANK, MLA_HEADS * LANES).astype(BF16)

    def out_att(a):
        a = a[SSD_D_INNER:].reshape(MLA_HEADS, MLA_V, D_MODEL)
        return jnp.pad(a, ((0, 0), (LANES - MLA_V, 0), (0, 0))).reshape(MLA_HEADS * LANES, D_MODEL).astype(BF16)

    p["w_in"] = _each(w["w_in"], w_in)
    p["mla_w_q_up"] = _each(w["mla_w_q_up"], q_up)
    p["w_out_ssd"] = _each(w["w_out_ab"], lambda a: a[:SSD_D_INNER].astype(BF16))
    p["w_out_att"] = _each(w["w_out_ab"], out_att)
    return p


def _rope_tables(t):
    pos = (jnp.arange(t) - PAD).astype(F32)
    inv = ROPE_BASE ** (-jnp.arange(0, MLA_ROPE, 2, dtype=F32) / MLA_ROPE)
    ang = pos[:, None] * inv[None, :]
    c, s = jnp.cos(ang), jnp.sin(ang)
    one, zero = jnp.ones((t, MLA_NOPE), F32), jnp.zeros((t, MLA_NOPE), F32)
    tail = LANES - ROPE_HI
    return (jnp.concatenate([one, c, c, one[:, :tail]], axis=1), jnp.concatenate([zero, -s, s, zero[:, :tail]], axis=1))


GRAD_KEYS = GAINS + HEAD_VECS + LRU_VECS + ("w_up", "w_down", "mla_w_kv_up", "rg_w_x", "rg_w_y", "rg_w_out", "ssd_conv_w",
                                            "rg_conv_w", "rg_w_a", "rg_w_i", "w_in", "mla_w_q_up", "w_out_ssd", "w_out_att")


def _device_step(x, meta, target, p, hooks=None):
    t = PAD + N_META + x.shape[0]
    cos, sin = _rope_tables(t)
    h = jnp.concatenate([jnp.zeros((PAD, D_MODEL), F32), meta, x], axis=0)
    n_even, n_odd = (DEPTH + 1) // 2, DEPTH // 2
    saved = []
    for l in range(DEPTH):
        if l % 2 == 0:
            carried = hooks.forward_exchange() if hooks and l == 0 else None
            h, sm, arrived = _ssdmla_fwd(h, p, l, l // 2, cos, sin, carried)
            if carried is not None:
                p = hooks.after_forward_exchange(arrived)
        else:
            h, sm = _rglru_fwd(h, p, l, l // 2)
        h, sp = _mlp_fwd(h, p, l)
        saved.append((sm, sp))
    sq, dh = _loss_and_grad(h, target, "loss")
    per_layer = {"mix_pre_g": DEPTH, "mix_post_g": DEPTH, "mlp_pre_g": DEPTH, "mlp_post_g": DEPTH, "w_up": DEPTH, "w_down": DEPTH}
    grads = {k: [None] * per_layer.get(k, n_odd if k.startswith("rg_") else n_even) for k in GRAD_KEYS}
    for l in reversed(range(DEPTH)):
        sm, sp = saved[l]
        dh = _mlp_bwd(dh, sp, p, l, grads)
        if l % 2 == 0:
            carry = functools.partial(hooks.backward_exchange, grads, l) if hooks else None
            dh, arrived = _ssdmla_bwd(dh, sm, p, l, l // 2, cos, sin, grads, carry)
            if hooks:
                hooks.after_backward_exchange(arrived, l)
        else:
            dh = _rglru_bwd(dh, sm, p, l, l // 2, grads)
    return sq, dh, grads


MESH = pl.DeviceIdType.MESH
ANY = pl.BlockSpec(memory_space=pl.ANY)


def _mesh_pos():
    return lax.axis_index("x"), lax.axis_index("y"), lax.axis_index("c")


def _other_chips(x, y):
    return [(1 - x, y), (x, 1 - y), (1 - x, 1 - y)]


def _remote(src, dst, send_sems, recv_sems, k, to):
    return pltpu.make_async_remote_copy(src_ref=src, dst_ref=dst, send_sem=send_sems.at[k], recv_sem=recv_sems.at[k],
                                        device_id=to, device_id_type=MESH)


class Exchange:
    def __init__(self, ins, outs, aliases, n_sems, plan):
        self.ins, self.outs, self.aliases, self.n_sems, self.plan = list(ins), list(outs), dict(aliases), n_sems, plan


def _sems(n):
    return [pltpu.SemaphoreType.DMA((n,)), pltpu.SemaphoreType.DMA((n,))]


def _run_exchange(name, ex):
    ni, no = len(ex.ins), len(ex.outs)

    def body(*refs):
        sends = ex.plan(refs[:ni], refs[ni:ni + no], refs[-2], refs[-1], False)
        for cp in sends:
            cp.start()
        for cp in ex.plan(refs[:ni], refs[ni:ni + no], refs[-2], refs[-1], True):
            cp.wait_recv()
        for cp in sends:
            cp.wait_send()

    return pl.pallas_call(body, name=name, in_specs=[ANY] * ni, out_specs=[ANY] * no, out_shape=ex.outs,
                          input_output_aliases=ex.aliases, scratch_shapes=_sems(ex.n_sems))(*ex.ins)


def _carry_call(body, name, grid, in_specs, out_specs, out_shape, scratch_shapes, args, ex):
    if ex is None:
        res = pl.pallas_call(body, name=name, grid=grid, in_specs=in_specs, out_specs=out_specs, out_shape=out_shape,
                             scratch_shapes=scratch_shapes, compiler_params=_params(("arbitrary",) * len(grid)))(*args)
        return res, None
    ni, no, ns, xi, xo = len(in_specs), len(out_specs), len(scratch_shapes), len(ex.ins), len(ex.outs)

    def wrapped(*refs):
        ins, xin = refs[:ni], refs[ni:ni + xi]
        outs, xout = refs[ni + xi:ni + xi + no], refs[ni + xi + no:ni + xi + no + xo]
        scr, send_sems, recv_sems = refs[ni + xi + no + xo:-2], refs[-2], refs[-1]
        pid = [pl.program_id(d) for d in range(len(grid))]
        first = functools.reduce(jnp.logical_and, [p == 0 for p in pid])
        last = functools.reduce(jnp.logical_and, [p == g - 1 for p, g in zip(pid, grid)])

        @pl.when(first)
        def _():
            for cp in ex.plan(xin, xout, send_sems, recv_sems, False):
                cp.start()

        body(*ins, *outs, *scr)

        @pl.when(last)
        def _():
            for cp in ex.plan(xin, xout, send_sems, recv_sems, True):
                cp.wait_recv()
            for cp in ex.plan(xin, xout, send_sems, recv_sems, False):
                cp.wait_send()

    res = pl.pallas_call(
        wrapped, name=name, grid=grid, in_specs=list(in_specs) + [ANY] * xi, out_specs=list(out_specs) + [ANY] * xo,
        out_shape=list(out_shape) + ex.outs, scratch_shapes=list(scratch_shapes) + _sems(ex.n_sems),
        input_output_aliases={ni + i: no + o for i, o in ex.aliases.items()},
        compiler_params=_params(("arbitrary",) * len(grid)))(*args, *ex.ins)
    return res[:no], res[no:]


def _gather_ici(srcs, bufs, ranges):
    n = len(srcs)

    def plan(in_refs, out_refs, ss, rs, arrivals):
        x, y, c = _mesh_pos()
        cps = []
        for t, (l0, nl) in enumerate(ranges):
            if nl:
                s, o, lr = in_refs[t], out_refs[t], pl.ds(l0, nl)
                for j, (cx, cy) in enumerate(_other_chips(x, y)):
                    chip = 2 * cx + cy if arrivals else 2 * x + y
                    cps.append(_remote(s.at[lr, c], o.at[chip, lr, c], ss, rs, (N_CHIPS - 1) * t + j, (cx, cy, c)))
        return cps

    outs = [jax.ShapeDtypeStruct((N_CHIPS,) + s.shape, s.dtype) for s in srcs]
    if bufs is None:
        return Exchange(srcs, outs, {}, (N_CHIPS - 1) * n, plan)
    return Exchange(list(srcs) + list(bufs), outs, {n + t: t for t in range(n)}, (N_CHIPS - 1) * n, plan)


def _gather_d2d(srcs, bufs, ranges):
    n = len(srcs)

    def plan(in_refs, out_refs, ss, rs, arrivals):
        x, y, c = _mesh_pos()
        sib, me = (x, y, 1 - c), 2 * x + y
        cps = []
        for t, (l0, nl) in enumerate(ranges):
            if nl:
                s, o, lr = in_refs[t], out_refs[t], pl.ds(l0, nl)
                for j, (cx, cy) in enumerate(_other_chips(x, y)):
                    slot = o.at[2 * cx + cy, lr, c]
                    cps.append(_remote(slot, o.at[2 * cx + cy, lr, 1 - c] if arrivals else slot, ss, rs, N_CHIPS * t + j, sib))
                cps.append(_remote(s.at[lr], o.at[me, lr], ss, rs, N_CHIPS * t + N_CHIPS - 1, sib))
        return cps

    outs = [jax.ShapeDtypeStruct(b.shape, b.dtype) for b in bufs]
    return Exchange(list(srcs) + list(bufs), outs, {n + t: t for t in range(n)}, N_CHIPS * n, plan)


def _gather_chips(srcs, name):
    ranges = [(0, s.shape[0]) for s in srcs]
    bufs = _run_exchange(name + "_ici", _gather_ici(srcs, None, ranges))
    return _run_exchange(name + "_d2d", _gather_d2d(srcs, bufs, ranges))


def _pair_exchange(gs):
    def plan(in_refs, out_refs, ss, rs, arrivals):
        x, y, c = _mesh_pos()
        return [_remote(g.at[pl.ds(0, N_CHIPS), 1 - c], o, ss, rs, t, (x, y, 1 - c)) for t, (g, o) in enumerate(zip(in_refs, out_refs))]

    return Exchange(gs, [jax.ShapeDtypeStruct((g.shape[0],) + g.shape[2:], g.dtype) for g in gs], {}, len(gs), plan)


def _chip_exchange(ps, slots, qs, q_shapes):
    n = len(ps)
    kept = [g for g, q in enumerate(qs) if q is not None]

    def plan(in_refs, out_refs, ss, rs, arrivals):
        x, y, c = _mesh_pos()
        return [_remote(in_refs[t].at[2 * cx + cy], out_refs[g].at[j, li], ss, rs, (N_CHIPS - 1) * t + j, (cx, cy, c))
                for t, (g, li) in enumerate(slots) for j, (cx, cy) in enumerate(_other_chips(x, y))]

    return Exchange(list(ps) + [qs[g] for g in kept], q_shapes, {n + i: g for i, g in enumerate(kept)}, (N_CHIPS - 1) * n, plan)


def _pair_share(fs):
    def plan(in_refs, out_refs, ss, rs, arrivals):
        x, y, c = _mesh_pos()
        return [_remote(o.at[pl.ds(0, o.shape[0]), c], o.at[pl.ds(0, o.shape[0]), 1 - c if arrivals else c], ss, rs, t, (x, y, 1 - c))
                for t, o in enumerate(out_refs)]

    return Exchange(fs, [jax.ShapeDtypeStruct(f.shape, f.dtype) for f in fs], {t: t for t in range(len(fs))}, len(fs), plan)


SUM_BLOCK = 512 * 1024


def _sum_pair(g, ra, c, name):
    n, _, h, w = g.shape
    tr = _tile(h, max(16, SUM_BLOCK // w), 16)

    def body(c_ref, g_ref, r_ref, o_ref):
        o_ref[...] = (g_ref[0] + r_ref[...]).astype(o_ref.dtype)

    return pl.pallas_call(
        body, name=name,
        grid_spec=pltpu.PrefetchScalarGridSpec(
            num_scalar_prefetch=1, grid=(n, h // tr),
            in_specs=[pl.BlockSpec((1, 1, tr, w), lambda s, i, cr: (s, cr[0], i, 0)), pl.BlockSpec((1, tr, w), lambda s, i, cr: (s, i, 0))],
            out_specs=pl.BlockSpec((1, tr, w), lambda s, i, cr: (s, i, 0))),
        out_shape=jax.ShapeDtypeStruct((n, h, w), BF16),
        compiler_params=_params(("parallel", "parallel")),
    )(c.reshape(1).astype(jnp.int32), g, ra)


def _sum_chips(ps, q, pos, name):
    nc, nl, h, w = q.shape
    tr = _tile(h, max(16, SUM_BLOCK // (w * nl)), 16)

    def body(x_ref, y_ref, c_ref, *refs):
        q_ref, o_ref = refs[nl], refs[nl + 1]
        for l in range(nl):
            acc = refs[l][0].astype(F32)
            for j in range(nc):
                acc = acc + q_ref[j, l].astype(F32)
            o_ref[l] = acc

    return pl.pallas_call(
        body, name=name,
        grid_spec=pltpu.PrefetchScalarGridSpec(
            num_scalar_prefetch=3, grid=(h // tr,),
            in_specs=[pl.BlockSpec((1, tr, w), lambda i, x, y, c: (2 * x[0] + y[0], i, 0))] * nl
            + [pl.BlockSpec((nc, nl, tr, w), lambda i, x, y, c: (0, 0, i, 0))],
            out_specs=pl.BlockSpec((nl, None, tr, w), lambda i, x, y, c: (0, c[0], i, 0))),
        out_shape=jax.ShapeDtypeStruct((nl, 2, h, w), F32),
        compiler_params=_params(("parallel",)),
    )(*pos, *ps, q)


def _adamw(g, w, m, v, name):
    def f(r0, gg, ww, mm, vv):
        m2 = ADAM_B1 * mm + (1.0 - ADAM_B1) * gg
        v2 = ADAM_B2 * vv + (1.0 - ADAM_B2) * jnp.square(gg)
        m_hat = m2 / (1.0 - ADAM_B1 ** ADAM_STEP)
        v_hat = v2 / (1.0 - ADAM_B2 ** ADAM_STEP)
        return gg, -ADAM_LR * (m_hat / (jnp.sqrt(v_hat) + ADAM_EPS) + ADAM_WD * ww), m2, v2

    return _rowwise(name, f, [g, w, m, v], [], [(g.shape[1], F32)] * 4, tr=_tile(g.shape[0], 512))


WEIGHTS = (
    ("meta_tokens", (N_META, D_MODEL), 1), ("mix_pre_g", (DEPTH, D_MODEL), None), ("mix_post_g", (DEPTH, D_MODEL), None),
    ("mlp_pre_g", (DEPTH, D_MODEL), None), ("mlp_post_g", (DEPTH, D_MODEL), None), ("w_up", (DEPTH, D_MODEL, D_FF), 2),
    ("w_down", (DEPTH, D_FF, D_MODEL), 1), ("w_in", (2, D_MODEL, 3248), 2), ("ssd_conv_w", (2, CONV_K, SSD_CONV_CH), 2),
    ("ssd_conv_b", (2, SSD_CONV_CH), None), ("ssd_dt_bias", (2, SSD_HEADS), None), ("ssd_a_log", (2, SSD_HEADS), None),
    ("ssd_d", (2, SSD_HEADS), None), ("ssd_norm_g", (2, SSD_D_INNER), None), ("mla_q_norm_g", (2, MLA_Q_RANK), None),
    ("mla_w_q_up", (2, MLA_Q_RANK, MLA_HEADS * (MLA_NOPE + MLA_ROPE)), 2), ("mla_kv_norm_g", (2, MLA_KV_RANK), None),
    ("mla_w_kv_up", (2, MLA_KV_RANK, MLA_HEADS * (MLA_NOPE + MLA_V)), 2), ("w_out_ab", (2, SSD_D_INNER + MLA_HEADS * MLA_V, D_MODEL), 1),
    ("rg_w_x", (2, D_MODEL, LRU_WIDTH), 2), ("rg_w_y", (2, D_MODEL, LRU_WIDTH), 2), ("rg_conv_w", (2, CONV_K, LRU_WIDTH), 2),
    ("rg_conv_b", (2, LRU_WIDTH), 1), ("rg_w_a", (2, LRU_BLOCKS, LRU_BLOCK, LRU_BLOCK), None), ("rg_b_a", (2, LRU_WIDTH), 1),
    ("rg_w_i", (2, LRU_BLOCKS, LRU_BLOCK, LRU_BLOCK), None), ("rg_b_i", (2, LRU_WIDTH), 1), ("rg_lambda", (2, LRU_WIDTH), 1),
    ("rg_w_out", (2, LRU_WIDTH, D_MODEL), 1),
)
BIG = {"w_up": "col", "w_down": "row", "w_in": "col", "mla_w_q_up": "col", "mla_w_kv_up": "col", "w_out_ab": "row",
       "rg_w_x": "col", "rg_w_y": "col", "rg_w_out": "row"}
DIRECT = ("w_up", "w_down")
FLAT_QUANTUM = 2 * 16 * LANES
TABLE = {name: (shape, d) for name, shape, d in WEIGHTS}
SMALL_SHARDED = tuple(name for name, _, d in WEIGHTS if d is not None and name not in BIG)
REPLICATED = tuple(name for name, _, d in WEIGHTS if d is None)


def _chips_to_full(a, kind):
    if kind == "col":
        return jnp.moveaxis(a, 0, 2).reshape(a.shape[1], a.shape[2], -1)
    return jnp.moveaxis(a, 0, 1).reshape(a.shape[1], -1, a.shape[3])


def _full_to_chips(g, kind):
    if kind == "col":
        return jnp.moveaxis(g.reshape(g.shape[0], N_CHIPS, -1), 1, 0)
    return g.reshape(N_CHIPS, -1, g.shape[1])


def _shard_shape(shape, d):
    return shape[:d] + (shape[d] // N_CHIPS,) + shape[d + 1:]


def _shard_major(full, d):
    s = full.shape
    return jnp.moveaxis(full.reshape(s[:d] + (N_CHIPS, s[d] // N_CHIPS) + s[d + 1:]), d, 0).reshape(N_CHIPS, -1)


def _from_shard_major(a, shape, d):
    ss = _shard_shape(shape, d)
    return jnp.moveaxis(a.reshape((N_CHIPS,) + ss), 0, d).reshape(shape)


def _pad_cols(a, quantum):
    n = a.shape[-1]
    return jnp.pad(a, [(0, 0)] * (a.ndim - 1) + [(0, -n % quantum)])


def _big_pieces(g):
    def w_in(a):
        return jnp.concatenate([a[:, :IN_DT_END], a[:, PROJ_CQ:PROJ_KR], a[:, PROJ_KR + ROPE_LO:PROJ_KR + ROPE_HI]], axis=1)

    def q_up(a):
        return a.reshape(MLA_Q_RANK, MLA_HEADS, LANES)[:, :, :MLA_NOPE + MLA_ROPE].reshape(MLA_Q_RANK, -1)

    def out_ab(sa):
        s, a = sa
        return jnp.concatenate([s, a.reshape(MLA_HEADS, LANES, D_MODEL)[:, LANES - MLA_V:, :].reshape(-1, D_MODEL)], axis=0)

    ident = lambda a: a
    full = {"w_down": _each(g["w_down"], ident), "w_in": _each(g["w_in"], w_in), "mla_w_q_up": _each(g["mla_w_q_up"], q_up),
            "mla_w_kv_up": _each(g["mla_w_kv_up"], ident),
            "w_out_ab": _each([None if s is None or a is None else (s, a) for s, a in zip(g["w_out_ssd"], g["w_out_att"])], out_ab),
            "rg_w_x": _each(g["rg_w_x"], ident), "rg_w_y": _each(g["rg_w_y"], ident), "rg_w_out": _each(g["rg_w_out"], ident)}
    return {name: (list(g[name]) if name == "w_up" else _each(full[name], lambda a, k=BIG[name]: _full_to_chips(a, k))) for name in BIG}


def _small_grads(g, dh):
    out = {k: jnp.stack(g[k])[:, 0, :] for k in GAINS}
    for k in HEAD_VECS:
        out[k] = jnp.stack(g[k])[:, 0, :SSD_HEADS]
    for k in LRU_VECS:
        out[k] = jnp.stack(g[k]).reshape(-1, LRU_WIDTH)
    for k in ("ssd_conv_w", "rg_conv_w", "rg_w_a", "rg_w_i"):
        out[k] = jnp.stack(g[k])
    out["meta_tokens"] = dh[PAD:PAD + N_META]
    return out


class StepExchanges:
    def __init__(self, w):
        self.w = w
        self.c = lax.axis_index("c")
        self.riding, self.ras = {}, {}
        small = _pad_cols(jnp.concatenate([w[n].reshape(-1) for n in SMALL_SHARDED]), FLAT_QUANTUM).reshape(1, 2, -1, LANES)
        self.srcs = [self._halves(w[n].astype(BF16)) for n in BIG] + [small]
        first = {n: (0, 1 if n in ("w_in", "mla_w_q_up", "mla_w_kv_up", "w_out_ab") else 0) for n in BIG}
        self.first = [first[n] for n in BIG] + [(0, 1)]
        self.rest = [(nl, TABLE[n][0][0] - nl) for n, (_, nl) in zip(BIG, self.first)] + [(0, 0)]
        bufs = _run_exchange("gather_first_ici", _gather_ici(self.srcs, None, self.first))
        self.bufs = _run_exchange("gather_first_d2d", _gather_d2d(self.srcs, bufs, self.first))

    @staticmethod
    def _halves(a):
        return a.reshape(a.shape[0], 2, a.shape[1] // 2, a.shape[2])

    def params(self, ranges):
        w = self.w
        full = {n: w[n] for n in REPLICATED}
        for name, buf, (l0, nl) in zip(BIG, self.bufs, ranges):
            a = buf.reshape(buf.shape[:2] + (-1, buf.shape[4]))
            have = range(l0, l0 + nl)
            if name in DIRECT:
                full[name] = [Gathered(a, BIG[name], l) if l in have else None for l in range(a.shape[1])]
            else:
                full[name] = [_chips_to_full(a[:, l:l + 1], BIG[name])[0] if l in have else None for l in range(a.shape[1])]
        got, off = self.bufs[-1].reshape(N_CHIPS, -1), 0
        for name in SMALL_SHARDED:
            shape, d = TABLE[name]
            n = int(np.prod(_shard_shape(shape, d)))
            full[name] = _from_shard_major(got[:, off:off + n], shape, d)
            off += n
        self.meta = full.pop("meta_tokens")
        return _layout_params(full)

    def forward_exchange(self):
        return _gather_ici(self.srcs, self.bufs, self.rest)

    def after_forward_exchange(self, arrived):
        self.bufs = _run_exchange("gather_rest_d2d", _gather_d2d(self.srcs, arrived, self.rest))
        return self.params([(0, TABLE[n][0][0]) for n in BIG])

    def _pair_sums(self, pieces, tag):
        keys = list(pieces)
        ras = _run_exchange("grads_pair_exchange_" + tag, _pair_exchange([pieces[k] for k in keys]))
        return {k: _sum_pair(pieces[k], ra, self.c, "grads_pair_sum") for k, ra in zip(keys, ras)}

    def _q_shapes(self):
        return [jax.ShapeDtypeStruct((N_CHIPS - 1, s.shape[0]) + s.shape[2:], BF16) for s in self.srcs[:-1]]

    def backward_exchange(self, grads, layer):
        big = _big_pieces(grads)
        pieces = {(g, l): pc.reshape(N_CHIPS, 2, pc.shape[1] // 2, pc.shape[2]) for g, name in enumerate(BIG)
                  for l, pc in enumerate(big[name]) if pc is not None and (g, l) not in self.riding}
        if layer > 0:
            self.riding = pieces
            return _pair_exchange(list(pieces.values()))
        self.ps = {k: _sum_pair(self.riding[k], ra, self.c, "grads_pair_sum") for k, ra in self.ras.items()}
        self.ps.update(self._pair_sums(pieces, "early"))
        self.early = list(self.ps)
        return _chip_exchange([self.ps[k] for k in self.early], self.early, [None] * len(BIG), self._q_shapes())

    def after_backward_exchange(self, arrived, layer):
        if layer > 0:
            self.ras = dict(zip(self.riding, arrived))
        else:
            self.qs = list(arrived)

    def finish(self, grads, dh):
        big, small = _big_pieces(grads), _small_grads(grads, dh)
        pieces = {(g, l): pc.reshape(N_CHIPS, 2, pc.shape[1] // 2, pc.shape[2])
                  for g, name in enumerate(BIG) for l, pc in enumerate(big[name]) if (g, l) not in self.ps}
        sharded = jnp.concatenate([_shard_major(small[n], TABLE[n][1]) for n in SMALL_SHARDED], axis=1)
        rep = _pad_cols(jnp.concatenate([small[n].reshape(-1) for n in REPLICATED]), N_CHIPS * FLAT_QUANTUM)
        n_sh, n_rep = sharded.shape[1], rep.shape[0] // N_CHIPS
        flat = _pad_cols(jnp.concatenate([sharded, rep.reshape(N_CHIPS, n_rep)], axis=1), FLAT_QUANTUM)
        pieces[(len(BIG), 0)] = flat.reshape(N_CHIPS, 2, -1, LANES)
        late = self._pair_sums(pieces, "late")
        self.ps.update(late)
        keys = list(late)
        small_q = jax.ShapeDtypeStruct((N_CHIPS - 1, 1) + late[(len(BIG), 0)].shape[1:], BF16)
        qs = _run_exchange("grads_chip_exchange_late",
                           _chip_exchange([late[k] for k in keys], keys, self.qs + [None], self._q_shapes() + [small_q]))
        pos = [lax.axis_index(a).reshape(1).astype(jnp.int32) for a in ("x", "y", "c")]
        sums = [_sum_chips([self.ps[(g, l)] for l in range(q.shape[1])], q, pos, "grads_chip_sum") for g, q in enumerate(qs)]
        outs = _run_exchange("grads_pair_share", _pair_share(sums))
        out = {name: o.reshape(o.shape[0], -1, o.shape[3]) for name, o in zip(BIG, outs)}
        f = outs[-1].reshape(-1)
        rep_all = _gather_chips([f[n_sh:n_sh + n_rep].reshape(1, 2, -1, LANES)], "grads_gather_replicated")[0].reshape(-1)
        off = 0
        for name in SMALL_SHARDED:
            ss = _shard_shape(*TABLE[name])
            n = int(np.prod(ss))
            out[name] = f[off:off + n].reshape(ss)
            off += n
        off = 0
        for name in REPLICATED:
            shape = TABLE[name][0]
            n = int(np.prod(shape))
            out[name] = rep_all[off:off + n].reshape(shape)
            off += n
        return out


def kernel(x, meta_tokens, mix_pre_g, mix_post_g, mlp_pre_g, mlp_post_g, w_up, w_down, w_in, ssd_conv_w, ssd_conv_b, ssd_dt_bias, ssd_a_log, ssd_d, ssd_norm_g, mla_q_norm_g, mla_w_q_up, mla_kv_norm_g, mla_w_kv_up, w_out_ab, rg_w_x, rg_w_y, rg_conv_w, rg_conv_b, rg_w_a, rg_b_a, rg_w_i, rg_b_i, rg_lambda, rg_w_out, loss_target, m_meta_tokens, m_mix_pre_g, m_mix_post_g, m_mlp_pre_g, m_mlp_post_g, m_w_up, m_w_down, m_w_in, m_ssd_conv_w, m_ssd_conv_b, m_ssd_dt_bias, m_ssd_a_log, m_ssd_d, m_ssd_norm_g, m_mla_q_norm_g, m_mla_w_q_up, m_mla_kv_norm_g, m_mla_w_kv_up, m_w_out_ab, m_rg_w_x, m_rg_w_y, m_rg_conv_w, m_rg_conv_b, m_rg_w_a, m_rg_b_a, m_rg_w_i, m_rg_b_i, m_rg_lambda, m_rg_w_out, v_meta_tokens, v_mix_pre_g, v_mix_post_g, v_mlp_pre_g, v_mlp_post_g, v_w_up, v_w_down, v_w_in, v_ssd_conv_w, v_ssd_conv_b, v_ssd_dt_bias, v_ssd_a_log, v_ssd_d, v_ssd_norm_g, v_mla_q_norm_g, v_mla_w_q_up, v_mla_kv_norm_g, v_mla_w_kv_up, v_w_out_ab, v_rg_w_x, v_rg_w_y, v_rg_conv_w, v_rg_conv_b, v_rg_w_a, v_rg_b_a, v_rg_w_i, v_rg_b_i, v_rg_lambda, v_rg_w_out):
    names = [n for n, _, _ in WEIGHTS]
    w = dict(zip(names, (meta_tokens, mix_pre_g, mix_post_g, mlp_pre_g, mlp_post_g, w_up, w_down, w_in, ssd_conv_w, ssd_conv_b, ssd_dt_bias, ssd_a_log, ssd_d, ssd_norm_g, mla_q_norm_g, mla_w_q_up, mla_kv_norm_g, mla_w_kv_up, w_out_ab, rg_w_x, rg_w_y, rg_conv_w, rg_conv_b, rg_w_a, rg_b_a, rg_w_i, rg_b_i, rg_lambda, rg_w_out)))
    m = dict(zip(names, (m_meta_tokens, m_mix_pre_g, m_mix_post_g, m_mlp_pre_g, m_mlp_post_g, m_w_up, m_w_down, m_w_in, m_ssd_conv_w, m_ssd_conv_b, m_ssd_dt_bias, m_ssd_a_log, m_ssd_d, m_ssd_norm_g, m_mla_q_norm_g, m_mla_w_q_up, m_mla_kv_norm_g, m_mla_w_kv_up, m_w_out_ab, m_rg_w_x, m_rg_w_y, m_rg_conv_w, m_rg_conv_b, m_rg_w_a, m_rg_b_a, m_rg_w_i, m_rg_b_i, m_rg_lambda, m_rg_w_out)))
    v = dict(zip(names, (v_meta_tokens, v_mix_pre_g, v_mix_post_g, v_mlp_pre_g, v_mlp_post_g, v_w_up, v_w_down, v_w_in, v_ssd_conv_w, v_ssd_conv_b, v_ssd_dt_bias, v_ssd_a_log, v_ssd_d, v_ssd_norm_g, v_mla_q_norm_g, v_mla_w_q_up, v_mla_kv_norm_g, v_mla_w_kv_up, v_w_out_ab, v_rg_w_x, v_rg_w_y, v_rg_conv_w, v_rg_conv_b, v_rg_w_a, v_rg_b_a, v_rg_w_i, v_rg_b_i, v_rg_lambda, v_rg_w_out)))
    ex = StepExchanges(w)
    p = ex.params(ex.first)
    sq, dh, grads = _device_step(x[0], ex.meta, loss_target[0], p, hooks=ex)
    loss = lax.psum(0.5 * sq[0, 0] / D_MODEL, ("x", "y", "c"))
    g = ex.finish(grads, dh)
    grad, delta, new_m, new_v = {}, {}, {}, {}
    for name in names:
        shape = g[name].shape
        two_d = (int(np.prod(shape[:-1])), shape[-1])
        res = _adamw(g[name].reshape(two_d), w[name].reshape(two_d), m[name].reshape(two_d), v[name].reshape(two_d), "adamw")
        grad[name], delta[name], new_m[name], new_v[name] = (r.reshape(shape) for r in res)
    grad_x = dh[PAD + N_META:][None]
    return (loss, grad_x, *[grad[n] for n in names], *[delta[n] for n in names], *[new_m[n] for n in names], *[new_v[n] for n in names])
```

```python
import functools

import jax
import jax.numpy as jnp
import numpy as np
from jax import lax
from jax.experimental import pallas as pl
from jax.experimental.pallas import tpu as pltpu

F32 = jnp.float32
BF16 = jnp.bfloat16

D_MODEL = 1024
DEPTH = 4
N_META = 16
CHUNK = 128
PAD = CHUNK - N_META
EPS = 1e-6
SSD_HEADS = 16
SSD_HEAD_DIM = 64
SSD_D_INNER = SSD_HEADS * SSD_HEAD_DIM
SSD_GROUPS = 2
SSD_STATE = 128
SSD_CONV_CH = SSD_D_INNER + 2 * SSD_GROUPS * SSD_STATE
MLA_HEADS = 16
MLA_NOPE = 64
MLA_ROPE = 32
MLA_V = 64
MLA_Q_RANK = 384
MLA_KV_RANK = 256
ROPE_BASE = 10000.0
LRU_WIDTH = 1280
LRU_BLOCKS = 10
LRU_BLOCK = 128
LRU_C = 8.0
D_FF = 4 * D_MODEL
ADAM_LR, ADAM_B1, ADAM_B2, ADAM_EPS, ADAM_WD, ADAM_STEP = 0.001, 0.9, 0.999, 1e-08, 0.01, 10

LANES = 128
VMEM_LIMIT = 56 * 1024 * 1024
MM_VMEM_BUDGET = 48 * 1024 * 1024
PROJ_Z, PROJ_XBC, PROJ_DT, PROJ_CQ, PROJ_CKV, PROJ_KR = 0, 1024, 2560, 2688, 3072, 3328
PROJ_W = 3456


def _tile(n, cap, mult=8):
    for t in range(min(n, cap), 0, -1):
        if n % t == 0 and t % mult == 0:
            return t
    return n


def _params(sem):
    return pltpu.CompilerParams(dimension_semantics=sem, vmem_limit_bytes=VMEM_LIMIT)


def _full_spec(shape, ngrid):
    nd = len(shape)
    if ngrid == 1:
        return pl.BlockSpec(shape, lambda i: (0,) * nd)
    if ngrid == 2:
        return pl.BlockSpec(shape, lambda i, j: (0,) * nd)
    return pl.BlockSpec(shape, lambda i, j, k: (0,) * nd)


_DIMS = {"nn": (((1,), (0,)), ((), ())), "nt": (((1,), (1,)), ((), ())), "tn": (((0,), (0,)), ((), ()))}


class Gathered:
    def __init__(self, arr, kind, layer):
        self.arr, self.kind, self.layer = arr, kind, layer
        _, _, r, c = arr.shape
        self.shape = (r, N_CHIPS * c) if kind == "col" else (N_CHIPS * r, c)


N_CHIPS = 4


def _mm(a, b, mode, name, out_dtype=F32, add=None, out_chip_major=False, extra=(), vecs=(), slots=(), post=None, out_dtypes=None):
    if mode == "nn":
        (m, kc), (_, n) = a.shape, b.shape
    elif mode == "nt":
        (m, kc), (n, _) = a.shape, b.shape
    else:
        (kc, m), (_, n) = a.shape, b.shape
    n_tile = n // N_CHIPS if out_chip_major else n
    across = isinstance(b, Gathered) and (mode, b.kind) in (("nn", "row"), ("nt", "col"))
    if mode == "tn":
        tm, tk = _tile(m, 1024, LANES), kc
        fits = [c for c in (1280, 1152, 1024, 768, 640, 512) if n_tile % c == 0 and
                2 * kc * (tm * a.dtype.itemsize + c * b.dtype.itemsize) + 2 * tm * c * 4 <= MM_VMEM_BUDGET]
        tn = fits[0] if fits else _tile(n_tile, 1280, LANES)
        if not fits:
            tk = _tile(kc, 1408, LANES)
    else:
        tn = _tile(n_tile, 1280, LANES)
        tk = _tile(kc, 4096, LANES)
        tm = _tile(m, 1056 if tk <= 1024 else 528, 16)
    nk = kc // tk
    if mode == "tn":
        a_spec = pl.BlockSpec((tk, tm), lambda i, j, k: (k, i))
    else:
        a_spec = pl.BlockSpec((tm, tk), lambda i, j, k: (i, k))
    b_arrs = [b]
    if isinstance(b, Gathered):
        layer = b.layer
        sr, sc = b.arr.shape[2:]
        if across:
            assert nk == 1 and kc == N_CHIPS * (sr if b.kind == "row" else sc)
            b_arrs = [b.arr] * N_CHIPS
            if b.kind == "row":
                b_specs = [pl.BlockSpec((None, None, sr, tn), lambda i, j, k, s=s: (s, layer, 0, j)) for s in range(N_CHIPS)]
            else:
                b_specs = [pl.BlockSpec((None, None, tn, sc), lambda i, j, k, s=s: (s, layer, j, 0)) for s in range(N_CHIPS)]
        else:
            b_arrs = [b.arr]
            br, bc = (tk, tn) if mode == "nn" else (tn, tk)
            assert mode in ("nn", "nt") and sr % br == 0 and sc % bc == 0

            def b_map(i, j, k):
                r, c = (k, j) if mode == "nn" else (j, k)
                if b.kind == "col":
                    return ((c * bc) // sc, layer, r, ((c * bc) % sc) // bc)
                return ((r * br) // sr, layer, ((r * br) % sr) // br, c)

            b_specs = [pl.BlockSpec((None, None, br, bc), b_map)]
    elif mode == "nt":
        b_specs = [pl.BlockSpec((tn, tk), lambda i, j, k: (j, k))]
    else:
        b_specs = [pl.BlockSpec((tk, tn), lambda i, j, k: (k, j))]
    nb = len(b_arrs)
    dims = _DIMS[mode]
    if out_chip_major:
        ns = n // N_CHIPS
        o_spec = pl.BlockSpec((None, tm, tn), lambda i, j, k: ((j * tn) // ns, i, ((j * tn) % ns) // tn))
        o_shape = jax.ShapeDtypeStruct((N_CHIPS, m, ns), out_dtype)
    else:
        o_spec = pl.BlockSpec((tm, tn), lambda i, j, k: (i, j))
        o_shape = jax.ShapeDtypeStruct((m, n), out_dtype)
    extra = list(extra) + ([add] if add is not None else [])
    if add is not None:
        post = lambda v, x: (v + x,)
    vecs, slots = list(vecs), list(slots)
    nx = len(extra) + len(vecs) + len(slots)
    out_dtypes = out_dtypes or [out_dtype]
    no = len(out_dtypes)

    def body(a_ref, *rest):
        b_refs, rest = rest[:nb], rest[nb:]
        o_refs, acc = rest[nx:nx + no], rest[nx + no:]
        if across:
            w = kc // N_CHIPS
            p = functools.reduce(jnp.add, [
                lax.dot_general(a_ref[:, s * w:(s + 1) * w].astype(BF16), b_refs[s][...].astype(BF16), dims, preferred_element_type=F32)
                for s in range(N_CHIPS)])
        else:
            p = lax.dot_general(a_ref[...].astype(BF16), b_refs[0][...].astype(BF16), dims, preferred_element_type=F32)

        def emit(v):
            res = post(v, *[r[...] for r in rest[:nx]]) if post else (v,)
            for o_ref, r in zip(o_refs, res):
                o_ref[...] = r.astype(o_ref.dtype)

        if nk == 1:
            emit(p)
        else:
            k = pl.program_id(2)

            @pl.when(k == 0)
            def _():
                acc[0][...] = p

            @pl.when(k > 0)
            def _():
                acc[0][...] += p

            @pl.when(k == nk - 1)
            def _():
                emit(acc[0][...])

    res = pl.pallas_call(
        body, name=name, grid=(m // tm, n // tn, nk),
        in_specs=[a_spec] + b_specs + [o_spec] * len(extra) + [pl.BlockSpec((1, tn), lambda i, j, k: (0, j))] * len(vecs)
        + [pl.BlockSpec((tm, LANES), lambda i, j, k: (i, 0))] * len(slots),
        out_specs=[o_spec] * no,
        out_shape=[jax.ShapeDtypeStruct(o_shape.shape, dt) for dt in out_dtypes],
        scratch_shapes=[pltpu.VMEM((tm, tn), F32)] if nk > 1 else [],
        compiler_params=_params(("parallel", "parallel", "arbitrary")),
    )(a, *b_arrs, *extra, *vecs, *slots)
    return res[0] if no == 1 else res


def _rowarg(r):
    return r if isinstance(r, tuple) else (r, r.shape[1], 0)


def _rowspec(r, tr, ncol):
    _, w, cb = r
    if ncol > 1:
        return pl.BlockSpec((tr, w // ncol), lambda j, i: (i, j))
    return pl.BlockSpec((tr, w), lambda j, i: (i, cb))


def _rowwise(name, f, rows, params, outs, tr=None, ncol=1):
    rows = [_rowarg(r) for r in rows]
    t = rows[0][0].shape[0]
    tr = tr or _tile(t, 528)
    nr, npm = len(rows), len(params)

    def body(*refs):
        vals = [r[...] for r in refs[:nr]] + [(p[0] if ncol > 1 else p[...]) for p in refs[nr:nr + npm]]
        res = f(pl.program_id(1) * tr, *vals)
        for o_ref, v in zip(refs[nr + npm:], res):
            o_ref[...] = v.astype(o_ref.dtype)

    def pspec(p):
        if ncol > 1:
            return pl.BlockSpec((1,) + p.shape[1:], lambda j, i, n=p.ndim: (j,) + (0,) * (n - 1))
        return _full_spec(p.shape, 2)

    return pl.pallas_call(
        body, name=name, grid=(ncol, t // tr),
        in_specs=[_rowspec(r, tr, ncol) for r in rows] + [pspec(p) for p in params],
        out_specs=[pl.BlockSpec((tr, w // ncol), lambda j, i: (i, j)) for w, _ in outs],
        out_shape=[jax.ShapeDtypeStruct((t, w), dt) for w, dt in outs],
        compiler_params=_params(("parallel", "parallel")),
    )(*[r[0] for r in rows], *params)


def _rowwise_vjp(name, f, rows, params, cts, tr=None, ncol=1, row_dtypes=None):
    rows = [_rowarg(r) for r in rows]
    cts = [_rowarg(c) for c in cts]
    t = rows[0][0].shape[0]
    tr = tr or _tile(t, 528)
    nr, npm, nc = len(rows), len(params), len(cts)
    row_dtypes = row_dtypes or [F32] * nr

    def body(*refs):
        i = pl.program_id(1)
        vals = [r[...] for r in refs[:nr]] + [(p[0] if ncol > 1 else p[...]) for p in refs[nr:nr + npm]]
        ct = tuple(c[...].astype(F32) for c in refs[nr + npm:nr + npm + nc])
        _, vjp = jax.vjp(lambda *a: tuple(f(i * tr, *a)), *vals)
        g = vjp(ct)
        outs = refs[nr + npm + nc:]
        for o_ref, v in zip(outs[:nr], g[:nr]):
            o_ref[...] = v.astype(o_ref.dtype)
        pg = [(v[None] if ncol > 1 else v) for v in g[nr:]]

        @pl.when(i == 0)
        def _():
            for o_ref, v in zip(outs[nr:], pg):
                o_ref[...] = v

        @pl.when(i > 0)
        def _():
            for o_ref, v in zip(outs[nr:], pg):
                o_ref[...] += v

    def pspec(p):
        if ncol > 1:
            return pl.BlockSpec((1,) + p.shape[1:], lambda j, i, n=p.ndim: (j,) + (0,) * (n - 1))
        return _full_spec(p.shape, 2)

    res = pl.pallas_call(
        body, name=name, grid=(ncol, t // tr),
        in_specs=[_rowspec(r, tr, ncol) for r in rows] + [pspec(p) for p in params] + [_rowspec(c, tr, ncol) for c in cts],
        out_specs=[pl.BlockSpec((tr, w // ncol), lambda j, i: (i, j)) for _, w, _ in rows] + [pspec(p) for p in params],
        out_shape=[jax.ShapeDtypeStruct((t, w), dt) for (_, w, _), dt in zip(rows, row_dtypes)]
        + [jax.ShapeDtypeStruct(p.shape, F32) for p in params],
        compiler_params=_params(("parallel", "arbitrary")),
    )(*[r[0] for r in rows], *params, *[c[0] for c in cts])
    return res[:nr], res[nr:]


def _valid(row0, tr):
    return (row0 + lax.broadcasted_iota(jnp.int32, (tr, 1), 0)) >= PAD


def _rms(x, g):
    return x * lax.rsqrt(jnp.mean(x * x, axis=-1, keepdims=True) + EPS) * g


def _softplus(x):
    return jnp.where(x < -15.0, jnp.exp(x), jnp.maximum(x, 0.0) + jnp.log(1.0 + jnp.exp(-jnp.abs(x))))


def _neg_expm1(z):
    return jnp.where(z > -0.01, -z * (1.0 + z * (0.5 + z * (1.0 / 6.0))), 1.0 - jnp.exp(z))


def _prenorm(h, g, name):
    return _rowwise(name, lambda r0, x, gg: (_rms(x, gg),), [h], [g], [(_rowarg(h)[1], BF16)])[0]


def _post_residual(m, h, g):
    assert m.shape[1] == D_MODEL
    return m, h + _rms(m, g)


def _postnorm_bwd(m, g, dh, name):
    (dm,), (dg,) = _rowwise_vjp(name, lambda r0, mm, gg: (_rms(mm, gg),), [m], [g], [dh], row_dtypes=[BF16])
    return dm, dg


def _prenorm_bwd_add(h, g, dhns, dh, name):
    t, w = h.shape
    tr = _tile(t, 528)
    nd = len(dhns)

    def body(h_ref, g_ref, *refs):
        dh_ref, o_ref, dg_ref = refs[nd:]
        i = pl.program_id(0)
        _, vjp = jax.vjp(_rms, h_ref[...], g_ref[...])
        dhn = refs[0][...].astype(F32)
        for r in refs[1:nd]:
            dhn = dhn + r[...].astype(F32)
        dx, dg = vjp(dhn)
        o_ref[...] = dh_ref[...] + dx

        @pl.when(i == 0)
        def _():
            dg_ref[...] = dg

        @pl.when(i > 0)
        def _():
            dg_ref[...] += dg

    row = pl.BlockSpec((tr, w), lambda i: (i, 0))
    return pl.pallas_call(
        body, name=name, grid=(t // tr,), in_specs=[row, _full_spec(g.shape, 1)] + [row] * (nd + 1),
        out_specs=[row, _full_spec(g.shape, 1)],
        out_shape=[jax.ShapeDtypeStruct((t, w), F32), jax.ShapeDtypeStruct(g.shape, F32)],
        compiler_params=_params(("arbitrary",)),
    )(h, g, *dhns, dh)


def _loss_and_grad(h, target, name):
    t, w = h.shape
    nb = t // CHUNK

    def body(h_ref, t_ref, s_ref, dh_ref):
        i = pl.program_id(0)

        @pl.when(i == 0)
        def _():
            s_ref[...] = jnp.zeros_like(s_ref)
            dh_ref[...] = jnp.zeros_like(dh_ref)

        @pl.when(i > 0)
        def _():
            err = h_ref[...] - t_ref[...]
            s_ref[...] += jnp.sum(err * err)
            dh_ref[...] = err * (1.0 / w)

    return pl.pallas_call(
        body, name=name, grid=(nb,),
        in_specs=[pl.BlockSpec((CHUNK, w), lambda i: (i, 0)), pl.BlockSpec((CHUNK, w), lambda i: (jnp.maximum(i - 1, 0), 0))],
        out_specs=[_full_spec((1, LANES), 1), pl.BlockSpec((CHUNK, w), lambda i: (i, 0))],
        out_shape=[jax.ShapeDtypeStruct((1, LANES), F32), jax.ShapeDtypeStruct((t, w), F32)],
        compiler_params=_params(("arbitrary",)),
    )(h, target)


def _mlp_fwd(h, p, l):
    hn = _prenorm(h, p["mlp_pre_g"][l], "mlp_prenorm")
    a, u = _mm(hn, p["w_up"][l], "nn", "mlp_up", post=lambda v: (v, jnp.square(jnp.maximum(v, 0.0))), out_dtypes=[BF16, BF16])
    d, h2 = _mm(u, p["w_down"][l], "nn", "mlp_down", extra=[h], vecs=[p["mlp_post_g"][l]], post=_post_residual, out_dtypes=[F32, F32])
    return h2, (h, hn, a, u, d)


def _mlp_bwd(dh, saved, p, l, grads):
    h, hn, a, u, d = saved
    dd, grads["mlp_post_g"][l] = _postnorm_bwd(d, p["mlp_post_g"][l], dh, "mlp_postnorm_bwd")
    grads["w_down"][l] = _mm(u, dd, "tn", "mlp_down_dw")
    da = _mm(dd, p["w_down"][l], "nt", "mlp_down_dx", extra=[a], post=lambda v, x: (2.0 * jnp.maximum(x.astype(F32), 0.0) * v,),
             out_dtypes=[BF16])
    grads["w_up"][l] = _mm(hn, da, "tn", "mlp_up_dw", out_chip_major=True)
    dhn = _mm(da, p["w_up"][l], "nt", "mlp_up_dx")
    dh, grads["mlp_pre_g"][l] = _prenorm_bwd_add(h, p["mlp_pre_g"][l], [dhn], dh, "mlp_prenorm_bwd")
    return dh


def _dot(a, b, mode):
    return lax.dot_general(a.astype(BF16), b.astype(BF16), _DIMS[mode], preferred_element_type=F32)


@jax.custom_vjp
def _bnn(a, b):
    return _dot(a, b, "nn")


_bnn.defvjp(lambda a, b: (_dot(a, b, "nn"), (a, b)), lambda r, ct: (_dot(ct, r[1], "nt"), _dot(r[0], ct, "tn")))


@jax.custom_vjp
def _bnt(a, b):
    return _dot(a, b, "nt")


_bnt.defvjp(lambda a, b: (_dot(a, b, "nt"), (a, b)), lambda r, ct: (_dot(ct, r[1], "nn"), _dot(ct, r[0], "tn")))


@jax.custom_vjp
def _btn(a, b):
    return _dot(a, b, "tn")


_btn.defvjp(lambda a, b: (_dot(a, b, "tn"), (a, b)), lambda r, ct: (_dot(r[1], ct, "nt"), _dot(r[0], ct, "nn")))


CONV_K = 4
HALO = 8


def _conv_fwd(x, w, b, name, cw, c0=0):
    t, c = x.shape[0], w.shape[1]
    tr = _tile(t, 528)
    hb = tr // HALO

    def body(x_ref, halo_ref, w_ref, b_ref, o_ref, ext):
        i = pl.program_id(1)
        ext[pl.ds(0, HALO), :] = jnp.where(i > 0, halo_ref[...], 0.0)
        ext[pl.ds(HALO, tr), :] = x_ref[...]
        acc = jnp.broadcast_to(b_ref[...], (tr, cw))
        for k in range(CONV_K):
            acc = acc + w_ref[pl.ds(k, 1), :] * ext[pl.ds(HALO - (CONV_K - 1) + k, tr), :]
        o_ref[...] = acc

    return pl.pallas_call(
        body, name=name, grid=(c // cw, t // tr),
        in_specs=[pl.BlockSpec((tr, cw), lambda j, i: (i, c0 + j)),
                  pl.BlockSpec((HALO, cw), lambda j, i: (jnp.maximum(i * hb - 1, 0), c0 + j)),
                  pl.BlockSpec((CONV_K, cw), lambda j, i: (0, j)), pl.BlockSpec((1, cw), lambda j, i: (0, j))],
        out_specs=pl.BlockSpec((tr, cw), lambda j, i: (i, j)),
        out_shape=jax.ShapeDtypeStruct((t, c), F32),
        scratch_shapes=[pltpu.VMEM((tr + HALO, cw), F32)],
        compiler_params=_params(("parallel", "parallel")),
    )(x, x, w, b)


def _conv_bwd(x, w, dy, name, cw, c0=0):
    t, c = x.shape[0], w.shape[1]
    tr = _tile(t, 528)
    hb = tr // HALO
    nb = t // tr

    def body(x_ref, xh_ref, w_ref, dy_ref, dyh_ref, dx_ref, dw_ref, db_ref, xe, de):
        c = cw
        i = pl.program_id(1)
        xe[pl.ds(0, HALO), :] = jnp.where(i > 0, xh_ref[...], 0.0)
        xe[pl.ds(HALO, tr), :] = x_ref[...]
        de[pl.ds(0, tr), :] = dy_ref[...]
        de[pl.ds(tr, HALO), :] = jnp.where(i < nb - 1, dyh_ref[...], 0.0)
        dy = dy_ref[...]
        acc = jnp.zeros((tr, c), F32)
        dw = jnp.zeros((CONV_K, c), F32)
        rows = lax.broadcasted_iota(jnp.int32, (CONV_K, 1), 0)
        for k in range(CONV_K):
            acc = acc + w_ref[pl.ds(k, 1), :] * de[pl.ds(CONV_K - 1 - k, tr), :]
            dwk = jnp.sum(dy * xe[pl.ds(HALO - (CONV_K - 1) + k, tr), :], axis=0, keepdims=True)
            dw = dw + jnp.where(rows == k, dwk, 0.0)
        dx_ref[...] = jnp.where(_valid(i * tr, tr), acc, 0.0).astype(dx_ref.dtype)
        db = jnp.sum(dy, axis=0, keepdims=True)

        @pl.when(i == 0)
        def _():
            dw_ref[...] = dw
            db_ref[...] = db

        @pl.when(i > 0)
        def _():
            dw_ref[...] += dw
            db_ref[...] += db

    row = pl.BlockSpec((tr, cw), lambda j, i: (i, j))
    return pl.pallas_call(
        body, name=name, grid=(c // cw, nb),
        in_specs=[pl.BlockSpec((tr, cw), lambda j, i: (i, c0 + j)),
                  pl.BlockSpec((HALO, cw), lambda j, i: (jnp.maximum(i * hb - 1, 0), c0 + j)),
                  pl.BlockSpec((CONV_K, cw), lambda j, i: (0, j)),
                  row, pl.BlockSpec((HALO, cw), lambda j, i: (jnp.minimum((i + 1) * hb, t // HALO - 1), j))],
        out_specs=[row, pl.BlockSpec((CONV_K, cw), lambda j, i: (0, j)), pl.BlockSpec((1, cw), lambda j, i: (0, j))],
        out_shape=[jax.ShapeDtypeStruct((t, c), BF16), jax.ShapeDtypeStruct((CONV_K, c), F32), jax.ShapeDtypeStruct((1, c), F32)],
        scratch_shapes=[pltpu.VMEM((tr + HALO, cw), F32), pltpu.VMEM((tr + HALO, cw), F32)],
        compiler_params=_params(("parallel", "arbitrary")),
    )(x, x, w, dy, dy)


SUB = 8


def _lru_scan(a, u, name):
    t, c = a.shape
    tr = _tile(t, 528)

    def body(a_ref, u_ref, o_ref, carry):
        @pl.when(pl.program_id(0) == 0)
        def _():
            carry[...] = jnp.zeros_like(carry)

        rows = lax.broadcasted_iota(jnp.int32, (SUB, 1), 0)

        def step(k, cin):
            r = pl.multiple_of(k * SUB, SUB)
            av, uv = a_ref[pl.ds(r, SUB), :], u_ref[pl.ds(r, SUB), :]
            for d in (1, 2, 4):
                m = rows >= d
                uv = uv + av * jnp.where(m, pltpu.roll(uv, d, 0), 0.0)
                av = av * jnp.where(m, pltpu.roll(av, d, 0), 1.0)
            hv = uv + av * cin
            o_ref[pl.ds(r, SUB), :] = hv
            return jnp.broadcast_to(hv[SUB - 1:SUB, :], (SUB, c))

        carry[...] = lax.fori_loop(0, tr // SUB, step, carry[...])

    row = pl.BlockSpec((tr, c), lambda i: (i, 0))
    return pl.pallas_call(
        body, name=name, grid=(t // tr,), in_specs=[row, row], out_specs=row,
        out_shape=jax.ShapeDtypeStruct((t, c), F32), scratch_shapes=[pltpu.VMEM((SUB, c), F32)],
        compiler_params=_params(("arbitrary",)),
    )(a, u)


def _lru_scan_bwd(a, hs, dy, name):
    t, c = a.shape
    tr = _tile(t, 528)
    nb, nt = t // tr, tr // SUB

    def body(a_ref, h_ref, hh_ref, dy_ref, du_ref, da_ref, gcar, acar):
        i = pl.program_id(0)

        @pl.when(i == 0)
        def _():
            gcar[...] = jnp.zeros_like(gcar)
            acar[...] = jnp.zeros_like(acar)

        rows = lax.broadcasted_iota(jnp.int32, (SUB, 1), 0)
        hhalo = jnp.where(i < nb - 1, hh_ref[...], 0.0)

        def step(kk, car):
            gin, a_next_first = car
            k = nt - 1 - kk
            r = pl.multiple_of(k * SUB, SUB)
            av, hv, dv = a_ref[pl.ds(r, SUB), :], h_ref[pl.ds(r, SUB), :], dy_ref[pl.ds(r, SUB), :]
            rp = pl.multiple_of(jnp.maximum(k - 1, 0) * SUB, SUB)
            hp = jnp.where(k > 0, h_ref[pl.ds(rp, SUB), :], hhalo)
            cv = jnp.where(rows < SUB - 1, pltpu.roll(av, SUB - 1, 0), a_next_first)
            gv = dv
            for d in (1, 2, 4):
                m = rows < SUB - d
                gv = gv + cv * jnp.where(m, pltpu.roll(gv, SUB - d, 0), 0.0)
                cv = cv * jnp.where(m, pltpu.roll(cv, SUB - d, 0), 1.0)
            gv = gv + cv * gin
            hprev = jnp.where(rows >= 1, pltpu.roll(hv, 1, 0), jnp.broadcast_to(hp[SUB - 1:SUB, :], (SUB, c)))
            du_ref[pl.ds(r, SUB), :] = gv
            da_ref[pl.ds(r, SUB), :] = gv * hprev
            return jnp.broadcast_to(gv[0:1, :], (SUB, c)), jnp.broadcast_to(av[0:1, :], (SUB, c))

        g, af = lax.fori_loop(0, nt, step, (gcar[...], acar[...]))
        gcar[...] = g
        acar[...] = af

    hb = tr // SUB
    row = pl.BlockSpec((tr, c), lambda i: (nb - 1 - i, 0))
    halo = pl.BlockSpec((SUB, c), lambda i: (jnp.maximum((nb - 1 - i) * hb - 1, 0), 0))
    return pl.pallas_call(
        body, name=name, grid=(nb,), in_specs=[row, row, halo, row], out_specs=[row, row],
        out_shape=[jax.ShapeDtypeStruct((t, c), F32)] * 2,
        scratch_shapes=[pltpu.VMEM((SUB, c), F32), pltpu.VMEM((SUB, c), F32)],
        compiler_params=_params(("arbitrary",)),
    )(a, hs, hs, dy)


def _lru_gates(row0, xr, wa, ba, wi, bi, lam):
    r = jax.nn.sigmoid(_bnn(xr, wa) + ba)
    i = jax.nn.sigmoid(_bnn(xr, wi) + bi)
    log_a = -LRU_C * r * _softplus(-lam)
    u = jnp.sqrt(_neg_expm1(2.0 * log_a)) * (i * xr)
    return jnp.exp(log_a), jnp.where(_valid(row0, xr.shape[0]), u, 0.0)


def _lru_gate_out(row0, hs, yw):
    return (hs * jax.nn.gelu(yw),)


def _rglru_fwd(h, p, l, o):
    hn = _prenorm(h, p["mix_pre_g"][l], "rg_prenorm")
    xw = _mm(hn, p["rg_w_x"][o], "nn", "rg_in_x")
    yw = _mm(hn, p["rg_w_y"][o], "nn", "rg_in_y")
    xr = _conv_fwd(xw, p["rg_conv_w"][o], p["rg_conv_b"][o], "rg_conv", cw=LRU_WIDTH // 2)
    gp = [p["rg_w_a"][o], p["rg_b_a"][o], p["rg_w_i"][o], p["rg_b_i"][o], p["rg_lambda"][o]]
    a, u = _rowwise("rg_gates", _lru_gates, [xr], gp, [(LRU_WIDTH, F32)] * 2, ncol=LRU_BLOCKS, tr=_tile(h.shape[0], 1056))
    hs = _lru_scan(a, u, "rg_scan")
    hg = _rowwise("rg_gate_out", _lru_gate_out, [hs, yw], [], [(LRU_WIDTH, BF16)])[0]
    m, h2 = _mm(hg, p["rg_w_out"][o], "nn", "rg_out", extra=[h], vecs=[p["mix_post_g"][l]], post=_post_residual, out_dtypes=[F32, F32])
    return h2, (h, hn, xw, yw, xr, a, hs, hg, m)


def _rglru_bwd(dh, saved, p, l, o, grads):
    h, hn, xw, yw, xr, a, hs, hg, m = saved
    dm, grads["mix_post_g"][l] = _postnorm_bwd(m, p["mix_post_g"][l], dh, "rg_postnorm_bwd")
    grads["rg_w_out"][o] = _mm(hg, dm, "tn", "rg_out_dw")
    dhg = _mm(dm, p["rg_w_out"][o], "nt", "rg_out_dx")
    (dhs, dyw), _ = _rowwise_vjp("rg_gate_out_bwd", _lru_gate_out, [hs, yw], [], [dhg], row_dtypes=[F32, BF16])
    du, da = _lru_scan_bwd(a, hs, dhs, "rg_scan_bwd")
    gp = [p["rg_w_a"][o], p["rg_b_a"][o], p["rg_w_i"][o], p["rg_b_i"][o], p["rg_lambda"][o]]
    (dxr,), gg = _rowwise_vjp("rg_gates_bwd", _lru_gates, [xr], gp, [da, du], ncol=LRU_BLOCKS, tr=_tile(h.shape[0], 1056))
    grads["rg_w_a"][o], grads["rg_b_a"][o], grads["rg_w_i"][o], grads["rg_b_i"][o], grads["rg_lambda"][o] = gg
    dxw, grads["rg_conv_w"][o], grads["rg_conv_b"][o] = _conv_bwd(xw, p["rg_conv_w"][o], dxr, "rg_conv_bwd", cw=LRU_WIDTH // 2)
    grads["rg_w_x"][o] = _mm(hn, dxw, "tn", "rg_in_x_dw")
    grads["rg_w_y"][o] = _mm(hn, dyw, "tn", "rg_in_y_dw")
    dhx = _mm(dxw, p["rg_w_x"][o], "nt", "rg_in_x_dx")
    dhy = _mm(dyw, p["rg_w_y"][o], "nt", "rg_in_y_dx")
    dh, grads["mix_pre_g"][l] = _prenorm_bwd_add(h, p["mix_pre_g"][l], [dhx, dhy], dh, "rg_prenorm_bwd")
    return dh


SSD_GW = SSD_D_INNER // SSD_GROUPS
SSD_GH = SSD_HEADS // SSD_GROUPS
XACT_B = SSD_D_INNER // SSD_STATE
XACT_C = XACT_B + SSD_GROUPS


def _hp(a, b, dims=_DIMS["nn"]):
    return lax.dot_general(a, b, dims, precision=lax.Precision.HIGHEST, preferred_element_type=F32)


def _split_dot(a, e, mode, parts):
    eb = e.astype(BF16)
    out, rest = None, a
    for _ in range(parts):
        term = rest.astype(BF16)
        rest = rest - term.astype(F32)
        if mode in ("nn", "nt"):
            prod = lax.dot_general(term, eb, _DIMS[mode], preferred_element_type=F32)
        else:
            prod = lax.dot_general(eb, term, _DIMS["nn" if mode == "left" else "tn"], preferred_element_type=F32)
        out = prod if out is None else out + prod
    return out


@jax.custom_vjp
def _select_nn(a, e):
    return _split_dot(a, e, "nn", 3)


_select_nn.defvjp(lambda a, e: (_split_dot(a, e, "nn", 3), e), lambda e, ct: (_split_dot(ct, e, "nt", 2), jnp.zeros_like(e)))


@jax.custom_vjp
def _select_left(e, a):
    return _split_dot(a, e, "left", 3)


_select_left.defvjp(lambda e, a: (_split_dot(a, e, "left", 3), e),
                    lambda e, ct: (jnp.zeros_like(e), _split_dot(ct, e, "left_t", 2)))


def _ssd_chunk(xs, bm, cm, dt, da, ht, g):
    l = CHUNK
    ri = lax.broadcasted_iota(jnp.int32, (l, l), 0)
    ci = lax.broadcasted_iota(jnp.int32, (l, l), 1)
    causal = ri >= ci
    tri = causal.astype(F32)
    hr = lax.broadcasted_iota(jnp.int32, (LANES, SSD_GW), 0)
    hc = lax.broadcasted_iota(jnp.int32, (LANES, SSD_GW), 1)
    expand = (hr == g * SSD_GH + hc // SSD_HEAD_DIM).astype(F32)
    acs = _select_left(tri, da)
    acs_t = acs.T
    acs_e = _select_nn(acs, expand)
    x = xs * _select_nn(dt, expand)
    gmat = _bnt(cm, bm)
    lane = lax.broadcasted_iota(jnp.int32, (1, LANES), 1)
    sub = lax.broadcasted_iota(jnp.int32, (LANES, 1), 0)
    colhead = lax.broadcasted_iota(jnp.int32, (1, SSD_GW), 1) // SSD_HEAD_DIM
    y = _bnn(cm, ht) * jnp.exp(acs_e)
    for k in range(SSD_GH):
        hh = g * SSD_GH + k
        col = jnp.sum(jnp.where(lane == hh, acs, 0.0), axis=1, keepdims=True)
        row = jnp.sum(jnp.where(sub == hh, acs_t, 0.0), axis=0, keepdims=True)
        decay = jnp.exp(jnp.where(causal, col - row, -1e30))
        y = y + _bnn(gmat * decay, jnp.where(colhead == k, x, 0.0))
    last = lax.broadcasted_iota(jnp.int32, (l, 1), 0) == l - 1
    a_last = jnp.sum(jnp.where(last, acs_e, 0.0), axis=0, keepdims=True)
    st = _btn(bm, x * jnp.exp(a_last - acs_e))
    return y, ht * jnp.exp(a_last) + st


def _ssd_specs(nc, rev):
    def cc(c):
        return nc - 1 - c if rev else c

    return [pl.BlockSpec((CHUNK, SSD_GW), lambda c, g: (cc(c), g)),
            pl.BlockSpec((CHUNK, SSD_STATE), lambda c, g: (cc(c), XACT_B + g)),
            pl.BlockSpec((CHUNK, SSD_STATE), lambda c, g: (cc(c), XACT_C + g)),
            pl.BlockSpec((CHUNK, LANES), lambda c, g: (cc(c), 0)),
            pl.BlockSpec((CHUNK, LANES), lambda c, g: (cc(c), 0))]


def _ssd_scan(xact, dt, da, name):
    t = xact.shape[0]
    nc = t // CHUNK

    def body(xs_ref, b_ref, c_ref, dt_ref, da_ref, y_ref, hs_ref, state):
        c, g = pl.program_id(0), pl.program_id(1)

        @pl.when(c == 0)
        def _():
            state[g] = jnp.zeros((SSD_STATE, SSD_GW), F32)

        ht = state[g]
        hs_ref[0] = ht
        y, ht2 = _ssd_chunk(xs_ref[...], b_ref[...], c_ref[...], dt_ref[...], da_ref[...], ht, g)
        y_ref[...] = y
        state[g] = ht2

    return pl.pallas_call(
        body, name=name, grid=(nc, SSD_GROUPS), in_specs=_ssd_specs(nc, False),
        out_specs=[pl.BlockSpec((CHUNK, SSD_GW), lambda c, g: (c, g)),
                   pl.BlockSpec((1, SSD_STATE, SSD_GW), lambda c, g: (c * SSD_GROUPS + g, 0, 0))],
        out_shape=[jax.ShapeDtypeStruct((t, SSD_D_INNER), F32), jax.ShapeDtypeStruct((nc * SSD_GROUPS, SSD_STATE, SSD_GW), F32)],
        scratch_shapes=[pltpu.VMEM((SSD_GROUPS, SSD_STATE, SSD_GW), F32)],
        compiler_params=_params(("arbitrary", "arbitrary")),
    )(xact, xact, xact, dt, da)


def _ssd_scan_bwd(xact, dt, da, hsave, dy, dxskip, name):
    t = xact.shape[0]
    nc = t // CHUNK

    def body(xs_ref, b_ref, c_ref, dt_ref, da_ref, hs_ref, dy_ref, sk_ref, dxs_ref, db_ref, dc_ref, ddt_ref, dda_ref, dstate):
        c, g = pl.program_id(0), pl.program_id(1)

        @pl.when(c == 0)
        def _():
            dstate[g] = jnp.zeros((SSD_STATE, SSD_GW), F32)

        _, vjp = jax.vjp(lambda *a: _ssd_chunk(*a, g), xs_ref[...], b_ref[...], c_ref[...], dt_ref[...], da_ref[...], hs_ref[0])
        dxs, dbm, dcm, ddt, dda, dht = vjp((dy_ref[...], dstate[g]))
        dxs_ref[...] = dxs + sk_ref[...]
        db_ref[...] = dbm
        dc_ref[...] = dcm
        dstate[g] = dht

        @pl.when(g == 0)
        def _():
            ddt_ref[...] = ddt
            dda_ref[...] = dda

        @pl.when(g > 0)
        def _():
            ddt_ref[...] += ddt
            dda_ref[...] += dda

    grp = pl.BlockSpec((CHUNK, SSD_GW), lambda c, g: (nc - 1 - c, g))
    st = pl.BlockSpec((CHUNK, SSD_STATE), lambda c, g: (nc - 1 - c, g))
    hd = pl.BlockSpec((CHUNK, LANES), lambda c, g: (nc - 1 - c, 0))
    return pl.pallas_call(
        body, name=name, grid=(nc, SSD_GROUPS),
        in_specs=_ssd_specs(nc, True) + [pl.BlockSpec((1, SSD_STATE, SSD_GW), lambda c, g: ((nc - 1 - c) * SSD_GROUPS + g, 0, 0)), grp, grp],
        out_specs=[grp, st, st, hd, hd],
        out_shape=[jax.ShapeDtypeStruct((t, SSD_D_INNER), F32), jax.ShapeDtypeStruct((t, SSD_GROUPS * SSD_STATE), F32),
                   jax.ShapeDtypeStruct((t, SSD_GROUPS * SSD_STATE), F32), jax.ShapeDtypeStruct((t, LANES), F32),
                   jax.ShapeDtypeStruct((t, LANES), F32)],
        scratch_shapes=[pltpu.VMEM((SSD_GROUPS, SSD_STATE, SSD_GW), F32)],
        compiler_params=_params(("arbitrary", "arbitrary")),
    )(xact, xact, xact, dt, da, hsave, dy, dxskip)


def _ssd_act(row0, xc):
    return (jnp.where(_valid(row0, xc.shape[0]), jax.nn.silu(xc), 0.0),)


def _ssd_dt(row0, dtraw, dt_bias, a_log):
    dt = jnp.where(_valid(row0, dtraw.shape[0]), _softplus(dtraw + dt_bias), 0.0)
    return dt, dt * -jnp.exp(a_log)


def _ssd_post(row0, y, xs, z, d_skip, norm_g):
    hr = lax.broadcasted_iota(jnp.int32, (LANES, SSD_D_INNER), 0)
    hc = lax.broadcasted_iota(jnp.int32, (LANES, SSD_D_INNER), 1)
    expand = (hr == hc // SSD_HEAD_DIM).astype(F32)
    d_e = jnp.sum(_hp(jnp.broadcast_to(d_skip, (SUB, LANES)), expand), axis=0, keepdims=True) * (1.0 / SUB)
    return (_rms((y + xs * d_e) * jax.nn.silu(z), norm_g),)


ROPE_LO, ROPE_MID, ROPE_HI = MLA_NOPE, MLA_NOPE + MLA_ROPE // 2, MLA_NOPE + MLA_ROPE
ATT_SCALE = (MLA_NOPE + MLA_ROPE) ** -0.5


def _slot_lane(width):
    return lax.broadcasted_iota(jnp.int32, (1, width), 1) % LANES


def _swap_halves(x):
    width = x.shape[1]
    lane = _slot_lane(width)
    sw = jnp.where(lane < ROPE_MID, pltpu.roll(x, width - MLA_ROPE // 2, 1), pltpu.roll(x, MLA_ROPE // 2, 1))
    return jnp.where((lane >= ROPE_LO) & (lane < ROPE_HI), sw, 0.0)


def _rope(x, cos, sin):
    n = x.shape[1] // LANES
    return x * jnp.tile(cos, (1, n)) + _swap_halves(x) * jnp.tile(sin, (1, n))


def _rope_t(dy, cos, sin):
    n = dy.shape[1] // LANES
    return dy * jnp.tile(cos, (1, n)) + _swap_halves(dy * jnp.tile(sin, (1, n)))


ATT_SCALE2 = ATT_SCALE * float(np.log2(np.e))
MASKED = -1e30


def _att_bias(blk):
    r = jnp.arange(blk)[:, None]
    c = jnp.arange(blk)[None, :]
    zero = jnp.zeros((blk, blk), F32)
    first = jnp.where(c >= PAD, 0.0, MASKED) + zero
    diag = jnp.where(c <= r, 0.0, MASKED).astype(F32)
    return jnp.stack([zero, first, diag, jnp.minimum(first, diag), zero + MASKED])


def _att_bias_index(j, i):
    return jnp.where(j > i, 4, jnp.where(j == 0, 1, 0) + jnp.where(j == i, 2, 0))


def _key_slots(row0, kv, kr):
    width = kv.shape[1]
    return jnp.where(_slot_lane(width) < MLA_NOPE, kv, jnp.tile(kr, (1, width // LANES))), kv


def _attn_fwd(qr, km, vb, name, carried=None):
    t = qr.shape[0]
    blk = _tile(t, 384, LANES)
    nq = t // blk

    bias = _att_bias(blk)

    def body(q_ref, k_ref, v_ref, b_ref, o_ref, s0, s1, p0, p1):
        i = pl.program_id(1)
        lane = lax.broadcasted_iota(jnp.int32, (1, LANES), 1)
        qb = q_ref[...]

        def rows(j):
            return pl.ds(pl.multiple_of(jnp.clip(j, 0, i) * blk, blk), blk)

        def scores(j):
            return lax.dot_general(qb, k_ref[rows(j), :], _DIMS["nt"], preferred_element_type=F32) + b_ref[_att_bias_index(j, i)]

        def half(j, car, s_cur, s_nxt, p_cur, p_prv):
            m, l, acc, al_prev = car
            s_nxt[...] = scores(j + 1)
            acc2 = al_prev * acc + lax.dot_general(p_prv[...], v_ref[rows(j - 1), :], _DIMS["nn"], preferred_element_type=F32)
            m2 = jnp.maximum(m, jnp.max(s_cur[...], axis=1, keepdims=True))
            al = jnp.exp2((m - m2) * ATT_SCALE2)
            pm = jnp.exp2(s_cur[...] * ATT_SCALE2 - m2 * ATT_SCALE2)
            p_cur[...] = pm.astype(BF16)
            return m2, al * l + jnp.sum(pm, axis=1, keepdims=True), acc2, al

        def step(jj, car):
            car = half(2 * jj, car, s0, s1, p0, p1)
            return half(2 * jj + 1, car, s1, s0, p1, p0)

        s0[...] = scores(0)
        p1[...] = jnp.zeros((blk, blk), BF16)
        car = (jnp.full((blk, 1), MASKED, F32), jnp.zeros((blk, 1), F32), jnp.zeros((blk, LANES), F32), jnp.ones((blk, 1), F32))
        steps = i // 2 + 1
        m, l, acc, al_last = lax.fori_loop(0, steps, step, car)
        acc = al_last * acc + lax.dot_general(p1[...], v_ref[rows(2 * steps - 1), :], _DIMS["nn"], preferred_element_type=F32)
        out = jnp.where(lane >= MLA_NOPE, acc / l, m * ATT_SCALE + jnp.log(l))
        o_ref[...] = jnp.where(_valid(i * blk, blk), out, 0.0)

    seq_h = pl.BlockSpec((t, LANES), lambda h, i: (0, h))
    (o,), carried_out = _carry_call(
        body, name, (MLA_HEADS, nq),
        [pl.BlockSpec((blk, LANES), lambda h, i: (i, h)), seq_h, seq_h, _full_spec(bias.shape, 2)],
        [pl.BlockSpec((blk, LANES), lambda h, i: (i, h))], [jax.ShapeDtypeStruct((t, MLA_HEADS * LANES), F32)],
        [pltpu.VMEM((blk, blk), F32)] * 2 + [pltpu.VMEM((blk, blk), BF16)] * 2, (qr, km, vb, bias), carried)
    return o, carried_out


def _attn_bwd(qr, km, vb, o, do, cos, sin, name, carried=None):
    t = qr.shape[0]
    blk = _tile(t, 384, LANES)
    nq = t // blk

    bias = _att_bias(blk)
    log2e = float(np.log2(np.e))

    def body(q_ref, o_ref, do_ref, k_ref, v_ref, b_ref, cos_ref, sin_ref, dq_out, dkv_ref, dkr_ref,
             s0, s1, dp0, dp1, p0, p1, ds0, ds1, dk_s, dv_s, dq_ref):
        h, j = pl.program_id(0), pl.program_id(1)
        lane = lax.broadcasted_iota(jnp.int32, (1, LANES), 1)

        @pl.when(j == 0)
        def _():
            dq_ref[...] = jnp.zeros_like(dq_ref)

        @pl.when((h == 0) & (j == 0))
        def _():
            dkr_ref[...] = jnp.zeros_like(dkr_ref)

        kmat, vmat = k_ref[...], v_ref[...]

        def rows(i):
            return pl.ds(pl.multiple_of(jnp.clip(i, j, nq - 1) * blk, blk), blk)

        def first_stage(i, s_buf, dp_buf):
            ic = jnp.minimum(i, nq - 1)
            s_buf[...] = lax.dot_general(q_ref[rows(ic), :], kmat, _DIMS["nt"], preferred_element_type=F32) + b_ref[_att_bias_index(j, ic)]
            dp_buf[...] = lax.dot_general(do_ref[rows(ic), :].astype(BF16), vmat, _DIMS["nt"], preferred_element_type=F32)

        def middle_stage(i, s_buf, dp_buf, p_buf, ds_buf):
            r = rows(i)
            ob, dob = o_ref[r, :], do_ref[r, :].astype(F32)
            delta = jnp.sum(dob * ob, axis=1, keepdims=True)
            pm = jnp.exp2(s_buf[...] * ATT_SCALE2 - ob[:, 0:1] * log2e)
            p_buf[...] = pm.astype(BF16)
            ds_buf[...] = (pm * (dp_buf[...] - delta) * ATT_SCALE).astype(BF16)

        def last_stage(i, p_buf, ds_buf):
            r = rows(i)
            dv_s[...] += lax.dot_general(p_buf[...], do_ref[r, :].astype(BF16), _DIMS["tn"], preferred_element_type=F32)
            dk_s[...] += lax.dot_general(ds_buf[...], q_ref[r, :], _DIMS["tn"], preferred_element_type=F32)
            dq_ref[r, :] += lax.dot_general(ds_buf[...], kmat, _DIMS["nn"], preferred_element_type=F32)

        n = nq - j
        dk_s[...] = jnp.zeros((blk, LANES), F32)
        dv_s[...] = jnp.zeros((blk, LANES), F32)
        first_stage(j, s0, dp0)
        first_stage(j + 1, s1, dp1)
        middle_stage(j, s0, dp0, p0, ds0)

        def step(tt, carry):
            i = j + 2 * tt + 1
            first_stage(i + 1, s0, dp0)
            last_stage(i - 1, p0, ds0)
            middle_stage(i, s1, dp1, p1, ds1)
            first_stage(i + 2, s1, dp1)
            last_stage(i, p1, ds1)
            middle_stage(i + 1, s0, dp0, p0, ds0)
            return carry

        lax.fori_loop(0, (n - 1) // 2, step, 0)

        @pl.when(n % 2 == 0)
        def _():
            last_stage(nq - 2, p0, ds0)
            middle_stage(nq - 1, s1, dp1, p1, ds1)
            last_stage(nq - 1, p1, ds1)

        @pl.when(n % 2 == 1)
        def _():
            last_stage(nq - 1, p0, ds0)

        dk = dk_s[...]
        dkv_ref[...] = jnp.where(lane < MLA_NOPE, dk, dv_s[...]).astype(dkv_ref.dtype)
        dkr_ref[rows(j), :] += jnp.where(lane >= MLA_NOPE, dk, 0.0)

        @pl.when(j == nq - 1)
        def _():
            dq_out[...] = _rope_t(dq_ref[...], cos_ref[...], sin_ref[...]).astype(dq_out.dtype)

    seq_h = pl.BlockSpec((t, LANES), lambda h, j: (0, h))
    seq = pl.BlockSpec((t, LANES), lambda h, j: (0, 0))
    blk_h = pl.BlockSpec((blk, LANES), lambda h, j: (j, h))
    return _carry_call(
        body, name, (MLA_HEADS, nq), [seq_h, seq_h, seq_h, blk_h, blk_h, _full_spec(bias.shape, 2), seq, seq],
        [seq_h, blk_h, seq],
        [jax.ShapeDtypeStruct((t, MLA_HEADS * LANES), BF16), jax.ShapeDtypeStruct((t, MLA_HEADS * LANES), BF16),
         jax.ShapeDtypeStruct((t, LANES), F32)],
        [pltpu.VMEM((blk, blk), F32)] * 4 + [pltpu.VMEM((blk, blk), BF16)] * 4 + [pltpu.VMEM((blk, LANES), F32)] * 2
        + [pltpu.VMEM((t, LANES), F32)], (qr, o, do, km, vb, bias, cos, sin), carried)


def _rms_rows(row0, x, g):
    return (_rms(x, g),)


def _ssdmla_fwd(h, p, l, e, cos, sin, carried=None):
    hn = _prenorm(h, p["mix_pre_g"][l], "sm_prenorm")
    proj = _mm(hn, p["w_in"][e], "nn", "sm_in")
    xc = _conv_fwd(proj, p["ssd_conv_w"][e], p["ssd_conv_b"][e], "ssd_conv", cw=SSD_GW, c0=PROJ_XBC // SSD_GW)
    xact = _rowwise("ssd_act", _ssd_act, [xc], [], [(SSD_CONV_CH, F32)])[0]
    dt, da = _rowwise("ssd_dt", _ssd_dt, [(proj, LANES, PROJ_DT // LANES)], [p["ssd_dt_bias"][e], p["ssd_a_log"][e]],
                      [(LANES, F32)] * 2)
    y, hsave = _ssd_scan(xact, dt, da, "ssd_scan")
    y_ssd = _rowwise("ssd_post", _ssd_post, [y, (xact, SSD_D_INNER, 0), (proj, SSD_D_INNER, 0)],
                     [p["ssd_d"][e], p["ssd_norm_g"][e]], [(SSD_D_INNER, BF16)])[0]
    cqn = _prenorm((proj, MLA_Q_RANK, PROJ_CQ // MLA_Q_RANK), p["mla_q_norm_g"][e], "mla_qnorm")
    ckvn = _prenorm((proj, MLA_KV_RANK, PROJ_CKV // MLA_KV_RANK), p["mla_kv_norm_g"][e], "mla_kvnorm")
    kr = _rowwise("mla_krope", lambda r0, x, c, s: (_rope(x, c, s),), [(proj, LANES, PROJ_KR // LANES), cos, sin], [],
                  [(LANES, F32)])[0]
    qr = _mm(cqn, p["mla_w_q_up"][e], "nn", "mla_q_up", slots=[cos, sin], post=lambda v, c, s: (_rope(v, c, s),), out_dtypes=[BF16])
    km, vb = _mm(ckvn, p["mla_w_kv_up"][e], "nn", "mla_kv_up", slots=[kr], post=lambda v, k: _key_slots(0, v, k),
                 out_dtypes=[BF16, BF16])
    o, carried_out = _attn_fwd(qr, km, vb, "mla_attn", carried)
    m1 = _mm(y_ssd, p["w_out_ssd"][e], "nn", "sm_out_ssd")
    m, h2 = _mm(o, p["w_out_att"][e], "nn", "sm_out_att", extra=[m1, h], vecs=[p["mix_post_g"][l]],
                post=lambda v, m1b, hb, g: _post_residual(v + m1b, hb, g), out_dtypes=[F32, F32])
    return h2, (h, hn, proj, xc, xact, dt, da, y, hsave, y_ssd, cqn, ckvn, qr, km, vb, o, m), carried_out


def _ssdmla_bwd(dh, saved, p, l, e, cos, sin, grads, carry=None):
    h, hn, proj, xc, xact, dt, da, y, hsave, y_ssd, cqn, ckvn, qr, km, vb, o, m = saved
    dm, grads["mix_post_g"][l] = _postnorm_bwd(m, p["mix_post_g"][l], dh, "sm_postnorm_bwd")
    grads["w_out_ssd"][e] = _mm(y_ssd, dm, "tn", "sm_out_ssd_dw")
    grads["w_out_att"][e] = _mm(o, dm, "tn", "sm_out_att_dw")
    dy_ssd = _mm(dm, p["w_out_ssd"][e], "nt", "sm_out_ssd_dx")
    do = _mm(dm, p["w_out_att"][e], "nt", "sm_out_att_dx", out_dtype=BF16)
    (dq, dkv, dkr), carried_out = _attn_bwd(qr, km, vb, o, do, cos, sin, "mla_attn_bwd", carry() if carry else None)
    dkr_raw = _rowwise("mla_krope_bwd", lambda r0, d, c, s: (_rope_t(d, c, s),), [dkr, cos, sin], [], [(LANES, F32)])[0]
    grads["mla_w_q_up"][e] = _mm(cqn, dq, "tn", "mla_q_up_dw")
    dcqn = _mm(dq, p["mla_w_q_up"][e], "nt", "mla_q_up_dx")
    (dcq,), (grads["mla_q_norm_g"][e],) = _rowwise_vjp(
        "mla_qnorm_bwd", _rms_rows, [(proj, MLA_Q_RANK, PROJ_CQ // MLA_Q_RANK)], [p["mla_q_norm_g"][e]], [dcqn])
    grads["mla_w_kv_up"][e] = _mm(ckvn, dkv, "tn", "mla_kv_up_dw")
    dckvn = _mm(dkv, p["mla_w_kv_up"][e], "nt", "mla_kv_up_dx")
    (dckv,), (grads["mla_kv_norm_g"][e],) = _rowwise_vjp(
        "mla_kvnorm_bwd", _rms_rows, [(proj, MLA_KV_RANK, PROJ_CKV // MLA_KV_RANK)], [p["mla_kv_norm_g"][e]], [dckvn])
    (dy, dxskip, dz), (grads["ssd_d"][e], grads["ssd_norm_g"][e]) = _rowwise_vjp(
        "ssd_post_bwd", _ssd_post, [y, (xact, SSD_D_INNER, 0), (proj, SSD_D_INNER, 0)], [p["ssd_d"][e], p["ssd_norm_g"][e]], [dy_ssd])
    dxs, db, dc, ddt, dda = _ssd_scan_bwd(xact, dt, da, hsave, dy, dxskip, "ssd_scan_bwd")
    dxact = jnp.concatenate([dxs, db, dc], axis=1)
    (dxc,), _ = _rowwise_vjp("ssd_act_bwd", _ssd_act, [xc], [], [dxact])
    dxbc, grads["ssd_conv_w"][e], grads["ssd_conv_b"][e] = _conv_bwd(
        proj, p["ssd_conv_w"][e], dxc, "ssd_conv_bwd", cw=SSD_GW, c0=PROJ_XBC // SSD_GW)
    (ddtraw,), (grads["ssd_dt_bias"][e], grads["ssd_a_log"][e]) = _rowwise_vjp(
        "ssd_dt_bwd", _ssd_dt, [(proj, LANES, PROJ_DT // LANES)], [p["ssd_dt_bias"][e], p["ssd_a_log"][e]], [ddt, dda])
    dproj = jnp.concatenate([dz, dxbc, ddtraw, dcq, dckv, dkr_raw], axis=1).astype(BF16)
    grads["w_in"][e] = _mm(hn, dproj, "tn", "sm_in_dw")
    dhn = _mm(dproj, p["w_in"][e], "nt", "sm_in_dx")
    dh, grads["mix_pre_g"][l] = _prenorm_bwd_add(h, p["mix_pre_g"][l], [dhn], dh, "sm_prenorm_bwd")
    return dh, carried_out


GAINS = ("mix_pre_g", "mix_post_g", "mlp_pre_g", "mlp_post_g", "ssd_norm_g", "mla_q_norm_g", "mla_kv_norm_g", "ssd_conv_b", "rg_conv_b")
HEAD_VECS = ("ssd_dt_bias", "ssd_a_log", "ssd_d")
LRU_VECS = ("rg_b_a", "rg_b_i", "rg_lambda")
IN_DT_END = SSD_D_INNER + SSD_CONV_CH + SSD_HEADS
IN_KR = IN_DT_END + MLA_Q_RANK + MLA_KV_RANK


def _each(a, f):
    layers = a if isinstance(a, list) else [a[i] for i in range(a.shape[0])]
    return [None if x is None else f(x) for x in layers]


def _layout_params(w):
    p = {k: _each(w[k], lambda a: a[None, :]) for k in GAINS}
    for k in HEAD_VECS:
        p[k] = _each(w[k], lambda a: jnp.pad(a, (0, LANES - SSD_HEADS))[None, :])
    for k in LRU_VECS:
        p[k] = _each(w[k], lambda a: a.reshape(LRU_BLOCKS, 1, LRU_BLOCK))
    for k in ("w_up", "w_down", "mla_w_kv_up", "rg_w_x", "rg_w_y", "rg_w_out"):
        p[k] = _each(w[k], lambda a: a if isinstance(a, Gathered) else a.astype(BF16))
    for k in ("ssd_conv_w", "rg_conv_w", "rg_w_a", "rg_w_i"):
        p[k] = _each(w[k], lambda a: a)

    def w_in(a):
        def zcols(n):
            return jnp.zeros((a.shape[0], n), a.dtype)

        return jnp.concatenate([a[:, :IN_DT_END], zcols(PROJ_CQ - IN_DT_END), a[:, IN_DT_END:IN_KR], zcols(ROPE_LO),
                                a[:, IN_KR:], zcols(LANES - ROPE_HI)], axis=1).astype(BF16)

    def q_up(a):
        a = a.reshape(MLA_Q_RANK, MLA_HEADS, MLA_NOPE + MLA_ROPE)
        return jnp.pad(a, ((0, 0), (0, 0), (0, LANES - MLA_NOPE - MLA_ROPE))).reshape(MLA_Q_RANK, MLA_HEADS * LANES).astype(BF16)

    def out_att(a):
        a = a[SSD_D_INNER:].reshape(MLA_HEADS, MLA_V, D_MODEL)
        return jnp.pad(a, ((0, 0), (LANES - MLA_V, 0), (0, 0))).reshape(MLA_HEADS * LANES, D_MODEL).astype(BF16)

    p["w_in"] = _each(w["w_in"], w_in)
    p["mla_w_q_up"] = _each(w["mla_w_q_up"], q_up)
    p["w_out_ssd"] = _each(w["w_out_ab"], lambda a: a[:SSD_D_INNER].astype(BF16))
    p["w_out_att"] = _each(w["w_out_ab"], out_att)
    return p


def _rope_tables(t):
    pos = (jnp.arange(t) - PAD).astype(F32)
    inv = ROPE_BASE ** (-jnp.arange(0, MLA_ROPE, 2, dtype=F32) / MLA_ROPE)
    ang = pos[:, None] * inv[None, :]
    c, s = jnp.cos(ang), jnp.sin(ang)
    one, zero = jnp.ones((t, MLA_NOPE), F32), jnp.zeros((t, MLA_NOPE), F32)
    tail = LANES - ROPE_HI
    return (jnp.concatenate([one, c, c, one[:, :tail]], axis=1), jnp.concatenate([zero, -s, s, zero[:, :tail]], axis=1))


GRAD_KEYS = GAINS + HEAD_VECS + LRU_VECS + ("w_up", "w_down", "mla_w_kv_up", "rg_w_x", "rg_w_y", "rg_w_out", "ssd_conv_w",
                                            "rg_conv_w", "rg_w_a", "rg_w_i", "w_in", "mla_w_q_up", "w_out_ssd", "w_out_att")


def _device_step(x, meta, target, p, hooks=None):
    t = PAD + N_META + x.shape[0]
    cos, sin = _rope_tables(t)
    h = jnp.concatenate([jnp.zeros((PAD, D_MODEL), F32), meta, x], axis=0)
    n_even, n_odd = (DEPTH + 1) // 2, DEPTH // 2
    saved = []
    for l in range(DEPTH):
        if l % 2 == 0:
            carried = hooks.forward_exchange() if hooks and l == 0 else None
            h, sm, arrived = _ssdmla_fwd(h, p, l, l // 2, cos, sin, carried)
            if carried is not None:
                p = hooks.after_forward_exchange(arrived)
        else:
            h, sm = _rglru_fwd(h, p, l, l // 2)
        h, sp = _mlp_fwd(h, p, l)
        saved.append((sm, sp))
    sq, dh = _loss_and_grad(h, target, "loss")
    per_layer = {"mix_pre_g": DEPTH, "mix_post_g": DEPTH, "mlp_pre_g": DEPTH, "mlp_post_g": DEPTH, "w_up": DEPTH, "w_down": DEPTH}
    grads = {k: [None] * per_layer.get(k, n_odd if k.startswith("rg_") else n_even) for k in GRAD_KEYS}
    for l in reversed(range(DEPTH)):
        sm, sp = saved[l]
        dh = _mlp_bwd(dh, sp, p, l, grads)
        if l % 2 == 0:
            carry = functools.partial(hooks.backward_exchange, grads, l) if hooks else None
            dh, arrived = _ssdmla_bwd(dh, sm, p, l, l // 2, cos, sin, grads, carry)
            if hooks:
                hooks.after_backward_exchange(arrived, l)
        else:
            dh = _rglru_bwd(dh, sm, p, l, l // 2, grads)
    return sq, dh, grads


MESH = pl.DeviceIdType.MESH
ANY = pl.BlockSpec(memory_space=pl.ANY)


def _mesh_pos():
    return lax.axis_index("x"), lax.axis_index("y"), lax.axis_index("c")


def _other_chips(x, y):
    return [(1 - x, y), (x, 1 - y), (1 - x, 1 - y)]


def _remote(src, dst, send_sems, recv_sems, k, to):
    return pltpu.make_async_remote_copy(src_ref=src, dst_ref=dst, send_sem=send_sems.at[k], recv_sem=recv_sems.at[k],
                                        device_id=to, device_id_type=MESH)


class Exchange:
    def __init__(self, ins, outs, aliases, n_sems, plan):
        self.ins, self.outs, self.aliases, self.n_sems, self.plan = list(ins), list(outs), dict(aliases), n_sems, plan


def _sems(n):
    return [pltpu.SemaphoreType.DMA((n,)), pltpu.SemaphoreType.DMA((n,))]


def _run_exchange(name, ex):
    ni, no = len(ex.ins), len(ex.outs)

    def body(*refs):
        sends = ex.plan(refs[:ni], refs[ni:ni + no], refs[-2], refs[-1], False)
        for cp in sends:
            cp.start()
        for cp in ex.plan(refs[:ni], refs[ni:ni + no], refs[-2], refs[-1], True):
            cp.wait_recv()
        for cp in sends:
            cp.wait_send()

    return pl.pallas_call(body, name=name, in_specs=[ANY] * ni, out_specs=[ANY] * no, out_shape=ex.outs,
                          input_output_aliases=ex.aliases, scratch_shapes=_sems(ex.n_sems))(*ex.ins)


def _carry_call(body, name, grid, in_specs, out_specs, out_shape, scratch_shapes, args, ex):
    if ex is None:
        res = pl.pallas_call(body, name=name, grid=grid, in_specs=in_specs, out_specs=out_specs, out_shape=out_shape,
                             scratch_shapes=scratch_shapes, compiler_params=_params(("arbitrary",) * len(grid)))(*args)
        return res, None
    ni, no, ns, xi, xo = len(in_specs), len(out_specs), len(scratch_shapes), len(ex.ins), len(ex.outs)

    def wrapped(*refs):
        ins, xin = refs[:ni], refs[ni:ni + xi]
        outs, xout = refs[ni + xi:ni + xi + no], refs[ni + xi + no:ni + xi + no + xo]
        scr, send_sems, recv_sems = refs[ni + xi + no + xo:-2], refs[-2], refs[-1]
        pid = [pl.program_id(d) for d in range(len(grid))]
        first = functools.reduce(jnp.logical_and, [p == 0 for p in pid])
        last = functools.reduce(jnp.logical_and, [p == g - 1 for p, g in zip(pid, grid)])

        @pl.when(first)
        def _():
            for cp in ex.plan(xin, xout, send_sems, recv_sems, False):
                cp.start()

        body(*ins, *outs, *scr)

        @pl.when(last)
        def _():
            for cp in ex.plan(xin, xout, send_sems, recv_sems, True):
                cp.wait_recv()
            for cp in ex.plan(xin, xout, send_sems, recv_sems, False):
                cp.wait_send()

    res = pl.pallas_call(
        wrapped, name=name, grid=grid, in_specs=list(in_specs) + [ANY] * xi, out_specs=list(out_specs) + [ANY] * xo,
        out_shape=list(out_shape) + ex.outs, scratch_shapes=list(scratch_shapes) + _sems(ex.n_sems),
        input_output_aliases={ni + i: no + o for i, o in ex.aliases.items()},
        compiler_params=_params(("arbitrary",) * len(grid)))(*args, *ex.ins)
    return res[:no], res[no:]


def _gather_ici(srcs, bufs, ranges):
    n = len(srcs)

    def plan(in_refs, out_refs, ss, rs, arrivals):
        x, y, c = _mesh_pos()
        cps = []
        for t, (l0, nl) in enumerate(ranges):
            if nl:
                s, o, lr = in_refs[t], out_refs[t], pl.ds(l0, nl)
                for j, (cx, cy) in enumerate(_other_chips(x, y)):
                    chip = 2 * cx + cy if arrivals else 2 * x + y
                    cps.append(_remote(s.at[lr, c], o.at[chip, lr, c], ss, rs, (N_CHIPS - 1) * t + j, (cx, cy, c)))
        return cps

    outs = [jax.ShapeDtypeStruct((N_CHIPS,) + s.shape, s.dtype) for s in srcs]
    if bufs is None:
        return Exchange(srcs, outs, {}, (N_CHIPS - 1) * n, plan)
    return Exchange(list(srcs) + list(bufs), outs, {n + t: t for t in range(n)}, (N_CHIPS - 1) * n, plan)


def _gather_d2d(srcs, bufs, ranges):
    n = len(srcs)

    def plan(in_refs, out_refs, ss, rs, arrivals):
        x, y, c = _mesh_pos()
        sib, me = (x, y, 1 - c), 2 * x + y
        cps = []
        for t, (l0, nl) in enumerate(ranges):
            if nl:
                s, o, lr = in_refs[t], out_refs[t], pl.ds(l0, nl)
                for j, (cx, cy) in enumerate(_other_chips(x, y)):
                    slot = o.at[2 * cx + cy, lr, c]
                    cps.append(_remote(slot, o.at[2 * cx + cy, lr, 1 - c] if arrivals else slot, ss, rs, N_CHIPS * t + j, sib))
                cps.append(_remote(s.at[lr], o.at[me, lr], ss, rs, N_CHIPS * t + N_CHIPS - 1, sib))
        return cps

    outs = [jax.ShapeDtypeStruct(b.shape, b.dtype) for b in bufs]
    return Exchange(list(srcs) + list(bufs), outs, {n + t: t for t in range(n)}, N_CHIPS * n, plan)


def _gather_chips(srcs, name):
    ranges = [(0, s.shape[0]) for s in srcs]
    bufs = _run_exchange(name + "_ici", _gather_ici(srcs, None, ranges))
    return _run_exchange(name + "_d2d", _gather_d2d(srcs, bufs, ranges))


def _pair_exchange(gs):
    def plan(in_refs, out_refs, ss, rs, arrivals):
        x, y, c = _mesh_pos()
        return [_remote(g.at[pl.ds(0, N_CHIPS), 1 - c], o, ss, rs, t, (x, y, 1 - c)) for t, (g, o) in enumerate(zip(in_refs, out_refs))]

    return Exchange(gs, [jax.ShapeDtypeStruct((g.shape[0],) + g.shape[2:], g.dtype) for g in gs], {}, len(gs), plan)


def _chip_exchange(ps, slots, qs, q_shapes):
    n = len(ps)
    kept = [g for g, q in enumerate(qs) if q is not None]

    def plan(in_refs, out_refs, ss, rs, arrivals):
        x, y, c = _mesh_pos()
        return [_remote(in_refs[t].at[2 * cx + cy], out_refs[g].at[j, li], ss, rs, (N_CHIPS - 1) * t + j, (cx, cy, c))
                for t, (g, li) in enumerate(slots) for j, (cx, cy) in enumerate(_other_chips(x, y))]

    return Exchange(list(ps) + [qs[g] for g in kept], q_shapes, {n + i: g for i, g in enumerate(kept)}, (N_CHIPS - 1) * n, plan)


def _pair_share(fs):
    def plan(in_refs, out_refs, ss, rs, arrivals):
        x, y, c = _mesh_pos()
        return [_remote(o.at[pl.ds(0, o.shape[0]), c], o.at[pl.ds(0, o.shape[0]), 1 - c if arrivals else c], ss, rs, t, (x, y, 1 - c))
                for t, o in enumerate(out_refs)]

    return Exchange(fs, [jax.ShapeDtypeStruct(f.shape, f.dtype) for f in fs], {t: t for t in range(len(fs))}, len(fs), plan)


SUM_BLOCK = 512 * 1024


def _sum_pairs(gs, ras, c, name):
    k = len(gs)
    n, _, h, w = gs[0].shape
    tr = _tile(h, max(16, SUM_BLOCK // (w * k)), 16)

    def body(c_ref, *refs):
        for g_ref, r_ref, o_ref in zip(refs[:k], refs[k:2 * k], refs[2 * k:]):
            o_ref[...] = (g_ref[0] + r_ref[...]).astype(o_ref.dtype)

    half = pl.BlockSpec((1, tr, w), lambda s, i, cr: (s, i, 0))
    return pl.pallas_call(
        body, name=name,
        grid_spec=pltpu.PrefetchScalarGridSpec(
            num_scalar_prefetch=1, grid=(n, h // tr),
            in_specs=[pl.BlockSpec((1, 1, tr, w), lambda s, i, cr: (s, cr[0], i, 0))] * k + [half] * k, out_specs=[half] * k),
        out_shape=[jax.ShapeDtypeStruct((n, h, w), BF16)] * k,
        compiler_params=_params(("parallel", "parallel")),
    )(c.reshape(1).astype(jnp.int32), *gs, *ras)


def _sum_pair_groups(pieces, ras, c, name):
    by_shape = {}
    for key, g in pieces.items():
        by_shape.setdefault(g.shape, []).append(key)
    out = {}
    for keys in by_shape.values():
        sums = _sum_pairs([pieces[key] for key in keys], [ras[key] for key in keys], c, name)
        out.update(zip(keys, sums))
    return out


def _sum_pair(g, ra, c, name):
    return _sum_pairs([g], [ra], c, name)[0]


def _sum_chips(ps, q, pos, name):
    nc, nl, h, w = q.shape
    tr = _tile(h, max(16, SUM_BLOCK // (w * nl)), 16)

    def body(x_ref, y_ref, c_ref, *refs):
        q_ref, o_ref = refs[nl], refs[nl + 1]
        for l in range(nl):
            acc = refs[l][0].astype(F32)
            for j in range(nc):
                acc = acc + q_ref[j, l].astype(F32)
            o_ref[l] = acc

    return pl.pallas_call(
        body, name=name,
        grid_spec=pltpu.PrefetchScalarGridSpec(
            num_scalar_prefetch=3, grid=(h // tr,),
            in_specs=[pl.BlockSpec((1, tr, w), lambda i, x, y, c: (2 * x[0] + y[0], i, 0))] * nl
            + [pl.BlockSpec((nc, nl, tr, w), lambda i, x, y, c: (0, 0, i, 0))],
            out_specs=pl.BlockSpec((nl, None, tr, w), lambda i, x, y, c: (0, c[0], i, 0))),
        out_shape=jax.ShapeDtypeStruct((nl, 2, h, w), F32),
        compiler_params=_params(("parallel",)),
    )(*pos, *ps, q)


def _adamw(g, w, m, v, name):
    def f(r0, gg, ww, mm, vv):
        m2 = ADAM_B1 * mm + (1.0 - ADAM_B1) * gg
        v2 = ADAM_B2 * vv + (1.0 - ADAM_B2) * jnp.square(gg)
        m_hat = m2 / (1.0 - ADAM_B1 ** ADAM_STEP)
        v_hat = v2 / (1.0 - ADAM_B2 ** ADAM_STEP)
        return gg, -ADAM_LR * (m_hat / (jnp.sqrt(v_hat) + ADAM_EPS) + ADAM_WD * ww), m2, v2

    return _rowwise(name, f, [g, w, m, v], [], [(g.shape[1], F32)] * 4, tr=_tile(g.shape[0], 512))


WEIGHTS = (
    ("meta_tokens", (N_META, D_MODEL), 1), ("mix_pre_g", (DEPTH, D_MODEL), None), ("mix_post_g", (DEPTH, D_MODEL), None),
    ("mlp_pre_g", (DEPTH, D_MODEL), None), ("mlp_post_g", (DEPTH, D_MODEL), None), ("w_up", (DEPTH, D_MODEL, D_FF), 2),
    ("w_down", (DEPTH, D_FF, D_MODEL), 1), ("w_in", (2, D_MODEL, 3248), 2), ("ssd_conv_w", (2, CONV_K, SSD_CONV_CH), 2),
    ("ssd_conv_b", (2, SSD_CONV_CH), None), ("ssd_dt_bias", (2, SSD_HEADS), None), ("ssd_a_log", (2, SSD_HEADS), None),
    ("ssd_d", (2, SSD_HEADS), None), ("ssd_norm_g", (2, SSD_D_INNER), None), ("mla_q_norm_g", (2, MLA_Q_RANK), None),
    ("mla_w_q_up", (2, MLA_Q_RANK, MLA_HEADS * (MLA_NOPE + MLA_ROPE)), 2), ("mla_kv_norm_g", (2, MLA_KV_RANK), None),
    ("mla_w_kv_up", (2, MLA_KV_RANK, MLA_HEADS * (MLA_NOPE + MLA_V)), 2), ("w_out_ab", (2, SSD_D_INNER + MLA_HEADS * MLA_V, D_MODEL), 1),
    ("rg_w_x", (2, D_MODEL, LRU_WIDTH), 2), ("rg_w_y", (2, D_MODEL, LRU_WIDTH), 2), ("rg_conv_w", (2, CONV_K, LRU_WIDTH), 2),
    ("rg_conv_b", (2, LRU_WIDTH), 1), ("rg_w_a", (2, LRU_BLOCKS, LRU_BLOCK, LRU_BLOCK), None), ("rg_b_a", (2, LRU_WIDTH), 1),
    ("rg_w_i", (2, LRU_BLOCKS, LRU_BLOCK, LRU_BLOCK), None), ("rg_b_i", (2, LRU_WIDTH), 1), ("rg_lambda", (2, LRU_WIDTH), 1),
    ("rg_w_out", (2, LRU_WIDTH, D_MODEL), 1),
)
BIG = {"w_up": "col", "w_down": "row", "w_in": "col", "mla_w_q_up": "col", "mla_w_kv_up": "col", "w_out_ab": "row",
       "rg_w_x": "col", "rg_w_y": "col", "rg_w_out": "row"}
DIRECT = ("w_up", "w_down")
FLAT_QUANTUM = 2 * 16 * LANES
TABLE = {name: (shape, d) for name, shape, d in WEIGHTS}
SMALL_SHARDED = tuple(name for name, _, d in WEIGHTS if d is not None and name not in BIG)
REPLICATED = tuple(name for name, _, d in WEIGHTS if d is None)


def _chips_to_full(a, kind):
    if kind == "col":
        return jnp.moveaxis(a, 0, 2).reshape(a.shape[1], a.shape[2], -1)
    return jnp.moveaxis(a, 0, 1).reshape(a.shape[1], -1, a.shape[3])


def _full_to_chips(g, kind):
    if kind == "col":
        return jnp.moveaxis(g.reshape(g.shape[0], N_CHIPS, -1), 1, 0)
    return g.reshape(N_CHIPS, -1, g.shape[1])


def _shard_shape(shape, d):
    return shape[:d] + (shape[d] // N_CHIPS,) + shape[d + 1:]


def _shard_major(full, d):
    s = full.shape
    return jnp.moveaxis(full.reshape(s[:d] + (N_CHIPS, s[d] // N_CHIPS) + s[d + 1:]), d, 0).reshape(N_CHIPS, -1)


def _from_shard_major(a, shape, d):
    ss = _shard_shape(shape, d)
    return jnp.moveaxis(a.reshape((N_CHIPS,) + ss), 0, d).reshape(shape)


def _pad_cols(a, quantum):
    n = a.shape[-1]
    return jnp.pad(a, [(0, 0)] * (a.ndim - 1) + [(0, -n % quantum)])


def _big_pieces(g):
    def w_in(a):
        return jnp.concatenate([a[:, :IN_DT_END], a[:, PROJ_CQ:PROJ_KR], a[:, PROJ_KR + ROPE_LO:PROJ_KR + ROPE_HI]], axis=1)

    def q_up(a):
        return a.reshape(MLA_Q_RANK, MLA_HEADS, LANES)[:, :, :MLA_NOPE + MLA_ROPE].reshape(MLA_Q_RANK, -1)

    def out_ab(sa):
        s, a = sa
        return jnp.concatenate([s, a.reshape(MLA_HEADS, LANES, D_MODEL)[:, LANES - MLA_V:, :].reshape(-1, D_MODEL)], axis=0)

    ident = lambda a: a
    full = {"w_down": _each(g["w_down"], ident), "w_in": _each(g["w_in"], w_in), "mla_w_q_up": _each(g["mla_w_q_up"], q_up),
            "mla_w_kv_up": _each(g["mla_w_kv_up"], ident),
            "w_out_ab": _each([None if s is None or a is None else (s, a) for s, a in zip(g["w_out_ssd"], g["w_out_att"])], out_ab),
            "rg_w_x": _each(g["rg_w_x"], ident), "rg_w_y": _each(g["rg_w_y"], ident), "rg_w_out": _each(g["rg_w_out"], ident)}
    return {name: (list(g[name]) if name == "w_up" else _each(full[name], lambda a, k=BIG[name]: _full_to_chips(a, k))) for name in BIG}


def _small_grads(g, dh):
    out = {k: jnp.stack(g[k])[:, 0, :] for k in GAINS}
    for k in HEAD_VECS:
        out[k] = jnp.stack(g[k])[:, 0, :SSD_HEADS]
    for k in LRU_VECS:
        out[k] = jnp.stack(g[k]).reshape(-1, LRU_WIDTH)
    for k in ("ssd_conv_w", "rg_conv_w", "rg_w_a", "rg_w_i"):
        out[k] = jnp.stack(g[k])
    out["meta_tokens"] = dh[PAD:PAD + N_META]
    return out


class StepExchanges:
    def __init__(self, w):
        self.w = w
        self.c = lax.axis_index("c")
        self.riding, self.ras = {}, {}
        small = _pad_cols(jnp.concatenate([w[n].reshape(-1) for n in SMALL_SHARDED]), FLAT_QUANTUM).reshape(1, 2, -1, LANES)
        self.srcs = [self._halves(w[n].astype(BF16)) for n in BIG] + [small]
        first = {n: (0, 1 if n in ("w_in", "mla_w_q_up", "mla_w_kv_up", "w_out_ab") else 0) for n in BIG}
        self.first = [first[n] for n in BIG] + [(0, 1)]
        self.rest = [(nl, TABLE[n][0][0] - nl) for n, (_, nl) in zip(BIG, self.first)] + [(0, 0)]
        bufs = _run_exchange("gather_first_ici", _gather_ici(self.srcs, None, self.first))
        self.bufs = _run_exchange("gather_first_d2d", _gather_d2d(self.srcs, bufs, self.first))

    @staticmethod
    def _halves(a):
        return a.reshape(a.shape[0], 2, a.shape[1] // 2, a.shape[2])

    def params(self, ranges):
        w = self.w
        full = {n: w[n] for n in REPLICATED}
        for name, buf, (l0, nl) in zip(BIG, self.bufs, ranges):
            a = buf.reshape(buf.shape[:2] + (-1, buf.shape[4]))
            have = range(l0, l0 + nl)
            if name in DIRECT:
                full[name] = [Gathered(a, BIG[name], l) if l in have else None for l in range(a.shape[1])]
            else:
                full[name] = [_chips_to_full(a[:, l:l + 1], BIG[name])[0] if l in have else None for l in range(a.shape[1])]
        got, off = self.bufs[-1].reshape(N_CHIPS, -1), 0
        for name in SMALL_SHARDED:
            shape, d = TABLE[name]
            n = int(np.prod(_shard_shape(shape, d)))
            full[name] = _from_shard_major(got[:, off:off + n], shape, d)
            off += n
        self.meta = full.pop("meta_tokens")
        return _layout_params(full)

    def forward_exchange(self):
        return _gather_ici(self.srcs, self.bufs, self.rest)

    def after_forward_exchange(self, arrived):
        self.bufs = _run_exchange("gather_rest_d2d", _gather_d2d(self.srcs, arrived, self.rest))
        return self.params([(0, TABLE[n][0][0]) for n in BIG])

    def _pair_sums(self, pieces, tag):
        keys = list(pieces)
        ras = _run_exchange("grads_pair_exchange_" + tag, _pair_exchange([pieces[k] for k in keys]))
        return _sum_pair_groups(pieces, dict(zip(keys, ras)), self.c, "grads_pair_sum")

    def _q_shapes(self):
        return [jax.ShapeDtypeStruct((N_CHIPS - 1, s.shape[0]) + s.shape[2:], BF16) for s in self.srcs[:-1]]

    def backward_exchange(self, grads, layer):
        big = _big_pieces(grads)
        pieces = {(g, l): pc.reshape(N_CHIPS, 2, pc.shape[1] // 2, pc.shape[2]) for g, name in enumerate(BIG)
                  for l, pc in enumerate(big[name]) if pc is not None and (g, l) not in self.riding}
        if layer > 0:
            self.riding = pieces
            return _pair_exchange(list(pieces.values()))
        self.ps = _sum_pair_groups(self.riding, self.ras, self.c, "grads_pair_sum")
        self.ps.update(self._pair_sums(pieces, "early"))
        self.early = list(self.ps)
        return _chip_exchange([self.ps[k] for k in self.early], self.early, [None] * len(BIG), self._q_shapes())

    def after_backward_exchange(self, arrived, layer):
        if layer > 0:
            self.ras = dict(zip(self.riding, arrived))
        else:
            self.qs = list(arrived)

    def finish(self, grads, dh):
        big, small = _big_pieces(grads), _small_grads(grads, dh)
        pieces = {(g, l): pc.reshape(N_CHIPS, 2, pc.shape[1] // 2, pc.shape[2])
                  for g, name in enumerate(BIG) for l, pc in enumerate(big[name]) if (g, l) not in self.ps}
        sharded = jnp.concatenate([_shard_major(small[n], TABLE[n][1]) for n in SMALL_SHARDED], axis=1)
        rep = _pad_cols(jnp.concatenate([small[n].reshape(-1) for n in REPLICATED]), N_CHIPS * FLAT_QUANTUM)
        n_sh, n_rep = sharded.shape[1], rep.shape[0] // N_CHIPS
        flat = _pad_cols(jnp.concatenate([sharded, rep.reshape(N_CHIPS, n_rep)], axis=1), FLAT_QUANTUM)
        pieces[(len(BIG), 0)] = flat.reshape(N_CHIPS, 2, -1, LANES)
        late = self._pair_sums(pieces, "late")
        self.ps.update(late)
        keys = list(late)
        small_q = jax.ShapeDtypeStruct((N_CHIPS - 1, 1) + late[(len(BIG), 0)].shape[1:], BF16)
        qs = _run_exchange("grads_chip_exchange_late",
                           _chip_exchange([late[k] for k in keys], keys, self.qs + [None], self._q_shapes() + [small_q]))
        pos = [lax.axis_index(a).reshape(1).astype(jnp.int32) for a in ("x", "y", "c")]
        sums = [_sum_chips([self.ps[(g, l)] for l in range(q.shape[1])], q, pos, "grads_chip_sum") for g, q in enumerate(qs)]
        outs = _run_exchange("grads_pair_share", _pair_share(sums))
        out = {name: o.reshape(o.shape[0], -1, o.shape[3]) for name, o in zip(BIG, outs)}
        f = outs[-1].reshape(-1)
        rep_all = _gather_chips([f[n_sh:n_sh + n_rep].reshape(1, 2, -1, LANES)], "grads_gather_replicated")[0].reshape(-1)
        off = 0
        for name in SMALL_SHARDED:
            ss = _shard_shape(*TABLE[name])
            n = int(np.prod(ss))
            out[name] = f[off:off + n].reshape(ss)
            off += n
        off = 0
        for name in REPLICATED:
            shape = TABLE[name][0]
            n = int(np.prod(shape))
            out[name] = rep_all[off:off + n].reshape(shape)
            off += n
        return out


def kernel(x, meta_tokens, mix_pre_g, mix_post_g, mlp_pre_g, mlp_post_g, w_up, w_down, w_in, ssd_conv_w, ssd_conv_b, ssd_dt_bias, ssd_a_log, ssd_d, ssd_norm_g, mla_q_norm_g, mla_w_q_up, mla_kv_norm_g, mla_w_kv_up, w_out_ab, rg_w_x, rg_w_y, rg_conv_w, rg_conv_b, rg_w_a, rg_b_a, rg_w_i, rg_b_i, rg_lambda, rg_w_out, loss_target, m_meta_tokens, m_mix_pre_g, m_mix_post_g, m_mlp_pre_g, m_mlp_post_g, m_w_up, m_w_down, m_w_in, m_ssd_conv_w, m_ssd_conv_b, m_ssd_dt_bias, m_ssd_a_log, m_ssd_d, m_ssd_norm_g, m_mla_q_norm_g, m_mla_w_q_up, m_mla_kv_norm_g, m_mla_w_kv_up, m_w_out_ab, m_rg_w_x, m_rg_w_y, m_rg_conv_w, m_rg_conv_b, m_rg_w_a, m_rg_b_a, m_rg_w_i, m_rg_b_i, m_rg_lambda, m_rg_w_out, v_meta_tokens, v_mix_pre_g, v_mix_post_g, v_mlp_pre_g, v_mlp_post_g, v_w_up, v_w_down, v_w_in, v_ssd_conv_w, v_ssd_conv_b, v_ssd_dt_bias, v_ssd_a_log, v_ssd_d, v_ssd_norm_g, v_mla_q_norm_g, v_mla_w_q_up, v_mla_kv_norm_g, v_mla_w_kv_up, v_w_out_ab, v_rg_w_x, v_rg_w_y, v_rg_conv_w, v_rg_conv_b, v_rg_w_a, v_rg_b_a, v_rg_w_i, v_rg_b_i, v_rg_lambda, v_rg_w_out):
    names = [n for n, _, _ in WEIGHTS]
    w = dict(zip(names, (meta_tokens, mix_pre_g, mix_post_g, mlp_pre_g, mlp_post_g, w_up, w_down, w_in, ssd_conv_w, ssd_conv_b, ssd_dt_bias, ssd_a_log, ssd_d, ssd_norm_g, mla_q_norm_g, mla_w_q_up, mla_kv_norm_g, mla_w_kv_up, w_out_ab, rg_w_x, rg_w_y, rg_conv_w, rg_conv_b, rg_w_a, rg_b_a, rg_w_i, rg_b_i, rg_lambda, rg_w_out)))
    m = dict(zip(names, (m_meta_tokens, m_mix_pre_g, m_mix_post_g, m_mlp_pre_g, m_mlp_post_g, m_w_up, m_w_down, m_w_in, m_ssd_conv_w, m_ssd_conv_b, m_ssd_dt_bias, m_ssd_a_log, m_ssd_d, m_ssd_norm_g, m_mla_q_norm_g, m_mla_w_q_up, m_mla_kv_norm_g, m_mla_w_kv_up, m_w_out_ab, m_rg_w_x, m_rg_w_y, m_rg_conv_w, m_rg_conv_b, m_rg_w_a, m_rg_b_a, m_rg_w_i, m_rg_b_i, m_rg_lambda, m_rg_w_out)))
    v = dict(zip(names, (v_meta_tokens, v_mix_pre_g, v_mix_post_g, v_mlp_pre_g, v_mlp_post_g, v_w_up, v_w_down, v_w_in, v_ssd_conv_w, v_ssd_conv_b, v_ssd_dt_bias, v_ssd_a_log, v_ssd_d, v_ssd_norm_g, v_mla_q_norm_g, v_mla_w_q_up, v_mla_kv_norm_g, v_mla_w_kv_up, v_w_out_ab, v_rg_w_x, v_rg_w_y, v_rg_conv_w, v_rg_conv_b, v_rg_w_a, v_rg_b_a, v_rg_w_i, v_rg_b_i, v_rg_lambda, v_rg_w_out)))
    ex = StepExchanges(w)
    p = ex.params(ex.first)
    sq, dh, grads = _device_step(x[0], ex.meta, loss_target[0], p, hooks=ex)
    loss = lax.psum(0.5 * sq[0, 0] / D_MODEL, ("x", "y", "c"))
    g = ex.finish(grads, dh)
    grad, delta, new_m, new_v = {}, {}, {}, {}
    for name in names:
        shape = g[name].shape
        two_d = (int(np.prod(shape[:-1])), shape[-1])
        res = _adamw(g[name].reshape(two_d), w[name].reshape(two_d), m[name].reshape(two_d), v[name].reshape(two_d), "adamw")
        grad[name], delta[name], new_m[name], new_v[name] = (r.reshape(shape) for r in res)
    grad_x = dh[PAD + N_META:][None]
    return (loss, grad_x, *[grad[n] for n in names], *[delta[n] for n in names], *[new_m[n] for n in names], *[new_v[n] for n in names])
```
